```python
import jax, jax.numpy as jnp
from jax import lax
import numpy as np

D_MODEL = 1024
BATCH = 16
SEQ = 2048
DEPTH = 4

N_MIXERS = 2
N_MLA = (DEPTH + 1) // 2
N_HGRN = DEPTH // 2

MLA_HEADS = 16
QK_NOPE = 64
QK_ROPE = 32
V_HEAD = 64
Q_LORA = 768
KV_LORA = 256
ROPE_THETA = 10000.0
Q_BLOCK = 128

HGRN_EXPAND = 128
HGRN_HEADS = D_MODEL // HGRN_EXPAND
HGRN_V = D_MODEL // HGRN_HEADS
HGRN_CHUNK = 64

D_FF = -(-8 * D_MODEL // (3 * 256)) * 256

ALPHA = (2.0 * DEPTH) ** 0.25
BETA = (8.0 * DEPTH) ** -0.25
LN_EPS = 1e-5
RMS_EPS = 1e-6

kernel_name = 'hybrid_mla_hgrn2_deepnorm_adaln'


def layer_norm(x, g, b):
    xf = x.astype(jnp.float32)
    mu = jnp.mean(xf, -1, keepdims=True)
    var = jnp.mean(jnp.square(xf - mu), -1, keepdims=True)
    return ((xf - mu) * lax.rsqrt(var + LN_EPS) * g + b).astype(x.dtype)


def rms_norm(x, g):
    xf = x.astype(jnp.float32)
    ms = jnp.mean(jnp.square(xf), -1, keepdims=True)
    return (xf * lax.rsqrt(ms + RMS_EPS) * g).astype(x.dtype)


def rope_cos_sin(positions):
    inv_freq = ROPE_THETA ** (-jnp.arange(0, QK_ROPE, 2, dtype=jnp.float32) / QK_ROPE)
    ang = positions.astype(jnp.float32)[..., None] * inv_freq
    return jnp.cos(ang), jnp.sin(ang)


def apply_rope(x, cos, sin):
    x1, x2 = jnp.split(x.astype(jnp.float32), 2, axis=-1)
    return jnp.concatenate([x1 * cos - x2 * sin, x1 * sin + x2 * cos], -1).astype(x.dtype)


def causal_mla_attention(q_nope, q_rope, k_nope, k_rope, v):
    S = q_nope.shape[1]
    scale = (QK_NOPE + QK_ROPE) ** -0.5
    neg = jnp.finfo(jnp.float32).min
    outs = []
    for blk in range(S // Q_BLOCK):
        q0 = blk * Q_BLOCK
        kend = q0 + Q_BLOCK
        s = (jnp.einsum('bqhd,bkhd->bhqk', q_nope[:, q0:kend], k_nope[:, :kend])
             + jnp.einsum('bqhr,bkr->bhqk', q_rope[:, q0:kend], k_rope[:, :kend]))
        s = s.astype(jnp.float32) * scale
        mask = (q0 + jnp.arange(Q_BLOCK))[:, None] >= jnp.arange(kend)[None, :]
        p = jax.nn.softmax(jnp.where(mask, s, neg), axis=-1).astype(v.dtype)
        outs.append(jnp.einsum('bhqk,bkhd->bqhd', p, v[:, :kend]))
    return jnp.concatenate(outs, axis=1)


def mla(h, cos, sin, w_in, q_norm_g, w_qb, kv_norm_g, w_kvb, w_o):
    B, S, _ = h.shape
    proj = h @ w_in
    q_lat, kv_lat, k_rope = jnp.split(proj, [Q_LORA, Q_LORA + KV_LORA], axis=-1)
    q = (rms_norm(q_lat, q_norm_g) @ w_qb).reshape(B, S, MLA_HEADS, QK_NOPE + QK_ROPE)
    kv = (rms_norm(kv_lat, kv_norm_g) @ w_kvb).reshape(B, S, MLA_HEADS, QK_NOPE + V_HEAD)
    q_nope, q_rope = jnp.split(q, [QK_NOPE], axis=-1)
    k_nope, v = jnp.split(kv, [QK_NOPE], axis=-1)
    q_rope = apply_rope(q_rope, cos[:, :, None, :], sin[:, :, None, :])
    k_rope = apply_rope(k_rope, cos, sin)
    o = causal_mla_attention(q_nope, q_rope, k_nope, k_rope, v)
    return o.reshape(B, S, MLA_HEADS * V_HEAD) @ w_o


def chunk_gated_recurrence(q, k, v, log_f):
    B, S, H, K = q.shape
    V = v.shape[-1]
    C = HGRN_CHUNK
    N = S // C

    def to_chunks(t):
        return t.reshape(B, N, C, H, t.shape[-1]).transpose(1, 0, 3, 2, 4)

    causal = jnp.tril(jnp.ones((C, C), dtype=bool))[:, :, None]

    def step(state, inp):
        q_c, k_c, v_c, g_c = inp
        b = jnp.cumsum(g_c, axis=-2)
        diff = b[..., :, None, :] - b[..., None, :, :]
        decay = jnp.where(causal, jnp.exp(jnp.where(causal, diff, 0.0)), 0.0)
        attn = jnp.einsum('bhtk,bhsk,bhtsk->bhts', q_c, k_c, decay)
        o = (jnp.einsum('bhts,bhsv->bhtv', attn, v_c)
             + jnp.einsum('bhtk,bhkv->bhtv', q_c * jnp.exp(b), state))
        b_last = b[..., -1:, :]
        state = (jnp.exp(b_last[..., 0, :])[..., None] * state
                 + jnp.einsum('bhsk,bhsv->bhkv', k_c * jnp.exp(b_last - b), v_c))
        return state, o

    state0 = jnp.zeros((B, H, K, V), jnp.float32)
    _, o = lax.scan(step, state0, (to_chunks(q), to_chunks(k), to_chunks(v), to_chunks(log_f)))
    return o.transpose(1, 0, 3, 2, 4).reshape(B, S, H, V)


def hgrn2(h, lb, w_in, g_norm_g, w_o):
    B, S, _ = h.shape
    HK = HGRN_HEADS * HGRN_EXPAND
    HV = HGRN_HEADS * HGRN_V
    q, fx, i, g = jnp.split(h @ w_in, [HK, 2 * HK, 2 * HK + HV], axis=-1)
    q = jax.nn.silu(q.astype(jnp.float32)).reshape(B, S, HGRN_HEADS, HGRN_EXPAND)
    fx = fx.astype(jnp.float32).reshape(B, S, HGRN_HEADS, HGRN_EXPAND)
    lb = lb.astype(jnp.float32).reshape(HGRN_HEADS, HGRN_EXPAND)
    sig = jax.nn.sigmoid(fx)
    f = lb + (1.0 - lb) * sig
    log_f = jnp.log(f)
    k = 1.0 - f
    v = i.astype(jnp.float32).reshape(B, S, HGRN_HEADS, HGRN_V)
    o = chunk_gated_recurrence(q, k, v, log_f)
    o = rms_norm(o, g_norm_g).reshape(B, S, HV).astype(h.dtype)
    return (o * jax.nn.silu(g)) @ w_o


def swiglu(h, w_in, w_out):
    gate, up = jnp.split(h @ w_in, 2, axis=-1)
    return (jax.nn.silu(gate) * up) @ w_out


def ada_mod(c, w, b):
    mod = (jax.nn.silu(c) @ w + b)[:, None, :]
    shift, scale, gate = jnp.split(mod, 3, axis=-1)
    return shift, scale, gate


def _w(k, shape, fan_in, scale=1.0):
    return jax.random.normal(k, shape, jnp.float32) * (scale * fan_in ** -0.5)


def _fwd_setup_inputs(seed: int = 0) -> dict:
    key = jax.random.key(seed)
    ks = jax.random.split(key, 24)
    D = D_MODEL
    x = jax.random.normal(ks[0], (BATCH, SEQ, D), jnp.float32)
    c = jax.random.normal(ks[1], (BATCH, D), jnp.float32)
    offsets = jax.random.randint(ks[2], (BATCH, 1), 0, 4096, dtype=jnp.int32)
    positions = offsets + jnp.arange(SEQ, dtype=jnp.int32)[None, :]
    mla_w_in = _w(ks[3], (N_MLA, D, Q_LORA + KV_LORA + QK_ROPE), D)
    mla_q_norm = 1.0 + 0.02 * jax.random.normal(ks[4], (N_MLA, Q_LORA), jnp.float32)
    mla_w_qb = _w(ks[5], (N_MLA, Q_LORA, MLA_HEADS * (QK_NOPE + QK_ROPE)), Q_LORA)
    mla_kv_norm = 1.0 + 0.02 * jax.random.normal(ks[6], (N_MLA, KV_LORA), jnp.float32)
    mla_w_kvb = _w(ks[7], (N_MLA, KV_LORA, MLA_HEADS * (QK_NOPE + V_HEAD)), KV_LORA)
    mla_w_o = _w(ks[8], (N_MLA, MLA_HEADS * V_HEAD, D), MLA_HEADS * V_HEAD, BETA)
    hgrn_lb = 0.5 * jax.random.normal(ks[9], (N_HGRN, HGRN_HEADS * HGRN_EXPAND), jnp.float32)
    hgrn_w_in = _w(ks[10], (N_HGRN, D, 2 * HGRN_HEADS * HGRN_EXPAND + HGRN_HEADS * HGRN_V + D), D)
    hgrn_g_norm = 1.0 + 0.02 * jax.random.normal(ks[11], (N_HGRN, HGRN_V), jnp.float32)
    hgrn_w_o = _w(ks[12], (N_HGRN, HGRN_HEADS * HGRN_V, D), HGRN_HEADS * HGRN_V, BETA)
    ffn_w_in = _w(ks[13], (DEPTH, D, 2 * D_FF), D)
    ffn_w_out = _w(ks[14], (DEPTH, D_FF, D), D_FF, BETA)
    ada_w = _w(ks[15], (DEPTH, 2, D, 3 * D), D, 0.1)
    ada_b = 0.01 * jax.random.normal(ks[16], (DEPTH, 2, 3 * D), jnp.float32)
    ln_g = 1.0 + 0.02 * jax.random.normal(ks[17], (DEPTH, 2, D), jnp.float32)
    ln_b = 0.01 * jax.random.normal(ks[18], (DEPTH, 2, D), jnp.float32)
    return {'x': x, 'c': c, 'positions': positions,
            'mla_w_in': mla_w_in, 'mla_q_norm': mla_q_norm, 'mla_w_qb': mla_w_qb,
            'mla_kv_norm': mla_kv_norm, 'mla_w_kvb': mla_w_kvb, 'mla_w_o': mla_w_o,
            'hgrn_lb': hgrn_lb, 'hgrn_w_in': hgrn_w_in, 'hgrn_g_norm': hgrn_g_norm, 'hgrn_w_o': hgrn_w_o,
            'ffn_w_in': ffn_w_in, 'ffn_w_out': ffn_w_out,
            'ada_w': ada_w, 'ada_b': ada_b, 'ln_g': ln_g, 'ln_b': ln_b}


def _fwd_reference(x, c, positions, mla_w_in, mla_q_norm, mla_w_qb, mla_kv_norm, mla_w_kvb, mla_w_o,
              hgrn_lb, hgrn_w_in, hgrn_g_norm, hgrn_w_o, ffn_w_in, ffn_w_out,
              ada_w, ada_b, ln_g, ln_b):
    cos, sin = rope_cos_sin(positions)
    lb_soft = jax.nn.softmax(hgrn_lb.astype(jnp.float32), axis=0)
    lower_bounds = jnp.cumsum(lb_soft, axis=0) - lb_soft[0]
    for layer in range(DEPTH):
        j = layer // N_MIXERS
        shift, scale, gate = ada_mod(c, ada_w[layer, 0], ada_b[layer, 0])
        h = x * (1.0 + scale) + shift
        if layer % N_MIXERS == 0:
            y = mla(h, cos, sin, mla_w_in[j], mla_q_norm[j], mla_w_qb[j],
                    mla_kv_norm[j], mla_w_kvb[j], mla_w_o[j])
        else:
            y = hgrn2(h, lower_bounds[j], hgrn_w_in[j], hgrn_g_norm[j], hgrn_w_o[j])
        x = layer_norm(ALPHA * x + (1.0 + gate) * y, ln_g[layer, 0], ln_b[layer, 0])
        shift, scale, gate = ada_mod(c, ada_w[layer, 1], ada_b[layer, 1])
        h = x * (1.0 + scale) + shift
        y = swiglu(h, ffn_w_in[layer], ffn_w_out[layer])
        x = layer_norm(ALPHA * x + (1.0 + gate) * y, ln_g[layer, 1], ln_b[layer, 1])
    return x


import jax as _jax
import jax.numpy as _jnp

TWIN_FORMAT = 'train_step'
FWD_PARAMS = ['x', 'c', 'positions', 'mla_w_in', 'mla_q_norm', 'mla_w_qb', 'mla_kv_norm', 'mla_w_kvb', 'mla_w_o', 'hgrn_lb', 'hgrn_w_in', 'hgrn_g_norm', 'hgrn_w_o', 'ffn_w_in', 'ffn_w_out', 'ada_w', 'ada_b', 'ln_g', 'ln_b']
TWIN_WEIGHTS = ['mla_w_in', 'mla_q_norm', 'mla_w_qb', 'mla_kv_norm', 'mla_w_kvb', 'mla_w_o', 'hgrn_lb', 'hgrn_w_in', 'hgrn_g_norm', 'hgrn_w_o', 'ffn_w_in', 'ffn_w_out', 'ada_w', 'ada_b', 'ln_g', 'ln_b']
TWIN_DIFF_INPUT = 'x'
TWIN_INPUTS = ['x', 'c', 'positions', 'mla_w_in', 'mla_q_norm', 'mla_w_qb', 'mla_kv_norm', 'mla_w_kvb', 'mla_w_o', 'hgrn_lb', 'hgrn_w_in', 'hgrn_g_norm', 'hgrn_w_o', 'ffn_w_in', 'ffn_w_out', 'ada_w', 'ada_b', 'ln_g', 'ln_b', 'loss_target', 'm_mla_w_in', 'm_mla_q_norm', 'm_mla_w_qb', 'm_mla_kv_norm', 'm_mla_w_kvb', 'm_mla_w_o', 'm_hgrn_lb', 'm_hgrn_w_in', 'm_hgrn_g_norm', 'm_hgrn_w_o', 'm_ffn_w_in', 'm_ffn_w_out', 'm_ada_w', 'm_ada_b', 'm_ln_g', 'm_ln_b', 'v_mla_w_in', 'v_mla_q_norm', 'v_mla_w_qb', 'v_mla_kv_norm', 'v_mla_w_kvb', 'v_mla_w_o', 'v_hgrn_lb', 'v_hgrn_w_in', 'v_hgrn_g_norm', 'v_hgrn_w_o', 'v_ffn_w_in', 'v_ffn_w_out', 'v_ada_w', 'v_ada_b', 'v_ln_g', 'v_ln_b']
TWIN_OUTPUTS = ['loss', 'grad_x', 'grad_mla_w_in', 'grad_mla_q_norm', 'grad_mla_w_qb', 'grad_mla_kv_norm', 'grad_mla_w_kvb', 'grad_mla_w_o', 'grad_hgrn_lb', 'grad_hgrn_w_in', 'grad_hgrn_g_norm', 'grad_hgrn_w_o', 'grad_ffn_w_in', 'grad_ffn_w_out', 'grad_ada_w', 'grad_ada_b', 'grad_ln_g', 'grad_ln_b', 'delta_mla_w_in', 'delta_mla_q_norm', 'delta_mla_w_qb', 'delta_mla_kv_norm', 'delta_mla_w_kvb', 'delta_mla_w_o', 'delta_hgrn_lb', 'delta_hgrn_w_in', 'delta_hgrn_g_norm', 'delta_hgrn_w_o', 'delta_ffn_w_in', 'delta_ffn_w_out', 'delta_ada_w', 'delta_ada_b', 'delta_ln_g', 'delta_ln_b', 'new_m_mla_w_in', 'new_m_mla_q_norm', 'new_m_mla_w_qb', 'new_m_mla_kv_norm', 'new_m_mla_w_kvb', 'new_m_mla_w_o', 'new_m_hgrn_lb', 'new_m_hgrn_w_in', 'new_m_hgrn_g_norm', 'new_m_hgrn_w_o', 'new_m_ffn_w_in', 'new_m_ffn_w_out', 'new_m_ada_w', 'new_m_ada_b', 'new_m_ln_g', 'new_m_ln_b', 'new_v_mla_w_in', 'new_v_mla_q_norm', 'new_v_mla_w_qb', 'new_v_mla_kv_norm', 'new_v_mla_w_kvb', 'new_v_mla_w_o', 'new_v_hgrn_lb', 'new_v_hgrn_w_in', 'new_v_hgrn_g_norm', 'new_v_hgrn_w_o', 'new_v_ffn_w_in', 'new_v_ffn_w_out', 'new_v_ada_w', 'new_v_ada_b', 'new_v_ln_g', 'new_v_ln_b']
TWIN_LEAF_KINDS = {'loss': 'loss', 'grad_x': 'grad_x', 'grad_mla_w_in': 'grad_w', 'grad_mla_q_norm': 'grad_w', 'grad_mla_w_qb': 'grad_w', 'grad_mla_kv_norm': 'grad_w', 'grad_mla_w_kvb': 'grad_w', 'grad_mla_w_o': 'grad_w', 'grad_hgrn_lb': 'grad_w', 'grad_hgrn_w_in': 'grad_w', 'grad_hgrn_g_norm': 'grad_w', 'grad_hgrn_w_o': 'grad_w', 'grad_ffn_w_in': 'grad_w', 'grad_ffn_w_out': 'grad_w', 'grad_ada_w': 'grad_w', 'grad_ada_b': 'grad_w', 'grad_ln_g': 'grad_w', 'grad_ln_b': 'grad_w', 'delta_mla_w_in': 'delta_w', 'delta_mla_q_norm': 'delta_w', 'delta_mla_w_qb': 'delta_w', 'delta_mla_kv_norm': 'delta_w', 'delta_mla_w_kvb': 'delta_w', 'delta_mla_w_o': 'delta_w', 'delta_hgrn_lb': 'delta_w', 'delta_hgrn_w_in': 'delta_w', 'delta_hgrn_g_norm': 'delta_w', 'delta_hgrn_w_o': 'delta_w', 'delta_ffn_w_in': 'delta_w', 'delta_ffn_w_out': 'delta_w', 'delta_ada_w': 'delta_w', 'delta_ada_b': 'delta_w', 'delta_ln_g': 'delta_w', 'delta_ln_b': 'delta_w', 'new_m_mla_w_in': 'new_m', 'new_m_mla_q_norm': 'new_m', 'new_m_mla_w_qb': 'new_m', 'new_m_mla_kv_norm': 'new_m', 'new_m_mla_w_kvb': 'new_m', 'new_m_mla_w_o': 'new_m', 'new_m_hgrn_lb': 'new_m', 'new_m_hgrn_w_in': 'new_m', 'new_m_hgrn_g_norm': 'new_m', 'new_m_hgrn_w_o': 'new_m', 'new_m_ffn_w_in': 'new_m', 'new_m_ffn_w_out': 'new_m', 'new_m_ada_w': 'new_m', 'new_m_ada_b': 'new_m', 'new_m_ln_g': 'new_m', 'new_m_ln_b': 'new_m', 'new_v_mla_w_in': 'new_v', 'new_v_mla_q_norm': 'new_v', 'new_v_mla_w_qb': 'new_v', 'new_v_mla_kv_norm': 'new_v', 'new_v_mla_w_kvb': 'new_v', 'new_v_mla_w_o': 'new_v', 'new_v_hgrn_lb': 'new_v', 'new_v_hgrn_w_in': 'new_v', 'new_v_hgrn_g_norm': 'new_v', 'new_v_hgrn_w_o': 'new_v', 'new_v_ffn_w_in': 'new_v', 'new_v_ffn_w_out': 'new_v', 'new_v_ada_w': 'new_v', 'new_v_ada_b': 'new_v', 'new_v_ln_g': 'new_v', 'new_v_ln_b': 'new_v'}


def _forward(args):
    return _fwd_reference(*[args[k] for k in FWD_PARAMS])


def _output_shape():
    out = _jax.eval_shape(lambda: _forward(_fwd_setup_inputs(0)))
    return out.shape, out.dtype

N_MICROBATCH = 1
ADAM_LR = 0.001
ADAM_B1 = 0.9
ADAM_B2 = 0.999
ADAM_EPS = 1e-08
ADAM_WD = 0.01
ADAM_STEP = 10
PER_EXAMPLE_BATCH_AXIS = {'x': 0, 'c': 0, 'positions': 0, 'loss_target': 0}
SHARED_INPUTS = []
_WEIGHT_DTYPES = {'mla_w_in': _jnp.float32, 'mla_q_norm': _jnp.float32, 'mla_w_qb': _jnp.float32, 'mla_kv_norm': _jnp.float32, 'mla_w_kvb': _jnp.float32, 'mla_w_o': _jnp.float32, 'hgrn_lb': _jnp.float32, 'hgrn_w_in': _jnp.float32, 'hgrn_g_norm': _jnp.float32, 'hgrn_w_o': _jnp.float32, 'ffn_w_in': _jnp.float32, 'ffn_w_out': _jnp.float32, 'ada_w': _jnp.float32, 'ada_b': _jnp.float32, 'ln_g': _jnp.float32, 'ln_b': _jnp.float32}
MOMENT_SCALE = {'mla_w_in': 1.583908e-02, 'mla_q_norm': 9.769472e-03, 'mla_w_qb': 6.937867e-03, 'mla_kv_norm': 2.884102e-02, 'mla_w_kvb': 9.337610e-03, 'mla_w_o': 2.702049e-02, 'hgrn_lb': 2.423698e-03, 'hgrn_w_in': 1.963607e-02, 'hgrn_g_norm': 7.797852e-02, 'hgrn_w_o': 6.456812e-02, 'ffn_w_in': 1.664663e-02, 'ffn_w_out': 6.463362e-02, 'ada_w': 2.363066e-02, 'ada_b': 4.158084e-02, 'ln_g': 1.137233e+01, 'ln_b': 3.755206e-01}


def _to_microbatches(a, axis):
    t = _jnp.moveaxis(a, axis, 0)
    t = t.reshape((N_MICROBATCH, t.shape[0] // N_MICROBATCH) + t.shape[1:])
    return _jnp.moveaxis(t, 1, axis + 1)


def setup_inputs(seed: int = 0) -> dict:
    inp = _fwd_setup_inputs(seed)
    key = _jax.random.fold_in(_jax.random.key(seed), 7919)
    shape, _ = _output_shape()
    out = dict(inp)
    out["loss_target"] = _jax.random.normal(_jax.random.fold_in(key, 0), shape, _jnp.float32)
    for i, name in enumerate(TWIN_WEIGHTS):
        w = inp[name].astype(_jnp.float32)
        if MOMENT_SCALE is None:
            s = _jnp.sqrt(_jnp.mean(_jnp.square(w)) + 1e-30)
        else:
            s = MOMENT_SCALE[name]
        km, kv = _jax.random.split(_jax.random.fold_in(key, i + 1))
        out[name] = w
        out["m_" + name] = s * _jax.random.normal(km, w.shape, _jnp.float32)
        out["v_" + name] = (s * s) * _jax.random.uniform(kv, w.shape, _jnp.float32, 0.5, 1.5)
    if N_MICROBATCH > 1:
        for name, axis in PER_EXAMPLE_BATCH_AXIS.items():
            out[name] = _to_microbatches(out[name], axis)
    return {'x': out['x'], 'c': out['c'], 'positions': out['positions'], 'mla_w_in': out['mla_w_in'], 'mla_q_norm': out['mla_q_norm'], 'mla_w_qb': out['mla_w_qb'], 'mla_kv_norm': out['mla_kv_norm'], 'mla_w_kvb': out['mla_w_kvb'], 'mla_w_o': out['mla_w_o'], 'hgrn_lb': out['hgrn_lb'], 'hgrn_w_in': out['hgrn_w_in'], 'hgrn_g_norm': out['hgrn_g_norm'], 'hgrn_w_o': out['hgrn_w_o'], 'ffn_w_in': out['ffn_w_in'], 'ffn_w_out': out['ffn_w_out'], 'ada_w': out['ada_w'], 'ada_b': out['ada_b'], 'ln_g': out['ln_g'], 'ln_b': out['ln_b'], 'loss_target': out['loss_target'], 'm_mla_w_in': out['m_mla_w_in'], 'm_mla_q_norm': out['m_mla_q_norm'], 'm_mla_w_qb': out['m_mla_w_qb'], 'm_mla_kv_norm': out['m_mla_kv_norm'], 'm_mla_w_kvb': out['m_mla_w_kvb'], 'm_mla_w_o': out['m_mla_w_o'], 'm_hgrn_lb': out['m_hgrn_lb'], 'm_hgrn_w_in': out['m_hgrn_w_in'], 'm_hgrn_g_norm': out['m_hgrn_g_norm'], 'm_hgrn_w_o': out['m_hgrn_w_o'], 'm_ffn_w_in': out['m_ffn_w_in'], 'm_ffn_w_out': out['m_ffn_w_out'], 'm_ada_w': out['m_ada_w'], 'm_ada_b': out['m_ada_b'], 'm_ln_g': out['m_ln_g'], 'm_ln_b': out['m_ln_b'], 'v_mla_w_in': out['v_mla_w_in'], 'v_mla_q_norm': out['v_mla_q_norm'], 'v_mla_w_qb': out['v_mla_w_qb'], 'v_mla_kv_norm': out['v_mla_kv_norm'], 'v_mla_w_kvb': out['v_mla_w_kvb'], 'v_mla_w_o': out['v_mla_w_o'], 'v_hgrn_lb': out['v_hgrn_lb'], 'v_hgrn_w_in': out['v_hgrn_w_in'], 'v_hgrn_g_norm': out['v_hgrn_g_norm'], 'v_hgrn_w_o': out['v_hgrn_w_o'], 'v_ffn_w_in': out['v_ffn_w_in'], 'v_ffn_w_out': out['v_ffn_w_out'], 'v_ada_w': out['v_ada_w'], 'v_ada_b': out['v_ada_b'], 'v_ln_g': out['v_ln_g'], 'v_ln_b': out['v_ln_b']}


def _loss(weights, diff, rest, loss_target):
    with _jax.named_scope("forward"):
        args = {**rest, TWIN_DIFF_INPUT: diff, **{k: w.astype(_WEIGHT_DTYPES[k]) for k, w in weights.items()}}
        y = _forward(args)
    with _jax.named_scope("loss_head"):
        err = _jnp.square(y.astype(_jnp.float32) - loss_target)
        return 0.5 * _jnp.sum(_jnp.mean(err, axis=-1)) if err.ndim else 0.5 * err


def _adamw(w, g, m, v):
    m = ADAM_B1 * m + (1.0 - ADAM_B1) * g
    v = ADAM_B2 * v + (1.0 - ADAM_B2) * _jnp.square(g)
    m_hat = m / (1.0 - ADAM_B1 ** ADAM_STEP)
    v_hat = v / (1.0 - ADAM_B2 ** ADAM_STEP)
    delta = -ADAM_LR * (m_hat / (_jnp.sqrt(v_hat) + ADAM_EPS) + ADAM_WD * w)
    return delta, m, v


def reference(x, c, positions, mla_w_in, mla_q_norm, mla_w_qb, mla_kv_norm, mla_w_kvb, mla_w_o, hgrn_lb, hgrn_w_in, hgrn_g_norm, hgrn_w_o, ffn_w_in, ffn_w_out, ada_w, ada_b, ln_g, ln_b, loss_target, m_mla_w_in, m_mla_q_norm, m_mla_w_qb, m_mla_kv_norm, m_mla_w_kvb, m_mla_w_o, m_hgrn_lb, m_hgrn_w_in, m_hgrn_g_norm, m_hgrn_w_o, m_ffn_w_in, m_ffn_w_out, m_ada_w, m_ada_b, m_ln_g, m_ln_b, v_mla_w_in, v_mla_q_norm, v_mla_w_qb, v_mla_kv_norm, v_mla_w_kvb, v_mla_w_o, v_hgrn_lb, v_hgrn_w_in, v_hgrn_g_norm, v_hgrn_w_o, v_ffn_w_in, v_ffn_w_out, v_ada_w, v_ada_b, v_ln_g, v_ln_b):
    given = dict(x=x, c=c, positions=positions, mla_w_in=mla_w_in, mla_q_norm=mla_q_norm, mla_w_qb=mla_w_qb, mla_kv_norm=mla_kv_norm, mla_w_kvb=mla_w_kvb, mla_w_o=mla_w_o, hgrn_lb=hgrn_lb, hgrn_w_in=hgrn_w_in, hgrn_g_norm=hgrn_g_norm, hgrn_w_o=hgrn_w_o, ffn_w_in=ffn_w_in, ffn_w_out=ffn_w_out, ada_w=ada_w, ada_b=ada_b, ln_g=ln_g, ln_b=ln_b, loss_target=loss_target, m_mla_w_in=m_mla_w_in, m_mla_q_norm=m_mla_q_norm, m_mla_w_qb=m_mla_w_qb, m_mla_kv_norm=m_mla_kv_norm, m_mla_w_kvb=m_mla_w_kvb, m_mla_w_o=m_mla_w_o, m_hgrn_lb=m_hgrn_lb, m_hgrn_w_in=m_hgrn_w_in, m_hgrn_g_norm=m_hgrn_g_norm, m_hgrn_w_o=m_hgrn_w_o, m_ffn_w_in=m_ffn_w_in, m_ffn_w_out=m_ffn_w_out, m_ada_w=m_ada_w, m_ada_b=m_ada_b, m_ln_g=m_ln_g, m_ln_b=m_ln_b, v_mla_w_in=v_mla_w_in, v_mla_q_norm=v_mla_q_norm, v_mla_w_qb=v_mla_w_qb, v_mla_kv_norm=v_mla_kv_norm, v_mla_w_kvb=v_mla_w_kvb, v_mla_w_o=v_mla_w_o, v_hgrn_lb=v_hgrn_lb, v_hgrn_w_in=v_hgrn_w_in, v_hgrn_g_norm=v_hgrn_g_norm, v_hgrn_w_o=v_hgrn_w_o, v_ffn_w_in=v_ffn_w_in, v_ffn_w_out=v_ffn_w_out, v_ada_w=v_ada_w, v_ada_b=v_ada_b, v_ln_g=v_ln_g, v_ln_b=v_ln_b)
    weights = {n: given[n] for n in TWIN_WEIGHTS}
    shared = {n: given[n] for n in SHARED_INPUTS}
    per_example = {n: given[n] for n in ['x', 'c', 'positions']}
    grad_fn = _jax.value_and_grad(_loss, argnums=(0, 1))

    def one_microbatch(ex, loss_target):
        ex = dict(ex)
        diff = ex.pop(TWIN_DIFF_INPUT)
        return grad_fn(weights, diff, {**shared, **ex}, loss_target)

    if N_MICROBATCH == 1:
        loss, (grad_w, grad_x) = one_microbatch(per_example, given["loss_target"])
    else:
        def body(carry, xs):
            loss_sum, grad_sum = carry
            l_k, (gw_k, gx_k) = one_microbatch(xs[0], xs[1])
            with _jax.named_scope("update"):
                return (loss_sum + l_k, _jax.tree.map(_jnp.add, grad_sum, gw_k)), gx_k

        init = (_jnp.zeros((), _jnp.float32), _jax.tree.map(_jnp.zeros_like, weights))
        (loss, grad_w), grad_x = _jax.lax.scan(body, init, (per_example, given["loss_target"]))
    with _jax.named_scope("update"):
        delta_w, new_m, new_v = {}, {}, {}
        for n in TWIN_WEIGHTS:
            delta_w[n], new_m[n], new_v[n] = _adamw(weights[n], grad_w[n], given["m_" + n], given["v_" + n])
    return (loss, grad_x, *[grad_w[n] for n in TWIN_WEIGHTS], *[delta_w[n] for n in TWIN_WEIGHTS],
            *[new_m[n] for n in TWIN_WEIGHTS], *[new_v[n] for n in TWIN_WEIGHTS])
```

```python
import functools

import jax
import jax.numpy as jnp
from jax import lax
from jax.experimental import pallas as pl
from jax.experimental.pallas import tpu as pltpu

F32 = jnp.float32
BF16 = jnp.bfloat16
SDS = jax.ShapeDtypeStruct
MESH = pl.DeviceIdType.MESH
HI = lax.Precision.HIGHEST

MLA_HEADS, QK_NOPE, QK_ROPE, V_HEAD = 16, 64, 32, 64
Q_LORA, KV_LORA = 768, 256
QK_DIM = QK_NOPE + QK_ROPE
ROPE_THETA = 10000.0
HGRN_K = 128
HGRN_CHUNK = 64
HGRN_SUB = 16
LN_EPS, RMS_EPS = 1e-5, 1e-6
ADAM_LR, ADAM_B1, ADAM_B2, ADAM_EPS, ADAM_WD, ADAM_STEP = 0.001, 0.9, 0.999, 1e-08, 0.01, 10
NEG = -1e30

VMEM_LIMIT_BYTES = 56 * 1024 * 1024
LANES = 128


def _cparams(*sem):
    return pltpu.CompilerParams(dimension_semantics=sem if sem else None, vmem_limit_bytes=VMEM_LIMIT_BYTES)


def _pick_tile(n, cap):
    best = 0
    for t in range(LANES, min(n, cap) + 1, LANES):
        if n % t == 0:
            best = t
    return best if best else n


def _bdot(a, b):
    return jnp.dot(a.astype(BF16), b.astype(BF16), preferred_element_type=F32)


def _bdot_nt(a, b):
    return lax.dot_general(a.astype(BF16), b.astype(BF16), (((1,), (1,)), ((), ())), preferred_element_type=F32)


def _bdot_tn(a, b):
    return lax.dot_general(a.astype(BF16), b.astype(BF16), (((0,), (0,)), ((), ())), preferred_element_type=F32)


def _hdot(a, b):
    return jnp.dot(a, b, precision=HI, preferred_element_type=F32)


def _hdot_nt(a, b):
    return lax.dot_general(a, b, (((1,), (1,)), ((), ())), precision=HI, preferred_element_type=F32)


def _hdot_tn(a, b):
    return lax.dot_general(a, b, (((0,), (0,)), ((), ())), precision=HI, preferred_element_type=F32)


def _mm_nn(a, w, out_dtype, name):
    M, K = a.shape
    G, _, n = w.shape
    tm = min(512, M)
    tn = _pick_tile(n, 1536)
    nps = n // tn

    def body(a_ref, w_ref, o_ref):
        o_ref[...] = _bdot(a_ref[...], w_ref[...]).astype(o_ref.dtype)

    return pl.pallas_call(
        body,
        grid=(G * nps, M // tm),
        in_specs=[
            pl.BlockSpec((tm, K), lambda j, i: (i, 0)),
            pl.BlockSpec((None, K, tn), lambda j, i: (j // nps, 0, j % nps)),
        ],
        out_specs=pl.BlockSpec((tm, tn), lambda j, i: (i, j)),
        out_shape=SDS((M, G * n), out_dtype),
        name=name,
        compiler_params=_cparams("parallel", "parallel"),
    )(a, w)


def _mm_nt(a, w, out_dtype, name):
    M = a.shape[0]
    G, K, n = w.shape
    tm = min(512, M)
    tk = _pick_tile(K, 1536)

    def body(a_ref, w_ref, o_ref, acc_ref):
        s = pl.program_id(2)

        @pl.when(s == 0)
        def _():
            acc_ref[...] = jnp.zeros_like(acc_ref)

        acc_ref[...] += _bdot_nt(a_ref[...], w_ref[...])

        @pl.when(s == G - 1)
        def _():
            o_ref[...] = acc_ref[...].astype(o_ref.dtype)

    return pl.pallas_call(
        body,
        grid=(K // tk, M // tm, G),
        in_specs=[
            pl.BlockSpec((tm, n), lambda kb, i, s: (i, s)),
            pl.BlockSpec((None, tk, n), lambda kb, i, s: (s, kb, 0)),
        ],
        out_specs=pl.BlockSpec((tm, tk), lambda kb, i, s: (i, kb)),
        out_shape=SDS((M, K), out_dtype),
        scratch_shapes=[pltpu.VMEM((tm, tk), F32)],
        name=name,
        compiler_params=_cparams("parallel", "parallel", "arbitrary"),
    )(a, w)


def _mm_tn(a, d, G, out_dtype, name):
    T, K = a.shape
    n = d.shape[1] // G
    tk = _pick_tile(K, 256)
    tn = _pick_tile(n, 1536)
    nps = n // tn

    def body(a_ref, d_ref, o_ref):
        o_ref[...] = _bdot_tn(a_ref[...], d_ref[...]).astype(o_ref.dtype)

    return pl.pallas_call(
        body,
        grid=(G * nps, K // tk),
        in_specs=[
            pl.BlockSpec((T, tk), lambda j, i: (0, i)),
            pl.BlockSpec((T, tn), lambda j, i: (0, j)),
        ],
        out_specs=pl.BlockSpec((None, tk, tn), lambda j, i: (j // nps, i, j % nps)),
        out_shape=SDS((G, K, n), out_dtype),
        name=name,
        compiler_params=_cparams("parallel", "parallel"),
    )(a, d)


def _rows_call(body, name, B, S, ins, outs, ts=256):
    ts = min(ts, S)
    in_specs, args = [], []
    for arr, kind in ins:
        W = arr.shape[-1]
        if kind == "row":
            in_specs.append(pl.BlockSpec((None, ts, W), lambda b, s: (b, s, 0)))
        elif kind == "ex":
            in_specs.append(pl.BlockSpec((None, 1, W), lambda b, s: (b, 0, 0)))
        else:
            in_specs.append(pl.BlockSpec((1, W), lambda b, s: (0, 0)))
        args.append(arr)
    out_specs, out_shape = [], []
    for W, dt, kind in outs:
        if kind == "row":
            out_specs.append(pl.BlockSpec((None, ts, W), lambda b, s: (b, s, 0)))
            out_shape.append(SDS((B, S, W), dt))
        else:
            out_specs.append(pl.BlockSpec((None, 1, W), lambda b, s: (b, 0, 0)))
            out_shape.append(SDS((B, 1, W), dt))
    return pl.pallas_call(
        body,
        grid=(B, S // ts),
        in_specs=in_specs,
        out_specs=out_specs,
        out_shape=out_shape,
        name=name,
        compiler_params=_cparams("parallel", "arbitrary"),
    )(*args)


def _acc(ref, val):
    @pl.when(pl.program_id(1) == 0)
    def _():
        ref[...] = jnp.zeros_like(ref)

    ref[...] += val


def _mod_fn(x, sc, sh):
    return x * (1.0 + sc) + sh


def _ln_fn(alpha, x, y, gate, g, b):
    z = alpha * x + (1.0 + gate) * y
    mu = jnp.mean(z, -1, keepdims=True)
    var = jnp.mean(jnp.square(z - mu), -1, keepdims=True)
    return (z - mu) * lax.rsqrt(var + LN_EPS) * g + b


def _modulate(x, sc, sh, name):
    B, S, D = x.shape

    def body(x_ref, sc_ref, sh_ref, h_ref):
        h_ref[...] = _mod_fn(x_ref[...], sc_ref[...], sh_ref[...]).astype(BF16)

    return _rows_call(body, name, B, S, [(x, "row"), (sc, "ex"), (sh, "ex")], [(D, BF16, "row")])[0]


def _ln_fwd(alpha, x, y, gate, g, b, name):
    B, S, D = x.shape

    def body(x_ref, y_ref, gate_ref, g_ref, b_ref, o_ref):
        o_ref[...] = _ln_fn(alpha, x_ref[...], y_ref[...], gate_ref[...], g_ref[...], b_ref[...])

    return _rows_call(
        body, name, B, S, [(x, "row"), (y, "row"), (gate, "ex"), (g, "par"), (b, "par")], [(D, F32, "row")]
    )[0]


def _ln_bwd(alpha, dout, x, y, gate, g, b, name):
    B, S, D = x.shape

    def body(do_ref, x_ref, y_ref, gate_ref, g_ref, b_ref, dxr_ref, dy_ref, dgate_ref, dg_ref, db_ref):
        _, vjp = jax.vjp(
            functools.partial(_ln_fn, alpha), x_ref[...], y_ref[...], gate_ref[...], g_ref[...], b_ref[...]
        )
        dx, dy, dgate, dg, db = vjp(do_ref[...])
        dxr_ref[...] = dx
        dy_ref[...] = dy.astype(BF16)
        _acc(dgate_ref, dgate)
        _acc(dg_ref, dg)
        _acc(db_ref, db)

    return _rows_call(
        body,
        name,
        B,
        S,
        [(dout, "row"), (x, "row"), (y, "row"), (gate, "ex"), (g, "par"), (b, "par")],
        [(D, F32, "row"), (D, BF16, "row"), (D, F32, "acc"), (D, F32, "acc"), (D, F32, "acc")],
    )


def _mod_bwd(dh, dxr, x, sc, name):
    B, S, D = x.shape

    def body(dh_ref, dxr_ref, x_ref, sc_ref, dx_ref, dsc_ref, dsh_ref):
        dh_v = dh_ref[...]
        dx_ref[...] = dxr_ref[...] + dh_v * (1.0 + sc_ref[...])
        _acc(dsc_ref, jnp.sum(dh_v * x_ref[...], axis=0, keepdims=True))
        _acc(dsh_ref, jnp.sum(dh_v, axis=0, keepdims=True))

    return _rows_call(
        body,
        name,
        B,
        S,
        [(dh, "row"), (dxr, "row"), (x, "row"), (sc, "ex")],
        [(D, F32, "row"), (D, F32, "acc"), (D, F32, "acc")],
    )


def _loss_head(y, target, name):
    B, S, D = y.shape

    def body(y_ref, t_ref, l_ref, dy_ref):
        e = y_ref[...] - t_ref[...]
        dy_ref[...] = e * (1.0 / D)
        part = 0.5 * jnp.sum(jnp.sum(e * e, axis=1, keepdims=True) * (1.0 / D), axis=0, keepdims=True)
        _acc(l_ref, jnp.broadcast_to(part, (1, LANES)))

    loss, dy = _rows_call(
        body, name, B, S, [(y, "row"), (target, "row")], [(LANES, F32, "acc"), (D, F32, "row")]
    )
    return jnp.sum(loss[:, 0, 0]), dy


def _swiglu_fn(u):
    F = u.shape[-1] // 2
    return jax.nn.silu(u[:, :F]) * u[:, F:]


def _swiglu_fwd(u, name):
    B, S, F2 = u.shape

    def body(u_ref, a_ref):
        a_ref[...] = _swiglu_fn(u_ref[...]).astype(BF16)

    return _rows_call(body, name, B, S, [(u, "row")], [(F2 // 2, BF16, "row")])[0]


def _swiglu_bwd(u, da, name):
    B, S, F2 = u.shape

    def body(u_ref, da_ref, du_ref):
        _, vjp = jax.vjp(_swiglu_fn, u_ref[...])
        du_ref[...] = vjp(da_ref[...])[0].astype(BF16)

    return _rows_call(body, name, B, S, [(u, "row"), (da, "row")], [(F2, BF16, "row")])[0]


def _rms_fn(x, g):
    return x * lax.rsqrt(jnp.mean(jnp.square(x), -1, keepdims=True) + RMS_EPS) * g


def _mla_mid_fwd(proj, gq, gkv, name):
    B, S, _ = proj.shape

    def body(p_ref, gq_ref, gkv_ref, qn_ref, kvn_ref):
        p = p_ref[...]
        qn_ref[...] = _rms_fn(p[:, :Q_LORA], gq_ref[...]).astype(BF16)
        kvn_ref[...] = _rms_fn(p[:, Q_LORA : Q_LORA + KV_LORA], gkv_ref[...]).astype(BF16)

    return _rows_call(
        body, name, B, S, [(proj, "row"), (gq, "par"), (gkv, "par")], [(Q_LORA, BF16, "row"), (KV_LORA, BF16, "row")]
    )


def _mla_mid_bwd(proj, dqn, dkvn, dkr, gq, gkv, name):
    B, S, W = proj.shape

    def body(p_ref, dqn_ref, dkvn_ref, dkr_ref, gq_ref, gkv_ref, dp_ref, dgq_ref, dgkv_ref):
        p = p_ref[...]
        _, vq = jax.vjp(_rms_fn, p[:, :Q_LORA], gq_ref[...])
        dql, dgq = vq(dqn_ref[...])
        _, vkv = jax.vjp(_rms_fn, p[:, Q_LORA : Q_LORA + KV_LORA], gkv_ref[...])
        dkvl, dgkv = vkv(dkvn_ref[...])
        dp_ref[:, :Q_LORA] = dql.astype(BF16)
        dp_ref[:, Q_LORA : Q_LORA + KV_LORA] = dkvl.astype(BF16)
        dp_ref[:, Q_LORA + KV_LORA :] = dkr_ref[...].astype(BF16)
        _acc(dgq_ref, dgq)
        _acc(dgkv_ref, dgkv)

    return _rows_call(
        body,
        name,
        B,
        S,
        [(proj, "row"), (dqn, "row"), (dkvn, "row"), (dkr, "row"), (gq, "par"), (gkv, "par")],
        [(W, BF16, "row"), (Q_LORA, F32, "acc"), (KV_LORA, F32, "acc")],
    )


def _rope(x, cos, sin):
    h = QK_ROPE // 2
    x1, x2 = x[:, :h], x[:, h:]
    return jnp.concatenate([x1 * cos - x2 * sin, x1 * sin + x2 * cos], axis=1)


def _rope_t(dy, cos, sin):
    h = QK_ROPE // 2
    d1, d2 = dy[:, :h], dy[:, h:]
    return jnp.concatenate([d1 * cos + d2 * sin, d2 * cos - d1 * sin], axis=1)


def _heads_call(body, name, B, S, ins, outs, ts=256):
    ts = min(ts, S)
    in_specs, args = [], []
    for arr, kind in ins:
        if kind == "row":
            in_specs.append(pl.BlockSpec((None, ts, arr.shape[-1]), lambda b, s: (b, s, 0)))
        else:
            in_specs.append(pl.BlockSpec((arr.shape[0], None, ts, arr.shape[-1]), lambda b, s: (0, b, s, 0)))
        args.append(arr)
    out_specs, out_shape = [], []
    for shape, dt, kind in outs:
        if kind == "row":
            out_specs.append(pl.BlockSpec((None, ts, shape[-1]), lambda b, s: (b, s, 0)))
        else:
            out_specs.append(pl.BlockSpec((shape[0], None, ts, shape[-1]), lambda b, s: (0, b, s, 0)))
        out_shape.append(SDS(shape, dt))
    return pl.pallas_call(
        body,
        grid=(B, S // ts),
        in_specs=in_specs,
        out_specs=out_specs,
        out_shape=out_shape,
        name=name,
        compiler_params=_cparams("parallel", "parallel"),
    )(*args)


def _mla_prep_fwd(q, kv, proj, cos, sin, name):
    B, S, _ = q.shape
    H = MLA_HEADS

    def body(q_ref, kv_ref, p_ref, cos_ref, sin_ref, qh_ref, kh_ref, vh_ref):
        cos_v, sin_v = cos_ref[...], sin_ref[...]
        kr = _rope(p_ref[:, Q_LORA + KV_LORA :], cos_v, sin_v).astype(BF16)
        for h in range(H):
            qn = q_ref[:, h * QK_DIM : h * QK_DIM + QK_NOPE]
            qr = _rope(q_ref[:, h * QK_DIM + QK_NOPE : (h + 1) * QK_DIM], cos_v, sin_v)
            qh_ref[h] = jnp.concatenate([qn, qr], axis=1).astype(BF16)
            kn = kv_ref[:, h * 128 : h * 128 + QK_NOPE].astype(BF16)
            kh_ref[h] = jnp.concatenate([kn, kr], axis=1)
            vh_ref[h] = kv_ref[:, h * 128 + QK_NOPE : (h + 1) * 128].astype(BF16)

    return _heads_call(
        body,
        name,
        B,
        S,
        [(q, "row"), (kv, "row"), (proj, "row"), (cos, "row"), (sin, "row")],
        [((H, B, S, QK_DIM), BF16, "heads"), ((H, B, S, QK_DIM), BF16, "heads"), ((H, B, S, V_HEAD), BF16, "heads")],
    )


def _mla_prep_bwd(dqh, dkh, dvh, cos, sin, name):
    H, B, S, _ = dqh.shape

    def body(dqh_ref, dkh_ref, dvh_ref, cos_ref, sin_ref, dq_ref, dkv_ref, dkr_ref):
        cos_v, sin_v = cos_ref[...], sin_ref[...]
        dkr = jnp.zeros((cos_v.shape[0], QK_ROPE), F32)
        for h in range(H):
            dqv = dqh_ref[h].astype(F32)
            dq_ref[:, h * QK_DIM : h * QK_DIM + QK_NOPE] = dqv[:, :QK_NOPE].astype(BF16)
            dq_ref[:, h * QK_DIM + QK_NOPE : (h + 1) * QK_DIM] = _rope_t(dqv[:, QK_NOPE:], cos_v, sin_v).astype(BF16)
            dkv = dkh_ref[h].astype(F32)
            dkv_ref[:, h * 128 : h * 128 + QK_NOPE] = dkv[:, :QK_NOPE].astype(BF16)
            dkv_ref[:, h * 128 + QK_NOPE : (h + 1) * 128] = dvh_ref[h]
            dkr = dkr + dkv[:, QK_NOPE:]
        dkr_ref[...] = _rope_t(dkr, cos_v, sin_v)

    return _heads_call(
        body,
        name,
        B,
        S,
        [(dqh, "heads"), (dkh, "heads"), (dvh, "heads"), (cos, "row"), (sin, "row")],
        [((B, S, H * QK_DIM), BF16, "row"), ((B, S, H * 128), BF16, "row"), ((B, S, QK_ROPE), F32, "row")],
    )


def _causal_mask(i, j, t):
    rows = i * t + lax.broadcasted_iota(jnp.int32, (t, t), 0)
    cols = j * t + lax.broadcasted_iota(jnp.int32, (t, t), 1)
    return rows >= cols


def _attn_fwd(qh, kh, vh, name):
    H, B, S, _ = qh.shape
    t = min(256, S)
    scale = QK_DIM**-0.5

    def body(q_ref, k_ref, v_ref, o_ref, lse_ref):
        i = pl.program_id(2)
        outs = []
        for hh in range(2):
            q = q_ref[hh]

            def step(j, carry, hh=hh, q=q):
                m, l, acc = carry
                rows = pl.ds(pl.multiple_of(j * t, t), t)
                s = _bdot_nt(q, k_ref[hh, rows, :]) * scale
                s = jnp.where(_causal_mask(i, j, t), s, NEG)
                m_new = jnp.maximum(m, jnp.max(s, axis=1, keepdims=True))
                p = jnp.exp(s - m_new)
                a = jnp.exp(m - m_new)
                l = a * l + jnp.sum(p, axis=1, keepdims=True)
                acc = a * acc + _bdot(p, v_ref[hh, rows, :])
                return m_new, l, acc

            init = (jnp.full((t, 1), NEG, F32), jnp.zeros((t, 1), F32), jnp.zeros((t, V_HEAD), F32))
            m, l, acc = lax.fori_loop(0, i + 1, step, init)
            outs.append(acc / l)
            lse_ref[hh] = m + jnp.log(l)
        o_ref[...] = jnp.concatenate(outs, axis=1).astype(BF16)

    return pl.pallas_call(
        body,
        grid=(B, H // 2, S // t),
        in_specs=[
            pl.BlockSpec((2, None, t, QK_DIM), lambda b, p, i: (p, b, i, 0)),
            pl.BlockSpec((2, None, S, QK_DIM), lambda b, p, i: (p, b, 0, 0)),
            pl.BlockSpec((2, None, S, V_HEAD), lambda b, p, i: (p, b, 0, 0)),
        ],
        out_specs=[
            pl.BlockSpec((None, t, 2 * V_HEAD), lambda b, p, i: (b, i, p)),
            pl.BlockSpec((2, None, t, 1), lambda b, p, i: (p, b, i, 0)),
        ],
        out_shape=[SDS((B, S, H * V_HEAD), BF16), SDS((H, B, S, 1), F32)],
        name=name,
        compiler_params=_cparams("parallel", "parallel", "arbitrary"),
    )(qh, kh, vh)


def _attn_bwd(qh, kh, vh, o, do, lse, name):
    H, B, S, _ = qh.shape
    t = min(256, S)
    nq = S // t
    scale = QK_DIM**-0.5

    def body(q_ref, k_ref, v_ref, o_ref, do_ref, lse_ref, dq_ref, dk_ref, dv_ref, dq_acc, delta_ref):
        for hh in range(2):
            lanes = slice(hh * V_HEAD, (hh + 1) * V_HEAD)
            delta_ref[...] = jnp.sum(
                o_ref[...].astype(F32)[:, lanes] * do_ref[...].astype(F32)[:, lanes], axis=1, keepdims=True
            )
            dq_acc[...] = jnp.zeros_like(dq_acc)

            def kloop(j, _, hh=hh, lanes=lanes):
                krows = pl.ds(pl.multiple_of(j * t, t), t)
                k = k_ref[hh, krows, :]
                v = v_ref[hh, krows, :]

                def qloop(i, carry):
                    dk, dv = carry
                    qrows = pl.ds(pl.multiple_of(i * t, t), t)
                    q = q_ref[hh, qrows, :]
                    do_i = do_ref[qrows, :][:, lanes]
                    s = _bdot_nt(q, k) * scale
                    p = jnp.where(_causal_mask(i, j, t), jnp.exp(s - lse_ref[hh, qrows, :]), 0.0)
                    dv = dv + _bdot_tn(p, do_i)
                    dp = _bdot_nt(do_i, v)
                    ds = (p * (dp - delta_ref[qrows, :]) * scale).astype(BF16)
                    dk = dk + _bdot_tn(ds, q)
                    dq_acc[qrows, :] += _bdot(ds, k)
                    return dk, dv

                dk, dv = lax.fori_loop(j, nq, qloop, (jnp.zeros((t, QK_DIM), F32), jnp.zeros((t, V_HEAD), F32)))
                dk_ref[hh, krows, :] = dk.astype(BF16)
                dv_ref[hh, krows, :] = dv.astype(BF16)
                return 0

            lax.fori_loop(0, nq, kloop, 0)
            dq_ref[hh] = dq_acc[...].astype(BF16)

    hspec = lambda w: pl.BlockSpec((2, None, S, w), lambda b, p: (p, b, 0, 0))
    ospec = pl.BlockSpec((None, S, 2 * V_HEAD), lambda b, p: (b, 0, p))
    return pl.pallas_call(
        body,
        grid=(B, H // 2),
        in_specs=[hspec(QK_DIM), hspec(QK_DIM), hspec(V_HEAD), ospec, ospec, hspec(1)],
        out_specs=[hspec(QK_DIM), hspec(QK_DIM), hspec(V_HEAD)],
        out_shape=[SDS((H, B, S, QK_DIM), BF16), SDS((H, B, S, QK_DIM), BF16), SDS((H, B, S, V_HEAD), BF16)],
        scratch_shapes=[pltpu.VMEM((S, QK_DIM), F32), pltpu.VMEM((S, 1), F32)],
        name=name,
        compiler_params=_cparams("parallel", "parallel"),
    )(qh, kh, vh, o, do, lse)


def _hgrn_pre(q, fx, lb):
    sig = jax.nn.sigmoid(fx)
    f = lb + (1.0 - lb) * sig
    return jax.nn.silu(q), 1.0 - f, jnp.log(f)


def _hgrn_gate(o, gg, gn):
    return _rms_fn(o, gn) * jax.nn.silu(gg)


def _tri(n, lower):
    r = lax.broadcasted_iota(jnp.int32, (n, n), 0)
    c = lax.broadcasted_iota(jnp.int32, (n, n), 1)
    return ((r >= c) if lower else (r <= c)).astype(F32)


def _hgrn_intra_fwd(qs, k, v, b):
    C, SB = qs.shape[0], min(HGRN_SUB, qs.shape[0])
    ridx = lax.broadcasted_iota(jnp.int32, (SB, 1), 0)
    outs = []
    for i in range(C // SB):
        r0 = i * SB
        qi, ki, vi, bi = qs[r0 : r0 + SB], k[r0 : r0 + SB], v[r0 : r0 + SB], b[r0 : r0 + SB]
        acc = jnp.zeros((SB, v.shape[1]), F32)
        for s in range(SB):
            mask = ridx >= s
            e = jnp.exp(jnp.where(mask, bi - bi[s : s + 1], 0.0))
            a = jnp.sum(jnp.where(mask, qi * ki[s : s + 1] * e, 0.0), axis=1, keepdims=True)
            acc = acc + a * vi[s : s + 1]
        if i > 0:
            ref = bi[0:1]
            qt = qi * jnp.exp(bi - ref)
            kt = k[:r0] * jnp.exp(ref - b[:r0])
            acc = acc + _bdot(_hdot_nt(qt, kt), v[:r0])
        outs.append(acc)
    return jnp.concatenate(outs, axis=0)


def _hgrn_intra_bwd(qs, k, v, b, do):
    C, SB = qs.shape[0], min(HGRN_SUB, qs.shape[0])
    nb = C // SB
    ridx = lax.broadcasted_iota(jnp.int32, (SB, 1), 0)
    dq_p = [None] * nb
    dk_p = [jnp.zeros((SB, k.shape[1]), F32) for _ in range(nb)]
    dv_p = [jnp.zeros((SB, v.shape[1]), F32) for _ in range(nb)]
    for i in range(nb):
        r0 = i * SB
        qi, ki, vi, bi, doi = qs[r0 : r0 + SB], k[r0 : r0 + SB], v[r0 : r0 + SB], b[r0 : r0 + SB], do[r0 : r0 + SB]
        dqi = jnp.zeros_like(qi)
        dki = jnp.zeros_like(ki)
        dvi = jnp.zeros_like(vi)
        for s in range(SB):
            mask = ridx >= s
            e = jnp.where(mask, jnp.exp(jnp.where(mask, bi - bi[s : s + 1], 0.0)), 0.0)
            da = jnp.sum(doi * vi[s : s + 1], axis=1, keepdims=True)
            a = jnp.sum(qi * ki[s : s + 1] * e, axis=1, keepdims=True)
            dqi = dqi + da * (ki[s : s + 1] * e)
            dk_row = jnp.sum(da * qi * e, axis=0, keepdims=True)
            dv_row = jnp.sum(a * doi, axis=0, keepdims=True)
            dki = jnp.where(ridx == s, dki + dk_row, dki)
            dvi = jnp.where(ridx == s, dvi + dv_row, dvi)
        if i > 0:
            ref = bi[0:1]
            eq = jnp.exp(bi - ref)
            ek = jnp.exp(ref - b[:r0])
            qt = qi * eq
            kt = k[:r0] * ek
            A = _hdot_nt(qt, kt)
            dA = _bdot_nt(doi, v[:r0])
            dvl = _bdot_tn(A, doi)
            dqi = dqi + _hdot(dA, kt) * eq
            dkl = _hdot_tn(dA, qt) * ek
            for j in range(i):
                dk_p[j] = dk_p[j] + dkl[j * SB : (j + 1) * SB]
                dv_p[j] = dv_p[j] + dvl[j * SB : (j + 1) * SB]
        dq_p[i] = dqi
        dk_p[i] = dk_p[i] + dki
        dv_p[i] = dv_p[i] + dvi
    return jnp.concatenate(dq_p, axis=0), jnp.concatenate(dk_p, axis=0), jnp.concatenate(dv_p, axis=0)


def _hgrn_fwd(proj, lb, gn, name):
    B, S, W = proj.shape
    HK = W // 4
    H = HK // HGRN_K
    C = min(HGRN_CHUNK, S)
    N = S // C

    def body(q_ref, f_ref, i_ref, g_ref, lb_ref, gn_ref, og_ref, o_ref, st_ref):
        lb_v, gn_v = lb_ref[...], gn_ref[...]
        tril = _tri(C, True)

        def chunk(n, st):
            rows = pl.ds(pl.multiple_of(n * C, C), C)
            qs, k, g = _hgrn_pre(q_ref[rows, :], f_ref[rows, :], lb_v)
            v = i_ref[rows, :]
            b = _hdot(tril, g)
            st_ref[n] = st
            o = _hgrn_intra_fwd(qs, k, v, b) + _bdot_nt(qs * jnp.exp(b), st)
            bl = b[C - 1 : C]
            st = st * jnp.exp(bl) + _bdot_tn(v, k * jnp.exp(bl - b))
            o_ref[rows, :] = o
            og_ref[rows, :] = _hgrn_gate(o, g_ref[rows, :], gn_v).astype(BF16)
            return st

        lax.fori_loop(0, N, chunk, jnp.zeros((HGRN_K, HGRN_K), F32))

    col = lambda part: pl.BlockSpec((None, S, HGRN_K), lambda b, h: (b, 0, part * H + h))
    return pl.pallas_call(
        body,
        grid=(B, H),
        in_specs=[col(0), col(1), col(2), col(3), pl.BlockSpec((1, HGRN_K), lambda b, h: (0, h)), pl.BlockSpec((1, HGRN_K), lambda b, h: (0, 0))],
        out_specs=[col(0), col(0), pl.BlockSpec((None, None, N, HGRN_K, HGRN_K), lambda b, h: (b, h, 0, 0, 0))],
        out_shape=[SDS((B, S, HK), BF16), SDS((B, S, HK), F32), SDS((B, H, N, HGRN_K, HGRN_K), F32)],
        name=name,
        compiler_params=_cparams("parallel", "parallel"),
    )(proj, proj, proj, proj, lb, gn)


def _hgrn_bwd(proj, lb, gn, o_pre, states, dog, name):
    B, S, W = proj.shape
    HK = W // 4
    H = HK // HGRN_K
    C = min(HGRN_CHUNK, S)
    N = S // C

    def body(q_ref, f_ref, i_ref, g_ref, lb_ref, gn_ref, o_ref, st_ref, dog_ref, dq_ref, df_ref, di_ref, dg_ref, dlb_ref, dgn_ref):
        lb_v, gn_v = lb_ref[...], gn_ref[...]
        tril = _tri(C, True)
        triu = _tri(C, False)

        def chunk(idx, carry):
            dst, dlb, dgn = carry
            n = N - 1 - idx
            rows = pl.ds(pl.multiple_of(n * C, C), C)
            (qs, k, g), pre_vjp = jax.vjp(_hgrn_pre, q_ref[rows, :], f_ref[rows, :], lb_v)
            v = i_ref[rows, :]
            _, gate_vjp = jax.vjp(_hgrn_gate, o_ref[rows, :], g_ref[rows, :], gn_v)
            do, dgg, dgn_c = gate_vjp(dog_ref[rows, :])
            b = _hdot(tril, g)
            st0 = st_ref[n]
            eb = jnp.exp(b)
            bl = b[C - 1 : C]
            ebl = jnp.exp(bl)
            ekb = jnp.exp(bl - b)
            qe = qs * eb
            kt = k * ekb
            dqs, dk, dv = _hgrn_intra_bwd(qs, k, v, b, do)
            dqs = dqs + _bdot(do, st0) * eb
            dk = dk + _bdot(v, dst) * ekb
            dv = dv + _bdot_nt(kt, dst)
            st1 = st0 * ebl + _bdot_tn(v, kt)
            dbl = jnp.sum(st1 * dst, axis=0, keepdims=True)
            dst = dst * ebl + _bdot_tn(do, qe)
            dgl = _hdot(triu, qs * dqs - k * dk) + dbl
            dq_pre, dfx, dlb_c = pre_vjp((dqs, dk, dgl))
            dq_ref[rows, :] = dq_pre.astype(BF16)
            df_ref[rows, :] = dfx.astype(BF16)
            di_ref[rows, :] = dv.astype(BF16)
            dg_ref[rows, :] = dgg.astype(BF16)
            return dst, dlb + dlb_c, dgn + dgn_c

        zero = jnp.zeros((1, HGRN_K), F32)
        _, dlb, dgn = lax.fori_loop(0, N, chunk, (jnp.zeros((HGRN_K, HGRN_K), F32), zero, zero))
        dlb_ref[...] = dlb
        dgn_ref[...] = dgn

    col = lambda part: pl.BlockSpec((None, S, HGRN_K), lambda b, h: (b, 0, part * H + h))
    vec = pl.BlockSpec((None, None, 1, HGRN_K), lambda b, h: (b, h, 0, 0))
    return pl.pallas_call(
        body,
        grid=(B, H),
        in_specs=[
            col(0), col(1), col(2), col(3),
            pl.BlockSpec((1, HGRN_K), lambda b, h: (0, h)),
            pl.BlockSpec((1, HGRN_K), lambda b, h: (0, 0)),
            col(0),
            pl.BlockSpec((None, None, N, HGRN_K, HGRN_K), lambda b, h: (b, h, 0, 0, 0)),
            col(0),
        ],
        out_specs=[col(0), col(0), col(0), col(0), vec, vec],
        out_shape=[SDS((B, S, HK), BF16)] * 4 + [SDS((B, H, 1, HGRN_K), F32)] * 2,
        name=name,
        compiler_params=_cparams("parallel", "parallel"),
    )(proj, proj, proj, proj, lb, gn, o_pre, states, dog)


def _ada_fwd(c_all, w, b, name):
    Bg, D = c_all.shape
    L, _, n = w.shape

    def body(c_ref, w_ref, b_ref, o_ref):
        o_ref[...] = _bdot(jax.nn.silu(c_ref[...]), w_ref[...]) + b_ref[...]

    return pl.pallas_call(
        body,
        grid=(L,),
        in_specs=[
            pl.BlockSpec((Bg, D), lambda l: (0, 0)),
            pl.BlockSpec((None, D, n), lambda l: (l, 0, 0)),
            pl.BlockSpec((None, 1, n), lambda l: (l, 0, 0)),
        ],
        out_specs=pl.BlockSpec((None, Bg, n), lambda l: (l, 0, 0)),
        out_shape=SDS((L, Bg, n), F32),
        name=name,
        compiler_params=_cparams("parallel"),
    )(c_all, w, b)


def _ada_bwd(c_all, dmod, name):
    Bg, D = c_all.shape
    L, _, n = dmod.shape

    def body(c_ref, d_ref, dw_ref, db_ref):
        d = d_ref[...]
        dw_ref[...] = _bdot_tn(jax.nn.silu(c_ref[...]), d)
        db_ref[...] = jnp.sum(d, axis=0, keepdims=True)

    return pl.pallas_call(
        body,
        grid=(L,),
        in_specs=[pl.BlockSpec((Bg, D), lambda l: (0, 0)), pl.BlockSpec((None, Bg, n), lambda l: (l, 0, 0))],
        out_specs=[pl.BlockSpec((None, D, n), lambda l: (l, 0, 0)), pl.BlockSpec((None, 1, n), lambda l: (l, 0, 0))],
        out_shape=[SDS((L, D, n), F32), SDS((L, 1, n), F32)],
        name=name,
        compiler_params=_cparams("parallel"),
    )(c_all, dmod)


def _adamw(w, gs, m, v, name):
    shape = w.shape
    cols = shape[-1]
    rows = w.size // cols
    tr = rows
    for cand in (512, 256, 128, 64, 32, 16, 8):
        if rows % cand == 0 and cand * cols * 4 <= 2 * 1024 * 1024:
            tr = cand
            break
    as2d = lambda a: a.reshape(rows, cols)
    ng = len(gs)
    c1 = 1.0 / (1.0 - ADAM_B1**ADAM_STEP)
    c2 = 1.0 / (1.0 - ADAM_B2**ADAM_STEP)

    def body(*refs):
        w_ref, m_ref, v_ref = refs[0], refs[1], refs[2]
        g_refs = refs[3 : 3 + ng]
        g_out, d_out, m_out, v_out = refs[3 + ng :]
        g = g_refs[0][...].astype(F32)
        for r in g_refs[1:]:
            g = g + r[...].astype(F32)
        m_new = ADAM_B1 * m_ref[...] + (1.0 - ADAM_B1) * g
        v_new = ADAM_B2 * v_ref[...] + (1.0 - ADAM_B2) * jnp.square(g)
        g_out[...] = g
        m_out[...] = m_new
        v_out[...] = v_new
        d_out[...] = -ADAM_LR * ((m_new * c1) / (jnp.sqrt(v_new * c2) + ADAM_EPS) + ADAM_WD * w_ref[...])

    spec = pl.BlockSpec((tr, cols), lambda i: (i, 0))
    outs = pl.pallas_call(
        body,
        grid=(rows // tr,),
        in_specs=[spec] * (3 + ng),
        out_specs=[spec] * 4,
        out_shape=[SDS((rows, cols), F32)] * 4,
        name=name,
        compiler_params=_cparams("parallel"),
    )(as2d(w), as2d(m), as2d(v), *[as2d(g) for g in gs])
    return tuple(o.reshape(shape) for o in outs)


def _sum4(own, recv, k_me, name):
    shape = own.shape[1:]
    cols = shape[-1]
    rows = own.size // 4 // cols
    tr = rows
    for cand in (512, 256, 128, 64, 32, 16):
        if rows % cand == 0 and cand * cols * 4 <= 2 * 1024 * 1024:
            tr = cand
            break

    def body(k_ref, own_ref, recv_ref, o_ref):
        acc = own_ref[...].astype(F32)
        for r in range(3):
            acc = acc + recv_ref[r].astype(F32)
        o_ref[...] = acc

    out = pl.pallas_call(
        body,
        grid_spec=pltpu.PrefetchScalarGridSpec(
            num_scalar_prefetch=1,
            grid=(rows // tr,),
            in_specs=[
                pl.BlockSpec((None, tr, cols), lambda i, k: (k[0], i, 0)),
                pl.BlockSpec((3, tr, cols), lambda i, k: (0, i, 0)),
            ],
            out_specs=pl.BlockSpec((tr, cols), lambda i, k: (i, 0)),
        ),
        out_shape=SDS((rows, cols), F32),
        name=name,
        compiler_params=_cparams("parallel"),
    )(k_me.reshape(1), own.reshape(4, rows, cols), recv.reshape(3, rows, cols))
    return out.reshape(shape)


def _my_place():
    return lax.axis_index("x"), lax.axis_index("y"), lax.axis_index("c")


def _flip(v, bit):
    return 1 - v if bit else v


def _allgather8(x, name):
    r, n = x.shape

    def body(x_ref, o_ref, send_sems, recv_sems, local_sem):
        mx, my, mc = _my_place()
        me = 4 * mx + 2 * my + mc
        mine = pltpu.make_async_copy(x_ref, o_ref.at[me], local_sem)
        mine.start()
        sends = []
        for rel in range(1, 8):
            peer = (_flip(mx, rel & 4), _flip(my, rel & 2), _flip(mc, rel & 1))
            cp = pltpu.make_async_remote_copy(
                src_ref=x_ref, dst_ref=o_ref.at[me], send_sem=send_sems.at[rel - 1], recv_sem=recv_sems.at[rel - 1],
                device_id=peer, device_id_type=MESH,
            )
            cp.start()
            sends.append(cp)
        for rel in range(1, 8):
            px, py, pc = _flip(mx, rel & 4), _flip(my, rel & 2), _flip(mc, rel & 1)
            pltpu.make_async_remote_copy(
                src_ref=x_ref, dst_ref=o_ref.at[4 * px + 2 * py + pc], send_sem=send_sems.at[rel - 1],
                recv_sem=recv_sems.at[rel - 1], device_id=(px, py, pc), device_id_type=MESH,
            ).wait_recv()
        for cp in sends:
            cp.wait_send()
        mine.wait()

    return pl.pallas_call(
        body,
        out_shape=SDS((8, r, n), x.dtype),
        in_specs=[pl.BlockSpec(memory_space=pl.ANY)],
        out_specs=pl.BlockSpec(memory_space=pl.ANY),
        scratch_shapes=[pltpu.SemaphoreType.DMA((7,)), pltpu.SemaphoreType.DMA((7,)), pltpu.SemaphoreType.DMA],
        name=name,
    )(x)


_HBM = pl.BlockSpec(memory_space=pl.ANY)


def _gather_chips(shards, name):
    n = len(shards)

    def body(*refs):
        ins, outs = refs[:n], refs[n : 2 * n]
        send_sems, recv_sems, local_sems = refs[2 * n :]
        mx, my, mc = _my_place()
        k_me = 2 * mx + my
        started = []
        for i in range(n):
            cp = pltpu.make_async_copy(ins[i], outs[i].at[k_me], local_sems.at[i])
            cp.start()
            started.append(cp)
        sends = []
        for i in range(n):
            for rel in range(1, 4):
                peer = (_flip(mx, rel & 2), _flip(my, rel & 1), mc)
                cp = pltpu.make_async_remote_copy(
                    src_ref=ins[i], dst_ref=outs[i].at[k_me], send_sem=send_sems.at[3 * i + rel - 1],
                    recv_sem=recv_sems.at[3 * i + rel - 1], device_id=peer, device_id_type=MESH,
                )
                cp.start()
                sends.append(cp)
        for i in range(n):
            for rel in range(1, 4):
                px, py = _flip(mx, rel & 2), _flip(my, rel & 1)
                pltpu.make_async_remote_copy(
                    src_ref=ins[i], dst_ref=outs[i].at[2 * px + py], send_sem=send_sems.at[3 * i + rel - 1],
                    recv_sem=recv_sems.at[3 * i + rel - 1], device_id=(px, py, mc), device_id_type=MESH,
                ).wait_recv()
        for cp in sends:
            cp.wait_send()
        for cp in started:
            cp.wait()

    return pl.pallas_call(
        body,
        out_shape=[SDS((4,) + s.shape, s.dtype) for s in shards],
        in_specs=[_HBM] * n,
        out_specs=[_HBM] * n,
        scratch_shapes=[pltpu.SemaphoreType.DMA((3 * n,)), pltpu.SemaphoreType.DMA((3 * n,)), pltpu.SemaphoreType.DMA((n,))],
        name=name,
    )(*shards)


def _scatter_chips(slabs, name):
    n = len(slabs)

    def body(*refs):
        ins, outs = refs[:n], refs[n : 2 * n]
        send_sems, recv_sems = refs[2 * n :]
        mx, my, mc = _my_place()
        sends = []
        for i in range(n):
            for rel in range(1, 4):
                px, py = _flip(mx, rel & 2), _flip(my, rel & 1)
                cp = pltpu.make_async_remote_copy(
                    src_ref=ins[i].at[2 * px + py], dst_ref=outs[i].at[rel - 1], send_sem=send_sems.at[3 * i + rel - 1],
                    recv_sem=recv_sems.at[3 * i + rel - 1], device_id=(px, py, mc), device_id_type=MESH,
                )
                cp.start()
                sends.append(cp)
        for cp in sends:
            cp.wait_recv()
        for cp in sends:
            cp.wait_send()

    return pl.pallas_call(
        body,
        out_shape=[SDS((3,) + s.shape[1:], s.dtype) for s in slabs],
        in_specs=[_HBM] * n,
        out_specs=[_HBM] * n,
        scratch_shapes=[pltpu.SemaphoreType.DMA((3 * n,)), pltpu.SemaphoreType.DMA((3 * n,))],
        name=name,
    )(*slabs)


def _swap_sibling(parts, name):
    n = len(parts)

    def body(*refs):
        ins, outs = refs[:n], refs[n : 2 * n]
        send_sems, recv_sems = refs[2 * n :]
        mx, my, mc = _my_place()
        sends = []
        for i in range(n):
            cp = pltpu.make_async_remote_copy(
                src_ref=ins[i], dst_ref=outs[i], send_sem=send_sems.at[i], recv_sem=recv_sems.at[i],
                device_id=(mx, my, 1 - mc), device_id_type=MESH,
            )
            cp.start()
            sends.append(cp)
        for cp in sends:
            cp.wait_recv()
        for cp in sends:
            cp.wait_send()

    return pl.pallas_call(
        body,
        out_shape=[SDS(s.shape, s.dtype) for s in parts],
        in_specs=[_HBM] * n,
        out_specs=[_HBM] * n,
        scratch_shapes=[pltpu.SemaphoreType.DMA((n,)), pltpu.SemaphoreType.DMA((n,))],
        name=name,
    )(*parts)


def _pad_rows(a, rows):
    return jnp.pad(a, ((0, rows - a.shape[0]), (0, 0)))


def kernel(x, c, positions, mla_w_in, mla_q_norm, mla_w_qb, mla_kv_norm, mla_w_kvb, mla_w_o, hgrn_lb, hgrn_w_in, hgrn_g_norm, hgrn_w_o, ffn_w_in, ffn_w_out, ada_w, ada_b, ln_g, ln_b, loss_target, m_mla_w_in, m_mla_q_norm, m_mla_w_qb, m_mla_kv_norm, m_mla_w_kvb, m_mla_w_o, m_hgrn_lb, m_hgrn_w_in, m_hgrn_g_norm, m_hgrn_w_o, m_ffn_w_in, m_ffn_w_out, m_ada_w, m_ada_b, m_ln_g, m_ln_b, v_mla_w_in, v_mla_q_norm, v_mla_w_qb, v_mla_kv_norm, v_mla_w_kvb, v_mla_w_o, v_hgrn_lb, v_hgrn_w_in, v_hgrn_g_norm, v_hgrn_w_o, v_ffn_w_in, v_ffn_w_out, v_ada_w, v_ada_b, v_ln_g, v_ln_b):
    B, S, D = x.shape
    T = B * S
    depth = ada_w.shape[0]
    n_mla, n_hgrn = mla_w_in.shape[0], hgrn_w_in.shape[0]
    n_sub = 2 * depth
    alpha = (2.0 * depth) ** 0.25
    mx, my, mc = _my_place()
    me = 4 * mx + 2 * my + mc
    k_me = 2 * mx + my
    Bg = 8 * B
    HK = hgrn_w_o.shape[1] * 4
    dq = D // 4

    lbw = hgrn_lb.shape[1]
    first = jnp.zeros((8, max(D, 4 * lbw)), F32)
    first = first.at[:B, :D].set(c).at[B : B + n_hgrn, :lbw].set(hgrn_lb)
    first_all = _allgather8(first, "gather_cond")
    c_all = first_all[:, :B, :D].reshape(Bg, D)
    lb_logits = jnp.concatenate([first_all[2 * k, B : B + n_hgrn, :lbw] for k in range(4)], axis=1)

    def lower_bounds_fn(logits):
        soft = jax.nn.softmax(logits, axis=0)
        return jnp.cumsum(soft, axis=0) - soft[0]

    lower_bounds, lower_bounds_vjp = jax.vjp(lower_bounds_fn, lb_logits)

    n_ada = ada_w.shape[-1]
    mod_part = _ada_fwd(c_all, ada_w.reshape(n_sub, D, n_ada), ada_b.reshape(n_sub, 1, n_ada), "ada_fwd")
    mod_all = _allgather8(mod_part.reshape(n_sub * Bg, n_ada), "gather_mod").reshape(8, n_sub, Bg, n_ada)
    mod = jnp.concatenate([mod_all[2 * k] for k in range(4)], axis=-1)
    mod = lax.dynamic_slice_in_dim(mod, me * B, B, axis=1)
    shift = [mod[j, :, None, :D] for j in range(n_sub)]
    scale = [mod[j, :, None, D : 2 * D] for j in range(n_sub)]
    gate = [mod[j, :, None, 2 * D :] for j in range(n_sub)]

    ln_rows = 2 * n_sub
    ln_local = _pad_rows(jnp.concatenate([ln_g.reshape(n_sub, dq), ln_b.reshape(n_sub, dq)], axis=0), -(-ln_rows // 8) * 8)
    ln_pad = jnp.zeros((ln_local.shape[0], -(-dq // LANES) * LANES), F32).at[:, :dq].set(ln_local)
    ln_all = _allgather8(ln_pad, "gather_ln")
    ln_full = jnp.concatenate([ln_all[2 * k, :ln_rows, :dq] for k in range(4)], axis=1)
    lng = [ln_full[j][None, :] for j in range(n_sub)]
    lnb = [ln_full[n_sub + j][None, :] for j in range(n_sub)]

    main = dict(mla_w_in=mla_w_in, mla_w_qb=mla_w_qb, mla_w_kvb=mla_w_kvb, mla_w_o=mla_w_o, hgrn_w_in=hgrn_w_in,
                hgrn_w_o=hgrn_w_o, ffn_w_in=ffn_w_in, ffn_w_out=ffn_w_out)
    names = list(main)
    gathered = dict(zip(names, _gather_chips([main[k].astype(BF16) for k in names], "gather_weights")))

    def col_w(name, l):
        return gathered[name][:, l]

    def row_w(name, l):
        g = gathered[name][:, l]
        return g.reshape(1, g.shape[0] * g.shape[1], g.shape[2])

    def full_w_in(l):
        g = gathered["mla_w_in"][:, l]
        return jnp.transpose(g, (1, 0, 2)).reshape(1, g.shape[1], 4 * g.shape[2])

    ang = positions.astype(F32)[..., None] * (ROPE_THETA ** (-jnp.arange(0, QK_ROPE, 2, dtype=F32) / QK_ROPE))
    cos, sin = jnp.cos(ang), jnp.sin(ang)

    gq = [mla_q_norm[j][None, :] for j in range(n_mla)]
    gkv = [mla_kv_norm[j][None, :] for j in range(n_mla)]
    gn = [hgrn_g_norm[j][None, :] for j in range(n_hgrn)]

    def r2(a):
        return a.reshape(T, a.shape[-1])

    def r3(a):
        return a.reshape(B, S, a.shape[-1])

    saved = []
    xs = x
    for layer in range(depth):
        j = layer // 2
        sub = 2 * layer
        tag = f"l{layer}"
        h = _modulate(xs, scale[sub], shift[sub], f"mod_{tag}a")
        if layer % 2 == 0:
            w_in = full_w_in(j)
            proj = r3(_mm_nn(r2(h), w_in, F32, f"mla_in_{tag}"))
            qn, kvn = _mla_mid_fwd(proj, gq[j], gkv[j], f"mla_mid_{tag}")
            q = r3(_mm_nn(r2(qn), col_w("mla_w_qb", j), F32, f"mla_qb_{tag}"))
            kv = r3(_mm_nn(r2(kvn), col_w("mla_w_kvb", j), F32, f"mla_kvb_{tag}"))
            qh, kh, vh = _mla_prep_fwd(q, kv, proj, cos, sin, f"mla_prep_{tag}")
            o, lse = _attn_fwd(qh, kh, vh, f"attn_{tag}")
            y = r3(_mm_nn(r2(o), row_w("mla_w_o", j), F32, f"mla_o_{tag}"))
            mix = (h, w_in, proj, qn, kvn, qh, kh, vh, o, lse)
        else:
            proj = r3(_mm_nn(r2(h), col_w("hgrn_w_in", j), F32, f"hgrn_in_{tag}"))
            og, o_pre, states = _hgrn_fwd(proj, lower_bounds[j][None, :], gn[j], f"hgrn_{tag}")
            y = r3(_mm_nn(r2(og), row_w("hgrn_w_o", j), F32, f"hgrn_o_{tag}"))
            mix = (h, proj, og, o_pre, states)
        x1 = _ln_fwd(alpha, xs, y, gate[sub], lng[sub], lnb[sub], f"ln_{tag}a")
        h2 = _modulate(x1, scale[sub + 1], shift[sub + 1], f"mod_{tag}b")
        u = r3(_mm_nn(r2(h2), col_w("ffn_w_in", layer), F32, f"ffn_in_{tag}"))
        a = _swiglu_fwd(u, f"swiglu_{tag}")
        y2 = r3(_mm_nn(r2(a), row_w("ffn_w_out", layer), F32, f"ffn_out_{tag}"))
        x2 = _ln_fwd(alpha, x1, y2, gate[sub + 1], lng[sub + 1], lnb[sub + 1], f"ln_{tag}b")
        saved.append((xs, y, x1, y2, mix, h2, u, a))
        xs = x2

    loss_local, dout = _loss_head(xs, loss_target, "loss_head")
    loss = lax.psum(loss_local, ("x", "y", "c"))

    gw = {k: [None] * main[k].shape[0] for k in names}
    d_shift, d_scale, d_gate = [None] * n_sub, [None] * n_sub, [None] * n_sub
    d_lng, d_lnb = [None] * n_sub, [None] * n_sub
    d_gq, d_gkv, d_gn, d_lbnd = [None] * n_mla, [None] * n_mla, [None] * n_hgrn, [None] * n_hgrn

    def rows4(g):
        return g.reshape(4, g.shape[1] // 4, g.shape[2])

    for layer in reversed(range(depth)):
        j = layer // 2
        sub = 2 * layer
        tag = f"l{layer}"
        xs, y, x1, y2, mix, h2, u, a = saved[layer]
        dxr, dy2, d_gate[sub + 1], d_lng[sub + 1], d_lnb[sub + 1] = _ln_bwd(
            alpha, dout, x1, y2, gate[sub + 1], lng[sub + 1], lnb[sub + 1], f"ln_bwd_{tag}b")
        da = r3(_mm_nt(r2(dy2), row_w("ffn_w_out", layer), F32, f"ffn_out_dx_{tag}"))
        gw["ffn_w_out"][layer] = rows4(_mm_tn(r2(a), r2(dy2), 1, BF16, f"ffn_out_dw_{tag}"))
        du = _swiglu_bwd(u, da, f"swiglu_bwd_{tag}")
        dh2 = r3(_mm_nt(r2(du), col_w("ffn_w_in", layer), F32, f"ffn_in_dx_{tag}"))
        gw["ffn_w_in"][layer] = _mm_tn(r2(h2), r2(du), 4, BF16, f"ffn_in_dw_{tag}")
        dout, d_scale[sub + 1], d_shift[sub + 1] = _mod_bwd(dh2, dxr, x1, scale[sub + 1], f"mod_bwd_{tag}b")
        dxr, dy, d_gate[sub], d_lng[sub], d_lnb[sub] = _ln_bwd(
            alpha, dout, xs, y, gate[sub], lng[sub], lnb[sub], f"ln_bwd_{tag}a")
        if layer % 2 == 0:
            h, w_in, proj, qn, kvn, qh, kh, vh, o, lse = mix
            do = r3(_mm_nt(r2(dy), row_w("mla_w_o", j), BF16, f"mla_o_dx_{tag}"))
            gw["mla_w_o"][j] = rows4(_mm_tn(r2(o), r2(dy), 1, BF16, f"mla_o_dw_{tag}"))
            dqh, dkh, dvh = _attn_bwd(qh, kh, vh, o, do, lse, f"attn_bwd_{tag}")
            dq_, dkv_, dkr = _mla_prep_bwd(dqh, dkh, dvh, cos, sin, f"mla_prep_bwd_{tag}")
            dqn = r3(_mm_nt(r2(dq_), col_w("mla_w_qb", j), F32, f"mla_qb_dx_{tag}"))
            gw["mla_w_qb"][j] = _mm_tn(r2(qn), r2(dq_), 4, BF16, f"mla_qb_dw_{tag}")
            dkvn = r3(_mm_nt(r2(dkv_), col_w("mla_w_kvb", j), F32, f"mla_kvb_dx_{tag}"))
            gw["mla_w_kvb"][j] = _mm_tn(r2(kvn), r2(dkv_), 4, BF16, f"mla_kvb_dw_{tag}")
            dproj, dgq_, dgkv_ = _mla_mid_bwd(proj, dqn, dkvn, dkr, gq[j], gkv[j], f"mla_mid_bwd_{tag}")
            d_gq[j], d_gkv[j] = dgq_.sum(0), dgkv_.sum(0)
            dh = r3(_mm_nt(r2(dproj), w_in, F32, f"mla_in_dx_{tag}"))
            gwin = _mm_tn(r2(h), r2(dproj), 1, BF16, f"mla_in_dw_{tag}")[0]
            gw["mla_w_in"][j] = jnp.transpose(gwin.reshape(gwin.shape[0], 4, gwin.shape[1] // 4), (1, 0, 2))
        else:
            h, proj, og, o_pre, states = mix
            dog = r3(_mm_nt(r2(dy), row_w("hgrn_w_o", j), F32, f"hgrn_o_dx_{tag}"))
            gw["hgrn_w_o"][j] = rows4(_mm_tn(r2(og), r2(dy), 1, BF16, f"hgrn_o_dw_{tag}"))
            dq_, df_, di_, dg_, dlb_, dgn_ = _hgrn_bwd(proj, lower_bounds[j][None, :], gn[j], o_pre, states, dog, f"hgrn_bwd_{tag}")
            dproj = jnp.concatenate([dq_, df_, di_, dg_], axis=-1)
            d_lbnd[j] = dlb_.sum(0).reshape(1, HK)
            d_gn[j] = dgn_.sum((0, 1))
            dh = r3(_mm_nt(r2(dproj), col_w("hgrn_w_in", j), F32, f"hgrn_in_dx_{tag}"))
            gw["hgrn_w_in"][j] = _mm_tn(r2(h), r2(dproj), 4, BF16, f"hgrn_in_dw_{tag}")
        dout, d_scale[sub], d_shift[sub] = _mod_bwd(dh, dxr, xs, scale[sub], f"mod_bwd_{tag}a")
    grad_x = dout

    slabs = [jnp.stack(gw[k], axis=1) for k in names]
    received = _scatter_chips(slabs, "scatter_grads")
    sums = [_sum4(s, r, k_me, f"sum4_{k}") for k, s, r in zip(names, slabs, received)]
    others = _swap_sibling(sums, "swap_sums")
    g_main = {k: (a_, b_) for k, a_, b_ in zip(names, sums, others)}

    dmod = jnp.stack([jnp.concatenate([d_shift[s_][:, 0], d_scale[s_][:, 0], d_gate[s_][:, 0]], axis=-1) for s_ in range(n_sub)])
    dmod_rows = _pad_rows(dmod.reshape(n_sub * B, 3 * D), -(-n_sub * B // 8) * 8)
    dmod_all = _allgather8(dmod_rows, "gather_dmod")[:, : n_sub * B].reshape(8, n_sub, B, 3 * D)
    dmod_all = jnp.transpose(dmod_all, (1, 0, 2, 3)).reshape(n_sub, Bg, 3 * D)
    dmod_mine = lax.dynamic_slice_in_dim(dmod_all, k_me * n_ada, n_ada, axis=2)
    g_ada_w, g_ada_b = _ada_bwd(c_all, dmod_mine, "ada_bwd")
    g_ada_w = g_ada_w.reshape(ada_w.shape)
    g_ada_b = g_ada_b.reshape(ada_b.shape)

    small = [jnp.stack(d_gq).reshape(-1), jnp.stack(d_gkv).reshape(-1), jnp.stack(d_gn).reshape(-1),
             jnp.stack(d_lbnd).reshape(-1), jnp.stack([d.sum(0) for d in d_lng]).reshape(-1),
             jnp.stack([d.sum(0) for d in d_lnb]).reshape(-1)]
    sizes = [s_.shape[0] for s_ in small]
    flat = jnp.concatenate(small)
    rows_small = -(-flat.shape[0] // (8 * LANES)) * 8
    flat = jnp.pad(flat, (0, rows_small * LANES - flat.shape[0])).reshape(rows_small, LANES)
    tot = _allgather8(flat, "gather_small")
    acc = tot[0]
    for d in range(1, 8):
        acc = acc + tot[d]
    acc = acc.reshape(-1)
    offs = [0]
    for s_ in sizes:
        offs.append(offs[-1] + s_)
    g_q_norm = acc[offs[0] : offs[1]].reshape(mla_q_norm.shape)
    g_kv_norm = acc[offs[1] : offs[2]].reshape(mla_kv_norm.shape)
    g_g_norm = acc[offs[2] : offs[3]].reshape(hgrn_g_norm.shape)
    g_lbnd = acc[offs[3] : offs[4]].reshape(n_hgrn, HK)
    g_lb_full = lower_bounds_vjp(g_lbnd)[0]
    g_hgrn_lb = lax.dynamic_slice_in_dim(g_lb_full, k_me * lbw, lbw, axis=1)
    g_lng = lax.dynamic_slice_in_dim(acc[offs[4] : offs[5]].reshape(n_sub, D), k_me * dq, dq, axis=1).reshape(ln_g.shape)
    g_lnb = lax.dynamic_slice_in_dim(acc[offs[5] : offs[6]].reshape(n_sub, D), k_me * dq, dq, axis=1).reshape(ln_b.shape)

    weights = dict(mla_w_in=mla_w_in, mla_q_norm=mla_q_norm, mla_w_qb=mla_w_qb, mla_kv_norm=mla_kv_norm, mla_w_kvb=mla_w_kvb,
                   mla_w_o=mla_w_o, hgrn_lb=hgrn_lb, hgrn_w_in=hgrn_w_in, hgrn_g_norm=hgrn_g_norm, hgrn_w_o=hgrn_w_o,
                   ffn_w_in=ffn_w_in, ffn_w_out=ffn_w_out, ada_w=ada_w, ada_b=ada_b, ln_g=ln_g, ln_b=ln_b)
    moms = dict(mla_w_in=(m_mla_w_in, v_mla_w_in), mla_q_norm=(m_mla_q_norm, v_mla_q_norm), mla_w_qb=(m_mla_w_qb, v_mla_w_qb),
                mla_kv_norm=(m_mla_kv_norm, v_mla_kv_norm), mla_w_kvb=(m_mla_w_kvb, v_mla_w_kvb), mla_w_o=(m_mla_w_o, v_mla_w_o),
                hgrn_lb=(m_hgrn_lb, v_hgrn_lb), hgrn_w_in=(m_hgrn_w_in, v_hgrn_w_in), hgrn_g_norm=(m_hgrn_g_norm, v_hgrn_g_norm),
                hgrn_w_o=(m_hgrn_w_o, v_hgrn_w_o), ffn_w_in=(m_ffn_w_in, v_ffn_w_in), ffn_w_out=(m_ffn_w_out, v_ffn_w_out),
                ada_w=(m_ada_w, v_ada_w), ada_b=(m_ada_b, v_ada_b), ln_g=(m_ln_g, v_ln_g), ln_b=(m_ln_b, v_ln_b))
    grads = dict(mla_q_norm=(g_q_norm,), mla_kv_norm=(g_kv_norm,), hgrn_lb=(g_hgrn_lb,), hgrn_g_norm=(g_g_norm,),
                 ada_w=(g_ada_w,), ada_b=(g_ada_b,), ln_g=(g_lng,), ln_b=(g_lnb,), **g_main)
    res = {k: _adamw(weights[k], [g_.reshape(weights[k].shape) for g_ in grads[k]], moms[k][0], moms[k][1], f"adamw_{k}")
           for k in weights}
    order = list(weights)
    return (loss, grad_x, *[res[k][0] for k in order], *[res[k][1] for k in order], *[res[k][2] for k in order],
            *[res[k][3] for k in order])
```

```python
import functools

import jax
import jax.numpy as jnp
from jax import lax
from jax.experimental import pallas as pl
from jax.experimental.pallas import tpu as pltpu

F32 = jnp.float32
BF16 = jnp.bfloat16
SDS = jax.ShapeDtypeStruct
MESH = pl.DeviceIdType.MESH
HI = lax.Precision.HIGHEST
MID = lax.Precision.HIGH

MLA_HEADS, QK_NOPE, QK_ROPE, V_HEAD = 16, 64, 32, 64
Q_LORA, KV_LORA = 768, 256
QK_DIM = QK_NOPE + QK_ROPE
ROPE_THETA = 10000.0
HGRN_K = 128
HGRN_CHUNK = 64
HGRN_SUB = 16
HGRN_PAR = 2
LN_EPS, RMS_EPS = 1e-5, 1e-6
ADAM_LR, ADAM_B1, ADAM_B2, ADAM_EPS, ADAM_WD, ADAM_STEP = 0.001, 0.9, 0.999, 1e-08, 0.01, 10
NEG = -1e30

VMEM_LIMIT_BYTES = 56 * 1024 * 1024
LANES = 128


def _cparams(*sem):
    return pltpu.CompilerParams(dimension_semantics=sem if sem else None, vmem_limit_bytes=VMEM_LIMIT_BYTES)


def _pick_tile(n, cap):
    best = 0
    for t in range(LANES, min(n, cap) + 1, LANES):
        if n % t == 0:
            best = t
    return best if best else n


def _bdot(a, b):
    return jnp.dot(a.astype(BF16), b.astype(BF16), preferred_element_type=F32)


def _bdot_nt(a, b):
    return lax.dot_general(a.astype(BF16), b.astype(BF16), (((1,), (1,)), ((), ())), preferred_element_type=F32)


def _bdot_tn(a, b):
    return lax.dot_general(a.astype(BF16), b.astype(BF16), (((0,), (0,)), ((), ())), preferred_element_type=F32)


def _hdot(a, b):
    return jnp.dot(a, b, precision=HI, preferred_element_type=F32)


def _mdot(a, b):
    return jnp.dot(a, b, precision=MID, preferred_element_type=F32)


def _mdot_nt(a, b):
    return lax.dot_general(a, b, (((1,), (1,)), ((), ())), precision=MID, preferred_element_type=F32)


def _mdot_tn(a, b):
    return lax.dot_general(a, b, (((0,), (0,)), ((), ())), precision=MID, preferred_element_type=F32)


def _mm_nn(a, w, out_dtype, name):
    M, K = a.shape
    G, _, n = w.shape
    tm = min(512, M)
    tn = _pick_tile(n, 1536)
    nps = n // tn

    def body(a_ref, w_ref, o_ref):
        o_ref[...] = _bdot(a_ref[...], w_ref[...]).astype(o_ref.dtype)

    return pl.pallas_call(
        body,
        grid=(G * nps, M // tm),
        in_specs=[
            pl.BlockSpec((tm, K), lambda j, i: (i, 0)),
            pl.BlockSpec((None, K, tn), lambda j, i: (j // nps, 0, j % nps)),
        ],
        out_specs=pl.BlockSpec((tm, tn), lambda j, i: (i, j)),
        out_shape=SDS((M, G * n), out_dtype),
        name=name,
        compiler_params=_cparams("parallel", "parallel"),
    )(a, w)


def _mm_nt(a, w, out_dtype, name):
    M = a.shape[0]
    G, K, n = w.shape
    tm = min(512, M)
    tk = _pick_tile(K, 1536)

    def body(a_ref, w_ref, o_ref, acc_ref):
        s = pl.program_id(2)

        @pl.when(s == 0)
        def _():
            acc_ref[...] = jnp.zeros_like(acc_ref)

        acc_ref[...] += _bdot_nt(a_ref[...], w_ref[...])

        @pl.when(s == G - 1)
        def _():
            o_ref[...] = acc_ref[...].astype(o_ref.dtype)

    return pl.pallas_call(
        body,
        grid=(K // tk, M // tm, G),
        in_specs=[
            pl.BlockSpec((tm, n), lambda kb, i, s: (i, s)),
            pl.BlockSpec((None, tk, n), lambda kb, i, s: (s, kb, 0)),
        ],
        out_specs=pl.BlockSpec((tm, tk), lambda kb, i, s: (i, kb)),
        out_shape=SDS((M, K), out_dtype),
        scratch_shapes=[pltpu.VMEM((tm, tk), F32)],
        name=name,
        compiler_params=_cparams("parallel", "parallel", "arbitrary"),
    )(a, w)


def _mm_tn(a, d, G, out_dtype, name):
    T, K = a.shape
    n = d.shape[1] // G
    tk = _pick_tile(K, 256)
    tn = _pick_tile(n, 1536)
    nps = n // tn

    def body(a_ref, d_ref, o_ref):
        o_ref[...] = _bdot_tn(a_ref[...], d_ref[...]).astype(o_ref.dtype)

    return pl.pallas_call(
        body,
        grid=(G * nps, K // tk),
        in_specs=[
            pl.BlockSpec((T, tk), lambda j, i: (0, i)),
            pl.BlockSpec((T, tn), lambda j, i: (0, j)),
        ],
        out_specs=pl.BlockSpec((None, tk, tn), lambda j, i: (j // nps, i, j % nps)),
        out_shape=SDS((G, K, n), out_dtype),
        name=name,
        compiler_params=_cparams("parallel", "parallel"),
    )(a, d)


def _rows_call(body, name, B, S, ins, outs, ts=256):
    ts = min(ts, S)
    in_specs, args = [], []
    for arr, kind in ins:
        W = arr.shape[-1]
        if kind == "row":
            in_specs.append(pl.BlockSpec((None, ts, W), lambda b, s: (b, s, 0)))
        elif kind == "ex":
            in_specs.append(pl.BlockSpec((None, 1, W), lambda b, s: (b, 0, 0)))
        else:
            in_specs.append(pl.BlockSpec((1, W), lambda b, s: (0, 0)))
        args.append(arr)
    out_specs, out_shape = [], []
    for W, dt, kind in outs:
        if kind == "row":
            out_specs.append(pl.BlockSpec((None, ts, W), lambda b, s: (b, s, 0)))
            out_shape.append(SDS((B, S, W), dt))
        else:
            out_specs.append(pl.BlockSpec((None, 1, W), lambda b, s: (b, 0, 0)))
            out_shape.append(SDS((B, 1, W), dt))
    return pl.pallas_call(
        body,
        grid=(B, S // ts),
        in_specs=in_specs,
        out_specs=out_specs,
        out_shape=out_shape,
        name=name,
        compiler_params=_cparams("parallel", "arbitrary"),
    )(*args)


def _acc(ref, val):
    @pl.when(pl.program_id(1) == 0)
    def _():
        ref[...] = jnp.zeros_like(ref)

    ref[...] += val


def _mod_fn(x, sc, sh):
    return x * (1.0 + sc) + sh


def _ln_fn(alpha, x, y, gate, g, b):
    z = alpha * x + (1.0 + gate) * y
    mu = jnp.mean(z, -1, keepdims=True)
    var = jnp.mean(jnp.square(z - mu), -1, keepdims=True)
    return (z - mu) * lax.rsqrt(var + LN_EPS) * g + b


def _modulate(x, sc, sh, name):
    B, S, D = x.shape

    def body(x_ref, sc_ref, sh_ref, h_ref):
        h_ref[...] = _mod_fn(x_ref[...], sc_ref[...], sh_ref[...]).astype(BF16)

    return _rows_call(body, name, B, S, [(x, "row"), (sc, "ex"), (sh, "ex")], [(D, BF16, "row")])[0]


def _ln_fwd(alpha, x, y, gate, g, b, name):
    B, S, D = x.shape

    def body(x_ref, y_ref, gate_ref, g_ref, b_ref, o_ref):
        o_ref[...] = _ln_fn(alpha, x_ref[...], y_ref[...], gate_ref[...], g_ref[...], b_ref[...])

    return _rows_call(
        body, name, B, S, [(x, "row"), (y, "row"), (gate, "ex"), (g, "par"), (b, "par")], [(D, F32, "row")]
    )[0]


def _ln_bwd(alpha, dout, x, y, gate, g, b, name):
    B, S, D = x.shape

    def body(do_ref, x_ref, y_ref, gate_ref, g_ref, b_ref, dxr_ref, dy_ref, dgate_ref, dg_ref, db_ref):
        _, vjp = jax.vjp(
            functools.partial(_ln_fn, alpha), x_ref[...], y_ref[...], gate_ref[...], g_ref[...], b_ref[...]
        )
        dx, dy, dgate, dg, db = vjp(do_ref[...])
        dxr_ref[...] = dx
        dy_ref[...] = dy.astype(BF16)
        _acc(dgate_ref, dgate)
        _acc(dg_ref, dg)
        _acc(db_ref, db)

    return _rows_call(
        body,
        name,
        B,
        S,
        [(dout, "row"), (x, "row"), (y, "row"), (gate, "ex"), (g, "par"), (b, "par")],
        [(D, F32, "row"), (D, BF16, "row"), (D, F32, "acc"), (D, F32, "acc"), (D, F32, "acc")],
    )


def _mod_bwd(dh, dxr, x, sc, name):
    B, S, D = x.shape

    def body(dh_ref, dxr_ref, x_ref, sc_ref, dx_ref, dsc_ref, dsh_ref):
        dh_v = dh_ref[...]
        dx_ref[...] = dxr_ref[...] + dh_v * (1.0 + sc_ref[...])
        _acc(dsc_ref, jnp.sum(dh_v * x_ref[...], axis=0, keepdims=True))
        _acc(dsh_ref, jnp.sum(dh_v, axis=0, keepdims=True))

    return _rows_call(
        body,
        name,
        B,
        S,
        [(dh, "row"), (dxr, "row"), (x, "row"), (sc, "ex")],
        [(D, F32, "row"), (D, F32, "acc"), (D, F32, "acc")],
    )


def _loss_head(y, target, name):
    B, S, D = y.shape

    def body(y_ref, t_ref, l_ref, dy_ref):
        e = y_ref[...] - t_ref[...]
        dy_ref[...] = e * (1.0 / D)
        part = 0.5 * jnp.sum(jnp.sum(e * e, axis=1, keepdims=True) * (1.0 / D), axis=0, keepdims=True)
        _acc(l_ref, jnp.broadcast_to(part, (1, LANES)))

    loss, dy = _rows_call(
        body, name, B, S, [(y, "row"), (target, "row")], [(LANES, F32, "acc"), (D, F32, "row")]
    )
    return jnp.sum(loss[:, 0, 0]), dy


def _swiglu_fn(u):
    F = u.shape[-1] // 2
    return jax.nn.silu(u[:, :F]) * u[:, F:]


def _swiglu_fwd(u, name):
    B, S, F2 = u.shape

    def body(u_ref, a_ref):
        a_ref[...] = _swiglu_fn(u_ref[...]).astype(BF16)

    return _rows_call(body, name, B, S, [(u, "row")], [(F2 // 2, BF16, "row")])[0]


def _swiglu_bwd(u, da, name):
    B, S, F2 = u.shape

    def body(u_ref, da_ref, du_ref):
        _, vjp = jax.vjp(_swiglu_fn, u_ref[...])
        du_ref[...] = vjp(da_ref[...])[0].astype(BF16)

    return _rows_call(body, name, B, S, [(u, "row"), (da, "row")], [(F2, BF16, "row")])[0]


def _rms_fn(x, g):
    return x * lax.rsqrt(jnp.mean(jnp.square(x), -1, keepdims=True) + RMS_EPS) * g


def _mla_mid_fwd(proj, gq, gkv, name):
    B, S, _ = proj.shape

    def body(p_ref, gq_ref, gkv_ref, qn_ref, kvn_ref):
        p = p_ref[...]
        qn_ref[...] = _rms_fn(p[:, :Q_LORA], gq_ref[...]).astype(BF16)
        kvn_ref[...] = _rms_fn(p[:, Q_LORA : Q_LORA + KV_LORA], gkv_ref[...]).astype(BF16)

    return _rows_call(
        body, name, B, S, [(proj, "row"), (gq, "par"), (gkv, "par")], [(Q_LORA, BF16, "row"), (KV_LORA, BF16, "row")]
    )


def _mla_mid_bwd(proj, dqn, dkvn, dkr, gq, gkv, name):
    B, S, W = proj.shape

    def body(p_ref, dqn_ref, dkvn_ref, dkr_ref, gq_ref, gkv_ref, dp_ref, dgq_ref, dgkv_ref):
        p = p_ref[...]
        _, vq = jax.vjp(_rms_fn, p[:, :Q_LORA], gq_ref[...])
        dql, dgq = vq(dqn_ref[...])
        _, vkv = jax.vjp(_rms_fn, p[:, Q_LORA : Q_LORA + KV_LORA], gkv_ref[...])
        dkvl, dgkv = vkv(dkvn_ref[...])
        dp_ref[:, :Q_LORA] = dql.astype(BF16)
        dp_ref[:, Q_LORA : Q_LORA + KV_LORA] = dkvl.astype(BF16)
        dp_ref[:, Q_LORA + KV_LORA :] = dkr_ref[...].astype(BF16)
        _acc(dgq_ref, dgq)
        _acc(dgkv_ref, dgkv)

    return _rows_call(
        body,
        name,
        B,
        S,
        [(proj, "row"), (dqn, "row"), (dkvn, "row"), (dkr, "row"), (gq, "par"), (gkv, "par")],
        [(W, BF16, "row"), (Q_LORA, F32, "acc"), (KV_LORA, F32, "acc")],
    )


def _rope(x, cos, sin):
    h = QK_ROPE // 2
    x1, x2 = x[:, :h], x[:, h:]
    return jnp.concatenate([x1 * cos - x2 * sin, x1 * sin + x2 * cos], axis=1)


def _rope_t(dy, cos, sin):
    h = QK_ROPE // 2
    d1, d2 = dy[:, :h], dy[:, h:]
    return jnp.concatenate([d1 * cos + d2 * sin, d2 * cos - d1 * sin], axis=1)


def _heads_call(body, name, B, S, ins, outs, ts=256):
    ts = min(ts, S)
    in_specs, args = [], []
    for arr, kind in ins:
        if kind == "row":
            in_specs.append(pl.BlockSpec((None, ts, arr.shape[-1]), lambda b, s: (b, s, 0)))
        else:
            in_specs.append(pl.BlockSpec((arr.shape[0], None, ts, arr.shape[-1]), lambda b, s: (0, b, s, 0)))
        args.append(arr)
    out_specs, out_shape = [], []
    for shape, dt, kind in outs:
        if kind == "row":
            out_specs.append(pl.BlockSpec((None, ts, shape[-1]), lambda b, s: (b, s, 0)))
        else:
            out_specs.append(pl.BlockSpec((shape[0], None, ts, shape[-1]), lambda b, s: (0, b, s, 0)))
        out_shape.append(SDS(shape, dt))
    return pl.pallas_call(
        body,
        grid=(B, S // ts),
        in_specs=in_specs,
        out_specs=out_specs,
        out_shape=out_shape,
        name=name,
        compiler_params=_cparams("parallel", "parallel"),
    )(*args)


def _mla_prep_fwd(q, kv, proj, cos, sin, name):
    B, S, _ = q.shape
    H = MLA_HEADS

    def body(q_ref, kv_ref, p_ref, cos_ref, sin_ref, qh_ref, kh_ref, vh_ref):
        cos_v, sin_v = cos_ref[...], sin_ref[...]
        kr = _rope(p_ref[:, Q_LORA + KV_LORA :], cos_v, sin_v).astype(BF16)
        for h in range(H):
            qn = q_ref[:, h * QK_DIM : h * QK_DIM + QK_NOPE]
            qr = _rope(q_ref[:, h * QK_DIM + QK_NOPE : (h + 1) * QK_DIM], cos_v, sin_v)
            qh_ref[h] = jnp.concatenate([qn, qr], axis=1).astype(BF16)
            kn = kv_ref[:, h * 128 : h * 128 + QK_NOPE].astype(BF16)
            kh_ref[h] = jnp.concatenate([kn, kr], axis=1)
            vh_ref[h] = kv_ref[:, h * 128 + QK_NOPE : (h + 1) * 128].astype(BF16)

    return _heads_call(
        body,
        name,
        B,
        S,
        [(q, "row"), (kv, "row"), (proj, "row"), (cos, "row"), (sin, "row")],
        [((H, B, S, QK_DIM), BF16, "heads"), ((H, B, S, QK_DIM), BF16, "heads"), ((H, B, S, V_HEAD), BF16, "heads")],
    )


def _mla_prep_bwd(dqh, dkh, dvh, cos, sin, name):
    H, B, S, _ = dqh.shape

    def body(dqh_ref, dkh_ref, dvh_ref, cos_ref, sin_ref, dq_ref, dkv_ref, dkr_ref):
        cos_v, sin_v = cos_ref[...], sin_ref[...]
        dkr = jnp.zeros((cos_v.shape[0], QK_ROPE), F32)
        for h in range(H):
            dqv = dqh_ref[h].astype(F32)
            dq_ref[:, h * QK_DIM : h * QK_DIM + QK_NOPE] = dqv[:, :QK_NOPE].astype(BF16)
            dq_ref[:, h * QK_DIM + QK_NOPE : (h + 1) * QK_DIM] = _rope_t(dqv[:, QK_NOPE:], cos_v, sin_v).astype(BF16)
            dkv = dkh_ref[h].astype(F32)
            dkv_ref[:, h * 128 : h * 128 + QK_NOPE] = dkv[:, :QK_NOPE].astype(BF16)
            dkv_ref[:, h * 128 + QK_NOPE : (h + 1) * 128] = dvh_ref[h]
            dkr = dkr + dkv[:, QK_NOPE:]
        dkr_ref[...] = _rope_t(dkr, cos_v, sin_v)

    return _heads_call(
        body,
        name,
        B,
        S,
        [(dqh, "heads"), (dkh, "heads"), (dvh, "heads"), (cos, "row"), (sin, "row")],
        [((B, S, H * QK_DIM), BF16, "row"), ((B, S, H * 128), BF16, "row"), ((B, S, QK_ROPE), F32, "row")],
    )


LOG2E = 1.4426950408889634


def _tril_mask(t):
    return lax.broadcasted_iota(jnp.int32, (t, t), 0) >= lax.broadcasted_iota(jnp.int32, (t, t), 1)


def _attn_fwd(qh, kh, vh, name):
    H, B, S, _ = qh.shape
    t = min(256, S)
    scale = QK_DIM**-0.5
    c2 = scale * LOG2E

    def body(q_ref, k_ref, v_ref, o_ref, lse_ref):
        i = pl.program_id(2)
        qs = [q_ref[0], q_ref[1]]

        def step(j, carry, diagonal):
            rows = pl.ds(pl.multiple_of(j * t, t), t)
            out = []
            for hh in range(2):
                m, l, acc = carry[hh]
                s = _bdot_nt(qs[hh], k_ref[hh, rows, :])
                if diagonal:
                    s = jnp.where(_tril_mask(t), s, NEG)
                m_new = jnp.maximum(m, jnp.max(s, axis=1, keepdims=True))
                p = jnp.exp2((s - m_new) * c2)
                a = jnp.exp2((m - m_new) * c2)
                l = a * l + jnp.sum(p, axis=1, keepdims=True)
                acc = a * acc + _bdot(p, v_ref[hh, rows, :])
                out.append((m_new, l, acc))
            return tuple(out)

        one = (jnp.full((t, 1), NEG, F32), jnp.zeros((t, 1), F32), jnp.zeros((t, V_HEAD), F32))
        carry = lax.fori_loop(0, i, lambda j, cy: step(j, cy, False), (one, one))
        carry = step(i, carry, True)
        outs = []
        for hh in range(2):
            m, l, acc = carry[hh]
            outs.append(acc / l)
            lse_ref[hh] = m * scale + jnp.log(l)
        o_ref[...] = jnp.concatenate(outs, axis=1).astype(BF16)

    return pl.pallas_call(
        body,
        grid=(B, H // 2, S // t),
        in_specs=[
            pl.BlockSpec((2, None, t, QK_DIM), lambda b, p, i: (p, b, i, 0)),
            pl.BlockSpec((2, None, S, QK_DIM), lambda b, p, i: (p, b, 0, 0)),
            pl.BlockSpec((2, None, S, V_HEAD), lambda b, p, i: (p, b, 0, 0)),
        ],
        out_specs=[
            pl.BlockSpec((None, t, 2 * V_HEAD), lambda b, p, i: (b, i, p)),
            pl.BlockSpec((2, None, t, 1), lambda b, p, i: (p, b, i, 0)),
        ],
        out_shape=[SDS((B, S, H * V_HEAD), BF16), SDS((H, B, S, 1), F32)],
        name=name,
        compiler_params=_cparams("parallel", "parallel", "arbitrary"),
    )(qh, kh, vh)


def _attn_bwd(qh, kh, vh, o, do, lse, name):
    H, B, S, _ = qh.shape
    t = min(256, S)
    nq = S // t
    scale = QK_DIM**-0.5
    c2 = scale * LOG2E

    def body(q_ref, k_ref, v_ref, o_ref, do_ref, lse_ref, dq_ref, dk_ref, dv_ref, dq_acc, delta_ref, lse2_ref):
        prod = o_ref[...].astype(F32) * do_ref[...].astype(F32)
        for hh in range(2):
            delta_ref[hh] = jnp.sum(prod[:, hh * V_HEAD : (hh + 1) * V_HEAD], axis=1, keepdims=True)
            lse2_ref[hh] = lse_ref[hh] * LOG2E
        dq_acc[...] = jnp.zeros_like(dq_acc)

        def kloop(j, _):
            krows = pl.ds(pl.multiple_of(j * t, t), t)
            ks = [k_ref[0, krows, :], k_ref[1, krows, :]]
            vs = [v_ref[0, krows, :], v_ref[1, krows, :]]

            def qstep(i, carry, diagonal):
                qrows = pl.ds(pl.multiple_of(i * t, t), t)
                do_i = do_ref[qrows, :]
                out = []
                for hh in range(2):
                    dk, dv = carry[hh]
                    q = q_ref[hh, qrows, :]
                    do_h = do_i[:, hh * V_HEAD : (hh + 1) * V_HEAD]
                    s = _bdot_nt(q, ks[hh])
                    p = jnp.exp2(s * c2 - lse2_ref[hh, qrows, :])
                    if diagonal:
                        p = jnp.where(_tril_mask(t), p, 0.0)
                    dv = dv + _bdot_tn(p, do_h)
                    dp = _bdot_nt(do_h, vs[hh])
                    ds = (p * (dp - delta_ref[hh, qrows, :])).astype(BF16)
                    dk = dk + _bdot_tn(ds, q)
                    dq_acc[hh, qrows, :] += _bdot(ds, ks[hh])
                    out.append((dk, dv))
                return tuple(out)

            one = (jnp.zeros((t, QK_DIM), F32), jnp.zeros((t, V_HEAD), F32))
            carry = qstep(j, (one, one), True)
            carry = lax.fori_loop(j + 1, nq, lambda i, cy: qstep(i, cy, False), carry)
            for hh in range(2):
                dk_ref[hh, krows, :] = (carry[hh][0] * scale).astype(BF16)
                dv_ref[hh, krows, :] = carry[hh][1].astype(BF16)
            return 0

        lax.fori_loop(0, nq, kloop, 0)
        dq_ref[...] = (dq_acc[...] * scale).astype(BF16)

    hspec = lambda w: pl.BlockSpec((2, None, S, w), lambda b, p: (p, b, 0, 0))
    ospec = pl.BlockSpec((None, S, 2 * V_HEAD), lambda b, p: (b, 0, p))
    return pl.pallas_call(
        body,
        grid=(B, H // 2),
        in_specs=[hspec(QK_DIM), hspec(QK_DIM), hspec(V_HEAD), ospec, ospec, hspec(1)],
        out_specs=[hspec(QK_DIM), hspec(QK_DIM), hspec(V_HEAD)],
        out_shape=[SDS((H, B, S, QK_DIM), BF16), SDS((H, B, S, QK_DIM), BF16), SDS((H, B, S, V_HEAD), BF16)],
        scratch_shapes=[pltpu.VMEM((2, S, QK_DIM), F32), pltpu.VMEM((2, S, 1), F32), pltpu.VMEM((2, S, 1), F32)],
        name=name,
        compiler_params=_cparams("parallel", "parallel"),
    )(qh, kh, vh, o, do, lse)


def _hgrn_pre(q, fx, lb):
    sig = jax.nn.sigmoid(fx)
    f = lb + (1.0 - lb) * sig
    return jax.nn.silu(q), 1.0 - f, jnp.log(f)


def _hgrn_gate(o, gg, gn):
    return _rms_fn(o, gn) * jax.nn.silu(gg)


def _tri(n, lower):
    r = lax.broadcasted_iota(jnp.int32, (n, n), 0)
    c = lax.broadcasted_iota(jnp.int32, (n, n), 1)
    return ((r >= c) if lower else (r <= c)).astype(F32)


def _hgrn_intra_fwd(qs, k, v, b):
    C, SB = qs.shape[0], min(HGRN_SUB, qs.shape[0])
    ridx = lax.broadcasted_iota(jnp.int32, (SB, 1), 0)
    outs = []
    for i in range(C // SB):
        r0 = i * SB
        qi, ki, vi, bi = qs[r0 : r0 + SB], k[r0 : r0 + SB], v[r0 : r0 + SB], b[r0 : r0 + SB]
        acc = jnp.zeros((SB, v.shape[1]), F32)
        for s in range(SB):
            mask = ridx >= s
            e = jnp.exp(jnp.where(mask, bi - bi[s : s + 1], 0.0))
            a = jnp.sum(jnp.where(mask, qi * ki[s : s + 1] * e, 0.0), axis=1, keepdims=True)
            acc = acc + a * vi[s : s + 1]
        if i > 0:
            ref = bi[0:1]
            qt = qi * jnp.exp(bi - ref)
            kt = k[:r0] * jnp.exp(ref - b[:r0])
            acc = acc + _bdot(_mdot_nt(qt, kt), v[:r0])
        outs.append(acc)
    return jnp.concatenate(outs, axis=0)


def _hgrn_intra_bwd(qs, k, v, b, do):
    C, SB = qs.shape[0], min(HGRN_SUB, qs.shape[0])
    nb = C // SB
    ridx = lax.broadcasted_iota(jnp.int32, (SB, 1), 0)
    dq_p = [None] * nb
    dk_p = [jnp.zeros((SB, k.shape[1]), F32) for _ in range(nb)]
    dv_p = [jnp.zeros((SB, v.shape[1]), F32) for _ in range(nb)]
    for i in range(nb):
        r0 = i * SB
        qi, ki, vi, bi, doi = qs[r0 : r0 + SB], k[r0 : r0 + SB], v[r0 : r0 + SB], b[r0 : r0 + SB], do[r0 : r0 + SB]
        dqi = jnp.zeros_like(qi)
        dki = jnp.zeros_like(ki)
        dvi = jnp.zeros_like(vi)
        for s in range(SB):
            mask = ridx >= s
            e = jnp.where(mask, jnp.exp(jnp.where(mask, bi - bi[s : s + 1], 0.0)), 0.0)
            da = jnp.sum(doi * vi[s : s + 1], axis=1, keepdims=True)
            a = jnp.sum(qi * ki[s : s + 1] * e, axis=1, keepdims=True)
            dqi = dqi + da * (ki[s : s + 1] * e)
            dk_row = jnp.sum(da * qi * e, axis=0, keepdims=True)
            dv_row = jnp.sum(a * doi, axis=0, keepdims=True)
            dki = jnp.where(ridx == s, dki + dk_row, dki)
            dvi = jnp.where(ridx == s, dvi + dv_row, dvi)
        if i > 0:
            ref = bi[0:1]
            eq = jnp.exp(bi - ref)
            ek = jnp.exp(ref - b[:r0])
            qt = qi * eq
            kt = k[:r0] * ek
            A = _mdot_nt(qt, kt)
            dA = _bdot_nt(doi, v[:r0])
            dvl = _bdot_tn(A, doi)
            dqi = dqi + _mdot(dA, kt) * eq
            dkl = _mdot_tn(dA, qt) * ek
            for j in range(i):
                dk_p[j] = dk_p[j] + dkl[j * SB : (j + 1) * SB]
                dv_p[j] = dv_p[j] + dvl[j * SB : (j + 1) * SB]
        dq_p[i] = dqi
        dk_p[i] = dk_p[i] + dki
        dv_p[i] = dv_p[i] + dvi
    return jnp.concatenate(dq_p, axis=0), jnp.concatenate(dk_p, axis=0), jnp.concatenate(dv_p, axis=0)


def _hgrn_fwd(proj, lb, gn, name):
    B, S, W = proj.shape
    HK = W // 4
    H = HK // HGRN_K
    C = min(HGRN_CHUNK, S)
    N = S // C

    HP = HGRN_PAR if H % HGRN_PAR == 0 else 1
    WP = HP * HGRN_K

    def body(q_ref, f_ref, i_ref, g_ref, lb_ref, gn_ref, og_ref, o_ref, st_ref):
        gn_v = gn_ref[...]
        tril = _tri(C, True)

        def chunk(n, sts):
            rows = pl.ds(pl.multiple_of(n * C, C), C)
            out = []
            for hh in range(HP):
                ln = slice(hh * HGRN_K, (hh + 1) * HGRN_K)
                st = sts[hh]
                qs, k, g = _hgrn_pre(q_ref[rows, ln], f_ref[rows, ln], lb_ref[:, ln])
                v = i_ref[rows, ln]
                b = _hdot(tril, g)
                st_ref[hh, n] = st
                o = _hgrn_intra_fwd(qs, k, v, b) + _bdot_nt(qs * jnp.exp(b), st)
                bl = b[C - 1 : C]
                out.append(st * jnp.exp(bl) + _bdot_tn(v, k * jnp.exp(bl - b)))
                o_ref[rows, ln] = o
                og_ref[rows, ln] = _hgrn_gate(o, g_ref[rows, ln], gn_v).astype(BF16)
            return tuple(out)

        lax.fori_loop(0, N, chunk, tuple(jnp.zeros((HGRN_K, HGRN_K), F32) for _ in range(HP)))

    col = lambda part: pl.BlockSpec((None, S, WP), lambda b, h: (b, 0, part * (H // HP) + h))
    return pl.pallas_call(
        body,
        grid=(B, H // HP),
        in_specs=[col(0), col(1), col(2), col(3), pl.BlockSpec((1, WP), lambda b, h: (0, h)), pl.BlockSpec((1, HGRN_K), lambda b, h: (0, 0))],
        out_specs=[col(0), col(0), pl.BlockSpec((None, HP, N, HGRN_K, HGRN_K), lambda b, h: (b, h, 0, 0, 0))],
        out_shape=[SDS((B, S, HK), BF16), SDS((B, S, HK), F32), SDS((B, H, N, HGRN_K, HGRN_K), F32)],
        name=name,
        compiler_params=_cparams("parallel", "parallel"),
    )(proj, proj, proj, proj, lb, gn)


def _hgrn_bwd(proj, lb, gn, o_pre, states, dog, name):
    B, S, W = proj.shape
    HK = W // 4
    H = HK // HGRN_K
    C = min(HGRN_CHUNK, S)
    N = S // C

    HP = HGRN_PAR if H % HGRN_PAR == 0 else 1
    WP = HP * HGRN_K

    def body(q_ref, f_ref, i_ref, g_ref, lb_ref, gn_ref, o_ref, st_ref, dog_ref, dq_ref, df_ref, di_ref, dg_ref, dlb_ref, dgn_ref):
        gn_v = gn_ref[...]
        tril = _tri(C, True)
        triu = _tri(C, False)

        def chunk(idx, carry):
            n = N - 1 - idx
            rows = pl.ds(pl.multiple_of(n * C, C), C)
            out = []
            for hh in range(HP):
                ln = slice(hh * HGRN_K, (hh + 1) * HGRN_K)
                dst, dlb, dgn = carry[hh]
                (qs, k, g), pre_vjp = jax.vjp(_hgrn_pre, q_ref[rows, ln], f_ref[rows, ln], lb_ref[:, ln])
                v = i_ref[rows, ln]
                _, gate_vjp = jax.vjp(_hgrn_gate, o_ref[rows, ln], g_ref[rows, ln], gn_v)
                do, dgg, dgn_c = gate_vjp(dog_ref[rows, ln])
                b = _hdot(tril, g)
                st0 = st_ref[hh, n]
                eb = jnp.exp(b)
                bl = b[C - 1 : C]
                ebl = jnp.exp(bl)
                ekb = jnp.exp(bl - b)
                qe = qs * eb
                kt = k * ekb
                dqs, dk, dv = _hgrn_intra_bwd(qs, k, v, b, do)
                dqs = dqs + _bdot(do, st0) * eb
                dk = dk + _bdot(v, dst) * ekb
                dv = dv + _bdot_nt(kt, dst)
                st1 = st0 * ebl + _bdot_tn(v, kt)
                dbl = jnp.sum(st1 * dst, axis=0, keepdims=True)
                dst = dst * ebl + _bdot_tn(do, qe)
                dgl = _hdot(triu, qs * dqs - k * dk) + dbl
                dq_pre, dfx, dlb_c = pre_vjp((dqs, dk, dgl))
                dq_ref[rows, ln] = dq_pre.astype(BF16)
                df_ref[rows, ln] = dfx.astype(BF16)
                di_ref[rows, ln] = dv.astype(BF16)
                dg_ref[rows, ln] = dgg.astype(BF16)
                out.append((dst, dlb + dlb_c, dgn + dgn_c))
            return tuple(out)

        zero = jnp.zeros((1, HGRN_K), F32)
        one = (jnp.zeros((HGRN_K, HGRN_K), F32), zero, zero)
        res = lax.fori_loop(0, N, chunk, tuple(one for _ in range(HP)))
        for hh in range(HP):
            dlb_ref[hh] = res[hh][1]
            dgn_ref[hh] = res[hh][2]

    col = lambda part: pl.BlockSpec((None, S, WP), lambda b, h: (b, 0, part * (H // HP) + h))
    vec = pl.BlockSpec((None, HP, 1, HGRN_K), lambda b, h: (b, h, 0, 0))
    return pl.pallas_call(
        body,
        grid=(B, H // HP),
        in_specs=[
            col(0), col(1), col(2), col(3),
            pl.BlockSpec((1, WP), lambda b, h: (0, h)),
            pl.BlockSpec((1, HGRN_K), lambda b, h: (0, 0)),
            col(0),
            pl.BlockSpec((None, HP, N, HGRN_K, HGRN_K), lambda b, h: (b, h, 0, 0, 0)),
            col(0),
        ],
        out_specs=[col(0), col(0), col(0), col(0), vec, vec],
        out_shape=[SDS((B, S, HK), BF16)] * 4 + [SDS((B, H, 1, HGRN_K), F32)] * 2,
        name=name,
        compiler_params=_cparams("parallel", "parallel"),
    )(proj, proj, proj, proj, lb, gn, o_pre, states, dog)


def _ada_fwd(c_all, w, b, name):
    Bg, D = c_all.shape
    L, _, n = w.shape

    def body(c_ref, w_ref, b_ref, o_ref):
        o_ref[...] = _bdot(jax.nn.silu(c_ref[...]), w_ref[...]) + b_ref[...]

    return pl.pallas_call(
        body,
        grid=(L,),
        in_specs=[
            pl.BlockSpec((Bg, D), lambda l: (0, 0)),
            pl.BlockSpec((None, D, n), lambda l: (l, 0, 0)),
            pl.BlockSpec((None, 1, n), lambda l: (l, 0, 0)),
        ],
        out_specs=pl.BlockSpec((None, Bg, n), lambda l: (l, 0, 0)),
        out_shape=SDS((L, Bg, n), F32),
        name=name,
        compiler_params=_cparams("parallel"),
    )(c_all, w, b)


def _ada_bwd(c_all, dmod, name):
    Bg, D = c_all.shape
    L, _, n = dmod.shape

    def body(c_ref, d_ref, dw_ref, db_ref):
        d = d_ref[...]
        dw_ref[...] = _bdot_tn(jax.nn.silu(c_ref[...]), d)
        db_ref[...] = jnp.sum(d, axis=0, keepdims=True)

    return pl.pallas_call(
        body,
        grid=(L,),
        in_specs=[pl.BlockSpec((Bg, D), lambda l: (0, 0)), pl.BlockSpec((None, Bg, n), lambda l: (l, 0, 0))],
        out_specs=[pl.BlockSpec((None, D, n), lambda l: (l, 0, 0)), pl.BlockSpec((None, 1, n), lambda l: (l, 0, 0))],
        out_shape=[SDS((L, D, n), F32), SDS((L, 1, n), F32)],
        name=name,
        compiler_params=_cparams("parallel"),
    )(c_all, dmod)


def _adamw(w, gs, m, v, name):
    shape = w.shape
    cols = shape[-1]
    rows = w.size // cols
    tr = rows
    for cand in (512, 256, 128, 64, 32, 16, 8):
        if rows % cand == 0 and cand * cols * 4 <= 2 * 1024 * 1024:
            tr = cand
            break
    as2d = lambda a: a.reshape(rows, cols)
    ng = len(gs)
    c1 = 1.0 / (1.0 - ADAM_B1**ADAM_STEP)
    c2 = 1.0 / (1.0 - ADAM_B2**ADAM_STEP)

    def body(*refs):
        w_ref, m_ref, v_ref = refs[0], refs[1], refs[2]
        g_refs = refs[3 : 3 + ng]
        g_out, d_out, m_out, v_out = refs[3 + ng :]
        g = g_refs[0][...].astype(F32)
        for r in g_refs[1:]:
            g = g + r[...].astype(F32)
        m_new = ADAM_B1 * m_ref[...] + (1.0 - ADAM_B1) * g
        v_new = ADAM_B2 * v_ref[...] + (1.0 - ADAM_B2) * jnp.square(g)
        g_out[...] = g
        m_out[...] = m_new
        v_out[...] = v_new
        d_out[...] = -ADAM_LR * ((m_new * c1) / (jnp.sqrt(v_new * c2) + ADAM_EPS) + ADAM_WD * w_ref[...])

    spec = pl.BlockSpec((tr, cols), lambda i: (i, 0))
    outs = pl.pallas_call(
        body,
        grid=(rows // tr,),
        in_specs=[spec] * (3 + ng),
        out_specs=[spec] * 4,
        out_shape=[SDS((rows, cols), F32)] * 4,
        name=name,
        compiler_params=_cparams("parallel"),
    )(as2d(w), as2d(m), as2d(v), *[as2d(g) for g in gs])
    return tuple(o.reshape(shape) for o in outs)


def _sum4(own, recv, k_me, name):
    shape = own.shape[1:]
    cols = shape[-1]
    rows = own.size // 4 // cols
    tr = rows
    for cand in (512, 256, 128, 64, 32, 16):
        if rows % cand == 0 and cand * cols * 4 <= 2 * 1024 * 1024:
            tr = cand
            break

    def body(k_ref, own_ref, recv_ref, o_ref):
        acc = own_ref[...].astype(F32)
        for r in range(3):
            acc = acc + recv_ref[r].astype(F32)
        o_ref[...] = acc

    out = pl.pallas_call(
        body,
        grid_spec=pltpu.PrefetchScalarGridSpec(
            num_scalar_prefetch=1,
            grid=(rows // tr,),
            in_specs=[
                pl.BlockSpec((None, tr, cols), lambda i, k: (k[0], i, 0)),
                pl.BlockSpec((3, tr, cols), lambda i, k: (0, i, 0)),
            ],
            out_specs=pl.BlockSpec((tr, cols), lambda i, k: (i, 0)),
        ),
        out_shape=SDS((rows, cols), F32),
        name=name,
        compiler_params=_cparams("parallel"),
    )(k_me.reshape(1), own.reshape(4, rows, cols), recv.reshape(3, rows, cols))
    return out.reshape(shape)


def _my_place():
    return lax.axis_index("x"), lax.axis_index("y"), lax.axis_index("c")


def _flip(v, bit):
    return 1 - v if bit else v


def _allgather8(x, name):
    r, n = x.shape

    def body(x_ref, o_ref, send_sems, recv_sems, local_sem):
        mx, my, mc = _my_place()
        me = 4 * mx + 2 * my + mc
        mine = pltpu.make_async_copy(x_ref, o_ref.at[me], local_sem)
        mine.start()
        sends = []
        for rel in range(1, 8):
            peer = (_flip(mx, rel & 4), _flip(my, rel & 2), _flip(mc, rel & 1))
            cp = pltpu.make_async_remote_copy(
                src_ref=x_ref, dst_ref=o_ref.at[me], send_sem=send_sems.at[rel - 1], recv_sem=recv_sems.at[rel - 1],
                device_id=peer, device_id_type=MESH,
            )
            cp.start()
            sends.append(cp)
        for rel in range(1, 8):
            px, py, pc = _flip(mx, rel & 4), _flip(my, rel & 2), _flip(mc, rel & 1)
            pltpu.make_async_remote_copy(
                src_ref=x_ref, dst_ref=o_ref.at[4 * px + 2 * py + pc], send_sem=send_sems.at[rel - 1],
                recv_sem=recv_sems.at[rel - 1], device_id=(px, py, pc), device_id_type=MESH,
            ).wait_recv()
        for cp in sends:
            cp.wait_send()
        mine.wait()

    return pl.pallas_call(
        body,
        out_shape=SDS((8, r, n), x.dtype),
        in_specs=[pl.BlockSpec(memory_space=pl.ANY)],
        out_specs=pl.BlockSpec(memory_space=pl.ANY),
        scratch_shapes=[pltpu.SemaphoreType.DMA((7,)), pltpu.SemaphoreType.DMA((7,)), pltpu.SemaphoreType.DMA],
        name=name,
    )(x)


_HBM = pl.BlockSpec(memory_space=pl.ANY)


def _gather_chips(shards, name):
    n = len(shards)

    def body(*refs):
        ins, outs = refs[:n], refs[n : 2 * n]
        send_sems, recv_sems, local_sems = refs[2 * n :]
        mx, my, mc = _my_place()
        k_me = 2 * mx + my
        started = []
        for i in range(n):
            cp = pltpu.make_async_copy(ins[i], outs[i].at[k_me], local_sems.at[i])
            cp.start()
            started.append(cp)
        sends = []
        for i in range(n):
            for rel in range(1, 4):
                peer = (_flip(mx, rel & 2), _flip(my, rel & 1), mc)
                cp = pltpu.make_async_remote_copy(
                    src_ref=ins[i], dst_ref=outs[i].at[k_me], send_sem=send_sems.at[3 * i + rel - 1],
                    recv_sem=recv_sems.at[3 * i + rel - 1], device_id=peer, device_id_type=MESH,
                )
                cp.start()
                sends.append(cp)
        for i in range(n):
            for rel in range(1, 4):
                px, py = _flip(mx, rel & 2), _flip(my, rel & 1)
                pltpu.make_async_remote_copy(
                    src_ref=ins[i], dst_ref=outs[i].at[2 * px + py], send_sem=send_sems.at[3 * i + rel - 1],
                    recv_sem=recv_sems.at[3 * i + rel - 1], device_id=(px, py, mc), device_id_type=MESH,
                ).wait_recv()
        for cp in sends:
            cp.wait_send()
        for cp in started:
            cp.wait()

    return pl.pallas_call(
        body,
        out_shape=[SDS((4,) + s.shape, s.dtype) for s in shards],
        in_specs=[_HBM] * n,
        out_specs=[_HBM] * n,
        scratch_shapes=[pltpu.SemaphoreType.DMA((3 * n,)), pltpu.SemaphoreType.DMA((3 * n,)), pltpu.SemaphoreType.DMA((n,))],
        name=name,
    )(*shards)


def _scatter_chips(slabs, name):
    n = len(slabs)

    def body(*refs):
        ins, outs = refs[:n], refs[n : 2 * n]
        send_sems, recv_sems = refs[2 * n :]
        mx, my, mc = _my_place()
        sends = []
        for i in range(n):
            for rel in range(1, 4):
                px, py = _flip(mx, rel & 2), _flip(my, rel & 1)
                cp = pltpu.make_async_remote_copy(
                    src_ref=ins[i].at[2 * px + py], dst_ref=outs[i].at[rel - 1], send_sem=send_sems.at[3 * i + rel - 1],
                    recv_sem=recv_sems.at[3 * i + rel - 1], device_id=(px, py, mc), device_id_type=MESH,
                )
                cp.start()
                sends.append(cp)
        for cp in sends:
            cp.wait_recv()
        for cp in sends:
            cp.wait_send()

    return pl.pallas_call(
        body,
        out_shape=[SDS((3,) + s.shape[1:], s.dtype) for s in slabs],
        in_specs=[_HBM] * n,
        out_specs=[_HBM] * n,
        scratch_shapes=[pltpu.SemaphoreType.DMA((3 * n,)), pltpu.SemaphoreType.DMA((3 * n,))],
        name=name,
    )(*slabs)


def _swap_sibling(parts, name):
    n = len(parts)

    def body(*refs):
        ins, outs = refs[:n], refs[n : 2 * n]
        send_sems, recv_sems = refs[2 * n :]
        mx, my, mc = _my_place()
        sends = []
        for i in range(n):
            cp = pltpu.make_async_remote_copy(
                src_ref=ins[i], dst_ref=outs[i], send_sem=send_sems.at[i], recv_sem=recv_sems.at[i],
                device_id=(mx, my, 1 - mc), device_id_type=MESH,
            )
            cp.start()
            sends.append(cp)
        for cp in sends:
            cp.wait_recv()
        for cp in sends:
            cp.wait_send()

    return pl.pallas_call(
        body,
        out_shape=[SDS(s.shape, s.dtype) for s in parts],
        in_specs=[_HBM] * n,
        out_specs=[_HBM] * n,
        scratch_shapes=[pltpu.SemaphoreType.DMA((n,)), pltpu.SemaphoreType.DMA((n,))],
        name=name,
    )(*parts)


def _pad_rows(a, rows):
    return jnp.pad(a, ((0, rows - a.shape[0]), (0, 0)))


def kernel(x, c, positions, mla_w_in, mla_q_norm, mla_w_qb, mla_kv_norm, mla_w_kvb, mla_w_o, hgrn_lb, hgrn_w_in, hgrn_g_norm, hgrn_w_o, ffn_w_in, ffn_w_out, ada_w, ada_b, ln_g, ln_b, loss_target, m_mla_w_in, m_mla_q_norm, m_mla_w_qb, m_mla_kv_norm, m_mla_w_kvb, m_mla_w_o, m_hgrn_lb, m_hgrn_w_in, m_hgrn_g_norm, m_hgrn_w_o, m_ffn_w_in, m_ffn_w_out, m_ada_w, m_ada_b, m_ln_g, m_ln_b, v_mla_w_in, v_mla_q_norm, v_mla_w_qb, v_mla_kv_norm, v_mla_w_kvb, v_mla_w_o, v_hgrn_lb, v_hgrn_w_in, v_hgrn_g_norm, v_hgrn_w_o, v_ffn_w_in, v_ffn_w_out, v_ada_w, v_ada_b, v_ln_g, v_ln_b):
    B, S, D = x.shape
    T = B * S
    depth = ada_w.shape[0]
    n_mla, n_hgrn = mla_w_in.shape[0], hgrn_w_in.shape[0]
    n_sub = 2 * depth
    alpha = (2.0 * depth) ** 0.25
    mx, my, mc = _my_place()
    me = 4 * mx + 2 * my + mc
    k_me = 2 * mx + my
    Bg = 8 * B
    HK = hgrn_w_o.shape[1] * 4
    dq = D // 4

    lbw = hgrn_lb.shape[1]
    first = jnp.zeros((8, max(D, 4 * lbw)), F32)
    first = first.at[:B, :D].set(c).at[B : B + n_hgrn, :lbw].set(hgrn_lb)
    first_all = _allgather8(first, "gather_cond")
    c_all = first_all[:, :B, :D].reshape(Bg, D)
    lb_logits = jnp.concatenate([first_all[2 * k, B : B + n_hgrn, :lbw] for k in range(4)], axis=1)

    def lower_bounds_fn(logits):
        soft = jax.nn.softmax(logits, axis=0)
        return jnp.cumsum(soft, axis=0) - soft[0]

    lower_bounds, lower_bounds_vjp = jax.vjp(lower_bounds_fn, lb_logits)

    n_ada = ada_w.shape[-1]
    mod_part = _ada_fwd(c_all, ada_w.reshape(n_sub, D, n_ada), ada_b.reshape(n_sub, 1, n_ada), "ada_fwd")
    mod_all = _allgather8(mod_part.reshape(n_sub * Bg, n_ada), "gather_mod").reshape(8, n_sub, Bg, n_ada)
    mod = jnp.concatenate([mod_all[2 * k] for k in range(4)], axis=-1)
    mod = lax.dynamic_slice_in_dim(mod, me * B, B, axis=1)
    shift = [mod[j, :, None, :D] for j in range(n_sub)]
    scale = [mod[j, :, None, D : 2 * D] for j in range(n_sub)]
    gate = [mod[j, :, None, 2 * D :] for j in range(n_sub)]

    ln_rows = 2 * n_sub
    ln_local = _pad_rows(jnp.concatenate([ln_g.reshape(n_sub, dq), ln_b.reshape(n_sub, dq)], axis=0), -(-ln_rows // 8) * 8)
    ln_pad = jnp.zeros((ln_local.shape[0], -(-dq // LANES) * LANES), F32).at[:, :dq].set(ln_local)
    ln_all = _allgather8(ln_pad, "gather_ln")
    ln_full = jnp.concatenate([ln_all[2 * k, :ln_rows, :dq] for k in range(4)], axis=1)
    lng = [ln_full[j][None, :] for j in range(n_sub)]
    lnb = [ln_full[n_sub + j][None, :] for j in range(n_sub)]

    main = dict(mla_w_in=mla_w_in, mla_w_qb=mla_w_qb, mla_w_kvb=mla_w_kvb, mla_w_o=mla_w_o, hgrn_w_in=hgrn_w_in,
                hgrn_w_o=hgrn_w_o, ffn_w_in=ffn_w_in, ffn_w_out=ffn_w_out)
    names = list(main)
    gathered = dict(zip(names, _gather_chips([main[k].astype(BF16) for k in names], "gather_weights")))

    def col_w(name, l):
        return gathered[name][:, l]

    def row_w(name, l):
        g = gathered[name][:, l]
        return g.reshape(1, g.shape[0] * g.shape[1], g.shape[2])

    def full_w_in(l):
        g = gathered["mla_w_in"][:, l]
        return jnp.transpose(g, (1, 0, 2)).reshape(1, g.shape[1], 4 * g.shape[2])

    ang = positions.astype(F32)[..., None] * (ROPE_THETA ** (-jnp.arange(0, QK_ROPE, 2, dtype=F32) / QK_ROPE))
    cos, sin = jnp.cos(ang), jnp.sin(ang)

    gq = [mla_q_norm[j][None, :] for j in range(n_mla)]
    gkv = [mla_kv_norm[j][None, :] for j in range(n_mla)]
    gn = [hgrn_g_norm[j][None, :] for j in range(n_hgrn)]

    def r2(a):
        return a.reshape(T, a.shape[-1])

    def r3(a):
        return a.reshape(B, S, a.shape[-1])

    saved = []
    xs = x
    for layer in range(depth):
        j = layer // 2
        sub = 2 * layer
        tag = f"l{layer}"
        h = _modulate(xs, scale[sub], shift[sub], f"mod_{tag}a")
        if layer % 2 == 0:
            w_in = full_w_in(j)
            proj = r3(_mm_nn(r2(h), w_in, F32, f"mla_in_{tag}"))
            qn, kvn = _mla_mid_fwd(proj, gq[j], gkv[j], f"mla_mid_{tag}")
            q = r3(_mm_nn(r2(qn), col_w("mla_w_qb", j), F32, f"mla_qb_{tag}"))
            kv = r3(_mm_nn(r2(kvn), col_w("mla_w_kvb", j), F32, f"mla_kvb_{tag}"))
            qh, kh, vh = _mla_prep_fwd(q, kv, proj, cos, sin, f"mla_prep_{tag}")
            o, lse = _attn_fwd(qh, kh, vh, f"attn_{tag}")
            y = r3(_mm_nn(r2(o), row_w("mla_w_o", j), F32, f"mla_o_{tag}"))
            mix = (h, w_in, proj, qn, kvn, qh, kh, vh, o, lse)
        else:
            proj = r3(_mm_nn(r2(h), col_w("hgrn_w_in", j), F32, f"hgrn_in_{tag}"))
            og, o_pre, states = _hgrn_fwd(proj, lower_bounds[j][None, :], gn[j], f"hgrn_{tag}")
            y = r3(_mm_nn(r2(og), row_w("hgrn_w_o", j), F32, f"hgrn_o_{tag}"))
            mix = (h, proj, og, o_pre, states)
        x1 = _ln_fwd(alpha, xs, y, gate[sub], lng[sub], lnb[sub], f"ln_{tag}a")
        h2 = _modulate(x1, scale[sub + 1], shift[sub + 1], f"mod_{tag}b")
        u = r3(_mm_nn(r2(h2), col_w("ffn_w_in", layer), F32, f"ffn_in_{tag}"))
        a = _swiglu_fwd(u, f"swiglu_{tag}")
        y2 = r3(_mm_nn(r2(a), row_w("ffn_w_out", layer), F32, f"ffn_out_{tag}"))
        x2 = _ln_fwd(alpha, x1, y2, gate[sub + 1], lng[sub + 1], lnb[sub + 1], f"ln_{tag}b")
        saved.append((xs, y, x1, y2, mix, h2, u, a))
        xs = x2

    loss_local, dout = _loss_head(xs, loss_target, "loss_head")
    loss = lax.psum(loss_local, ("x", "y", "c"))

    gw = {k: [None] * main[k].shape[0] for k in names}
    d_shift, d_scale, d_gate = [None] * n_sub, [None] * n_sub, [None] * n_sub
    d_lng, d_lnb = [None] * n_sub, [None] * n_sub
    d_gq, d_gkv, d_gn, d_lbnd = [None] * n_mla, [None] * n_mla, [None] * n_hgrn, [None] * n_hgrn

    def rows4(g):
        return g.reshape(4, g.shape[1] // 4, g.shape[2])

    for layer in reversed(range(depth)):
        j = layer // 2
        sub = 2 * layer
        tag = f"l{layer}"
        xs, y, x1, y2, mix, h2, u, a = saved[layer]
        dxr, dy2, d_gate[sub + 1], d_lng[sub + 1], d_lnb[sub + 1] = _ln_bwd(
            alpha, dout, x1, y2, gate[sub + 1], lng[sub + 1], lnb[sub + 1], f"ln_bwd_{tag}b")
        da = r3(_mm_nt(r2(dy2), row_w("ffn_w_out", layer), F32, f"ffn_out_dx_{tag}"))
        gw["ffn_w_out"][layer] = rows4(_mm_tn(r2(a), r2(dy2), 1, BF16, f"ffn_out_dw_{tag}"))
        du = _swiglu_bwd(u, da, f"swiglu_bwd_{tag}")
        dh2 = r3(_mm_nt(r2(du), col_w("ffn_w_in", layer), F32, f"ffn_in_dx_{tag}"))
        gw["ffn_w_in"][layer] = _mm_tn(r2(h2), r2(du), 4, BF16, f"ffn_in_dw_{tag}")
        dout, d_scale[sub + 1], d_shift[sub + 1] = _mod_bwd(dh2, dxr, x1, scale[sub + 1], f"mod_bwd_{tag}b")
        dxr, dy, d_gate[sub], d_lng[sub], d_lnb[sub] = _ln_bwd(
            alpha, dout, xs, y, gate[sub], lng[sub], lnb[sub], f"ln_bwd_{tag}a")
        if layer % 2 == 0:
            h, w_in, proj, qn, kvn, qh, kh, vh, o, lse = mix
            do = r3(_mm_nt(r2(dy), row_w("mla_w_o", j), BF16, f"mla_o_dx_{tag}"))
            gw["mla_w_o"][j] = rows4(_mm_tn(r2(o), r2(dy), 1, BF16, f"mla_o_dw_{tag}"))
            dqh, dkh, dvh = _attn_bwd(qh, kh, vh, o, do, lse, f"attn_bwd_{tag}")
            dq_, dkv_, dkr = _mla_prep_bwd(dqh, dkh, dvh, cos, sin, f"mla_prep_bwd_{tag}")
            dqn = r3(_mm_nt(r2(dq_), col_w("mla_w_qb", j), F32, f"mla_qb_dx_{tag}"))
            gw["mla_w_qb"][j] = _mm_tn(r2(qn), r2(dq_), 4, BF16, f"mla_qb_dw_{tag}")
            dkvn = r3(_mm_nt(r2(dkv_), col_w("mla_w_kvb", j), F32, f"mla_kvb_dx_{tag}"))
            gw["mla_w_kvb"][j] = _mm_tn(r2(kvn), r2(dkv_), 4, BF16, f"mla_kvb_dw_{tag}")
            dproj, dgq_, dgkv_ = _mla_mid_bwd(proj, dqn, dkvn, dkr, gq[j], gkv[j], f"mla_mid_bwd_{tag}")
            d_gq[j], d_gkv[j] = dgq_.sum(0), dgkv_.sum(0)
            dh = r3(_mm_nt(r2(dproj), w_in, F32, f"mla_in_dx_{tag}"))
            gwin = _mm_tn(r2(h), r2(dproj), 1, BF16, f"mla_in_dw_{tag}")[0]
            gw["mla_w_in"][j] = jnp.transpose(gwin.reshape(gwin.shape[0], 4, gwin.shape[1] // 4), (1, 0, 2))
        else:
            h, proj, og, o_pre, states = mix
            dog = r3(_mm_nt(r2(dy), row_w("hgrn_w_o", j), F32, f"hgrn_o_dx_{tag}"))
            gw["hgrn_w_o"][j] = rows4(_mm_tn(r2(og), r2(dy), 1, BF16, f"hgrn_o_dw_{tag}"))
            dq_, df_, di_, dg_, dlb_, dgn_ = _hgrn_bwd(proj, lower_bounds[j][None, :], gn[j], o_pre, states, dog, f"hgrn_bwd_{tag}")
            dproj = jnp.concatenate([dq_, df_, di_, dg_], axis=-1)
            d_lbnd[j] = dlb_.sum(0).reshape(1, HK)
            d_gn[j] = dgn_.sum((0, 1))
            dh = r3(_mm_nt(r2(dproj), col_w("hgrn_w_in", j), F32, f"hgrn_in_dx_{tag}"))
            gw["hgrn_w_in"][j] = _mm_tn(r2(h), r2(dproj), 4, BF16, f"hgrn_in_dw_{tag}")
        dout, d_scale[sub], d_shift[sub] = _mod_bwd(dh, dxr, xs, scale[sub], f"mod_bwd_{tag}a")
    grad_x = dout

    slabs = [jnp.stack(gw[k], axis=1) for k in names]
    received = _scatter_chips(slabs, "scatter_grads")
    sums = [_sum4(s, r, k_me, f"sum4_{k}") for k, s, r in zip(names, slabs, received)]
    others = _swap_sibling(sums, "swap_sums")
    g_main = {k: (a_, b_) for k, a_, b_ in zip(names, sums, others)}

    dmod = jnp.stack([jnp.concatenate([d_shift[s_][:, 0], d_scale[s_][:, 0], d_gate[s_][:, 0]], axis=-1) for s_ in range(n_sub)])
    dmod_rows = _pad_rows(dmod.reshape(n_sub * B, 3 * D), -(-n_sub * B // 8) * 8)
    dmod_all = _allgather8(dmod_rows, "gather_dmod")[:, : n_sub * B].reshape(8, n_sub, B, 3 * D)
    dmod_all = jnp.transpose(dmod_all, (1, 0, 2, 3)).reshape(n_sub, Bg, 3 * D)
    dmod_mine = lax.dynamic_slice_in_dim(dmod_all, k_me * n_ada, n_ada, axis=2)
    g_ada_w, g_ada_b = _ada_bwd(c_all, dmod_mine, "ada_bwd")
    g_ada_w = g_ada_w.reshape(ada_w.shape)
    g_ada_b = g_ada_b.reshape(ada_b.shape)

    small = [jnp.stack(d_gq).reshape(-1), jnp.stack(d_gkv).reshape(-1), jnp.stack(d_gn).reshape(-1),
             jnp.stack(d_lbnd).reshape(-1), jnp.stack([d.sum(0) for d in d_lng]).reshape(-1),
             jnp.stack([d.sum(0) for d in d_lnb]).reshape(-1)]
    sizes = [s_.shape[0] for s_ in small]
    flat = jnp.concatenate(small)
    rows_small = -(-flat.shape[0] // (8 * LANES)) * 8
    flat = jnp.pad(flat, (0, rows_small * LANES - flat.shape[0])).reshape(rows_small, LANES)
    tot = _allgather8(flat, "gather_small")
    acc = tot[0]
    for d in range(1, 8):
        acc = acc + tot[d]
    acc = acc.reshape(-1)
    offs = [0]
    for s_ in sizes:
        offs.append(offs[-1] + s_)
    g_q_norm = acc[offs[0] : offs[1]].reshape(mla_q_norm.shape)
    g_kv_norm = acc[offs[1] : offs[2]].reshape(mla_kv_norm.shape)
    g_g_norm = acc[offs[2] : offs[3]].reshape(hgrn_g_norm.shape)
    g_lbnd = acc[offs[3] : offs[4]].reshape(n_hgrn, HK)
    g_lb_full = lower_bounds_vjp(g_lbnd)[0]
    g_hgrn_lb = lax.dynamic_slice_in_dim(g_lb_full, k_me * lbw, lbw, axis=1)
    g_lng = lax.dynamic_slice_in_dim(acc[offs[4] : offs[5]].reshape(n_sub, D), k_me * dq, dq, axis=1).reshape(ln_g.shape)
    g_lnb = lax.dynamic_slice_in_dim(acc[offs[5] : offs[6]].reshape(n_sub, D), k_me * dq, dq, axis=1).reshape(ln_b.shape)

    weights = dict(mla_w_in=mla_w_in, mla_q_norm=mla_q_norm, mla_w_qb=mla_w_qb, mla_kv_norm=mla_kv_norm, mla_w_kvb=mla_w_kvb,
                   mla_w_o=mla_w_o, hgrn_lb=hgrn_lb, hgrn_w_in=hgrn_w_in, hgrn_g_norm=hgrn_g_norm, hgrn_w_o=hgrn_w_o,
                   ffn_w_in=ffn_w_in, ffn_w_out=ffn_w_out, ada_w=ada_w, ada_b=ada_b, ln_g=ln_g, ln_b=ln_b)
    moms = dict(mla_w_in=(m_mla_w_in, v_mla_w_in), mla_q_norm=(m_mla_q_norm, v_mla_q_norm), mla_w_qb=(m_mla_w_qb, v_mla_w_qb),
                mla_kv_norm=(m_mla_kv_norm, v_mla_kv_norm), mla_w_kvb=(m_mla_w_kvb, v_mla_w_kvb), mla_w_o=(m_mla_w_o, v_mla_w_o),
                hgrn_lb=(m_hgrn_lb, v_hgrn_lb), hgrn_w_in=(m_hgrn_w_in, v_hgrn_w_in), hgrn_g_norm=(m_hgrn_g_norm, v_hgrn_g_norm),
                hgrn_w_o=(m_hgrn_w_o, v_hgrn_w_o), ffn_w_in=(m_ffn_w_in, v_ffn_w_in), ffn_w_out=(m_ffn_w_out, v_ffn_w_out),
                ada_w=(m_ada_w, v_ada_w), ada_b=(m_ada_b, v_ada_b), ln_g=(m_ln_g, v_ln_g), ln_b=(m_ln_b, v_ln_b))
    grads = dict(mla_q_norm=(g_q_norm,), mla_kv_norm=(g_kv_norm,), hgrn_lb=(g_hgrn_lb,), hgrn_g_norm=(g_g_norm,),
                 ada_w=(g_ada_w,), ada_b=(g_ada_b,), ln_g=(g_lng,), ln_b=(g_lnb,), **g_main)
    res = {k: _adamw(weights[k], [g_.reshape(weights[k].shape) for g_ in grads[k]], moms[k][0], moms[k][1], f"adamw_{k}")
           for k in weights}
    order = list(weights)
    return (loss, grad_x, *[res[k][0] for k in order], *[res[k][1] for k in order], *[res[k][2] for k in order],
            *[res[k][3] for k in order])
```

```python
import functools

import jax
import jax.numpy as jnp
from jax import lax
from jax.experimental import pallas as pl
from jax.experimental.pallas import tpu as pltpu

F32 = jnp.float32
BF16 = jnp.bfloat16
SDS = jax.ShapeDtypeStruct
MESH = pl.DeviceIdType.MESH
HI = lax.Precision.HIGHEST
MID = lax.Precision.HIGH

MLA_HEADS, QK_NOPE, QK_ROPE, V_HEAD = 16, 64, 32, 64
Q_LORA, KV_LORA = 768, 256
QK_DIM = QK_NOPE + QK_ROPE
ROPE_THETA = 10000.0
HGRN_K = 128
HGRN_CHUNK = 64
HGRN_SUB = 16
HGRN_PAR = 2
LN_EPS, RMS_EPS = 1e-5, 1e-6
ADAM_LR, ADAM_B1, ADAM_B2, ADAM_EPS, ADAM_WD, ADAM_STEP = 0.001, 0.9, 0.999, 1e-08, 0.01, 10
NEG = -1e30

VMEM_LIMIT_BYTES = 56 * 1024 * 1024
LANES = 128


def _cparams(*sem):
    return pltpu.CompilerParams(dimension_semantics=sem if sem else None, vmem_limit_bytes=VMEM_LIMIT_BYTES)


def _pick_tile(n, cap):
    best = 0
    for t in range(LANES, min(n, cap) + 1, LANES):
        if n % t == 0:
            best = t
    return best if best else n


def _bdot(a, b):
    return jnp.dot(a.astype(BF16), b.astype(BF16), preferred_element_type=F32)


def _bdot_nt(a, b):
    return lax.dot_general(a.astype(BF16), b.astype(BF16), (((1,), (1,)), ((), ())), preferred_element_type=F32)


def _bdot_tn(a, b):
    return lax.dot_general(a.astype(BF16), b.astype(BF16), (((0,), (0,)), ((), ())), preferred_element_type=F32)


def _hdot(a, b):
    return jnp.dot(a, b, precision=HI, preferred_element_type=F32)


def _mdot(a, b):
    return jnp.dot(a, b, precision=MID, preferred_element_type=F32)


def _mdot_nt(a, b):
    return lax.dot_general(a, b, (((1,), (1,)), ((), ())), precision=MID, preferred_element_type=F32)


def _mdot_tn(a, b):
    return lax.dot_general(a, b, (((0,), (0,)), ((), ())), precision=MID, preferred_element_type=F32)


def _mm_nn(a, w, out_dtype, name):
    M, K = a.shape
    G, _, n = w.shape
    tm = min(512, M)
    tn = _pick_tile(n, 1536)
    nps = n // tn

    def body(a_ref, w_ref, o_ref):
        o_ref[...] = _bdot(a_ref[...], w_ref[...]).astype(o_ref.dtype)

    return pl.pallas_call(
        body,
        grid=(G * nps, M // tm),
        in_specs=[
            pl.BlockSpec((tm, K), lambda j, i: (i, 0)),
            pl.BlockSpec((None, K, tn), lambda j, i: (j // nps, 0, j % nps)),
        ],
        out_specs=pl.BlockSpec((tm, tn), lambda j, i: (i, j)),
        out_shape=SDS((M, G * n), out_dtype),
        name=name,
        compiler_params=_cparams("parallel", "parallel"),
    )(a, w)


def _mm_nt(a, w, out_dtype, name):
    M = a.shape[0]
    G, K, n = w.shape
    tm = min(512, M)
    tk = _pick_tile(K, 1536)

    def body(a_ref, w_ref, o_ref, acc_ref):
        s = pl.program_id(2)

        @pl.when(s == 0)
        def _():
            acc_ref[...] = jnp.zeros_like(acc_ref)

        acc_ref[...] += _bdot_nt(a_ref[...], w_ref[...])

        @pl.when(s == G - 1)
        def _():
            o_ref[...] = acc_ref[...].astype(o_ref.dtype)

    return pl.pallas_call(
        body,
        grid=(K // tk, M // tm, G),
        in_specs=[
            pl.BlockSpec((tm, n), lambda kb, i, s: (i, s)),
            pl.BlockSpec((None, tk, n), lambda kb, i, s: (s, kb, 0)),
        ],
        out_specs=pl.BlockSpec((tm, tk), lambda kb, i, s: (i, kb)),
        out_shape=SDS((M, K), out_dtype),
        scratch_shapes=[pltpu.VMEM((tm, tk), F32)],
        name=name,
        compiler_params=_cparams("parallel", "parallel", "arbitrary"),
    )(a, w)


def _mm_tn(a, d, G, out_dtype, name):
    T, K = a.shape
    n = d.shape[1] // G
    tk = _pick_tile(K, 256)
    tn = _pick_tile(n, 1536)
    nps = n // tn

    def body(a_ref, d_ref, o_ref):
        o_ref[...] = _bdot_tn(a_ref[...], d_ref[...]).astype(o_ref.dtype)

    return pl.pallas_call(
        body,
        grid=(G * nps, K // tk),
        in_specs=[
            pl.BlockSpec((T, tk), lambda j, i: (0, i)),
            pl.BlockSpec((T, tn), lambda j, i: (0, j)),
        ],
        out_specs=pl.BlockSpec((None, tk, tn), lambda j, i: (j // nps, i, j % nps)),
        out_shape=SDS((G, K, n), out_dtype),
        name=name,
        compiler_params=_cparams("parallel", "parallel"),
    )(a, d)


def _rows_call(body, name, B, S, ins, outs, ts=256):
    ts = min(ts, S)
    in_specs, args = [], []
    for arr, kind in ins:
        W = arr.shape[-1]
        if kind == "row":
            in_specs.append(pl.BlockSpec((None, ts, W), lambda b, s: (b, s, 0)))
        elif kind == "ex":
            in_specs.append(pl.BlockSpec((None, 1, W), lambda b, s: (b, 0, 0)))
        else:
            in_specs.append(pl.BlockSpec((1, W), lambda b, s: (0, 0)))
        args.append(arr)
    out_specs, out_shape = [], []
    for W, dt, kind in outs:
        if kind == "row":
            out_specs.append(pl.BlockSpec((None, ts, W), lambda b, s: (b, s, 0)))
            out_shape.append(SDS((B, S, W), dt))
        else:
            out_specs.append(pl.BlockSpec((None, 1, W), lambda b, s: (b, 0, 0)))
            out_shape.append(SDS((B, 1, W), dt))
    return pl.pallas_call(
        body,
        grid=(B, S // ts),
        in_specs=in_specs,
        out_specs=out_specs,
        out_shape=out_shape,
        name=name,
        compiler_params=_cparams("parallel", "arbitrary"),
    )(*args)


def _acc(ref, val):
    @pl.when(pl.program_id(1) == 0)
    def _():
        ref[...] = jnp.zeros_like(ref)

    ref[...] += val


def _mod_fn(x, sc, sh):
    return x * (1.0 + sc) + sh


def _ln_fn(alpha, x, y, gate, g, b):
    z = alpha * x + (1.0 + gate) * y
    mu = jnp.mean(z, -1, keepdims=True)
    var = jnp.mean(jnp.square(z - mu), -1, keepdims=True)
    return (z - mu) * lax.rsqrt(var + LN_EPS) * g + b


def _modulate(x, sc, sh, name):
    B, S, D = x.shape

    def body(x_ref, sc_ref, sh_ref, h_ref):
        h_ref[...] = _mod_fn(x_ref[...], sc_ref[...], sh_ref[...]).astype(BF16)

    return _rows_call(body, name, B, S, [(x, "row"), (sc, "ex"), (sh, "ex")], [(D, BF16, "row")])[0]


def _ln_fwd(alpha, x, y, gate, g, b, name):
    B, S, D = x.shape

    def body(x_ref, y_ref, gate_ref, g_ref, b_ref, o_ref):
        o_ref[...] = _ln_fn(alpha, x_ref[...], y_ref[...], gate_ref[...], g_ref[...], b_ref[...])

    return _rows_call(
        body, name, B, S, [(x, "row"), (y, "row"), (gate, "ex"), (g, "par"), (b, "par")], [(D, F32, "row")]
    )[0]


def _ln_bwd(alpha, dout, x, y, gate, g, b, name):
    B, S, D = x.shape

    def body(do_ref, x_ref, y_ref, gate_ref, g_ref, b_ref, dxr_ref, dy_ref, dgate_ref, dg_ref, db_ref):
        _, vjp = jax.vjp(
            functools.partial(_ln_fn, alpha), x_ref[...], y_ref[...], gate_ref[...], g_ref[...], b_ref[...]
        )
        dx, dy, dgate, dg, db = vjp(do_ref[...])
        dxr_ref[...] = dx
        dy_ref[...] = dy.astype(BF16)
        _acc(dgate_ref, dgate)
        _acc(dg_ref, dg)
        _acc(db_ref, db)

    return _rows_call(
        body,
        name,
        B,
        S,
        [(dout, "row"), (x, "row"), (y, "row"), (gate, "ex"), (g, "par"), (b, "par")],
        [(D, F32, "row"), (D, BF16, "row"), (D, F32, "acc"), (D, F32, "acc"), (D, F32, "acc")],
    )


def _mod_bwd(dh, dxr, x, sc, name):
    B, S, D = x.shape

    def body(dh_ref, dxr_ref, x_ref, sc_ref, dx_ref, dsc_ref, dsh_ref):
        dh_v = dh_ref[...]
        dx_ref[...] = dxr_ref[...] + dh_v * (1.0 + sc_ref[...])
        _acc(dsc_ref, jnp.sum(dh_v * x_ref[...], axis=0, keepdims=True))
        _acc(dsh_ref, jnp.sum(dh_v, axis=0, keepdims=True))

    return _rows_call(
        body,
        name,
        B,
        S,
        [(dh, "row"), (dxr, "row"), (x, "row"), (sc, "ex")],
        [(D, F32, "row"), (D, F32, "acc"), (D, F32, "acc")],
    )


def _loss_head(y, target, name):
    B, S, D = y.shape

    def body(y_ref, t_ref, l_ref, dy_ref):
        e = y_ref[...] - t_ref[...]
        dy_ref[...] = e * (1.0 / D)
        part = 0.5 * jnp.sum(jnp.sum(e * e, axis=1, keepdims=True) * (1.0 / D), axis=0, keepdims=True)
        _acc(l_ref, jnp.broadcast_to(part, (1, LANES)))

    loss, dy = _rows_call(
        body, name, B, S, [(y, "row"), (target, "row")], [(LANES, F32, "acc"), (D, F32, "row")]
    )
    return jnp.sum(loss[:, 0, 0]), dy


def _swiglu_fn(u):
    F = u.shape[-1] // 2
    return jax.nn.silu(u[:, :F]) * u[:, F:]


def _swiglu_fwd(u, name):
    B, S, F2 = u.shape

    def body(u_ref, a_ref):
        a_ref[...] = _swiglu_fn(u_ref[...]).astype(BF16)

    return _rows_call(body, name, B, S, [(u, "row")], [(F2 // 2, BF16, "row")])[0]


def _swiglu_bwd(u, da, name):
    B, S, F2 = u.shape

    def body(u_ref, da_ref, du_ref):
        _, vjp = jax.vjp(_swiglu_fn, u_ref[...])
        du_ref[...] = vjp(da_ref[...])[0].astype(BF16)

    return _rows_call(body, name, B, S, [(u, "row"), (da, "row")], [(F2, BF16, "row")])[0]


def _rms_fn(x, g):
    return x * lax.rsqrt(jnp.mean(jnp.square(x), -1, keepdims=True) + RMS_EPS) * g


def _mla_mid_fwd(proj, gq, gkv, name):
    B, S, _ = proj.shape

    def body(p_ref, gq_ref, gkv_ref, qn_ref, kvn_ref):
        p = p_ref[...]
        qn_ref[...] = _rms_fn(p[:, :Q_LORA], gq_ref[...]).astype(BF16)
        kvn_ref[...] = _rms_fn(p[:, Q_LORA : Q_LORA + KV_LORA], gkv_ref[...]).astype(BF16)

    return _rows_call(
        body, name, B, S, [(proj, "row"), (gq, "par"), (gkv, "par")], [(Q_LORA, BF16, "row"), (KV_LORA, BF16, "row")]
    )


def _mla_mid_bwd(proj, dqn, dkvn, dkr, gq, gkv, name):
    B, S, W = proj.shape

    def body(p_ref, dqn_ref, dkvn_ref, dkr_ref, gq_ref, gkv_ref, dp_ref, dgq_ref, dgkv_ref):
        p = p_ref[...]
        _, vq = jax.vjp(_rms_fn, p[:, :Q_LORA], gq_ref[...])
        dql, dgq = vq(dqn_ref[...])
        _, vkv = jax.vjp(_rms_fn, p[:, Q_LORA : Q_LORA + KV_LORA], gkv_ref[...])
        dkvl, dgkv = vkv(dkvn_ref[...])
        dp_ref[:, :Q_LORA] = dql.astype(BF16)
        dp_ref[:, Q_LORA : Q_LORA + KV_LORA] = dkvl.astype(BF16)
        dp_ref[:, Q_LORA + KV_LORA :] = dkr_ref[...].astype(BF16)
        _acc(dgq_ref, dgq)
        _acc(dgkv_ref, dgkv)

    return _rows_call(
        body,
        name,
        B,
        S,
        [(proj, "row"), (dqn, "row"), (dkvn, "row"), (dkr, "row"), (gq, "par"), (gkv, "par")],
        [(W, BF16, "row"), (Q_LORA, F32, "acc"), (KV_LORA, F32, "acc")],
    )


def _rope(x, cos, sin):
    h = QK_ROPE // 2
    x1, x2 = x[:, :h], x[:, h:]
    return jnp.concatenate([x1 * cos - x2 * sin, x1 * sin + x2 * cos], axis=1)


def _rope_t(dy, cos, sin):
    h = QK_ROPE // 2
    d1, d2 = dy[:, :h], dy[:, h:]
    return jnp.concatenate([d1 * cos + d2 * sin, d2 * cos - d1 * sin], axis=1)


def _heads_call(body, name, B, S, ins, outs, ts=256):
    ts = min(ts, S)
    in_specs, args = [], []
    for arr, kind in ins:
        if kind == "row":
            in_specs.append(pl.BlockSpec((None, ts, arr.shape[-1]), lambda b, s: (b, s, 0)))
        else:
            in_specs.append(pl.BlockSpec((arr.shape[0], None, ts, arr.shape[-1]), lambda b, s: (0, b, s, 0)))
        args.append(arr)
    out_specs, out_shape = [], []
    for shape, dt, kind in outs:
        if kind == "row":
            out_specs.append(pl.BlockSpec((None, ts, shape[-1]), lambda b, s: (b, s, 0)))
        else:
            out_specs.append(pl.BlockSpec((shape[0], None, ts, shape[-1]), lambda b, s: (0, b, s, 0)))
        out_shape.append(SDS(shape, dt))
    return pl.pallas_call(
        body,
        grid=(B, S // ts),
        in_specs=in_specs,
        out_specs=out_specs,
        out_shape=out_shape,
        name=name,
        compiler_params=_cparams("parallel", "parallel"),
    )(*args)


def _mla_prep_fwd(q, kv, proj, cos, sin, name):
    B, S, _ = q.shape
    H = MLA_HEADS

    def body(q_ref, kv_ref, p_ref, cos_ref, sin_ref, qh_ref, kh_ref, vh_ref):
        cos_v, sin_v = cos_ref[...], sin_ref[...]
        kr = _rope(p_ref[:, Q_LORA + KV_LORA :], cos_v, sin_v).astype(BF16)
        for h in range(H):
            qn = q_ref[:, h * QK_DIM : h * QK_DIM + QK_NOPE]
            qr = _rope(q_ref[:, h * QK_DIM + QK_NOPE : (h + 1) * QK_DIM], cos_v, sin_v)
            qh_ref[h] = jnp.concatenate([qn, qr], axis=1).astype(BF16)
            kn = kv_ref[:, h * 128 : h * 128 + QK_NOPE].astype(BF16)
            kh_ref[h] = jnp.concatenate([kn, kr], axis=1)
            vh_ref[h] = kv_ref[:, h * 128 + QK_NOPE : (h + 1) * 128].astype(BF16)

    return _heads_call(
        body,
        name,
        B,
        S,
        [(q, "row"), (kv, "row"), (proj, "row"), (cos, "row"), (sin, "row")],
        [((H, B, S, QK_DIM), BF16, "heads"), ((H, B, S, QK_DIM), BF16, "heads"), ((H, B, S, V_HEAD), BF16, "heads")],
    )


def _mla_prep_bwd(dqh, dkh, dvh, cos, sin, name):
    H, B, S, _ = dqh.shape

    def body(dqh_ref, dkh_ref, dvh_ref, cos_ref, sin_ref, dq_ref, dkv_ref, dkr_ref):
        cos_v, sin_v = cos_ref[...], sin_ref[...]
        dkr = jnp.zeros((cos_v.shape[0], QK_ROPE), F32)
        for h in range(H):
            dqv = dqh_ref[h].astype(F32)
            dq_ref[:, h * QK_DIM : h * QK_DIM + QK_NOPE] = dqv[:, :QK_NOPE].astype(BF16)
            dq_ref[:, h * QK_DIM + QK_NOPE : (h + 1) * QK_DIM] = _rope_t(dqv[:, QK_NOPE:], cos_v, sin_v).astype(BF16)
            dkv = dkh_ref[h].astype(F32)
            dkv_ref[:, h * 128 : h * 128 + QK_NOPE] = dkv[:, :QK_NOPE].astype(BF16)
            dkv_ref[:, h * 128 + QK_NOPE : (h + 1) * 128] = dvh_ref[h]
            dkr = dkr + dkv[:, QK_NOPE:]
        dkr_ref[...] = _rope_t(dkr, cos_v, sin_v)

    return _heads_call(
        body,
        name,
        B,
        S,
        [(dqh, "heads"), (dkh, "heads"), (dvh, "heads"), (cos, "row"), (sin, "row")],
        [((B, S, H * QK_DIM), BF16, "row"), ((B, S, H * 128), BF16, "row"), ((B, S, QK_ROPE), F32, "row")],
    )


LOG2E = 1.4426950408889634


def _tril_mask(t):
    return lax.broadcasted_iota(jnp.int32, (t, t), 0) >= lax.broadcasted_iota(jnp.int32, (t, t), 1)


def _attn_fwd(qh, kh, vh, name):
    H, B, S, _ = qh.shape
    t = min(256, S)
    scale = QK_DIM**-0.5
    c2 = scale * LOG2E

    def body(q_ref, k_ref, v_ref, o_ref, lse_ref):
        i = pl.program_id(2)
        qs = [q_ref[0], q_ref[1]]

        def step(j, carry, diagonal):
            rows = pl.ds(pl.multiple_of(j * t, t), t)
            out = []
            for hh in range(2):
                m, l, acc = carry[hh]
                s = _bdot_nt(qs[hh], k_ref[hh, rows, :])
                if diagonal:
                    s = jnp.where(_tril_mask(t), s, NEG)
                m_new = jnp.maximum(m, jnp.max(s, axis=1, keepdims=True))
                p = jnp.exp2((s - m_new) * c2)
                a = jnp.exp2((m - m_new) * c2)
                l = a * l + jnp.sum(p, axis=1, keepdims=True)
                acc = a * acc + _bdot(p, v_ref[hh, rows, :])
                out.append((m_new, l, acc))
            return tuple(out)

        one = (jnp.full((t, 1), NEG, F32), jnp.zeros((t, 1), F32), jnp.zeros((t, V_HEAD), F32))
        carry = lax.fori_loop(0, i, lambda j, cy: step(j, cy, False), (one, one))
        carry = step(i, carry, True)
        outs = []
        for hh in range(2):
            m, l, acc = carry[hh]
            outs.append(acc / l)
            lse_ref[hh] = m * scale + jnp.log(l)
        o_ref[...] = jnp.concatenate(outs, axis=1).astype(BF16)

    return pl.pallas_call(
        body,
        grid=(B, H // 2, S // t),
        in_specs=[
            pl.BlockSpec((2, None, t, QK_DIM), lambda b, p, i: (p, b, i, 0)),
            pl.BlockSpec((2, None, S, QK_DIM), lambda b, p, i: (p, b, 0, 0)),
            pl.BlockSpec((2, None, S, V_HEAD), lambda b, p, i: (p, b, 0, 0)),
        ],
        out_specs=[
            pl.BlockSpec((None, t, 2 * V_HEAD), lambda b, p, i: (b, i, p)),
            pl.BlockSpec((2, None, t, 1), lambda b, p, i: (p, b, i, 0)),
        ],
        out_shape=[SDS((B, S, H * V_HEAD), BF16), SDS((H, B, S, 1), F32)],
        name=name,
        compiler_params=_cparams("parallel", "parallel", "arbitrary"),
    )(qh, kh, vh)


def _attn_bwd(qh, kh, vh, o, do, lse, name):
    H, B, S, _ = qh.shape
    t = min(256, S)
    nq = S // t
    scale = QK_DIM**-0.5
    c2 = scale * LOG2E

    def body(q_ref, k_ref, v_ref, o_ref, do_ref, lse_ref, dq_ref, dk_ref, dv_ref, dq_acc, delta_ref, lse2_ref):
        prod = o_ref[...].astype(F32) * do_ref[...].astype(F32)
        for hh in range(2):
            delta_ref[hh] = jnp.sum(prod[:, hh * V_HEAD : (hh + 1) * V_HEAD], axis=1, keepdims=True)
            lse2_ref[hh] = lse_ref[hh] * LOG2E
        dq_acc[...] = jnp.zeros_like(dq_acc)

        def kloop(j, _):
            krows = pl.ds(pl.multiple_of(j * t, t), t)
            ks = [k_ref[0, krows, :], k_ref[1, krows, :]]
            vs = [v_ref[0, krows, :], v_ref[1, krows, :]]

            def qstep(i, carry, diagonal):
                qrows = pl.ds(pl.multiple_of(i * t, t), t)
                do_i = do_ref[qrows, :]
                out = []
                for hh in range(2):
                    dk, dv = carry[hh]
                    q = q_ref[hh, qrows, :]
                    do_h = do_i[:, hh * V_HEAD : (hh + 1) * V_HEAD]
                    s = _bdot_nt(q, ks[hh])
                    p = jnp.exp2(s * c2 - lse2_ref[hh, qrows, :])
                    if diagonal:
                        p = jnp.where(_tril_mask(t), p, 0.0)
                    dv = dv + _bdot_tn(p, do_h)
                    dp = _bdot_nt(do_h, vs[hh])
                    ds = (p * (dp - delta_ref[hh, qrows, :])).astype(BF16)
                    dk = dk + _bdot_tn(ds, q)
                    dq_acc[hh, qrows, :] += _bdot(ds, ks[hh])
                    out.append((dk, dv))
                return tuple(out)

            one = (jnp.zeros((t, QK_DIM), F32), jnp.zeros((t, V_HEAD), F32))
            carry = qstep(j, (one, one), True)
            carry = lax.fori_loop(j + 1, nq, lambda i, cy: qstep(i, cy, False), carry)
            for hh in range(2):
                dk_ref[hh, krows, :] = (carry[hh][0] * scale).astype(BF16)
                dv_ref[hh, krows, :] = carry[hh][1].astype(BF16)
            return 0

        lax.fori_loop(0, nq, kloop, 0)
        dq_ref[...] = (dq_acc[...] * scale).astype(BF16)

    hspec = lambda w: pl.BlockSpec((2, None, S, w), lambda b, p: (p, b, 0, 0))
    ospec = pl.BlockSpec((None, S, 2 * V_HEAD), lambda b, p: (b, 0, p))
    return pl.pallas_call(
        body,
        grid=(B, H // 2),
        in_specs=[hspec(QK_DIM), hspec(QK_DIM), hspec(V_HEAD), ospec, ospec, hspec(1)],
        out_specs=[hspec(QK_DIM), hspec(QK_DIM), hspec(V_HEAD)],
        out_shape=[SDS((H, B, S, QK_DIM), BF16), SDS((H, B, S, QK_DIM), BF16), SDS((H, B, S, V_HEAD), BF16)],
        scratch_shapes=[pltpu.VMEM((2, S, QK_DIM), F32), pltpu.VMEM((2, S, 1), F32), pltpu.VMEM((2, S, 1), F32)],
        name=name,
        compiler_params=_cparams("parallel", "parallel"),
    )(qh, kh, vh, o, do, lse)


def _hgrn_pre(q, fx, lb):
    sig = jax.nn.sigmoid(fx)
    f = lb + (1.0 - lb) * sig
    return jax.nn.silu(q), 1.0 - f, jnp.log(f)


def _hgrn_gate(o, gg, gn):
    return _rms_fn(o, gn) * jax.nn.silu(gg)


def _tri(n, lower):
    r = lax.broadcasted_iota(jnp.int32, (n, n), 0)
    c = lax.broadcasted_iota(jnp.int32, (n, n), 1)
    return ((r >= c) if lower else (r <= c)).astype(F32)


def _hgrn_intra_fwd(qs, k, v, b):
    C, SB = qs.shape[0], min(HGRN_SUB, qs.shape[0])
    ridx = lax.broadcasted_iota(jnp.int32, (SB, 1), 0)
    outs = []
    for i in range(C // SB):
        r0 = i * SB
        qi, ki, vi, bi = qs[r0 : r0 + SB], k[r0 : r0 + SB], v[r0 : r0 + SB], b[r0 : r0 + SB]
        acc = jnp.zeros((SB, v.shape[1]), F32)
        for s in range(SB):
            mask = ridx >= s
            e = jnp.exp(jnp.where(mask, bi - bi[s : s + 1], 0.0))
            a = jnp.sum(jnp.where(mask, qi * ki[s : s + 1] * e, 0.0), axis=1, keepdims=True)
            acc = acc + a * vi[s : s + 1]
        if i > 0:
            ref = bi[0:1]
            qt = qi * jnp.exp(bi - ref)
            kt = k[:r0] * jnp.exp(ref - b[:r0])
            acc = acc + _bdot(_mdot_nt(qt, kt), v[:r0])
        outs.append(acc)
    return jnp.concatenate(outs, axis=0)


def _hgrn_intra_bwd(qs, k, v, b, do):
    C, SB = qs.shape[0], min(HGRN_SUB, qs.shape[0])
    nb = C // SB
    ridx = lax.broadcasted_iota(jnp.int32, (SB, 1), 0)
    dq_p = [None] * nb
    dk_p = [jnp.zeros((SB, k.shape[1]), F32) for _ in range(nb)]
    dv_p = [jnp.zeros((SB, v.shape[1]), F32) for _ in range(nb)]
    for i in range(nb):
        r0 = i * SB
        qi, ki, vi, bi, doi = qs[r0 : r0 + SB], k[r0 : r0 + SB], v[r0 : r0 + SB], b[r0 : r0 + SB], do[r0 : r0 + SB]
        dqi = jnp.zeros_like(qi)
        dki = jnp.zeros_like(ki)
        dvi = jnp.zeros_like(vi)
        for s in range(SB):
            mask = ridx >= s
            e = jnp.where(mask, jnp.exp(jnp.where(mask, bi - bi[s : s + 1], 0.0)), 0.0)
            da = jnp.sum(doi * vi[s : s + 1], axis=1, keepdims=True)
            a = jnp.sum(qi * ki[s : s + 1] * e, axis=1, keepdims=True)
            dqi = dqi + da * (ki[s : s + 1] * e)
            dk_row = jnp.sum(da * qi * e, axis=0, keepdims=True)
            dv_row = jnp.sum(a * doi, axis=0, keepdims=True)
            dki = jnp.where(ridx == s, dki + dk_row, dki)
            dvi = jnp.where(ridx == s, dvi + dv_row, dvi)
        if i > 0:
            ref = bi[0:1]
            eq = jnp.exp(bi - ref)
            ek = jnp.exp(ref - b[:r0])
            qt = qi * eq
            kt = k[:r0] * ek
            A = _mdot_nt(qt, kt)
            dA = _bdot_nt(doi, v[:r0])
            dvl = _bdot_tn(A, doi)
            dqi = dqi + _mdot(dA, kt) * eq
            dkl = _mdot_tn(dA, qt) * ek
            for j in range(i):
                dk_p[j] = dk_p[j] + dkl[j * SB : (j + 1) * SB]
                dv_p[j] = dv_p[j] + dvl[j * SB : (j + 1) * SB]
        dq_p[i] = dqi
        dk_p[i] = dk_p[i] + dki
        dv_p[i] = dv_p[i] + dvi
    return jnp.concatenate(dq_p, axis=0), jnp.concatenate(dk_p, axis=0), jnp.concatenate(dv_p, axis=0)


def _hgrn_fwd(proj, lb, gn, name):
    B, S, W = proj.shape
    HK = W // 4
    H = HK // HGRN_K
    C = min(HGRN_CHUNK, S)
    N = S // C

    HP = HGRN_PAR if H % HGRN_PAR == 0 else 1
    WP = HP * HGRN_K

    def body(q_ref, f_ref, i_ref, g_ref, lb_ref, gn_ref, og_ref, o_ref, st_ref):
        gn_v = gn_ref[...]
        tril = _tri(C, True)

        def chunk(n, sts):
            rows = pl.ds(pl.multiple_of(n * C, C), C)
            out = []
            for hh in range(HP):
                ln = slice(hh * HGRN_K, (hh + 1) * HGRN_K)
                st = sts[hh]
                qs, k, g = _hgrn_pre(q_ref[rows, ln], f_ref[rows, ln], lb_ref[:, ln])
                v = i_ref[rows, ln]
                b = _hdot(tril, g)
                st_ref[hh, n] = st
                o = _hgrn_intra_fwd(qs, k, v, b) + _bdot_nt(qs * jnp.exp(b), st)
                bl = b[C - 1 : C]
                out.append(st * jnp.exp(bl) + _bdot_tn(v, k * jnp.exp(bl - b)))
                o_ref[rows, ln] = o
                og_ref[rows, ln] = _hgrn_gate(o, g_ref[rows, ln], gn_v).astype(BF16)
            return tuple(out)

        lax.fori_loop(0, N, chunk, tuple(jnp.zeros((HGRN_K, HGRN_K), F32) for _ in range(HP)))

    col = lambda part: pl.BlockSpec((None, S, WP), lambda b, h: (b, 0, part * (H // HP) + h))
    return pl.pallas_call(
        body,
        grid=(B, H // HP),
        in_specs=[col(0), col(1), col(2), col(3), pl.BlockSpec((1, WP), lambda b, h: (0, h)), pl.BlockSpec((1, HGRN_K), lambda b, h: (0, 0))],
        out_specs=[col(0), col(0), pl.BlockSpec((None, HP, N, HGRN_K, HGRN_K), lambda b, h: (b, h, 0, 0, 0))],
        out_shape=[SDS((B, S, HK), BF16), SDS((B, S, HK), F32), SDS((B, H, N, HGRN_K, HGRN_K), F32)],
        name=name,
        compiler_params=_cparams("parallel", "parallel"),
    )(proj, proj, proj, proj, lb, gn)


def _hgrn_bwd(proj, lb, gn, o_pre, states, dog, name):
    B, S, W = proj.shape
    HK = W // 4
    H = HK // HGRN_K
    C = min(HGRN_CHUNK, S)
    N = S // C

    HP = HGRN_PAR if H % HGRN_PAR == 0 else 1
    WP = HP * HGRN_K

    def body(q_ref, f_ref, i_ref, g_ref, lb_ref, gn_ref, o_ref, st_ref, dog_ref, dq_ref, df_ref, di_ref, dg_ref, dlb_ref, dgn_ref):
        gn_v = gn_ref[...]
        tril = _tri(C, True)
        triu = _tri(C, False)

        def chunk(idx, carry):
            n = N - 1 - idx
            rows = pl.ds(pl.multiple_of(n * C, C), C)
            out = []
            for hh in range(HP):
                ln = slice(hh * HGRN_K, (hh + 1) * HGRN_K)
                dst, dlb, dgn = carry[hh]
                (qs, k, g), pre_vjp = jax.vjp(_hgrn_pre, q_ref[rows, ln], f_ref[rows, ln], lb_ref[:, ln])
                v = i_ref[rows, ln]
                _, gate_vjp = jax.vjp(_hgrn_gate, o_ref[rows, ln], g_ref[rows, ln], gn_v)
                do, dgg, dgn_c = gate_vjp(dog_ref[rows, ln])
                b = _hdot(tril, g)
                st0 = st_ref[hh, n]
                eb = jnp.exp(b)
                bl = b[C - 1 : C]
                ebl = jnp.exp(bl)
                ekb = jnp.exp(bl - b)
                qe = qs * eb
                kt = k * ekb
                dqs, dk, dv = _hgrn_intra_bwd(qs, k, v, b, do)
                dqs = dqs + _bdot(do, st0) * eb
                dk = dk + _bdot(v, dst) * ekb
                dv = dv + _bdot_nt(kt, dst)
                st1 = st0 * ebl + _bdot_tn(v, kt)
                dbl = jnp.sum(st1 * dst, axis=0, keepdims=True)
                dst = dst * ebl + _bdot_tn(do, qe)
                dgl = _hdot(triu, qs * dqs - k * dk) + dbl
                dq_pre, dfx, dlb_c = pre_vjp((dqs, dk, dgl))
                dq_ref[rows, ln] = dq_pre.astype(BF16)
                df_ref[rows, ln] = dfx.astype(BF16)
                di_ref[rows, ln] = dv.astype(BF16)
                dg_ref[rows, ln] = dgg.astype(BF16)
                out.append((dst, dlb + dlb_c, dgn + dgn_c))
            return tuple(out)

        zero = jnp.zeros((1, HGRN_K), F32)
        one = (jnp.zeros((HGRN_K, HGRN_K), F32), zero, zero)
        res = lax.fori_loop(0, N, chunk, tuple(one for _ in range(HP)))
        for hh in range(HP):
            dlb_ref[hh] = res[hh][1]
            dgn_ref[hh] = res[hh][2]

    col = lambda part: pl.BlockSpec((None, S, WP), lambda b, h: (b, 0, part * (H // HP) + h))
    vec = pl.BlockSpec((None, HP, 1, HGRN_K), lambda b, h: (b, h, 0, 0))
    return pl.pallas_call(
        body,
        grid=(B, H // HP),
        in_specs=[
            col(0), col(1), col(2), col(3),
            pl.BlockSpec((1, WP), lambda b, h: (0, h)),
            pl.BlockSpec((1, HGRN_K), lambda b, h: (0, 0)),
            col(0),
            pl.BlockSpec((None, HP, N, HGRN_K, HGRN_K), lambda b, h: (b, h, 0, 0, 0)),
            col(0),
        ],
        out_specs=[col(0), col(0), col(0), col(0), vec, vec],
        out_shape=[SDS((B, S, HK), BF16)] * 4 + [SDS((B, H, 1, HGRN_K), F32)] * 2,
        name=name,
        compiler_params=_cparams("parallel", "parallel"),
    )(proj, proj, proj, proj, lb, gn, o_pre, states, dog)


def _ada_fwd(c_all, w, b, name):
    Bg, D = c_all.shape
    L, _, n = w.shape

    def body(c_ref, w_ref, b_ref, o_ref):
        o_ref[...] = _bdot(jax.nn.silu(c_ref[...]), w_ref[...]) + b_ref[...]

    return pl.pallas_call(
        body,
        grid=(L,),
        in_specs=[
            pl.BlockSpec((Bg, D), lambda l: (0, 0)),
            pl.BlockSpec((None, D, n), lambda l: (l, 0, 0)),
            pl.BlockSpec((None, 1, n), lambda l: (l, 0, 0)),
        ],
        out_specs=pl.BlockSpec((None, Bg, n), lambda l: (l, 0, 0)),
        out_shape=SDS((L, Bg, n), F32),
        name=name,
        compiler_params=_cparams("parallel"),
    )(c_all, w, b)


def _ada_bwd(c_all, dmod, name):
    Bg, D = c_all.shape
    L, _, n = dmod.shape

    def body(c_ref, d_ref, dw_ref, db_ref):
        d = d_ref[...]
        dw_ref[...] = _bdot_tn(jax.nn.silu(c_ref[...]), d)
        db_ref[...] = jnp.sum(d, axis=0, keepdims=True)

    return pl.pallas_call(
        body,
        grid=(L,),
        in_specs=[pl.BlockSpec((Bg, D), lambda l: (0, 0)), pl.BlockSpec((None, Bg, n), lambda l: (l, 0, 0))],
        out_specs=[pl.BlockSpec((None, D, n), lambda l: (l, 0, 0)), pl.BlockSpec((None, 1, n), lambda l: (l, 0, 0))],
        out_shape=[SDS((L, D, n), F32), SDS((L, 1, n), F32)],
        name=name,
        compiler_params=_cparams("parallel"),
    )(c_all, dmod)


def _adamw(w, gs, m, v, name):
    shape = w.shape
    cols = shape[-1]
    rows = w.size // cols
    tr = rows
    for cand in (512, 256, 128, 64, 32, 16, 8):
        if rows % cand == 0 and cand * cols * 4 <= 2 * 1024 * 1024:
            tr = cand
            break
    as2d = lambda a: a.reshape(rows, cols)
    ng = len(gs)
    c1 = 1.0 / (1.0 - ADAM_B1**ADAM_STEP)
    c2 = 1.0 / (1.0 - ADAM_B2**ADAM_STEP)

    def body(*refs):
        w_ref, m_ref, v_ref = refs[0], refs[1], refs[2]
        g_refs = refs[3 : 3 + ng]
        g_out, d_out, m_out, v_out = refs[3 + ng :]
        g = g_refs[0][...].astype(F32)
        for r in g_refs[1:]:
            g = g + r[...].astype(F32)
        m_new = ADAM_B1 * m_ref[...] + (1.0 - ADAM_B1) * g
        v_new = ADAM_B2 * v_ref[...] + (1.0 - ADAM_B2) * jnp.square(g)
        g_out[...] = g
        m_out[...] = m_new
        v_out[...] = v_new
        d_out[...] = -ADAM_LR * ((m_new * c1) / (jnp.sqrt(v_new * c2) + ADAM_EPS) + ADAM_WD * w_ref[...])

    spec = pl.BlockSpec((tr, cols), lambda i: (i, 0))
    outs = pl.pallas_call(
        body,
        grid=(rows // tr,),
        in_specs=[spec] * (3 + ng),
        out_specs=[spec] * 4,
        out_shape=[SDS((rows, cols), F32)] * 4,
        name=name,
        compiler_params=_cparams("parallel"),
    )(as2d(w), as2d(m), as2d(v), *[as2d(g) for g in gs])
    return tuple(o.reshape(shape) for o in outs)


def _sum4(own, recv, name):
    shape = own.shape
    cols = shape[-1]
    rows = own.size // cols
    tr = rows
    for cand in (512, 256, 128, 64, 32, 16):
        if rows % cand == 0 and cand * cols * 4 <= 2 * 1024 * 1024:
            tr = cand
            break

    def body(own_ref, recv_ref, o_ref):
        acc = own_ref[...].astype(F32)
        for r in range(3):
            acc = acc + recv_ref[r].astype(F32)
        o_ref[...] = acc

    out = pl.pallas_call(
        body,
        grid=(rows // tr,),
        in_specs=[pl.BlockSpec((tr, cols), lambda i: (i, 0)), pl.BlockSpec((3, tr, cols), lambda i: (0, i, 0))],
        out_specs=pl.BlockSpec((tr, cols), lambda i: (i, 0)),
        out_shape=SDS((rows, cols), F32),
        name=name,
        compiler_params=_cparams("parallel"),
    )(own.reshape(rows, cols), recv.reshape(3, rows, cols))
    return out.reshape(shape)


def _my_place():
    return lax.axis_index("x"), lax.axis_index("y"), lax.axis_index("c")


def _flip(v, bit):
    return 1 - v if bit else v


def _allgather8(x, name):
    r, n = x.shape

    def body(x_ref, o_ref, send_sems, recv_sems, local_sem):
        mx, my, mc = _my_place()
        me = 4 * mx + 2 * my + mc
        mine = pltpu.make_async_copy(x_ref, o_ref.at[me], local_sem)
        mine.start()
        sends = []
        for rel in range(1, 8):
            peer = (_flip(mx, rel & 4), _flip(my, rel & 2), _flip(mc, rel & 1))
            cp = pltpu.make_async_remote_copy(
                src_ref=x_ref, dst_ref=o_ref.at[me], send_sem=send_sems.at[rel - 1], recv_sem=recv_sems.at[rel - 1],
                device_id=peer, device_id_type=MESH,
            )
            cp.start()
            sends.append(cp)
        for rel in range(1, 8):
            px, py, pc = _flip(mx, rel & 4), _flip(my, rel & 2), _flip(mc, rel & 1)
            pltpu.make_async_remote_copy(
                src_ref=x_ref, dst_ref=o_ref.at[4 * px + 2 * py + pc], send_sem=send_sems.at[rel - 1],
                recv_sem=recv_sems.at[rel - 1], device_id=(px, py, pc), device_id_type=MESH,
            ).wait_recv()
        for cp in sends:
            cp.wait_send()
        mine.wait()

    return pl.pallas_call(
        body,
        out_shape=SDS((8, r, n), x.dtype),
        in_specs=[pl.BlockSpec(memory_space=pl.ANY)],
        out_specs=pl.BlockSpec(memory_space=pl.ANY),
        scratch_shapes=[pltpu.SemaphoreType.DMA((7,)), pltpu.SemaphoreType.DMA((7,)), pltpu.SemaphoreType.DMA],
        name=name,
    )(x)


_HBM = pl.BlockSpec(memory_space=pl.ANY)


_SEM = pl.BlockSpec(memory_space=pltpu.SEMAPHORE)
_HBM_ONLY = pl.BlockSpec(memory_space=pltpu.HBM)
_EFFECT = pltpu.SideEffectType.DATAFLOW_SIDE_EFFECTING


def _in_hbm(a):
    return pltpu.with_memory_space_constraint(a, pltpu.HBM)


def _place_own(shards, name):
    n = len(shards)

    def body(*refs):
        ins, outs, sems = refs[:n], refs[n : 2 * n], refs[2 * n]
        mx, my, _ = _my_place()
        cps = [pltpu.make_async_copy(ins[i], outs[i].at[2 * mx + my], sems.at[i]) for i in range(n)]
        for cp in cps:
            cp.start()
        for cp in cps:
            cp.wait()

    return pl.pallas_call(
        body,
        out_shape=[SDS((4,) + s.shape, s.dtype) for s in shards],
        in_specs=[_HBM] * n,
        out_specs=[_HBM] * n,
        scratch_shapes=[pltpu.SemaphoreType.DMA((n,))],
        name=name,
    )(*shards)


def _gather_start(shards, lands, after, name):
    n = len(shards)

    def body(*refs):
        ins, land = refs[:n], refs[n : 2 * n]
        send_sems, recv_sems = refs[2 * n + 1], refs[2 * n + 2]
        token = refs[-1]
        mx, my, mc = _my_place()
        for i in range(n):
            for rel in range(1, 4):
                pltpu.make_async_remote_copy(
                    src_ref=ins[i], dst_ref=land[i].at[2 * mx + my], send_sem=send_sems.at[3 * i + rel - 1],
                    recv_sem=recv_sems.at[3 * i + rel - 1], device_id=(_flip(mx, rel & 2), _flip(my, rel & 1), mc),
                    device_id_type=MESH,
                ).start()
        token[...] = jnp.zeros_like(token)

    outs = pl.pallas_call(
        body,
        name=name,
        out_shape=(
            pltpu.SemaphoreType.DMA((3 * n,)), pltpu.SemaphoreType.DMA((3 * n,)),
            *[pltpu.HBM(a.shape, a.dtype) for a in shards], *[pltpu.HBM(a.shape, a.dtype) for a in lands],
            SDS((8, LANES), F32),
        ),
        in_specs=[_HBM_ONLY] * (2 * n) + [_HBM],
        out_specs=(_SEM, _SEM, *[_HBM_ONLY] * (2 * n), pl.BlockSpec(memory_space=pltpu.VMEM)),
        input_output_aliases={i: 2 + i for i in range(2 * n)},
        compiler_params=pltpu.CompilerParams(has_side_effects=_EFFECT),
    )(*[_in_hbm(a) for a in shards], *[_in_hbm(a) for a in lands], after)
    return outs[0], outs[1], list(outs[2 : 2 + n]), list(outs[2 + n : 2 + 2 * n]), outs[-1]


def _gather_wait(send_sems, recv_sems, shards, lands, after, name):
    n = len(shards)

    def body(*refs):
        ins, land = refs[:n], refs[n : 2 * n]
        s_sems, r_sems = refs[2 * n], refs[2 * n + 1]
        mx, my, mc = _my_place()
        for i in range(n):
            for rel in range(1, 4):
                px, py = _flip(mx, rel & 2), _flip(my, rel & 1)
                cp = pltpu.make_async_remote_copy(
                    src_ref=ins[i], dst_ref=land[i].at[2 * px + py], send_sem=s_sems.at[3 * i + rel - 1],
                    recv_sem=r_sems.at[3 * i + rel - 1], device_id=(px, py, mc), device_id_type=MESH,
                )
                cp.wait_send()
                cp.wait_recv()

    outs = pl.pallas_call(
        body,
        name=name,
        out_shape=(*[pltpu.HBM(a.shape, a.dtype) for a in shards], *[pltpu.HBM(a.shape, a.dtype) for a in lands]),
        in_specs=[_HBM_ONLY] * (2 * n) + [_SEM, _SEM, _HBM],
        out_specs=[_HBM_ONLY] * (2 * n),
        input_output_aliases={i: i for i in range(2 * n)},
        compiler_params=pltpu.CompilerParams(has_side_effects=_EFFECT),
    )(*shards, *lands, send_sems, recv_sems, after)
    return list(outs[n:])


def _scatter_start(slabs, lands, places, name):
    n = len(slabs)

    def body(*refs):
        ins, land = refs[:n], refs[n : 2 * n]
        send_sems, recv_sems = refs[2 * n], refs[2 * n + 1]
        token = refs[-1]
        mx, my, mc = _my_place()
        for i in range(n):
            for rel in range(1, 4):
                px, py = _flip(mx, rel & 2), _flip(my, rel & 1)
                pltpu.make_async_remote_copy(
                    src_ref=ins[i].at[2 * px + py], dst_ref=land[i].at[rel - 1, places[i]],
                    send_sem=send_sems.at[3 * i + rel - 1], recv_sem=recv_sems.at[3 * i + rel - 1],
                    device_id=(px, py, mc), device_id_type=MESH,
                ).start()
        token[...] = jnp.zeros_like(token)

    outs = pl.pallas_call(
        body,
        name=name,
        out_shape=(
            pltpu.SemaphoreType.DMA((3 * n,)), pltpu.SemaphoreType.DMA((3 * n,)),
            *[pltpu.HBM(a.shape, a.dtype) for a in slabs], *[pltpu.HBM(a.shape, a.dtype) for a in lands],
            SDS((8, LANES), F32),
        ),
        in_specs=[_HBM_ONLY] * (2 * n),
        out_specs=(_SEM, _SEM, *[_HBM_ONLY] * (2 * n), pl.BlockSpec(memory_space=pltpu.VMEM)),
        input_output_aliases={i: 2 + i for i in range(2 * n)},
        compiler_params=pltpu.CompilerParams(has_side_effects=_EFFECT),
    )(*[_in_hbm(a) for a in slabs], *[_in_hbm(a) for a in lands])
    return outs[0], outs[1], list(outs[2 : 2 + n]), list(outs[2 + n : 2 + 2 * n]), outs[-1]


def _scatter_wait(send_sems, recv_sems, slabs, lands, places, after, name):
    n = len(slabs)

    def body(*refs):
        ins, land = refs[:n], refs[n : 2 * n]
        s_sems, r_sems = refs[2 * n], refs[2 * n + 1]
        mx, my, mc = _my_place()
        for i in range(n):
            for rel in range(1, 4):
                px, py = _flip(mx, rel & 2), _flip(my, rel & 1)
                cp = pltpu.make_async_remote_copy(
                    src_ref=ins[i].at[2 * px + py], dst_ref=land[i].at[rel - 1, places[i]],
                    send_sem=s_sems.at[3 * i + rel - 1], recv_sem=r_sems.at[3 * i + rel - 1],
                    device_id=(px, py, mc), device_id_type=MESH,
                )
                cp.wait_send()
                cp.wait_recv()

    outs = pl.pallas_call(
        body,
        name=name,
        out_shape=(*[pltpu.HBM(a.shape, a.dtype) for a in slabs], *[pltpu.HBM(a.shape, a.dtype) for a in lands]),
        in_specs=[_HBM_ONLY] * (2 * n) + [_SEM, _SEM, _HBM],
        out_specs=[_HBM_ONLY] * (2 * n),
        input_output_aliases={i: i for i in range(2 * n)},
        compiler_params=pltpu.CompilerParams(has_side_effects=_EFFECT),
    )(*slabs, *lands, send_sems, recv_sems, after)
    return list(outs[:n]), list(outs[n:])


def _swap_sibling(parts, name):
    n = len(parts)

    def body(*refs):
        ins, outs = refs[:n], refs[n : 2 * n]
        send_sems, recv_sems = refs[2 * n :]
        mx, my, mc = _my_place()
        sends = []
        for i in range(n):
            cp = pltpu.make_async_remote_copy(
                src_ref=ins[i], dst_ref=outs[i], send_sem=send_sems.at[i], recv_sem=recv_sems.at[i],
                device_id=(mx, my, 1 - mc), device_id_type=MESH,
            )
            cp.start()
            sends.append(cp)
        for cp in sends:
            cp.wait_recv()
        for cp in sends:
            cp.wait_send()

    return pl.pallas_call(
        body,
        out_shape=[SDS(s.shape, s.dtype) for s in parts],
        in_specs=[_HBM] * n,
        out_specs=[_HBM] * n,
        scratch_shapes=[pltpu.SemaphoreType.DMA((n,)), pltpu.SemaphoreType.DMA((n,))],
        name=name,
    )(*parts)


def _pad_rows(a, rows):
    return jnp.pad(a, ((0, rows - a.shape[0]), (0, 0)))


def kernel(x, c, positions, mla_w_in, mla_q_norm, mla_w_qb, mla_kv_norm, mla_w_kvb, mla_w_o, hgrn_lb, hgrn_w_in, hgrn_g_norm, hgrn_w_o, ffn_w_in, ffn_w_out, ada_w, ada_b, ln_g, ln_b, loss_target, m_mla_w_in, m_mla_q_norm, m_mla_w_qb, m_mla_kv_norm, m_mla_w_kvb, m_mla_w_o, m_hgrn_lb, m_hgrn_w_in, m_hgrn_g_norm, m_hgrn_w_o, m_ffn_w_in, m_ffn_w_out, m_ada_w, m_ada_b, m_ln_g, m_ln_b, v_mla_w_in, v_mla_q_norm, v_mla_w_qb, v_mla_kv_norm, v_mla_w_kvb, v_mla_w_o, v_hgrn_lb, v_hgrn_w_in, v_hgrn_g_norm, v_hgrn_w_o, v_ffn_w_in, v_ffn_w_out, v_ada_w, v_ada_b, v_ln_g, v_ln_b):
    B, S, D = x.shape
    T = B * S
    depth = ada_w.shape[0]
    n_mla, n_hgrn = mla_w_in.shape[0], hgrn_w_in.shape[0]
    n_sub = 2 * depth
    alpha = (2.0 * depth) ** 0.25
    mx, my, mc = _my_place()
    me = 4 * mx + 2 * my + mc
    k_me = 2 * mx + my
    Bg = 8 * B
    HK = hgrn_w_o.shape[1] * 4
    dq = D // 4

    lbw = hgrn_lb.shape[1]
    first = jnp.zeros((8, max(D, 4 * lbw)), F32)
    first = first.at[:B, :D].set(c).at[B : B + n_hgrn, :lbw].set(hgrn_lb)
    first_all = _allgather8(first, "gather_cond")
    c_all = first_all[:, :B, :D].reshape(Bg, D)
    lb_logits = jnp.concatenate([first_all[2 * k, B : B + n_hgrn, :lbw] for k in range(4)], axis=1)

    def lower_bounds_fn(logits):
        soft = jax.nn.softmax(logits, axis=0)
        return jnp.cumsum(soft, axis=0) - soft[0]

    lower_bounds, lower_bounds_vjp = jax.vjp(lower_bounds_fn, lb_logits)

    n_ada = ada_w.shape[-1]
    mod_part = _ada_fwd(c_all, ada_w.reshape(n_sub, D, n_ada), ada_b.reshape(n_sub, 1, n_ada), "ada_fwd")
    mod_all = _allgather8(mod_part.reshape(n_sub * Bg, n_ada), "gather_mod").reshape(8, n_sub, Bg, n_ada)
    mod = jnp.concatenate([mod_all[2 * k] for k in range(4)], axis=-1)
    mod = lax.dynamic_slice_in_dim(mod, me * B, B, axis=1)
    shift = [mod[j, :, None, :D] for j in range(n_sub)]
    scale = [mod[j, :, None, D : 2 * D] for j in range(n_sub)]
    gate = [mod[j, :, None, 2 * D :] for j in range(n_sub)]

    ln_rows = 2 * n_sub
    ln_local = _pad_rows(jnp.concatenate([ln_g.reshape(n_sub, dq), ln_b.reshape(n_sub, dq)], axis=0), -(-ln_rows // 8) * 8)
    ln_pad = jnp.zeros((ln_local.shape[0], -(-dq // LANES) * LANES), F32).at[:, :dq].set(ln_local)
    ln_all = _allgather8(ln_pad, "gather_ln")
    ln_full = jnp.concatenate([ln_all[2 * k, :ln_rows, :dq] for k in range(4)], axis=1)
    lng = [ln_full[j][None, :] for j in range(n_sub)]
    lnb = [ln_full[n_sub + j][None, :] for j in range(n_sub)]

    main = dict(mla_w_in=mla_w_in, mla_w_qb=mla_w_qb, mla_w_kvb=mla_w_kvb, mla_w_o=mla_w_o, hgrn_w_in=hgrn_w_in,
                hgrn_w_o=hgrn_w_o, ffn_w_in=ffn_w_in, ffn_w_out=ffn_w_out)
    names = list(main)

    def layer_kinds(layer):
        j = layer // 2
        mixer = ["mla_w_in", "mla_w_qb", "mla_w_kvb", "mla_w_o"] if layer % 2 == 0 else ["hgrn_w_in", "hgrn_w_o"]
        return [(k, j) for k in mixer] + [("ffn_w_in", layer), ("ffn_w_out", layer)]

    gathers = []
    after = first
    for layer in range(depth):
        shards = [main[k][i].astype(BF16) for k, i in layer_kinds(layer)]
        lands = _place_own(shards, f"place_own_l{layer}")
        ssem, rsem, shards, lands, after = _gather_start(shards, lands, after, f"gather_start_l{layer}")
        gathers.append((ssem, rsem, shards, lands))
    scale[0] = scale[0] + after[0, 0]

    def row_w(g):
        return g.reshape(1, g.shape[0] * g.shape[1], g.shape[2])

    def full_w_in(g):
        return jnp.transpose(g, (1, 0, 2)).reshape(1, g.shape[1], 4 * g.shape[2])

    ang = positions.astype(F32)[..., None] * (ROPE_THETA ** (-jnp.arange(0, QK_ROPE, 2, dtype=F32) / QK_ROPE))
    cos, sin = jnp.cos(ang), jnp.sin(ang)

    gq = [mla_q_norm[j][None, :] for j in range(n_mla)]
    gkv = [mla_kv_norm[j][None, :] for j in range(n_mla)]
    gn = [hgrn_g_norm[j][None, :] for j in range(n_hgrn)]

    def r2(a):
        return a.reshape(T, a.shape[-1])

    def r3(a):
        return a.reshape(B, S, a.shape[-1])

    saved = []
    xs = x
    for layer in range(depth):
        j = layer // 2
        sub = 2 * layer
        tag = f"l{layer}"
        ssem, rsem, shards, lands = gathers[layer]
        lands = _gather_wait(ssem, rsem, shards, lands, xs if layer else scale[0], f"gather_wait_{tag}")
        wl = {k: g for (k, _), g in zip(layer_kinds(layer), lands)}
        h = _modulate(xs, scale[sub], shift[sub], f"mod_{tag}a")
        if layer % 2 == 0:
            wl["mla_w_in"] = full_w_in(wl["mla_w_in"])
            proj = r3(_mm_nn(r2(h), wl["mla_w_in"], F32, f"mla_in_{tag}"))
            qn, kvn = _mla_mid_fwd(proj, gq[j], gkv[j], f"mla_mid_{tag}")
            q = r3(_mm_nn(r2(qn), wl["mla_w_qb"], F32, f"mla_qb_{tag}"))
            kv = r3(_mm_nn(r2(kvn), wl["mla_w_kvb"], F32, f"mla_kvb_{tag}"))
            qh, kh, vh = _mla_prep_fwd(q, kv, proj, cos, sin, f"mla_prep_{tag}")
            o, lse = _attn_fwd(qh, kh, vh, f"attn_{tag}")
            wl["mla_w_o"] = row_w(wl["mla_w_o"])
            y = r3(_mm_nn(r2(o), wl["mla_w_o"], F32, f"mla_o_{tag}"))
            mix = (h, proj, qn, kvn, qh, kh, vh, o, lse)
        else:
            proj = r3(_mm_nn(r2(h), wl["hgrn_w_in"], F32, f"hgrn_in_{tag}"))
            og, o_pre, states = _hgrn_fwd(proj, lower_bounds[j][None, :], gn[j], f"hgrn_{tag}")
            wl["hgrn_w_o"] = row_w(wl["hgrn_w_o"])
            y = r3(_mm_nn(r2(og), wl["hgrn_w_o"], F32, f"hgrn_o_{tag}"))
            mix = (h, proj, og, o_pre, states)
        x1 = _ln_fwd(alpha, xs, y, gate[sub], lng[sub], lnb[sub], f"ln_{tag}a")
        h2 = _modulate(x1, scale[sub + 1], shift[sub + 1], f"mod_{tag}b")
        u = r3(_mm_nn(r2(h2), wl["ffn_w_in"], F32, f"ffn_in_{tag}"))
        a = _swiglu_fwd(u, f"swiglu_{tag}")
        wl["ffn_w_out"] = row_w(wl["ffn_w_out"])
        y2 = r3(_mm_nn(r2(a), wl["ffn_w_out"], F32, f"ffn_out_{tag}"))
        x2 = _ln_fwd(alpha, x1, y2, gate[sub + 1], lng[sub + 1], lnb[sub + 1], f"ln_{tag}b")
        saved.append((xs, y, x1, y2, mix, h2, u, a, wl))
        xs = x2

    loss_local, dout = _loss_head(xs, loss_target, "loss_head")
    loss = lax.psum(loss_local, ("x", "y", "c"))

    gw = {k: [None] * main[k].shape[0] for k in names}
    land = {k: lax.empty((3,) + main[k].shape, BF16) for k in names}
    scatters = []
    d_shift, d_scale, d_gate = [None] * n_sub, [None] * n_sub, [None] * n_sub
    d_lng, d_lnb = [None] * n_sub, [None] * n_sub
    d_gq, d_gkv, d_gn, d_lbnd = [None] * n_mla, [None] * n_mla, [None] * n_hgrn, [None] * n_hgrn

    def rows4(g):
        return g.reshape(4, g.shape[1] // 4, g.shape[2])

    for layer in reversed(range(depth)):
        j = layer // 2
        sub = 2 * layer
        tag = f"l{layer}"
        xs, y, x1, y2, mix, h2, u, a, wl = saved[layer]
        dxr, dy2, d_gate[sub + 1], d_lng[sub + 1], d_lnb[sub + 1] = _ln_bwd(
            alpha, dout, x1, y2, gate[sub + 1], lng[sub + 1], lnb[sub + 1], f"ln_bwd_{tag}b")
        da = r3(_mm_nt(r2(dy2), wl["ffn_w_out"], F32, f"ffn_out_dx_{tag}"))
        gw["ffn_w_out"][layer] = rows4(_mm_tn(r2(a), r2(dy2), 1, BF16, f"ffn_out_dw_{tag}"))
        du = _swiglu_bwd(u, da, f"swiglu_bwd_{tag}")
        dh2 = r3(_mm_nt(r2(du), wl["ffn_w_in"], F32, f"ffn_in_dx_{tag}"))
        gw["ffn_w_in"][layer] = _mm_tn(r2(h2), r2(du), 4, BF16, f"ffn_in_dw_{tag}")
        dout, d_scale[sub + 1], d_shift[sub + 1] = _mod_bwd(dh2, dxr, x1, scale[sub + 1], f"mod_bwd_{tag}b")
        dxr, dy, d_gate[sub], d_lng[sub], d_lnb[sub] = _ln_bwd(
            alpha, dout, xs, y, gate[sub], lng[sub], lnb[sub], f"ln_bwd_{tag}a")
        if layer % 2 == 0:
            h, proj, qn, kvn, qh, kh, vh, o, lse = mix
            do = r3(_mm_nt(r2(dy), wl["mla_w_o"], BF16, f"mla_o_dx_{tag}"))
            gw["mla_w_o"][j] = rows4(_mm_tn(r2(o), r2(dy), 1, BF16, f"mla_o_dw_{tag}"))
            dqh, dkh, dvh = _attn_bwd(qh, kh, vh, o, do, lse, f"attn_bwd_{tag}")
            dq_, dkv_, dkr = _mla_prep_bwd(dqh, dkh, dvh, cos, sin, f"mla_prep_bwd_{tag}")
            dqn = r3(_mm_nt(r2(dq_), wl["mla_w_qb"], F32, f"mla_qb_dx_{tag}"))
            gw["mla_w_qb"][j] = _mm_tn(r2(qn), r2(dq_), 4, BF16, f"mla_qb_dw_{tag}")
            dkvn = r3(_mm_nt(r2(dkv_), wl["mla_w_kvb"], F32, f"mla_kvb_dx_{tag}"))
            gw["mla_w_kvb"][j] = _mm_tn(r2(kvn), r2(dkv_), 4, BF16, f"mla_kvb_dw_{tag}")
            dproj, dgq_, dgkv_ = _mla_mid_bwd(proj, dqn, dkvn, dkr, gq[j], gkv[j], f"mla_mid_bwd_{tag}")
            d_gq[j], d_gkv[j] = dgq_.sum(0), dgkv_.sum(0)
            dh = r3(_mm_nt(r2(dproj), wl["mla_w_in"], F32, f"mla_in_dx_{tag}"))
            gwin = _mm_tn(r2(h), r2(dproj), 1, BF16, f"mla_in_dw_{tag}")[0]
            gw["mla_w_in"][j] = jnp.transpose(gwin.reshape(gwin.shape[0], 4, gwin.shape[1] // 4), (1, 0, 2))
        else:
            h, proj, og, o_pre, states = mix
            dog = r3(_mm_nt(r2(dy), wl["hgrn_w_o"], F32, f"hgrn_o_dx_{tag}"))
            gw["hgrn_w_o"][j] = rows4(_mm_tn(r2(og), r2(dy), 1, BF16, f"hgrn_o_dw_{tag}"))
            dq_, df_, di_, dg_, dlb_, dgn_ = _hgrn_bwd(proj, lower_bounds[j][None, :], gn[j], o_pre, states, dog, f"hgrn_bwd_{tag}")
            dproj = jnp.concatenate([dq_, df_, di_, dg_], axis=-1)
            d_lbnd[j] = dlb_.sum(0).reshape(1, HK)
            d_gn[j] = dgn_.sum((0, 1))
            dh = r3(_mm_nt(r2(dproj), wl["hgrn_w_in"], F32, f"hgrn_in_dx_{tag}"))
            gw["hgrn_w_in"][j] = _mm_tn(r2(h), r2(dproj), 4, BF16, f"hgrn_in_dw_{tag}")
        kinds = layer_kinds(layer)
        ssem, rsem, slabs_t, lands_t, token = _scatter_start(
            [gw[k][i] for k, i in kinds], [land[k] for k, _ in kinds], [i for _, i in kinds], f"scatter_start_{tag}")
        for (k, i), s_t, l_t in zip(kinds, slabs_t, lands_t):
            gw[k][i], land[k] = s_t, l_t
        scatters.append((layer, ssem, rsem))
        if layer > 0:
            gate[sub - 1] = gate[sub - 1] + token[0, 0]
        dout, d_scale[sub], d_shift[sub] = _mod_bwd(dh, dxr, xs, scale[sub], f"mod_bwd_{tag}a")
    grad_x = dout

    for layer, ssem, rsem in scatters:
        kinds = layer_kinds(layer)
        slabs_t, lands_t = _scatter_wait(
            ssem, rsem, [gw[k][i] for k, i in kinds], [land[k] for k, _ in kinds], [i for _, i in kinds], grad_x,
            f"scatter_wait_l{layer}")
        for (k, i), s_t, l_t in zip(kinds, slabs_t, lands_t):
            gw[k][i], land[k] = s_t, l_t
    sums = [_sum4(jnp.stack([lax.dynamic_index_in_dim(g, k_me, 0, keepdims=False) for g in gw[k]]), land[k], f"sum4_{k}")
            for k in names]
    others = _swap_sibling(sums, "swap_sums")
    g_main = {k: (a_, b_) for k, a_, b_ in zip(names, sums, others)}

    dmod = jnp.stack([jnp.concatenate([d_shift[s_][:, 0], d_scale[s_][:, 0], d_gate[s_][:, 0]], axis=-1) for s_ in range(n_sub)])
    dmod_rows = _pad_rows(dmod.reshape(n_sub * B, 3 * D), -(-n_sub * B // 8) * 8)
    dmod_all = _allgather8(dmod_rows, "gather_dmod")[:, : n_sub * B].reshape(8, n_sub, B, 3 * D)
    dmod_all = jnp.transpose(dmod_all, (1, 0, 2, 3)).reshape(n_sub, Bg, 3 * D)
    dmod_mine = lax.dynamic_slice_in_dim(dmod_all, k_me * n_ada, n_ada, axis=2)
    g_ada_w, g_ada_b = _ada_bwd(c_all, dmod_mine, "ada_bwd")
    g_ada_w = g_ada_w.reshape(ada_w.shape)
    g_ada_b = g_ada_b.reshape(ada_b.shape)

    small = [jnp.stack(d_gq).reshape(-1), jnp.stack(d_gkv).reshape(-1), jnp.stack(d_gn).reshape(-1),
             jnp.stack(d_lbnd).reshape(-1), jnp.stack([d.sum(0) for d in d_lng]).reshape(-1),
             jnp.stack([d.sum(0) for d in d_lnb]).reshape(-1)]
    sizes = [s_.shape[0] for s_ in small]
    flat = jnp.concatenate(small)
    rows_small = -(-flat.shape[0] // (8 * LANES)) * 8
    flat = jnp.pad(flat, (0, rows_small * LANES - flat.shape[0])).reshape(rows_small, LANES)
    tot = _allgather8(flat, "gather_small")
    acc = tot[0]
    for d in range(1, 8):
        acc = acc + tot[d]
    acc = acc.reshape(-1)
    offs = [0]
    for s_ in sizes:
        offs.append(offs[-1] + s_)
    g_q_norm = acc[offs[0] : offs[1]].reshape(mla_q_norm.shape)
    g_kv_norm = acc[offs[1] : offs[2]].reshape(mla_kv_norm.shape)
    g_g_norm = acc[offs[2] : offs[3]].reshape(hgrn_g_norm.shape)
    g_lbnd = acc[offs[3] : offs[4]].reshape(n_hgrn, HK)
    g_lb_full = lower_bounds_vjp(g_lbnd)[0]
    g_hgrn_lb = lax.dynamic_slice_in_dim(g_lb_full, k_me * lbw, lbw, axis=1)
    g_lng = lax.dynamic_slice_in_dim(acc[offs[4] : offs[5]].reshape(n_sub, D), k_me * dq, dq, axis=1).reshape(ln_g.shape)
    g_lnb = lax.dynamic_slice_in_dim(acc[offs[5] : offs[6]].reshape(n_sub, D), k_me * dq, dq, axis=1).reshape(ln_b.shape)

    weights = dict(mla_w_in=mla_w_in, mla_q_norm=mla_q_norm, mla_w_qb=mla_w_qb, mla_kv_norm=mla_kv_norm, mla_w_kvb=mla_w_kvb,
                   mla_w_o=mla_w_o, hgrn_lb=hgrn_lb, hgrn_w_in=hgrn_w_in, hgrn_g_norm=hgrn_g_norm, hgrn_w_o=hgrn_w_o,
                   ffn_w_in=ffn_w_in, ffn_w_out=ffn_w_out, ada_w=ada_w, ada_b=ada_b, ln_g=ln_g, ln_b=ln_b)
    moms = dict(mla_w_in=(m_mla_w_in, v_mla_w_in), mla_q_norm=(m_mla_q_norm, v_mla_q_norm), mla_w_qb=(m_mla_w_qb, v_mla_w_qb),
                mla_kv_norm=(m_mla_kv_norm, v_mla_kv_norm), mla_w_kvb=(m_mla_w_kvb, v_mla_w_kvb), mla_w_o=(m_mla_w_o, v_mla_w_o),
                hgrn_lb=(m_hgrn_lb, v_hgrn_lb), hgrn_w_in=(m_hgrn_w_in, v_hgrn_w_in), hgrn_g_norm=(m_hgrn_g_norm, v_hgrn_g_norm),
                hgrn_w_o=(m_hgrn_w_o, v_hgrn_w_o), ffn_w_in=(m_ffn_w_in, v_ffn_w_in), ffn_w_out=(m_ffn_w_out, v_ffn_w_out),
                ada_w=(m_ada_w, v_ada_w), ada_b=(m_ada_b, v_ada_b), ln_g=(m_ln_g, v_ln_g), ln_b=(m_ln_b, v_ln_b))
    grads = dict(mla_q_norm=(g_q_norm,), mla_kv_norm=(g_kv_norm,), hgrn_lb=(g_hgrn_lb,), hgrn_g_norm=(g_g_norm,),
                 ada_w=(g_ada_w,), ada_b=(g_ada_b,), ln_g=(g_lng,), ln_b=(g_lnb,), **g_main)
    res = {k: _adamw(weights[k], [g_.reshape(weights[k].shape) for g_ in grads[k]], moms[k][0], moms[k][1], f"adamw_{k}")
           for k in weights}
    order = list(weights)
    return (loss, grad_x, *[res[k][0] for k in order], *[res[k][1] for k in order], *[res[k][2] for k in order],
            *[res[k][3] for k in order])
```

```python
import functools

import jax
import jax.numpy as jnp
from jax import lax
from jax.experimental import pallas as pl
from jax.experimental.pallas import tpu as pltpu

F32 = jnp.float32
BF16 = jnp.bfloat16
SDS = jax.ShapeDtypeStruct
MESH = pl.DeviceIdType.MESH
HI = lax.Precision.HIGHEST
MID = lax.Precision.HIGH

MLA_HEADS, QK_NOPE, QK_ROPE, V_HEAD = 16, 64, 32, 64
Q_LORA, KV_LORA = 768, 256
QK_DIM = QK_NOPE + QK_ROPE
ROPE_THETA = 10000.0
HGRN_K = 128
HGRN_CHUNK = 64
HGRN_SUB = 16
HGRN_PAR = 2
LN_EPS, RMS_EPS = 1e-5, 1e-6
ADAM_LR, ADAM_B1, ADAM_B2, ADAM_EPS, ADAM_WD, ADAM_STEP = 0.001, 0.9, 0.999, 1e-08, 0.01, 10
NEG = -1e30

VMEM_LIMIT_BYTES = 56 * 1024 * 1024
LANES = 128


def _cparams(*sem):
    return pltpu.CompilerParams(dimension_semantics=sem if sem else None, vmem_limit_bytes=VMEM_LIMIT_BYTES)


def _pick_tile(n, cap):
    best = 0
    for t in range(LANES, min(n, cap) + 1, LANES):
        if n % t == 0:
            best = t
    return best if best else n


def _bdot(a, b):
    return jnp.dot(a.astype(BF16), b.astype(BF16), preferred_element_type=F32)


def _bdot_nt(a, b):
    return lax.dot_general(a.astype(BF16), b.astype(BF16), (((1,), (1,)), ((), ())), preferred_element_type=F32)


def _bdot_tn(a, b):
    return lax.dot_general(a.astype(BF16), b.astype(BF16), (((0,), (0,)), ((), ())), preferred_element_type=F32)


def _hdot(a, b):
    return jnp.dot(a, b, precision=HI, preferred_element_type=F32)


def _mdot(a, b):
    return jnp.dot(a, b, precision=MID, preferred_element_type=F32)


def _mdot_nt(a, b):
    return lax.dot_general(a, b, (((1,), (1,)), ((), ())), precision=MID, preferred_element_type=F32)


def _mdot_tn(a, b):
    return lax.dot_general(a, b, (((0,), (0,)), ((), ())), precision=MID, preferred_element_type=F32)


def _mm_nn(a, w, out_dtype, name):
    M, K = a.shape
    G, _, n = w.shape
    tm = min(512, M)
    tn = _pick_tile(n, 1536)
    nps = n // tn

    def body(a_ref, w_ref, o_ref):
        o_ref[...] = _bdot(a_ref[...], w_ref[...]).astype(o_ref.dtype)

    return pl.pallas_call(
        body,
        grid=(G * nps, M // tm),
        in_specs=[
            pl.BlockSpec((tm, K), lambda j, i: (i, 0)),
            pl.BlockSpec((None, K, tn), lambda j, i: (j // nps, 0, j % nps)),
        ],
        out_specs=pl.BlockSpec((tm, tn), lambda j, i: (i, j)),
        out_shape=SDS((M, G * n), out_dtype),
        name=name,
        compiler_params=_cparams("parallel", "parallel"),
    )(a, w)


def _mm_nt(a, w, out_dtype, name):
    M = a.shape[0]
    G, K, n = w.shape
    tm = min(512, M)
    tk = _pick_tile(K, 1536)

    def body(a_ref, w_ref, o_ref, acc_ref):
        s = pl.program_id(2)

        @pl.when(s == 0)
        def _():
            acc_ref[...] = jnp.zeros_like(acc_ref)

        acc_ref[...] += _bdot_nt(a_ref[...], w_ref[...])

        @pl.when(s == G - 1)
        def _():
            o_ref[...] = acc_ref[...].astype(o_ref.dtype)

    return pl.pallas_call(
        body,
        grid=(K // tk, M // tm, G),
        in_specs=[
            pl.BlockSpec((tm, n), lambda kb, i, s: (i, s)),
            pl.BlockSpec((None, tk, n), lambda kb, i, s: (s, kb, 0)),
        ],
        out_specs=pl.BlockSpec((tm, tk), lambda kb, i, s: (i, kb)),
        out_shape=SDS((M, K), out_dtype),
        scratch_shapes=[pltpu.VMEM((tm, tk), F32)],
        name=name,
        compiler_params=_cparams("parallel", "parallel", "arbitrary"),
    )(a, w)


def _mm_tn(a, d, G, out_dtype, name):
    T, K = a.shape
    n = d.shape[1] // G
    tk = _pick_tile(K, 256)
    tn = _pick_tile(n, 1536)
    nps = n // tn

    def body(a_ref, d_ref, o_ref):
        o_ref[...] = _bdot_tn(a_ref[...], d_ref[...]).astype(o_ref.dtype)

    return pl.pallas_call(
        body,
        grid=(G * nps, K // tk),
        in_specs=[
            pl.BlockSpec((T, tk), lambda j, i: (0, i)),
            pl.BlockSpec((T, tn), lambda j, i: (0, j)),
        ],
        out_specs=pl.BlockSpec((None, tk, tn), lambda j, i: (j // nps, i, j % nps)),
        out_shape=SDS((G, K, n), out_dtype),
        name=name,
        compiler_params=_cparams("parallel", "parallel"),
    )(a, d)


def _rows_call(body, name, B, S, ins, outs, ts=256):
    ts = min(ts, S)
    in_specs, args = [], []
    for arr, kind in ins:
        W = arr.shape[-1]
        if kind == "row":
            in_specs.append(pl.BlockSpec((None, ts, W), lambda b, s: (b, s, 0)))
        elif kind == "ex":
            in_specs.append(pl.BlockSpec((None, 1, W), lambda b, s: (b, 0, 0)))
        else:
            in_specs.append(pl.BlockSpec((1, W), lambda b, s: (0, 0)))
        args.append(arr)
    out_specs, out_shape = [], []
    for W, dt, kind in outs:
        if kind == "row":
            out_specs.append(pl.BlockSpec((None, ts, W), lambda b, s: (b, s, 0)))
            out_shape.append(SDS((B, S, W), dt))
        else:
            out_specs.append(pl.BlockSpec((None, 1, W), lambda b, s: (b, 0, 0)))
            out_shape.append(SDS((B, 1, W), dt))
    return pl.pallas_call(
        body,
        grid=(B, S // ts),
        in_specs=in_specs,
        out_specs=out_specs,
        out_shape=out_shape,
        name=name,
        compiler_params=_cparams("parallel", "arbitrary"),
    )(*args)


def _acc(ref, val):
    @pl.when(pl.program_id(1) == 0)
    def _():
        ref[...] = jnp.zeros_like(ref)

    ref[...] += val


def _mod_fn(x, sc, sh):
    return x * (1.0 + sc) + sh


def _ln_fn(alpha, x, y, gate, g, b):
    z = alpha * x + (1.0 + gate) * y
    mu = jnp.mean(z, -1, keepdims=True)
    var = jnp.mean(jnp.square(z - mu), -1, keepdims=True)
    return (z - mu) * lax.rsqrt(var + LN_EPS) * g + b


def _modulate(x, sc, sh, name):
    B, S, D = x.shape

    def body(x_ref, sc_ref, sh_ref, h_ref):
        h_ref[...] = _mod_fn(x_ref[...], sc_ref[...], sh_ref[...]).astype(BF16)

    return _rows_call(body, name, B, S, [(x, "row"), (sc, "ex"), (sh, "ex")], [(D, BF16, "row")])[0]


def _ln_fwd(alpha, x, y, gate, g, b, name):
    B, S, D = x.shape

    def body(x_ref, y_ref, gate_ref, g_ref, b_ref, o_ref):
        o_ref[...] = _ln_fn(alpha, x_ref[...], y_ref[...], gate_ref[...], g_ref[...], b_ref[...])

    return _rows_call(
        body, name, B, S, [(x, "row"), (y, "row"), (gate, "ex"), (g, "par"), (b, "par")], [(D, F32, "row")]
    )[0]


def _ln_bwd(alpha, dout, x, y, gate, g, b, name):
    B, S, D = x.shape

    def body(do_ref, x_ref, y_ref, gate_ref, g_ref, b_ref, dxr_ref, dy_ref, dgate_ref, dg_ref, db_ref):
        _, vjp = jax.vjp(
            functools.partial(_ln_fn, alpha), x_ref[...], y_ref[...], gate_ref[...], g_ref[...], b_ref[...]
        )
        dx, dy, dgate, dg, db = vjp(do_ref[...])
        dxr_ref[...] = dx
        dy_ref[...] = dy.astype(BF16)
        _acc(dgate_ref, dgate)
        _acc(dg_ref, dg)
        _acc(db_ref, db)

    return _rows_call(
        body,
        name,
        B,
        S,
        [(dout, "row"), (x, "row"), (y, "row"), (gate, "ex"), (g, "par"), (b, "par")],
        [(D, F32, "row"), (D, BF16, "row"), (D, F32, "acc"), (D, F32, "acc"), (D, F32, "acc")],
    )


def _mod_bwd(dh, dxr, x, sc, name):
    B, S, D = x.shape

    def body(dh_ref, dxr_ref, x_ref, sc_ref, dx_ref, dsc_ref, dsh_ref):
        dh_v = dh_ref[...]
        dx_ref[...] = dxr_ref[...] + dh_v * (1.0 + sc_ref[...])
        _acc(dsc_ref, jnp.sum(dh_v * x_ref[...], axis=0, keepdims=True))
        _acc(dsh_ref, jnp.sum(dh_v, axis=0, keepdims=True))

    return _rows_call(
        body,
        name,
        B,
        S,
        [(dh, "row"), (dxr, "row"), (x, "row"), (sc, "ex")],
        [(D, F32, "row"), (D, F32, "acc"), (D, F32, "acc")],
    )


def _loss_head(y, target, name):
    B, S, D = y.shape

    def body(y_ref, t_ref, l_ref, dy_ref):
        e = y_ref[...] - t_ref[...]
        dy_ref[...] = e * (1.0 / D)
        part = 0.5 * jnp.sum(jnp.sum(e * e, axis=1, keepdims=True) * (1.0 / D), axis=0, keepdims=True)
        _acc(l_ref, jnp.broadcast_to(part, (1, LANES)))

    loss, dy = _rows_call(
        body, name, B, S, [(y, "row"), (target, "row")], [(LANES, F32, "acc"), (D, F32, "row")]
    )
    return jnp.sum(loss[:, 0, 0]), dy


def _swiglu_fn(u):
    F = u.shape[-1] // 2
    return jax.nn.silu(u[:, :F]) * u[:, F:]


def _swiglu_fwd(u, name):
    B, S, F2 = u.shape

    def body(u_ref, a_ref):
        a_ref[...] = _swiglu_fn(u_ref[...]).astype(BF16)

    return _rows_call(body, name, B, S, [(u, "row")], [(F2 // 2, BF16, "row")])[0]


def _swiglu_bwd(u, da, name):
    B, S, F2 = u.shape

    def body(u_ref, da_ref, du_ref):
        _, vjp = jax.vjp(_swiglu_fn, u_ref[...])
        du_ref[...] = vjp(da_ref[...])[0].astype(BF16)

    return _rows_call(body, name, B, S, [(u, "row"), (da, "row")], [(F2, BF16, "row")])[0]


def _rms_fn(x, g):
    return x * lax.rsqrt(jnp.mean(jnp.square(x), -1, keepdims=True) + RMS_EPS) * g


def _mla_mid_fwd(proj, gq, gkv, name):
    B, S, _ = proj.shape

    def body(p_ref, gq_ref, gkv_ref, qn_ref, kvn_ref):
        p = p_ref[...]
        qn_ref[...] = _rms_fn(p[:, :Q_LORA], gq_ref[...]).astype(BF16)
        kvn_ref[...] = _rms_fn(p[:, Q_LORA : Q_LORA + KV_LORA], gkv_ref[...]).astype(BF16)

    return _rows_call(
        body, name, B, S, [(proj, "row"), (gq, "par"), (gkv, "par")], [(Q_LORA, BF16, "row"), (KV_LORA, BF16, "row")]
    )


def _mla_mid_bwd(proj, dqn, dkvn, dkr, gq, gkv, name):
    B, S, W = proj.shape

    def body(p_ref, dqn_ref, dkvn_ref, dkr_ref, gq_ref, gkv_ref, dp_ref, dgq_ref, dgkv_ref):
        p = p_ref[...]
        _, vq = jax.vjp(_rms_fn, p[:, :Q_LORA], gq_ref[...])
        dql, dgq = vq(dqn_ref[...])
        _, vkv = jax.vjp(_rms_fn, p[:, Q_LORA : Q_LORA + KV_LORA], gkv_ref[...])
        dkvl, dgkv = vkv(dkvn_ref[...])
        dp_ref[:, :Q_LORA] = dql.astype(BF16)
        dp_ref[:, Q_LORA : Q_LORA + KV_LORA] = dkvl.astype(BF16)
        dp_ref[:, Q_LORA + KV_LORA :] = dkr_ref[...].astype(BF16)
        _acc(dgq_ref, dgq)
        _acc(dgkv_ref, dgkv)

    return _rows_call(
        body,
        name,
        B,
        S,
        [(proj, "row"), (dqn, "row"), (dkvn, "row"), (dkr, "row"), (gq, "par"), (gkv, "par")],
        [(W, BF16, "row"), (Q_LORA, F32, "acc"), (KV_LORA, F32, "acc")],
    )


def _rope(x, cos, sin):
    h = QK_ROPE // 2
    x1, x2 = x[:, :h], x[:, h:]
    return jnp.concatenate([x1 * cos - x2 * sin, x1 * sin + x2 * cos], axis=1)


def _rope_t(dy, cos, sin):
    h = QK_ROPE // 2
    d1, d2 = dy[:, :h], dy[:, h:]
    return jnp.concatenate([d1 * cos + d2 * sin, d2 * cos - d1 * sin], axis=1)


def _heads_call(body, name, B, S, ins, outs, ts=256):
    ts = min(ts, S)
    in_specs, args = [], []
    for arr, kind in ins:
        if kind == "row":
            in_specs.append(pl.BlockSpec((None, ts, arr.shape[-1]), lambda b, s: (b, s, 0)))
        else:
            in_specs.append(pl.BlockSpec((arr.shape[0], None, ts, arr.shape[-1]), lambda b, s: (0, b, s, 0)))
        args.append(arr)
    out_specs, out_shape = [], []
    for shape, dt, kind in outs:
        if kind == "row":
            out_specs.append(pl.BlockSpec((None, ts, shape[-1]), lambda b, s: (b, s, 0)))
        else:
            out_specs.append(pl.BlockSpec((shape[0], None, ts, shape[-1]), lambda b, s: (0, b, s, 0)))
        out_shape.append(SDS(shape, dt))
    return pl.pallas_call(
        body,
        grid=(B, S // ts),
        in_specs=in_specs,
        out_specs=out_specs,
        out_shape=out_shape,
        name=name,
        compiler_params=_cparams("parallel", "parallel"),
    )(*args)


def _mla_prep_fwd(q, kv, proj, cos, sin, name):
    B, S, _ = q.shape
    H = MLA_HEADS

    def body(q_ref, kv_ref, p_ref, cos_ref, sin_ref, qh_ref, kh_ref, vh_ref):
        cos_v, sin_v = cos_ref[...], sin_ref[...]
        kr = _rope(p_ref[:, Q_LORA + KV_LORA :], cos_v, sin_v).astype(BF16)
        for h in range(H):
            qn = q_ref[:, h * QK_DIM : h * QK_DIM + QK_NOPE]
            qr = _rope(q_ref[:, h * QK_DIM + QK_NOPE : (h + 1) * QK_DIM], cos_v, sin_v)
            qh_ref[h] = jnp.concatenate([qn, qr], axis=1).astype(BF16)
            kn = kv_ref[:, h * 128 : h * 128 + QK_NOPE].astype(BF16)
            kh_ref[h] = jnp.concatenate([kn, kr], axis=1)
            vh_ref[h] = kv_ref[:, h * 128 + QK_NOPE : (h + 1) * 128].astype(BF16)

    return _heads_call(
        body,
        name,
        B,
        S,
        [(q, "row"), (kv, "row"), (proj, "row"), (cos, "row"), (sin, "row")],
        [((H, B, S, QK_DIM), BF16, "heads"), ((H, B, S, QK_DIM), BF16, "heads"), ((H, B, S, V_HEAD), BF16, "heads")],
    )


def _mla_prep_bwd(dqh, dkh, dvh, cos, sin, name):
    H, B, S, _ = dqh.shape

    def body(dqh_ref, dkh_ref, dvh_ref, cos_ref, sin_ref, dq_ref, dkv_ref, dkr_ref):
        cos_v, sin_v = cos_ref[...], sin_ref[...]
        dkr = jnp.zeros((cos_v.shape[0], QK_ROPE), F32)
        for h in range(H):
            dqv = dqh_ref[h].astype(F32)
            dq_ref[:, h * QK_DIM : h * QK_DIM + QK_NOPE] = dqv[:, :QK_NOPE].astype(BF16)
            dq_ref[:, h * QK_DIM + QK_NOPE : (h + 1) * QK_DIM] = _rope_t(dqv[:, QK_NOPE:], cos_v, sin_v).astype(BF16)
            dkv = dkh_ref[h].astype(F32)
            dkv_ref[:, h * 128 : h * 128 + QK_NOPE] = dkv[:, :QK_NOPE].astype(BF16)
            dkv_ref[:, h * 128 + QK_NOPE : (h + 1) * 128] = dvh_ref[h]
            dkr = dkr + dkv[:, QK_NOPE:]
        dkr_ref[...] = _rope_t(dkr, cos_v, sin_v)

    return _heads_call(
        body,
        name,
        B,
        S,
        [(dqh, "heads"), (dkh, "heads"), (dvh, "heads"), (cos, "row"), (sin, "row")],
        [((B, S, H * QK_DIM), BF16, "row"), ((B, S, H * 128), BF16, "row"), ((B, S, QK_ROPE), F32, "row")],
    )


LOG2E = 1.4426950408889634


def _tril_mask(t):
    return lax.broadcasted_iota(jnp.int32, (t, t), 0) >= lax.broadcasted_iota(jnp.int32, (t, t), 1)


def _attn_fwd(qh, kh, vh, name):
    H, B, S, _ = qh.shape
    t = min(256, S)
    scale = QK_DIM**-0.5
    c2 = scale * LOG2E

    def body(q_ref, k_ref, v_ref, o_ref, lse_ref):
        i = pl.program_id(2)
        qs = [q_ref[0], q_ref[1]]

        def step(j, carry, diagonal):
            rows = pl.ds(pl.multiple_of(j * t, t), t)
            out = []
            for hh in range(2):
                m, l, acc = carry[hh]
                s = _bdot_nt(qs[hh], k_ref[hh, rows, :])
                if diagonal:
                    s = jnp.where(_tril_mask(t), s, NEG)
                m_new = jnp.maximum(m, jnp.max(s, axis=1, keepdims=True))
                p = jnp.exp2((s - m_new) * c2)
                a = jnp.exp2((m - m_new) * c2)
                l = a * l + jnp.sum(p, axis=1, keepdims=True)
                acc = a * acc + _bdot(p, v_ref[hh, rows, :])
                out.append((m_new, l, acc))
            return tuple(out)

        one = (jnp.full((t, 1), NEG, F32), jnp.zeros((t, 1), F32), jnp.zeros((t, V_HEAD), F32))
        carry = lax.fori_loop(0, i, lambda j, cy: step(j, cy, False), (one, one))
        carry = step(i, carry, True)
        outs = []
        for hh in range(2):
            m, l, acc = carry[hh]
            outs.append(acc / l)
            lse_ref[hh] = m * scale + jnp.log(l)
        o_ref[...] = jnp.concatenate(outs, axis=1).astype(BF16)

    return pl.pallas_call(
        body,
        grid=(B, H // 2, S // t),
        in_specs=[
            pl.BlockSpec((2, None, t, QK_DIM), lambda b, p, i: (p, b, i, 0)),
            pl.BlockSpec((2, None, S, QK_DIM), lambda b, p, i: (p, b, 0, 0)),
            pl.BlockSpec((2, None, S, V_HEAD), lambda b, p, i: (p, b, 0, 0)),
        ],
        out_specs=[
            pl.BlockSpec((None, t, 2 * V_HEAD), lambda b, p, i: (b, i, p)),
            pl.BlockSpec((2, None, t, 1), lambda b, p, i: (p, b, i, 0)),
        ],
        out_shape=[SDS((B, S, H * V_HEAD), BF16), SDS((H, B, S, 1), F32)],
        name=name,
        compiler_params=_cparams("parallel", "parallel", "arbitrary"),
    )(qh, kh, vh)


def _attn_bwd(qh, kh, vh, o, do, lse, name):
    H, B, S, _ = qh.shape
    t = min(256, S)
    nq = S // t
    scale = QK_DIM**-0.5
    c2 = scale * LOG2E

    def body(q_ref, k_ref, v_ref, o_ref, do_ref, lse_ref, dq_ref, dk_ref, dv_ref, dq_acc, delta_ref, lse2_ref):
        prod = o_ref[...].astype(F32) * do_ref[...].astype(F32)
        for hh in range(2):
            delta_ref[hh] = jnp.sum(prod[:, hh * V_HEAD : (hh + 1) * V_HEAD], axis=1, keepdims=True)
            lse2_ref[hh] = lse_ref[hh] * LOG2E
        dq_acc[...] = jnp.zeros_like(dq_acc)

        def kloop(j, _):
            krows = pl.ds(pl.multiple_of(j * t, t), t)
            ks = [k_ref[0, krows, :], k_ref[1, krows, :]]
            vs = [v_ref[0, krows, :], v_ref[1, krows, :]]

            def qstep(i, carry, diagonal):
                qrows = pl.ds(pl.multiple_of(i * t, t), t)
                do_i = do_ref[qrows, :]
                out = []
                for hh in range(2):
                    dk, dv = carry[hh]
                    q = q_ref[hh, qrows, :]
                    do_h = do_i[:, hh * V_HEAD : (hh + 1) * V_HEAD]
                    s = _bdot_nt(q, ks[hh])
                    p = jnp.exp2(s * c2 - lse2_ref[hh, qrows, :])
                    if diagonal:
                        p = jnp.where(_tril_mask(t), p, 0.0)
                    dv = dv + _bdot_tn(p, do_h)
                    dp = _bdot_nt(do_h, vs[hh])
                    ds = (p * (dp - delta_ref[hh, qrows, :])).astype(BF16)
                    dk = dk + _bdot_tn(ds, q)
                    dq_acc[hh, qrows, :] += _bdot(ds, ks[hh])
                    out.append((dk, dv))
                return tuple(out)

            one = (jnp.zeros((t, QK_DIM), F32), jnp.zeros((t, V_HEAD), F32))
            carry = qstep(j, (one, one), True)
            carry = lax.fori_loop(j + 1, nq, lambda i, cy: qstep(i, cy, False), carry)
            for hh in range(2):
                dk_ref[hh, krows, :] = (carry[hh][0] * scale).astype(BF16)
                dv_ref[hh, krows, :] = carry[hh][1].astype(BF16)
            return 0

        lax.fori_loop(0, nq, kloop, 0)
        dq_ref[...] = (dq_acc[...] * scale).astype(BF16)

    hspec = lambda w: pl.BlockSpec((2, None, S, w), lambda b, p: (p, b, 0, 0))
    ospec = pl.BlockSpec((None, S, 2 * V_HEAD), lambda b, p: (b, 0, p))
    return pl.pallas_call(
        body,
        grid=(B, H // 2),
        in_specs=[hspec(QK_DIM), hspec(QK_DIM), hspec(V_HEAD), ospec, ospec, hspec(1)],
        out_specs=[hspec(QK_DIM), hspec(QK_DIM), hspec(V_HEAD)],
        out_shape=[SDS((H, B, S, QK_DIM), BF16), SDS((H, B, S, QK_DIM), BF16), SDS((H, B, S, V_HEAD), BF16)],
        scratch_shapes=[pltpu.VMEM((2, S, QK_DIM), F32), pltpu.VMEM((2, S, 1), F32), pltpu.VMEM((2, S, 1), F32)],
        name=name,
        compiler_params=_cparams("parallel", "parallel"),
    )(qh, kh, vh, o, do, lse)


def _hgrn_pre(q, fx, lb):
    sig = jax.nn.sigmoid(fx)
    f = lb + (1.0 - lb) * sig
    return jax.nn.silu(q), 1.0 - f, jnp.log(f)


def _hgrn_gate(o, gg, gn):
    return _rms_fn(o, gn) * jax.nn.silu(gg)


def _tri(n, lower):
    r = lax.broadcasted_iota(jnp.int32, (n, n), 0)
    c = lax.broadcasted_iota(jnp.int32, (n, n), 1)
    return ((r >= c) if lower else (r <= c)).astype(F32)


def _hgrn_intra_fwd(qs, k, v, b):
    C, SB = qs.shape[0], min(HGRN_SUB, qs.shape[0])
    ridx = lax.broadcasted_iota(jnp.int32, (SB, 1), 0)
    outs = []
    for i in range(C // SB):
        r0 = i * SB
        qi, ki, vi, bi = qs[r0 : r0 + SB], k[r0 : r0 + SB], v[r0 : r0 + SB], b[r0 : r0 + SB]
        acc = jnp.zeros((SB, v.shape[1]), F32)
        for s in range(SB):
            mask = ridx >= s
            e = jnp.exp(jnp.where(mask, bi - bi[s : s + 1], 0.0))
            a = jnp.sum(jnp.where(mask, qi * ki[s : s + 1] * e, 0.0), axis=1, keepdims=True)
            acc = acc + a * vi[s : s + 1]
        if i > 0:
            ref = bi[0:1]
            qt = qi * jnp.exp(bi - ref)
            kt = k[:r0] * jnp.exp(ref - b[:r0])
            acc = acc + _bdot(_mdot_nt(qt, kt), v[:r0])
        outs.append(acc)
    return jnp.concatenate(outs, axis=0)


def _hgrn_intra_bwd(qs, k, v, b, do):
    C, SB = qs.shape[0], min(HGRN_SUB, qs.shape[0])
    nb = C // SB
    ridx = lax.broadcasted_iota(jnp.int32, (SB, 1), 0)
    dq_p = [None] * nb
    dk_p = [jnp.zeros((SB, k.shape[1]), F32) for _ in range(nb)]
    dv_p = [jnp.zeros((SB, v.shape[1]), F32) for _ in range(nb)]
    for i in range(nb):
        r0 = i * SB
        qi, ki, vi, bi, doi = qs[r0 : r0 + SB], k[r0 : r0 + SB], v[r0 : r0 + SB], b[r0 : r0 + SB], do[r0 : r0 + SB]
        dqi = jnp.zeros_like(qi)
        dki = jnp.zeros_like(ki)
        dvi = jnp.zeros_like(vi)
        for s in range(SB):
            mask = ridx >= s
            e = jnp.where(mask, jnp.exp(jnp.where(mask, bi - bi[s : s + 1], 0.0)), 0.0)
            da = jnp.sum(doi * vi[s : s + 1], axis=1, keepdims=True)
            a = jnp.sum(qi * ki[s : s + 1] * e, axis=1, keepdims=True)
            dqi = dqi + da * (ki[s : s + 1] * e)
            dk_row = jnp.sum(da * qi * e, axis=0, keepdims=True)
            dv_row = jnp.sum(a * doi, axis=0, keepdims=True)
            dki = jnp.where(ridx == s, dki + dk_row, dki)
            dvi = jnp.where(ridx == s, dvi + dv_row, dvi)
        if i > 0:
            ref = bi[0:1]
            eq = jnp.exp(bi - ref)
            ek = jnp.exp(ref - b[:r0])
            qt = qi * eq
            kt = k[:r0] * ek
            A = _mdot_nt(qt, kt)
            dA = _bdot_nt(doi, v[:r0])
            dvl = _bdot_tn(A, doi)
            dqi = dqi + _mdot(dA, kt) * eq
            dkl = _mdot_tn(dA, qt) * ek
            for j in range(i):
                dk_p[j] = dk_p[j] + dkl[j * SB : (j + 1) * SB]
                dv_p[j] = dv_p[j] + dvl[j * SB : (j + 1) * SB]
        dq_p[i] = dqi
        dk_p[i] = dk_p[i] + dki
        dv_p[i] = dv_p[i] + dvi
    return jnp.concatenate(dq_p, axis=0), jnp.concatenate(dk_p, axis=0), jnp.concatenate(dv_p, axis=0)


def _hgrn_fwd(proj, lb, gn, name):
    B, S, W = proj.shape
    HK = W // 4
    H = HK // HGRN_K
    C = min(HGRN_CHUNK, S)
    N = S // C

    HP = HGRN_PAR if H % HGRN_PAR == 0 else 1
    WP = HP * HGRN_K

    def body(q_ref, f_ref, i_ref, g_ref, lb_ref, gn_ref, og_ref, o_ref, st_ref):
        gn_v = gn_ref[...]
        tril = _tri(C, True)

        def chunk(n, sts):
            rows = pl.ds(pl.multiple_of(n * C, C), C)
            out = []
            for hh in range(HP):
                ln = slice(hh * HGRN_K, (hh + 1) * HGRN_K)
                st = sts[hh]
                qs, k, g = _hgrn_pre(q_ref[rows, ln], f_ref[rows, ln], lb_ref[:, ln])
                v = i_ref[rows, ln]
                b = _hdot(tril, g)
                st_ref[hh, n] = st
                o = _hgrn_intra_fwd(qs, k, v, b) + _bdot_nt(qs * jnp.exp(b), st)
                bl = b[C - 1 : C]
                out.append(st * jnp.exp(bl) + _bdot_tn(v, k * jnp.exp(bl - b)))
                o_ref[rows, ln] = o
                og_ref[rows, ln] = _hgrn_gate(o, g_ref[rows, ln], gn_v).astype(BF16)
            return tuple(out)

        lax.fori_loop(0, N, chunk, tuple(jnp.zeros((HGRN_K, HGRN_K), F32) for _ in range(HP)))

    col = lambda part: pl.BlockSpec((None, S, WP), lambda b, h: (b, 0, part * (H // HP) + h))
    return pl.pallas_call(
        body,
        grid=(B, H // HP),
        in_specs=[col(0), col(1), col(2), col(3), pl.BlockSpec((1, WP), lambda b, h: (0, h)), pl.BlockSpec((1, HGRN_K), lambda b, h: (0, 0))],
        out_specs=[col(0), col(0), pl.BlockSpec((None, HP, N, HGRN_K, HGRN_K), lambda b, h: (b, h, 0, 0, 0))],
        out_shape=[SDS((B, S, HK), BF16), SDS((B, S, HK), F32), SDS((B, H, N, HGRN_K, HGRN_K), F32)],
        name=name,
        compiler_params=_cparams("parallel", "parallel"),
    )(proj, proj, proj, proj, lb, gn)


def _hgrn_bwd(proj, lb, gn, o_pre, states, dog, name):
    B, S, W = proj.shape
    HK = W // 4
    H = HK // HGRN_K
    C = min(HGRN_CHUNK, S)
    N = S // C

    HP = HGRN_PAR if H % HGRN_PAR == 0 else 1
    WP = HP * HGRN_K

    def body(q_ref, f_ref, i_ref, g_ref, lb_ref, gn_ref, o_ref, st_ref, dog_ref, dq_ref, df_ref, di_ref, dg_ref, dlb_ref, dgn_ref):
        gn_v = gn_ref[...]
        tril = _tri(C, True)
        triu = _tri(C, False)

        def chunk(idx, carry):
            n = N - 1 - idx
            rows = pl.ds(pl.multiple_of(n * C, C), C)
            out = []
            for hh in range(HP):
                ln = slice(hh * HGRN_K, (hh + 1) * HGRN_K)
                dst, dlb, dgn = carry[hh]
                (qs, k, g), pre_vjp = jax.vjp(_hgrn_pre, q_ref[rows, ln], f_ref[rows, ln], lb_ref[:, ln])
                v = i_ref[rows, ln]
                _, gate_vjp = jax.vjp(_hgrn_gate, o_ref[rows, ln], g_ref[rows, ln], gn_v)
                do, dgg, dgn_c = gate_vjp(dog_ref[rows, ln])
                b = _hdot(tril, g)
                st0 = st_ref[hh, n]
                eb = jnp.exp(b)
                bl = b[C - 1 : C]
                ebl = jnp.exp(bl)
                ekb = jnp.exp(bl - b)
                qe = qs * eb
                kt = k * ekb
                dqs, dk, dv = _hgrn_intra_bwd(qs, k, v, b, do)
                dqs = dqs + _bdot(do, st0) * eb
                dk = dk + _bdot(v, dst) * ekb
                dv = dv + _bdot_nt(kt, dst)
                st1 = st0 * ebl + _bdot_tn(v, kt)
                dbl = jnp.sum(st1 * dst, axis=0, keepdims=True)
                dst = dst * ebl + _bdot_tn(do, qe)
                dgl = _hdot(triu, qs * dqs - k * dk) + dbl
                dq_pre, dfx, dlb_c = pre_vjp((dqs, dk, dgl))
                dq_ref[rows, ln] = dq_pre.astype(BF16)
                df_ref[rows, ln] = dfx.astype(BF16)
                di_ref[rows, ln] = dv.astype(BF16)
                dg_ref[rows, ln] = dgg.astype(BF16)
                out.append((dst, dlb + dlb_c, dgn + dgn_c))
            return tuple(out)

        zero = jnp.zeros((1, HGRN_K), F32)
        one = (jnp.zeros((HGRN_K, HGRN_K), F32), zero, zero)
        res = lax.fori_loop(0, N, chunk, tuple(one for _ in range(HP)))
        for hh in range(HP):
            dlb_ref[hh] = res[hh][1]
            dgn_ref[hh] = res[hh][2]

    col = lambda part: pl.BlockSpec((None, S, WP), lambda b, h: (b, 0, part * (H // HP) + h))
    vec = pl.BlockSpec((None, HP, 1, HGRN_K), lambda b, h: (b, h, 0, 0))
    return pl.pallas_call(
        body,
        grid=(B, H // HP),
        in_specs=[
            col(0), col(1), col(2), col(3),
            pl.BlockSpec((1, WP), lambda b, h: (0, h)),
            pl.BlockSpec((1, HGRN_K), lambda b, h: (0, 0)),
            col(0),
            pl.BlockSpec((None, HP, N, HGRN_K, HGRN_K), lambda b, h: (b, h, 0, 0, 0)),
            col(0),
        ],
        out_specs=[col(0), col(0), col(0), col(0), vec, vec],
        out_shape=[SDS((B, S, HK), BF16)] * 4 + [SDS((B, H, 1, HGRN_K), F32)] * 2,
        name=name,
        compiler_params=_cparams("parallel", "parallel"),
    )(proj, proj, proj, proj, lb, gn, o_pre, states, dog)


def _ada_fwd(c_all, w, b, name):
    Bg, D = c_all.shape
    L, _, n = w.shape

    def body(c_ref, w_ref, b_ref, o_ref):
        o_ref[...] = _bdot(jax.nn.silu(c_ref[...]), w_ref[...]) + b_ref[...]

    return pl.pallas_call(
        body,
        grid=(L,),
        in_specs=[
            pl.BlockSpec((Bg, D), lambda l: (0, 0)),
            pl.BlockSpec((None, D, n), lambda l: (l, 0, 0)),
            pl.BlockSpec((None, 1, n), lambda l: (l, 0, 0)),
        ],
        out_specs=pl.BlockSpec((None, Bg, n), lambda l: (l, 0, 0)),
        out_shape=SDS((L, Bg, n), F32),
        name=name,
        compiler_params=_cparams("parallel"),
    )(c_all, w, b)


def _ada_bwd(c_all, dmod, name):
    Bg, D = c_all.shape
    L, _, n = dmod.shape

    def body(c_ref, d_ref, dw_ref, db_ref):
        d = d_ref[...]
        dw_ref[...] = _bdot_tn(jax.nn.silu(c_ref[...]), d)
        db_ref[...] = jnp.sum(d, axis=0, keepdims=True)

    return pl.pallas_call(
        body,
        grid=(L,),
        in_specs=[pl.BlockSpec((Bg, D), lambda l: (0, 0)), pl.BlockSpec((None, Bg, n), lambda l: (l, 0, 0))],
        out_specs=[pl.BlockSpec((None, D, n), lambda l: (l, 0, 0)), pl.BlockSpec((None, 1, n), lambda l: (l, 0, 0))],
        out_shape=[SDS((L, D, n), F32), SDS((L, 1, n), F32)],
        name=name,
        compiler_params=_cparams("parallel"),
    )(c_all, dmod)


def _adamw(w, gs, m, v, name):
    shape = w.shape
    cols = shape[-1]
    rows = w.size // cols
    tr = rows
    for cand in (512, 256, 128, 64, 32, 16, 8):
        if rows % cand == 0 and cand * cols * 4 <= 2 * 1024 * 1024:
            tr = cand
            break
    as2d = lambda a: a.reshape(rows, cols)
    ng = len(gs)
    c1 = 1.0 / (1.0 - ADAM_B1**ADAM_STEP)
    c2 = 1.0 / (1.0 - ADAM_B2**ADAM_STEP)

    def body(*refs):
        w_ref, m_ref, v_ref = refs[0], refs[1], refs[2]
        g_refs = refs[3 : 3 + ng]
        g_out, d_out, m_out, v_out = refs[3 + ng :]
        g = g_refs[0][...].astype(F32)
        for r in g_refs[1:]:
            g = g + r[...].astype(F32)
        m_new = ADAM_B1 * m_ref[...] + (1.0 - ADAM_B1) * g
        v_new = ADAM_B2 * v_ref[...] + (1.0 - ADAM_B2) * jnp.square(g)
        g_out[...] = g
        m_out[...] = m_new
        v_out[...] = v_new
        d_out[...] = -ADAM_LR * ((m_new * c1) / (jnp.sqrt(v_new * c2) + ADAM_EPS) + ADAM_WD * w_ref[...])

    spec = pl.BlockSpec((tr, cols), lambda i: (i, 0))
    outs = pl.pallas_call(
        body,
        grid=(rows // tr,),
        in_specs=[spec] * (3 + ng),
        out_specs=[spec] * 4,
        out_shape=[SDS((rows, cols), F32)] * 4,
        name=name,
        compiler_params=_cparams("parallel"),
    )(as2d(w), as2d(m), as2d(v), *[as2d(g) for g in gs])
    return tuple(o.reshape(shape) for o in outs)


def _sum4(own, recv, name):
    shape = own.shape
    cols = shape[-1]
    rows = own.size // cols
    tr = rows
    for cand in (512, 256, 128, 64, 32, 16):
        if rows % cand == 0 and cand * cols * 4 <= 2 * 1024 * 1024:
            tr = cand
            break

    def body(own_ref, recv_ref, o_ref):
        acc = own_ref[...].astype(F32)
        for r in range(3):
            acc = acc + recv_ref[r].astype(F32)
        o_ref[...] = acc

    out = pl.pallas_call(
        body,
        grid=(rows // tr,),
        in_specs=[pl.BlockSpec((tr, cols), lambda i: (i, 0)), pl.BlockSpec((3, tr, cols), lambda i: (0, i, 0))],
        out_specs=pl.BlockSpec((tr, cols), lambda i: (i, 0)),
        out_shape=SDS((rows, cols), F32),
        name=name,
        compiler_params=_cparams("parallel"),
    )(own.reshape(rows, cols), recv.reshape(3, rows, cols))
    return out.reshape(shape)


def _my_place():
    return lax.axis_index("x"), lax.axis_index("y"), lax.axis_index("c")


def _flip(v, bit):
    return 1 - v if bit else v


def _allgather8(x, name):
    r, n = x.shape

    def body(x_ref, o_ref, send_sems, recv_sems, local_sem):
        mx, my, mc = _my_place()
        me = 4 * mx + 2 * my + mc
        mine = pltpu.make_async_copy(x_ref, o_ref.at[me], local_sem)
        mine.start()
        sends = []
        for rel in range(1, 8):
            peer = (_flip(mx, rel & 4), _flip(my, rel & 2), _flip(mc, rel & 1))
            cp = pltpu.make_async_remote_copy(
                src_ref=x_ref, dst_ref=o_ref.at[me], send_sem=send_sems.at[rel - 1], recv_sem=recv_sems.at[rel - 1],
                device_id=peer, device_id_type=MESH,
            )
            cp.start()
            sends.append(cp)
        for rel in range(1, 8):
            px, py, pc = _flip(mx, rel & 4), _flip(my, rel & 2), _flip(mc, rel & 1)
            pltpu.make_async_remote_copy(
                src_ref=x_ref, dst_ref=o_ref.at[4 * px + 2 * py + pc], send_sem=send_sems.at[rel - 1],
                recv_sem=recv_sems.at[rel - 1], device_id=(px, py, pc), device_id_type=MESH,
            ).wait_recv()
        for cp in sends:
            cp.wait_send()
        mine.wait()

    return pl.pallas_call(
        body,
        out_shape=SDS((8, r, n), x.dtype),
        in_specs=[pl.BlockSpec(memory_space=pl.ANY)],
        out_specs=pl.BlockSpec(memory_space=pl.ANY),
        scratch_shapes=[pltpu.SemaphoreType.DMA((7,)), pltpu.SemaphoreType.DMA((7,)), pltpu.SemaphoreType.DMA],
        name=name,
    )(x)


_HBM = pl.BlockSpec(memory_space=pl.ANY)


_SEM = pl.BlockSpec(memory_space=pltpu.SEMAPHORE)
_HBM_ONLY = pl.BlockSpec(memory_space=pltpu.HBM)
_EFFECT = pltpu.SideEffectType.DATAFLOW_SIDE_EFFECTING


def _in_hbm(a):
    return pltpu.with_memory_space_constraint(a, pltpu.HBM)


def _gather_start(lands, after, name):
    n = len(lands)

    def body(*refs):
        land = refs[:n]
        send_sems, recv_sems = refs[n + 1], refs[n + 2]
        token = refs[-1]
        mx, my, mc = _my_place()
        for i in range(n):
            for rel in range(1, 4):
                pltpu.make_async_remote_copy(
                    src_ref=land[i].at[2 * mx + my], dst_ref=land[i].at[2 * mx + my],
                    send_sem=send_sems.at[3 * i + rel - 1], recv_sem=recv_sems.at[3 * i + rel - 1],
                    device_id=(_flip(mx, rel & 2), _flip(my, rel & 1), mc), device_id_type=MESH,
                ).start()
        token[...] = jnp.zeros_like(token)

    outs = pl.pallas_call(
        body,
        name=name,
        out_shape=(
            pltpu.SemaphoreType.DMA((3 * n,)), pltpu.SemaphoreType.DMA((3 * n,)),
            *[pltpu.HBM(a.shape, a.dtype) for a in lands], SDS((8, LANES), F32),
        ),
        in_specs=[_HBM_ONLY] * n + [_HBM],
        out_specs=(_SEM, _SEM, *[_HBM_ONLY] * n, pl.BlockSpec(memory_space=pltpu.VMEM)),
        input_output_aliases={i: 2 + i for i in range(n)},
        compiler_params=pltpu.CompilerParams(has_side_effects=_EFFECT),
    )(*[_in_hbm(a) for a in lands], after)
    return outs[0], outs[1], list(outs[2 : 2 + n]), outs[-1]


def _gather_wait(send_sems, recv_sems, lands, after, name):
    n = len(lands)

    def body(*refs):
        land = refs[:n]
        s_sems, r_sems = refs[n], refs[n + 1]
        mx, my, mc = _my_place()
        for i in range(n):
            for rel in range(1, 4):
                px, py = _flip(mx, rel & 2), _flip(my, rel & 1)
                cp = pltpu.make_async_remote_copy(
                    src_ref=land[i].at[2 * mx + my], dst_ref=land[i].at[2 * px + py],
                    send_sem=s_sems.at[3 * i + rel - 1], recv_sem=r_sems.at[3 * i + rel - 1],
                    device_id=(px, py, mc), device_id_type=MESH,
                )
                cp.wait_send()
                cp.wait_recv()

    outs = pl.pallas_call(
        body,
        name=name,
        out_shape=tuple(pltpu.HBM(a.shape, a.dtype) for a in lands),
        in_specs=[_HBM_ONLY] * n + [_SEM, _SEM, _HBM],
        out_specs=[_HBM_ONLY] * n,
        input_output_aliases={i: i for i in range(n)},
        compiler_params=pltpu.CompilerParams(has_side_effects=_EFFECT),
    )(*lands, send_sems, recv_sems, after)
    return list(outs)


def _scatter_start(slabs, lands, places, name):
    n = len(slabs)

    def body(*refs):
        ins, land = refs[:n], refs[n : 2 * n]
        send_sems, recv_sems = refs[2 * n], refs[2 * n + 1]
        token = refs[-1]
        mx, my, mc = _my_place()
        for i in range(n):
            for rel in range(1, 4):
                px, py = _flip(mx, rel & 2), _flip(my, rel & 1)
                pltpu.make_async_remote_copy(
                    src_ref=ins[i].at[2 * px + py], dst_ref=land[i].at[rel - 1, places[i]],
                    send_sem=send_sems.at[3 * i + rel - 1], recv_sem=recv_sems.at[3 * i + rel - 1],
                    device_id=(px, py, mc), device_id_type=MESH,
                ).start()
        token[...] = jnp.zeros_like(token)

    outs = pl.pallas_call(
        body,
        name=name,
        out_shape=(
            pltpu.SemaphoreType.DMA((3 * n,)), pltpu.SemaphoreType.DMA((3 * n,)),
            *[pltpu.HBM(a.shape, a.dtype) for a in slabs], *[pltpu.HBM(a.shape, a.dtype) for a in lands],
            SDS((8, LANES), F32),
        ),
        in_specs=[_HBM_ONLY] * (2 * n),
        out_specs=(_SEM, _SEM, *[_HBM_ONLY] * (2 * n), pl.BlockSpec(memory_space=pltpu.VMEM)),
        input_output_aliases={i: 2 + i for i in range(2 * n)},
        compiler_params=pltpu.CompilerParams(has_side_effects=_EFFECT),
    )(*[_in_hbm(a) for a in slabs], *[_in_hbm(a) for a in lands])
    return outs[0], outs[1], list(outs[2 : 2 + n]), list(outs[2 + n : 2 + 2 * n]), outs[-1]


def _scatter_wait(send_sems, recv_sems, slabs, lands, places, after, name):
    n = len(slabs)

    def body(*refs):
        ins, land = refs[:n], refs[n : 2 * n]
        s_sems, r_sems = refs[2 * n], refs[2 * n + 1]
        mx, my, mc = _my_place()
        for i in range(n):
            for rel in range(1, 4):
                px, py = _flip(mx, rel & 2), _flip(my, rel & 1)
                cp = pltpu.make_async_remote_copy(
                    src_ref=ins[i].at[2 * px + py], dst_ref=land[i].at[rel - 1, places[i]],
                    send_sem=s_sems.at[3 * i + rel - 1], recv_sem=r_sems.at[3 * i + rel - 1],
                    device_id=(px, py, mc), device_id_type=MESH,
                )
                cp.wait_send()
                cp.wait_recv()

    outs = pl.pallas_call(
        body,
        name=name,
        out_shape=(*[pltpu.HBM(a.shape, a.dtype) for a in slabs], *[pltpu.HBM(a.shape, a.dtype) for a in lands]),
        in_specs=[_HBM_ONLY] * (2 * n) + [_SEM, _SEM, _HBM],
        out_specs=[_HBM_ONLY] * (2 * n),
        input_output_aliases={i: i for i in range(2 * n)},
        compiler_params=pltpu.CompilerParams(has_side_effects=_EFFECT),
    )(*slabs, *lands, send_sems, recv_sems, after)
    return list(outs[:n]), list(outs[n:])


def _swap_sibling(parts, name):
    n = len(parts)

    def body(*refs):
        ins, outs = refs[:n], refs[n : 2 * n]
        send_sems, recv_sems = refs[2 * n :]
        mx, my, mc = _my_place()
        sends = []
        for i in range(n):
            cp = pltpu.make_async_remote_copy(
                src_ref=ins[i], dst_ref=outs[i], send_sem=send_sems.at[i], recv_sem=recv_sems.at[i],
                device_id=(mx, my, 1 - mc), device_id_type=MESH,
            )
            cp.start()
            sends.append(cp)
        for cp in sends:
            cp.wait_recv()
        for cp in sends:
            cp.wait_send()

    return pl.pallas_call(
        body,
        out_shape=[SDS(s.shape, s.dtype) for s in parts],
        in_specs=[_HBM] * n,
        out_specs=[_HBM] * n,
        scratch_shapes=[pltpu.SemaphoreType.DMA((n,)), pltpu.SemaphoreType.DMA((n,))],
        name=name,
    )(*parts)


def _pad_rows(a, rows):
    return jnp.pad(a, ((0, rows - a.shape[0]), (0, 0)))


def kernel(x, c, positions, mla_w_in, mla_q_norm, mla_w_qb, mla_kv_norm, mla_w_kvb, mla_w_o, hgrn_lb, hgrn_w_in, hgrn_g_norm, hgrn_w_o, ffn_w_in, ffn_w_out, ada_w, ada_b, ln_g, ln_b, loss_target, m_mla_w_in, m_mla_q_norm, m_mla_w_qb, m_mla_kv_norm, m_mla_w_kvb, m_mla_w_o, m_hgrn_lb, m_hgrn_w_in, m_hgrn_g_norm, m_hgrn_w_o, m_ffn_w_in, m_ffn_w_out, m_ada_w, m_ada_b, m_ln_g, m_ln_b, v_mla_w_in, v_mla_q_norm, v_mla_w_qb, v_mla_kv_norm, v_mla_w_kvb, v_mla_w_o, v_hgrn_lb, v_hgrn_w_in, v_hgrn_g_norm, v_hgrn_w_o, v_ffn_w_in, v_ffn_w_out, v_ada_w, v_ada_b, v_ln_g, v_ln_b):
    B, S, D = x.shape
    T = B * S
    depth = ada_w.shape[0]
    n_mla, n_hgrn = mla_w_in.shape[0], hgrn_w_in.shape[0]
    n_sub = 2 * depth
    alpha = (2.0 * depth) ** 0.25
    mx, my, mc = _my_place()
    me = 4 * mx + 2 * my + mc
    k_me = 2 * mx + my
    Bg = 8 * B
    HK = hgrn_w_o.shape[1] * 4
    dq = D // 4

    lbw = hgrn_lb.shape[1]
    first = jnp.zeros((8, max(D, 4 * lbw)), F32)
    first = first.at[:B, :D].set(c).at[B : B + n_hgrn, :lbw].set(hgrn_lb)
    first_all = _allgather8(first, "gather_cond")
    c_all = first_all[:, :B, :D].reshape(Bg, D)
    lb_logits = jnp.concatenate([first_all[2 * k, B : B + n_hgrn, :lbw] for k in range(4)], axis=1)

    def lower_bounds_fn(logits):
        soft = jax.nn.softmax(logits, axis=0)
        return jnp.cumsum(soft, axis=0) - soft[0]

    lower_bounds, lower_bounds_vjp = jax.vjp(lower_bounds_fn, lb_logits)

    n_ada = ada_w.shape[-1]
    mod_part = _ada_fwd(c_all, ada_w.reshape(n_sub, D, n_ada), ada_b.reshape(n_sub, 1, n_ada), "ada_fwd")
    mod_all = _allgather8(mod_part.reshape(n_sub * Bg, n_ada), "gather_mod").reshape(8, n_sub, Bg, n_ada)
    mod = jnp.concatenate([mod_all[2 * k] for k in range(4)], axis=-1)
    mod = lax.dynamic_slice_in_dim(mod, me * B, B, axis=1)
    shift = [mod[j, :, None, :D] for j in range(n_sub)]
    scale = [mod[j, :, None, D : 2 * D] for j in range(n_sub)]
    gate = [mod[j, :, None, 2 * D :] for j in range(n_sub)]

    ln_rows = 2 * n_sub
    ln_local = _pad_rows(jnp.concatenate([ln_g.reshape(n_sub, dq), ln_b.reshape(n_sub, dq)], axis=0), -(-ln_rows // 8) * 8)
    ln_pad = jnp.zeros((ln_local.shape[0], -(-dq // LANES) * LANES), F32).at[:, :dq].set(ln_local)
    ln_all = _allgather8(ln_pad, "gather_ln")
    ln_full = jnp.concatenate([ln_all[2 * k, :ln_rows, :dq] for k in range(4)], axis=1)
    lng = [ln_full[j][None, :] for j in range(n_sub)]
    lnb = [ln_full[n_sub + j][None, :] for j in range(n_sub)]

    main = dict(mla_w_in=mla_w_in, mla_w_qb=mla_w_qb, mla_w_kvb=mla_w_kvb, mla_w_o=mla_w_o, hgrn_w_in=hgrn_w_in,
                hgrn_w_o=hgrn_w_o, ffn_w_in=ffn_w_in, ffn_w_out=ffn_w_out)
    names = list(main)

    def group_kinds(layer, part):
        if part:
            return [("ffn_w_in", layer), ("ffn_w_out", layer)]
        mixer = ["mla_w_in", "mla_w_qb", "mla_w_kvb", "mla_w_o"] if layer % 2 == 0 else ["hgrn_w_in", "hgrn_w_o"]
        return [(k, layer // 2) for k in mixer]

    gathers = {}
    after = first
    for layer in range(depth):
        for part in range(2):
            lands = [lax.dynamic_update_index_in_dim(lax.empty((4,) + main[k].shape[1:], BF16), main[k][i].astype(BF16), k_me, 0)
                     for k, i in group_kinds(layer, part)]
            ssem, rsem, lands, after = _gather_start(lands, after, f"gather_start_l{layer}p{part}")
            gathers[layer, part] = (ssem, rsem, lands)
    scale[0] = scale[0] + after[0, 0]

    def row_w(g):
        return g.reshape(1, g.shape[0] * g.shape[1], g.shape[2])

    def full_w_in(g):
        return jnp.transpose(g, (1, 0, 2)).reshape(1, g.shape[1], 4 * g.shape[2])

    ang = positions.astype(F32)[..., None] * (ROPE_THETA ** (-jnp.arange(0, QK_ROPE, 2, dtype=F32) / QK_ROPE))
    cos, sin = jnp.cos(ang), jnp.sin(ang)

    gq = [mla_q_norm[j][None, :] for j in range(n_mla)]
    gkv = [mla_kv_norm[j][None, :] for j in range(n_mla)]
    gn = [hgrn_g_norm[j][None, :] for j in range(n_hgrn)]

    def r2(a):
        return a.reshape(T, a.shape[-1])

    def r3(a):
        return a.reshape(B, S, a.shape[-1])

    saved = []
    xs = x
    for layer in range(depth):
        j = layer // 2
        sub = 2 * layer
        tag = f"l{layer}"
        ssem, rsem, lands = gathers[layer, 0]
        lands = _gather_wait(ssem, rsem, lands, xs if layer else scale[0], f"gather_wait_{tag}p0")
        wl = {k: g for (k, _), g in zip(group_kinds(layer, 0), lands)}
        h = _modulate(xs, scale[sub], shift[sub], f"mod_{tag}a")
        if layer % 2 == 0:
            wl["mla_w_in"] = full_w_in(wl["mla_w_in"])
            proj = r3(_mm_nn(r2(h), wl["mla_w_in"], F32, f"mla_in_{tag}"))
            qn, kvn = _mla_mid_fwd(proj, gq[j], gkv[j], f"mla_mid_{tag}")
            q = r3(_mm_nn(r2(qn), wl["mla_w_qb"], F32, f"mla_qb_{tag}"))
            kv = r3(_mm_nn(r2(kvn), wl["mla_w_kvb"], F32, f"mla_kvb_{tag}"))
            qh, kh, vh = _mla_prep_fwd(q, kv, proj, cos, sin, f"mla_prep_{tag}")
            o, lse = _attn_fwd(qh, kh, vh, f"attn_{tag}")
            wl["mla_w_o"] = row_w(wl["mla_w_o"])
            y = r3(_mm_nn(r2(o), wl["mla_w_o"], F32, f"mla_o_{tag}"))
            mix = (h, proj, qn, kvn, qh, kh, vh, o, lse)
        else:
            proj = r3(_mm_nn(r2(h), wl["hgrn_w_in"], F32, f"hgrn_in_{tag}"))
            og, o_pre, states = _hgrn_fwd(proj, lower_bounds[j][None, :], gn[j], f"hgrn_{tag}")
            wl["hgrn_w_o"] = row_w(wl["hgrn_w_o"])
            y = r3(_mm_nn(r2(og), wl["hgrn_w_o"], F32, f"hgrn_o_{tag}"))
            mix = (h, proj, og, o_pre, states)
        x1 = _ln_fwd(alpha, xs, y, gate[sub], lng[sub], lnb[sub], f"ln_{tag}a")
        ssem, rsem, lands = gathers[layer, 1]
        lands = _gather_wait(ssem, rsem, lands, x1, f"gather_wait_{tag}p1")
        wl.update({k: g for (k, _), g in zip(group_kinds(layer, 1), lands)})
        h2 = _modulate(x1, scale[sub + 1], shift[sub + 1], f"mod_{tag}b")
        u = r3(_mm_nn(r2(h2), wl["ffn_w_in"], F32, f"ffn_in_{tag}"))
        a = _swiglu_fwd(u, f"swiglu_{tag}")
        wl["ffn_w_out"] = row_w(wl["ffn_w_out"])
        y2 = r3(_mm_nn(r2(a), wl["ffn_w_out"], F32, f"ffn_out_{tag}"))
        x2 = _ln_fwd(alpha, x1, y2, gate[sub + 1], lng[sub + 1], lnb[sub + 1], f"ln_{tag}b")
        saved.append((xs, y, x1, y2, mix, h2, u, a, wl))
        xs = x2

    loss_local, dout = _loss_head(xs, loss_target, "loss_head")
    loss = lax.psum(loss_local, ("x", "y", "c"))

    gw = {k: [None] * main[k].shape[0] for k in names}
    land = {k: lax.empty((3,) + main[k].shape, BF16) for k in names}
    scatters = []
    d_shift, d_scale, d_gate = [None] * n_sub, [None] * n_sub, [None] * n_sub
    d_lng, d_lnb = [None] * n_sub, [None] * n_sub
    d_gq, d_gkv, d_gn, d_lbnd = [None] * n_mla, [None] * n_mla, [None] * n_hgrn, [None] * n_hgrn

    def rows4(g):
        return g.reshape(4, g.shape[1] // 4, g.shape[2])

    def start_scatter(layer, part, token_to):
        kinds = group_kinds(layer, part)
        ssem, rsem, slabs_t, lands_t, token = _scatter_start(
            [gw[k][i] for k, i in kinds], [land[k] for k, _ in kinds], [i for _, i in kinds], f"scatter_start_l{layer}p{part}")
        for (k, i), s_t, l_t in zip(kinds, slabs_t, lands_t):
            gw[k][i], land[k] = s_t, l_t
        scatters.append((layer, part, ssem, rsem))
        if token_to is not None:
            gate[token_to] = gate[token_to] + token[0, 0]

    for layer in reversed(range(depth)):
        j = layer // 2
        sub = 2 * layer
        tag = f"l{layer}"
        xs, y, x1, y2, mix, h2, u, a, wl = saved[layer]
        dxr, dy2, d_gate[sub + 1], d_lng[sub + 1], d_lnb[sub + 1] = _ln_bwd(
            alpha, dout, x1, y2, gate[sub + 1], lng[sub + 1], lnb[sub + 1], f"ln_bwd_{tag}b")
        da = r3(_mm_nt(r2(dy2), wl["ffn_w_out"], F32, f"ffn_out_dx_{tag}"))
        gw["ffn_w_out"][layer] = rows4(_mm_tn(r2(a), r2(dy2), 1, BF16, f"ffn_out_dw_{tag}"))
        du = _swiglu_bwd(u, da, f"swiglu_bwd_{tag}")
        dh2 = r3(_mm_nt(r2(du), wl["ffn_w_in"], F32, f"ffn_in_dx_{tag}"))
        gw["ffn_w_in"][layer] = _mm_tn(r2(h2), r2(du), 4, BF16, f"ffn_in_dw_{tag}")
        start_scatter(layer, 1, sub)
        dout, d_scale[sub + 1], d_shift[sub + 1] = _mod_bwd(dh2, dxr, x1, scale[sub + 1], f"mod_bwd_{tag}b")
        dxr, dy, d_gate[sub], d_lng[sub], d_lnb[sub] = _ln_bwd(
            alpha, dout, xs, y, gate[sub], lng[sub], lnb[sub], f"ln_bwd_{tag}a")
        if layer % 2 == 0:
            h, proj, qn, kvn, qh, kh, vh, o, lse = mix
            do = r3(_mm_nt(r2(dy), wl["mla_w_o"], BF16, f"mla_o_dx_{tag}"))
            gw["mla_w_o"][j] = rows4(_mm_tn(r2(o), r2(dy), 1, BF16, f"mla_o_dw_{tag}"))
            dqh, dkh, dvh = _attn_bwd(qh, kh, vh, o, do, lse, f"attn_bwd_{tag}")
            dq_, dkv_, dkr = _mla_prep_bwd(dqh, dkh, dvh, cos, sin, f"mla_prep_bwd_{tag}")
            dqn = r3(_mm_nt(r2(dq_), wl["mla_w_qb"], F32, f"mla_qb_dx_{tag}"))
            gw["mla_w_qb"][j] = _mm_tn(r2(qn), r2(dq_), 4, BF16, f"mla_qb_dw_{tag}")
            dkvn = r3(_mm_nt(r2(dkv_), wl["mla_w_kvb"], F32, f"mla_kvb_dx_{tag}"))
            gw["mla_w_kvb"][j] = _mm_tn(r2(kvn), r2(dkv_), 4, BF16, f"mla_kvb_dw_{tag}")
            dproj, dgq_, dgkv_ = _mla_mid_bwd(proj, dqn, dkvn, dkr, gq[j], gkv[j], f"mla_mid_bwd_{tag}")
            d_gq[j], d_gkv[j] = dgq_.sum(0), dgkv_.sum(0)
            dh = r3(_mm_nt(r2(dproj), wl["mla_w_in"], F32, f"mla_in_dx_{tag}"))
            gwin = _mm_tn(r2(h), r2(dproj), 1, BF16, f"mla_in_dw_{tag}")[0]
            gw["mla_w_in"][j] = jnp.transpose(gwin.reshape(gwin.shape[0], 4, gwin.shape[1] // 4), (1, 0, 2))
        else:
            h, proj, og, o_pre, states = mix
            dog = r3(_mm_nt(r2(dy), wl["hgrn_w_o"], F32, f"hgrn_o_dx_{tag}"))
            gw["hgrn_w_o"][j] = rows4(_mm_tn(r2(og), r2(dy), 1, BF16, f"hgrn_o_dw_{tag}"))
            dq_, df_, di_, dg_, dlb_, dgn_ = _hgrn_bwd(proj, lower_bounds[j][None, :], gn[j], o_pre, states, dog, f"hgrn_bwd_{tag}")
            dproj = jnp.concatenate([dq_, df_, di_, dg_], axis=-1)
            d_lbnd[j] = dlb_.sum(0).reshape(1, HK)
            d_gn[j] = dgn_.sum((0, 1))
            dh = r3(_mm_nt(r2(dproj), wl["hgrn_w_in"], F32, f"hgrn_in_dx_{tag}"))
            gw["hgrn_w_in"][j] = _mm_tn(r2(h), r2(dproj), 4, BF16, f"hgrn_in_dw_{tag}")
        start_scatter(layer, 0, sub - 1 if layer else None)
        dout, d_scale[sub], d_shift[sub] = _mod_bwd(dh, dxr, xs, scale[sub], f"mod_bwd_{tag}a")
    grad_x = dout

    for layer, part, ssem, rsem in scatters:
        kinds = group_kinds(layer, part)
        slabs_t, lands_t = _scatter_wait(
            ssem, rsem, [gw[k][i] for k, i in kinds], [land[k] for k, _ in kinds], [i for _, i in kinds], grad_x,
            f"scatter_wait_l{layer}p{part}")
        for (k, i), s_t, l_t in zip(kinds, slabs_t, lands_t):
            gw[k][i], land[k] = s_t, l_t
    sums = [_sum4(jnp.stack([lax.dynamic_index_in_dim(g, k_me, 0, keepdims=False) for g in gw[k]]), land[k], f"sum4_{k}")
            for k in names]
    others = _swap_sibling(sums, "swap_sums")
    g_main = {k: (a_, b_) for k, a_, b_ in zip(names, sums, others)}

    dmod = jnp.stack([jnp.concatenate([d_shift[s_][:, 0], d_scale[s_][:, 0], d_gate[s_][:, 0]], axis=-1) for s_ in range(n_sub)])
    dmod_rows = _pad_rows(dmod.reshape(n_sub * B, 3 * D), -(-n_sub * B // 8) * 8)
    dmod_all = _allgather8(dmod_rows, "gather_dmod")[:, : n_sub * B].reshape(8, n_sub, B, 3 * D)
    dmod_all = jnp.transpose(dmod_all, (1, 0, 2, 3)).reshape(n_sub, Bg, 3 * D)
    dmod_mine = lax.dynamic_slice_in_dim(dmod_all, k_me * n_ada, n_ada, axis=2)
    g_ada_w, g_ada_b = _ada_bwd(c_all, dmod_mine, "ada_bwd")
    g_ada_w = g_ada_w.reshape(ada_w.shape)
    g_ada_b = g_ada_b.reshape(ada_b.shape)

    small = [jnp.stack(d_gq).reshape(-1), jnp.stack(d_gkv).reshape(-1), jnp.stack(d_gn).reshape(-1),
             jnp.stack(d_lbnd).reshape(-1), jnp.stack([d.sum(0) for d in d_lng]).reshape(-1),
             jnp.stack([d.sum(0) for d in d_lnb]).reshape(-1)]
    sizes = [s_.shape[0] for s_ in small]
    flat = jnp.concatenate(small)
    rows_small = -(-flat.shape[0] // (8 * LANES)) * 8
    flat = jnp.pad(flat, (0, rows_small * LANES - flat.shape[0])).reshape(rows_small, LANES)
    tot = _allgather8(flat, "gather_small")
    acc = tot[0]
    for d in range(1, 8):
        acc = acc + tot[d]
    acc = acc.reshape(-1)
    offs = [0]
    for s_ in sizes:
        offs.append(offs[-1] + s_)
    g_q_norm = acc[offs[0] : offs[1]].reshape(mla_q_norm.shape)
    g_kv_norm = acc[offs[1] : offs[2]].reshape(mla_kv_norm.shape)
    g_g_norm = acc[offs[2] : offs[3]].reshape(hgrn_g_norm.shape)
    g_lbnd = acc[offs[3] : offs[4]].reshape(n_hgrn, HK)
    g_lb_full = lower_bounds_vjp(g_lbnd)[0]
    g_hgrn_lb = lax.dynamic_slice_in_dim(g_lb_full, k_me * lbw, lbw, axis=1)
    g_lng = lax.dynamic_slice_in_dim(acc[offs[4] : offs[5]].reshape(n_sub, D), k_me * dq, dq, axis=1).reshape(ln_g.shape)
    g_lnb = lax.dynamic_slice_in_dim(acc[offs[5] : offs[6]].reshape(n_sub, D), k_me * dq, dq, axis=1).reshape(ln_b.shape)

    weights = dict(mla_w_in=mla_w_in, mla_q_norm=mla_q_norm, mla_w_qb=mla_w_qb, mla_kv_norm=mla_kv_norm, mla_w_kvb=mla_w_kvb,
                   mla_w_o=mla_w_o, hgrn_lb=hgrn_lb, hgrn_w_in=hgrn_w_in, hgrn_g_norm=hgrn_g_norm, hgrn_w_o=hgrn_w_o,
                   ffn_w_in=ffn_w_in, ffn_w_out=ffn_w_out, ada_w=ada_w, ada_b=ada_b, ln_g=ln_g, ln_b=ln_b)
    moms = dict(mla_w_in=(m_mla_w_in, v_mla_w_in), mla_q_norm=(m_mla_q_norm, v_mla_q_norm), mla_w_qb=(m_mla_w_qb, v_mla_w_qb),
                mla_kv_norm=(m_mla_kv_norm, v_mla_kv_norm), mla_w_kvb=(m_mla_w_kvb, v_mla_w_kvb), mla_w_o=(m_mla_w_o, v_mla_w_o),
                hgrn_lb=(m_hgrn_lb, v_hgrn_lb), hgrn_w_in=(m_hgrn_w_in, v_hgrn_w_in), hgrn_g_norm=(m_hgrn_g_norm, v_hgrn_g_norm),
                hgrn_w_o=(m_hgrn_w_o, v_hgrn_w_o), ffn_w_in=(m_ffn_w_in, v_ffn_w_in), ffn_w_out=(m_ffn_w_out, v_ffn_w_out),
                ada_w=(m_ada_w, v_ada_w), ada_b=(m_ada_b, v_ada_b), ln_g=(m_ln_g, v_ln_g), ln_b=(m_ln_b, v_ln_b))
    grads = dict(mla_q_norm=(g_q_norm,), mla_kv_norm=(g_kv_norm,), hgrn_lb=(g_hgrn_lb,), hgrn_g_norm=(g_g_norm,),
                 ada_w=(g_ada_w,), ada_b=(g_ada_b,), ln_g=(g_lng,), ln_b=(g_lnb,), **g_main)
    res = {k: _adamw(weights[k], [g_.reshape(weights[k].shape) for g_ in grads[k]], moms[k][0], moms[k][1], f"adamw_{k}")
           for k in weights}
    order = list(weights)
    return (loss, grad_x, *[res[k][0] for k in order], *[res[k][1] for k in order], *[res[k][2] for k in order],
            *[res[k][3] for k in order])
```

```python
import functools

import jax
import jax.numpy as jnp
from jax import lax
from jax.experimental import pallas as pl
from jax.experimental.pallas import tpu as pltpu

F32 = jnp.float32
BF16 = jnp.bfloat16
SDS = jax.ShapeDtypeStruct
MESH = pl.DeviceIdType.MESH
HI = lax.Precision.HIGHEST
MID = lax.Precision.HIGH

MLA_HEADS, QK_NOPE, QK_ROPE, V_HEAD = 16, 64, 32, 64
Q_LORA, KV_LORA = 768, 256
QK_DIM = QK_NOPE + QK_ROPE
ROPE_THETA = 10000.0
HGRN_K = 128
HGRN_CHUNK = 64
HGRN_SUB = 16
HGRN_PAR = 2
LN_EPS, RMS_EPS = 1e-5, 1e-6
ADAM_LR, ADAM_B1, ADAM_B2, ADAM_EPS, ADAM_WD, ADAM_STEP = 0.001, 0.9, 0.999, 1e-08, 0.01, 10
NEG = -1e30

VMEM_LIMIT_BYTES = 56 * 1024 * 1024
LANES = 128


def _cparams(*sem):
    return pltpu.CompilerParams(dimension_semantics=sem if sem else None, vmem_limit_bytes=VMEM_LIMIT_BYTES)


def _pick_tile(n, cap):
    best = 0
    for t in range(LANES, min(n, cap) + 1, LANES):
        if n % t == 0:
            best = t
    return best if best else n


def _bdot(a, b):
    return jnp.dot(a.astype(BF16), b.astype(BF16), preferred_element_type=F32)


def _bdot_nt(a, b):
    return lax.dot_general(a.astype(BF16), b.astype(BF16), (((1,), (1,)), ((), ())), preferred_element_type=F32)


def _bdot_tn(a, b):
    return lax.dot_general(a.astype(BF16), b.astype(BF16), (((0,), (0,)), ((), ())), preferred_element_type=F32)


def _hdot(a, b):
    return jnp.dot(a, b, precision=HI, preferred_element_type=F32)


def _mdot(a, b):
    return jnp.dot(a, b, precision=MID, preferred_element_type=F32)


def _mdot_nt(a, b):
    return lax.dot_general(a, b, (((1,), (1,)), ((), ())), precision=MID, preferred_element_type=F32)


def _mdot_tn(a, b):
    return lax.dot_general(a, b, (((0,), (0,)), ((), ())), precision=MID, preferred_element_type=F32)


def _mm_nn(a, w, out_dtype, name):
    M, K = a.shape
    G, _, n = w.shape
    tm = min(512, M)
    tn = _pick_tile(n, 1536)
    nps = n // tn

    def body(a_ref, w_ref, o_ref):
        o_ref[...] = _bdot(a_ref[...], w_ref[...]).astype(o_ref.dtype)

    return pl.pallas_call(
        body,
        grid=(G * nps, M // tm),
        in_specs=[
            pl.BlockSpec((tm, K), lambda j, i: (i, 0)),
            pl.BlockSpec((None, K, tn), lambda j, i: (j // nps, 0, j % nps)),
        ],
        out_specs=pl.BlockSpec((tm, tn), lambda j, i: (i, j)),
        out_shape=SDS((M, G * n), out_dtype),
        name=name,
        compiler_params=_cparams("parallel", "parallel"),
    )(a, w)


def _mm_nt(a, w, out_dtype, name):
    M = a.shape[0]
    G, K, n = w.shape
    tm = min(512, M)
    tk = _pick_tile(K, 1536)

    def body(a_ref, w_ref, o_ref, acc_ref):
        s = pl.program_id(2)

        @pl.when(s == 0)
        def _():
            acc_ref[...] = jnp.zeros_like(acc_ref)

        acc_ref[...] += _bdot_nt(a_ref[...], w_ref[...])

        @pl.when(s == G - 1)
        def _():
            o_ref[...] = acc_ref[...].astype(o_ref.dtype)

    return pl.pallas_call(
        body,
        grid=(K // tk, M // tm, G),
        in_specs=[
            pl.BlockSpec((tm, n), lambda kb, i, s: (i, s)),
            pl.BlockSpec((None, tk, n), lambda kb, i, s: (s, kb, 0)),
        ],
        out_specs=pl.BlockSpec((tm, tk), lambda kb, i, s: (i, kb)),
        out_shape=SDS((M, K), out_dtype),
        scratch_shapes=[pltpu.VMEM((tm, tk), F32)],
        name=name,
        compiler_params=_cparams("parallel", "parallel", "arbitrary"),
    )(a, w)


def _mm_tn(a, d, G, out_dtype, name):
    T, K = a.shape
    n = d.shape[1] // G
    tk = _pick_tile(K, 256)
    tn = _pick_tile(n, 1536)
    nps = n // tn

    def body(a_ref, d_ref, o_ref):
        o_ref[...] = _bdot_tn(a_ref[...], d_ref[...]).astype(o_ref.dtype)

    return pl.pallas_call(
        body,
        grid=(G * nps, K // tk),
        in_specs=[
            pl.BlockSpec((T, tk), lambda j, i: (0, i)),
            pl.BlockSpec((T, tn), lambda j, i: (0, j)),
        ],
        out_specs=pl.BlockSpec((None, tk, tn), lambda j, i: (j // nps, i, j % nps)),
        out_shape=SDS((G, K, n), out_dtype),
        name=name,
        compiler_params=_cparams("parallel", "parallel"),
    )(a, d)


def _rows_call(body, name, B, S, ins, outs, ts=256):
    ts = min(ts, S)
    in_specs, args = [], []
    for arr, kind in ins:
        W = arr.shape[-1]
        if kind == "row":
            in_specs.append(pl.BlockSpec((None, ts, W), lambda b, s: (b, s, 0)))
        elif kind == "ex":
            in_specs.append(pl.BlockSpec((None, 1, W), lambda b, s: (b, 0, 0)))
        else:
            in_specs.append(pl.BlockSpec((1, W), lambda b, s: (0, 0)))
        args.append(arr)
    out_specs, out_shape = [], []
    for W, dt, kind in outs:
        if kind == "row":
            out_specs.append(pl.BlockSpec((None, ts, W), lambda b, s: (b, s, 0)))
            out_shape.append(SDS((B, S, W), dt))
        else:
            out_specs.append(pl.BlockSpec((None, 1, W), lambda b, s: (b, 0, 0)))
            out_shape.append(SDS((B, 1, W), dt))
    return pl.pallas_call(
        body,
        grid=(B, S // ts),
        in_specs=in_specs,
        out_specs=out_specs,
        out_shape=out_shape,
        name=name,
        compiler_params=_cparams("parallel", "arbitrary"),
    )(*args)


def _acc(ref, val):
    @pl.when(pl.program_id(1) == 0)
    def _():
        ref[...] = jnp.zeros_like(ref)

    ref[...] += val


def _mod_fn(x, sc, sh):
    return x * (1.0 + sc) + sh


def _ln_fn(alpha, x, y, gate, g, b):
    z = alpha * x + (1.0 + gate) * y
    mu = jnp.mean(z, -1, keepdims=True)
    var = jnp.mean(jnp.square(z - mu), -1, keepdims=True)
    return (z - mu) * lax.rsqrt(var + LN_EPS) * g + b


def _modulate(x, sc, sh, name):
    B, S, D = x.shape

    def body(x_ref, sc_ref, sh_ref, h_ref):
        h_ref[...] = _mod_fn(x_ref[...], sc_ref[...], sh_ref[...]).astype(BF16)

    return _rows_call(body, name, B, S, [(x, "row"), (sc, "ex"), (sh, "ex")], [(D, BF16, "row")])[0]


def _ln_fwd(alpha, x, y, gate, g, b, name):
    B, S, D = x.shape

    def body(x_ref, y_ref, gate_ref, g_ref, b_ref, o_ref):
        o_ref[...] = _ln_fn(alpha, x_ref[...], y_ref[...], gate_ref[...], g_ref[...], b_ref[...])

    return _rows_call(
        body, name, B, S, [(x, "row"), (y, "row"), (gate, "ex"), (g, "par"), (b, "par")], [(D, F32, "row")]
    )[0]


def _ln_bwd(alpha, dout, x, y, gate, g, b, name):
    B, S, D = x.shape

    def body(do_ref, x_ref, y_ref, gate_ref, g_ref, b_ref, dxr_ref, dy_ref, dgate_ref, dg_ref, db_ref):
        _, vjp = jax.vjp(
            functools.partial(_ln_fn, alpha), x_ref[...], y_ref[...], gate_ref[...], g_ref[...], b_ref[...]
        )
        dx, dy, dgate, dg, db = vjp(do_ref[...])
        dxr_ref[...] = dx
        dy_ref[...] = dy.astype(BF16)
        _acc(dgate_ref, dgate)
        _acc(dg_ref, dg)
        _acc(db_ref, db)

    return _rows_call(
        body,
        name,
        B,
        S,
        [(dout, "row"), (x, "row"), (y, "row"), (gate, "ex"), (g, "par"), (b, "par")],
        [(D, F32, "row"), (D, BF16, "row"), (D, F32, "acc"), (D, F32, "acc"), (D, F32, "acc")],
    )


def _mod_bwd(dh, dxr, x, sc, name):
    B, S, D = x.shape

    def body(dh_ref, dxr_ref, x_ref, sc_ref, dx_ref, dsc_ref, dsh_ref):
        dh_v = dh_ref[...]
        dx_ref[...] = dxr_ref[...] + dh_v * (1.0 + sc_ref[...])
        _acc(dsc_ref, jnp.sum(dh_v * x_ref[...], axis=0, keepdims=True))
        _acc(dsh_ref, jnp.sum(dh_v, axis=0, keepdims=True))

    return _rows_call(
        body,
        name,
        B,
        S,
        [(dh, "row"), (dxr, "row"), (x, "row"), (sc, "ex")],
        [(D, F32, "row"), (D, F32, "acc"), (D, F32, "acc")],
    )


def _loss_head(y, target, name):
    B, S, D = y.shape

    def body(y_ref, t_ref, l_ref, dy_ref):
        e = y_ref[...] - t_ref[...]
        dy_ref[...] = e * (1.0 / D)
        part = 0.5 * jnp.sum(jnp.sum(e * e, axis=1, keepdims=True) * (1.0 / D), axis=0, keepdims=True)
        _acc(l_ref, jnp.broadcast_to(part, (1, LANES)))

    loss, dy = _rows_call(
        body, name, B, S, [(y, "row"), (target, "row")], [(LANES, F32, "acc"), (D, F32, "row")]
    )
    return jnp.sum(loss[:, 0, 0]), dy


def _swiglu_fn(u):
    F = u.shape[-1] // 2
    return jax.nn.silu(u[:, :F]) * u[:, F:]


def _swiglu_fwd(u, name):
    B, S, F2 = u.shape

    def body(u_ref, a_ref):
        a_ref[...] = _swiglu_fn(u_ref[...]).astype(BF16)

    return _rows_call(body, name, B, S, [(u, "row")], [(F2 // 2, BF16, "row")])[0]


def _swiglu_bwd(u, da, name):
    B, S, F2 = u.shape

    def body(u_ref, da_ref, du_ref):
        _, vjp = jax.vjp(_swiglu_fn, u_ref[...])
        du_ref[...] = vjp(da_ref[...])[0].astype(BF16)

    return _rows_call(body, name, B, S, [(u, "row"), (da, "row")], [(F2, BF16, "row")])[0]


def _rms_fn(x, g):
    return x * lax.rsqrt(jnp.mean(jnp.square(x), -1, keepdims=True) + RMS_EPS) * g


def _mla_mid_fwd(proj, gq, gkv, name):
    B, S, _ = proj.shape

    def body(p_ref, gq_ref, gkv_ref, qn_ref, kvn_ref):
        p = p_ref[...]
        qn_ref[...] = _rms_fn(p[:, :Q_LORA], gq_ref[...]).astype(BF16)
        kvn_ref[...] = _rms_fn(p[:, Q_LORA : Q_LORA + KV_LORA], gkv_ref[...]).astype(BF16)

    return _rows_call(
        body, name, B, S, [(proj, "row"), (gq, "par"), (gkv, "par")], [(Q_LORA, BF16, "row"), (KV_LORA, BF16, "row")]
    )


def _mla_mid_bwd(proj, dqn, dkvn, dkr, gq, gkv, name):
    B, S, W = proj.shape

    def body(p_ref, dqn_ref, dkvn_ref, dkr_ref, gq_ref, gkv_ref, dp_ref, dgq_ref, dgkv_ref):
        p = p_ref[...]
        _, vq = jax.vjp(_rms_fn, p[:, :Q_LORA], gq_ref[...])
        dql, dgq = vq(dqn_ref[...])
        _, vkv = jax.vjp(_rms_fn, p[:, Q_LORA : Q_LORA + KV_LORA], gkv_ref[...])
        dkvl, dgkv = vkv(dkvn_ref[...])
        dp_ref[:, :Q_LORA] = dql.astype(BF16)
        dp_ref[:, Q_LORA : Q_LORA + KV_LORA] = dkvl.astype(BF16)
        dp_ref[:, Q_LORA + KV_LORA :] = dkr_ref[...].astype(BF16)
        _acc(dgq_ref, dgq)
        _acc(dgkv_ref, dgkv)

    return _rows_call(
        body,
        name,
        B,
        S,
        [(proj, "row"), (dqn, "row"), (dkvn, "row"), (dkr, "row"), (gq, "par"), (gkv, "par")],
        [(W, BF16, "row"), (Q_LORA, F32, "acc"), (KV_LORA, F32, "acc")],
    )


def _rope(x, cos, sin):
    h = QK_ROPE // 2
    x1, x2 = x[:, :h], x[:, h:]
    return jnp.concatenate([x1 * cos - x2 * sin, x1 * sin + x2 * cos], axis=1)


def _rope_t(dy, cos, sin):
    h = QK_ROPE // 2
    d1, d2 = dy[:, :h], dy[:, h:]
    return jnp.concatenate([d1 * cos + d2 * sin, d2 * cos - d1 * sin], axis=1)


def _heads_call(body, name, B, S, ins, outs, ts=256):
    ts = min(ts, S)
    in_specs, args = [], []
    for arr, kind in ins:
        if kind == "row":
            in_specs.append(pl.BlockSpec((None, ts, arr.shape[-1]), lambda b, s: (b, s, 0)))
        else:
            in_specs.append(pl.BlockSpec((arr.shape[0], None, ts, arr.shape[-1]), lambda b, s: (0, b, s, 0)))
        args.append(arr)
    out_specs, out_shape = [], []
    for shape, dt, kind in outs:
        if kind == "row":
            out_specs.append(pl.BlockSpec((None, ts, shape[-1]), lambda b, s: (b, s, 0)))
        else:
            out_specs.append(pl.BlockSpec((shape[0], None, ts, shape[-1]), lambda b, s: (0, b, s, 0)))
        out_shape.append(SDS(shape, dt))
    return pl.pallas_call(
        body,
        grid=(B, S // ts),
        in_specs=in_specs,
        out_specs=out_specs,
        out_shape=out_shape,
        name=name,
        compiler_params=_cparams("parallel", "parallel"),
    )(*args)


def _mla_prep_fwd(q, kv, proj, cos, sin, name):
    B, S, _ = q.shape
    H = MLA_HEADS

    def body(q_ref, kv_ref, p_ref, cos_ref, sin_ref, qh_ref, kh_ref, vh_ref):
        cos_v, sin_v = cos_ref[...], sin_ref[...]
        kr = _rope(p_ref[:, Q_LORA + KV_LORA :], cos_v, sin_v).astype(BF16)
        for h in range(H):
            qn = q_ref[:, h * QK_DIM : h * QK_DIM + QK_NOPE]
            qr = _rope(q_ref[:, h * QK_DIM + QK_NOPE : (h + 1) * QK_DIM], cos_v, sin_v)
            qh_ref[h] = jnp.concatenate([qn, qr], axis=1).astype(BF16)
            kn = kv_ref[:, h * 128 : h * 128 + QK_NOPE].astype(BF16)
            kh_ref[h] = jnp.concatenate([kn, kr], axis=1)
            vh_ref[h] = kv_ref[:, h * 128 + QK_NOPE : (h + 1) * 128].astype(BF16)

    return _heads_call(
        body,
        name,
        B,
        S,
        [(q, "row"), (kv, "row"), (proj, "row"), (cos, "row"), (sin, "row")],
        [((H, B, S, QK_DIM), BF16, "heads"), ((H, B, S, QK_DIM), BF16, "heads"), ((H, B, S, V_HEAD), BF16, "heads")],
    )


def _mla_prep_bwd(dqh, dkh, dvh, cos, sin, name):
    H, B, S, _ = dqh.shape

    def body(dqh_ref, dkh_ref, dvh_ref, cos_ref, sin_ref, dq_ref, dkv_ref, dkr_ref):
        cos_v, sin_v = cos_ref[...], sin_ref[...]
        dkr = jnp.zeros((cos_v.shape[0], QK_ROPE), F32)
        for h in range(H):
            dqv = dqh_ref[h].astype(F32)
            dq_ref[:, h * QK_DIM : h * QK_DIM + QK_NOPE] = dqv[:, :QK_NOPE].astype(BF16)
            dq_ref[:, h * QK_DIM + QK_NOPE : (h + 1) * QK_DIM] = _rope_t(dqv[:, QK_NOPE:], cos_v, sin_v).astype(BF16)
            dkv = dkh_ref[h].astype(F32)
            dkv_ref[:, h * 128 : h * 128 + QK_NOPE] = dkv[:, :QK_NOPE].astype(BF16)
            dkv_ref[:, h * 128 + QK_NOPE : (h + 1) * 128] = dvh_ref[h]
            dkr = dkr + dkv[:, QK_NOPE:]
        dkr_ref[...] = _rope_t(dkr, cos_v, sin_v)

    return _heads_call(
        body,
        name,
        B,
        S,
        [(dqh, "heads"), (dkh, "heads"), (dvh, "heads"), (cos, "row"), (sin, "row")],
        [((B, S, H * QK_DIM), BF16, "row"), ((B, S, H * 128), BF16, "row"), ((B, S, QK_ROPE), F32, "row")],
    )


LOG2E = 1.4426950408889634


def _tril_mask(t):
    return lax.broadcasted_iota(jnp.int32, (t, t), 0) >= lax.broadcasted_iota(jnp.int32, (t, t), 1)


def _attn_fwd(qh, kh, vh, name):
    H, B, S, _ = qh.shape
    t = min(256, S)
    scale = QK_DIM**-0.5
    c2 = scale * LOG2E

    def body(q_ref, k_ref, v_ref, o_ref, lse_ref):
        i = pl.program_id(2)
        qs = [q_ref[0], q_ref[1]]

        def step(j, carry, diagonal):
            rows = pl.ds(pl.multiple_of(j * t, t), t)
            out = []
            for hh in range(2):
                m, l, acc = carry[hh]
                s = _bdot_nt(qs[hh], k_ref[hh, rows, :])
                if diagonal:
                    s = jnp.where(_tril_mask(t), s, NEG)
                m_new = jnp.maximum(m, jnp.max(s, axis=1, keepdims=True))
                p = jnp.exp2((s - m_new) * c2)
                a = jnp.exp2((m - m_new) * c2)
                l = a * l + jnp.sum(p, axis=1, keepdims=True)
                acc = a * acc + _bdot(p, v_ref[hh, rows, :])
                out.append((m_new, l, acc))
            return tuple(out)

        one = (jnp.full((t, 1), NEG, F32), jnp.zeros((t, 1), F32), jnp.zeros((t, V_HEAD), F32))
        carry = lax.fori_loop(0, i, lambda j, cy: step(j, cy, False), (one, one))
        carry = step(i, carry, True)
        outs = []
        for hh in range(2):
            m, l, acc = carry[hh]
            outs.append(acc / l)
            lse_ref[hh] = m * scale + jnp.log(l)
        o_ref[...] = jnp.concatenate(outs, axis=1).astype(BF16)

    return pl.pallas_call(
        body,
        grid=(B, H // 2, S // t),
        in_specs=[
            pl.BlockSpec((2, None, t, QK_DIM), lambda b, p, i: (p, b, i, 0)),
            pl.BlockSpec((2, None, S, QK_DIM), lambda b, p, i: (p, b, 0, 0)),
            pl.BlockSpec((2, None, S, V_HEAD), lambda b, p, i: (p, b, 0, 0)),
        ],
        out_specs=[
            pl.BlockSpec((None, t, 2 * V_HEAD), lambda b, p, i: (b, i, p)),
            pl.BlockSpec((2, None, t, 1), lambda b, p, i: (p, b, i, 0)),
        ],
        out_shape=[SDS((B, S, H * V_HEAD), BF16), SDS((H, B, S, 1), F32)],
        name=name,
        compiler_params=_cparams("parallel", "parallel", "arbitrary"),
    )(qh, kh, vh)


def _attn_bwd(qh, kh, vh, o, do, lse, name):
    H, B, S, _ = qh.shape
    t = min(256, S)
    nq = S // t
    scale = QK_DIM**-0.5
    c2 = scale * LOG2E

    def body(q_ref, k_ref, v_ref, o_ref, do_ref, lse_ref, dq_ref, dk_ref, dv_ref, dq_acc, delta_ref, lse2_ref):
        prod = o_ref[...].astype(F32) * do_ref[...].astype(F32)
        for hh in range(2):
            delta_ref[hh] = jnp.sum(prod[:, hh * V_HEAD : (hh + 1) * V_HEAD], axis=1, keepdims=True)
            lse2_ref[hh] = lse_ref[hh] * LOG2E
        dq_acc[...] = jnp.zeros_like(dq_acc)

        def kloop(j, _):
            krows = pl.ds(pl.multiple_of(j * t, t), t)
            ks = [k_ref[0, krows, :], k_ref[1, krows, :]]
            vs = [v_ref[0, krows, :], v_ref[1, krows, :]]

            def qstep(i, carry, diagonal):
                qrows = pl.ds(pl.multiple_of(i * t, t), t)
                do_i = do_ref[qrows, :]
                out = []
                for hh in range(2):
                    dk, dv = carry[hh]
                    q = q_ref[hh, qrows, :]
                    do_h = do_i[:, hh * V_HEAD : (hh + 1) * V_HEAD]
                    s = _bdot_nt(q, ks[hh])
                    p = jnp.exp2(s * c2 - lse2_ref[hh, qrows, :])
                    if diagonal:
                        p = jnp.where(_tril_mask(t), p, 0.0)
                    dv = dv + _bdot_tn(p, do_h)
                    dp = _bdot_nt(do_h, vs[hh])
                    ds = (p * (dp - delta_ref[hh, qrows, :])).astype(BF16)
                    dk = dk + _bdot_tn(ds, q)
                    dq_acc[hh, qrows, :] += _bdot(ds, ks[hh])
                    out.append((dk, dv))
                return tuple(out)

            one = (jnp.zeros((t, QK_DIM), F32), jnp.zeros((t, V_HEAD), F32))
            carry = qstep(j, (one, one), True)
            carry = lax.fori_loop(j + 1, nq, lambda i, cy: qstep(i, cy, False), carry)
            for hh in range(2):
                dk_ref[hh, krows, :] = (carry[hh][0] * scale).astype(BF16)
                dv_ref[hh, krows, :] = carry[hh][1].astype(BF16)
            return 0

        lax.fori_loop(0, nq, kloop, 0)
        dq_ref[...] = (dq_acc[...] * scale).astype(BF16)

    hspec = lambda w: pl.BlockSpec((2, None, S, w), lambda b, p: (p, b, 0, 0))
    ospec = pl.BlockSpec((None, S, 2 * V_HEAD), lambda b, p: (b, 0, p))
    return pl.pallas_call(
        body,
        grid=(B, H // 2),
        in_specs=[hspec(QK_DIM), hspec(QK_DIM), hspec(V_HEAD), ospec, ospec, hspec(1)],
        out_specs=[hspec(QK_DIM), hspec(QK_DIM), hspec(V_HEAD)],
        out_shape=[SDS((H, B, S, QK_DIM), BF16), SDS((H, B, S, QK_DIM), BF16), SDS((H, B, S, V_HEAD), BF16)],
        scratch_shapes=[pltpu.VMEM((2, S, QK_DIM), F32), pltpu.VMEM((2, S, 1), F32), pltpu.VMEM((2, S, 1), F32)],
        name=name,
        compiler_params=_cparams("parallel", "parallel"),
    )(qh, kh, vh, o, do, lse)


def _hgrn_pre(q, fx, lb):
    sig = jax.nn.sigmoid(fx)
    f = lb + (1.0 - lb) * sig
    return jax.nn.silu(q), 1.0 - f, jnp.log(f)


def _hgrn_gate(o, gg, gn):
    return _rms_fn(o, gn) * jax.nn.silu(gg)


def _tri(n, lower):
    r = lax.broadcasted_iota(jnp.int32, (n, n), 0)
    c = lax.broadcasted_iota(jnp.int32, (n, n), 1)
    return ((r >= c) if lower else (r <= c)).astype(F32)


def _hgrn_intra_fwd(qs, k, v, b):
    C, SB = qs.shape[0], min(HGRN_SUB, qs.shape[0])
    ridx = lax.broadcasted_iota(jnp.int32, (SB, 1), 0)
    outs = []
    for i in range(C // SB):
        r0 = i * SB
        qi, ki, vi, bi = qs[r0 : r0 + SB], k[r0 : r0 + SB], v[r0 : r0 + SB], b[r0 : r0 + SB]
        acc = jnp.zeros((SB, v.shape[1]), F32)
        for s in range(SB):
            mask = ridx >= s
            e = jnp.exp(jnp.where(mask, bi - bi[s : s + 1], 0.0))
            a = jnp.sum(jnp.where(mask, qi * ki[s : s + 1] * e, 0.0), axis=1, keepdims=True)
            acc = acc + a * vi[s : s + 1]
        if i > 0:
            ref = bi[0:1]
            qt = qi * jnp.exp(bi - ref)
            kt = k[:r0] * jnp.exp(ref - b[:r0])
            acc = acc + _bdot(_mdot_nt(qt, kt), v[:r0])
        outs.append(acc)
    return jnp.concatenate(outs, axis=0)


def _hgrn_intra_bwd(qs, k, v, b, do):
    C, SB = qs.shape[0], min(HGRN_SUB, qs.shape[0])
    nb = C // SB
    ridx = lax.broadcasted_iota(jnp.int32, (SB, 1), 0)
    dq_p = [None] * nb
    dk_p = [jnp.zeros((SB, k.shape[1]), F32) for _ in range(nb)]
    dv_p = [jnp.zeros((SB, v.shape[1]), F32) for _ in range(nb)]
    for i in range(nb):
        r0 = i * SB
        qi, ki, vi, bi, doi = qs[r0 : r0 + SB], k[r0 : r0 + SB], v[r0 : r0 + SB], b[r0 : r0 + SB], do[r0 : r0 + SB]
        dqi = jnp.zeros_like(qi)
        dki = jnp.zeros_like(ki)
        dvi = jnp.zeros_like(vi)
        for s in range(SB):
            mask = ridx >= s
            e = jnp.where(mask, jnp.exp(jnp.where(mask, bi - bi[s : s + 1], 0.0)), 0.0)
            da = jnp.sum(doi * vi[s : s + 1], axis=1, keepdims=True)
            a = jnp.sum(qi * ki[s : s + 1] * e, axis=1, keepdims=True)
            dqi = dqi + da * (ki[s : s + 1] * e)
            dk_row = jnp.sum(da * qi * e, axis=0, keepdims=True)
            dv_row = jnp.sum(a * doi, axis=0, keepdims=True)
            dki = jnp.where(ridx == s, dki + dk_row, dki)
            dvi = jnp.where(ridx == s, dvi + dv_row, dvi)
        if i > 0:
            ref = bi[0:1]
            eq = jnp.exp(bi - ref)
            ek = jnp.exp(ref - b[:r0])
            qt = qi * eq
            kt = k[:r0] * ek
            A = _mdot_nt(qt, kt)
            dA = _bdot_nt(doi, v[:r0])
            dvl = _bdot_tn(A, doi)
            dqi = dqi + _mdot(dA, kt) * eq
            dkl = _mdot_tn(dA, qt) * ek
            for j in range(i):
                dk_p[j] = dk_p[j] + dkl[j * SB : (j + 1) * SB]
                dv_p[j] = dv_p[j] + dvl[j * SB : (j + 1) * SB]
        dq_p[i] = dqi
        dk_p[i] = dk_p[i] + dki
        dv_p[i] = dv_p[i] + dvi
    return jnp.concatenate(dq_p, axis=0), jnp.concatenate(dk_p, axis=0), jnp.concatenate(dv_p, axis=0)


def _hgrn_fwd(proj, lb, gn, name):
    B, S, W = proj.shape
    HK = W // 4
    H = HK // HGRN_K
    C = min(HGRN_CHUNK, S)
    N = S // C

    HP = HGRN_PAR if H % HGRN_PAR == 0 else 1
    WP = HP * HGRN_K

    def body(q_ref, f_ref, i_ref, g_ref, lb_ref, gn_ref, og_ref, o_ref, st_ref):
        gn_v = gn_ref[...]
        tril = _tri(C, True)

        def chunk(n, sts):
            rows = pl.ds(pl.multiple_of(n * C, C), C)
            out = []
            for hh in range(HP):
                ln = slice(hh * HGRN_K, (hh + 1) * HGRN_K)
                st = sts[hh]
                qs, k, g = _hgrn_pre(q_ref[rows, ln], f_ref[rows, ln], lb_ref[:, ln])
                v = i_ref[rows, ln]
                b = _hdot(tril, g)
                st_ref[hh, n] = st
                o = _hgrn_intra_fwd(qs, k, v, b) + _bdot_nt(qs * jnp.exp(b), st)
                bl = b[C - 1 : C]
                out.append(st * jnp.exp(bl) + _bdot_tn(v, k * jnp.exp(bl - b)))
                o_ref[rows, ln] = o
                og_ref[rows, ln] = _hgrn_gate(o, g_ref[rows, ln], gn_v).astype(BF16)
            return tuple(out)

        lax.fori_loop(0, N, chunk, tuple(jnp.zeros((HGRN_K, HGRN_K), F32) for _ in range(HP)))

    col = lambda part: pl.BlockSpec((None, S, WP), lambda b, h: (b, 0, part * (H // HP) + h))
    return pl.pallas_call(
        body,
        grid=(B, H // HP),
        in_specs=[col(0), col(1), col(2), col(3), pl.BlockSpec((1, WP), lambda b, h: (0, h)), pl.BlockSpec((1, HGRN_K), lambda b, h: (0, 0))],
        out_specs=[col(0), col(0), pl.BlockSpec((None, HP, N, HGRN_K, HGRN_K), lambda b, h: (b, h, 0, 0, 0))],
        out_shape=[SDS((B, S, HK), BF16), SDS((B, S, HK), F32), SDS((B, H, N, HGRN_K, HGRN_K), F32)],
        name=name,
        compiler_params=_cparams("parallel", "parallel"),
    )(proj, proj, proj, proj, lb, gn)


def _hgrn_bwd(proj, lb, gn, o_pre, states, dog, name):
    B, S, W = proj.shape
    HK = W // 4
    H = HK // HGRN_K
    C = min(HGRN_CHUNK, S)
    N = S // C

    HP = HGRN_PAR if H % HGRN_PAR == 0 else 1
    WP = HP * HGRN_K

    def body(q_ref, f_ref, i_ref, g_ref, lb_ref, gn_ref, o_ref, st_ref, dog_ref, dq_ref, df_ref, di_ref, dg_ref, dlb_ref, dgn_ref):
        gn_v = gn_ref[...]
        tril = _tri(C, True)
        triu = _tri(C, False)

        def chunk(idx, carry):
            n = N - 1 - idx
            rows = pl.ds(pl.multiple_of(n * C, C), C)
            out = []
            for hh in range(HP):
                ln = slice(hh * HGRN_K, (hh + 1) * HGRN_K)
                dst, dlb, dgn = carry[hh]
                (qs, k, g), pre_vjp = jax.vjp(_hgrn_pre, q_ref[rows, ln], f_ref[rows, ln], lb_ref[:, ln])
                v = i_ref[rows, ln]
                _, gate_vjp = jax.vjp(_hgrn_gate, o_ref[rows, ln], g_ref[rows, ln], gn_v)
                do, dgg, dgn_c = gate_vjp(dog_ref[rows, ln])
                b = _hdot(tril, g)
                st0 = st_ref[hh, n]
                eb = jnp.exp(b)
                bl = b[C - 1 : C]
                ebl = jnp.exp(bl)
                ekb = jnp.exp(bl - b)
                qe = qs * eb
                kt = k * ekb
                dqs, dk, dv = _hgrn_intra_bwd(qs, k, v, b, do)
                dqs = dqs + _bdot(do, st0) * eb
                dk = dk + _bdot(v, dst) * ekb
                dv = dv + _bdot_nt(kt, dst)
                st1 = st0 * ebl + _bdot_tn(v, kt)
                dbl = jnp.sum(st1 * dst, axis=0, keepdims=True)
                dst = dst * ebl + _bdot_tn(do, qe)
                dgl = _hdot(triu, qs * dqs - k * dk) + dbl
                dq_pre, dfx, dlb_c = pre_vjp((dqs, dk, dgl))
                dq_ref[rows, ln] = dq_pre.astype(BF16)
                df_ref[rows, ln] = dfx.astype(BF16)
                di_ref[rows, ln] = dv.astype(BF16)
                dg_ref[rows, ln] = dgg.astype(BF16)
                out.append((dst, dlb + dlb_c, dgn + dgn_c))
            return tuple(out)

        zero = jnp.zeros((1, HGRN_K), F32)
        one = (jnp.zeros((HGRN_K, HGRN_K), F32), zero, zero)
        res = lax.fori_loop(0, N, chunk, tuple(one for _ in range(HP)))
        for hh in range(HP):
            dlb_ref[hh] = res[hh][1]
            dgn_ref[hh] = res[hh][2]

    col = lambda part: pl.BlockSpec((None, S, WP), lambda b, h: (b, 0, part * (H // HP) + h))
    vec = pl.BlockSpec((None, HP, 1, HGRN_K), lambda b, h: (b, h, 0, 0))
    return pl.pallas_call(
        body,
        grid=(B, H // HP),
        in_specs=[
            col(0), col(1), col(2), col(3),
            pl.BlockSpec((1, WP), lambda b, h: (0, h)),
            pl.BlockSpec((1, HGRN_K), lambda b, h: (0, 0)),
            col(0),
            pl.BlockSpec((None, HP, N, HGRN_K, HGRN_K), lambda b, h: (b, h, 0, 0, 0)),
            col(0),
        ],
        out_specs=[col(0), col(0), col(0), col(0), vec, vec],
        out_shape=[SDS((B, S, HK), BF16)] * 4 + [SDS((B, H, 1, HGRN_K), F32)] * 2,
        name=name,
        compiler_params=_cparams("parallel", "parallel"),
    )(proj, proj, proj, proj, lb, gn, o_pre, states, dog)


def _ada_fwd(c_all, w, b, name):
    Bg, D = c_all.shape
    L, _, n = w.shape

    def body(c_ref, w_ref, b_ref, o_ref):
        o_ref[...] = _bdot(jax.nn.silu(c_ref[...]), w_ref[...]) + b_ref[...]

    return pl.pallas_call(
        body,
        grid=(L,),
        in_specs=[
            pl.BlockSpec((Bg, D), lambda l: (0, 0)),
            pl.BlockSpec((None, D, n), lambda l: (l, 0, 0)),
            pl.BlockSpec((None, 1, n), lambda l: (l, 0, 0)),
        ],
        out_specs=pl.BlockSpec((None, Bg, n), lambda l: (l, 0, 0)),
        out_shape=SDS((L, Bg, n), F32),
        name=name,
        compiler_params=_cparams("parallel"),
    )(c_all, w, b)


def _ada_bwd(c_all, dmod, name):
    Bg, D = c_all.shape
    L, _, n = dmod.shape

    def body(c_ref, d_ref, dw_ref, db_ref):
        d = d_ref[...]
        dw_ref[...] = _bdot_tn(jax.nn.silu(c_ref[...]), d)
        db_ref[...] = jnp.sum(d, axis=0, keepdims=True)

    return pl.pallas_call(
        body,
        grid=(L,),
        in_specs=[pl.BlockSpec((Bg, D), lambda l: (0, 0)), pl.BlockSpec((None, Bg, n), lambda l: (l, 0, 0))],
        out_specs=[pl.BlockSpec((None, D, n), lambda l: (l, 0, 0)), pl.BlockSpec((None, 1, n), lambda l: (l, 0, 0))],
        out_shape=[SDS((L, D, n), F32), SDS((L, 1, n), F32)],
        name=name,
        compiler_params=_cparams("parallel"),
    )(c_all, dmod)


def _adamw(w, gs, m, v, name):
    shape = w.shape
    cols = shape[-1]
    rows = w.size // cols
    tr = rows
    for cand in (512, 256, 128, 64, 32, 16, 8):
        if rows % cand == 0 and cand * cols * 4 <= 2 * 1024 * 1024:
            tr = cand
            break
    as2d = lambda a: a.reshape(rows, cols)
    ng = len(gs)
    c1 = 1.0 / (1.0 - ADAM_B1**ADAM_STEP)
    c2 = 1.0 / (1.0 - ADAM_B2**ADAM_STEP)

    def body(*refs):
        w_ref, m_ref, v_ref = refs[0], refs[1], refs[2]
        g_refs = refs[3 : 3 + ng]
        g_out, d_out, m_out, v_out = refs[3 + ng :]
        g = g_refs[0][...].astype(F32)
        for r in g_refs[1:]:
            g = g + r[...].astype(F32)
        m_new = ADAM_B1 * m_ref[...] + (1.0 - ADAM_B1) * g
        v_new = ADAM_B2 * v_ref[...] + (1.0 - ADAM_B2) * jnp.square(g)
        g_out[...] = g
        m_out[...] = m_new
        v_out[...] = v_new
        d_out[...] = -ADAM_LR * ((m_new * c1) / (jnp.sqrt(v_new * c2) + ADAM_EPS) + ADAM_WD * w_ref[...])

    spec = pl.BlockSpec((tr, cols), lambda i: (i, 0))
    outs = pl.pallas_call(
        body,
        grid=(rows // tr,),
        in_specs=[spec] * (3 + ng),
        out_specs=[spec] * 4,
        out_shape=[SDS((rows, cols), F32)] * 4,
        name=name,
        compiler_params=_cparams("parallel"),
    )(as2d(w), as2d(m), as2d(v), *[as2d(g) for g in gs])
    return tuple(o.reshape(shape) for o in outs)


def _sum4(own, recv, name):
    shape = own.shape
    cols = shape[-1]
    rows = own.size // cols
    tr = rows
    for cand in (512, 256, 128, 64, 32, 16):
        if rows % cand == 0 and cand * cols * 4 <= 2 * 1024 * 1024:
            tr = cand
            break

    def body(own_ref, recv_ref, o_ref):
        acc = own_ref[...].astype(F32)
        for r in range(3):
            acc = acc + recv_ref[r].astype(F32)
        o_ref[...] = acc

    out = pl.pallas_call(
        body,
        grid=(rows // tr,),
        in_specs=[pl.BlockSpec((tr, cols), lambda i: (i, 0)), pl.BlockSpec((3, tr, cols), lambda i: (0, i, 0))],
        out_specs=pl.BlockSpec((tr, cols), lambda i: (i, 0)),
        out_shape=SDS((rows, cols), F32),
        name=name,
        compiler_params=_cparams("parallel"),
    )(own.reshape(rows, cols), recv.reshape(3, rows, cols))
    return out.reshape(shape)


def _my_place():
    return lax.axis_index("x"), lax.axis_index("y"), lax.axis_index("c")


def _flip(v, bit):
    return 1 - v if bit else v


def _allgather8(x, name):
    r, n = x.shape

    def body(x_ref, o_ref, send_sems, recv_sems, local_sem):
        mx, my, mc = _my_place()
        me = 4 * mx + 2 * my + mc
        mine = pltpu.make_async_copy(x_ref, o_ref.at[me], local_sem)
        mine.start()
        sends = []
        for rel in range(1, 8):
            peer = (_flip(mx, rel & 4), _flip(my, rel & 2), _flip(mc, rel & 1))
            cp = pltpu.make_async_remote_copy(
                src_ref=x_ref, dst_ref=o_ref.at[me], send_sem=send_sems.at[rel - 1], recv_sem=recv_sems.at[rel - 1],
                device_id=peer, device_id_type=MESH,
            )
            cp.start()
            sends.append(cp)
        for rel in range(1, 8):
            px, py, pc = _flip(mx, rel & 4), _flip(my, rel & 2), _flip(mc, rel & 1)
            pltpu.make_async_remote_copy(
                src_ref=x_ref, dst_ref=o_ref.at[4 * px + 2 * py + pc], send_sem=send_sems.at[rel - 1],
                recv_sem=recv_sems.at[rel - 1], device_id=(px, py, pc), device_id_type=MESH,
            ).wait_recv()
        for cp in sends:
            cp.wait_send()
        mine.wait()

    return pl.pallas_call(
        body,
        out_shape=SDS((8, r, n), x.dtype),
        in_specs=[pl.BlockSpec(memory_space=pl.ANY)],
        out_specs=pl.BlockSpec(memory_space=pl.ANY),
        scratch_shapes=[pltpu.SemaphoreType.DMA((7,)), pltpu.SemaphoreType.DMA((7,)), pltpu.SemaphoreType.DMA],
        name=name,
    )(x)


_HBM = pl.BlockSpec(memory_space=pl.ANY)


_SEM = pl.BlockSpec(memory_space=pltpu.SEMAPHORE)
_HBM_ONLY = pl.BlockSpec(memory_space=pltpu.HBM)
_EFFECT = pltpu.SideEffectType.DATAFLOW_SIDE_EFFECTING


def _in_hbm(a):
    return pltpu.with_memory_space_constraint(a, pltpu.HBM)


def _gather_start(lands, after, name):
    n = len(lands)

    def body(*refs):
        land = refs[:n]
        send_sems, recv_sems = refs[n + 1], refs[n + 2]
        token = refs[-1]
        mx, my, mc = _my_place()
        for i in range(n):
            for rel in range(1, 4):
                pltpu.make_async_remote_copy(
                    src_ref=land[i].at[2 * mx + my], dst_ref=land[i].at[2 * mx + my],
                    send_sem=send_sems.at[3 * i + rel - 1], recv_sem=recv_sems.at[3 * i + rel - 1],
                    device_id=(_flip(mx, rel & 2), _flip(my, rel & 1), mc), device_id_type=MESH,
                ).start()
        token[...] = jnp.zeros_like(token)

    outs = pl.pallas_call(
        body,
        name=name,
        out_shape=(
            pltpu.SemaphoreType.DMA((3 * n,)), pltpu.SemaphoreType.DMA((3 * n,)),
            *[pltpu.HBM(a.shape, a.dtype) for a in lands], SDS((8, LANES), F32),
        ),
        in_specs=[_HBM_ONLY] * n + [_HBM],
        out_specs=(_SEM, _SEM, *[_HBM_ONLY] * n, pl.BlockSpec(memory_space=pltpu.VMEM)),
        input_output_aliases={i: 2 + i for i in range(n)},
        compiler_params=pltpu.CompilerParams(has_side_effects=_EFFECT),
    )(*[_in_hbm(a) for a in lands], after)
    return outs[0], outs[1], list(outs[2 : 2 + n]), outs[-1]


def _gather_wait(send_sems, recv_sems, lands, after, name):
    n = len(lands)

    def body(*refs):
        land = refs[:n]
        s_sems, r_sems = refs[n], refs[n + 1]
        mx, my, mc = _my_place()
        for i in range(n):
            for rel in range(1, 4):
                px, py = _flip(mx, rel & 2), _flip(my, rel & 1)
                cp = pltpu.make_async_remote_copy(
                    src_ref=land[i].at[2 * mx + my], dst_ref=land[i].at[2 * px + py],
                    send_sem=s_sems.at[3 * i + rel - 1], recv_sem=r_sems.at[3 * i + rel - 1],
                    device_id=(px, py, mc), device_id_type=MESH,
                )
                cp.wait_send()
                cp.wait_recv()

    outs = pl.pallas_call(
        body,
        name=name,
        out_shape=tuple(pltpu.HBM(a.shape, a.dtype) for a in lands),
        in_specs=[_HBM_ONLY] * n + [_SEM, _SEM, _HBM],
        out_specs=[_HBM_ONLY] * n,
        input_output_aliases={i: i for i in range(n)},
        compiler_params=pltpu.CompilerParams(has_side_effects=_EFFECT),
    )(*lands, send_sems, recv_sems, after)
    return list(outs)


def _scatter_start(slabs, lands, places, name):
    n = len(slabs)

    def body(*refs):
        ins, land = refs[:n], refs[n : 2 * n]
        send_sems, recv_sems = refs[2 * n], refs[2 * n + 1]
        token = refs[-1]
        mx, my, mc = _my_place()
        for i in range(n):
            for rel in range(1, 4):
                px, py = _flip(mx, rel & 2), _flip(my, rel & 1)
                pltpu.make_async_remote_copy(
                    src_ref=ins[i].at[2 * px + py], dst_ref=land[i].at[rel - 1, places[i]],
                    send_sem=send_sems.at[3 * i + rel - 1], recv_sem=recv_sems.at[3 * i + rel - 1],
                    device_id=(px, py, mc), device_id_type=MESH,
                ).start()
        token[...] = jnp.zeros_like(token)

    outs = pl.pallas_call(
        body,
        name=name,
        out_shape=(
            pltpu.SemaphoreType.DMA((3 * n,)), pltpu.SemaphoreType.DMA((3 * n,)),
            *[pltpu.HBM(a.shape, a.dtype) for a in slabs], *[pltpu.HBM(a.shape, a.dtype) for a in lands],
            SDS((8, LANES), F32),
        ),
        in_specs=[_HBM_ONLY] * (2 * n),
        out_specs=(_SEM, _SEM, *[_HBM_ONLY] * (2 * n), pl.BlockSpec(memory_space=pltpu.VMEM)),
        input_output_aliases={i: 2 + i for i in range(2 * n)},
        compiler_params=pltpu.CompilerParams(has_side_effects=_EFFECT),
    )(*[_in_hbm(a) for a in slabs], *[_in_hbm(a) for a in lands])
    return outs[0], outs[1], list(outs[2 : 2 + n]), list(outs[2 + n : 2 + 2 * n]), outs[-1]


def _scatter_wait(send_sems, recv_sems, slabs, lands, places, after, name):
    n = len(slabs)

    def body(*refs):
        ins, land = refs[:n], refs[n : 2 * n]
        s_sems, r_sems = refs[2 * n], refs[2 * n + 1]
        mx, my, mc = _my_place()
        for i in range(n):
            for rel in range(1, 4):
                px, py = _flip(mx, rel & 2), _flip(my, rel & 1)
                cp = pltpu.make_async_remote_copy(
                    src_ref=ins[i].at[2 * px + py], dst_ref=land[i].at[rel - 1, places[i]],
                    send_sem=s_sems.at[3 * i + rel - 1], recv_sem=r_sems.at[3 * i + rel - 1],
                    device_id=(px, py, mc), device_id_type=MESH,
                )
                cp.wait_send()
                cp.wait_recv()

    outs = pl.pallas_call(
        body,
        name=name,
        out_shape=(*[pltpu.HBM(a.shape, a.dtype) for a in slabs], *[pltpu.HBM(a.shape, a.dtype) for a in lands]),
        in_specs=[_HBM_ONLY] * (2 * n) + [_SEM, _SEM, _HBM],
        out_specs=[_HBM_ONLY] * (2 * n),
        input_output_aliases={i: i for i in range(2 * n)},
        compiler_params=pltpu.CompilerParams(has_side_effects=_EFFECT),
    )(*slabs, *lands, send_sems, recv_sems, after)
    return list(outs[:n]), list(outs[n:])


def _swap_sibling(parts, name):
    n = len(parts)

    def body(*refs):
        ins, outs = refs[:n], refs[n : 2 * n]
        send_sems, recv_sems = refs[2 * n :]
        mx, my, mc = _my_place()
        sends = []
        for i in range(n):
            cp = pltpu.make_async_remote_copy(
                src_ref=ins[i], dst_ref=outs[i], send_sem=send_sems.at[i], recv_sem=recv_sems.at[i],
                device_id=(mx, my, 1 - mc), device_id_type=MESH,
            )
            cp.start()
            sends.append(cp)
        for cp in sends:
            cp.wait_recv()
        for cp in sends:
            cp.wait_send()

    return pl.pallas_call(
        body,
        out_shape=[SDS(s.shape, s.dtype) for s in parts],
        in_specs=[_HBM] * n,
        out_specs=[_HBM] * n,
        scratch_shapes=[pltpu.SemaphoreType.DMA((n,)), pltpu.SemaphoreType.DMA((n,))],
        name=name,
    )(*parts)


def _pad_rows(a, rows):
    return jnp.pad(a, ((0, rows - a.shape[0]), (0, 0)))


def kernel(x, c, positions, mla_w_in, mla_q_norm, mla_w_qb, mla_kv_norm, mla_w_kvb, mla_w_o, hgrn_lb, hgrn_w_in, hgrn_g_norm, hgrn_w_o, ffn_w_in, ffn_w_out, ada_w, ada_b, ln_g, ln_b, loss_target, m_mla_w_in, m_mla_q_norm, m_mla_w_qb, m_mla_kv_norm, m_mla_w_kvb, m_mla_w_o, m_hgrn_lb, m_hgrn_w_in, m_hgrn_g_norm, m_hgrn_w_o, m_ffn_w_in, m_ffn_w_out, m_ada_w, m_ada_b, m_ln_g, m_ln_b, v_mla_w_in, v_mla_q_norm, v_mla_w_qb, v_mla_kv_norm, v_mla_w_kvb, v_mla_w_o, v_hgrn_lb, v_hgrn_w_in, v_hgrn_g_norm, v_hgrn_w_o, v_ffn_w_in, v_ffn_w_out, v_ada_w, v_ada_b, v_ln_g, v_ln_b):
    B, S, D = x.shape
    T = B * S
    depth = ada_w.shape[0]
    n_mla, n_hgrn = mla_w_in.shape[0], hgrn_w_in.shape[0]
    n_sub = 2 * depth
    alpha = (2.0 * depth) ** 0.25
    mx, my, mc = _my_place()
    me = 4 * mx + 2 * my + mc
    k_me = 2 * mx + my
    Bg = 8 * B
    HK = hgrn_w_o.shape[1] * 4
    dq = D // 4

    lbw = hgrn_lb.shape[1]
    first = jnp.zeros((8, max(D, 4 * lbw)), F32)
    first = first.at[:B, :D].set(c).at[B : B + n_hgrn, :lbw].set(hgrn_lb)
    first_all = _allgather8(first, "gather_cond")
    c_all = first_all[:, :B, :D].reshape(Bg, D)
    lb_logits = jnp.concatenate([first_all[2 * k, B : B + n_hgrn, :lbw] for k in range(4)], axis=1)

    def lower_bounds_fn(logits):
        soft = jax.nn.softmax(logits, axis=0)
        return jnp.cumsum(soft, axis=0) - soft[0]

    lower_bounds, lower_bounds_vjp = jax.vjp(lower_bounds_fn, lb_logits)

    n_ada = ada_w.shape[-1]
    mod_part = _ada_fwd(c_all, ada_w.reshape(n_sub, D, n_ada), ada_b.reshape(n_sub, 1, n_ada), "ada_fwd")
    mod_all = _allgather8(mod_part.reshape(n_sub * Bg, n_ada), "gather_mod").reshape(8, n_sub, Bg, n_ada)
    mod = jnp.concatenate([mod_all[2 * k] for k in range(4)], axis=-1)
    mod = lax.dynamic_slice_in_dim(mod, me * B, B, axis=1)
    shift = [mod[j, :, None, :D] for j in range(n_sub)]
    scale = [mod[j, :, None, D : 2 * D] for j in range(n_sub)]
    gate = [mod[j, :, None, 2 * D :] for j in range(n_sub)]

    ln_rows = 2 * n_sub
    ln_local = _pad_rows(jnp.concatenate([ln_g.reshape(n_sub, dq), ln_b.reshape(n_sub, dq)], axis=0), -(-ln_rows // 8) * 8)
    ln_pad = jnp.zeros((ln_local.shape[0], -(-dq // LANES) * LANES), F32).at[:, :dq].set(ln_local)
    ln_all = _allgather8(ln_pad, "gather_ln")
    ln_full = jnp.concatenate([ln_all[2 * k, :ln_rows, :dq] for k in range(4)], axis=1)
    lng = [ln_full[j][None, :] for j in range(n_sub)]
    lnb = [ln_full[n_sub + j][None, :] for j in range(n_sub)]

    main = dict(mla_w_in=mla_w_in, mla_w_qb=mla_w_qb, mla_w_kvb=mla_w_kvb, mla_w_o=mla_w_o, hgrn_w_in=hgrn_w_in,
                hgrn_w_o=hgrn_w_o, ffn_w_in=ffn_w_in, ffn_w_out=ffn_w_out)
    names = list(main)

    def group_kinds(layer, part):
        if part:
            return [("ffn_w_in", layer), ("ffn_w_out", layer)]
        mixer = ["mla_w_in", "mla_w_qb", "mla_w_kvb", "mla_w_o"] if layer % 2 == 0 else ["hgrn_w_in", "hgrn_w_o"]
        return [(k, layer // 2) for k in mixer]

    gathers = {}
    after = mod_all[0, 0, :8, :LANES] + ln_all[0, :8, :LANES]
    for layer in range(depth):
        for part in range(2):
            lands = [lax.dynamic_update_index_in_dim(lax.empty((4,) + main[k].shape[1:], BF16), main[k][i].astype(BF16), k_me, 0)
                     for k, i in group_kinds(layer, part)]
            ssem, rsem, lands, after = _gather_start(lands, after, f"gather_start_l{layer}p{part}")
            gathers[layer, part] = (ssem, rsem, lands)
    scale[0] = scale[0] + after[0, 0]

    def row_w(g):
        return g.reshape(1, g.shape[0] * g.shape[1], g.shape[2])

    def full_w_in(g):
        return jnp.transpose(g, (1, 0, 2)).reshape(1, g.shape[1], 4 * g.shape[2])

    ang = positions.astype(F32)[..., None] * (ROPE_THETA ** (-jnp.arange(0, QK_ROPE, 2, dtype=F32) / QK_ROPE))
    cos, sin = jnp.cos(ang), jnp.sin(ang)

    gq = [mla_q_norm[j][None, :] for j in range(n_mla)]
    gkv = [mla_kv_norm[j][None, :] for j in range(n_mla)]
    gn = [hgrn_g_norm[j][None, :] for j in range(n_hgrn)]

    def r2(a):
        return a.reshape(T, a.shape[-1])

    def r3(a):
        return a.reshape(B, S, a.shape[-1])

    saved = []
    xs = x
    for layer in range(depth):
        j = layer // 2
        sub = 2 * layer
        tag = f"l{layer}"
        ssem, rsem, lands = gathers[layer, 0]
        lands = _gather_wait(ssem, rsem, lands, xs if layer else scale[0], f"gather_wait_{tag}p0")
        wl = {k: g for (k, _), g in zip(group_kinds(layer, 0), lands)}
        h = _modulate(xs, scale[sub], shift[sub], f"mod_{tag}a")
        if layer % 2 == 0:
            wl["mla_w_in"] = full_w_in(wl["mla_w_in"])
            proj = r3(_mm_nn(r2(h), wl["mla_w_in"], F32, f"mla_in_{tag}"))
            qn, kvn = _mla_mid_fwd(proj, gq[j], gkv[j], f"mla_mid_{tag}")
            q = r3(_mm_nn(r2(qn), wl["mla_w_qb"], F32, f"mla_qb_{tag}"))
            kv = r3(_mm_nn(r2(kvn), wl["mla_w_kvb"], F32, f"mla_kvb_{tag}"))
            qh, kh, vh = _mla_prep_fwd(q, kv, proj, cos, sin, f"mla_prep_{tag}")
            o, lse = _attn_fwd(qh, kh, vh, f"attn_{tag}")
            wl["mla_w_o"] = row_w(wl["mla_w_o"])
            y = r3(_mm_nn(r2(o), wl["mla_w_o"], F32, f"mla_o_{tag}"))
            mix = (h, proj, qn, kvn, qh, kh, vh, o, lse)
        else:
            proj = r3(_mm_nn(r2(h), wl["hgrn_w_in"], F32, f"hgrn_in_{tag}"))
            og, o_pre, states = _hgrn_fwd(proj, lower_bounds[j][None, :], gn[j], f"hgrn_{tag}")
            wl["hgrn_w_o"] = row_w(wl["hgrn_w_o"])
            y = r3(_mm_nn(r2(og), wl["hgrn_w_o"], F32, f"hgrn_o_{tag}"))
            mix = (h, proj, og, o_pre, states)
        x1 = _ln_fwd(alpha, xs, y, gate[sub], lng[sub], lnb[sub], f"ln_{tag}a")
        ssem, rsem, lands = gathers[layer, 1]
        lands = _gather_wait(ssem, rsem, lands, x1, f"gather_wait_{tag}p1")
        wl.update({k: g for (k, _), g in zip(group_kinds(layer, 1), lands)})
        h2 = _modulate(x1, scale[sub + 1], shift[sub + 1], f"mod_{tag}b")
        u = r3(_mm_nn(r2(h2), wl["ffn_w_in"], F32, f"ffn_in_{tag}"))
        a = _swiglu_fwd(u, f"swiglu_{tag}")
        wl["ffn_w_out"] = row_w(wl["ffn_w_out"])
        y2 = r3(_mm_nn(r2(a), wl["ffn_w_out"], F32, f"ffn_out_{tag}"))
        x2 = _ln_fwd(alpha, x1, y2, gate[sub + 1], lng[sub + 1], lnb[sub + 1], f"ln_{tag}b")
        saved.append((xs, y, x1, y2, mix, h2, u, a, wl))
        xs = x2

    loss_local, dout = _loss_head(xs, loss_target, "loss_head")
    loss = lax.psum(loss_local, ("x", "y", "c"))

    gw = {k: [None] * main[k].shape[0] for k in names}
    land = {k: lax.empty((3,) + main[k].shape, BF16) for k in names}
    scatters = []
    d_shift, d_scale, d_gate = [None] * n_sub, [None] * n_sub, [None] * n_sub
    d_lng, d_lnb = [None] * n_sub, [None] * n_sub
    d_gq, d_gkv, d_gn, d_lbnd = [None] * n_mla, [None] * n_mla, [None] * n_hgrn, [None] * n_hgrn

    def rows4(g):
        return g.reshape(4, g.shape[1] // 4, g.shape[2])

    def start_scatter(layer, part, token_to):
        kinds = group_kinds(layer, part)
        ssem, rsem, slabs_t, lands_t, token = _scatter_start(
            [gw[k][i] for k, i in kinds], [land[k] for k, _ in kinds], [i for _, i in kinds], f"scatter_start_l{layer}p{part}")
        for (k, i), s_t, l_t in zip(kinds, slabs_t, lands_t):
            gw[k][i], land[k] = s_t, l_t
        scatters.append((layer, part, ssem, rsem))
        if token_to is not None:
            gate[token_to] = gate[token_to] + token[0, 0]

    for layer in reversed(range(depth)):
        j = layer // 2
        sub = 2 * layer
        tag = f"l{layer}"
        xs, y, x1, y2, mix, h2, u, a, wl = saved[layer]
        dxr, dy2, d_gate[sub + 1], d_lng[sub + 1], d_lnb[sub + 1] = _ln_bwd(
            alpha, dout, x1, y2, gate[sub + 1], lng[sub + 1], lnb[sub + 1], f"ln_bwd_{tag}b")
        da = r3(_mm_nt(r2(dy2), wl["ffn_w_out"], F32, f"ffn_out_dx_{tag}"))
        gw["ffn_w_out"][layer] = rows4(_mm_tn(r2(a), r2(dy2), 1, BF16, f"ffn_out_dw_{tag}"))
        du = _swiglu_bwd(u, da, f"swiglu_bwd_{tag}")
        dh2 = r3(_mm_nt(r2(du), wl["ffn_w_in"], F32, f"ffn_in_dx_{tag}"))
        gw["ffn_w_in"][layer] = _mm_tn(r2(h2), r2(du), 4, BF16, f"ffn_in_dw_{tag}")
        start_scatter(layer, 1, sub)
        dout, d_scale[sub + 1], d_shift[sub + 1] = _mod_bwd(dh2, dxr, x1, scale[sub + 1], f"mod_bwd_{tag}b")
        dxr, dy, d_gate[sub], d_lng[sub], d_lnb[sub] = _ln_bwd(
            alpha, dout, xs, y, gate[sub], lng[sub], lnb[sub], f"ln_bwd_{tag}a")
        if layer % 2 == 0:
            h, proj, qn, kvn, qh, kh, vh, o, lse = mix
            do = r3(_mm_nt(r2(dy), wl["mla_w_o"], BF16, f"mla_o_dx_{tag}"))
            gw["mla_w_o"][j] = rows4(_mm_tn(r2(o), r2(dy), 1, BF16, f"mla_o_dw_{tag}"))
            dqh, dkh, dvh = _attn_bwd(qh, kh, vh, o, do, lse, f"attn_bwd_{tag}")
            dq_, dkv_, dkr = _mla_prep_bwd(dqh, dkh, dvh, cos, sin, f"mla_prep_bwd_{tag}")
            dqn = r3(_mm_nt(r2(dq_), wl["mla_w_qb"], F32, f"mla_qb_dx_{tag}"))
            gw["mla_w_qb"][j] = _mm_tn(r2(qn), r2(dq_), 4, BF16, f"mla_qb_dw_{tag}")
            dkvn = r3(_mm_nt(r2(dkv_), wl["mla_w_kvb"], F32, f"mla_kvb_dx_{tag}"))
            gw["mla_w_kvb"][j] = _mm_tn(r2(kvn), r2(dkv_), 4, BF16, f"mla_kvb_dw_{tag}")
            dproj, dgq_, dgkv_ = _mla_mid_bwd(proj, dqn, dkvn, dkr, gq[j], gkv[j], f"mla_mid_bwd_{tag}")
            d_gq[j], d_gkv[j] = dgq_.sum(0), dgkv_.sum(0)
            dh = r3(_mm_nt(r2(dproj), wl["mla_w_in"], F32, f"mla_in_dx_{tag}"))
            gwin = _mm_tn(r2(h), r2(dproj), 1, BF16, f"mla_in_dw_{tag}")[0]
            gw["mla_w_in"][j] = jnp.transpose(gwin.reshape(gwin.shape[0], 4, gwin.shape[1] // 4), (1, 0, 2))
        else:
            h, proj, og, o_pre, states = mix
            dog = r3(_mm_nt(r2(dy), wl["hgrn_w_o"], F32, f"hgrn_o_dx_{tag}"))
            gw["hgrn_w_o"][j] = rows4(_mm_tn(r2(og), r2(dy), 1, BF16, f"hgrn_o_dw_{tag}"))
            dq_, df_, di_, dg_, dlb_, dgn_ = _hgrn_bwd(proj, lower_bounds[j][None, :], gn[j], o_pre, states, dog, f"hgrn_bwd_{tag}")
            dproj = jnp.concatenate([dq_, df_, di_, dg_], axis=-1)
            d_lbnd[j] = dlb_.sum(0).reshape(1, HK)
            d_gn[j] = dgn_.sum((0, 1))
            dh = r3(_mm_nt(r2(dproj), wl["hgrn_w_in"], F32, f"hgrn_in_dx_{tag}"))
            gw["hgrn_w_in"][j] = _mm_tn(r2(h), r2(dproj), 4, BF16, f"hgrn_in_dw_{tag}")
        start_scatter(layer, 0, sub - 1 if layer else None)
        dout, d_scale[sub], d_shift[sub] = _mod_bwd(dh, dxr, xs, scale[sub], f"mod_bwd_{tag}a")
    grad_x = dout

    for layer, part, ssem, rsem in scatters:
        kinds = group_kinds(layer, part)
        slabs_t, lands_t = _scatter_wait(
            ssem, rsem, [gw[k][i] for k, i in kinds], [land[k] for k, _ in kinds], [i for _, i in kinds], grad_x,
            f"scatter_wait_l{layer}p{part}")
        for (k, i), s_t, l_t in zip(kinds, slabs_t, lands_t):
            gw[k][i], land[k] = s_t, l_t
    sums = [_sum4(jnp.stack([lax.dynamic_index_in_dim(g, k_me, 0, keepdims=False) for g in gw[k]]), land[k], f"sum4_{k}")
            for k in names]
    others = _swap_sibling(sums, "swap_sums")
    g_main = {k: (a_, b_) for k, a_, b_ in zip(names, sums, others)}

    dmod = jnp.stack([jnp.concatenate([d_shift[s_][:, 0], d_scale[s_][:, 0], d_gate[s_][:, 0]], axis=-1) for s_ in range(n_sub)])
    dmod_rows = _pad_rows(dmod.reshape(n_sub * B, 3 * D), -(-n_sub * B // 8) * 8)
    dmod_all = _allgather8(dmod_rows, "gather_dmod")[:, : n_sub * B].reshape(8, n_sub, B, 3 * D)
    dmod_all = jnp.transpose(dmod_all, (1, 0, 2, 3)).reshape(n_sub, Bg, 3 * D)
    dmod_mine = lax.dynamic_slice_in_dim(dmod_all, k_me * n_ada, n_ada, axis=2)
    g_ada_w, g_ada_b = _ada_bwd(c_all, dmod_mine, "ada_bwd")
    g_ada_w = g_ada_w.reshape(ada_w.shape)
    g_ada_b = g_ada_b.reshape(ada_b.shape)

    small = [jnp.stack(d_gq).reshape(-1), jnp.stack(d_gkv).reshape(-1), jnp.stack(d_gn).reshape(-1),
             jnp.stack(d_lbnd).reshape(-1), jnp.stack([d.sum(0) for d in d_lng]).reshape(-1),
             jnp.stack([d.sum(0) for d in d_lnb]).reshape(-1)]
    sizes = [s_.shape[0] for s_ in small]
    flat = jnp.concatenate(small)
    rows_small = -(-flat.shape[0] // (8 * LANES)) * 8
    flat = jnp.pad(flat, (0, rows_small * LANES - flat.shape[0])).reshape(rows_small, LANES)
    tot = _allgather8(flat, "gather_small")
    acc = tot[0]
    for d in range(1, 8):
        acc = acc + tot[d]
    acc = acc.reshape(-1)
    offs = [0]
    for s_ in sizes:
        offs.append(offs[-1] + s_)
    g_q_norm = acc[offs[0] : offs[1]].reshape(mla_q_norm.shape)
    g_kv_norm = acc[offs[1] : offs[2]].reshape(mla_kv_norm.shape)
    g_g_norm = acc[offs[2] : offs[3]].reshape(hgrn_g_norm.shape)
    g_lbnd = acc[offs[3] : offs[4]].reshape(n_hgrn, HK)
    g_lb_full = lower_bounds_vjp(g_lbnd)[0]
    g_hgrn_lb = lax.dynamic_slice_in_dim(g_lb_full, k_me * lbw, lbw, axis=1)
    g_lng = lax.dynamic_slice_in_dim(acc[offs[4] : offs[5]].reshape(n_sub, D), k_me * dq, dq, axis=1).reshape(ln_g.shape)
    g_lnb = lax.dynamic_slice_in_dim(acc[offs[5] : offs[6]].reshape(n_sub, D), k_me * dq, dq, axis=1).reshape(ln_b.shape)

    weights = dict(mla_w_in=mla_w_in, mla_q_norm=mla_q_norm, mla_w_qb=mla_w_qb, mla_kv_norm=mla_kv_norm, mla_w_kvb=mla_w_kvb,
                   mla_w_o=mla_w_o, hgrn_lb=hgrn_lb, hgrn_w_in=hgrn_w_in, hgrn_g_norm=hgrn_g_norm, hgrn_w_o=hgrn_w_o,
                   ffn_w_in=ffn_w_in, ffn_w_out=ffn_w_out, ada_w=ada_w, ada_b=ada_b, ln_g=ln_g, ln_b=ln_b)
    moms = dict(mla_w_in=(m_mla_w_in, v_mla_w_in), mla_q_norm=(m_mla_q_norm, v_mla_q_norm), mla_w_qb=(m_mla_w_qb, v_mla_w_qb),
                mla_kv_norm=(m_mla_kv_norm, v_mla_kv_norm), mla_w_kvb=(m_mla_w_kvb, v_mla_w_kvb), mla_w_o=(m_mla_w_o, v_mla_w_o),
                hgrn_lb=(m_hgrn_lb, v_hgrn_lb), hgrn_w_in=(m_hgrn_w_in, v_hgrn_w_in), hgrn_g_norm=(m_hgrn_g_norm, v_hgrn_g_norm),
                hgrn_w_o=(m_hgrn_w_o, v_hgrn_w_o), ffn_w_in=(m_ffn_w_in, v_ffn_w_in), ffn_w_out=(m_ffn_w_out, v_ffn_w_out),
                ada_w=(m_ada_w, v_ada_w), ada_b=(m_ada_b, v_ada_b), ln_g=(m_ln_g, v_ln_g), ln_b=(m_ln_b, v_ln_b))
    grads = dict(mla_q_norm=(g_q_norm,), mla_kv_norm=(g_kv_norm,), hgrn_lb=(g_hgrn_lb,), hgrn_g_norm=(g_g_norm,),
                 ada_w=(g_ada_w,), ada_b=(g_ada_b,), ln_g=(g_lng,), ln_b=(g_lnb,), **g_main)
    res = {k: _adamw(weights[k], [g_.reshape(weights[k].shape) for g_ in grads[k]], moms[k][0], moms[k][1], f"adamw_{k}")
           for k in weights}
    order = list(weights)
    return (loss, grad_x, *[res[k][0] for k in order], *[res[k][1] for k in order], *[res[k][2] for k in order],
            *[res[k][3] for k in order])
```

```python
import functools

import jax
import jax.numpy as jnp
from jax import lax
from jax.experimental import pallas as pl
from jax.experimental.pallas import tpu as pltpu

F32 = jnp.float32
BF16 = jnp.bfloat16
SDS = jax.ShapeDtypeStruct
MESH = pl.DeviceIdType.MESH
HI = lax.Precision.HIGHEST
MID = lax.Precision.HIGH

MLA_HEADS, QK_NOPE, QK_ROPE, V_HEAD = 16, 64, 32, 64
Q_LORA, KV_LORA = 768, 256
QK_DIM = QK_NOPE + QK_ROPE
ROPE_THETA = 10000.0
HGRN_K = 128
HGRN_CHUNK = 64
HGRN_SUB = 32
HGRN_PAR = 2
LN_EPS, RMS_EPS = 1e-5, 1e-6
ADAM_LR, ADAM_B1, ADAM_B2, ADAM_EPS, ADAM_WD, ADAM_STEP = 0.001, 0.9, 0.999, 1e-08, 0.01, 10
NEG = -1e30

VMEM_LIMIT_BYTES = 56 * 1024 * 1024
LANES = 128


def _cparams(*sem):
    return pltpu.CompilerParams(dimension_semantics=sem if sem else None, vmem_limit_bytes=VMEM_LIMIT_BYTES)


def _pick_tile(n, cap):
    best = 0
    for t in range(LANES, min(n, cap) + 1, LANES):
        if n % t == 0:
            best = t
    return best if best else n


def _bdot(a, b):
    return jnp.dot(a.astype(BF16), b.astype(BF16), preferred_element_type=F32)


def _bdot_nt(a, b):
    return lax.dot_general(a.astype(BF16), b.astype(BF16), (((1,), (1,)), ((), ())), preferred_element_type=F32)


def _bdot_tn(a, b):
    return lax.dot_general(a.astype(BF16), b.astype(BF16), (((0,), (0,)), ((), ())), preferred_element_type=F32)


def _hdot(a, b):
    return jnp.dot(a, b, precision=HI, preferred_element_type=F32)


def _mdot(a, b):
    return jnp.dot(a, b, precision=MID, preferred_element_type=F32)


def _mdot_nt(a, b):
    return lax.dot_general(a, b, (((1,), (1,)), ((), ())), precision=MID, preferred_element_type=F32)


def _mdot_tn(a, b):
    return lax.dot_general(a, b, (((0,), (0,)), ((), ())), precision=MID, preferred_element_type=F32)


def _mm_nn(a, w, out_dtype, name):
    M, K = a.shape
    G, _, n = w.shape
    tm = min(512, M)
    tn = _pick_tile(n, 1536)
    nps = n // tn

    def body(a_ref, w_ref, o_ref):
        o_ref[...] = _bdot(a_ref[...], w_ref[...]).astype(o_ref.dtype)

    return pl.pallas_call(
        body,
        grid=(G * nps, M // tm),
        in_specs=[
            pl.BlockSpec((tm, K), lambda j, i: (i, 0)),
            pl.BlockSpec((None, K, tn), lambda j, i: (j // nps, 0, j % nps)),
        ],
        out_specs=pl.BlockSpec((tm, tn), lambda j, i: (i, j)),
        out_shape=SDS((M, G * n), out_dtype),
        name=name,
        compiler_params=_cparams("parallel", "parallel"),
    )(a, w)


def _mm_nt(a, w, out_dtype, name):
    M = a.shape[0]
    G, K, n = w.shape
    tm = min(512, M)
    tk = _pick_tile(K, 1536)

    def body(a_ref, w_ref, o_ref, acc_ref):
        s = pl.program_id(2)

        @pl.when(s == 0)
        def _():
            acc_ref[...] = jnp.zeros_like(acc_ref)

        acc_ref[...] += _bdot_nt(a_ref[...], w_ref[...])

        @pl.when(s == G - 1)
        def _():
            o_ref[...] = acc_ref[...].astype(o_ref.dtype)

    return pl.pallas_call(
        body,
        grid=(K // tk, M // tm, G),
        in_specs=[
            pl.BlockSpec((tm, n), lambda kb, i, s: (i, s)),
            pl.BlockSpec((None, tk, n), lambda kb, i, s: (s, kb, 0)),
        ],
        out_specs=pl.BlockSpec((tm, tk), lambda kb, i, s: (i, kb)),
        out_shape=SDS((M, K), out_dtype),
        scratch_shapes=[pltpu.VMEM((tm, tk), F32)],
        name=name,
        compiler_params=_cparams("parallel", "parallel", "arbitrary"),
    )(a, w)


def _mm_tn(a, d, G, out_dtype, name):
    T, K = a.shape
    n = d.shape[1] // G
    tk = _pick_tile(K, 256)
    tn = _pick_tile(n, 1536)
    nps = n // tn

    def body(a_ref, d_ref, o_ref):
        o_ref[...] = _bdot_tn(a_ref[...], d_ref[...]).astype(o_ref.dtype)

    return pl.pallas_call(
        body,
        grid=(G * nps, K // tk),
        in_specs=[
            pl.BlockSpec((T, tk), lambda j, i: (0, i)),
            pl.BlockSpec((T, tn), lambda j, i: (0, j)),
        ],
        out_specs=pl.BlockSpec((None, tk, tn), lambda j, i: (j // nps, i, j % nps)),
        out_shape=SDS((G, K, n), out_dtype),
        name=name,
        compiler_params=_cparams("parallel", "parallel"),
    )(a, d)


def _rows_call(body, name, B, S, ins, outs, ts=256):
    ts = min(ts, S)
    in_specs, args = [], []
    for arr, kind in ins:
        W = arr.shape[-1]
        if kind == "row":
            in_specs.append(pl.BlockSpec((None, ts, W), lambda b, s: (b, s, 0)))
        elif kind == "ex":
            in_specs.append(pl.BlockSpec((None, 1, W), lambda b, s: (b, 0, 0)))
        else:
            in_specs.append(pl.BlockSpec((1, W), lambda b, s: (0, 0)))
        args.append(arr)
    out_specs, out_shape = [], []
    for W, dt, kind in outs:
        if kind == "row":
            out_specs.append(pl.BlockSpec((None, ts, W), lambda b, s: (b, s, 0)))
            out_shape.append(SDS((B, S, W), dt))
        else:
            out_specs.append(pl.BlockSpec((None, 1, W), lambda b, s: (b, 0, 0)))
            out_shape.append(SDS((B, 1, W), dt))
    return pl.pallas_call(
        body,
        grid=(B, S // ts),
        in_specs=in_specs,
        out_specs=out_specs,
        out_shape=out_shape,
        name=name,
        compiler_params=_cparams("parallel", "arbitrary"),
    )(*args)


def _acc(ref, val):
    @pl.when(pl.program_id(1) == 0)
    def _():
        ref[...] = jnp.zeros_like(ref)

    ref[...] += val


def _mod_fn(x, sc, sh):
    return x * (1.0 + sc) + sh


def _ln_fn(alpha, x, y, gate, g, b):
    z = alpha * x + (1.0 + gate) * y
    mu = jnp.mean(z, -1, keepdims=True)
    var = jnp.mean(jnp.square(z - mu), -1, keepdims=True)
    return (z - mu) * lax.rsqrt(var + LN_EPS) * g + b


def _modulate(x, sc, sh, name):
    B, S, D = x.shape

    def body(x_ref, sc_ref, sh_ref, h_ref):
        h_ref[...] = _mod_fn(x_ref[...], sc_ref[...], sh_ref[...]).astype(BF16)

    return _rows_call(body, name, B, S, [(x, "row"), (sc, "ex"), (sh, "ex")], [(D, BF16, "row")])[0]


def _ln_fwd(alpha, x, y, gate, g, b, name):
    B, S, D = x.shape

    def body(x_ref, y_ref, gate_ref, g_ref, b_ref, o_ref):
        o_ref[...] = _ln_fn(alpha, x_ref[...], y_ref[...], gate_ref[...], g_ref[...], b_ref[...])

    return _rows_call(
        body, name, B, S, [(x, "row"), (y, "row"), (gate, "ex"), (g, "par"), (b, "par")], [(D, F32, "row")]
    )[0]


def _ln_bwd(alpha, dout, x, y, gate, g, b, name):
    B, S, D = x.shape

    def body(do_ref, x_ref, y_ref, gate_ref, g_ref, b_ref, dxr_ref, dy_ref, dgate_ref, dg_ref, db_ref):
        _, vjp = jax.vjp(
            functools.partial(_ln_fn, alpha), x_ref[...], y_ref[...], gate_ref[...], g_ref[...], b_ref[...]
        )
        dx, dy, dgate, dg, db = vjp(do_ref[...])
        dxr_ref[...] = dx
        dy_ref[...] = dy.astype(BF16)
        _acc(dgate_ref, dgate)
        _acc(dg_ref, dg)
        _acc(db_ref, db)

    return _rows_call(
        body,
        name,
        B,
        S,
        [(dout, "row"), (x, "row"), (y, "row"), (gate, "ex"), (g, "par"), (b, "par")],
        [(D, F32, "row"), (D, BF16, "row"), (D, F32, "acc"), (D, F32, "acc"), (D, F32, "acc")],
    )


def _mod_bwd(dh, dxr, x, sc, name):
    B, S, D = x.shape

    def body(dh_ref, dxr_ref, x_ref, sc_ref, dx_ref, dsc_ref, dsh_ref):
        dh_v = dh_ref[...]
        dx_ref[...] = dxr_ref[...] + dh_v * (1.0 + sc_ref[...])
        _acc(dsc_ref, jnp.sum(dh_v * x_ref[...], axis=0, keepdims=True))
        _acc(dsh_ref, jnp.sum(dh_v, axis=0, keepdims=True))

    return _rows_call(
        body,
        name,
        B,
        S,
        [(dh, "row"), (dxr, "row"), (x, "row"), (sc, "ex")],
        [(D, F32, "row"), (D, F32, "acc"), (D, F32, "acc")],
    )


def _loss_head(y, target, name):
    B, S, D = y.shape

    def body(y_ref, t_ref, l_ref, dy_ref):
        e = y_ref[...] - t_ref[...]
        dy_ref[...] = e * (1.0 / D)
        part = 0.5 * jnp.sum(jnp.sum(e * e, axis=1, keepdims=True) * (1.0 / D), axis=0, keepdims=True)
        _acc(l_ref, jnp.broadcast_to(part, (1, LANES)))

    loss, dy = _rows_call(
        body, name, B, S, [(y, "row"), (target, "row")], [(LANES, F32, "acc"), (D, F32, "row")]
    )
    return jnp.sum(loss[:, 0, 0]), dy


def _swiglu_fn(u):
    F = u.shape[-1] // 2
    return jax.nn.silu(u[:, :F]) * u[:, F:]


def _swiglu_fwd(u, name):
    B, S, F2 = u.shape

    def body(u_ref, a_ref):
        a_ref[...] = _swiglu_fn(u_ref[...]).astype(BF16)

    return _rows_call(body, name, B, S, [(u, "row")], [(F2 // 2, BF16, "row")])[0]


def _swiglu_bwd(u, da, name):
    B, S, F2 = u.shape

    def body(u_ref, da_ref, du_ref):
        _, vjp = jax.vjp(_swiglu_fn, u_ref[...])
        du_ref[...] = vjp(da_ref[...])[0].astype(BF16)

    return _rows_call(body, name, B, S, [(u, "row"), (da, "row")], [(F2, BF16, "row")])[0]


def _rms_fn(x, g):
    return x * lax.rsqrt(jnp.mean(jnp.square(x), -1, keepdims=True) + RMS_EPS) * g


def _mla_mid_fwd(proj, gq, gkv, name):
    B, S, _ = proj.shape

    def body(p_ref, gq_ref, gkv_ref, qn_ref, kvn_ref):
        p = p_ref[...]
        qn_ref[...] = _rms_fn(p[:, :Q_LORA], gq_ref[...]).astype(BF16)
        kvn_ref[...] = _rms_fn(p[:, Q_LORA : Q_LORA + KV_LORA], gkv_ref[...]).astype(BF16)

    return _rows_call(
        body, name, B, S, [(proj, "row"), (gq, "par"), (gkv, "par")], [(Q_LORA, BF16, "row"), (KV_LORA, BF16, "row")]
    )


def _mla_mid_bwd(proj, dqn, dkvn, dkr, gq, gkv, name):
    B, S, W = proj.shape

    def body(p_ref, dqn_ref, dkvn_ref, dkr_ref, gq_ref, gkv_ref, dp_ref, dgq_ref, dgkv_ref):
        p = p_ref[...]
        _, vq = jax.vjp(_rms_fn, p[:, :Q_LORA], gq_ref[...])
        dql, dgq = vq(dqn_ref[...])
        _, vkv = jax.vjp(_rms_fn, p[:, Q_LORA : Q_LORA + KV_LORA], gkv_ref[...])
        dkvl, dgkv = vkv(dkvn_ref[...])
        dp_ref[:, :Q_LORA] = dql.astype(BF16)
        dp_ref[:, Q_LORA : Q_LORA + KV_LORA] = dkvl.astype(BF16)
        dp_ref[:, Q_LORA + KV_LORA :] = dkr_ref[...].astype(BF16)
        _acc(dgq_ref, dgq)
        _acc(dgkv_ref, dgkv)

    return _rows_call(
        body,
        name,
        B,
        S,
        [(proj, "row"), (dqn, "row"), (dkvn, "row"), (dkr, "row"), (gq, "par"), (gkv, "par")],
        [(W, BF16, "row"), (Q_LORA, F32, "acc"), (KV_LORA, F32, "acc")],
    )


def _rope(x, cos, sin):
    h = QK_ROPE // 2
    x1, x2 = x[:, :h], x[:, h:]
    return jnp.concatenate([x1 * cos - x2 * sin, x1 * sin + x2 * cos], axis=1)


def _rope_t(dy, cos, sin):
    h = QK_ROPE // 2
    d1, d2 = dy[:, :h], dy[:, h:]
    return jnp.concatenate([d1 * cos + d2 * sin, d2 * cos - d1 * sin], axis=1)


def _heads_call(body, name, B, S, ins, outs, ts=256):
    ts = min(ts, S)
    in_specs, args = [], []
    for arr, kind in ins:
        if kind == "row":
            in_specs.append(pl.BlockSpec((None, ts, arr.shape[-1]), lambda b, s: (b, s, 0)))
        else:
            in_specs.append(pl.BlockSpec((arr.shape[0], None, ts, arr.shape[-1]), lambda b, s: (0, b, s, 0)))
        args.append(arr)
    out_specs, out_shape = [], []
    for shape, dt, kind in outs:
        if kind == "row":
            out_specs.append(pl.BlockSpec((None, ts, shape[-1]), lambda b, s: (b, s, 0)))
        else:
            out_specs.append(pl.BlockSpec((shape[0], None, ts, shape[-1]), lambda b, s: (0, b, s, 0)))
        out_shape.append(SDS(shape, dt))
    return pl.pallas_call(
        body,
        grid=(B, S // ts),
        in_specs=in_specs,
        out_specs=out_specs,
        out_shape=out_shape,
        name=name,
        compiler_params=_cparams("parallel", "parallel"),
    )(*args)


def _mla_prep_fwd(q, kv, proj, cos, sin, name):
    B, S, _ = q.shape
    H = MLA_HEADS

    def body(q_ref, kv_ref, p_ref, cos_ref, sin_ref, qh_ref, kh_ref, vh_ref):
        cos_v, sin_v = cos_ref[...], sin_ref[...]
        kr = _rope(p_ref[:, Q_LORA + KV_LORA :], cos_v, sin_v).astype(BF16)
        for h in range(H):
            qn = q_ref[:, h * QK_DIM : h * QK_DIM + QK_NOPE]
            qr = _rope(q_ref[:, h * QK_DIM + QK_NOPE : (h + 1) * QK_DIM], cos_v, sin_v)
            qh_ref[h] = jnp.concatenate([qn, qr], axis=1).astype(BF16)
            kn = kv_ref[:, h * 128 : h * 128 + QK_NOPE].astype(BF16)
            kh_ref[h] = jnp.concatenate([kn, kr], axis=1)
            vh_ref[h] = kv_ref[:, h * 128 + QK_NOPE : (h + 1) * 128].astype(BF16)

    return _heads_call(
        body,
        name,
        B,
        S,
        [(q, "row"), (kv, "row"), (proj, "row"), (cos, "row"), (sin, "row")],
        [((H, B, S, QK_DIM), BF16, "heads"), ((H, B, S, QK_DIM), BF16, "heads"), ((H, B, S, V_HEAD), BF16, "heads")],
    )


def _mla_prep_bwd(dqh, dkh, dvh, cos, sin, name):
    H, B, S, _ = dqh.shape

    def body(dqh_ref, dkh_ref, dvh_ref, cos_ref, sin_ref, dq_ref, dkv_ref, dkr_ref):
        cos_v, sin_v = cos_ref[...], sin_ref[...]
        dkr = jnp.zeros((cos_v.shape[0], QK_ROPE), F32)
        for h in range(H):
            dqv = dqh_ref[h].astype(F32)
            dq_ref[:, h * QK_DIM : h * QK_DIM + QK_NOPE] = dqv[:, :QK_NOPE].astype(BF16)
            dq_ref[:, h * QK_DIM + QK_NOPE : (h + 1) * QK_DIM] = _rope_t(dqv[:, QK_NOPE:], cos_v, sin_v).astype(BF16)
            dkv = dkh_ref[h].astype(F32)
            dkv_ref[:, h * 128 : h * 128 + QK_NOPE] = dkv[:, :QK_NOPE].astype(BF16)
            dkv_ref[:, h * 128 + QK_NOPE : (h + 1) * 128] = dvh_ref[h]
            dkr = dkr + dkv[:, QK_NOPE:]
        dkr_ref[...] = _rope_t(dkr, cos_v, sin_v)

    return _heads_call(
        body,
        name,
        B,
        S,
        [(dqh, "heads"), (dkh, "heads"), (dvh, "heads"), (cos, "row"), (sin, "row")],
        [((B, S, H * QK_DIM), BF16, "row"), ((B, S, H * 128), BF16, "row"), ((B, S, QK_ROPE), F32, "row")],
    )


LOG2E = 1.4426950408889634
ATTN_TILE = 512


def _tril_mask(t):
    return lax.broadcasted_iota(jnp.int32, (t, t), 0) >= lax.broadcasted_iota(jnp.int32, (t, t), 1)


def _attn_fwd(qh, kh, vh, name):
    H, B, S, _ = qh.shape
    t = min(ATTN_TILE, S)
    scale = QK_DIM**-0.5
    c2 = scale * LOG2E

    def body(q_ref, k_ref, v_ref, o_ref, lse_ref):
        i = pl.program_id(2)
        qs = [q_ref[0], q_ref[1]]

        def step(j, carry, diagonal):
            rows = pl.ds(pl.multiple_of(j * t, t), t)
            out = []
            for hh in range(2):
                m, l, acc = carry[hh]
                s = _bdot_nt(qs[hh], k_ref[hh, rows, :])
                if diagonal:
                    s = jnp.where(_tril_mask(t), s, NEG)
                m_new = jnp.maximum(m, jnp.max(s, axis=1, keepdims=True))
                p = jnp.exp2((s - m_new) * c2)
                a = jnp.exp2((m - m_new) * c2)
                l = a * l + jnp.sum(p, axis=1, keepdims=True)
                acc = a * acc + _bdot(p, v_ref[hh, rows, :])
                out.append((m_new, l, acc))
            return tuple(out)

        one = (jnp.full((t, 1), NEG, F32), jnp.zeros((t, 1), F32), jnp.zeros((t, V_HEAD), F32))
        carry = lax.fori_loop(0, i, lambda j, cy: step(j, cy, False), (one, one))
        carry = step(i, carry, True)
        outs = []
        for hh in range(2):
            m, l, acc = carry[hh]
            outs.append(acc / l)
            lse_ref[hh] = m * scale + jnp.log(l)
        o_ref[...] = jnp.concatenate(outs, axis=1).astype(BF16)

    return pl.pallas_call(
        body,
        grid=(B, H // 2, S // t),
        in_specs=[
            pl.BlockSpec((2, None, t, QK_DIM), lambda b, p, i: (p, b, i, 0)),
            pl.BlockSpec((2, None, S, QK_DIM), lambda b, p, i: (p, b, 0, 0)),
            pl.BlockSpec((2, None, S, V_HEAD), lambda b, p, i: (p, b, 0, 0)),
        ],
        out_specs=[
            pl.BlockSpec((None, t, 2 * V_HEAD), lambda b, p, i: (b, i, p)),
            pl.BlockSpec((2, None, t, 1), lambda b, p, i: (p, b, i, 0)),
        ],
        out_shape=[SDS((B, S, H * V_HEAD), BF16), SDS((H, B, S, 1), F32)],
        name=name,
        compiler_params=_cparams("parallel", "parallel", "arbitrary"),
    )(qh, kh, vh)


def _attn_bwd(qh, kh, vh, o, do, lse, name):
    H, B, S, _ = qh.shape
    t = min(ATTN_TILE, S)
    nq = S // t
    scale = QK_DIM**-0.5
    c2 = scale * LOG2E

    def body(q_ref, k_ref, v_ref, o_ref, do_ref, lse_ref, dq_ref, dk_ref, dv_ref, dq_acc, delta_ref, lse2_ref):
        prod = o_ref[...].astype(F32) * do_ref[...].astype(F32)
        for hh in range(2):
            delta_ref[hh] = jnp.sum(prod[:, hh * V_HEAD : (hh + 1) * V_HEAD], axis=1, keepdims=True)
            lse2_ref[hh] = lse_ref[hh] * LOG2E
        dq_acc[...] = jnp.zeros_like(dq_acc)

        def kloop(j, _):
            krows = pl.ds(pl.multiple_of(j * t, t), t)
            ks = [k_ref[0, krows, :], k_ref[1, krows, :]]
            vs = [v_ref[0, krows, :], v_ref[1, krows, :]]

            def qstep(i, carry, diagonal):
                qrows = pl.ds(pl.multiple_of(i * t, t), t)
                do_i = do_ref[qrows, :]
                out = []
                for hh in range(2):
                    dk, dv = carry[hh]
                    q = q_ref[hh, qrows, :]
                    do_h = do_i[:, hh * V_HEAD : (hh + 1) * V_HEAD]
                    s = _bdot_nt(q, ks[hh])
                    p = jnp.exp2(s * c2 - lse2_ref[hh, qrows, :])
                    if diagonal:
                        p = jnp.where(_tril_mask(t), p, 0.0)
                    dv = dv + _bdot_tn(p, do_h)
                    dp = _bdot_nt(do_h, vs[hh])
                    ds = (p * (dp - delta_ref[hh, qrows, :])).astype(BF16)
                    dk = dk + _bdot_tn(ds, q)
                    dq_acc[hh, qrows, :] += _bdot(ds, ks[hh])
                    out.append((dk, dv))
                return tuple(out)

            one = (jnp.zeros((t, QK_DIM), F32), jnp.zeros((t, V_HEAD), F32))
            carry = qstep(j, (one, one), True)
            carry = lax.fori_loop(j + 1, nq, lambda i, cy: qstep(i, cy, False), carry)
            for hh in range(2):
                dk_ref[hh, krows, :] = (carry[hh][0] * scale).astype(BF16)
                dv_ref[hh, krows, :] = carry[hh][1].astype(BF16)
            return 0

        lax.fori_loop(0, nq, kloop, 0)
        dq_ref[...] = (dq_acc[...] * scale).astype(BF16)

    hspec = lambda w: pl.BlockSpec((2, None, S, w), lambda b, p: (p, b, 0, 0))
    ospec = pl.BlockSpec((None, S, 2 * V_HEAD), lambda b, p: (b, 0, p))
    return pl.pallas_call(
        body,
        grid=(B, H // 2),
        in_specs=[hspec(QK_DIM), hspec(QK_DIM), hspec(V_HEAD), ospec, ospec, hspec(1)],
        out_specs=[hspec(QK_DIM), hspec(QK_DIM), hspec(V_HEAD)],
        out_shape=[SDS((H, B, S, QK_DIM), BF16), SDS((H, B, S, QK_DIM), BF16), SDS((H, B, S, V_HEAD), BF16)],
        scratch_shapes=[pltpu.VMEM((2, S, QK_DIM), F32), pltpu.VMEM((2, S, 1), F32), pltpu.VMEM((2, S, 1), F32)],
        name=name,
        compiler_params=_cparams("parallel", "parallel"),
    )(qh, kh, vh, o, do, lse)


def _hgrn_pre(q, fx, lb):
    sig = jax.nn.sigmoid(fx)
    f = lb + (1.0 - lb) * sig
    return jax.nn.silu(q), 1.0 - f, jnp.log(f)


def _hgrn_gate(o, gg, gn):
    return _rms_fn(o, gn) * jax.nn.silu(gg)


def _tri(n, lower):
    r = lax.broadcasted_iota(jnp.int32, (n, n), 0)
    c = lax.broadcasted_iota(jnp.int32, (n, n), 1)
    return ((r >= c) if lower else (r <= c)).astype(F32)


def _hgrn_intra_fwd(qs, k, v, b):
    C, SB = qs.shape[0], min(HGRN_SUB, qs.shape[0])
    ridx = lax.broadcasted_iota(jnp.int32, (SB, 1), 0)
    outs = []
    for i in range(C // SB):
        r0 = i * SB
        qi, ki, vi, bi = qs[r0 : r0 + SB], k[r0 : r0 + SB], v[r0 : r0 + SB], b[r0 : r0 + SB]
        acc = jnp.zeros((SB, v.shape[1]), F32)
        for s in range(SB):
            mask = ridx >= s
            e = jnp.exp(jnp.where(mask, bi - bi[s : s + 1], 0.0))
            a = jnp.sum(jnp.where(mask, qi * ki[s : s + 1] * e, 0.0), axis=1, keepdims=True)
            acc = acc + a * vi[s : s + 1]
        if i > 0:
            ref = bi[0:1]
            qt = qi * jnp.exp(bi - ref)
            kt = k[:r0] * jnp.exp(ref - b[:r0])
            acc = acc + _bdot(_mdot_nt(qt, kt), v[:r0])
        outs.append(acc)
    return jnp.concatenate(outs, axis=0)


def _hgrn_intra_bwd(qs, k, v, b, do):
    C, SB = qs.shape[0], min(HGRN_SUB, qs.shape[0])
    nb = C // SB
    ridx = lax.broadcasted_iota(jnp.int32, (SB, 1), 0)
    dq_p = [None] * nb
    dk_p = [jnp.zeros((SB, k.shape[1]), F32) for _ in range(nb)]
    dv_p = [jnp.zeros((SB, v.shape[1]), F32) for _ in range(nb)]
    for i in range(nb):
        r0 = i * SB
        qi, ki, vi, bi, doi = qs[r0 : r0 + SB], k[r0 : r0 + SB], v[r0 : r0 + SB], b[r0 : r0 + SB], do[r0 : r0 + SB]
        dqi = jnp.zeros_like(qi)
        dki = jnp.zeros_like(ki)
        dvi = jnp.zeros_like(vi)
        for s in range(SB):
            mask = ridx >= s
            e = jnp.where(mask, jnp.exp(jnp.where(mask, bi - bi[s : s + 1], 0.0)), 0.0)
            da = jnp.sum(doi * vi[s : s + 1], axis=1, keepdims=True)
            a = jnp.sum(qi * ki[s : s + 1] * e, axis=1, keepdims=True)
            dqi = dqi + da * (ki[s : s + 1] * e)
            dk_row = jnp.sum(da * qi * e, axis=0, keepdims=True)
            dv_row = jnp.sum(a * doi, axis=0, keepdims=True)
            dki = jnp.where(ridx == s, dki + dk_row, dki)
            dvi = jnp.where(ridx == s, dvi + dv_row, dvi)
        if i > 0:
            ref = bi[0:1]
            eq = jnp.exp(bi - ref)
            ek = jnp.exp(ref - b[:r0])
            qt = qi * eq
            kt = k[:r0] * ek
            A = _mdot_nt(qt, kt)
            dA = _bdot_nt(doi, v[:r0])
            dvl = _bdot_tn(A, doi)
            dqi = dqi + _mdot(dA, kt) * eq
            dkl = _mdot_tn(dA, qt) * ek
            for j in range(i):
                dk_p[j] = dk_p[j] + dkl[j * SB : (j + 1) * SB]
                dv_p[j] = dv_p[j] + dvl[j * SB : (j + 1) * SB]
        dq_p[i] = dqi
        dk_p[i] = dk_p[i] + dki
        dv_p[i] = dv_p[i] + dvi
    return jnp.concatenate(dq_p, axis=0), jnp.concatenate(dk_p, axis=0), jnp.concatenate(dv_p, axis=0)


def _hgrn_fwd(proj, lb, gn, name):
    B, S, W = proj.shape
    HK = W // 4
    H = HK // HGRN_K
    C = min(HGRN_CHUNK, S)
    N = S // C

    HP = HGRN_PAR if H % HGRN_PAR == 0 else 1
    WP = HP * HGRN_K

    def body(q_ref, f_ref, i_ref, g_ref, lb_ref, gn_ref, og_ref, o_ref, st_ref):
        gn_v = gn_ref[...]
        tril = _tri(C, True)

        def chunk(n, sts):
            rows = pl.ds(pl.multiple_of(n * C, C), C)
            out = []
            for hh in range(HP):
                ln = slice(hh * HGRN_K, (hh + 1) * HGRN_K)
                st = sts[hh]
                qs, k, g = _hgrn_pre(q_ref[rows, ln], f_ref[rows, ln], lb_ref[:, ln])
                v = i_ref[rows, ln]
                b = _hdot(tril, g)
                st_ref[hh, n] = st
                o = _hgrn_intra_fwd(qs, k, v, b) + _bdot_nt(qs * jnp.exp(b), st)
                bl = b[C - 1 : C]
                out.append(st * jnp.exp(bl) + _bdot_tn(v, k * jnp.exp(bl - b)))
                o_ref[rows, ln] = o
                og_ref[rows, ln] = _hgrn_gate(o, g_ref[rows, ln], gn_v).astype(BF16)
            return tuple(out)

        lax.fori_loop(0, N, chunk, tuple(jnp.zeros((HGRN_K, HGRN_K), F32) for _ in range(HP)))

    col = lambda part: pl.BlockSpec((None, S, WP), lambda b, h: (b, 0, part * (H // HP) + h))
    return pl.pallas_call(
        body,
        grid=(B, H // HP),
        in_specs=[col(0), col(1), col(2), col(3), pl.BlockSpec((1, WP), lambda b, h: (0, h)), pl.BlockSpec((1, HGRN_K), lambda b, h: (0, 0))],
        out_specs=[col(0), col(0), pl.BlockSpec((None, HP, N, HGRN_K, HGRN_K), lambda b, h: (b, h, 0, 0, 0))],
        out_shape=[SDS((B, S, HK), BF16), SDS((B, S, HK), F32), SDS((B, H, N, HGRN_K, HGRN_K), F32)],
        name=name,
        compiler_params=_cparams("parallel", "parallel"),
    )(proj, proj, proj, proj, lb, gn)


def _hgrn_bwd(proj, lb, gn, o_pre, states, dog, name):
    B, S, W = proj.shape
    HK = W // 4
    H = HK // HGRN_K
    C = min(HGRN_CHUNK, S)
    N = S // C

    HP = HGRN_PAR if H % HGRN_PAR == 0 else 1
    WP = HP * HGRN_K

    def body(q_ref, f_ref, i_ref, g_ref, lb_ref, gn_ref, o_ref, st_ref, dog_ref, dq_ref, df_ref, di_ref, dg_ref, dlb_ref, dgn_ref):
        gn_v = gn_ref[...]
        tril = _tri(C, True)
        triu = _tri(C, False)

        def chunk(idx, carry):
            n = N - 1 - idx
            rows = pl.ds(pl.multiple_of(n * C, C), C)
            out = []
            for hh in range(HP):
                ln = slice(hh * HGRN_K, (hh + 1) * HGRN_K)
                dst, dlb, dgn = carry[hh]
                (qs, k, g), pre_vjp = jax.vjp(_hgrn_pre, q_ref[rows, ln], f_ref[rows, ln], lb_ref[:, ln])
                v = i_ref[rows, ln]
                _, gate_vjp = jax.vjp(_hgrn_gate, o_ref[rows, ln], g_ref[rows, ln], gn_v)
                do, dgg, dgn_c = gate_vjp(dog_ref[rows, ln])
                b = _hdot(tril, g)
                st0 = st_ref[hh, n]
                eb = jnp.exp(b)
                bl = b[C - 1 : C]
                ebl = jnp.exp(bl)
                ekb = jnp.exp(bl - b)
                qe = qs * eb
                kt = k * ekb
                dqs, dk, dv = _hgrn_intra_bwd(qs, k, v, b, do)
                dqs = dqs + _bdot(do, st0) * eb
                dk = dk + _bdot(v, dst) * ekb
                dv = dv + _bdot_nt(kt, dst)
                st1 = st0 * ebl + _bdot_tn(v, kt)
                dbl = jnp.sum(st1 * dst, axis=0, keepdims=True)
                dst = dst * ebl + _bdot_tn(do, qe)
                dgl = _hdot(triu, qs * dqs - k * dk) + dbl
                dq_pre, dfx, dlb_c = pre_vjp((dqs, dk, dgl))
                dq_ref[rows, ln] = dq_pre.astype(BF16)
                df_ref[rows, ln] = dfx.astype(BF16)
                di_ref[rows, ln] = dv.astype(BF16)
                dg_ref[rows, ln] = dgg.astype(BF16)
                out.append((dst, dlb + dlb_c, dgn + dgn_c))
            return tuple(out)

        zero = jnp.zeros((1, HGRN_K), F32)
        one = (jnp.zeros((HGRN_K, HGRN_K), F32), zero, zero)
        res = lax.fori_loop(0, N, chunk, tuple(one for _ in range(HP)))
        for hh in range(HP):
            dlb_ref[hh] = res[hh][1]
            dgn_ref[hh] = res[hh][2]

    col = lambda part: pl.BlockSpec((None, S, WP), lambda b, h: (b, 0, part * (H // HP) + h))
    vec = pl.BlockSpec((None, HP, 1, HGRN_K), lambda b, h: (b, h, 0, 0))
    return pl.pallas_call(
        body,
        grid=(B, H // HP),
        in_specs=[
            col(0), col(1), col(2), col(3),
            pl.BlockSpec((1, WP), lambda b, h: (0, h)),
            pl.BlockSpec((1, HGRN_K), lambda b, h: (0, 0)),
            col(0),
            pl.BlockSpec((None, HP, N, HGRN_K, HGRN_K), lambda b, h: (b, h, 0, 0, 0)),
            col(0),
        ],
        out_specs=[col(0), col(0), col(0), col(0), vec, vec],
        out_shape=[SDS((B, S, HK), BF16)] * 4 + [SDS((B, H, 1, HGRN_K), F32)] * 2,
        name=name,
        compiler_params=_cparams("parallel", "parallel"),
    )(proj, proj, proj, proj, lb, gn, o_pre, states, dog)


def _ada_fwd(c_all, w, b, name):
    Bg, D = c_all.shape
    L, _, n = w.shape

    def body(c_ref, w_ref, b_ref, o_ref):
        o_ref[...] = _bdot(jax.nn.silu(c_ref[...]), w_ref[...]) + b_ref[...]

    return pl.pallas_call(
        body,
        grid=(L,),
        in_specs=[
            pl.BlockSpec((Bg, D), lambda l: (0, 0)),
            pl.BlockSpec((None, D, n), lambda l: (l, 0, 0)),
            pl.BlockSpec((None, 1, n), lambda l: (l, 0, 0)),
        ],
        out_specs=pl.BlockSpec((None, Bg, n), lambda l: (l, 0, 0)),
        out_shape=SDS((L, Bg, n), F32),
        name=name,
        compiler_params=_cparams("parallel"),
    )(c_all, w, b)


def _ada_bwd(c_all, dmod, name):
    Bg, D = c_all.shape
    L, _, n = dmod.shape

    def body(c_ref, d_ref, dw_ref, db_ref):
        d = d_ref[...]
        dw_ref[...] = _bdot_tn(jax.nn.silu(c_ref[...]), d)
        db_ref[...] = jnp.sum(d, axis=0, keepdims=True)

    return pl.pallas_call(
        body,
        grid=(L,),
        in_specs=[pl.BlockSpec((Bg, D), lambda l: (0, 0)), pl.BlockSpec((None, Bg, n), lambda l: (l, 0, 0))],
        out_specs=[pl.BlockSpec((None, D, n), lambda l: (l, 0, 0)), pl.BlockSpec((None, 1, n), lambda l: (l, 0, 0))],
        out_shape=[SDS((L, D, n), F32), SDS((L, 1, n), F32)],
        name=name,
        compiler_params=_cparams("parallel"),
    )(c_all, dmod)


def _adamw(w, gs, m, v, name):
    shape = w.shape
    cols = shape[-1]
    rows = w.size // cols
    tr = rows
    for cand in (512, 256, 128, 64, 32, 16, 8):
        if rows % cand == 0 and cand * cols * 4 <= 2 * 1024 * 1024:
            tr = cand
            break
    as2d = lambda a: a.reshape(rows, cols)
    ng = len(gs)
    c1 = 1.0 / (1.0 - ADAM_B1**ADAM_STEP)
    c2 = 1.0 / (1.0 - ADAM_B2**ADAM_STEP)

    def body(*refs):
        w_ref, m_ref, v_ref = refs[0], refs[1], refs[2]
        g_refs = refs[3 : 3 + ng]
        g_out, d_out, m_out, v_out = refs[3 + ng :]
        g = g_refs[0][...].astype(F32)
        for r in g_refs[1:]:
            g = g + r[...].astype(F32)
        m_new = ADAM_B1 * m_ref[...] + (1.0 - ADAM_B1) * g
        v_new = ADAM_B2 * v_ref[...] + (1.0 - ADAM_B2) * jnp.square(g)
        g_out[...] = g
        m_out[...] = m_new
        v_out[...] = v_new
        d_out[...] = -ADAM_LR * ((m_new * c1) / (jnp.sqrt(v_new * c2) + ADAM_EPS) + ADAM_WD * w_ref[...])

    spec = pl.BlockSpec((tr, cols), lambda i: (i, 0))
    outs = pl.pallas_call(
        body,
        grid=(rows // tr,),
        in_specs=[spec] * (3 + ng),
        out_specs=[spec] * 4,
        out_shape=[SDS((rows, cols), F32)] * 4,
        name=name,
        compiler_params=_cparams("parallel"),
    )(as2d(w), as2d(m), as2d(v), *[as2d(g) for g in gs])
    return tuple(o.reshape(shape) for o in outs)


def _sum4(own, recv, name):
    shape = own.shape
    cols = shape[-1]
    rows = own.size // cols
    tr = rows
    for cand in (512, 256, 128, 64, 32, 16):
        if rows % cand == 0 and cand * cols * 4 <= 2 * 1024 * 1024:
            tr = cand
            break

    def body(own_ref, recv_ref, o_ref):
        acc = own_ref[...].astype(F32)
        for r in range(3):
            acc = acc + recv_ref[r].astype(F32)
        o_ref[...] = acc

    out = pl.pallas_call(
        body,
        grid=(rows // tr,),
        in_specs=[pl.BlockSpec((tr, cols), lambda i: (i, 0)), pl.BlockSpec((3, tr, cols), lambda i: (0, i, 0))],
        out_specs=pl.BlockSpec((tr, cols), lambda i: (i, 0)),
        out_shape=SDS((rows, cols), F32),
        name=name,
        compiler_params=_cparams("parallel"),
    )(own.reshape(rows, cols), recv.reshape(3, rows, cols))
    return out.reshape(shape)


def _my_place():
    return lax.axis_index("x"), lax.axis_index("y"), lax.axis_index("c")


def _flip(v, bit):
    return 1 - v if bit else v


def _allgather8(x, name):
    r, n = x.shape

    def body(x_ref, o_ref, send_sems, recv_sems, local_sem):
        mx, my, mc = _my_place()
        me = 4 * mx + 2 * my + mc
        mine = pltpu.make_async_copy(x_ref, o_ref.at[me], local_sem)
        mine.start()
        sends = []
        for rel in range(1, 8):
            peer = (_flip(mx, rel & 4), _flip(my, rel & 2), _flip(mc, rel & 1))
            cp = pltpu.make_async_remote_copy(
                src_ref=x_ref, dst_ref=o_ref.at[me], send_sem=send_sems.at[rel - 1], recv_sem=recv_sems.at[rel - 1],
                device_id=peer, device_id_type=MESH,
            )
            cp.start()
            sends.append(cp)
        for rel in range(1, 8):
            px, py, pc = _flip(mx, rel & 4), _flip(my, rel & 2), _flip(mc, rel & 1)
            pltpu.make_async_remote_copy(
                src_ref=x_ref, dst_ref=o_ref.at[4 * px + 2 * py + pc], send_sem=send_sems.at[rel - 1],
                recv_sem=recv_sems.at[rel - 1], device_id=(px, py, pc), device_id_type=MESH,
            ).wait_recv()
        for cp in sends:
            cp.wait_send()
        mine.wait()

    return pl.pallas_call(
        body,
        out_shape=SDS((8, r, n), x.dtype),
        in_specs=[pl.BlockSpec(memory_space=pl.ANY)],
        out_specs=pl.BlockSpec(memory_space=pl.ANY),
        scratch_shapes=[pltpu.SemaphoreType.DMA((7,)), pltpu.SemaphoreType.DMA((7,)), pltpu.SemaphoreType.DMA],
        name=name,
    )(x)


_HBM = pl.BlockSpec(memory_space=pl.ANY)


_SEM = pl.BlockSpec(memory_space=pltpu.SEMAPHORE)
_HBM_ONLY = pl.BlockSpec(memory_space=pltpu.HBM)
_EFFECT = pltpu.SideEffectType.DATAFLOW_SIDE_EFFECTING


def _in_hbm(a):
    return pltpu.with_memory_space_constraint(a, pltpu.HBM)


def _gather_start(lands, after, name):
    n = len(lands)

    def body(*refs):
        land = refs[:n]
        send_sems, recv_sems = refs[n + 1], refs[n + 2]
        token = refs[-1]
        mx, my, mc = _my_place()
        for i in range(n):
            for rel in range(1, 4):
                pltpu.make_async_remote_copy(
                    src_ref=land[i].at[2 * mx + my], dst_ref=land[i].at[2 * mx + my],
                    send_sem=send_sems.at[3 * i + rel - 1], recv_sem=recv_sems.at[3 * i + rel - 1],
                    device_id=(_flip(mx, rel & 2), _flip(my, rel & 1), mc), device_id_type=MESH,
                ).start()
        token[...] = jnp.zeros_like(token)

    outs = pl.pallas_call(
        body,
        name=name,
        out_shape=(
            pltpu.SemaphoreType.DMA((3 * n,)), pltpu.SemaphoreType.DMA((3 * n,)),
            *[pltpu.HBM(a.shape, a.dtype) for a in lands], SDS((8, LANES), F32),
        ),
        in_specs=[_HBM_ONLY] * n + [_HBM],
        out_specs=(_SEM, _SEM, *[_HBM_ONLY] * n, pl.BlockSpec(memory_space=pltpu.VMEM)),
        input_output_aliases={i: 2 + i for i in range(n)},
        compiler_params=pltpu.CompilerParams(has_side_effects=_EFFECT),
    )(*[_in_hbm(a) for a in lands], after)
    return outs[0], outs[1], list(outs[2 : 2 + n]), outs[-1]


def _gather_wait(send_sems, recv_sems, lands, after, name):
    n = len(lands)

    def body(*refs):
        land = refs[:n]
        s_sems, r_sems = refs[n], refs[n + 1]
        mx, my, mc = _my_place()
        for i in range(n):
            for rel in range(1, 4):
                px, py = _flip(mx, rel & 2), _flip(my, rel & 1)
                cp = pltpu.make_async_remote_copy(
                    src_ref=land[i].at[2 * mx + my], dst_ref=land[i].at[2 * px + py],
                    send_sem=s_sems.at[3 * i + rel - 1], recv_sem=r_sems.at[3 * i + rel - 1],
                    device_id=(px, py, mc), device_id_type=MESH,
                )
                cp.wait_send()
                cp.wait_recv()

    outs = pl.pallas_call(
        body,
        name=name,
        out_shape=tuple(pltpu.HBM(a.shape, a.dtype) for a in lands),
        in_specs=[_HBM_ONLY] * n + [_SEM, _SEM, _HBM],
        out_specs=[_HBM_ONLY] * n,
        input_output_aliases={i: i for i in range(n)},
        compiler_params=pltpu.CompilerParams(has_side_effects=_EFFECT),
    )(*lands, send_sems, recv_sems, after)
    return list(outs)


def _scatter_start(slabs, lands, places, name):
    n = len(slabs)

    def body(*refs):
        ins, land = refs[:n], refs[n : 2 * n]
        send_sems, recv_sems = refs[2 * n], refs[2 * n + 1]
        token = refs[-1]
        mx, my, mc = _my_place()
        for i in range(n):
            for rel in range(1, 4):
                px, py = _flip(mx, rel & 2), _flip(my, rel & 1)
                pltpu.make_async_remote_copy(
                    src_ref=ins[i].at[2 * px + py], dst_ref=land[i].at[rel - 1, places[i]],
                    send_sem=send_sems.at[3 * i + rel - 1], recv_sem=recv_sems.at[3 * i + rel - 1],
                    device_id=(px, py, mc), device_id_type=MESH,
                ).start()
        token[...] = jnp.zeros_like(token)

    outs = pl.pallas_call(
        body,
        name=name,
        out_shape=(
            pltpu.SemaphoreType.DMA((3 * n,)), pltpu.SemaphoreType.DMA((3 * n,)),
            *[pltpu.HBM(a.shape, a.dtype) for a in slabs], *[pltpu.HBM(a.shape, a.dtype) for a in lands],
            SDS((8, LANES), F32),
        ),
        in_specs=[_HBM_ONLY] * (2 * n),
        out_specs=(_SEM, _SEM, *[_HBM_ONLY] * (2 * n), pl.BlockSpec(memory_space=pltpu.VMEM)),
        input_output_aliases={i: 2 + i for i in range(2 * n)},
        compiler_params=pltpu.CompilerParams(has_side_effects=_EFFECT),
    )(*[_in_hbm(a) for a in slabs], *[_in_hbm(a) for a in lands])
    return outs[0], outs[1], list(outs[2 : 2 + n]), list(outs[2 + n : 2 + 2 * n]), outs[-1]


def _scatter_wait(send_sems, recv_sems, slabs, lands, places, after, name):
    n = len(slabs)

    def body(*refs):
        ins, land = refs[:n], refs[n : 2 * n]
        s_sems, r_sems = refs[2 * n], refs[2 * n + 1]
        mx, my, mc = _my_place()
        for i in range(n):
            for rel in range(1, 4):
                px, py = _flip(mx, rel & 2), _flip(my, rel & 1)
                cp = pltpu.make_async_remote_copy(
                    src_ref=ins[i].at[2 * px + py], dst_ref=land[i].at[rel - 1, places[i]],
                    send_sem=s_sems.at[3 * i + rel - 1], recv_sem=r_sems.at[3 * i + rel - 1],
                    device_id=(px, py, mc), device_id_type=MESH,
                )
                cp.wait_send()
                cp.wait_recv()

    outs = pl.pallas_call(
        body,
        name=name,
        out_shape=(*[pltpu.HBM(a.shape, a.dtype) for a in slabs], *[pltpu.HBM(a.shape, a.dtype) for a in lands]),
        in_specs=[_HBM_ONLY] * (2 * n) + [_SEM, _SEM, _HBM],
        out_specs=[_HBM_ONLY] * (2 * n),
        input_output_aliases={i: i for i in range(2 * n)},
        compiler_params=pltpu.CompilerParams(has_side_effects=_EFFECT),
    )(*slabs, *lands, send_sems, recv_sems, after)
    return list(outs[:n]), list(outs[n:])


def _swap_sibling(parts, name):
    n = len(parts)

    def body(*refs):
        ins, outs = refs[:n], refs[n : 2 * n]
        send_sems, recv_sems = refs[2 * n :]
        mx, my, mc = _my_place()
        sends = []
        for i in range(n):
            cp = pltpu.make_async_remote_copy(
                src_ref=ins[i], dst_ref=outs[i], send_sem=send_sems.at[i], recv_sem=recv_sems.at[i],
                device_id=(mx, my, 1 - mc), device_id_type=MESH,
            )
            cp.start()
            sends.append(cp)
        for cp in sends:
            cp.wait_recv()
        for cp in sends:
            cp.wait_send()

    return pl.pallas_call(
        body,
        out_shape=[SDS(s.shape, s.dtype) for s in parts],
        in_specs=[_HBM] * n,
        out_specs=[_HBM] * n,
        scratch_shapes=[pltpu.SemaphoreType.DMA((n,)), pltpu.SemaphoreType.DMA((n,))],
        name=name,
    )(*parts)


def _pad_rows(a, rows):
    return jnp.pad(a, ((0, rows - a.shape[0]), (0, 0)))


def kernel(x, c, positions, mla_w_in, mla_q_norm, mla_w_qb, mla_kv_norm, mla_w_kvb, mla_w_o, hgrn_lb, hgrn_w_in, hgrn_g_norm, hgrn_w_o, ffn_w_in, ffn_w_out, ada_w, ada_b, ln_g, ln_b, loss_target, m_mla_w_in, m_mla_q_norm, m_mla_w_qb, m_mla_kv_norm, m_mla_w_kvb, m_mla_w_o, m_hgrn_lb, m_hgrn_w_in, m_hgrn_g_norm, m_hgrn_w_o, m_ffn_w_in, m_ffn_w_out, m_ada_w, m_ada_b, m_ln_g, m_ln_b, v_mla_w_in, v_mla_q_norm, v_mla_w_qb, v_mla_kv_norm, v_mla_w_kvb, v_mla_w_o, v_hgrn_lb, v_hgrn_w_in, v_hgrn_g_norm, v_hgrn_w_o, v_ffn_w_in, v_ffn_w_out, v_ada_w, v_ada_b, v_ln_g, v_ln_b):
    B, S, D = x.shape
    T = B * S
    depth = ada_w.shape[0]
    n_mla, n_hgrn = mla_w_in.shape[0], hgrn_w_in.shape[0]
    n_sub = 2 * depth
    alpha = (2.0 * depth) ** 0.25
    mx, my, mc = _my_place()
    me = 4 * mx + 2 * my + mc
    k_me = 2 * mx + my
    Bg = 8 * B
    HK = hgrn_w_o.shape[1] * 4
    dq = D // 4

    lbw = hgrn_lb.shape[1]
    first = jnp.zeros((8, max(D, 4 * lbw)), F32)
    first = first.at[:B, :D].set(c).at[B : B + n_hgrn, :lbw].set(hgrn_lb)
    first_all = _allgather8(first, "gather_cond")
    c_all = first_all[:, :B, :D].reshape(Bg, D)
    lb_logits = jnp.concatenate([first_all[2 * k, B : B + n_hgrn, :lbw] for k in range(4)], axis=1)

    def lower_bounds_fn(logits):
        soft = jax.nn.softmax(logits, axis=0)
        return jnp.cumsum(soft, axis=0) - soft[0]

    lower_bounds, lower_bounds_vjp = jax.vjp(lower_bounds_fn, lb_logits)

    n_ada = ada_w.shape[-1]
    mod_part = _ada_fwd(c_all, ada_w.reshape(n_sub, D, n_ada), ada_b.reshape(n_sub, 1, n_ada), "ada_fwd")
    mod_all = _allgather8(mod_part.reshape(n_sub * Bg, n_ada), "gather_mod").reshape(8, n_sub, Bg, n_ada)
    mod = jnp.concatenate([mod_all[2 * k] for k in range(4)], axis=-1)
    mod = lax.dynamic_slice_in_dim(mod, me * B, B, axis=1)
    shift = [mod[j, :, None, :D] for j in range(n_sub)]
    scale = [mod[j, :, None, D : 2 * D] for j in range(n_sub)]
    gate = [mod[j, :, None, 2 * D :] for j in range(n_sub)]

    ln_rows = 2 * n_sub
    ln_local = _pad_rows(jnp.concatenate([ln_g.reshape(n_sub, dq), ln_b.reshape(n_sub, dq)], axis=0), -(-ln_rows // 8) * 8)
    ln_pad = jnp.zeros((ln_local.shape[0], -(-dq // LANES) * LANES), F32).at[:, :dq].set(ln_local)
    ln_all = _allgather8(ln_pad, "gather_ln")
    ln_full = jnp.concatenate([ln_all[2 * k, :ln_rows, :dq] for k in range(4)], axis=1)
    lng = [ln_full[j][None, :] for j in range(n_sub)]
    lnb = [ln_full[n_sub + j][None, :] for j in range(n_sub)]

    main = dict(mla_w_in=mla_w_in, mla_w_qb=mla_w_qb, mla_w_kvb=mla_w_kvb, mla_w_o=mla_w_o, hgrn_w_in=hgrn_w_in,
                hgrn_w_o=hgrn_w_o, ffn_w_in=ffn_w_in, ffn_w_out=ffn_w_out)
    names = list(main)

    def group_kinds(layer, part):
        if part:
            return [("ffn_w_in", layer), ("ffn_w_out", layer)]
        mixer = ["mla_w_in", "mla_w_qb", "mla_w_kvb", "mla_w_o"] if layer % 2 == 0 else ["hgrn_w_in", "hgrn_w_o"]
        return [(k, layer // 2) for k in mixer]

    gathers = {}
    after = mod_all[0, 0, :8, :LANES] + ln_all[0, :8, :LANES]
    for layer in range(depth):
        for part in range(2):
            lands = [lax.dynamic_update_index_in_dim(lax.empty((4,) + main[k].shape[1:], BF16), main[k][i].astype(BF16), k_me, 0)
                     for k, i in group_kinds(layer, part)]
            ssem, rsem, lands, after = _gather_start(lands, after, f"gather_start_l{layer}p{part}")
            gathers[layer, part] = (ssem, rsem, lands)
    scale[0] = scale[0] + after[0, 0]

    def row_w(g):
        return g.reshape(1, g.shape[0] * g.shape[1], g.shape[2])

    def full_w_in(g):
        return jnp.transpose(g, (1, 0, 2)).reshape(1, g.shape[1], 4 * g.shape[2])

    ang = positions.astype(F32)[..., None] * (ROPE_THETA ** (-jnp.arange(0, QK_ROPE, 2, dtype=F32) / QK_ROPE))
    cos, sin = jnp.cos(ang), jnp.sin(ang)

    gq = [mla_q_norm[j][None, :] for j in range(n_mla)]
    gkv = [mla_kv_norm[j][None, :] for j in range(n_mla)]
    gn = [hgrn_g_norm[j][None, :] for j in range(n_hgrn)]

    def r2(a):
        return a.reshape(T, a.shape[-1])

    def r3(a):
        return a.reshape(B, S, a.shape[-1])

    saved = []
    xs = x
    for layer in range(depth):
        j = layer // 2
        sub = 2 * layer
        tag = f"l{layer}"
        ssem, rsem, lands = gathers[layer, 0]
        lands = _gather_wait(ssem, rsem, lands, xs if layer else scale[0], f"gather_wait_{tag}p0")
        wl = {k: g for (k, _), g in zip(group_kinds(layer, 0), lands)}
        h = _modulate(xs, scale[sub], shift[sub], f"mod_{tag}a")
        if layer % 2 == 0:
            wl["mla_w_in"] = full_w_in(wl["mla_w_in"])
            proj = r3(_mm_nn(r2(h), wl["mla_w_in"], F32, f"mla_in_{tag}"))
            qn, kvn = _mla_mid_fwd(proj, gq[j], gkv[j], f"mla_mid_{tag}")
            q = r3(_mm_nn(r2(qn), wl["mla_w_qb"], F32, f"mla_qb_{tag}"))
            kv = r3(_mm_nn(r2(kvn), wl["mla_w_kvb"], F32, f"mla_kvb_{tag}"))
            qh, kh, vh = _mla_prep_fwd(q, kv, proj, cos, sin, f"mla_prep_{tag}")
            o, lse = _attn_fwd(qh, kh, vh, f"attn_{tag}")
            wl["mla_w_o"] = row_w(wl["mla_w_o"])
            y = r3(_mm_nn(r2(o), wl["mla_w_o"], F32, f"mla_o_{tag}"))
            mix = (h, proj, qn, kvn, qh, kh, vh, o, lse)
        else:
            proj = r3(_mm_nn(r2(h), wl["hgrn_w_in"], F32, f"hgrn_in_{tag}"))
            og, o_pre, states = _hgrn_fwd(proj, lower_bounds[j][None, :], gn[j], f"hgrn_{tag}")
            wl["hgrn_w_o"] = row_w(wl["hgrn_w_o"])
            y = r3(_mm_nn(r2(og), wl["hgrn_w_o"], F32, f"hgrn_o_{tag}"))
            mix = (h, proj, og, o_pre, states)
        x1 = _ln_fwd(alpha, xs, y, gate[sub], lng[sub], lnb[sub], f"ln_{tag}a")
        ssem, rsem, lands = gathers[layer, 1]
        lands = _gather_wait(ssem, rsem, lands, x1, f"gather_wait_{tag}p1")
        wl.update({k: g for (k, _), g in zip(group_kinds(layer, 1), lands)})
        h2 = _modulate(x1, scale[sub + 1], shift[sub + 1], f"mod_{tag}b")
        u = r3(_mm_nn(r2(h2), wl["ffn_w_in"], F32, f"ffn_in_{tag}"))
        a = _swiglu_fwd(u, f"swiglu_{tag}")
        wl["ffn_w_out"] = row_w(wl["ffn_w_out"])
        y2 = r3(_mm_nn(r2(a), wl["ffn_w_out"], F32, f"ffn_out_{tag}"))
        x2 = _ln_fwd(alpha, x1, y2, gate[sub + 1], lng[sub + 1], lnb[sub + 1], f"ln_{tag}b")
        saved.append((xs, y, x1, y2, mix, h2, u, a, wl))
        xs = x2

    loss_local, dout = _loss_head(xs, loss_target, "loss_head")
    loss = lax.psum(loss_local, ("x", "y", "c"))

    gw = {k: [None] * main[k].shape[0] for k in names}
    land = {k: lax.empty((3,) + main[k].shape, BF16) for k in names}
    scatters = []
    d_shift, d_scale, d_gate = [None] * n_sub, [None] * n_sub, [None] * n_sub
    d_lng, d_lnb = [None] * n_sub, [None] * n_sub
    d_gq, d_gkv, d_gn, d_lbnd = [None] * n_mla, [None] * n_mla, [None] * n_hgrn, [None] * n_hgrn

    def rows4(g):
        return g.reshape(4, g.shape[1] // 4, g.shape[2])

    def start_scatter(layer, part, token_to):
        kinds = group_kinds(layer, part)
        ssem, rsem, slabs_t, lands_t, token = _scatter_start(
            [gw[k][i] for k, i in kinds], [land[k] for k, _ in kinds], [i for _, i in kinds], f"scatter_start_l{layer}p{part}")
        for (k, i), s_t, l_t in zip(kinds, slabs_t, lands_t):
            gw[k][i], land[k] = s_t, l_t
        scatters.append((layer, part, ssem, rsem))
        if token_to is not None:
            gate[token_to] = gate[token_to] + token[0, 0]

    for layer in reversed(range(depth)):
        j = layer // 2
        sub = 2 * layer
        tag = f"l{layer}"
        xs, y, x1, y2, mix, h2, u, a, wl = saved[layer]
        dxr, dy2, d_gate[sub + 1], d_lng[sub + 1], d_lnb[sub + 1] = _ln_bwd(
            alpha, dout, x1, y2, gate[sub + 1], lng[sub + 1], lnb[sub + 1], f"ln_bwd_{tag}b")
        da = r3(_mm_nt(r2(dy2), wl["ffn_w_out"], F32, f"ffn_out_dx_{tag}"))
        gw["ffn_w_out"][layer] = rows4(_mm_tn(r2(a), r2(dy2), 1, BF16, f"ffn_out_dw_{tag}"))
        du = _swiglu_bwd(u, da, f"swiglu_bwd_{tag}")
        dh2 = r3(_mm_nt(r2(du), wl["ffn_w_in"], F32, f"ffn_in_dx_{tag}"))
        gw["ffn_w_in"][layer] = _mm_tn(r2(h2), r2(du), 4, BF16, f"ffn_in_dw_{tag}")
        start_scatter(layer, 1, sub)
        dout, d_scale[sub + 1], d_shift[sub + 1] = _mod_bwd(dh2, dxr, x1, scale[sub + 1], f"mod_bwd_{tag}b")
        dxr, dy, d_gate[sub], d_lng[sub], d_lnb[sub] = _ln_bwd(
            alpha, dout, xs, y, gate[sub], lng[sub], lnb[sub], f"ln_bwd_{tag}a")
        if layer % 2 == 0:
            h, proj, qn, kvn, qh, kh, vh, o, lse = mix
            do = r3(_mm_nt(r2(dy), wl["mla_w_o"], BF16, f"mla_o_dx_{tag}"))
            gw["mla_w_o"][j] = rows4(_mm_tn(r2(o), r2(dy), 1, BF16, f"mla_o_dw_{tag}"))
            dqh, dkh, dvh = _attn_bwd(qh, kh, vh, o, do, lse, f"attn_bwd_{tag}")
            dq_, dkv_, dkr = _mla_prep_bwd(dqh, dkh, dvh, cos, sin, f"mla_prep_bwd_{tag}")
            dqn = r3(_mm_nt(r2(dq_), wl["mla_w_qb"], F32, f"mla_qb_dx_{tag}"))
            gw["mla_w_qb"][j] = _mm_tn(r2(qn), r2(dq_), 4, BF16, f"mla_qb_dw_{tag}")
            dkvn = r3(_mm_nt(r2(dkv_), wl["mla_w_kvb"], F32, f"mla_kvb_dx_{tag}"))
            gw["mla_w_kvb"][j] = _mm_tn(r2(kvn), r2(dkv_), 4, BF16, f"mla_kvb_dw_{tag}")
            dproj, dgq_, dgkv_ = _mla_mid_bwd(proj, dqn, dkvn, dkr, gq[j], gkv[j], f"mla_mid_bwd_{tag}")
            d_gq[j], d_gkv[j] = dgq_.sum(0), dgkv_.sum(0)
            dh = r3(_mm_nt(r2(dproj), wl["mla_w_in"], F32, f"mla_in_dx_{tag}"))
            gwin = _mm_tn(r2(h), r2(dproj), 1, BF16, f"mla_in_dw_{tag}")[0]
            gw["mla_w_in"][j] = jnp.transpose(gwin.reshape(gwin.shape[0], 4, gwin.shape[1] // 4), (1, 0, 2))
        else:
            h, proj, og, o_pre, states = mix
            dog = r3(_mm_nt(r2(dy), wl["hgrn_w_o"], F32, f"hgrn_o_dx_{tag}"))
            gw["hgrn_w_o"][j] = rows4(_mm_tn(r2(og), r2(dy), 1, BF16, f"hgrn_o_dw_{tag}"))
            dq_, df_, di_, dg_, dlb_, dgn_ = _hgrn_bwd(proj, lower_bounds[j][None, :], gn[j], o_pre, states, dog, f"hgrn_bwd_{tag}")
            dproj = jnp.concatenate([dq_, df_, di_, dg_], axis=-1)
            d_lbnd[j] = dlb_.sum(0).reshape(1, HK)
            d_gn[j] = dgn_.sum((0, 1))
            dh = r3(_mm_nt(r2(dproj), wl["hgrn_w_in"], F32, f"hgrn_in_dx_{tag}"))
            gw["hgrn_w_in"][j] = _mm_tn(r2(h), r2(dproj), 4, BF16, f"hgrn_in_dw_{tag}")
        start_scatter(layer, 0, sub - 1 if layer else None)
        dout, d_scale[sub], d_shift[sub] = _mod_bwd(dh, dxr, xs, scale[sub], f"mod_bwd_{tag}a")
    grad_x = dout

    for layer, part, ssem, rsem in scatters:
        kinds = group_kinds(layer, part)
        slabs_t, lands_t = _scatter_wait(
            ssem, rsem, [gw[k][i] for k, i in kinds], [land[k] for k, _ in kinds], [i for _, i in kinds], grad_x,
            f"scatter_wait_l{layer}p{part}")
        for (k, i), s_t, l_t in zip(kinds, slabs_t, lands_t):
            gw[k][i], land[k] = s_t, l_t
    sums = [_sum4(jnp.stack([lax.dynamic_index_in_dim(g, k_me, 0, keepdims=False) for g in gw[k]]), land[k], f"sum4_{k}")
            for k in names]
    others = _swap_sibling(sums, "swap_sums")
    g_main = {k: (a_, b_) for k, a_, b_ in zip(names, sums, others)}

    dmod = jnp.stack([jnp.concatenate([d_shift[s_][:, 0], d_scale[s_][:, 0], d_gate[s_][:, 0]], axis=-1) for s_ in range(n_sub)])
    dmod_rows = _pad_rows(dmod.reshape(n_sub * B, 3 * D), -(-n_sub * B // 8) * 8)
    dmod_all = _allgather8(dmod_rows, "gather_dmod")[:, : n_sub * B].reshape(8, n_sub, B, 3 * D)
    dmod_all = jnp.transpose(dmod_all, (1, 0, 2, 3)).reshape(n_sub, Bg, 3 * D)
    dmod_mine = lax.dynamic_slice_in_dim(dmod_all, k_me * n_ada, n_ada, axis=2)
    g_ada_w, g_ada_b = _ada_bwd(c_all, dmod_mine, "ada_bwd")
    g_ada_w = g_ada_w.reshape(ada_w.shape)
    g_ada_b = g_ada_b.reshape(ada_b.shape)

    small = [jnp.stack(d_gq).reshape(-1), jnp.stack(d_gkv).reshape(-1), jnp.stack(d_gn).reshape(-1),
             jnp.stack(d_lbnd).reshape(-1), jnp.stack([d.sum(0) for d in d_lng]).reshape(-1),
             jnp.stack([d.sum(0) for d in d_lnb]).reshape(-1)]
    sizes = [s_.shape[0] for s_ in small]
    flat = jnp.concatenate(small)
    rows_small = -(-flat.shape[0] // (8 * LANES)) * 8
    flat = jnp.pad(flat, (0, rows_small * LANES - flat.shape[0])).reshape(rows_small, LANES)
    tot = _allgather8(flat, "gather_small")
    acc = tot[0]
    for d in range(1, 8):
        acc = acc + tot[d]
    acc = acc.reshape(-1)
    offs = [0]
    for s_ in sizes:
        offs.append(offs[-1] + s_)
    g_q_norm = acc[offs[0] : offs[1]].reshape(mla_q_norm.shape)
    g_kv_norm = acc[offs[1] : offs[2]].reshape(mla_kv_norm.shape)
    g_g_norm = acc[offs[2] : offs[3]].reshape(hgrn_g_norm.shape)
    g_lbnd = acc[offs[3] : offs[4]].reshape(n_hgrn, HK)
    g_lb_full = lower_bounds_vjp(g_lbnd)[0]
    g_hgrn_lb = lax.dynamic_slice_in_dim(g_lb_full, k_me * lbw, lbw, axis=1)
    g_lng = lax.dynamic_slice_in_dim(acc[offs[4] : offs[5]].reshape(n_sub, D), k_me * dq, dq, axis=1).reshape(ln_g.shape)
    g_lnb = lax.dynamic_slice_in_dim(acc[offs[5] : offs[6]].reshape(n_sub, D), k_me * dq, dq, axis=1).reshape(ln_b.shape)

    weights = dict(mla_w_in=mla_w_in, mla_q_norm=mla_q_norm, mla_w_qb=mla_w_qb, mla_kv_norm=mla_kv_norm, mla_w_kvb=mla_w_kvb,
                   mla_w_o=mla_w_o, hgrn_lb=hgrn_lb, hgrn_w_in=hgrn_w_in, hgrn_g_norm=hgrn_g_norm, hgrn_w_o=hgrn_w_o,
                   ffn_w_in=ffn_w_in, ffn_w_out=ffn_w_out, ada_w=ada_w, ada_b=ada_b, ln_g=ln_g, ln_b=ln_b)
    moms = dict(mla_w_in=(m_mla_w_in, v_mla_w_in), mla_q_norm=(m_mla_q_norm, v_mla_q_norm), mla_w_qb=(m_mla_w_qb, v_mla_w_qb),
                mla_kv_norm=(m_mla_kv_norm, v_mla_kv_norm), mla_w_kvb=(m_mla_w_kvb, v_mla_w_kvb), mla_w_o=(m_mla_w_o, v_mla_w_o),
                hgrn_lb=(m_hgrn_lb, v_hgrn_lb), hgrn_w_in=(m_hgrn_w_in, v_hgrn_w_in), hgrn_g_norm=(m_hgrn_g_norm, v_hgrn_g_norm),
                hgrn_w_o=(m_hgrn_w_o, v_hgrn_w_o), ffn_w_in=(m_ffn_w_in, v_ffn_w_in), ffn_w_out=(m_ffn_w_out, v_ffn_w_out),
                ada_w=(m_ada_w, v_ada_w), ada_b=(m_ada_b, v_ada_b), ln_g=(m_ln_g, v_ln_g), ln_b=(m_ln_b, v_ln_b))
    grads = dict(mla_q_norm=(g_q_norm,), mla_kv_norm=(g_kv_norm,), hgrn_lb=(g_hgrn_lb,), hgrn_g_norm=(g_g_norm,),
                 ada_w=(g_ada_w,), ada_b=(g_ada_b,), ln_g=(g_lng,), ln_b=(g_lnb,), **g_main)
    res = {k: _adamw(weights[k], [g_.reshape(weights[k].shape) for g_ in grads[k]], moms[k][0], moms[k][1], f"adamw_{k}")
           for k in weights}
    order = list(weights)
    return (loss, grad_x, *[res[k][0] for k in order], *[res[k][1] for k in order], *[res[k][2] for k in order],
            *[res[k][3] for k in order])
```

```python
import functools

import jax
import jax.numpy as jnp
from jax import lax
from jax.experimental import pallas as pl
from jax.experimental.pallas import tpu as pltpu

F32 = jnp.float32
BF16 = jnp.bfloat16
SDS = jax.ShapeDtypeStruct
MESH = pl.DeviceIdType.MESH
HI = lax.Precision.HIGHEST
MID = lax.Precision.HIGH

MLA_HEADS, QK_NOPE, QK_ROPE, V_HEAD = 16, 64, 32, 64
Q_LORA, KV_LORA = 768, 256
QK_DIM = QK_NOPE + QK_ROPE
ROPE_THETA = 10000.0
HGRN_K = 128
HGRN_CHUNK = 64
HGRN_SUB = 32
HGRN_PAR = 2
LN_EPS, RMS_EPS = 1e-5, 1e-6
ADAM_LR, ADAM_B1, ADAM_B2, ADAM_EPS, ADAM_WD, ADAM_STEP = 0.001, 0.9, 0.999, 1e-08, 0.01, 10
NEG = -1e30

VMEM_LIMIT_BYTES = 56 * 1024 * 1024
LANES = 128
SUBLANES = 8


def _cparams(*sem):
    return pltpu.CompilerParams(dimension_semantics=sem if sem else None, vmem_limit_bytes=VMEM_LIMIT_BYTES)


def _pick_tile(n, cap):
    best = 0
    for t in range(LANES, min(n, cap) + 1, LANES):
        if n % t == 0:
            best = t
    return best if best else n


def _bdot(a, b):
    return jnp.dot(a.astype(BF16), b.astype(BF16), preferred_element_type=F32)


def _bdot_nt(a, b):
    return lax.dot_general(a.astype(BF16), b.astype(BF16), (((1,), (1,)), ((), ())), preferred_element_type=F32)


def _bdot_tn(a, b):
    return lax.dot_general(a.astype(BF16), b.astype(BF16), (((0,), (0,)), ((), ())), preferred_element_type=F32)


def _hdot(a, b):
    return jnp.dot(a, b, precision=HI, preferred_element_type=F32)


def _mdot(a, b):
    return jnp.dot(a, b, precision=MID, preferred_element_type=F32)


def _mdot_nt(a, b):
    return lax.dot_general(a, b, (((1,), (1,)), ((), ())), precision=MID, preferred_element_type=F32)


def _mdot_tn(a, b):
    return lax.dot_general(a, b, (((0,), (0,)), ((), ())), precision=MID, preferred_element_type=F32)


def _mm_nn(a, w, out_dtype, name):
    M, K = a.shape
    G, _, n = w.shape
    tm = min(512, M)
    tn = _pick_tile(n, 1536)
    nps = n // tn

    def body(a_ref, w_ref, o_ref):
        o_ref[...] = _bdot(a_ref[...], w_ref[...]).astype(o_ref.dtype)

    return pl.pallas_call(
        body,
        grid=(G * nps, M // tm),
        in_specs=[
            pl.BlockSpec((tm, K), lambda j, i: (i, 0)),
            pl.BlockSpec((None, K, tn), lambda j, i: (j // nps, 0, j % nps)),
        ],
        out_specs=pl.BlockSpec((tm, tn), lambda j, i: (i, j)),
        out_shape=SDS((M, G * n), out_dtype),
        name=name,
        compiler_params=_cparams("parallel", "parallel"),
    )(a, w)


def _mm_nt(a, w, out_dtype, name):
    M = a.shape[0]
    G, K, n = w.shape
    tm = min(512, M)
    tk = _pick_tile(K, 1536)

    def body(a_ref, w_ref, o_ref, acc_ref):
        s = pl.program_id(2)

        @pl.when(s == 0)
        def _():
            acc_ref[...] = jnp.zeros_like(acc_ref)

        acc_ref[...] += _bdot_nt(a_ref[...], w_ref[...])

        @pl.when(s == G - 1)
        def _():
            o_ref[...] = acc_ref[...].astype(o_ref.dtype)

    return pl.pallas_call(
        body,
        grid=(K // tk, M // tm, G),
        in_specs=[
            pl.BlockSpec((tm, n), lambda kb, i, s: (i, s)),
            pl.BlockSpec((None, tk, n), lambda kb, i, s: (s, kb, 0)),
        ],
        out_specs=pl.BlockSpec((tm, tk), lambda kb, i, s: (i, kb)),
        out_shape=SDS((M, K), out_dtype),
        scratch_shapes=[pltpu.VMEM((tm, tk), F32)],
        name=name,
        compiler_params=_cparams("parallel", "parallel", "arbitrary"),
    )(a, w)


def _mm_tn(a, d, G, out_dtype, name):
    T, K = a.shape
    n = d.shape[1] // G
    tk = _pick_tile(K, 256)
    tn = _pick_tile(n, 1536)
    nps = n // tn

    def body(a_ref, d_ref, o_ref):
        o_ref[...] = _bdot_tn(a_ref[...], d_ref[...]).astype(o_ref.dtype)

    return pl.pallas_call(
        body,
        grid=(G * nps, K // tk),
        in_specs=[
            pl.BlockSpec((T, tk), lambda j, i: (0, i)),
            pl.BlockSpec((T, tn), lambda j, i: (0, j)),
        ],
        out_specs=pl.BlockSpec((None, tk, tn), lambda j, i: (j // nps, i, j % nps)),
        out_shape=SDS((G, K, n), out_dtype),
        name=name,
        compiler_params=_cparams("parallel", "parallel"),
    )(a, d)


def _rows_call(body, name, B, S, ins, outs, ts=256):
    ts = min(ts, S)
    in_specs, args = [], []
    for arr, kind in ins:
        W = arr.shape[-1]
        if kind == "row":
            in_specs.append(pl.BlockSpec((None, ts, W), lambda b, s: (b, s, 0)))
        elif kind == "ex":
            in_specs.append(pl.BlockSpec((None, 1, W), lambda b, s: (b, 0, 0)))
        else:
            in_specs.append(pl.BlockSpec((1, W), lambda b, s: (0, 0)))
        args.append(arr)
    out_specs, out_shape = [], []
    for W, dt, kind in outs:
        if kind == "row":
            out_specs.append(pl.BlockSpec((None, ts, W), lambda b, s: (b, s, 0)))
            out_shape.append(SDS((B, S, W), dt))
        else:
            out_specs.append(pl.BlockSpec((None, 1, W), lambda b, s: (b, 0, 0)))
            out_shape.append(SDS((B, 1, W), dt))
    return pl.pallas_call(
        body,
        grid=(B, S // ts),
        in_specs=in_specs,
        out_specs=out_specs,
        out_shape=out_shape,
        name=name,
        compiler_params=_cparams("parallel", "arbitrary"),
    )(*args)


def _acc(ref, val):
    @pl.when(pl.program_id(1) == 0)
    def _():
        ref[...] = jnp.zeros_like(ref)

    ref[...] += val


def _mod_fn(x, sc, sh):
    return x * (1.0 + sc) + sh


def _ln_fn(alpha, x, y, gate, g, b):
    z = alpha * x + (1.0 + gate) * y
    mu = jnp.mean(z, -1, keepdims=True)
    var = jnp.mean(jnp.square(z - mu), -1, keepdims=True)
    return (z - mu) * lax.rsqrt(var + LN_EPS) * g + b


def _modulate(x, sc, sh, name):
    B, S, D = x.shape

    def body(x_ref, sc_ref, sh_ref, h_ref):
        h_ref[...] = _mod_fn(x_ref[...], sc_ref[...], sh_ref[...]).astype(BF16)

    return _rows_call(body, name, B, S, [(x, "row"), (sc, "ex"), (sh, "ex")], [(D, BF16, "row")])[0]


def _ln_fwd(alpha, x, y, gate, g, b, name):
    B, S, D = x.shape

    def body(x_ref, y_ref, gate_ref, g_ref, b_ref, o_ref):
        o_ref[...] = _ln_fn(alpha, x_ref[...], y_ref[...], gate_ref[...], g_ref[...], b_ref[...])

    return _rows_call(
        body, name, B, S, [(x, "row"), (y, "row"), (gate, "ex"), (g, "par"), (b, "par")], [(D, F32, "row")]
    )[0]


def _ln_bwd(alpha, dout, x, y, gate, g, b, name):
    B, S, D = x.shape

    def body(do_ref, x_ref, y_ref, gate_ref, g_ref, b_ref, dxr_ref, dy_ref, dgate_ref, dg_ref, db_ref):
        _, vjp = jax.vjp(
            functools.partial(_ln_fn, alpha), x_ref[...], y_ref[...], gate_ref[...], g_ref[...], b_ref[...]
        )
        dx, dy, dgate, dg, db = vjp(do_ref[...])
        dxr_ref[...] = dx
        dy_ref[...] = dy.astype(BF16)
        _acc(dgate_ref, dgate)
        _acc(dg_ref, dg)
        _acc(db_ref, db)

    return _rows_call(
        body,
        name,
        B,
        S,
        [(dout, "row"), (x, "row"), (y, "row"), (gate, "ex"), (g, "par"), (b, "par")],
        [(D, F32, "row"), (D, BF16, "row"), (D, F32, "acc"), (D, F32, "acc"), (D, F32, "acc")],
    )


def _mod_bwd(dh, dxr, x, sc, name):
    B, S, D = x.shape

    def body(dh_ref, dxr_ref, x_ref, sc_ref, dx_ref, dsc_ref, dsh_ref):
        dh_v = dh_ref[...]
        dx_ref[...] = dxr_ref[...] + dh_v * (1.0 + sc_ref[...])
        _acc(dsc_ref, jnp.sum(dh_v * x_ref[...], axis=0, keepdims=True))
        _acc(dsh_ref, jnp.sum(dh_v, axis=0, keepdims=True))

    return _rows_call(
        body,
        name,
        B,
        S,
        [(dh, "row"), (dxr, "row"), (x, "row"), (sc, "ex")],
        [(D, F32, "row"), (D, F32, "acc"), (D, F32, "acc")],
    )


def _loss_head(y, target, name):
    B, S, D = y.shape

    def body(y_ref, t_ref, l_ref, dy_ref):
        e = y_ref[...] - t_ref[...]
        dy_ref[...] = e * (1.0 / D)
        part = 0.5 * jnp.sum(jnp.sum(e * e, axis=1, keepdims=True) * (1.0 / D), axis=0, keepdims=True)
        _acc(l_ref, jnp.broadcast_to(part, (1, LANES)))

    loss, dy = _rows_call(
        body, name, B, S, [(y, "row"), (target, "row")], [(LANES, F32, "acc"), (D, F32, "row")]
    )
    return jnp.sum(loss[:, 0, 0]), dy


def _swiglu_fn(u):
    F = u.shape[-1] // 2
    return jax.nn.silu(u[:, :F]) * u[:, F:]


def _swiglu_fwd(u, name):
    B, S, F2 = u.shape

    def body(u_ref, a_ref):
        a_ref[...] = _swiglu_fn(u_ref[...]).astype(BF16)

    return _rows_call(body, name, B, S, [(u, "row")], [(F2 // 2, BF16, "row")])[0]


def _swiglu_bwd(u, da, name):
    B, S, F2 = u.shape

    def body(u_ref, da_ref, du_ref):
        _, vjp = jax.vjp(_swiglu_fn, u_ref[...])
        du_ref[...] = vjp(da_ref[...])[0].astype(BF16)

    return _rows_call(body, name, B, S, [(u, "row"), (da, "row")], [(F2, BF16, "row")])[0]


def _rms_fn(x, g):
    return x * lax.rsqrt(jnp.mean(jnp.square(x), -1, keepdims=True) + RMS_EPS) * g


def _mla_mid_fwd(proj, gq, gkv, name):
    B, S, _ = proj.shape

    def body(p_ref, gq_ref, gkv_ref, qn_ref, kvn_ref):
        p = p_ref[...]
        qn_ref[...] = _rms_fn(p[:, :Q_LORA], gq_ref[...]).astype(BF16)
        kvn_ref[...] = _rms_fn(p[:, Q_LORA : Q_LORA + KV_LORA], gkv_ref[...]).astype(BF16)

    return _rows_call(
        body, name, B, S, [(proj, "row"), (gq, "par"), (gkv, "par")], [(Q_LORA, BF16, "row"), (KV_LORA, BF16, "row")]
    )


def _mla_mid_bwd(proj, dqn, dkvn, dkr, gq, gkv, name):
    B, S, W = proj.shape

    def body(p_ref, dqn_ref, dkvn_ref, dkr_ref, gq_ref, gkv_ref, dp_ref, dgq_ref, dgkv_ref):
        p = p_ref[...]
        _, vq = jax.vjp(_rms_fn, p[:, :Q_LORA], gq_ref[...])
        dql, dgq = vq(dqn_ref[...])
        _, vkv = jax.vjp(_rms_fn, p[:, Q_LORA : Q_LORA + KV_LORA], gkv_ref[...])
        dkvl, dgkv = vkv(dkvn_ref[...])
        dp_ref[:, :Q_LORA] = dql.astype(BF16)
        dp_ref[:, Q_LORA : Q_LORA + KV_LORA] = dkvl.astype(BF16)
        dp_ref[:, Q_LORA + KV_LORA :] = dkr_ref[...].astype(BF16)
        _acc(dgq_ref, dgq)
        _acc(dgkv_ref, dgkv)

    return _rows_call(
        body,
        name,
        B,
        S,
        [(proj, "row"), (dqn, "row"), (dkvn, "row"), (dkr, "row"), (gq, "par"), (gkv, "par")],
        [(W, BF16, "row"), (Q_LORA, F32, "acc"), (KV_LORA, F32, "acc")],
    )


def _rope(x, cos, sin):
    h = QK_ROPE // 2
    x1, x2 = x[:, :h], x[:, h:]
    return jnp.concatenate([x1 * cos - x2 * sin, x1 * sin + x2 * cos], axis=1)


def _rope_t(dy, cos, sin):
    h = QK_ROPE // 2
    d1, d2 = dy[:, :h], dy[:, h:]
    return jnp.concatenate([d1 * cos + d2 * sin, d2 * cos - d1 * sin], axis=1)


def _heads_call(body, name, B, S, ins, outs, ts=256):
    ts = min(ts, S)
    in_specs, args = [], []
    for arr, kind in ins:
        if kind == "row":
            in_specs.append(pl.BlockSpec((None, ts, arr.shape[-1]), lambda b, s: (b, s, 0)))
        else:
            in_specs.append(pl.BlockSpec((arr.shape[0], None, ts, arr.shape[-1]), lambda b, s: (0, b, s, 0)))
        args.append(arr)
    out_specs, out_shape = [], []
    for shape, dt, kind in outs:
        if kind == "row":
            out_specs.append(pl.BlockSpec((None, ts, shape[-1]), lambda b, s: (b, s, 0)))
        else:
            out_specs.append(pl.BlockSpec((shape[0], None, ts, shape[-1]), lambda b, s: (0, b, s, 0)))
        out_shape.append(SDS(shape, dt))
    return pl.pallas_call(
        body,
        grid=(B, S // ts),
        in_specs=in_specs,
        out_specs=out_specs,
        out_shape=out_shape,
        name=name,
        compiler_params=_cparams("parallel", "parallel"),
    )(*args)


def _mla_prep_fwd(q, kv, proj, cos, sin, name):
    B, S, _ = q.shape
    H = MLA_HEADS

    def body(q_ref, kv_ref, p_ref, cos_ref, sin_ref, qh_ref, kh_ref, vh_ref):
        cos_v, sin_v = cos_ref[...], sin_ref[...]
        kr = _rope(p_ref[:, Q_LORA + KV_LORA :], cos_v, sin_v).astype(BF16)
        for h in range(H):
            qn = q_ref[:, h * QK_DIM : h * QK_DIM + QK_NOPE]
            qr = _rope(q_ref[:, h * QK_DIM + QK_NOPE : (h + 1) * QK_DIM], cos_v, sin_v)
            qh_ref[h] = jnp.concatenate([qn, qr], axis=1).astype(BF16)
            kn = kv_ref[:, h * 128 : h * 128 + QK_NOPE].astype(BF16)
            kh_ref[h] = jnp.concatenate([kn, kr], axis=1)
            vh_ref[h] = kv_ref[:, h * 128 + QK_NOPE : (h + 1) * 128].astype(BF16)

    return _heads_call(
        body,
        name,
        B,
        S,
        [(q, "row"), (kv, "row"), (proj, "row"), (cos, "row"), (sin, "row")],
        [((H, B, S, QK_DIM), BF16, "heads"), ((H, B, S, QK_DIM), BF16, "heads"), ((H, B, S, V_HEAD), BF16, "heads")],
    )


def _mla_prep_bwd(dqh, dkh, dvh, cos, sin, name):
    H, B, S, _ = dqh.shape

    def body(dqh_ref, dkh_ref, dvh_ref, cos_ref, sin_ref, dq_ref, dkv_ref, dkr_ref):
        cos_v, sin_v = cos_ref[...], sin_ref[...]
        dkr = jnp.zeros((cos_v.shape[0], QK_ROPE), F32)
        for h in range(H):
            dqv = dqh_ref[h].astype(F32)
            dq_ref[:, h * QK_DIM : h * QK_DIM + QK_NOPE] = dqv[:, :QK_NOPE].astype(BF16)
            dq_ref[:, h * QK_DIM + QK_NOPE : (h + 1) * QK_DIM] = _rope_t(dqv[:, QK_NOPE:], cos_v, sin_v).astype(BF16)
            dkv = dkh_ref[h].astype(F32)
            dkv_ref[:, h * 128 : h * 128 + QK_NOPE] = dkv[:, :QK_NOPE].astype(BF16)
            dkv_ref[:, h * 128 + QK_NOPE : (h + 1) * 128] = dvh_ref[h]
            dkr = dkr + dkv[:, QK_NOPE:]
        dkr_ref[...] = _rope_t(dkr, cos_v, sin_v)

    return _heads_call(
        body,
        name,
        B,
        S,
        [(dqh, "heads"), (dkh, "heads"), (dvh, "heads"), (cos, "row"), (sin, "row")],
        [((B, S, H * QK_DIM), BF16, "row"), ((B, S, H * 128), BF16, "row"), ((B, S, QK_ROPE), F32, "row")],
    )


LOG2E = 1.4426950408889634
ATTN_TILE = 512


def _tril_mask(t):
    return lax.broadcasted_iota(jnp.int32, (t, t), 0) >= lax.broadcasted_iota(jnp.int32, (t, t), 1)


def _attn_fwd(qh, kh, vh, name):
    H, B, S, _ = qh.shape
    t = min(ATTN_TILE, S)
    scale = QK_DIM**-0.5
    c2 = scale * LOG2E

    def body(q_ref, k_ref, v_ref, o_ref, lse_ref):
        i = pl.program_id(2)
        qs = [q_ref[0], q_ref[1]]

        def step(j, carry, diagonal):
            rows = pl.ds(pl.multiple_of(j * t, t), t)
            out = []
            for hh in range(2):
                m, l, acc = carry[hh]
                s = _bdot_nt(qs[hh], k_ref[hh, rows, :])
                if diagonal:
                    s = jnp.where(_tril_mask(t), s, NEG)
                m_new = jnp.maximum(m, jnp.max(s, axis=1, keepdims=True))
                p = jnp.exp2((s - m_new) * c2)
                a = jnp.exp2((m - m_new) * c2)
                l = a * l + jnp.sum(p, axis=1, keepdims=True)
                acc = a * acc + _bdot(p, v_ref[hh, rows, :])
                out.append((m_new, l, acc))
            return tuple(out)

        one = (jnp.full((t, 1), NEG, F32), jnp.zeros((t, 1), F32), jnp.zeros((t, V_HEAD), F32))
        carry = lax.fori_loop(0, i, lambda j, cy: step(j, cy, False), (one, one))
        carry = step(i, carry, True)
        outs = []
        for hh in range(2):
            m, l, acc = carry[hh]
            outs.append(acc / l)
            lse_ref[hh] = m * scale + jnp.log(l)
        o_ref[...] = jnp.concatenate(outs, axis=1).astype(BF16)

    return pl.pallas_call(
        body,
        grid=(B, H // 2, S // t),
        in_specs=[
            pl.BlockSpec((2, None, t, QK_DIM), lambda b, p, i: (p, b, i, 0)),
            pl.BlockSpec((2, None, S, QK_DIM), lambda b, p, i: (p, b, 0, 0)),
            pl.BlockSpec((2, None, S, V_HEAD), lambda b, p, i: (p, b, 0, 0)),
        ],
        out_specs=[
            pl.BlockSpec((None, t, 2 * V_HEAD), lambda b, p, i: (b, i, p)),
            pl.BlockSpec((2, None, t, 1), lambda b, p, i: (p, b, i, 0)),
        ],
        out_shape=[SDS((B, S, H * V_HEAD), BF16), SDS((H, B, S, 1), F32)],
        name=name,
        compiler_params=_cparams("parallel", "parallel", "arbitrary"),
    )(qh, kh, vh)


def _attn_bwd(qh, kh, vh, o, do, lse, name):
    H, B, S, _ = qh.shape
    t = min(ATTN_TILE, S)
    nq = S // t
    scale = QK_DIM**-0.5
    c2 = scale * LOG2E

    def body(q_ref, k_ref, v_ref, o_ref, do_ref, lse_ref, dq_ref, dk_ref, dv_ref, dq_acc, delta_ref, lse2_ref):
        prod = o_ref[...].astype(F32) * do_ref[...].astype(F32)
        for hh in range(2):
            delta_ref[hh] = jnp.sum(prod[:, hh * V_HEAD : (hh + 1) * V_HEAD], axis=1, keepdims=True)
            lse2_ref[hh] = lse_ref[hh] * LOG2E
        dq_acc[...] = jnp.zeros_like(dq_acc)

        def kloop(j, _):
            krows = pl.ds(pl.multiple_of(j * t, t), t)
            ks = [k_ref[0, krows, :], k_ref[1, krows, :]]
            vs = [v_ref[0, krows, :], v_ref[1, krows, :]]

            def qstep(i, carry, diagonal):
                qrows = pl.ds(pl.multiple_of(i * t, t), t)
                do_i = do_ref[qrows, :]
                out = []
                for hh in range(2):
                    dk, dv = carry[hh]
                    q = q_ref[hh, qrows, :]
                    do_h = do_i[:, hh * V_HEAD : (hh + 1) * V_HEAD]
                    s = _bdot_nt(q, ks[hh])
                    p = jnp.exp2(s * c2 - lse2_ref[hh, qrows, :])
                    if diagonal:
                        p = jnp.where(_tril_mask(t), p, 0.0)
                    dv = dv + _bdot_tn(p, do_h)
                    dp = _bdot_nt(do_h, vs[hh])
                    ds = (p * (dp - delta_ref[hh, qrows, :])).astype(BF16)
                    dk = dk + _bdot_tn(ds, q)
                    dq_acc[hh, qrows, :] += _bdot(ds, ks[hh])
                    out.append((dk, dv))
                return tuple(out)

            one = (jnp.zeros((t, QK_DIM), F32), jnp.zeros((t, V_HEAD), F32))
            carry = qstep(j, (one, one), True)
            carry = lax.fori_loop(j + 1, nq, lambda i, cy: qstep(i, cy, False), carry)
            for hh in range(2):
                dk_ref[hh, krows, :] = (carry[hh][0] * scale).astype(BF16)
                dv_ref[hh, krows, :] = carry[hh][1].astype(BF16)
            return 0

        lax.fori_loop(0, nq, kloop, 0)
        dq_ref[...] = (dq_acc[...] * scale).astype(BF16)

    hspec = lambda w: pl.BlockSpec((2, None, S, w), lambda b, p: (p, b, 0, 0))
    ospec = pl.BlockSpec((None, S, 2 * V_HEAD), lambda b, p: (b, 0, p))
    return pl.pallas_call(
        body,
        grid=(B, H // 2),
        in_specs=[hspec(QK_DIM), hspec(QK_DIM), hspec(V_HEAD), ospec, ospec, hspec(1)],
        out_specs=[hspec(QK_DIM), hspec(QK_DIM), hspec(V_HEAD)],
        out_shape=[SDS((H, B, S, QK_DIM), BF16), SDS((H, B, S, QK_DIM), BF16), SDS((H, B, S, V_HEAD), BF16)],
        scratch_shapes=[pltpu.VMEM((2, S, QK_DIM), F32), pltpu.VMEM((2, S, 1), F32), pltpu.VMEM((2, S, 1), F32)],
        name=name,
        compiler_params=_cparams("parallel", "parallel"),
    )(qh, kh, vh, o, do, lse)


def _hgrn_pre(q, fx, lb):
    sig = jax.nn.sigmoid(fx)
    f = lb + (1.0 - lb) * sig
    return jax.nn.silu(q), 1.0 - f, jnp.log(f)


def _hgrn_gate(o, gg, gn):
    return _rms_fn(o, gn) * jax.nn.silu(gg)


def _tri(n, lower):
    r = lax.broadcasted_iota(jnp.int32, (n, n), 0)
    c = lax.broadcasted_iota(jnp.int32, (n, n), 1)
    return ((r >= c) if lower else (r <= c)).astype(F32)


def _hgrn_intra_fwd(qs, k, v, b):
    C, SB = qs.shape[0], min(HGRN_SUB, qs.shape[0])
    ridx = lax.broadcasted_iota(jnp.int32, (SUBLANES, 1), 0)
    outs = []
    for i in range(C // SB):
        r0 = i * SB
        qi, ki, vi, bi = qs[r0 : r0 + SB], k[r0 : r0 + SB], v[r0 : r0 + SB], b[r0 : r0 + SB]
        ng = SB // SUBLANES
        qg = [qi[g * SUBLANES : (g + 1) * SUBLANES] for g in range(ng)]
        bg = [bi[g * SUBLANES : (g + 1) * SUBLANES] for g in range(ng)]
        accg = [jnp.zeros((SUBLANES, v.shape[1]), F32) for _ in range(ng)]
        for s in range(SB):
            gs, so = divmod(s, SUBLANES)
            k_s, v_s, b_s = ki[s : s + 1], vi[s : s + 1], bi[s : s + 1]
            for tg in range(gs, ng):
                if tg == gs:
                    mask = ridx >= so
                    w = jnp.where(mask, qg[tg] * k_s * jnp.exp(jnp.where(mask, bg[tg] - b_s, 0.0)), 0.0)
                else:
                    w = qg[tg] * k_s * jnp.exp(bg[tg] - b_s)
                accg[tg] = accg[tg] + jnp.sum(w, axis=1, keepdims=True) * v_s
        acc = jnp.concatenate(accg, axis=0)
        if i > 0:
            ref = bi[0:1]
            qt = qi * jnp.exp(bi - ref)
            kt = k[:r0] * jnp.exp(ref - b[:r0])
            acc = acc + _bdot(_mdot_nt(qt, kt), v[:r0])
        outs.append(acc)
    return jnp.concatenate(outs, axis=0)


def _hgrn_intra_bwd(qs, k, v, b, do):
    C, SB = qs.shape[0], min(HGRN_SUB, qs.shape[0])
    nb = C // SB
    ridx = lax.broadcasted_iota(jnp.int32, (SUBLANES, 1), 0)
    dq_p = [None] * nb
    dk_p = [jnp.zeros((SB, k.shape[1]), F32) for _ in range(nb)]
    dv_p = [jnp.zeros((SB, v.shape[1]), F32) for _ in range(nb)]
    for i in range(nb):
        r0 = i * SB
        qi, ki, vi, bi, doi = qs[r0 : r0 + SB], k[r0 : r0 + SB], v[r0 : r0 + SB], b[r0 : r0 + SB], do[r0 : r0 + SB]
        ng = SB // SUBLANES
        qg = [qi[g * SUBLANES : (g + 1) * SUBLANES] for g in range(ng)]
        bg = [bi[g * SUBLANES : (g + 1) * SUBLANES] for g in range(ng)]
        dog = [doi[g * SUBLANES : (g + 1) * SUBLANES] for g in range(ng)]
        dqg =[jnp.zeros((SUBLANES, k.shape[1]), F32) for _ in range(ng)]
        dkg = [jnp.zeros((SUBLANES, k.shape[1]), F32) for _ in range(ng)]
        dvg = [jnp.zeros((SUBLANES, v.shape[1]), F32) for _ in range(ng)]
        for s in range(SB):
            gs, so = divmod(s, SUBLANES)
            k_s, v_s, b_s = ki[s : s + 1], vi[s : s + 1], bi[s : s + 1]
            dk_s = jnp.zeros((SUBLANES, k.shape[1]), F32)
            dv_s = jnp.zeros((SUBLANES, v.shape[1]), F32)
            for tg in range(gs, ng):
                if tg == gs:
                    mask = ridx >= so
                    e = jnp.where(mask, jnp.exp(jnp.where(mask, bg[tg] - b_s, 0.0)), 0.0)
                else:
                    e = jnp.exp(bg[tg] - b_s)
                da = jnp.sum(dog[tg] * v_s, axis=1, keepdims=True)
                qe = qg[tg] * e
                a = jnp.sum(qe * k_s, axis=1, keepdims=True)
                dqg[tg] = dqg[tg] + da * (k_s * e)
                dk_s = dk_s + da * qe
                dv_s = dv_s + a * dog[tg]
            dkg[gs] = jnp.where(ridx == so, dkg[gs] + jnp.sum(dk_s, axis=0, keepdims=True), dkg[gs])
            dvg[gs] = jnp.where(ridx == so, dvg[gs] + jnp.sum(dv_s, axis=0, keepdims=True), dvg[gs])
        dqi = jnp.concatenate(dqg, axis=0)
        dki = jnp.concatenate(dkg, axis=0)
        dvi = jnp.concatenate(dvg, axis=0)
        if i > 0:
            ref = bi[0:1]
            eq = jnp.exp(bi - ref)
            ek = jnp.exp(ref - b[:r0])
            qt = qi * eq
            kt = k[:r0] * ek
            A = _mdot_nt(qt, kt)
            dA = _bdot_nt(doi, v[:r0])
            dvl = _bdot_tn(A, doi)
            dqi = dqi + _mdot(dA, kt) * eq
            dkl = _mdot_tn(dA, qt) * ek
            for j in range(i):
                dk_p[j] = dk_p[j] + dkl[j * SB : (j + 1) * SB]
                dv_p[j] = dv_p[j] + dvl[j * SB : (j + 1) * SB]
        dq_p[i] = dqi
        dk_p[i] = dk_p[i] + dki
        dv_p[i] = dv_p[i] + dvi
    return jnp.concatenate(dq_p, axis=0), jnp.concatenate(dk_p, axis=0), jnp.concatenate(dv_p, axis=0)


def _hgrn_fwd(proj, lb, gn, name):
    B, S, W = proj.shape
    HK = W // 4
    H = HK // HGRN_K
    C = min(HGRN_CHUNK, S)
    N = S // C

    HP = HGRN_PAR if H % HGRN_PAR == 0 else 1
    WP = HP * HGRN_K

    def body(q_ref, f_ref, i_ref, g_ref, lb_ref, gn_ref, og_ref, o_ref, st_ref):
        gn_v = gn_ref[...]
        tril = _tri(C, True)

        def chunk(n, sts):
            rows = pl.ds(pl.multiple_of(n * C, C), C)
            out = []
            for hh in range(HP):
                ln = slice(hh * HGRN_K, (hh + 1) * HGRN_K)
                st = sts[hh]
                qs, k, g = _hgrn_pre(q_ref[rows, ln], f_ref[rows, ln], lb_ref[:, ln])
                v = i_ref[rows, ln]
                b = _hdot(tril, g)
                st_ref[hh, n] = st
                o = _hgrn_intra_fwd(qs, k, v, b) + _bdot_nt(qs * jnp.exp(b), st)
                bl = b[C - 1 : C]
                out.append(st * jnp.exp(bl) + _bdot_tn(v, k * jnp.exp(bl - b)))
                o_ref[rows, ln] = o
                og_ref[rows, ln] = _hgrn_gate(o, g_ref[rows, ln], gn_v).astype(BF16)
            return tuple(out)

        lax.fori_loop(0, N, chunk, tuple(jnp.zeros((HGRN_K, HGRN_K), F32) for _ in range(HP)))

    col = lambda part: pl.BlockSpec((None, S, WP), lambda b, h: (b, 0, part * (H // HP) + h))
    return pl.pallas_call(
        body,
        grid=(B, H // HP),
        in_specs=[col(0), col(1), col(2), col(3), pl.BlockSpec((1, WP), lambda b, h: (0, h)), pl.BlockSpec((1, HGRN_K), lambda b, h: (0, 0))],
        out_specs=[col(0), col(0), pl.BlockSpec((None, HP, N, HGRN_K, HGRN_K), lambda b, h: (b, h, 0, 0, 0))],
        out_shape=[SDS((B, S, HK), BF16), SDS((B, S, HK), F32), SDS((B, H, N, HGRN_K, HGRN_K), F32)],
        name=name,
        compiler_params=_cparams("parallel", "parallel"),
    )(proj, proj, proj, proj, lb, gn)


def _hgrn_bwd(proj, lb, gn, o_pre, states, dog, name):
    B, S, W = proj.shape
    HK = W // 4
    H = HK // HGRN_K
    C = min(HGRN_CHUNK, S)
    N = S // C

    HP = HGRN_PAR if H % HGRN_PAR == 0 else 1
    WP = HP * HGRN_K

    def body(q_ref, f_ref, i_ref, g_ref, lb_ref, gn_ref, o_ref, st_ref, dog_ref, dq_ref, df_ref, di_ref, dg_ref, dlb_ref, dgn_ref):
        gn_v = gn_ref[...]
        tril = _tri(C, True)
        triu = _tri(C, False)

        def chunk(idx, carry):
            n = N - 1 - idx
            rows = pl.ds(pl.multiple_of(n * C, C), C)
            out = []
            for hh in range(HP):
                ln = slice(hh * HGRN_K, (hh + 1) * HGRN_K)
                dst, dlb, dgn = carry[hh]
                (qs, k, g), pre_vjp = jax.vjp(_hgrn_pre, q_ref[rows, ln], f_ref[rows, ln], lb_ref[:, ln])
                v = i_ref[rows, ln]
                _, gate_vjp = jax.vjp(_hgrn_gate, o_ref[rows, ln], g_ref[rows, ln], gn_v)
                do, dgg, dgn_c = gate_vjp(dog_ref[rows, ln])
                b = _hdot(tril, g)
                st0 = st_ref[hh, n]
                eb = jnp.exp(b)
                bl = b[C - 1 : C]
                ebl = jnp.exp(bl)
                ekb = jnp.exp(bl - b)
                qe = qs * eb
                kt = k * ekb
                dqs, dk, dv = _hgrn_intra_bwd(qs, k, v, b, do)
                dqs = dqs + _bdot(do, st0) * eb
                dk = dk + _bdot(v, dst) * ekb
                dv = dv + _bdot_nt(kt, dst)
                st1 = st0 * ebl + _bdot_tn(v, kt)
                dbl = jnp.sum(st1 * dst, axis=0, keepdims=True)
                dst = dst * ebl + _bdot_tn(do, qe)
                dgl = _hdot(triu, qs * dqs - k * dk) + dbl
                dq_pre, dfx, dlb_c = pre_vjp((dqs, dk, dgl))
                dq_ref[rows, ln] = dq_pre.astype(BF16)
                df_ref[rows, ln] = dfx.astype(BF16)
                di_ref[rows, ln] = dv.astype(BF16)
                dg_ref[rows, ln] = dgg.astype(BF16)
                out.append((dst, dlb + dlb_c, dgn + dgn_c))
            return tuple(out)

        zero = jnp.zeros((1, HGRN_K), F32)
        one = (jnp.zeros((HGRN_K, HGRN_K), F32), zero, zero)
        res = lax.fori_loop(0, N, chunk, tuple(one for _ in range(HP)))
        for hh in range(HP):
            dlb_ref[hh] = res[hh][1]
            dgn_ref[hh] = res[hh][2]

    col = lambda part: pl.BlockSpec((None, S, WP), lambda b, h: (b, 0, part * (H // HP) + h))
    vec = pl.BlockSpec((None, HP, 1, HGRN_K), lambda b, h: (b, h, 0, 0))
    return pl.pallas_call(
        body,
        grid=(B, H // HP),
        in_specs=[
            col(0), col(1), col(2), col(3),
            pl.BlockSpec((1, WP), lambda b, h: (0, h)),
            pl.BlockSpec((1, HGRN_K), lambda b, h: (0, 0)),
            col(0),
            pl.BlockSpec((None, HP, N, HGRN_K, HGRN_K), lambda b, h: (b, h, 0, 0, 0)),
            col(0),
        ],
        out_specs=[col(0), col(0), col(0), col(0), vec, vec],
        out_shape=[SDS((B, S, HK), BF16)] * 4 + [SDS((B, H, 1, HGRN_K), F32)] * 2,
        name=name,
        compiler_params=_cparams("parallel", "parallel"),
    )(proj, proj, proj, proj, lb, gn, o_pre, states, dog)


def _ada_fwd(c_all, w, b, name):
    Bg, D = c_all.shape
    L, _, n = w.shape

    def body(c_ref, w_ref, b_ref, o_ref):
        o_ref[...] = _bdot(jax.nn.silu(c_ref[...]), w_ref[...]) + b_ref[...]

    return pl.pallas_call(
        body,
        grid=(L,),
        in_specs=[
            pl.BlockSpec((Bg, D), lambda l: (0, 0)),
            pl.BlockSpec((None, D, n), lambda l: (l, 0, 0)),
            pl.BlockSpec((None, 1, n), lambda l: (l, 0, 0)),
        ],
        out_specs=pl.BlockSpec((None, Bg, n), lambda l: (l, 0, 0)),
        out_shape=SDS((L, Bg, n), F32),
        name=name,
        compiler_params=_cparams("parallel"),
    )(c_all, w, b)


def _ada_bwd(c_all, dmod, name):
    Bg, D = c_all.shape
    L, _, n = dmod.shape

    def body(c_ref, d_ref, dw_ref, db_ref):
        d = d_ref[...]
        dw_ref[...] = _bdot_tn(jax.nn.silu(c_ref[...]), d)
        db_ref[...] = jnp.sum(d, axis=0, keepdims=True)

    return pl.pallas_call(
        body,
        grid=(L,),
        in_specs=[pl.BlockSpec((Bg, D), lambda l: (0, 0)), pl.BlockSpec((None, Bg, n), lambda l: (l, 0, 0))],
        out_specs=[pl.BlockSpec((None, D, n), lambda l: (l, 0, 0)), pl.BlockSpec((None, 1, n), lambda l: (l, 0, 0))],
        out_shape=[SDS((L, D, n), F32), SDS((L, 1, n), F32)],
        name=name,
        compiler_params=_cparams("parallel"),
    )(c_all, dmod)


def _adamw(w, gs, m, v, name):
    shape = w.shape
    cols = shape[-1]
    rows = w.size // cols
    tr = rows
    for cand in (512, 256, 128, 64, 32, 16, 8):
        if rows % cand == 0 and cand * cols * 4 <= 2 * 1024 * 1024:
            tr = cand
            break
    as2d = lambda a: a.reshape(rows, cols)
    ng = len(gs)
    c1 = 1.0 / (1.0 - ADAM_B1**ADAM_STEP)
    c2 = 1.0 / (1.0 - ADAM_B2**ADAM_STEP)

    def body(*refs):
        w_ref, m_ref, v_ref = refs[0], refs[1], refs[2]
        g_refs = refs[3 : 3 + ng]
        g_out, d_out, m_out, v_out = refs[3 + ng :]
        g = g_refs[0][...].astype(F32)
        for r in g_refs[1:]:
            g = g + r[...].astype(F32)
        m_new = ADAM_B1 * m_ref[...] + (1.0 - ADAM_B1) * g
        v_new = ADAM_B2 * v_ref[...] + (1.0 - ADAM_B2) * jnp.square(g)
        g_out[...] = g
        m_out[...] = m_new
        v_out[...] = v_new
        d_out[...] = -ADAM_LR * ((m_new * c1) / (jnp.sqrt(v_new * c2) + ADAM_EPS) + ADAM_WD * w_ref[...])

    spec = pl.BlockSpec((tr, cols), lambda i: (i, 0))
    outs = pl.pallas_call(
        body,
        grid=(rows // tr,),
        in_specs=[spec] * (3 + ng),
        out_specs=[spec] * 4,
        out_shape=[SDS((rows, cols), F32)] * 4,
        name=name,
        compiler_params=_cparams("parallel"),
    )(as2d(w), as2d(m), as2d(v), *[as2d(g) for g in gs])
    return tuple(o.reshape(shape) for o in outs)


def _sum4(own, recv, name):
    shape = own.shape
    cols = shape[-1]
    rows = own.size // cols
    tr = rows
    for cand in (512, 256, 128, 64, 32, 16):
        if rows % cand == 0 and cand * cols * 4 <= 2 * 1024 * 1024:
            tr = cand
            break

    def body(own_ref, recv_ref, o_ref):
        acc = own_ref[...].astype(F32)
        for r in range(3):
            acc = acc + recv_ref[r].astype(F32)
        o_ref[...] = acc

    out = pl.pallas_call(
        body,
        grid=(rows // tr,),
        in_specs=[pl.BlockSpec((tr, cols), lambda i: (i, 0)), pl.BlockSpec((3, tr, cols), lambda i: (0, i, 0))],
        out_specs=pl.BlockSpec((tr, cols), lambda i: (i, 0)),
        out_shape=SDS((rows, cols), F32),
        name=name,
        compiler_params=_cparams("parallel"),
    )(own.reshape(rows, cols), recv.reshape(3, rows, cols))
    return out.reshape(shape)


def _my_place():
    return lax.axis_index("x"), lax.axis_index("y"), lax.axis_index("c")


def _flip(v, bit):
    return 1 - v if bit else v


def _allgather8(x, name):
    r, n = x.shape

    def body(x_ref, o_ref, send_sems, recv_sems, local_sem):
        mx, my, mc = _my_place()
        me = 4 * mx + 2 * my + mc
        mine = pltpu.make_async_copy(x_ref, o_ref.at[me], local_sem)
        mine.start()
        sends = []
        for rel in range(1, 8):
            peer = (_flip(mx, rel & 4), _flip(my, rel & 2), _flip(mc, rel & 1))
            cp = pltpu.make_async_remote_copy(
                src_ref=x_ref, dst_ref=o_ref.at[me], send_sem=send_sems.at[rel - 1], recv_sem=recv_sems.at[rel - 1],
                device_id=peer, device_id_type=MESH,
            )
            cp.start()
            sends.append(cp)
        for rel in range(1, 8):
            px, py, pc = _flip(mx, rel & 4), _flip(my, rel & 2), _flip(mc, rel & 1)
            pltpu.make_async_remote_copy(
                src_ref=x_ref, dst_ref=o_ref.at[4 * px + 2 * py + pc], send_sem=send_sems.at[rel - 1],
                recv_sem=recv_sems.at[rel - 1], device_id=(px, py, pc), device_id_type=MESH,
            ).wait_recv()
        for cp in sends:
            cp.wait_send()
        mine.wait()

    return pl.pallas_call(
        body,
        out_shape=SDS((8, r, n), x.dtype),
        in_specs=[pl.BlockSpec(memory_space=pl.ANY)],
        out_specs=pl.BlockSpec(memory_space=pl.ANY),
        scratch_shapes=[pltpu.SemaphoreType.DMA((7,)), pltpu.SemaphoreType.DMA((7,)), pltpu.SemaphoreType.DMA],
        name=name,
    )(x)


_HBM = pl.BlockSpec(memory_space=pl.ANY)


_SEM = pl.BlockSpec(memory_space=pltpu.SEMAPHORE)
_HBM_ONLY = pl.BlockSpec(memory_space=pltpu.HBM)
_EFFECT = pltpu.SideEffectType.DATAFLOW_SIDE_EFFECTING


def _in_hbm(a):
    return pltpu.with_memory_space_constraint(a, pltpu.HBM)


def _gather_start(lands, after, name):
    n = len(lands)

    def body(*refs):
        land = refs[:n]
        send_sems, recv_sems = refs[n + 1], refs[n + 2]
        token = refs[-1]
        mx, my, mc = _my_place()
        for i in range(n):
            for rel in range(1, 4):
                pltpu.make_async_remote_copy(
                    src_ref=land[i].at[2 * mx + my], dst_ref=land[i].at[2 * mx + my],
                    send_sem=send_sems.at[3 * i + rel - 1], recv_sem=recv_sems.at[3 * i + rel - 1],
                    device_id=(_flip(mx, rel & 2), _flip(my, rel & 1), mc), device_id_type=MESH,
                ).start()
        token[...] = jnp.zeros_like(token)

    outs = pl.pallas_call(
        body,
        name=name,
        out_shape=(
            pltpu.SemaphoreType.DMA((3 * n,)), pltpu.SemaphoreType.DMA((3 * n,)),
            *[pltpu.HBM(a.shape, a.dtype) for a in lands], SDS((8, LANES), F32),
        ),
        in_specs=[_HBM_ONLY] * n + [_HBM],
        out_specs=(_SEM, _SEM, *[_HBM_ONLY] * n, pl.BlockSpec(memory_space=pltpu.VMEM)),
        input_output_aliases={i: 2 + i for i in range(n)},
        compiler_params=pltpu.CompilerParams(has_side_effects=_EFFECT),
    )(*[_in_hbm(a) for a in lands], after)
    return outs[0], outs[1], list(outs[2 : 2 + n]), outs[-1]


def _gather_wait(send_sems, recv_sems, lands, after, name):
    n = len(lands)

    def body(*refs):
        land = refs[:n]
        s_sems, r_sems = refs[n], refs[n + 1]
        mx, my, mc = _my_place()
        for i in range(n):
            for rel in range(1, 4):
                px, py = _flip(mx, rel & 2), _flip(my, rel & 1)
                cp = pltpu.make_async_remote_copy(
                    src_ref=land[i].at[2 * mx + my], dst_ref=land[i].at[2 * px + py],
                    send_sem=s_sems.at[3 * i + rel - 1], recv_sem=r_sems.at[3 * i + rel - 1],
                    device_id=(px, py, mc), device_id_type=MESH,
                )
                cp.wait_send()
                cp.wait_recv()

    outs = pl.pallas_call(
        body,
        name=name,
        out_shape=tuple(pltpu.HBM(a.shape, a.dtype) for a in lands),
        in_specs=[_HBM_ONLY] * n + [_SEM, _SEM, _HBM],
        out_specs=[_HBM_ONLY] * n,
        input_output_aliases={i: i for i in range(n)},
        compiler_params=pltpu.CompilerParams(has_side_effects=_EFFECT),
    )(*lands, send_sems, recv_sems, after)
    return list(outs)


def _scatter_start(slabs, lands, places, name):
    n = len(slabs)

    def body(*refs):
        ins, land = refs[:n], refs[n : 2 * n]
        send_sems, recv_sems = refs[2 * n], refs[2 * n + 1]
        token = refs[-1]
        mx, my, mc = _my_place()
        for i in range(n):
            for rel in range(1, 4):
                px, py = _flip(mx, rel & 2), _flip(my, rel & 1)
                pltpu.make_async_remote_copy(
                    src_ref=ins[i].at[2 * px + py], dst_ref=land[i].at[rel - 1, places[i]],
                    send_sem=send_sems.at[3 * i + rel - 1], recv_sem=recv_sems.at[3 * i + rel - 1],
                    device_id=(px, py, mc), device_id_type=MESH,
                ).start()
        token[...] = jnp.zeros_like(token)

    outs = pl.pallas_call(
        body,
        name=name,
        out_shape=(
            pltpu.SemaphoreType.DMA((3 * n,)), pltpu.SemaphoreType.DMA((3 * n,)),
            *[pltpu.HBM(a.shape, a.dtype) for a in slabs], *[pltpu.HBM(a.shape, a.dtype) for a in lands],
            SDS((8, LANES), F32),
        ),
        in_specs=[_HBM_ONLY] * (2 * n),
        out_specs=(_SEM, _SEM, *[_HBM_ONLY] * (2 * n), pl.BlockSpec(memory_space=pltpu.VMEM)),
        input_output_aliases={i: 2 + i for i in range(2 * n)},
        compiler_params=pltpu.CompilerParams(has_side_effects=_EFFECT),
    )(*[_in_hbm(a) for a in slabs], *[_in_hbm(a) for a in lands])
    return outs[0], outs[1], list(outs[2 : 2 + n]), list(outs[2 + n : 2 + 2 * n]), outs[-1]


def _scatter_wait(send_sems, recv_sems, slabs, lands, places, after, name):
    n = len(slabs)

    def body(*refs):
        ins, land = refs[:n], refs[n : 2 * n]
        s_sems, r_sems = refs[2 * n], refs[2 * n + 1]
        mx, my, mc = _my_place()
        for i in range(n):
            for rel in range(1, 4):
                px, py = _flip(mx, rel & 2), _flip(my, rel & 1)
                cp = pltpu.make_async_remote_copy(
                    src_ref=ins[i].at[2 * px + py], dst_ref=land[i].at[rel - 1, places[i]],
                    send_sem=s_sems.at[3 * i + rel - 1], recv_sem=r_sems.at[3 * i + rel - 1],
                    device_id=(px, py, mc), device_id_type=MESH,
                )
                cp.wait_send()
                cp.wait_recv()

    outs = pl.pallas_call(
        body,
        name=name,
        out_shape=(*[pltpu.HBM(a.shape, a.dtype) for a in slabs], *[pltpu.HBM(a.shape, a.dtype) for a in lands]),
        in_specs=[_HBM_ONLY] * (2 * n) + [_SEM, _SEM, _HBM],
        out_specs=[_HBM_ONLY] * (2 * n),
        input_output_aliases={i: i for i in range(2 * n)},
        compiler_params=pltpu.CompilerParams(has_side_effects=_EFFECT),
    )(*slabs, *lands, send_sems, recv_sems, after)
    return list(outs[:n]), list(outs[n:])


def _swap_sibling(parts, name):
    n = len(parts)

    def body(*refs):
        ins, outs = refs[:n], refs[n : 2 * n]
        send_sems, recv_sems = refs[2 * n :]
        mx, my, mc = _my_place()
        sends = []
        for i in range(n):
            cp = pltpu.make_async_remote_copy(
                src_ref=ins[i], dst_ref=outs[i], send_sem=send_sems.at[i], recv_sem=recv_sems.at[i],
                device_id=(mx, my, 1 - mc), device_id_type=MESH,
            )
            cp.start()
            sends.append(cp)
        for cp in sends:
            cp.wait_recv()
        for cp in sends:
            cp.wait_send()

    return pl.pallas_call(
        body,
        out_shape=[SDS(s.shape, s.dtype) for s in parts],
        in_specs=[_HBM] * n,
        out_specs=[_HBM] * n,
        scratch_shapes=[pltpu.SemaphoreType.DMA((n,)), pltpu.SemaphoreType.DMA((n,))],
        name=name,
    )(*parts)


def _pad_rows(a, rows):
    return jnp.pad(a, ((0, rows - a.shape[0]), (0, 0)))


def kernel(x, c, positions, mla_w_in, mla_q_norm, mla_w_qb, mla_kv_norm, mla_w_kvb, mla_w_o, hgrn_lb, hgrn_w_in, hgrn_g_norm, hgrn_w_o, ffn_w_in, ffn_w_out, ada_w, ada_b, ln_g, ln_b, loss_target, m_mla_w_in, m_mla_q_norm, m_mla_w_qb, m_mla_kv_norm, m_mla_w_kvb, m_mla_w_o, m_hgrn_lb, m_hgrn_w_in, m_hgrn_g_norm, m_hgrn_w_o, m_ffn_w_in, m_ffn_w_out, m_ada_w, m_ada_b, m_ln_g, m_ln_b, v_mla_w_in, v_mla_q_norm, v_mla_w_qb, v_mla_kv_norm, v_mla_w_kvb, v_mla_w_o, v_hgrn_lb, v_hgrn_w_in, v_hgrn_g_norm, v_hgrn_w_o, v_ffn_w_in, v_ffn_w_out, v_ada_w, v_ada_b, v_ln_g, v_ln_b):
    B, S, D = x.shape
    T = B * S
    depth = ada_w.shape[0]
    n_mla, n_hgrn = mla_w_in.shape[0], hgrn_w_in.shape[0]
    n_sub = 2 * depth
    alpha = (2.0 * depth) ** 0.25
    mx, my, mc = _my_place()
    me = 4 * mx + 2 * my + mc
    k_me = 2 * mx + my
    Bg = 8 * B
    HK = hgrn_w_o.shape[1] * 4
    dq = D // 4

    lbw = hgrn_lb.shape[1]
    first = jnp.zeros((8, max(D, 4 * lbw)), F32)
    first = first.at[:B, :D].set(c).at[B : B + n_hgrn, :lbw].set(hgrn_lb)
    first_all = _allgather8(first, "gather_cond")
    c_all = first_all[:, :B, :D].reshape(Bg, D)
    lb_logits = jnp.concatenate([first_all[2 * k, B : B + n_hgrn, :lbw] for k in range(4)], axis=1)

    def lower_bounds_fn(logits):
        soft = jax.nn.softmax(logits, axis=0)
        return jnp.cumsum(soft, axis=0) - soft[0]

    lower_bounds, lower_bounds_vjp = jax.vjp(lower_bounds_fn, lb_logits)

    n_ada = ada_w.shape[-1]
    mod_part = _ada_fwd(c_all, ada_w.reshape(n_sub, D, n_ada), ada_b.reshape(n_sub, 1, n_ada), "ada_fwd")
    mod_all = _allgather8(mod_part.reshape(n_sub * Bg, n_ada), "gather_mod").reshape(8, n_sub, Bg, n_ada)
    mod = jnp.concatenate([mod_all[2 * k] for k in range(4)], axis=-1)
    mod = lax.dynamic_slice_in_dim(mod, me * B, B, axis=1)
    shift = [mod[j, :, None, :D] for j in range(n_sub)]
    scale = [mod[j, :, None, D : 2 * D] for j in range(n_sub)]
    gate = [mod[j, :, None, 2 * D :] for j in range(n_sub)]

    ln_rows = 2 * n_sub
    ln_local = _pad_rows(jnp.concatenate([ln_g.reshape(n_sub, dq), ln_b.reshape(n_sub, dq)], axis=0), -(-ln_rows // 8) * 8)
    ln_pad = jnp.zeros((ln_local.shape[0], -(-dq // LANES) * LANES), F32).at[:, :dq].set(ln_local)
    ln_all = _allgather8(ln_pad, "gather_ln")
    ln_full = jnp.concatenate([ln_all[2 * k, :ln_rows, :dq] for k in range(4)], axis=1)
    lng = [ln_full[j][None, :] for j in range(n_sub)]
    lnb = [ln_full[n_sub + j][None, :] for j in range(n_sub)]

    main = dict(mla_w_in=mla_w_in, mla_w_qb=mla_w_qb, mla_w_kvb=mla_w_kvb, mla_w_o=mla_w_o, hgrn_w_in=hgrn_w_in,
                hgrn_w_o=hgrn_w_o, ffn_w_in=ffn_w_in, ffn_w_out=ffn_w_out)
    names = list(main)

    def group_kinds(layer, part):
        if part:
            return [("ffn_w_in", layer), ("ffn_w_out", layer)]
        mixer = ["mla_w_in", "mla_w_qb", "mla_w_kvb", "mla_w_o"] if layer % 2 == 0 else ["hgrn_w_in", "hgrn_w_o"]
        return [(k, layer // 2) for k in mixer]

    gathers = {}
    after = mod_all[0, 0, :8, :LANES] + ln_all[0, :8, :LANES]
    for layer in range(depth):
        for part in range(2):
            lands = [lax.dynamic_update_index_in_dim(lax.empty((4,) + main[k].shape[1:], BF16), main[k][i].astype(BF16), k_me, 0)
                     for k, i in group_kinds(layer, part)]
            ssem, rsem, lands, after = _gather_start(lands, after, f"gather_start_l{layer}p{part}")
            gathers[layer, part] = (ssem, rsem, lands)
    scale[0] = scale[0] + after[0, 0]

    def row_w(g):
        return g.reshape(1, g.shape[0] * g.shape[1], g.shape[2])

    def full_w_in(g):
        return jnp.transpose(g, (1, 0, 2)).reshape(1, g.shape[1], 4 * g.shape[2])

    ang = positions.astype(F32)[..., None] * (ROPE_THETA ** (-jnp.arange(0, QK_ROPE, 2, dtype=F32) / QK_ROPE))
    cos, sin = jnp.cos(ang), jnp.sin(ang)

    gq = [mla_q_norm[j][None, :] for j in range(n_mla)]
    gkv = [mla_kv_norm[j][None, :] for j in range(n_mla)]
    gn = [hgrn_g_norm[j][None, :] for j in range(n_hgrn)]

    def r2(a):
        return a.reshape(T, a.shape[-1])

    def r3(a):
        return a.reshape(B, S, a.shape[-1])

    saved = []
    xs = x
    for layer in range(depth):
        j = layer // 2
        sub = 2 * layer
        tag = f"l{layer}"
        ssem, rsem, lands = gathers[layer, 0]
        lands = _gather_wait(ssem, rsem, lands, xs if layer else scale[0], f"gather_wait_{tag}p0")
        wl = {k: g for (k, _), g in zip(group_kinds(layer, 0), lands)}
        h = _modulate(xs, scale[sub], shift[sub], f"mod_{tag}a")
        if layer % 2 == 0:
            wl["mla_w_in"] = full_w_in(wl["mla_w_in"])
            proj = r3(_mm_nn(r2(h), wl["mla_w_in"], F32, f"mla_in_{tag}"))
            qn, kvn = _mla_mid_fwd(proj, gq[j], gkv[j], f"mla_mid_{tag}")
            q = r3(_mm_nn(r2(qn), wl["mla_w_qb"], F32, f"mla_qb_{tag}"))
            kv = r3(_mm_nn(r2(kvn), wl["mla_w_kvb"], F32, f"mla_kvb_{tag}"))
            qh, kh, vh = _mla_prep_fwd(q, kv, proj, cos, sin, f"mla_prep_{tag}")
            o, lse = _attn_fwd(qh, kh, vh, f"attn_{tag}")
            wl["mla_w_o"] = row_w(wl["mla_w_o"])
            y = r3(_mm_nn(r2(o), wl["mla_w_o"], F32, f"mla_o_{tag}"))
            mix = (h, proj, qn, kvn, qh, kh, vh, o, lse)
        else:
            proj = r3(_mm_nn(r2(h), wl["hgrn_w_in"], F32, f"hgrn_in_{tag}"))
            og, o_pre, states = _hgrn_fwd(proj, lower_bounds[j][None, :], gn[j], f"hgrn_{tag}")
            wl["hgrn_w_o"] = row_w(wl["hgrn_w_o"])
            y = r3(_mm_nn(r2(og), wl["hgrn_w_o"], F32, f"hgrn_o_{tag}"))
            mix = (h, proj, og, o_pre, states)
        x1 = _ln_fwd(alpha, xs, y, gate[sub], lng[sub], lnb[sub], f"ln_{tag}a")
        ssem, rsem, lands = gathers[layer, 1]
        lands = _gather_wait(ssem, rsem, lands, x1, f"gather_wait_{tag}p1")
        wl.update({k: g for (k, _), g in zip(group_kinds(layer, 1), lands)})
        h2 = _modulate(x1, scale[sub + 1], shift[sub + 1], f"mod_{tag}b")
        u = r3(_mm_nn(r2(h2), wl["ffn_w_in"], F32, f"ffn_in_{tag}"))
        a = _swiglu_fwd(u, f"swiglu_{tag}")
        wl["ffn_w_out"] = row_w(wl["ffn_w_out"])
        y2 = r3(_mm_nn(r2(a), wl["ffn_w_out"], F32, f"ffn_out_{tag}"))
        x2 = _ln_fwd(alpha, x1, y2, gate[sub + 1], lng[sub + 1], lnb[sub + 1], f"ln_{tag}b")
        saved.append((xs, y, x1, y2, mix, h2, u, a, wl))
        xs = x2

    loss_local, dout = _loss_head(xs, loss_target, "loss_head")
    loss = lax.psum(loss_local, ("x", "y", "c"))

    gw = {k: [None] * main[k].shape[0] for k in names}
    land = {k: lax.empty((3,) + main[k].shape, BF16) for k in names}
    scatters = []
    d_shift, d_scale, d_gate = [None] * n_sub, [None] * n_sub, [None] * n_sub
    d_lng, d_lnb = [None] * n_sub, [None] * n_sub
    d_gq, d_gkv, d_gn, d_lbnd = [None] * n_mla, [None] * n_mla, [None] * n_hgrn, [None] * n_hgrn

    def rows4(g):
        return g.reshape(4, g.shape[1] // 4, g.shape[2])

    def start_scatter(layer, part, token_to):
        kinds = group_kinds(layer, part)
        ssem, rsem, slabs_t, lands_t, token = _scatter_start(
            [gw[k][i] for k, i in kinds], [land[k] for k, _ in kinds], [i for _, i in kinds], f"scatter_start_l{layer}p{part}")
        for (k, i), s_t, l_t in zip(kinds, slabs_t, lands_t):
            gw[k][i], land[k] = s_t, l_t
        scatters.append((layer, part, ssem, rsem))
        if token_to is not None:
            gate[token_to] = gate[token_to] + token[0, 0]

    for layer in reversed(range(depth)):
        j = layer // 2
        sub = 2 * layer
        tag = f"l{layer}"
        xs, y, x1, y2, mix, h2, u, a, wl = saved[layer]
        dxr, dy2, d_gate[sub + 1], d_lng[sub + 1], d_lnb[sub + 1] = _ln_bwd(
            alpha, dout, x1, y2, gate[sub + 1], lng[sub + 1], lnb[sub + 1], f"ln_bwd_{tag}b")
        da = r3(_mm_nt(r2(dy2), wl["ffn_w_out"], F32, f"ffn_out_dx_{tag}"))
        gw["ffn_w_out"][layer] = rows4(_mm_tn(r2(a), r2(dy2), 1, BF16, f"ffn_out_dw_{tag}"))
        du = _swiglu_bwd(u, da, f"swiglu_bwd_{tag}")
        dh2 = r3(_mm_nt(r2(du), wl["ffn_w_in"], F32, f"ffn_in_dx_{tag}"))
        gw["ffn_w_in"][layer] = _mm_tn(r2(h2), r2(du), 4, BF16, f"ffn_in_dw_{tag}")
        start_scatter(layer, 1, sub)
        dout, d_scale[sub + 1], d_shift[sub + 1] = _mod_bwd(dh2, dxr, x1, scale[sub + 1], f"mod_bwd_{tag}b")
        dxr, dy, d_gate[sub], d_lng[sub], d_lnb[sub] = _ln_bwd(
            alpha, dout, xs, y, gate[sub], lng[sub], lnb[sub], f"ln_bwd_{tag}a")
        if layer % 2 == 0:
            h, proj, qn, kvn, qh, kh, vh, o, lse = mix
            do = r3(_mm_nt(r2(dy), wl["mla_w_o"], BF16, f"mla_o_dx_{tag}"))
            gw["mla_w_o"][j] = rows4(_mm_tn(r2(o), r2(dy), 1, BF16, f"mla_o_dw_{tag}"))
            dqh, dkh, dvh = _attn_bwd(qh, kh, vh, o, do, lse, f"attn_bwd_{tag}")
            dq_, dkv_, dkr = _mla_prep_bwd(dqh, dkh, dvh, cos, sin, f"mla_prep_bwd_{tag}")
            dqn = r3(_mm_nt(r2(dq_), wl["mla_w_qb"], F32, f"mla_qb_dx_{tag}"))
            gw["mla_w_qb"][j] = _mm_tn(r2(qn), r2(dq_), 4, BF16, f"mla_qb_dw_{tag}")
            dkvn = r3(_mm_nt(r2(dkv_), wl["mla_w_kvb"], F32, f"mla_kvb_dx_{tag}"))
            gw["mla_w_kvb"][j] = _mm_tn(r2(kvn), r2(dkv_), 4, BF16, f"mla_kvb_dw_{tag}")
            dproj, dgq_, dgkv_ = _mla_mid_bwd(proj, dqn, dkvn, dkr, gq[j], gkv[j], f"mla_mid_bwd_{tag}")
            d_gq[j], d_gkv[j] = dgq_.sum(0), dgkv_.sum(0)
            dh = r3(_mm_nt(r2(dproj), wl["mla_w_in"], F32, f"mla_in_dx_{tag}"))
            gwin = _mm_tn(r2(h), r2(dproj), 1, BF16, f"mla_in_dw_{tag}")[0]
            gw["mla_w_in"][j] = jnp.transpose(gwin.reshape(gwin.shape[0], 4, gwin.shape[1] // 4), (1, 0, 2))
        else:
            h, proj, og, o_pre, states = mix
            dog = r3(_mm_nt(r2(dy), wl["hgrn_w_o"], F32, f"hgrn_o_dx_{tag}"))
            gw["hgrn_w_o"][j] = rows4(_mm_tn(r2(og), r2(dy), 1, BF16, f"hgrn_o_dw_{tag}"))
            dq_, df_, di_, dg_, dlb_, dgn_ = _hgrn_bwd(proj, lower_bounds[j][None, :], gn[j], o_pre, states, dog, f"hgrn_bwd_{tag}")
            dproj = jnp.concatenate([dq_, df_, di_, dg_], axis=-1)
            d_lbnd[j] = dlb_.sum(0).reshape(1, HK)
            d_gn[j] = dgn_.sum((0, 1))
            dh = r3(_mm_nt(r2(dproj), wl["hgrn_w_in"], F32, f"hgrn_in_dx_{tag}"))
            gw["hgrn_w_in"][j] = _mm_tn(r2(h), r2(dproj), 4, BF16, f"hgrn_in_dw_{tag}")
        start_scatter(layer, 0, sub - 1 if layer else None)
        dout, d_scale[sub], d_shift[sub] = _mod_bwd(dh, dxr, xs, scale[sub], f"mod_bwd_{tag}a")
    grad_x = dout

    for layer, part, ssem, rsem in scatters:
        kinds = group_kinds(layer, part)
        slabs_t, lands_t = _scatter_wait(
            ssem, rsem, [gw[k][i] for k, i in kinds], [land[k] for k, _ in kinds], [i for _, i in kinds], grad_x,
            f"scatter_wait_l{layer}p{part}")
        for (k, i), s_t, l_t in zip(kinds, slabs_t, lands_t):
            gw[k][i], land[k] = s_t, l_t
    sums = [_sum4(jnp.stack([lax.dynamic_index_in_dim(g, k_me, 0, keepdims=False) for g in gw[k]]), land[k], f"sum4_{k}")
            for k in names]
    others = _swap_sibling(sums, "swap_sums")
    g_main = {k: (a_, b_) for k, a_, b_ in zip(names, sums, others)}

    dmod = jnp.stack([jnp.concatenate([d_shift[s_][:, 0], d_scale[s_][:, 0], d_gate[s_][:, 0]], axis=-1) for s_ in range(n_sub)])
    dmod_rows = _pad_rows(dmod.reshape(n_sub * B, 3 * D), -(-n_sub * B // 8) * 8)
    dmod_all = _allgather8(dmod_rows, "gather_dmod")[:, : n_sub * B].reshape(8, n_sub, B, 3 * D)
    dmod_all = jnp.transpose(dmod_all, (1, 0, 2, 3)).reshape(n_sub, Bg, 3 * D)
    dmod_mine = lax.dynamic_slice_in_dim(dmod_all, k_me * n_ada, n_ada, axis=2)
    g_ada_w, g_ada_b = _ada_bwd(c_all, dmod_mine, "ada_bwd")
    g_ada_w = g_ada_w.reshape(ada_w.shape)
    g_ada_b = g_ada_b.reshape(ada_b.shape)

    small = [jnp.stack(d_gq).reshape(-1), jnp.stack(d_gkv).reshape(-1), jnp.stack(d_gn).reshape(-1),
             jnp.stack(d_lbnd).reshape(-1), jnp.stack([d.sum(0) for d in d_lng]).reshape(-1),
             jnp.stack([d.sum(0) for d in d_lnb]).reshape(-1)]
    sizes = [s_.shape[0] for s_ in small]
    flat = jnp.concatenate(small)
    rows_small = -(-flat.shape[0] // (8 * LANES)) * 8
    flat = jnp.pad(flat, (0, rows_small * LANES - flat.shape[0])).reshape(rows_small, LANES)
    tot = _allgather8(flat, "gather_small")
    acc = tot[0]
    for d in range(1, 8):
        acc = acc + tot[d]
    acc = acc.reshape(-1)
    offs = [0]
    for s_ in sizes:
        offs.append(offs[-1] + s_)
    g_q_norm = acc[offs[0] : offs[1]].reshape(mla_q_norm.shape)
    g_kv_norm = acc[offs[1] : offs[2]].reshape(mla_kv_norm.shape)
    g_g_norm = acc[offs[2] : offs[3]].reshape(hgrn_g_norm.shape)
    g_lbnd = acc[offs[3] : offs[4]].reshape(n_hgrn, HK)
    g_lb_full = lower_bounds_vjp(g_lbnd)[0]
    g_hgrn_lb = lax.dynamic_slice_in_dim(g_lb_full, k_me * lbw, lbw, axis=1)
    g_lng = lax.dynamic_slice_in_dim(acc[offs[4] : offs[5]].reshape(n_sub, D), k_me * dq, dq, axis=1).reshape(ln_g.shape)
    g_lnb = lax.dynamic_slice_in_dim(acc[offs[5] : offs[6]].reshape(n_sub, D), k_me * dq, dq, axis=1).reshape(ln_b.shape)

    weights = dict(mla_w_in=mla_w_in, mla_q_norm=mla_q_norm, mla_w_qb=mla_w_qb, mla_kv_norm=mla_kv_norm, mla_w_kvb=mla_w_kvb,
                   mla_w_o=mla_w_o, hgrn_lb=hgrn_lb, hgrn_w_in=hgrn_w_in, hgrn_g_norm=hgrn_g_norm, hgrn_w_o=hgrn_w_o,
                   ffn_w_in=ffn_w_in, ffn_w_out=ffn_w_out, ada_w=ada_w, ada_b=ada_b, ln_g=ln_g, ln_b=ln_b)
    moms = dict(mla_w_in=(m_mla_w_in, v_mla_w_in), mla_q_norm=(m_mla_q_norm, v_mla_q_norm), mla_w_qb=(m_mla_w_qb, v_mla_w_qb),
                mla_kv_norm=(m_mla_kv_norm, v_mla_kv_norm), mla_w_kvb=(m_mla_w_kvb, v_mla_w_kvb), mla_w_o=(m_mla_w_o, v_mla_w_o),
                hgrn_lb=(m_hgrn_lb, v_hgrn_lb), hgrn_w_in=(m_hgrn_w_in, v_hgrn_w_in), hgrn_g_norm=(m_hgrn_g_norm, v_hgrn_g_norm),
                hgrn_w_o=(m_hgrn_w_o, v_hgrn_w_o), ffn_w_in=(m_ffn_w_in, v_ffn_w_in), ffn_w_out=(m_ffn_w_out, v_ffn_w_out),
                ada_w=(m_ada_w, v_ada_w), ada_b=(m_ada_b, v_ada_b), ln_g=(m_ln_g, v_ln_g), ln_b=(m_ln_b, v_ln_b))
    grads = dict(mla_q_norm=(g_q_norm,), mla_kv_norm=(g_kv_norm,), hgrn_lb=(g_hgrn_lb,), hgrn_g_norm=(g_g_norm,),
                 ada_w=(g_ada_w,), ada_b=(g_ada_b,), ln_g=(g_lng,), ln_b=(g_lnb,), **g_main)
    res = {k: _adamw(weights[k], [g_.reshape(weights[k].shape) for g_ in grads[k]], moms[k][0], moms[k][1], f"adamw_{k}")
           for k in weights}
    order = list(weights)
    return (loss, grad_x, *[res[k][0] for k in order], *[res[k][1] for k in order], *[res[k][2] for k in order],
            *[res[k][3] for k in order])
```

```python
import functools

import jax
import jax.numpy as jnp
from jax import lax
from jax.experimental import pallas as pl
from jax.experimental.pallas import tpu as pltpu

F32 = jnp.float32
BF16 = jnp.bfloat16
SDS = jax.ShapeDtypeStruct
MESH = pl.DeviceIdType.MESH
HI = lax.Precision.HIGHEST
MID = lax.Precision.HIGH

MLA_HEADS, QK_NOPE, QK_ROPE, V_HEAD = 16, 64, 32, 64
Q_LORA, KV_LORA = 768, 256
QK_DIM = QK_NOPE + QK_ROPE
ROPE_THETA = 10000.0
HGRN_K = 128
HGRN_CHUNK = 64
HGRN_SUB = 32
HGRN_PAR = 2
LN_EPS, RMS_EPS = 1e-5, 1e-6
ADAM_LR, ADAM_B1, ADAM_B2, ADAM_EPS, ADAM_WD, ADAM_STEP = 0.001, 0.9, 0.999, 1e-08, 0.01, 10
NEG = -1e30

VMEM_LIMIT_BYTES = 56 * 1024 * 1024
LANES = 128
SUBLANES = 8


def _cparams(*sem):
    return pltpu.CompilerParams(dimension_semantics=sem if sem else None, vmem_limit_bytes=VMEM_LIMIT_BYTES)


def _pick_tile(n, cap):
    best = 0
    for t in range(LANES, min(n, cap) + 1, LANES):
        if n % t == 0:
            best = t
    return best if best else n


def _bdot(a, b):
    return jnp.dot(a.astype(BF16), b.astype(BF16), preferred_element_type=F32)


def _bdot_nt(a, b):
    return lax.dot_general(a.astype(BF16), b.astype(BF16), (((1,), (1,)), ((), ())), preferred_element_type=F32)


def _bdot_tn(a, b):
    return lax.dot_general(a.astype(BF16), b.astype(BF16), (((0,), (0,)), ((), ())), preferred_element_type=F32)


def _hdot(a, b):
    return jnp.dot(a, b, precision=HI, preferred_element_type=F32)


def _mdot(a, b):
    return jnp.dot(a, b, precision=MID, preferred_element_type=F32)


def _mdot_nt(a, b):
    return lax.dot_general(a, b, (((1,), (1,)), ((), ())), precision=MID, preferred_element_type=F32)


def _mdot_tn(a, b):
    return lax.dot_general(a, b, (((0,), (0,)), ((), ())), precision=MID, preferred_element_type=F32)


def _mm_nn(a, w, out_dtype, name):
    M, K = a.shape
    G, _, n = w.shape
    tm = min(512, M)
    tn = _pick_tile(n, 1536)
    nps = n // tn

    def body(a_ref, w_ref, o_ref):
        o_ref[...] = _bdot(a_ref[...], w_ref[...]).astype(o_ref.dtype)

    return pl.pallas_call(
        body,
        grid=(G * nps, M // tm),
        in_specs=[
            pl.BlockSpec((tm, K), lambda j, i: (i, 0)),
            pl.BlockSpec((None, K, tn), lambda j, i: (j // nps, 0, j % nps)),
        ],
        out_specs=pl.BlockSpec((tm, tn), lambda j, i: (i, j)),
        out_shape=SDS((M, G * n), out_dtype),
        name=name,
        compiler_params=_cparams("parallel", "parallel"),
    )(a, w)


def _mm_nt(a, w, out_dtype, name):
    M = a.shape[0]
    G, K, n = w.shape
    tm = min(512, M)
    tk = _pick_tile(K, 1536)

    def body(a_ref, w_ref, o_ref, acc_ref):
        s = pl.program_id(2)

        @pl.when(s == 0)
        def _():
            acc_ref[...] = jnp.zeros_like(acc_ref)

        acc_ref[...] += _bdot_nt(a_ref[...], w_ref[...])

        @pl.when(s == G - 1)
        def _():
            o_ref[...] = acc_ref[...].astype(o_ref.dtype)

    return pl.pallas_call(
        body,
        grid=(K // tk, M // tm, G),
        in_specs=[
            pl.BlockSpec((tm, n), lambda kb, i, s: (i, s)),
            pl.BlockSpec((None, tk, n), lambda kb, i, s: (s, kb, 0)),
        ],
        out_specs=pl.BlockSpec((tm, tk), lambda kb, i, s: (i, kb)),
        out_shape=SDS((M, K), out_dtype),
        scratch_shapes=[pltpu.VMEM((tm, tk), F32)],
        name=name,
        compiler_params=_cparams("parallel", "parallel", "arbitrary"),
    )(a, w)


def _mm_tn(a, d, G, out_dtype, name):
    T, K = a.shape
    n = d.shape[1] // G
    tk = _pick_tile(K, 256)
    tn = _pick_tile(n, 1536)
    nps = n // tn

    def body(a_ref, d_ref, o_ref):
        o_ref[...] = _bdot_tn(a_ref[...], d_ref[...]).astype(o_ref.dtype)

    return pl.pallas_call(
        body,
        grid=(G * nps, K // tk),
        in_specs=[
            pl.BlockSpec((T, tk), lambda j, i: (0, i)),
            pl.BlockSpec((T, tn), lambda j, i: (0, j)),
        ],
        out_specs=pl.BlockSpec((None, tk, tn), lambda j, i: (j // nps, i, j % nps)),
        out_shape=SDS((G, K, n), out_dtype),
        name=name,
        compiler_params=_cparams("parallel", "parallel"),
    )(a, d)


def _rows_call(body, name, B, S, ins, outs, ts=256):
    ts = min(ts, S)
    in_specs, args = [], []
    for arr, kind in ins:
        W = arr.shape[-1]
        if kind == "row":
            in_specs.append(pl.BlockSpec((None, ts, W), lambda b, s: (b, s, 0)))
        elif kind == "ex":
            in_specs.append(pl.BlockSpec((None, 1, W), lambda b, s: (b, 0, 0)))
        else:
            in_specs.append(pl.BlockSpec((1, W), lambda b, s: (0, 0)))
        args.append(arr)
    out_specs, out_shape = [], []
    for W, dt, kind in outs:
        if kind == "row":
            out_specs.append(pl.BlockSpec((None, ts, W), lambda b, s: (b, s, 0)))
            out_shape.append(SDS((B, S, W), dt))
        else:
            out_specs.append(pl.BlockSpec((None, 1, W), lambda b, s: (b, 0, 0)))
            out_shape.append(SDS((B, 1, W), dt))
    return pl.pallas_call(
        body,
        grid=(B, S // ts),
        in_specs=in_specs,
        out_specs=out_specs,
        out_shape=out_shape,
        name=name,
        compiler_params=_cparams("parallel", "arbitrary"),
    )(*args)


def _acc(ref, val):
    @pl.when(pl.program_id(1) == 0)
    def _():
        ref[...] = jnp.zeros_like(ref)

    ref[...] += val


def _mod_fn(x, sc, sh):
    return x * (1.0 + sc) + sh


def _ln_fn(alpha, x, y, gate, g, b):
    z = alpha * x + (1.0 + gate) * y
    mu = jnp.mean(z, -1, keepdims=True)
    var = jnp.mean(jnp.square(z - mu), -1, keepdims=True)
    return (z - mu) * lax.rsqrt(var + LN_EPS) * g + b


def _modulate(x, sc, sh, name):
    B, S, D = x.shape

    def body(x_ref, sc_ref, sh_ref, h_ref):
        h_ref[...] = _mod_fn(x_ref[...], sc_ref[...], sh_ref[...]).astype(BF16)

    return _rows_call(body, name, B, S, [(x, "row"), (sc, "ex"), (sh, "ex")], [(D, BF16, "row")])[0]


def _ln_fwd(alpha, x, y, gate, g, b, name):
    B, S, D = x.shape

    def body(x_ref, y_ref, gate_ref, g_ref, b_ref, o_ref):
        o_ref[...] = _ln_fn(alpha, x_ref[...], y_ref[...], gate_ref[...], g_ref[...], b_ref[...])

    return _rows_call(
        body, name, B, S, [(x, "row"), (y, "row"), (gate, "ex"), (g, "par"), (b, "par")], [(D, F32, "row")]
    )[0]


def _ln_mod_fwd(alpha, x, y, gate, g, b, sc_next, sh_next, name):
    B, S, D = x.shape

    def body(x_ref, y_ref, gate_ref, g_ref, b_ref, sc_ref, sh_ref, o_ref, h_ref):
        out = _ln_fn(alpha, x_ref[...], y_ref[...], gate_ref[...], g_ref[...], b_ref[...])
        o_ref[...] = out
        h_ref[...] = _mod_fn(out, sc_ref[...], sh_ref[...]).astype(BF16)

    return _rows_call(
        body, name, B, S,
        [(x, "row"), (y, "row"), (gate, "ex"), (g, "par"), (b, "par"), (sc_next, "ex"), (sh_next, "ex")],
        [(D, F32, "row"), (D, BF16, "row")],
    )


def _ln_mod_bwd(alpha, dh, dxr_next, sc_next, x, y, gate, g, b, name):
    B, S, D = x.shape

    def body(dh_ref, dxr_ref, sc_ref, x_ref, y_ref, gate_ref, g_ref, b_ref,
             dx_ref, dy_ref, dgate_ref, dg_ref, db_ref, dsc_ref, dsh_ref):
        out, vjp = jax.vjp(
            functools.partial(_ln_fn, alpha), x_ref[...], y_ref[...], gate_ref[...], g_ref[...], b_ref[...]
        )
        dh_v = dh_ref[...]
        dx, dy, dgate, dg, db = vjp(dxr_ref[...] + dh_v * (1.0 + sc_ref[...]))
        dx_ref[...] = dx
        dy_ref[...] = dy.astype(BF16)
        _acc(dgate_ref, dgate)
        _acc(dg_ref, dg)
        _acc(db_ref, db)
        _acc(dsc_ref, jnp.sum(dh_v * out, axis=0, keepdims=True))
        _acc(dsh_ref, jnp.sum(dh_v, axis=0, keepdims=True))

    return _rows_call(
        body, name, B, S,
        [(dh, "row"), (dxr_next, "row"), (sc_next, "ex"), (x, "row"), (y, "row"), (gate, "ex"), (g, "par"), (b, "par")],
        [(D, F32, "row"), (D, BF16, "row")] + [(D, F32, "acc")] * 5,
    )


def _ln_bwd(alpha, dout, x, y, gate, g, b, name):
    B, S, D = x.shape

    def body(do_ref, x_ref, y_ref, gate_ref, g_ref, b_ref, dxr_ref, dy_ref, dgate_ref, dg_ref, db_ref):
        _, vjp = jax.vjp(
            functools.partial(_ln_fn, alpha), x_ref[...], y_ref[...], gate_ref[...], g_ref[...], b_ref[...]
        )
        dx, dy, dgate, dg, db = vjp(do_ref[...])
        dxr_ref[...] = dx
        dy_ref[...] = dy.astype(BF16)
        _acc(dgate_ref, dgate)
        _acc(dg_ref, dg)
        _acc(db_ref, db)

    return _rows_call(
        body,
        name,
        B,
        S,
        [(dout, "row"), (x, "row"), (y, "row"), (gate, "ex"), (g, "par"), (b, "par")],
        [(D, F32, "row"), (D, BF16, "row"), (D, F32, "acc"), (D, F32, "acc"), (D, F32, "acc")],
    )


def _mod_bwd(dh, dxr, x, sc, name):
    B, S, D = x.shape

    def body(dh_ref, dxr_ref, x_ref, sc_ref, dx_ref, dsc_ref, dsh_ref):
        dh_v = dh_ref[...]
        dx_ref[...] = dxr_ref[...] + dh_v * (1.0 + sc_ref[...])
        _acc(dsc_ref, jnp.sum(dh_v * x_ref[...], axis=0, keepdims=True))
        _acc(dsh_ref, jnp.sum(dh_v, axis=0, keepdims=True))

    return _rows_call(
        body,
        name,
        B,
        S,
        [(dh, "row"), (dxr, "row"), (x, "row"), (sc, "ex")],
        [(D, F32, "row"), (D, F32, "acc"), (D, F32, "acc")],
    )


def _loss_head(y, target, name):
    B, S, D = y.shape

    def body(y_ref, t_ref, l_ref, dy_ref):
        e = y_ref[...] - t_ref[...]
        dy_ref[...] = e * (1.0 / D)
        part = 0.5 * jnp.sum(jnp.sum(e * e, axis=1, keepdims=True) * (1.0 / D), axis=0, keepdims=True)
        _acc(l_ref, jnp.broadcast_to(part, (1, LANES)))

    loss, dy = _rows_call(
        body, name, B, S, [(y, "row"), (target, "row")], [(LANES, F32, "acc"), (D, F32, "row")]
    )
    return jnp.sum(loss[:, 0, 0]), dy


def _ffn_in(h, w, name):
    M, K = h.shape
    G, _, n = w.shape
    assert G == 4
    tm = min(512, M)
    tn = _pick_tile(n, 1536)
    nps = n // tn
    half = 2 * nps

    def body(h_ref, wg_ref, wu_ref, a_ref, g_ref, u_ref):
        hv = h_ref[...]
        g = _bdot(hv, wg_ref[...])
        u = _bdot(hv, wu_ref[...])
        a_ref[...] = (jax.nn.silu(g) * u).astype(BF16)
        g_ref[...] = g.astype(BF16)
        u_ref[...] = u.astype(BF16)

    out = pl.BlockSpec((tm, tn), lambda j, i: (i, j))
    return pl.pallas_call(
        body,
        grid=(half, M // tm),
        in_specs=[
            pl.BlockSpec((tm, K), lambda j, i: (i, 0)),
            pl.BlockSpec((None, K, tn), lambda j, i: (j // nps, 0, j % nps)),
            pl.BlockSpec((None, K, tn), lambda j, i: (2 + j // nps, 0, j % nps)),
        ],
        out_specs=[out, out, out],
        out_shape=[SDS((M, 2 * n), BF16)] * 3,
        name=name,
        compiler_params=_cparams("parallel", "parallel"),
    )(h, w, w)


def _swiglu_bwd(g, u, da, name):
    B, S, F = g.shape

    def body(g_ref, u_ref, da_ref, du_ref):
        _, vjp = jax.vjp(lambda gv, uv: jax.nn.silu(gv) * uv, g_ref[...].astype(F32), u_ref[...].astype(F32))
        dg, du = vjp(da_ref[...])
        du_ref[:, :F] = dg.astype(BF16)
        du_ref[:, F:] = du.astype(BF16)

    return _rows_call(body, name, B, S, [(g, "row"), (u, "row"), (da, "row")], [(2 * F, BF16, "row")])[0]


def _rms_fn(x, g):
    return x * lax.rsqrt(jnp.mean(jnp.square(x), -1, keepdims=True) + RMS_EPS) * g


def _mla_mid_fwd(proj, gq, gkv, name):
    B, S, _ = proj.shape

    def body(p_ref, gq_ref, gkv_ref, qn_ref, kvn_ref):
        p = p_ref[...]
        qn_ref[...] = _rms_fn(p[:, :Q_LORA], gq_ref[...]).astype(BF16)
        kvn_ref[...] = _rms_fn(p[:, Q_LORA : Q_LORA + KV_LORA], gkv_ref[...]).astype(BF16)

    return _rows_call(
        body, name, B, S, [(proj, "row"), (gq, "par"), (gkv, "par")], [(Q_LORA, BF16, "row"), (KV_LORA, BF16, "row")]
    )


def _mla_mid_bwd(proj, dqn, dkvn, dkr, gq, gkv, name):
    B, S, W = proj.shape

    def body(p_ref, dqn_ref, dkvn_ref, dkr_ref, gq_ref, gkv_ref, dp_ref, dgq_ref, dgkv_ref):
        p = p_ref[...]
        _, vq = jax.vjp(_rms_fn, p[:, :Q_LORA], gq_ref[...])
        dql, dgq = vq(dqn_ref[...])
        _, vkv = jax.vjp(_rms_fn, p[:, Q_LORA : Q_LORA + KV_LORA], gkv_ref[...])
        dkvl, dgkv = vkv(dkvn_ref[...])
        dp_ref[:, :Q_LORA] = dql.astype(BF16)
        dp_ref[:, Q_LORA : Q_LORA + KV_LORA] = dkvl.astype(BF16)
        dp_ref[:, Q_LORA + KV_LORA :] = dkr_ref[...].astype(BF16)
        _acc(dgq_ref, dgq)
        _acc(dgkv_ref, dgkv)

    return _rows_call(
        body,
        name,
        B,
        S,
        [(proj, "row"), (dqn, "row"), (dkvn, "row"), (dkr, "row"), (gq, "par"), (gkv, "par")],
        [(W, BF16, "row"), (Q_LORA, F32, "acc"), (KV_LORA, F32, "acc")],
    )


def _rope(x, cos, sin):
    h = QK_ROPE // 2
    x1, x2 = x[:, :h], x[:, h:]
    return jnp.concatenate([x1 * cos - x2 * sin, x1 * sin + x2 * cos], axis=1)


def _rope_t(dy, cos, sin):
    h = QK_ROPE // 2
    d1, d2 = dy[:, :h], dy[:, h:]
    return jnp.concatenate([d1 * cos + d2 * sin, d2 * cos - d1 * sin], axis=1)


def _heads_call(body, name, B, S, ins, outs, ts=256):
    ts = min(ts, S)
    in_specs, args = [], []
    for arr, kind in ins:
        if kind == "row":
            in_specs.append(pl.BlockSpec((None, ts, arr.shape[-1]), lambda b, s: (b, s, 0)))
        else:
            in_specs.append(pl.BlockSpec((arr.shape[0], None, ts, arr.shape[-1]), lambda b, s: (0, b, s, 0)))
        args.append(arr)
    out_specs, out_shape = [], []
    for shape, dt, kind in outs:
        if kind == "row":
            out_specs.append(pl.BlockSpec((None, ts, shape[-1]), lambda b, s: (b, s, 0)))
        else:
            out_specs.append(pl.BlockSpec((shape[0], None, ts, shape[-1]), lambda b, s: (0, b, s, 0)))
        out_shape.append(SDS(shape, dt))
    return pl.pallas_call(
        body,
        grid=(B, S // ts),
        in_specs=in_specs,
        out_specs=out_specs,
        out_shape=out_shape,
        name=name,
        compiler_params=_cparams("parallel", "parallel"),
    )(*args)


def _mla_prep_fwd(q, kv, proj, cos, sin, name):
    B, S, _ = q.shape
    H = MLA_HEADS

    def body(q_ref, kv_ref, p_ref, cos_ref, sin_ref, qh_ref, kh_ref, vh_ref):
        cos_v, sin_v = cos_ref[...], sin_ref[...]
        kr = _rope(p_ref[:, Q_LORA + KV_LORA :], cos_v, sin_v).astype(BF16)
        for h in range(H):
            qn = q_ref[:, h * QK_DIM : h * QK_DIM + QK_NOPE]
            qr = _rope(q_ref[:, h * QK_DIM + QK_NOPE : (h + 1) * QK_DIM], cos_v, sin_v)
            qh_ref[h] = jnp.concatenate([qn, qr], axis=1).astype(BF16)
            kn = kv_ref[:, h * 128 : h * 128 + QK_NOPE].astype(BF16)
            kh_ref[h] = jnp.concatenate([kn, kr], axis=1)
            vh_ref[h] = kv_ref[:, h * 128 + QK_NOPE : (h + 1) * 128].astype(BF16)

    return _heads_call(
        body,
        name,
        B,
        S,
        [(q, "row"), (kv, "row"), (proj, "row"), (cos, "row"), (sin, "row")],
        [((H, B, S, QK_DIM), BF16, "heads"), ((H, B, S, QK_DIM), BF16, "heads"), ((H, B, S, V_HEAD), BF16, "heads")],
    )


def _mla_prep_bwd(dqh, dkh, dvh, cos, sin, name):
    H, B, S, _ = dqh.shape

    def body(dqh_ref, dkh_ref, dvh_ref, cos_ref, sin_ref, dq_ref, dkv_ref, dkr_ref):
        cos_v, sin_v = cos_ref[...], sin_ref[...]
        dkr = jnp.zeros((cos_v.shape[0], QK_ROPE), F32)
        for h in range(H):
            dqv = dqh_ref[h].astype(F32)
            dq_ref[:, h * QK_DIM : h * QK_DIM + QK_NOPE] = dqv[:, :QK_NOPE].astype(BF16)
            dq_ref[:, h * QK_DIM + QK_NOPE : (h + 1) * QK_DIM] = _rope_t(dqv[:, QK_NOPE:], cos_v, sin_v).astype(BF16)
            dkv = dkh_ref[h].astype(F32)
            dkv_ref[:, h * 128 : h * 128 + QK_NOPE] = dkv[:, :QK_NOPE].astype(BF16)
            dkv_ref[:, h * 128 + QK_NOPE : (h + 1) * 128] = dvh_ref[h]
            dkr = dkr + dkv[:, QK_NOPE:]
        dkr_ref[...] = _rope_t(dkr, cos_v, sin_v)

    return _heads_call(
        body,
        name,
        B,
        S,
        [(dqh, "heads"), (dkh, "heads"), (dvh, "heads"), (cos, "row"), (sin, "row")],
        [((B, S, H * QK_DIM), BF16, "row"), ((B, S, H * 128), BF16, "row"), ((B, S, QK_ROPE), F32, "row")],
    )


LOG2E = 1.4426950408889634
ATTN_TILE = 512


def _tril_mask(t):
    return lax.broadcasted_iota(jnp.int32, (t, t), 0) >= lax.broadcasted_iota(jnp.int32, (t, t), 1)


def _attn_fwd(qh, kh, vh, name):
    H, B, S, _ = qh.shape
    t = min(ATTN_TILE, S)
    scale = QK_DIM**-0.5
    c2 = scale * LOG2E

    def body(q_ref, k_ref, v_ref, o_ref, lse_ref):
        i = pl.program_id(2)
        qs = [q_ref[0], q_ref[1]]

        def step(j, carry, diagonal):
            rows = pl.ds(pl.multiple_of(j * t, t), t)
            out = []
            for hh in range(2):
                m, l, acc = carry[hh]
                s = _bdot_nt(qs[hh], k_ref[hh, rows, :])
                if diagonal:
                    s = jnp.where(_tril_mask(t), s, NEG)
                m_new = jnp.maximum(m, jnp.max(s, axis=1, keepdims=True))
                p = jnp.exp2((s - m_new) * c2)
                a = jnp.exp2((m - m_new) * c2)
                l = a * l + jnp.sum(p, axis=1, keepdims=True)
                acc = a * acc + _bdot(p, v_ref[hh, rows, :])
                out.append((m_new, l, acc))
            return tuple(out)

        one = (jnp.full((t, 1), NEG, F32), jnp.zeros((t, 1), F32), jnp.zeros((t, V_HEAD), F32))
        carry = lax.fori_loop(0, i, lambda j, cy: step(j, cy, False), (one, one))
        carry = step(i, carry, True)
        outs = []
        for hh in range(2):
            m, l, acc = carry[hh]
            outs.append(acc / l)
            lse_ref[hh] = m * scale + jnp.log(l)
        o_ref[...] = jnp.concatenate(outs, axis=1).astype(BF16)

    return pl.pallas_call(
        body,
        grid=(B, H // 2, S // t),
        in_specs=[
            pl.BlockSpec((2, None, t, QK_DIM), lambda b, p, i: (p, b, i, 0)),
            pl.BlockSpec((2, None, S, QK_DIM), lambda b, p, i: (p, b, 0, 0)),
            pl.BlockSpec((2, None, S, V_HEAD), lambda b, p, i: (p, b, 0, 0)),
        ],
        out_specs=[
            pl.BlockSpec((None, t, 2 * V_HEAD), lambda b, p, i: (b, i, p)),
            pl.BlockSpec((2, None, t, 1), lambda b, p, i: (p, b, i, 0)),
        ],
        out_shape=[SDS((B, S, H * V_HEAD), BF16), SDS((H, B, S, 1), F32)],
        name=name,
        compiler_params=_cparams("parallel", "parallel", "arbitrary"),
    )(qh, kh, vh)


def _attn_bwd(qh, kh, vh, o, do, lse, name):
    H, B, S, _ = qh.shape
    t = min(ATTN_TILE, S)
    nq = S // t
    scale = QK_DIM**-0.5
    c2 = scale * LOG2E

    def body(q_ref, k_ref, v_ref, o_ref, do_ref, lse_ref, dq_ref, dk_ref, dv_ref, dq_acc, delta_ref, lse2_ref):
        prod = o_ref[...].astype(F32) * do_ref[...].astype(F32)
        for hh in range(2):
            delta_ref[hh] = jnp.sum(prod[:, hh * V_HEAD : (hh + 1) * V_HEAD], axis=1, keepdims=True)
            lse2_ref[hh] = lse_ref[hh] * LOG2E
        dq_acc[...] = jnp.zeros_like(dq_acc)

        def kloop(j, _):
            krows = pl.ds(pl.multiple_of(j * t, t), t)
            ks = [k_ref[0, krows, :], k_ref[1, krows, :]]
            vs = [v_ref[0, krows, :], v_ref[1, krows, :]]

            def qstep(i, carry, diagonal):
                qrows = pl.ds(pl.multiple_of(i * t, t), t)
                do_i = do_ref[qrows, :]
                out = []
                for hh in range(2):
                    dk, dv = carry[hh]
                    q = q_ref[hh, qrows, :]
                    do_h = do_i[:, hh * V_HEAD : (hh + 1) * V_HEAD]
                    s = _bdot_nt(q, ks[hh])
                    p = jnp.exp2(s * c2 - lse2_ref[hh, qrows, :])
                    if diagonal:
                        p = jnp.where(_tril_mask(t), p, 0.0)
                    dv = dv + _bdot_tn(p, do_h)
                    dp = _bdot_nt(do_h, vs[hh])
                    ds = (p * (dp - delta_ref[hh, qrows, :])).astype(BF16)
                    dk = dk + _bdot_tn(ds, q)
                    dq_acc[hh, qrows, :] += _bdot(ds, ks[hh])
                    out.append((dk, dv))
                return tuple(out)

            one = (jnp.zeros((t, QK_DIM), F32), jnp.zeros((t, V_HEAD), F32))
            carry = qstep(j, (one, one), True)
            carry = lax.fori_loop(j + 1, nq, lambda i, cy: qstep(i, cy, False), carry)
            for hh in range(2):
                dk_ref[hh, krows, :] = (carry[hh][0] * scale).astype(BF16)
                dv_ref[hh, krows, :] = carry[hh][1].astype(BF16)
            return 0

        lax.fori_loop(0, nq, kloop, 0)
        dq_ref[...] = (dq_acc[...] * scale).astype(BF16)

    hspec = lambda w: pl.BlockSpec((2, None, S, w), lambda b, p: (p, b, 0, 0))
    ospec = pl.BlockSpec((None, S, 2 * V_HEAD), lambda b, p: (b, 0, p))
    return pl.pallas_call(
        body,
        grid=(B, H // 2),
        in_specs=[hspec(QK_DIM), hspec(QK_DIM), hspec(V_HEAD), ospec, ospec, hspec(1)],
        out_specs=[hspec(QK_DIM), hspec(QK_DIM), hspec(V_HEAD)],
        out_shape=[SDS((H, B, S, QK_DIM), BF16), SDS((H, B, S, QK_DIM), BF16), SDS((H, B, S, V_HEAD), BF16)],
        scratch_shapes=[pltpu.VMEM((2, S, QK_DIM), F32), pltpu.VMEM((2, S, 1), F32), pltpu.VMEM((2, S, 1), F32)],
        name=name,
        compiler_params=_cparams("parallel", "parallel"),
    )(qh, kh, vh, o, do, lse)


def _hgrn_pre(q, fx, lb):
    sig = jax.nn.sigmoid(fx)
    f = lb + (1.0 - lb) * sig
    return jax.nn.silu(q), 1.0 - f, jnp.log(f)


def _hgrn_gate(o, gg, gn):
    return _rms_fn(o, gn) * jax.nn.silu(gg)


def _tri(n, lower):
    r = lax.broadcasted_iota(jnp.int32, (n, n), 0)
    c = lax.broadcasted_iota(jnp.int32, (n, n), 1)
    return ((r >= c) if lower else (r <= c)).astype(F32)


def _hgrn_intra_fwd(qs, k, v, b):
    C, SB = qs.shape[0], min(HGRN_SUB, qs.shape[0])
    ridx = lax.broadcasted_iota(jnp.int32, (SUBLANES, 1), 0)
    outs = []
    for i in range(C // SB):
        r0 = i * SB
        qi, ki, vi, bi = qs[r0 : r0 + SB], k[r0 : r0 + SB], v[r0 : r0 + SB], b[r0 : r0 + SB]
        ng = SB // SUBLANES
        qg = [qi[g * SUBLANES : (g + 1) * SUBLANES] for g in range(ng)]
        bg = [bi[g * SUBLANES : (g + 1) * SUBLANES] for g in range(ng)]
        accg = [jnp.zeros((SUBLANES, v.shape[1]), F32) for _ in range(ng)]
        for s in range(SB):
            gs, so = divmod(s, SUBLANES)
            k_s, v_s, b_s = ki[s : s + 1], vi[s : s + 1], bi[s : s + 1]
            for tg in range(gs, ng):
                if tg == gs:
                    mask = ridx >= so
                    w = jnp.where(mask, qg[tg] * k_s * jnp.exp(jnp.where(mask, bg[tg] - b_s, 0.0)), 0.0)
                else:
                    w = qg[tg] * k_s * jnp.exp(bg[tg] - b_s)
                accg[tg] = accg[tg] + jnp.sum(w, axis=1, keepdims=True) * v_s
        acc = jnp.concatenate(accg, axis=0)
        if i > 0:
            ref = bi[0:1]
            qt = qi * jnp.exp(bi - ref)
            kt = k[:r0] * jnp.exp(ref - b[:r0])
            acc = acc + _bdot(_mdot_nt(qt, kt), v[:r0])
        outs.append(acc)
    return jnp.concatenate(outs, axis=0)


def _hgrn_intra_bwd(qs, k, v, b, do):
    C, SB = qs.shape[0], min(HGRN_SUB, qs.shape[0])
    nb = C // SB
    ridx = lax.broadcasted_iota(jnp.int32, (SUBLANES, 1), 0)
    dq_p = [None] * nb
    dk_p = [jnp.zeros((SB, k.shape[1]), F32) for _ in range(nb)]
    dv_p = [jnp.zeros((SB, v.shape[1]), F32) for _ in range(nb)]
    for i in range(nb):
        r0 = i * SB
        qi, ki, vi, bi, doi = qs[r0 : r0 + SB], k[r0 : r0 + SB], v[r0 : r0 + SB], b[r0 : r0 + SB], do[r0 : r0 + SB]
        ng = SB // SUBLANES
        qg = [qi[g * SUBLANES : (g + 1) * SUBLANES] for g in range(ng)]
        bg = [bi[g * SUBLANES : (g + 1) * SUBLANES] for g in range(ng)]
        dog = [doi[g * SUBLANES : (g + 1) * SUBLANES] for g in range(ng)]
        dqg =[jnp.zeros((SUBLANES, k.shape[1]), F32) for _ in range(ng)]
        dkg = [jnp.zeros((SUBLANES, k.shape[1]), F32) for _ in range(ng)]
        dvg = [jnp.zeros((SUBLANES, v.shape[1]), F32) for _ in range(ng)]
        for s in range(SB):
            gs, so = divmod(s, SUBLANES)
            k_s, v_s, b_s = ki[s : s + 1], vi[s : s + 1], bi[s : s + 1]
            dk_s = jnp.zeros((SUBLANES, k.shape[1]), F32)
            dv_s = jnp.zeros((SUBLANES, v.shape[1]), F32)
            for tg in range(gs, ng):
                if tg == gs:
                    mask = ridx >= so
                    e = jnp.where(mask, jnp.exp(jnp.where(mask, bg[tg] - b_s, 0.0)), 0.0)
                else:
                    e = jnp.exp(bg[tg] - b_s)
                da = jnp.sum(dog[tg] * v_s, axis=1, keepdims=True)
                qe = qg[tg] * e
                a = jnp.sum(qe * k_s, axis=1, keepdims=True)
                dqg[tg] = dqg[tg] + da * (k_s * e)
                dk_s = dk_s + da * qe
                dv_s = dv_s + a * dog[tg]
            dkg[gs] = jnp.where(ridx == so, dkg[gs] + jnp.sum(dk_s, axis=0, keepdims=True), dkg[gs])
            dvg[gs] = jnp.where(ridx == so, dvg[gs] + jnp.sum(dv_s, axis=0, keepdims=True), dvg[gs])
        dqi = jnp.concatenate(dqg, axis=0)
        dki = jnp.concatenate(dkg, axis=0)
        dvi = jnp.concatenate(dvg, axis=0)
        if i > 0:
            ref = bi[0:1]
            eq = jnp.exp(bi - ref)
            ek = jnp.exp(ref - b[:r0])
            qt = qi * eq
            kt = k[:r0] * ek
            A = _mdot_nt(qt, kt)
            dA = _bdot_nt(doi, v[:r0])
            dvl = _bdot_tn(A, doi)
            dqi = dqi + _mdot(dA, kt) * eq
            dkl = _mdot_tn(dA, qt) * ek
            for j in range(i):
                dk_p[j] = dk_p[j] + dkl[j * SB : (j + 1) * SB]
                dv_p[j] = dv_p[j] + dvl[j * SB : (j + 1) * SB]
        dq_p[i] = dqi
        dk_p[i] = dk_p[i] + dki
        dv_p[i] = dv_p[i] + dvi
    return jnp.concatenate(dq_p, axis=0), jnp.concatenate(dk_p, axis=0), jnp.concatenate(dv_p, axis=0)


def _hgrn_fwd(proj, lb, gn, name):
    B, S, W = proj.shape
    HK = W // 4
    H = HK // HGRN_K
    C = min(HGRN_CHUNK, S)
    N = S // C

    HP = HGRN_PAR if H % HGRN_PAR == 0 else 1
    WP = HP * HGRN_K

    def body(q_ref, f_ref, i_ref, g_ref, lb_ref, gn_ref, og_ref, o_ref, st_ref):
        gn_v = gn_ref[...]
        tril = _tri(C, True)

        def chunk(n, sts):
            rows = pl.ds(pl.multiple_of(n * C, C), C)
            out = []
            for hh in range(HP):
                ln = slice(hh * HGRN_K, (hh + 1) * HGRN_K)
                st = sts[hh]
                qs, k, g = _hgrn_pre(q_ref[rows, ln], f_ref[rows, ln], lb_ref[:, ln])
                v = i_ref[rows, ln]
                b = _hdot(tril, g)
                st_ref[hh, n] = st
                o = _hgrn_intra_fwd(qs, k, v, b) + _bdot_nt(qs * jnp.exp(b), st)
                bl = b[C - 1 : C]
                out.append(st * jnp.exp(bl) + _bdot_tn(v, k * jnp.exp(bl - b)))
                o_ref[rows, ln] = o
                og_ref[rows, ln] = _hgrn_gate(o, g_ref[rows, ln], gn_v).astype(BF16)
            return tuple(out)

        lax.fori_loop(0, N, chunk, tuple(jnp.zeros((HGRN_K, HGRN_K), F32) for _ in range(HP)))

    col = lambda part: pl.BlockSpec((None, S, WP), lambda b, h: (b, 0, part * (H // HP) + h))
    return pl.pallas_call(
        body,
        grid=(B, H // HP),
        in_specs=[col(0), col(1), col(2), col(3), pl.BlockSpec((1, WP), lambda b, h: (0, h)), pl.BlockSpec((1, HGRN_K), lambda b, h: (0, 0))],
        out_specs=[col(0), col(0), pl.BlockSpec((None, HP, N, HGRN_K, HGRN_K), lambda b, h: (b, h, 0, 0, 0))],
        out_shape=[SDS((B, S, HK), BF16), SDS((B, S, HK), F32), SDS((B, H, N, HGRN_K, HGRN_K), F32)],
        name=name,
        compiler_params=_cparams("parallel", "parallel"),
    )(proj, proj, proj, proj, lb, gn)


def _hgrn_bwd(proj, lb, gn, o_pre, states, dog, name):
    B, S, W = proj.shape
    HK = W // 4
    H = HK // HGRN_K
    C = min(HGRN_CHUNK, S)
    N = S // C

    HP = HGRN_PAR if H % HGRN_PAR == 0 else 1
    WP = HP * HGRN_K

    def body(q_ref, f_ref, i_ref, g_ref, lb_ref, gn_ref, o_ref, st_ref, dog_ref, dq_ref, df_ref, di_ref, dg_ref, dlb_ref, dgn_ref):
        gn_v = gn_ref[...]
        tril = _tri(C, True)
        triu = _tri(C, False)

        def chunk(idx, carry):
            n = N - 1 - idx
            rows = pl.ds(pl.multiple_of(n * C, C), C)
            out = []
            for hh in range(HP):
                ln = slice(hh * HGRN_K, (hh + 1) * HGRN_K)
                dst, dlb, dgn = carry[hh]
                (qs, k, g), pre_vjp = jax.vjp(_hgrn_pre, q_ref[rows, ln], f_ref[rows, ln], lb_ref[:, ln])
                v = i_ref[rows, ln]
                _, gate_vjp = jax.vjp(_hgrn_gate, o_ref[rows, ln], g_ref[rows, ln], gn_v)
                do, dgg, dgn_c = gate_vjp(dog_ref[rows, ln])
                b = _hdot(tril, g)
                st0 = st_ref[hh, n]
                eb = jnp.exp(b)
                bl = b[C - 1 : C]
                ebl = jnp.exp(bl)
                ekb = jnp.exp(bl - b)
                qe = qs * eb
                kt = k * ekb
                dqs, dk, dv = _hgrn_intra_bwd(qs, k, v, b, do)
                dqs = dqs + _bdot(do, st0) * eb
                dk = dk + _bdot(v, dst) * ekb
                dv = dv + _bdot_nt(kt, dst)
                st1 = st0 * ebl + _bdot_tn(v, kt)
                dbl = jnp.sum(st1 * dst, axis=0, keepdims=True)
                dst = dst * ebl + _bdot_tn(do, qe)
                dgl = _hdot(triu, qs * dqs - k * dk) + dbl
                dq_pre, dfx, dlb_c = pre_vjp((dqs, dk, dgl))
                dq_ref[rows, ln] = dq_pre.astype(BF16)
                df_ref[rows, ln] = dfx.astype(BF16)
                di_ref[rows, ln] = dv.astype(BF16)
                dg_ref[rows, ln] = dgg.astype(BF16)
                out.append((dst, dlb + dlb_c, dgn + dgn_c))
            return tuple(out)

        zero = jnp.zeros((1, HGRN_K), F32)
        one = (jnp.zeros((HGRN_K, HGRN_K), F32), zero, zero)
        res = lax.fori_loop(0, N, chunk, tuple(one for _ in range(HP)))
        for hh in range(HP):
            dlb_ref[hh] = res[hh][1]
            dgn_ref[hh] = res[hh][2]

    col = lambda part: pl.BlockSpec((None, S, WP), lambda b, h: (b, 0, part * (H // HP) + h))
    vec = pl.BlockSpec((None, HP, 1, HGRN_K), lambda b, h: (b, h, 0, 0))
    return pl.pallas_call(
        body,
        grid=(B, H // HP),
        in_specs=[
            col(0), col(1), col(2), col(3),
            pl.BlockSpec((1, WP), lambda b, h: (0, h)),
            pl.BlockSpec((1, HGRN_K), lambda b, h: (0, 0)),
            col(0),
            pl.BlockSpec((None, HP, N, HGRN_K, HGRN_K), lambda b, h: (b, h, 0, 0, 0)),
            col(0),
        ],
        out_specs=[col(0), col(0), col(0), col(0), vec, vec],
        out_shape=[SDS((B, S, HK), BF16)] * 4 + [SDS((B, H, 1, HGRN_K), F32)] * 2,
        name=name,
        compiler_params=_cparams("parallel", "parallel"),
    )(proj, proj, proj, proj, lb, gn, o_pre, states, dog)


def _ada_fwd(c_all, w, b, name):
    Bg, D = c_all.shape
    L, _, n = w.shape

    def body(c_ref, w_ref, b_ref, o_ref):
        o_ref[...] = _bdot(jax.nn.silu(c_ref[...]), w_ref[...]) + b_ref[...]

    return pl.pallas_call(
        body,
        grid=(L,),
        in_specs=[
            pl.BlockSpec((Bg, D), lambda l: (0, 0)),
            pl.BlockSpec((None, D, n), lambda l: (l, 0, 0)),
            pl.BlockSpec((None, 1, n), lambda l: (l, 0, 0)),
        ],
        out_specs=pl.BlockSpec((None, Bg, n), lambda l: (l, 0, 0)),
        out_shape=SDS((L, Bg, n), F32),
        name=name,
        compiler_params=_cparams("parallel"),
    )(c_all, w, b)


def _ada_bwd(c_all, dmod, name):
    Bg, D = c_all.shape
    L, _, n = dmod.shape

    def body(c_ref, d_ref, dw_ref, db_ref):
        d = d_ref[...]
        dw_ref[...] = _bdot_tn(jax.nn.silu(c_ref[...]), d)
        db_ref[...] = jnp.sum(d, axis=0, keepdims=True)

    return pl.pallas_call(
        body,
        grid=(L,),
        in_specs=[pl.BlockSpec((Bg, D), lambda l: (0, 0)), pl.BlockSpec((None, Bg, n), lambda l: (l, 0, 0))],
        out_specs=[pl.BlockSpec((None, D, n), lambda l: (l, 0, 0)), pl.BlockSpec((None, 1, n), lambda l: (l, 0, 0))],
        out_shape=[SDS((L, D, n), F32), SDS((L, 1, n), F32)],
        name=name,
        compiler_params=_cparams("parallel"),
    )(c_all, dmod)


def _adamw(w, gs, m, v, name):
    shape = w.shape
    cols = shape[-1]
    rows = w.size // cols
    tr = rows
    for cand in (512, 256, 128, 64, 32, 16, 8):
        if rows % cand == 0 and cand * cols * 4 <= 2 * 1024 * 1024:
            tr = cand
            break
    as2d = lambda a: a.reshape(rows, cols)
    ng = len(gs)
    c1 = 1.0 / (1.0 - ADAM_B1**ADAM_STEP)
    c2 = 1.0 / (1.0 - ADAM_B2**ADAM_STEP)

    def body(*refs):
        w_ref, m_ref, v_ref = refs[0], refs[1], refs[2]
        g_refs = refs[3 : 3 + ng]
        g_out, d_out, m_out, v_out = refs[3 + ng :]
        g = g_refs[0][...].astype(F32)
        for r in g_refs[1:]:
            g = g + r[...].astype(F32)
        m_new = ADAM_B1 * m_ref[...] + (1.0 - ADAM_B1) * g
        v_new = ADAM_B2 * v_ref[...] + (1.0 - ADAM_B2) * jnp.square(g)
        g_out[...] = g
        m_out[...] = m_new
        v_out[...] = v_new
        d_out[...] = -ADAM_LR * ((m_new * c1) / (jnp.sqrt(v_new * c2) + ADAM_EPS) + ADAM_WD * w_ref[...])

    spec = pl.BlockSpec((tr, cols), lambda i: (i, 0))
    outs = pl.pallas_call(
        body,
        grid=(rows // tr,),
        in_specs=[spec] * (3 + ng),
        out_specs=[spec] * 4,
        out_shape=[SDS((rows, cols), F32)] * 4,
        name=name,
        compiler_params=_cparams("parallel"),
    )(as2d(w), as2d(m), as2d(v), *[as2d(g) for g in gs])
    return tuple(o.reshape(shape) for o in outs)


def _sum4(own, recv, name):
    shape = own.shape
    cols = shape[-1]
    rows = own.size // cols
    tr = rows
    for cand in (512, 256, 128, 64, 32, 16):
        if rows % cand == 0 and cand * cols * 4 <= 2 * 1024 * 1024:
            tr = cand
            break

    def body(own_ref, recv_ref, o_ref):
        acc = own_ref[...].astype(F32)
        for r in range(3):
            acc = acc + recv_ref[r].astype(F32)
        o_ref[...] = acc

    out = pl.pallas_call(
        body,
        grid=(rows // tr,),
        in_specs=[pl.BlockSpec((tr, cols), lambda i: (i, 0)), pl.BlockSpec((3, tr, cols), lambda i: (0, i, 0))],
        out_specs=pl.BlockSpec((tr, cols), lambda i: (i, 0)),
        out_shape=SDS((rows, cols), F32),
        name=name,
        compiler_params=_cparams("parallel"),
    )(own.reshape(rows, cols), recv.reshape(3, rows, cols))
    return out.reshape(shape)


def _my_place():
    return lax.axis_index("x"), lax.axis_index("y"), lax.axis_index("c")


def _flip(v, bit):
    return 1 - v if bit else v


def _allgather8(x, name):
    r, n = x.shape

    def body(x_ref, o_ref, send_sems, recv_sems, local_sem):
        mx, my, mc = _my_place()
        me = 4 * mx + 2 * my + mc
        mine = pltpu.make_async_copy(x_ref, o_ref.at[me], local_sem)
        mine.start()
        sends = []
        for rel in range(1, 8):
            peer = (_flip(mx, rel & 4), _flip(my, rel & 2), _flip(mc, rel & 1))
            cp = pltpu.make_async_remote_copy(
                src_ref=x_ref, dst_ref=o_ref.at[me], send_sem=send_sems.at[rel - 1], recv_sem=recv_sems.at[rel - 1],
                device_id=peer, device_id_type=MESH,
            )
            cp.start()
            sends.append(cp)
        for rel in range(1, 8):
            px, py, pc = _flip(mx, rel & 4), _flip(my, rel & 2), _flip(mc, rel & 1)
            pltpu.make_async_remote_copy(
                src_ref=x_ref, dst_ref=o_ref.at[4 * px + 2 * py + pc], send_sem=send_sems.at[rel - 1],
                recv_sem=recv_sems.at[rel - 1], device_id=(px, py, pc), device_id_type=MESH,
            ).wait_recv()
        for cp in sends:
            cp.wait_send()
        mine.wait()

    return pl.pallas_call(
        body,
        out_shape=SDS((8, r, n), x.dtype),
        in_specs=[pl.BlockSpec(memory_space=pl.ANY)],
        out_specs=pl.BlockSpec(memory_space=pl.ANY),
        scratch_shapes=[pltpu.SemaphoreType.DMA((7,)), pltpu.SemaphoreType.DMA((7,)), pltpu.SemaphoreType.DMA],
        name=name,
    )(x)


_HBM = pl.BlockSpec(memory_space=pl.ANY)


_SEM = pl.BlockSpec(memory_space=pltpu.SEMAPHORE)
_HBM_ONLY = pl.BlockSpec(memory_space=pltpu.HBM)
_EFFECT = pltpu.SideEffectType.DATAFLOW_SIDE_EFFECTING


def _in_hbm(a):
    return pltpu.with_memory_space_constraint(a, pltpu.HBM)


def _gather_start(lands, after, name):
    n = len(lands)

    def body(*refs):
        land = refs[:n]
        send_sems, recv_sems = refs[n + 1], refs[n + 2]
        token = refs[-1]
        mx, my, mc = _my_place()
        for i in range(n):
            for rel in range(1, 4):
                pltpu.make_async_remote_copy(
                    src_ref=land[i].at[2 * mx + my], dst_ref=land[i].at[2 * mx + my],
                    send_sem=send_sems.at[3 * i + rel - 1], recv_sem=recv_sems.at[3 * i + rel - 1],
                    device_id=(_flip(mx, rel & 2), _flip(my, rel & 1), mc), device_id_type=MESH,
                ).start()
        token[...] = jnp.zeros_like(token)

    outs = pl.pallas_call(
        body,
        name=name,
        out_shape=(
            pltpu.SemaphoreType.DMA((3 * n,)), pltpu.SemaphoreType.DMA((3 * n,)),
            *[pltpu.HBM(a.shape, a.dtype) for a in lands], SDS((8, LANES), F32),
        ),
        in_specs=[_HBM_ONLY] * n + [_HBM],
        out_specs=(_SEM, _SEM, *[_HBM_ONLY] * n, pl.BlockSpec(memory_space=pltpu.VMEM)),
        input_output_aliases={i: 2 + i for i in range(n)},
        compiler_params=pltpu.CompilerParams(has_side_effects=_EFFECT),
    )(*[_in_hbm(a) for a in lands], after)
    return outs[0], outs[1], list(outs[2 : 2 + n]), outs[-1]


def _gather_wait(send_sems, recv_sems, lands, after, name):
    n = len(lands)

    def body(*refs):
        land = refs[:n]
        s_sems, r_sems = refs[n], refs[n + 1]
        mx, my, mc = _my_place()
        for i in range(n):
            for rel in range(1, 4):
                px, py = _flip(mx, rel & 2), _flip(my, rel & 1)
                cp = pltpu.make_async_remote_copy(
                    src_ref=land[i].at[2 * mx + my], dst_ref=land[i].at[2 * px + py],
                    send_sem=s_sems.at[3 * i + rel - 1], recv_sem=r_sems.at[3 * i + rel - 1],
                    device_id=(px, py, mc), device_id_type=MESH,
                )
                cp.wait_send()
                cp.wait_recv()

    outs = pl.pallas_call(
        body,
        name=name,
        out_shape=tuple(pltpu.HBM(a.shape, a.dtype) for a in lands),
        in_specs=[_HBM_ONLY] * n + [_SEM, _SEM, _HBM],
        out_specs=[_HBM_ONLY] * n,
        input_output_aliases={i: i for i in range(n)},
        compiler_params=pltpu.CompilerParams(has_side_effects=_EFFECT),
    )(*lands, send_sems, recv_sems, after)
    return list(outs)


def _scatter_start(slabs, lands, places, name):
    n = len(slabs)

    def body(*refs):
        ins, land = refs[:n], refs[n : 2 * n]
        send_sems, recv_sems = refs[2 * n], refs[2 * n + 1]
        token = refs[-1]
        mx, my, mc = _my_place()
        for i in range(n):
            for rel in range(1, 4):
                px, py = _flip(mx, rel & 2), _flip(my, rel & 1)
                pltpu.make_async_remote_copy(
                    src_ref=ins[i].at[2 * px + py], dst_ref=land[i].at[rel - 1, places[i]],
                    send_sem=send_sems.at[3 * i + rel - 1], recv_sem=recv_sems.at[3 * i + rel - 1],
                    device_id=(px, py, mc), device_id_type=MESH,
                ).start()
        token[...] = jnp.zeros_like(token)

    outs = pl.pallas_call(
        body,
        name=name,
        out_shape=(
            pltpu.SemaphoreType.DMA((3 * n,)), pltpu.SemaphoreType.DMA((3 * n,)),
            *[pltpu.HBM(a.shape, a.dtype) for a in slabs], *[pltpu.HBM(a.shape, a.dtype) for a in lands],
            SDS((8, LANES), F32),
        ),
        in_specs=[_HBM_ONLY] * (2 * n),
        out_specs=(_SEM, _SEM, *[_HBM_ONLY] * (2 * n), pl.BlockSpec(memory_space=pltpu.VMEM)),
        input_output_aliases={i: 2 + i for i in range(2 * n)},
        compiler_params=pltpu.CompilerParams(has_side_effects=_EFFECT),
    )(*[_in_hbm(a) for a in slabs], *[_in_hbm(a) for a in lands])
    return outs[0], outs[1], list(outs[2 : 2 + n]), list(outs[2 + n : 2 + 2 * n]), outs[-1]


def _scatter_wait(send_sems, recv_sems, slabs, lands, places, after, name):
    n = len(slabs)

    def body(*refs):
        ins, land = refs[:n], refs[n : 2 * n]
        s_sems, r_sems = refs[2 * n], refs[2 * n + 1]
        mx, my, mc = _my_place()
        for i in range(n):
            for rel in range(1, 4):
                px, py = _flip(mx, rel & 2), _flip(my, rel & 1)
                cp = pltpu.make_async_remote_copy(
                    src_ref=ins[i].at[2 * px + py], dst_ref=land[i].at[rel - 1, places[i]],
                    send_sem=s_sems.at[3 * i + rel - 1], recv_sem=r_sems.at[3 * i + rel - 1],
                    device_id=(px, py, mc), device_id_type=MESH,
                )
                cp.wait_send()
                cp.wait_recv()

    outs = pl.pallas_call(
        body,
        name=name,
        out_shape=(*[pltpu.HBM(a.shape, a.dtype) for a in slabs], *[pltpu.HBM(a.shape, a.dtype) for a in lands]),
        in_specs=[_HBM_ONLY] * (2 * n) + [_SEM, _SEM, _HBM],
        out_specs=[_HBM_ONLY] * (2 * n),
        input_output_aliases={i: i for i in range(2 * n)},
        compiler_params=pltpu.CompilerParams(has_side_effects=_EFFECT),
    )(*slabs, *lands, send_sems, recv_sems, after)
    return list(outs[:n]), list(outs[n:])


def _swap_sibling(parts, name):
    n = len(parts)

    def body(*refs):
        ins, outs = refs[:n], refs[n : 2 * n]
        send_sems, recv_sems = refs[2 * n :]
        mx, my, mc = _my_place()
        sends = []
        for i in range(n):
            cp = pltpu.make_async_remote_copy(
                src_ref=ins[i], dst_ref=outs[i], send_sem=send_sems.at[i], recv_sem=recv_sems.at[i],
                device_id=(mx, my, 1 - mc), device_id_type=MESH,
            )
            cp.start()
            sends.append(cp)
        for cp in sends:
            cp.wait_recv()
        for cp in sends:
            cp.wait_send()

    return pl.pallas_call(
        body,
        out_shape=[SDS(s.shape, s.dtype) for s in parts],
        in_specs=[_HBM] * n,
        out_specs=[_HBM] * n,
        scratch_shapes=[pltpu.SemaphoreType.DMA((n,)), pltpu.SemaphoreType.DMA((n,))],
        name=name,
    )(*parts)


def _pad_rows(a, rows):
    return jnp.pad(a, ((0, rows - a.shape[0]), (0, 0)))


def kernel(x, c, positions, mla_w_in, mla_q_norm, mla_w_qb, mla_kv_norm, mla_w_kvb, mla_w_o, hgrn_lb, hgrn_w_in, hgrn_g_norm, hgrn_w_o, ffn_w_in, ffn_w_out, ada_w, ada_b, ln_g, ln_b, loss_target, m_mla_w_in, m_mla_q_norm, m_mla_w_qb, m_mla_kv_norm, m_mla_w_kvb, m_mla_w_o, m_hgrn_lb, m_hgrn_w_in, m_hgrn_g_norm, m_hgrn_w_o, m_ffn_w_in, m_ffn_w_out, m_ada_w, m_ada_b, m_ln_g, m_ln_b, v_mla_w_in, v_mla_q_norm, v_mla_w_qb, v_mla_kv_norm, v_mla_w_kvb, v_mla_w_o, v_hgrn_lb, v_hgrn_w_in, v_hgrn_g_norm, v_hgrn_w_o, v_ffn_w_in, v_ffn_w_out, v_ada_w, v_ada_b, v_ln_g, v_ln_b):
    B, S, D = x.shape
    T = B * S
    depth = ada_w.shape[0]
    n_mla, n_hgrn = mla_w_in.shape[0], hgrn_w_in.shape[0]
    n_sub = 2 * depth
    alpha = (2.0 * depth) ** 0.25
    mx, my, mc = _my_place()
    me = 4 * mx + 2 * my + mc
    k_me = 2 * mx + my
    Bg = 8 * B
    HK = hgrn_w_o.shape[1] * 4
    dq = D // 4

    lbw = hgrn_lb.shape[1]
    first = jnp.zeros((8, max(D, 4 * lbw)), F32)
    first = first.at[:B, :D].set(c).at[B : B + n_hgrn, :lbw].set(hgrn_lb)
    first_all = _allgather8(first, "gather_cond")
    c_all = first_all[:, :B, :D].reshape(Bg, D)
    lb_logits = jnp.concatenate([first_all[2 * k, B : B + n_hgrn, :lbw] for k in range(4)], axis=1)

    def lower_bounds_fn(logits):
        soft = jax.nn.softmax(logits, axis=0)
        return jnp.cumsum(soft, axis=0) - soft[0]

    lower_bounds, lower_bounds_vjp = jax.vjp(lower_bounds_fn, lb_logits)

    n_ada = ada_w.shape[-1]
    mod_part = _ada_fwd(c_all, ada_w.reshape(n_sub, D, n_ada), ada_b.reshape(n_sub, 1, n_ada), "ada_fwd")
    mod_all = _allgather8(mod_part.reshape(n_sub * Bg, n_ada), "gather_mod").reshape(8, n_sub, Bg, n_ada)
    mod = jnp.concatenate([mod_all[2 * k] for k in range(4)], axis=-1)
    mod = lax.dynamic_slice_in_dim(mod, me * B, B, axis=1)
    shift = [mod[j, :, None, :D] for j in range(n_sub)]
    scale = [mod[j, :, None, D : 2 * D] for j in range(n_sub)]
    gate = [mod[j, :, None, 2 * D :] for j in range(n_sub)]

    ln_rows = 2 * n_sub
    ln_local = _pad_rows(jnp.concatenate([ln_g.reshape(n_sub, dq), ln_b.reshape(n_sub, dq)], axis=0), -(-ln_rows // 8) * 8)
    ln_pad = jnp.zeros((ln_local.shape[0], -(-dq // LANES) * LANES), F32).at[:, :dq].set(ln_local)
    ln_all = _allgather8(ln_pad, "gather_ln")
    ln_full = jnp.concatenate([ln_all[2 * k, :ln_rows, :dq] for k in range(4)], axis=1)
    lng = [ln_full[j][None, :] for j in range(n_sub)]
    lnb = [ln_full[n_sub + j][None, :] for j in range(n_sub)]

    main = dict(mla_w_in=mla_w_in, mla_w_qb=mla_w_qb, mla_w_kvb=mla_w_kvb, mla_w_o=mla_w_o, hgrn_w_in=hgrn_w_in,
                hgrn_w_o=hgrn_w_o, ffn_w_in=ffn_w_in, ffn_w_out=ffn_w_out)
    names = list(main)

    def group_kinds(layer, part):
        if part:
            return [("ffn_w_in", layer), ("ffn_w_out", layer)]
        mixer = ["mla_w_in", "mla_w_qb", "mla_w_kvb", "mla_w_o"] if layer % 2 == 0 else ["hgrn_w_in", "hgrn_w_o"]
        return [(k, layer // 2) for k in mixer]

    gathers = {}
    after = mod_all[0, 0, :8, :LANES] + ln_all[0, :8, :LANES]
    for layer in range(depth):
        for part in range(2):
            lands = [lax.dynamic_update_index_in_dim(lax.empty((4,) + main[k].shape[1:], BF16), main[k][i].astype(BF16), k_me, 0)
                     for k, i in group_kinds(layer, part)]
            ssem, rsem, lands, after = _gather_start(lands, after, f"gather_start_l{layer}p{part}")
            gathers[layer, part] = (ssem, rsem, lands)
    scale[0] = scale[0] + after[0, 0]

    def row_w(g):
        return g.reshape(1, g.shape[0] * g.shape[1], g.shape[2])

    def full_w_in(g):
        return jnp.transpose(g, (1, 0, 2)).reshape(1, g.shape[1], 4 * g.shape[2])

    ang = positions.astype(F32)[..., None] * (ROPE_THETA ** (-jnp.arange(0, QK_ROPE, 2, dtype=F32) / QK_ROPE))
    cos, sin = jnp.cos(ang), jnp.sin(ang)

    gq = [mla_q_norm[j][None, :] for j in range(n_mla)]
    gkv = [mla_kv_norm[j][None, :] for j in range(n_mla)]
    gn = [hgrn_g_norm[j][None, :] for j in range(n_hgrn)]

    def r2(a):
        return a.reshape(T, a.shape[-1])

    def r3(a):
        return a.reshape(B, S, a.shape[-1])

    saved = []
    xs = x
    for layer in range(depth):
        j = layer // 2
        sub = 2 * layer
        tag = f"l{layer}"
        ssem, rsem, lands = gathers[layer, 0]
        lands = _gather_wait(ssem, rsem, lands, xs if layer else scale[0], f"gather_wait_{tag}p0")
        wl = {k: g for (k, _), g in zip(group_kinds(layer, 0), lands)}
        if layer == 0:
            h = _modulate(xs, scale[sub], shift[sub], f"mod_{tag}a")
        if layer % 2 == 0:
            wl["mla_w_in"] = full_w_in(wl["mla_w_in"])
            proj = r3(_mm_nn(r2(h), wl["mla_w_in"], F32, f"mla_in_{tag}"))
            qn, kvn = _mla_mid_fwd(proj, gq[j], gkv[j], f"mla_mid_{tag}")
            q = r3(_mm_nn(r2(qn), wl["mla_w_qb"], F32, f"mla_qb_{tag}"))
            kv = r3(_mm_nn(r2(kvn), wl["mla_w_kvb"], F32, f"mla_kvb_{tag}"))
            qh, kh, vh = _mla_prep_fwd(q, kv, proj, cos, sin, f"mla_prep_{tag}")
            o, lse = _attn_fwd(qh, kh, vh, f"attn_{tag}")
            wl["mla_w_o"] = row_w(wl["mla_w_o"])
            y = r3(_mm_nn(r2(o), wl["mla_w_o"], F32, f"mla_o_{tag}"))
            mix = (h, proj, qn, kvn, qh, kh, vh, o, lse)
        else:
            proj = r3(_mm_nn(r2(h), wl["hgrn_w_in"], F32, f"hgrn_in_{tag}"))
            og, o_pre, states = _hgrn_fwd(proj, lower_bounds[j][None, :], gn[j], f"hgrn_{tag}")
            wl["hgrn_w_o"] = row_w(wl["hgrn_w_o"])
            y = r3(_mm_nn(r2(og), wl["hgrn_w_o"], F32, f"hgrn_o_{tag}"))
            mix = (h, proj, og, o_pre, states)
        x1, h2 = _ln_mod_fwd(alpha, xs, y, gate[sub], lng[sub], lnb[sub], scale[sub + 1], shift[sub + 1], f"ln_{tag}a")
        ssem, rsem, lands = gathers[layer, 1]
        lands = _gather_wait(ssem, rsem, lands, x1, f"gather_wait_{tag}p1")
        wl.update({k: g for (k, _), g in zip(group_kinds(layer, 1), lands)})
        a, ug, uu = [r3(t_) for t_ in _ffn_in(r2(h2), wl["ffn_w_in"], f"ffn_in_{tag}")]
        wl["ffn_w_out"] = row_w(wl["ffn_w_out"])
        y2 = r3(_mm_nn(r2(a), wl["ffn_w_out"], F32, f"ffn_out_{tag}"))
        if layer + 1 < depth:
            x2, h_next = _ln_mod_fwd(alpha, x1, y2, gate[sub + 1], lng[sub + 1], lnb[sub + 1], scale[sub + 2], shift[sub + 2], f"ln_{tag}b")
        else:
            x2, h_next = _ln_fwd(alpha, x1, y2, gate[sub + 1], lng[sub + 1], lnb[sub + 1], f"ln_{tag}b"), None
        saved.append((xs, y, x1, y2, mix, h2, ug, uu, a, wl))
        xs, h = x2, h_next

    loss_local, dout = _loss_head(xs, loss_target, "loss_head")
    loss = lax.psum(loss_local, ("x", "y", "c"))

    gw = {k: [None] * main[k].shape[0] for k in names}
    land = {k: lax.empty((3,) + main[k].shape, BF16) for k in names}
    scatters = []
    d_shift, d_scale, d_gate = [None] * n_sub, [None] * n_sub, [None] * n_sub
    d_lng, d_lnb = [None] * n_sub, [None] * n_sub
    d_gq, d_gkv, d_gn, d_lbnd = [None] * n_mla, [None] * n_mla, [None] * n_hgrn, [None] * n_hgrn

    def rows4(g):
        return g.reshape(4, g.shape[1] // 4, g.shape[2])

    def start_scatter(layer, part, token_to):
        kinds = group_kinds(layer, part)
        ssem, rsem, slabs_t, lands_t, token = _scatter_start(
            [gw[k][i] for k, i in kinds], [land[k] for k, _ in kinds], [i for _, i in kinds], f"scatter_start_l{layer}p{part}")
        for (k, i), s_t, l_t in zip(kinds, slabs_t, lands_t):
            gw[k][i], land[k] = s_t, l_t
        scatters.append((layer, part, ssem, rsem))
        if token_to is not None:
            gate[token_to] = gate[token_to] + token[0, 0]

    for layer in reversed(range(depth)):
        j = layer // 2
        sub = 2 * layer
        tag = f"l{layer}"
        xs, y, x1, y2, mix, h2, ug, uu, a, wl = saved[layer]
        if layer + 1 == depth:
            dxr, dy2, d_gate[sub + 1], d_lng[sub + 1], d_lnb[sub + 1] = _ln_bwd(
                alpha, dout, x1, y2, gate[sub + 1], lng[sub + 1], lnb[sub + 1], f"ln_bwd_{tag}b")
        else:
            dxr, dy2, d_gate[sub + 1], d_lng[sub + 1], d_lnb[sub + 1], d_scale[sub + 2], d_shift[sub + 2] = _ln_mod_bwd(
                alpha, dh, dxr, scale[sub + 2], x1, y2, gate[sub + 1], lng[sub + 1], lnb[sub + 1], f"ln_bwd_{tag}b")
        da = r3(_mm_nt(r2(dy2), wl["ffn_w_out"], F32, f"ffn_out_dx_{tag}"))
        gw["ffn_w_out"][layer] = rows4(_mm_tn(r2(a), r2(dy2), 1, BF16, f"ffn_out_dw_{tag}"))
        du = _swiglu_bwd(ug, uu, da, f"swiglu_bwd_{tag}")
        dh2 = r3(_mm_nt(r2(du), wl["ffn_w_in"], F32, f"ffn_in_dx_{tag}"))
        gw["ffn_w_in"][layer] = _mm_tn(r2(h2), r2(du), 4, BF16, f"ffn_in_dw_{tag}")
        start_scatter(layer, 1, sub)
        dxr, dy, d_gate[sub], d_lng[sub], d_lnb[sub], d_scale[sub + 1], d_shift[sub + 1] = _ln_mod_bwd(
            alpha, dh2, dxr, scale[sub + 1], xs, y, gate[sub], lng[sub], lnb[sub], f"ln_bwd_{tag}a")
        if layer % 2 == 0:
            h, proj, qn, kvn, qh, kh, vh, o, lse = mix
            do = r3(_mm_nt(r2(dy), wl["mla_w_o"], BF16, f"mla_o_dx_{tag}"))
            gw["mla_w_o"][j] = rows4(_mm_tn(r2(o), r2(dy), 1, BF16, f"mla_o_dw_{tag}"))
            dqh, dkh, dvh = _attn_bwd(qh, kh, vh, o, do, lse, f"attn_bwd_{tag}")
            dq_, dkv_, dkr = _mla_prep_bwd(dqh, dkh, dvh, cos, sin, f"mla_prep_bwd_{tag}")
            dqn = r3(_mm_nt(r2(dq_), wl["mla_w_qb"], F32, f"mla_qb_dx_{tag}"))
            gw["mla_w_qb"][j] = _mm_tn(r2(qn), r2(dq_), 4, BF16, f"mla_qb_dw_{tag}")
            dkvn = r3(_mm_nt(r2(dkv_), wl["mla_w_kvb"], F32, f"mla_kvb_dx_{tag}"))
            gw["mla_w_kvb"][j] = _mm_tn(r2(kvn), r2(dkv_), 4, BF16, f"mla_kvb_dw_{tag}")
            dproj, dgq_, dgkv_ = _mla_mid_bwd(proj, dqn, dkvn, dkr, gq[j], gkv[j], f"mla_mid_bwd_{tag}")
            d_gq[j], d_gkv[j] = dgq_.sum(0), dgkv_.sum(0)
            dh = r3(_mm_nt(r2(dproj), wl["mla_w_in"], F32, f"mla_in_dx_{tag}"))
            gwin = _mm_tn(r2(h), r2(dproj), 1, BF16, f"mla_in_dw_{tag}")[0]
            gw["mla_w_in"][j] = jnp.transpose(gwin.reshape(gwin.shape[0], 4, gwin.shape[1] // 4), (1, 0, 2))
        else:
            h, proj, og, o_pre, states = mix
            dog = r3(_mm_nt(r2(dy), wl["hgrn_w_o"], F32, f"hgrn_o_dx_{tag}"))
            gw["hgrn_w_o"][j] = rows4(_mm_tn(r2(og), r2(dy), 1, BF16, f"hgrn_o_dw_{tag}"))
            dq_, df_, di_, dg_, dlb_, dgn_ = _hgrn_bwd(proj, lower_bounds[j][None, :], gn[j], o_pre, states, dog, f"hgrn_bwd_{tag}")
            dproj = jnp.concatenate([dq_, df_, di_, dg_], axis=-1)
            d_lbnd[j] = dlb_.sum(0).reshape(1, HK)
            d_gn[j] = dgn_.sum((0, 1))
            dh = r3(_mm_nt(r2(dproj), wl["hgrn_w_in"], F32, f"hgrn_in_dx_{tag}"))
            gw["hgrn_w_in"][j] = _mm_tn(r2(h), r2(dproj), 4, BF16, f"hgrn_in_dw_{tag}")
        start_scatter(layer, 0, sub - 1 if layer else None)
    grad_x, d_scale[0], d_shift[0] = _mod_bwd(dh, dxr, x, scale[0], "mod_bwd_l0a")

    for layer, part, ssem, rsem in scatters:
        kinds = group_kinds(layer, part)
        slabs_t, lands_t = _scatter_wait(
            ssem, rsem, [gw[k][i] for k, i in kinds], [land[k] for k, _ in kinds], [i for _, i in kinds], grad_x,
            f"scatter_wait_l{layer}p{part}")
        for (k, i), s_t, l_t in zip(kinds, slabs_t, lands_t):
            gw[k][i], land[k] = s_t, l_t
    sums = [_sum4(jnp.stack([lax.dynamic_index_in_dim(g, k_me, 0, keepdims=False) for g in gw[k]]), land[k], f"sum4_{k}")
            for k in names]
    others = _swap_sibling(sums, "swap_sums")
    g_main = {k: (a_, b_) for k, a_, b_ in zip(names, sums, others)}

    dmod = jnp.stack([jnp.concatenate([d_shift[s_][:, 0], d_scale[s_][:, 0], d_gate[s_][:, 0]], axis=-1) for s_ in range(n_sub)])
    dmod_rows = _pad_rows(dmod.reshape(n_sub * B, 3 * D), -(-n_sub * B // 8) * 8)
    dmod_all = _allgather8(dmod_rows, "gather_dmod")[:, : n_sub * B].reshape(8, n_sub, B, 3 * D)
    dmod_all = jnp.transpose(dmod_all, (1, 0, 2, 3)).reshape(n_sub, Bg, 3 * D)
    dmod_mine = lax.dynamic_slice_in_dim(dmod_all, k_me * n_ada, n_ada, axis=2)
    g_ada_w, g_ada_b = _ada_bwd(c_all, dmod_mine, "ada_bwd")
    g_ada_w = g_ada_w.reshape(ada_w.shape)
    g_ada_b = g_ada_b.reshape(ada_b.shape)

    small = [jnp.stack(d_gq).reshape(-1), jnp.stack(d_gkv).reshape(-1), jnp.stack(d_gn).reshape(-1),
             jnp.stack(d_lbnd).reshape(-1), jnp.stack([d.sum(0) for d in d_lng]).reshape(-1),
             jnp.stack([d.sum(0) for d in d_lnb]).reshape(-1)]
    sizes = [s_.shape[0] for s_ in small]
    flat = jnp.concatenate(small)
    rows_small = -(-flat.shape[0] // (8 * LANES)) * 8
    flat = jnp.pad(flat, (0, rows_small * LANES - flat.shape[0])).reshape(rows_small, LANES)
    tot = _allgather8(flat, "gather_small")
    acc = tot[0]
    for d in range(1, 8):
        acc = acc + tot[d]
    acc = acc.reshape(-1)
    offs = [0]
    for s_ in sizes:
        offs.append(offs[-1] + s_)
    g_q_norm = acc[offs[0] : offs[1]].reshape(mla_q_norm.shape)
    g_kv_norm = acc[offs[1] : offs[2]].reshape(mla_kv_norm.shape)
    g_g_norm = acc[offs[2] : offs[3]].reshape(hgrn_g_norm.shape)
    g_lbnd = acc[offs[3] : offs[4]].reshape(n_hgrn, HK)
    g_lb_full = lower_bounds_vjp(g_lbnd)[0]
    g_hgrn_lb = lax.dynamic_slice_in_dim(g_lb_full, k_me * lbw, lbw, axis=1)
    g_lng = lax.dynamic_slice_in_dim(acc[offs[4] : offs[5]].reshape(n_sub, D), k_me * dq, dq, axis=1).reshape(ln_g.shape)
    g_lnb = lax.dynamic_slice_in_dim(acc[offs[5] : offs[6]].reshape(n_sub, D), k_me * dq, dq, axis=1).reshape(ln_b.shape)

    weights = dict(mla_w_in=mla_w_in, mla_q_norm=mla_q_norm, mla_w_qb=mla_w_qb, mla_kv_norm=mla_kv_norm, mla_w_kvb=mla_w_kvb,
                   mla_w_o=mla_w_o, hgrn_lb=hgrn_lb, hgrn_w_in=hgrn_w_in, hgrn_g_norm=hgrn_g_norm, hgrn_w_o=hgrn_w_o,
                   ffn_w_in=ffn_w_in, ffn_w_out=ffn_w_out, ada_w=ada_w, ada_b=ada_b, ln_g=ln_g, ln_b=ln_b)
    moms = dict(mla_w_in=(m_mla_w_in, v_mla_w_in), mla_q_norm=(m_mla_q_norm, v_mla_q_norm), mla_w_qb=(m_mla_w_qb, v_mla_w_qb),
                mla_kv_norm=(m_mla_kv_norm, v_mla_kv_norm), mla_w_kvb=(m_mla_w_kvb, v_mla_w_kvb), mla_w_o=(m_mla_w_o, v_mla_w_o),
                hgrn_lb=(m_hgrn_lb, v_hgrn_lb), hgrn_w_in=(m_hgrn_w_in, v_hgrn_w_in), hgrn_g_norm=(m_hgrn_g_norm, v_hgrn_g_norm),
                hgrn_w_o=(m_hgrn_w_o, v_hgrn_w_o), ffn_w_in=(m_ffn_w_in, v_ffn_w_in), ffn_w_out=(m_ffn_w_out, v_ffn_w_out),
                ada_w=(m_ada_w, v_ada_w), ada_b=(m_ada_b, v_ada_b), ln_g=(m_ln_g, v_ln_g), ln_b=(m_ln_b, v_ln_b))
    grads = dict(mla_q_norm=(g_q_norm,), mla_kv_norm=(g_kv_norm,), hgrn_lb=(g_hgrn_lb,), hgrn_g_norm=(g_g_norm,),
                 ada_w=(g_ada_w,), ada_b=(g_ada_b,), ln_g=(g_lng,), ln_b=(g_lnb,), **g_main)
    res = {k: _adamw(weights[k], [g_.reshape(weights[k].shape) for g_ in grads[k]], moms[k][0], moms[k][1], f"adamw_{k}")
           for k in weights}
    order = list(weights)
    return (loss, grad_x, *[res[k][0] for k in order], *[res[k][1] for k in order], *[res[k][2] for k in order],
            *[res[k][3] for k in order])
```

```python
import functools

import jax
import jax.numpy as jnp
from jax import lax
from jax.experimental import pallas as pl
from jax.experimental.pallas import tpu as pltpu

F32 = jnp.float32
BF16 = jnp.bfloat16
SDS = jax.ShapeDtypeStruct
MESH = pl.DeviceIdType.MESH
HI = lax.Precision.HIGHEST
MID = lax.Precision.HIGH

MLA_HEADS, QK_NOPE, QK_ROPE, V_HEAD = 16, 64, 32, 64
Q_LORA, KV_LORA = 768, 256
QK_DIM = QK_NOPE + QK_ROPE
ROPE_THETA = 10000.0
HGRN_K = 128
HGRN_CHUNK = 64
HGRN_SUB = 32
HGRN_PAR = 2
LN_EPS, RMS_EPS = 1e-5, 1e-6
ADAM_LR, ADAM_B1, ADAM_B2, ADAM_EPS, ADAM_WD, ADAM_STEP = 0.001, 0.9, 0.999, 1e-08, 0.01, 10
NEG = -1e30

VMEM_LIMIT_BYTES = 56 * 1024 * 1024
LANES = 128
SUBLANES = 8


def _cparams(*sem):
    return pltpu.CompilerParams(dimension_semantics=sem if sem else None, vmem_limit_bytes=VMEM_LIMIT_BYTES)


def _pick_tile(n, cap):
    best = 0
    for t in range(LANES, min(n, cap) + 1, LANES):
        if n % t == 0:
            best = t
    return best if best else n


def _bdot(a, b):
    return jnp.dot(a.astype(BF16), b.astype(BF16), preferred_element_type=F32)


def _bdot_nt(a, b):
    return lax.dot_general(a.astype(BF16), b.astype(BF16), (((1,), (1,)), ((), ())), preferred_element_type=F32)


def _bdot_tn(a, b):
    return lax.dot_general(a.astype(BF16), b.astype(BF16), (((0,), (0,)), ((), ())), preferred_element_type=F32)


def _hdot(a, b):
    return jnp.dot(a, b, precision=HI, preferred_element_type=F32)


def _mdot(a, b):
    return jnp.dot(a, b, precision=MID, preferred_element_type=F32)


def _mdot_nt(a, b):
    return lax.dot_general(a, b, (((1,), (1,)), ((), ())), precision=MID, preferred_element_type=F32)


def _mdot_tn(a, b):
    return lax.dot_general(a, b, (((0,), (0,)), ((), ())), precision=MID, preferred_element_type=F32)


def _mm_nn(a, w, out_dtype, name):
    M, K = a.shape
    G, _, n = w.shape
    tm = min(512, M)
    tn = _pick_tile(n, 1536)
    nps = n // tn

    def body(a_ref, w_ref, o_ref):
        o_ref[...] = _bdot(a_ref[...], w_ref[...]).astype(o_ref.dtype)

    return pl.pallas_call(
        body,
        grid=(G * nps, M // tm),
        in_specs=[
            pl.BlockSpec((tm, K), lambda j, i: (i, 0)),
            pl.BlockSpec((None, K, tn), lambda j, i: (j // nps, 0, j % nps)),
        ],
        out_specs=pl.BlockSpec((tm, tn), lambda j, i: (i, j)),
        out_shape=SDS((M, G * n), out_dtype),
        name=name,
        compiler_params=_cparams("parallel", "parallel"),
    )(a, w)


def _mm_nt(a, w, out_dtype, name):
    M = a.shape[0]
    G, K, n = w.shape
    tm = min(512, M)
    tk = _pick_tile(K, 1536)

    def body(a_ref, w_ref, o_ref, acc_ref):
        s = pl.program_id(2)

        @pl.when(s == 0)
        def _():
            acc_ref[...] = jnp.zeros_like(acc_ref)

        acc_ref[...] += _bdot_nt(a_ref[...], w_ref[...])

        @pl.when(s == G - 1)
        def _():
            o_ref[...] = acc_ref[...].astype(o_ref.dtype)

    return pl.pallas_call(
        body,
        grid=(K // tk, M // tm, G),
        in_specs=[
            pl.BlockSpec((tm, n), lambda kb, i, s: (i, s)),
            pl.BlockSpec((None, tk, n), lambda kb, i, s: (s, kb, 0)),
        ],
        out_specs=pl.BlockSpec((tm, tk), lambda kb, i, s: (i, kb)),
        out_shape=SDS((M, K), out_dtype),
        scratch_shapes=[pltpu.VMEM((tm, tk), F32)],
        name=name,
        compiler_params=_cparams("parallel", "parallel", "arbitrary"),
    )(a, w)


def _mm_tn(a, d, G, out_dtype, name):
    T, K = a.shape
    n = d.shape[1] // G
    tk = _pick_tile(K, 256)
    tn = _pick_tile(n, 1536)
    nps = n // tn

    def body(a_ref, d_ref, o_ref):
        o_ref[...] = _bdot_tn(a_ref[...], d_ref[...]).astype(o_ref.dtype)

    return pl.pallas_call(
        body,
        grid=(G * nps, K // tk),
        in_specs=[
            pl.BlockSpec((T, tk), lambda j, i: (0, i)),
            pl.BlockSpec((T, tn), lambda j, i: (0, j)),
        ],
        out_specs=pl.BlockSpec((None, tk, tn), lambda j, i: (j // nps, i, j % nps)),
        out_shape=SDS((G, K, n), out_dtype),
        name=name,
        compiler_params=_cparams("parallel", "parallel"),
    )(a, d)


def _rows_call(body, name, B, S, ins, outs, ts=256):
    ts = min(ts, S)
    in_specs, args = [], []
    for arr, kind in ins:
        W = arr.shape[-1]
        if kind == "row":
            in_specs.append(pl.BlockSpec((None, ts, W), lambda b, s: (b, s, 0)))
        elif kind == "ex":
            in_specs.append(pl.BlockSpec((None, 1, W), lambda b, s: (b, 0, 0)))
        else:
            in_specs.append(pl.BlockSpec((1, W), lambda b, s: (0, 0)))
        args.append(arr)
    out_specs, out_shape = [], []
    for W, dt, kind in outs:
        if kind == "row":
            out_specs.append(pl.BlockSpec((None, ts, W), lambda b, s: (b, s, 0)))
            out_shape.append(SDS((B, S, W), dt))
        else:
            out_specs.append(pl.BlockSpec((None, 1, W), lambda b, s: (b, 0, 0)))
            out_shape.append(SDS((B, 1, W), dt))
    return pl.pallas_call(
        body,
        grid=(B, S // ts),
        in_specs=in_specs,
        out_specs=out_specs,
        out_shape=out_shape,
        name=name,
        compiler_params=_cparams("parallel", "arbitrary"),
    )(*args)


def _acc(ref, val):
    @pl.when(pl.program_id(1) == 0)
    def _():
        ref[...] = jnp.zeros_like(ref)

    ref[...] += val


def _mod_fn(x, sc, sh):
    return x * (1.0 + sc) + sh


def _ln_fn(alpha, x, y, gate, g, b):
    z = alpha * x + (1.0 + gate) * y
    mu = jnp.mean(z, -1, keepdims=True)
    var = jnp.mean(jnp.square(z - mu), -1, keepdims=True)
    return (z - mu) * lax.rsqrt(var + LN_EPS) * g + b


def _modulate(x, sc, sh, name):
    B, S, D = x.shape

    def body(x_ref, sc_ref, sh_ref, h_ref):
        h_ref[...] = _mod_fn(x_ref[...], sc_ref[...], sh_ref[...]).astype(BF16)

    return _rows_call(body, name, B, S, [(x, "row"), (sc, "ex"), (sh, "ex")], [(D, BF16, "row")])[0]


def _ln_fwd(alpha, x, y, gate, g, b, name):
    B, S, D = x.shape

    def body(x_ref, y_ref, gate_ref, g_ref, b_ref, o_ref):
        o_ref[...] = _ln_fn(alpha, x_ref[...], y_ref[...], gate_ref[...], g_ref[...], b_ref[...])

    return _rows_call(
        body, name, B, S, [(x, "row"), (y, "row"), (gate, "ex"), (g, "par"), (b, "par")], [(D, F32, "row")]
    )[0]


def _ln_mod_fwd(alpha, x, y, gate, g, b, sc_next, sh_next, name):
    B, S, D = x.shape

    def body(x_ref, y_ref, gate_ref, g_ref, b_ref, sc_ref, sh_ref, o_ref, h_ref):
        out = _ln_fn(alpha, x_ref[...], y_ref[...], gate_ref[...], g_ref[...], b_ref[...])
        o_ref[...] = out
        h_ref[...] = _mod_fn(out, sc_ref[...], sh_ref[...]).astype(BF16)

    return _rows_call(
        body, name, B, S,
        [(x, "row"), (y, "row"), (gate, "ex"), (g, "par"), (b, "par"), (sc_next, "ex"), (sh_next, "ex")],
        [(D, F32, "row"), (D, BF16, "row")],
    )


def _ln_mod_bwd(alpha, dh, dxr_next, sc_next, x, y, gate, g, b, name):
    B, S, D = x.shape

    def body(dh_ref, dxr_ref, sc_ref, x_ref, y_ref, gate_ref, g_ref, b_ref,
             dx_ref, dy_ref, dgate_ref, dg_ref, db_ref, dsc_ref, dsh_ref):
        out, vjp = jax.vjp(
            functools.partial(_ln_fn, alpha), x_ref[...], y_ref[...], gate_ref[...], g_ref[...], b_ref[...]
        )
        dh_v = dh_ref[...]
        dx, dy, dgate, dg, db = vjp(dxr_ref[...] + dh_v * (1.0 + sc_ref[...]))
        dx_ref[...] = dx
        dy_ref[...] = dy.astype(BF16)
        _acc(dgate_ref, dgate)
        _acc(dg_ref, dg)
        _acc(db_ref, db)
        _acc(dsc_ref, jnp.sum(dh_v * out, axis=0, keepdims=True))
        _acc(dsh_ref, jnp.sum(dh_v, axis=0, keepdims=True))

    return _rows_call(
        body, name, B, S,
        [(dh, "row"), (dxr_next, "row"), (sc_next, "ex"), (x, "row"), (y, "row"), (gate, "ex"), (g, "par"), (b, "par")],
        [(D, F32, "row"), (D, BF16, "row")] + [(D, F32, "acc")] * 5,
    )


def _ln_bwd(alpha, dout, x, y, gate, g, b, name):
    B, S, D = x.shape

    def body(do_ref, x_ref, y_ref, gate_ref, g_ref, b_ref, dxr_ref, dy_ref, dgate_ref, dg_ref, db_ref):
        _, vjp = jax.vjp(
            functools.partial(_ln_fn, alpha), x_ref[...], y_ref[...], gate_ref[...], g_ref[...], b_ref[...]
        )
        dx, dy, dgate, dg, db = vjp(do_ref[...])
        dxr_ref[...] = dx
        dy_ref[...] = dy.astype(BF16)
        _acc(dgate_ref, dgate)
        _acc(dg_ref, dg)
        _acc(db_ref, db)

    return _rows_call(
        body,
        name,
        B,
        S,
        [(dout, "row"), (x, "row"), (y, "row"), (gate, "ex"), (g, "par"), (b, "par")],
        [(D, F32, "row"), (D, BF16, "row"), (D, F32, "acc"), (D, F32, "acc"), (D, F32, "acc")],
    )


def _mod_bwd(dh, dxr, x, sc, name):
    B, S, D = x.shape

    def body(dh_ref, dxr_ref, x_ref, sc_ref, dx_ref, dsc_ref, dsh_ref):
        dh_v = dh_ref[...]
        dx_ref[...] = dxr_ref[...] + dh_v * (1.0 + sc_ref[...])
        _acc(dsc_ref, jnp.sum(dh_v * x_ref[...], axis=0, keepdims=True))
        _acc(dsh_ref, jnp.sum(dh_v, axis=0, keepdims=True))

    return _rows_call(
        body,
        name,
        B,
        S,
        [(dh, "row"), (dxr, "row"), (x, "row"), (sc, "ex")],
        [(D, F32, "row"), (D, F32, "acc"), (D, F32, "acc")],
    )


def _loss_head(y, target, name):
    B, S, D = y.shape

    def body(y_ref, t_ref, l_ref, dy_ref):
        e = y_ref[...] - t_ref[...]
        dy_ref[...] = e * (1.0 / D)
        part = 0.5 * jnp.sum(jnp.sum(e * e, axis=1, keepdims=True) * (1.0 / D), axis=0, keepdims=True)
        _acc(l_ref, jnp.broadcast_to(part, (1, LANES)))

    loss, dy = _rows_call(
        body, name, B, S, [(y, "row"), (target, "row")], [(LANES, F32, "acc"), (D, F32, "row")]
    )
    return jnp.sum(loss[:, 0, 0]), dy


def _ffn_in(h, w, name):
    M, K = h.shape
    G, _, n = w.shape
    assert G == 4
    tm = min(512, M)
    tn = _pick_tile(n, 1536)
    nps = n // tn
    half = 2 * nps

    def body(h_ref, wg_ref, wu_ref, a_ref, g_ref, u_ref):
        hv = h_ref[...]
        g = _bdot(hv, wg_ref[...])
        u = _bdot(hv, wu_ref[...])
        a_ref[...] = (jax.nn.silu(g) * u).astype(BF16)
        g_ref[...] = g.astype(BF16)
        u_ref[...] = u.astype(BF16)

    out = pl.BlockSpec((tm, tn), lambda j, i: (i, j))
    return pl.pallas_call(
        body,
        grid=(half, M // tm),
        in_specs=[
            pl.BlockSpec((tm, K), lambda j, i: (i, 0)),
            pl.BlockSpec((None, K, tn), lambda j, i: (j // nps, 0, j % nps)),
            pl.BlockSpec((None, K, tn), lambda j, i: (2 + j // nps, 0, j % nps)),
        ],
        out_specs=[out, out, out],
        out_shape=[SDS((M, 2 * n), BF16)] * 3,
        name=name,
        compiler_params=_cparams("parallel", "parallel"),
    )(h, w, w)


def _swiglu_bwd(g, u, da, name):
    B, S, F = g.shape

    def body(g_ref, u_ref, da_ref, du_ref):
        _, vjp = jax.vjp(lambda gv, uv: jax.nn.silu(gv) * uv, g_ref[...].astype(F32), u_ref[...].astype(F32))
        dg, du = vjp(da_ref[...])
        du_ref[:, :F] = dg.astype(BF16)
        du_ref[:, F:] = du.astype(BF16)

    return _rows_call(body, name, B, S, [(g, "row"), (u, "row"), (da, "row")], [(2 * F, BF16, "row")])[0]


def _rms_fn(x, g):
    return x * lax.rsqrt(jnp.mean(jnp.square(x), -1, keepdims=True) + RMS_EPS) * g


def _mla_mid_fwd(proj, gq, gkv, name):
    B, S, _ = proj.shape

    def body(p_ref, gq_ref, gkv_ref, qn_ref, kvn_ref):
        p = p_ref[...]
        qn_ref[...] = _rms_fn(p[:, :Q_LORA], gq_ref[...]).astype(BF16)
        kvn_ref[...] = _rms_fn(p[:, Q_LORA : Q_LORA + KV_LORA], gkv_ref[...]).astype(BF16)

    return _rows_call(
        body, name, B, S, [(proj, "row"), (gq, "par"), (gkv, "par")], [(Q_LORA, BF16, "row"), (KV_LORA, BF16, "row")]
    )


def _mla_mid_bwd(proj, dqn, dkvn, dkr, gq, gkv, name):
    B, S, W = proj.shape

    def body(p_ref, dqn_ref, dkvn_ref, dkr_ref, gq_ref, gkv_ref, dp_ref, dgq_ref, dgkv_ref):
        p = p_ref[...]
        _, vq = jax.vjp(_rms_fn, p[:, :Q_LORA], gq_ref[...])
        dql, dgq = vq(dqn_ref[...])
        _, vkv = jax.vjp(_rms_fn, p[:, Q_LORA : Q_LORA + KV_LORA], gkv_ref[...])
        dkvl, dgkv = vkv(dkvn_ref[...])
        dp_ref[:, :Q_LORA] = dql.astype(BF16)
        dp_ref[:, Q_LORA : Q_LORA + KV_LORA] = dkvl.astype(BF16)
        dp_ref[:, Q_LORA + KV_LORA :] = dkr_ref[...].astype(BF16)
        _acc(dgq_ref, dgq)
        _acc(dgkv_ref, dgkv)

    return _rows_call(
        body,
        name,
        B,
        S,
        [(proj, "row"), (dqn, "row"), (dkvn, "row"), (dkr, "row"), (gq, "par"), (gkv, "par")],
        [(W, BF16, "row"), (Q_LORA, F32, "acc"), (KV_LORA, F32, "acc")],
    )


def _rope(x, cos, sin):
    h = QK_ROPE // 2
    x1, x2 = x[:, :h], x[:, h:]
    return jnp.concatenate([x1 * cos - x2 * sin, x1 * sin + x2 * cos], axis=1)


def _rope_t(dy, cos, sin):
    h = QK_ROPE // 2
    d1, d2 = dy[:, :h], dy[:, h:]
    return jnp.concatenate([d1 * cos + d2 * sin, d2 * cos - d1 * sin], axis=1)


def _heads_call(body, name, B, S, ins, outs, ts=256):
    ts = min(ts, S)
    in_specs, args = [], []
    for arr, kind in ins:
        if kind == "row":
            in_specs.append(pl.BlockSpec((None, ts, arr.shape[-1]), lambda b, s: (b, s, 0)))
        else:
            in_specs.append(pl.BlockSpec((arr.shape[0], None, ts, arr.shape[-1]), lambda b, s: (0, b, s, 0)))
        args.append(arr)
    out_specs, out_shape = [], []
    for shape, dt, kind in outs:
        if kind == "row":
            out_specs.append(pl.BlockSpec((None, ts, shape[-1]), lambda b, s: (b, s, 0)))
        else:
            out_specs.append(pl.BlockSpec((shape[0], None, ts, shape[-1]), lambda b, s: (0, b, s, 0)))
        out_shape.append(SDS(shape, dt))
    return pl.pallas_call(
        body,
        grid=(B, S // ts),
        in_specs=in_specs,
        out_specs=out_specs,
        out_shape=out_shape,
        name=name,
        compiler_params=_cparams("parallel", "parallel"),
    )(*args)


def _mla_prep_fwd(q, kv, proj, cos, sin, name):
    B, S, _ = q.shape
    H = MLA_HEADS

    def body(q_ref, kv_ref, p_ref, cos_ref, sin_ref, qh_ref, kh_ref, vh_ref):
        cos_v, sin_v = cos_ref[...], sin_ref[...]
        kr = _rope(p_ref[:, Q_LORA + KV_LORA :], cos_v, sin_v).astype(BF16)
        for h in range(H):
            qn = q_ref[:, h * QK_DIM : h * QK_DIM + QK_NOPE]
            qr = _rope(q_ref[:, h * QK_DIM + QK_NOPE : (h + 1) * QK_DIM], cos_v, sin_v)
            qh_ref[h] = jnp.concatenate([qn, qr], axis=1).astype(BF16)
            kn = kv_ref[:, h * 128 : h * 128 + QK_NOPE].astype(BF16)
            kh_ref[h] = jnp.concatenate([kn, kr], axis=1)
            vh_ref[h] = kv_ref[:, h * 128 + QK_NOPE : (h + 1) * 128].astype(BF16)

    return _heads_call(
        body,
        name,
        B,
        S,
        [(q, "row"), (kv, "row"), (proj, "row"), (cos, "row"), (sin, "row")],
        [((H, B, S, QK_DIM), BF16, "heads"), ((H, B, S, QK_DIM), BF16, "heads"), ((H, B, S, V_HEAD), BF16, "heads")],
    )


def _mla_prep_bwd(dqh, dkh, dvh, cos, sin, name):
    H, B, S, _ = dqh.shape

    def body(dqh_ref, dkh_ref, dvh_ref, cos_ref, sin_ref, dq_ref, dkv_ref, dkr_ref):
        cos_v, sin_v = cos_ref[...], sin_ref[...]
        dkr = jnp.zeros((cos_v.shape[0], QK_ROPE), F32)
        for h in range(H):
            dqv = dqh_ref[h].astype(F32)
            dq_ref[:, h * QK_DIM : h * QK_DIM + QK_NOPE] = dqv[:, :QK_NOPE].astype(BF16)
            dq_ref[:, h * QK_DIM + QK_NOPE : (h + 1) * QK_DIM] = _rope_t(dqv[:, QK_NOPE:], cos_v, sin_v).astype(BF16)
            dkv = dkh_ref[h].astype(F32)
            dkv_ref[:, h * 128 : h * 128 + QK_NOPE] = dkv[:, :QK_NOPE].astype(BF16)
            dkv_ref[:, h * 128 + QK_NOPE : (h + 1) * 128] = dvh_ref[h]
            dkr = dkr + dkv[:, QK_NOPE:]
        dkr_ref[...] = _rope_t(dkr, cos_v, sin_v)

    return _heads_call(
        body,
        name,
        B,
        S,
        [(dqh, "heads"), (dkh, "heads"), (dvh, "heads"), (cos, "row"), (sin, "row")],
        [((B, S, H * QK_DIM), BF16, "row"), ((B, S, H * 128), BF16, "row"), ((B, S, QK_ROPE), F32, "row")],
    )


LOG2E = 1.4426950408889634
ATTN_TILE = 1024


def _tril_mask(t):
    return lax.broadcasted_iota(jnp.int32, (t, t), 0) >= lax.broadcasted_iota(jnp.int32, (t, t), 1)


def _attn_fwd(qh, kh, vh, name):
    H, B, S, _ = qh.shape
    t = min(ATTN_TILE, S)
    scale = QK_DIM**-0.5
    c2 = scale * LOG2E

    def body(q_ref, k_ref, v_ref, o_ref, lse_ref):
        i = pl.program_id(2)
        qs = [q_ref[0], q_ref[1]]

        def step(j, carry, diagonal):
            rows = pl.ds(pl.multiple_of(j * t, t), t)
            out = []
            for hh in range(2):
                m, l, acc = carry[hh]
                s = _bdot_nt(qs[hh], k_ref[hh, rows, :])
                if diagonal:
                    s = jnp.where(_tril_mask(t), s, NEG)
                m_new = jnp.maximum(m, jnp.max(s, axis=1, keepdims=True))
                p = jnp.exp2((s - m_new) * c2)
                a = jnp.exp2((m - m_new) * c2)
                l = a * l + jnp.sum(p, axis=1, keepdims=True)
                acc = a * acc + _bdot(p, v_ref[hh, rows, :])
                out.append((m_new, l, acc))
            return tuple(out)

        one = (jnp.full((t, 1), NEG, F32), jnp.zeros((t, 1), F32), jnp.zeros((t, V_HEAD), F32))
        carry = lax.fori_loop(0, i, lambda j, cy: step(j, cy, False), (one, one))
        carry = step(i, carry, True)
        outs = []
        for hh in range(2):
            m, l, acc = carry[hh]
            outs.append(acc / l)
            lse_ref[hh] = m * scale + jnp.log(l)
        o_ref[...] = jnp.concatenate(outs, axis=1).astype(BF16)

    return pl.pallas_call(
        body,
        grid=(B, H // 2, S // t),
        in_specs=[
            pl.BlockSpec((2, None, t, QK_DIM), lambda b, p, i: (p, b, i, 0)),
            pl.BlockSpec((2, None, S, QK_DIM), lambda b, p, i: (p, b, 0, 0)),
            pl.BlockSpec((2, None, S, V_HEAD), lambda b, p, i: (p, b, 0, 0)),
        ],
        out_specs=[
            pl.BlockSpec((None, t, 2 * V_HEAD), lambda b, p, i: (b, i, p)),
            pl.BlockSpec((2, None, t, 1), lambda b, p, i: (p, b, i, 0)),
        ],
        out_shape=[SDS((B, S, H * V_HEAD), BF16), SDS((H, B, S, 1), F32)],
        name=name,
        compiler_params=_cparams("parallel", "parallel", "arbitrary"),
    )(qh, kh, vh)


def _attn_bwd(qh, kh, vh, o, do, lse, name):
    H, B, S, _ = qh.shape
    t = min(ATTN_TILE, S)
    nq = S // t
    scale = QK_DIM**-0.5
    c2 = scale * LOG2E

    def body(q_ref, k_ref, v_ref, o_ref, do_ref, lse_ref, dq_ref, dk_ref, dv_ref, dq_acc, delta_ref, lse2_ref):
        prod = o_ref[...].astype(F32) * do_ref[...].astype(F32)
        for hh in range(2):
            delta_ref[hh] = jnp.sum(prod[:, hh * V_HEAD : (hh + 1) * V_HEAD], axis=1, keepdims=True)
            lse2_ref[hh] = lse_ref[hh] * LOG2E
        dq_acc[...] = jnp.zeros_like(dq_acc)

        def kloop(j, _):
            krows = pl.ds(pl.multiple_of(j * t, t), t)
            ks = [k_ref[0, krows, :], k_ref[1, krows, :]]
            vs = [v_ref[0, krows, :], v_ref[1, krows, :]]

            def qstep(i, carry, diagonal):
                qrows = pl.ds(pl.multiple_of(i * t, t), t)
                do_i = do_ref[qrows, :]
                out = []
                for hh in range(2):
                    dk, dv = carry[hh]
                    q = q_ref[hh, qrows, :]
                    do_h = do_i[:, hh * V_HEAD : (hh + 1) * V_HEAD]
                    s = _bdot_nt(q, ks[hh])
                    p = jnp.exp2(s * c2 - lse2_ref[hh, qrows, :])
                    if diagonal:
                        p = jnp.where(_tril_mask(t), p, 0.0)
                    dv = dv + _bdot_tn(p, do_h)
                    dp = _bdot_nt(do_h, vs[hh])
                    ds = (p * (dp - delta_ref[hh, qrows, :])).astype(BF16)
                    dk = dk + _bdot_tn(ds, q)
                    dq_acc[hh, qrows, :] += _bdot(ds, ks[hh])
                    out.append((dk, dv))
                return tuple(out)

            one = (jnp.zeros((t, QK_DIM), F32), jnp.zeros((t, V_HEAD), F32))
            carry = qstep(j, (one, one), True)
            carry = lax.fori_loop(j + 1, nq, lambda i, cy: qstep(i, cy, False), carry)
            for hh in range(2):
                dk_ref[hh, krows, :] = (carry[hh][0] * scale).astype(BF16)
                dv_ref[hh, krows, :] = carry[hh][1].astype(BF16)
            return 0

        lax.fori_loop(0, nq, kloop, 0)
        dq_ref[...] = (dq_acc[...] * scale).astype(BF16)

    hspec = lambda w: pl.BlockSpec((2, None, S, w), lambda b, p: (p, b, 0, 0))
    ospec = pl.BlockSpec((None, S, 2 * V_HEAD), lambda b, p: (b, 0, p))
    return pl.pallas_call(
        body,
        grid=(B, H // 2),
        in_specs=[hspec(QK_DIM), hspec(QK_DIM), hspec(V_HEAD), ospec, ospec, hspec(1)],
        out_specs=[hspec(QK_DIM), hspec(QK_DIM), hspec(V_HEAD)],
        out_shape=[SDS((H, B, S, QK_DIM), BF16), SDS((H, B, S, QK_DIM), BF16), SDS((H, B, S, V_HEAD), BF16)],
        scratch_shapes=[pltpu.VMEM((2, S, QK_DIM), F32), pltpu.VMEM((2, S, 1), F32), pltpu.VMEM((2, S, 1), F32)],
        name=name,
        compiler_params=_cparams("parallel", "parallel"),
    )(qh, kh, vh, o, do, lse)


def _hgrn_pre(q, fx, lb):
    sig = jax.nn.sigmoid(fx)
    f = lb + (1.0 - lb) * sig
    return jax.nn.silu(q), 1.0 - f, jnp.log(f)


def _hgrn_gate(o, gg, gn):
    return _rms_fn(o, gn) * jax.nn.silu(gg)


def _tri(n, lower):
    r = lax.broadcasted_iota(jnp.int32, (n, n), 0)
    c = lax.broadcasted_iota(jnp.int32, (n, n), 1)
    return ((r >= c) if lower else (r <= c)).astype(F32)


def _hgrn_intra_fwd(qs, k, v, b):
    C, SB = qs.shape[0], min(HGRN_SUB, qs.shape[0])
    ridx = lax.broadcasted_iota(jnp.int32, (SUBLANES, 1), 0)
    outs = []
    for i in range(C // SB):
        r0 = i * SB
        qi, ki, vi, bi = qs[r0 : r0 + SB], k[r0 : r0 + SB], v[r0 : r0 + SB], b[r0 : r0 + SB]
        ng = SB // SUBLANES
        qg = [qi[g * SUBLANES : (g + 1) * SUBLANES] for g in range(ng)]
        bg = [bi[g * SUBLANES : (g + 1) * SUBLANES] for g in range(ng)]
        accg = [jnp.zeros((SUBLANES, v.shape[1]), F32) for _ in range(ng)]
        for s in range(SB):
            gs, so = divmod(s, SUBLANES)
            k_s, v_s, b_s = ki[s : s + 1], vi[s : s + 1], bi[s : s + 1]
            for tg in range(gs, ng):
                if tg == gs:
                    mask = ridx >= so
                    w = jnp.where(mask, qg[tg] * k_s * jnp.exp(jnp.where(mask, bg[tg] - b_s, 0.0)), 0.0)
                else:
                    w = qg[tg] * k_s * jnp.exp(bg[tg] - b_s)
                accg[tg] = accg[tg] + jnp.sum(w, axis=1, keepdims=True) * v_s
        acc = jnp.concatenate(accg, axis=0)
        if i > 0:
            ref = bi[0:1]
            qt = qi * jnp.exp(bi - ref)
            kt = k[:r0] * jnp.exp(ref - b[:r0])
            acc = acc + _bdot(_mdot_nt(qt, kt), v[:r0])
        outs.append(acc)
    return jnp.concatenate(outs, axis=0)


def _hgrn_intra_bwd(qs, k, v, b, do):
    C, SB = qs.shape[0], min(HGRN_SUB, qs.shape[0])
    nb = C // SB
    ridx = lax.broadcasted_iota(jnp.int32, (SUBLANES, 1), 0)
    dq_p = [None] * nb
    dk_p = [jnp.zeros((SB, k.shape[1]), F32) for _ in range(nb)]
    dv_p = [jnp.zeros((SB, v.shape[1]), F32) for _ in range(nb)]
    for i in range(nb):
        r0 = i * SB
        qi, ki, vi, bi, doi = qs[r0 : r0 + SB], k[r0 : r0 + SB], v[r0 : r0 + SB], b[r0 : r0 + SB], do[r0 : r0 + SB]
        ng = SB // SUBLANES
        qg = [qi[g * SUBLANES : (g + 1) * SUBLANES] for g in range(ng)]
        bg = [bi[g * SUBLANES : (g + 1) * SUBLANES] for g in range(ng)]
        dog = [doi[g * SUBLANES : (g + 1) * SUBLANES] for g in range(ng)]
        dqg =[jnp.zeros((SUBLANES, k.shape[1]), F32) for _ in range(ng)]
        dkg = [jnp.zeros((SUBLANES, k.shape[1]), F32) for _ in range(ng)]
        dvg = [jnp.zeros((SUBLANES, v.shape[1]), F32) for _ in range(ng)]
        for s in range(SB):
            gs, so = divmod(s, SUBLANES)
            k_s, v_s, b_s = ki[s : s + 1], vi[s : s + 1], bi[s : s + 1]
            dk_s = jnp.zeros((SUBLANES, k.shape[1]), F32)
            dv_s = jnp.zeros((SUBLANES, v.shape[1]), F32)
            for tg in range(gs, ng):
                if tg == gs:
                    mask = ridx >= so
                    e = jnp.where(mask, jnp.exp(jnp.where(mask, bg[tg] - b_s, 0.0)), 0.0)
                else:
                    e = jnp.exp(bg[tg] - b_s)
                da = jnp.sum(dog[tg] * v_s, axis=1, keepdims=True)
                qe = qg[tg] * e
                a = jnp.sum(qe * k_s, axis=1, keepdims=True)
                dqg[tg] = dqg[tg] + da * (k_s * e)
                dk_s = dk_s + da * qe
                dv_s = dv_s + a * dog[tg]
            dkg[gs] = jnp.where(ridx == so, dkg[gs] + jnp.sum(dk_s, axis=0, keepdims=True), dkg[gs])
            dvg[gs] = jnp.where(ridx == so, dvg[gs] + jnp.sum(dv_s, axis=0, keepdims=True), dvg[gs])
        dqi = jnp.concatenate(dqg, axis=0)
        dki = jnp.concatenate(dkg, axis=0)
        dvi = jnp.concatenate(dvg, axis=0)
        if i > 0:
            ref = bi[0:1]
            eq = jnp.exp(bi - ref)
            ek = jnp.exp(ref - b[:r0])
            qt = qi * eq
            kt = k[:r0] * ek
            A = _mdot_nt(qt, kt)
            dA = _bdot_nt(doi, v[:r0])
            dvl = _bdot_tn(A, doi)
            dqi = dqi + _mdot(dA, kt) * eq
            dkl = _mdot_tn(dA, qt) * ek
            for j in range(i):
                dk_p[j] = dk_p[j] + dkl[j * SB : (j + 1) * SB]
                dv_p[j] = dv_p[j] + dvl[j * SB : (j + 1) * SB]
        dq_p[i] = dqi
        dk_p[i] = dk_p[i] + dki
        dv_p[i] = dv_p[i] + dvi
    return jnp.concatenate(dq_p, axis=0), jnp.concatenate(dk_p, axis=0), jnp.concatenate(dv_p, axis=0)


def _hgrn_fwd(proj, lb, gn, name):
    B, S, W = proj.shape
    HK = W // 4
    H = HK // HGRN_K
    C = min(HGRN_CHUNK, S)
    N = S // C

    HP = HGRN_PAR if H % HGRN_PAR == 0 else 1
    WP = HP * HGRN_K

    def body(q_ref, f_ref, i_ref, g_ref, lb_ref, gn_ref, og_ref, o_ref, st_ref):
        gn_v = gn_ref[...]
        tril = _tri(C, True)

        def chunk(n, sts):
            rows = pl.ds(pl.multiple_of(n * C, C), C)
            out = []
            for hh in range(HP):
                ln = slice(hh * HGRN_K, (hh + 1) * HGRN_K)
                st = sts[hh]
                qs, k, g = _hgrn_pre(q_ref[rows, ln], f_ref[rows, ln], lb_ref[:, ln])
                v = i_ref[rows, ln]
                b = _hdot(tril, g)
                st_ref[hh, n] = st
                o = _hgrn_intra_fwd(qs, k, v, b) + _bdot_nt(qs * jnp.exp(b), st)
                bl = b[C - 1 : C]
                out.append(st * jnp.exp(bl) + _bdot_tn(v, k * jnp.exp(bl - b)))
                o_ref[rows, ln] = o
                og_ref[rows, ln] = _hgrn_gate(o, g_ref[rows, ln], gn_v).astype(BF16)
            return tuple(out)

        lax.fori_loop(0, N, chunk, tuple(jnp.zeros((HGRN_K, HGRN_K), F32) for _ in range(HP)))

    col = lambda part: pl.BlockSpec((None, S, WP), lambda b, h: (b, 0, part * (H // HP) + h))
    return pl.pallas_call(
        body,
        grid=(B, H // HP),
        in_specs=[col(0), col(1), col(2), col(3), pl.BlockSpec((1, WP), lambda b, h: (0, h)), pl.BlockSpec((1, HGRN_K), lambda b, h: (0, 0))],
        out_specs=[col(0), col(0), pl.BlockSpec((None, HP, N, HGRN_K, HGRN_K), lambda b, h: (b, h, 0, 0, 0))],
        out_shape=[SDS((B, S, HK), BF16), SDS((B, S, HK), F32), SDS((B, H, N, HGRN_K, HGRN_K), F32)],
        name=name,
        compiler_params=_cparams("parallel", "parallel"),
    )(proj, proj, proj, proj, lb, gn)


def _hgrn_bwd(proj, lb, gn, o_pre, states, dog, name):
    B, S, W = proj.shape
    HK = W // 4
    H = HK // HGRN_K
    C = min(HGRN_CHUNK, S)
    N = S // C

    HP = HGRN_PAR if H % HGRN_PAR == 0 else 1
    WP = HP * HGRN_K

    def body(q_ref, f_ref, i_ref, g_ref, lb_ref, gn_ref, o_ref, st_ref, dog_ref, dq_ref, df_ref, di_ref, dg_ref, dlb_ref, dgn_ref):
        gn_v = gn_ref[...]
        tril = _tri(C, True)
        triu = _tri(C, False)

        def chunk(idx, carry):
            n = N - 1 - idx
            rows = pl.ds(pl.multiple_of(n * C, C), C)
            out = []
            for hh in range(HP):
                ln = slice(hh * HGRN_K, (hh + 1) * HGRN_K)
                dst, dlb, dgn = carry[hh]
                (qs, k, g), pre_vjp = jax.vjp(_hgrn_pre, q_ref[rows, ln], f_ref[rows, ln], lb_ref[:, ln])
                v = i_ref[rows, ln]
                _, gate_vjp = jax.vjp(_hgrn_gate, o_ref[rows, ln], g_ref[rows, ln], gn_v)
                do, dgg, dgn_c = gate_vjp(dog_ref[rows, ln])
                b = _hdot(tril, g)
                st0 = st_ref[hh, n]
                eb = jnp.exp(b)
                bl = b[C - 1 : C]
                ebl = jnp.exp(bl)
                ekb = jnp.exp(bl - b)
                qe = qs * eb
                kt = k * ekb
                dqs, dk, dv = _hgrn_intra_bwd(qs, k, v, b, do)
                dqs = dqs + _bdot(do, st0) * eb
                dk = dk + _bdot(v, dst) * ekb
                dv = dv + _bdot_nt(kt, dst)
                st1 = st0 * ebl + _bdot_tn(v, kt)
                dbl = jnp.sum(st1 * dst, axis=0, keepdims=True)
                dst = dst * ebl + _bdot_tn(do, qe)
                dgl = _hdot(triu, qs * dqs - k * dk) + dbl
                dq_pre, dfx, dlb_c = pre_vjp((dqs, dk, dgl))
                dq_ref[rows, ln] = dq_pre.astype(BF16)
                df_ref[rows, ln] = dfx.astype(BF16)
                di_ref[rows, ln] = dv.astype(BF16)
                dg_ref[rows, ln] = dgg.astype(BF16)
                out.append((dst, dlb + dlb_c, dgn + dgn_c))
            return tuple(out)

        zero = jnp.zeros((1, HGRN_K), F32)
        one = (jnp.zeros((HGRN_K, HGRN_K), F32), zero, zero)
        res = lax.fori_loop(0, N, chunk, tuple(one for _ in range(HP)))
        for hh in range(HP):
            dlb_ref[hh] = res[hh][1]
            dgn_ref[hh] = res[hh][2]

    col = lambda part: pl.BlockSpec((None, S, WP), lambda b, h: (b, 0, part * (H // HP) + h))
    vec = pl.BlockSpec((None, HP, 1, HGRN_K), lambda b, h: (b, h, 0, 0))
    return pl.pallas_call(
        body,
        grid=(B, H // HP),
        in_specs=[
            col(0), col(1), col(2), col(3),
            pl.BlockSpec((1, WP), lambda b, h: (0, h)),
            pl.BlockSpec((1, HGRN_K), lambda b, h: (0, 0)),
            col(0),
            pl.BlockSpec((None, HP, N, HGRN_K, HGRN_K), lambda b, h: (b, h, 0, 0, 0)),
            col(0),
        ],
        out_specs=[col(0), col(0), col(0), col(0), vec, vec],
        out_shape=[SDS((B, S, HK), BF16)] * 4 + [SDS((B, H, 1, HGRN_K), F32)] * 2,
        name=name,
        compiler_params=_cparams("parallel", "parallel"),
    )(proj, proj, proj, proj, lb, gn, o_pre, states, dog)


def _ada_fwd(c_all, w, b, name):
    Bg, D = c_all.shape
    L, _, n = w.shape

    def body(c_ref, w_ref, b_ref, o_ref):
        o_ref[...] = _bdot(jax.nn.silu(c_ref[...]), w_ref[...]) + b_ref[...]

    return pl.pallas_call(
        body,
        grid=(L,),
        in_specs=[
            pl.BlockSpec((Bg, D), lambda l: (0, 0)),
            pl.BlockSpec((None, D, n), lambda l: (l, 0, 0)),
            pl.BlockSpec((None, 1, n), lambda l: (l, 0, 0)),
        ],
        out_specs=pl.BlockSpec((None, Bg, n), lambda l: (l, 0, 0)),
        out_shape=SDS((L, Bg, n), F32),
        name=name,
        compiler_params=_cparams("parallel"),
    )(c_all, w, b)


def _ada_bwd(c_all, dmod, name):
    Bg, D = c_all.shape
    L, _, n = dmod.shape

    def body(c_ref, d_ref, dw_ref, db_ref):
        d = d_ref[...]
        dw_ref[...] = _bdot_tn(jax.nn.silu(c_ref[...]), d)
        db_ref[...] = jnp.sum(d, axis=0, keepdims=True)

    return pl.pallas_call(
        body,
        grid=(L,),
        in_specs=[pl.BlockSpec((Bg, D), lambda l: (0, 0)), pl.BlockSpec((None, Bg, n), lambda l: (l, 0, 0))],
        out_specs=[pl.BlockSpec((None, D, n), lambda l: (l, 0, 0)), pl.BlockSpec((None, 1, n), lambda l: (l, 0, 0))],
        out_shape=[SDS((L, D, n), F32), SDS((L, 1, n), F32)],
        name=name,
        compiler_params=_cparams("parallel"),
    )(c_all, dmod)


def _adamw(w, gs, m, v, name):
    shape = w.shape
    cols = shape[-1]
    rows = w.size // cols
    tr = rows
    for cand in (512, 256, 128, 64, 32, 16, 8):
        if rows % cand == 0 and cand * cols * 4 <= 2 * 1024 * 1024:
            tr = cand
            break
    as2d = lambda a: a.reshape(rows, cols)
    ng = len(gs)
    c1 = 1.0 / (1.0 - ADAM_B1**ADAM_STEP)
    c2 = 1.0 / (1.0 - ADAM_B2**ADAM_STEP)

    def body(*refs):
        w_ref, m_ref, v_ref = refs[0], refs[1], refs[2]
        g_refs = refs[3 : 3 + ng]
        g_out, d_out, m_out, v_out = refs[3 + ng :]
        g = g_refs[0][...].astype(F32)
        for r in g_refs[1:]:
            g = g + r[...].astype(F32)
        m_new = ADAM_B1 * m_ref[...] + (1.0 - ADAM_B1) * g
        v_new = ADAM_B2 * v_ref[...] + (1.0 - ADAM_B2) * jnp.square(g)
        g_out[...] = g
        m_out[...] = m_new
        v_out[...] = v_new
        d_out[...] = -ADAM_LR * ((m_new * c1) / (jnp.sqrt(v_new * c2) + ADAM_EPS) + ADAM_WD * w_ref[...])

    spec = pl.BlockSpec((tr, cols), lambda i: (i, 0))
    outs = pl.pallas_call(
        body,
        grid=(rows // tr,),
        in_specs=[spec] * (3 + ng),
        out_specs=[spec] * 4,
        out_shape=[SDS((rows, cols), F32)] * 4,
        name=name,
        compiler_params=_cparams("parallel"),
    )(as2d(w), as2d(m), as2d(v), *[as2d(g) for g in gs])
    return tuple(o.reshape(shape) for o in outs)


def _sum4(own, recv, name):
    shape = own.shape
    cols = shape[-1]
    rows = own.size // cols
    tr = rows
    for cand in (512, 256, 128, 64, 32, 16):
        if rows % cand == 0 and cand * cols * 4 <= 2 * 1024 * 1024:
            tr = cand
            break

    def body(own_ref, recv_ref, o_ref):
        acc = own_ref[...].astype(F32)
        for r in range(3):
            acc = acc + recv_ref[r].astype(F32)
        o_ref[...] = acc

    out = pl.pallas_call(
        body,
        grid=(rows // tr,),
        in_specs=[pl.BlockSpec((tr, cols), lambda i: (i, 0)), pl.BlockSpec((3, tr, cols), lambda i: (0, i, 0))],
        out_specs=pl.BlockSpec((tr, cols), lambda i: (i, 0)),
        out_shape=SDS((rows, cols), F32),
        name=name,
        compiler_params=_cparams("parallel"),
    )(own.reshape(rows, cols), recv.reshape(3, rows, cols))
    return out.reshape(shape)


def _my_place():
    return lax.axis_index("x"), lax.axis_index("y"), lax.axis_index("c")


def _flip(v, bit):
    return 1 - v if bit else v


def _allgather8(x, name):
    r, n = x.shape

    def body(x_ref, o_ref, send_sems, recv_sems, local_sem):
        mx, my, mc = _my_place()
        me = 4 * mx + 2 * my + mc
        mine = pltpu.make_async_copy(x_ref, o_ref.at[me], local_sem)
        mine.start()
        sends = []
        for rel in range(1, 8):
            peer = (_flip(mx, rel & 4), _flip(my, rel & 2), _flip(mc, rel & 1))
            cp = pltpu.make_async_remote_copy(
                src_ref=x_ref, dst_ref=o_ref.at[me], send_sem=send_sems.at[rel - 1], recv_sem=recv_sems.at[rel - 1],
                device_id=peer, device_id_type=MESH,
            )
            cp.start()
            sends.append(cp)
        for rel in range(1, 8):
            px, py, pc = _flip(mx, rel & 4), _flip(my, rel & 2), _flip(mc, rel & 1)
            pltpu.make_async_remote_copy(
                src_ref=x_ref, dst_ref=o_ref.at[4 * px + 2 * py + pc], send_sem=send_sems.at[rel - 1],
                recv_sem=recv_sems.at[rel - 1], device_id=(px, py, pc), device_id_type=MESH,
            ).wait_recv()
        for cp in sends:
            cp.wait_send()
        mine.wait()

    return pl.pallas_call(
        body,
        out_shape=SDS((8, r, n), x.dtype),
        in_specs=[pl.BlockSpec(memory_space=pl.ANY)],
        out_specs=pl.BlockSpec(memory_space=pl.ANY),
        scratch_shapes=[pltpu.SemaphoreType.DMA((7,)), pltpu.SemaphoreType.DMA((7,)), pltpu.SemaphoreType.DMA],
        name=name,
    )(x)


_HBM = pl.BlockSpec(memory_space=pl.ANY)


_SEM = pl.BlockSpec(memory_space=pltpu.SEMAPHORE)
_HBM_ONLY = pl.BlockSpec(memory_space=pltpu.HBM)
_EFFECT = pltpu.SideEffectType.DATAFLOW_SIDE_EFFECTING


def _in_hbm(a):
    return pltpu.with_memory_space_constraint(a, pltpu.HBM)


def _gather_start(lands, after, name):
    n = len(lands)

    def body(*refs):
        land = refs[:n]
        send_sems, recv_sems = refs[n + 1], refs[n + 2]
        token = refs[-1]
        mx, my, mc = _my_place()
        for i in range(n):
            for rel in range(1, 4):
                pltpu.make_async_remote_copy(
                    src_ref=land[i].at[2 * mx + my], dst_ref=land[i].at[2 * mx + my],
                    send_sem=send_sems.at[3 * i + rel - 1], recv_sem=recv_sems.at[3 * i + rel - 1],
                    device_id=(_flip(mx, rel & 2), _flip(my, rel & 1), mc), device_id_type=MESH,
                ).start()
        token[...] = jnp.zeros_like(token)

    outs = pl.pallas_call(
        body,
        name=name,
        out_shape=(
            pltpu.SemaphoreType.DMA((3 * n,)), pltpu.SemaphoreType.DMA((3 * n,)),
            *[pltpu.HBM(a.shape, a.dtype) for a in lands], SDS((8, LANES), F32),
        ),
        in_specs=[_HBM_ONLY] * n + [_HBM],
        out_specs=(_SEM, _SEM, *[_HBM_ONLY] * n, pl.BlockSpec(memory_space=pltpu.VMEM)),
        input_output_aliases={i: 2 + i for i in range(n)},
        compiler_params=pltpu.CompilerParams(has_side_effects=_EFFECT),
    )(*[_in_hbm(a) for a in lands], after)
    return outs[0], outs[1], list(outs[2 : 2 + n]), outs[-1]


def _gather_wait(send_sems, recv_sems, lands, after, name):
    n = len(lands)

    def body(*refs):
        land = refs[:n]
        s_sems, r_sems = refs[n], refs[n + 1]
        mx, my, mc = _my_place()
        for i in range(n):
            for rel in range(1, 4):
                px, py = _flip(mx, rel & 2), _flip(my, rel & 1)
                cp = pltpu.make_async_remote_copy(
                    src_ref=land[i].at[2 * mx + my], dst_ref=land[i].at[2 * px + py],
                    send_sem=s_sems.at[3 * i + rel - 1], recv_sem=r_sems.at[3 * i + rel - 1],
                    device_id=(px, py, mc), device_id_type=MESH,
                )
                cp.wait_send()
                cp.wait_recv()

    outs = pl.pallas_call(
        body,
        name=name,
        out_shape=tuple(pltpu.HBM(a.shape, a.dtype) for a in lands),
        in_specs=[_HBM_ONLY] * n + [_SEM, _SEM, _HBM],
        out_specs=[_HBM_ONLY] * n,
        input_output_aliases={i: i for i in range(n)},
        compiler_params=pltpu.CompilerParams(has_side_effects=_EFFECT),
    )(*lands, send_sems, recv_sems, after)
    return list(outs)


def _scatter_start(slabs, lands, places, name):
    n = len(slabs)

    def body(*refs):
        ins, land = refs[:n], refs[n : 2 * n]
        send_sems, recv_sems = refs[2 * n], refs[2 * n + 1]
        token = refs[-1]
        mx, my, mc = _my_place()
        for i in range(n):
            for rel in range(1, 4):
                px, py = _flip(mx, rel & 2), _flip(my, rel & 1)
                pltpu.make_async_remote_copy(
                    src_ref=ins[i].at[2 * px + py], dst_ref=land[i].at[rel - 1, places[i]],
                    send_sem=send_sems.at[3 * i + rel - 1], recv_sem=recv_sems.at[3 * i + rel - 1],
                    device_id=(px, py, mc), device_id_type=MESH,
                ).start()
        token[...] = jnp.zeros_like(token)

    outs = pl.pallas_call(
        body,
        name=name,
        out_shape=(
            pltpu.SemaphoreType.DMA((3 * n,)), pltpu.SemaphoreType.DMA((3 * n,)),
            *[pltpu.HBM(a.shape, a.dtype) for a in slabs], *[pltpu.HBM(a.shape, a.dtype) for a in lands],
            SDS((8, LANES), F32),
        ),
        in_specs=[_HBM_ONLY] * (2 * n),
        out_specs=(_SEM, _SEM, *[_HBM_ONLY] * (2 * n), pl.BlockSpec(memory_space=pltpu.VMEM)),
        input_output_aliases={i: 2 + i for i in range(2 * n)},
        compiler_params=pltpu.CompilerParams(has_side_effects=_EFFECT),
    )(*[_in_hbm(a) for a in slabs], *[_in_hbm(a) for a in lands])
    return outs[0], outs[1], list(outs[2 : 2 + n]), list(outs[2 + n : 2 + 2 * n]), outs[-1]


def _scatter_wait(send_sems, recv_sems, slabs, lands, places, after, name):
    n = len(slabs)

    def body(*refs):
        ins, land = refs[:n], refs[n : 2 * n]
        s_sems, r_sems = refs[2 * n], refs[2 * n + 1]
        mx, my, mc = _my_place()
        for i in range(n):
            for rel in range(1, 4):
                px, py = _flip(mx, rel & 2), _flip(my, rel & 1)
                cp = pltpu.make_async_remote_copy(
                    src_ref=ins[i].at[2 * px + py], dst_ref=land[i].at[rel - 1, places[i]],
                    send_sem=s_sems.at[3 * i + rel - 1], recv_sem=r_sems.at[3 * i + rel - 1],
                    device_id=(px, py, mc), device_id_type=MESH,
                )
                cp.wait_send()
                cp.wait_recv()

    outs = pl.pallas_call(
        body,
        name=name,
        out_shape=(*[pltpu.HBM(a.shape, a.dtype) for a in slabs], *[pltpu.HBM(a.shape, a.dtype) for a in lands]),
        in_specs=[_HBM_ONLY] * (2 * n) + [_SEM, _SEM, _HBM],
        out_specs=[_HBM_ONLY] * (2 * n),
        input_output_aliases={i: i for i in range(2 * n)},
        compiler_params=pltpu.CompilerParams(has_side_effects=_EFFECT),
    )(*slabs, *lands, send_sems, recv_sems, after)
    return list(outs[:n]), list(outs[n:])


def _swap_start(parts, after, name):
    n = len(parts)
    lands = [lax.empty(a.shape, a.dtype) for a in parts]

    def body(*refs):
        ins, land = refs[:n], refs[n : 2 * n]
        send_sems, recv_sems = refs[2 * n + 1], refs[2 * n + 2]
        mx, my, mc = _my_place()
        for i in range(n):
            pltpu.make_async_remote_copy(
                src_ref=ins[i], dst_ref=land[i], send_sem=send_sems.at[i], recv_sem=recv_sems.at[i],
                device_id=(mx, my, 1 - mc), device_id_type=MESH,
            ).start()

    outs = pl.pallas_call(
        body,
        name=name,
        out_shape=(
            pltpu.SemaphoreType.DMA((n,)), pltpu.SemaphoreType.DMA((n,)),
            *[pltpu.HBM(a.shape, a.dtype) for a in parts], *[pltpu.HBM(a.shape, a.dtype) for a in lands],
        ),
        in_specs=[_HBM_ONLY] * (2 * n) + [_HBM],
        out_specs=(_SEM, _SEM, *[_HBM_ONLY] * (2 * n)),
        input_output_aliases={i: 2 + i for i in range(2 * n)},
        compiler_params=pltpu.CompilerParams(has_side_effects=_EFFECT),
    )(*[_in_hbm(a) for a in parts], *[_in_hbm(a) for a in lands], after)
    return outs[0], outs[1], list(outs[2 : 2 + n]), list(outs[2 + n :])


def _swap_wait(send_sems, recv_sems, parts, lands, after, name):
    n = len(parts)

    def body(*refs):
        ins, land = refs[:n], refs[n : 2 * n]
        s_sems, r_sems = refs[2 * n], refs[2 * n + 1]
        mx, my, mc = _my_place()
        for i in range(n):
            cp = pltpu.make_async_remote_copy(
                src_ref=ins[i], dst_ref=land[i], send_sem=s_sems.at[i], recv_sem=r_sems.at[i],
                device_id=(mx, my, 1 - mc), device_id_type=MESH,
            )
            cp.wait_send()
            cp.wait_recv()

    outs = pl.pallas_call(
        body,
        name=name,
        out_shape=(*[pltpu.HBM(a.shape, a.dtype) for a in parts], *[pltpu.HBM(a.shape, a.dtype) for a in lands]),
        in_specs=[_HBM_ONLY] * (2 * n) + [_SEM, _SEM, _HBM],
        out_specs=[_HBM_ONLY] * (2 * n),
        input_output_aliases={i: i for i in range(2 * n)},
        compiler_params=pltpu.CompilerParams(has_side_effects=_EFFECT),
    )(*parts, *lands, send_sems, recv_sems, after)
    return list(outs[:n]), list(outs[n:])


def _pad_rows(a, rows):
    return jnp.pad(a, ((0, rows - a.shape[0]), (0, 0)))


def kernel(x, c, positions, mla_w_in, mla_q_norm, mla_w_qb, mla_kv_norm, mla_w_kvb, mla_w_o, hgrn_lb, hgrn_w_in, hgrn_g_norm, hgrn_w_o, ffn_w_in, ffn_w_out, ada_w, ada_b, ln_g, ln_b, loss_target, m_mla_w_in, m_mla_q_norm, m_mla_w_qb, m_mla_kv_norm, m_mla_w_kvb, m_mla_w_o, m_hgrn_lb, m_hgrn_w_in, m_hgrn_g_norm, m_hgrn_w_o, m_ffn_w_in, m_ffn_w_out, m_ada_w, m_ada_b, m_ln_g, m_ln_b, v_mla_w_in, v_mla_q_norm, v_mla_w_qb, v_mla_kv_norm, v_mla_w_kvb, v_mla_w_o, v_hgrn_lb, v_hgrn_w_in, v_hgrn_g_norm, v_hgrn_w_o, v_ffn_w_in, v_ffn_w_out, v_ada_w, v_ada_b, v_ln_g, v_ln_b):
    B, S, D = x.shape
    T = B * S
    depth = ada_w.shape[0]
    n_mla, n_hgrn = mla_w_in.shape[0], hgrn_w_in.shape[0]
    n_sub = 2 * depth
    alpha = (2.0 * depth) ** 0.25
    mx, my, mc = _my_place()
    me = 4 * mx + 2 * my + mc
    k_me = 2 * mx + my
    Bg = 8 * B
    HK = hgrn_w_o.shape[1] * 4
    dq = D // 4

    lbw = hgrn_lb.shape[1]
    first = jnp.zeros((8, max(D, 4 * lbw)), F32)
    first = first.at[:B, :D].set(c).at[B : B + n_hgrn, :lbw].set(hgrn_lb)
    first_all = _allgather8(first, "gather_cond")
    c_all = first_all[:, :B, :D].reshape(Bg, D)
    lb_logits = jnp.concatenate([first_all[2 * k, B : B + n_hgrn, :lbw] for k in range(4)], axis=1)

    def lower_bounds_fn(logits):
        soft = jax.nn.softmax(logits, axis=0)
        return jnp.cumsum(soft, axis=0) - soft[0]

    lower_bounds, lower_bounds_vjp = jax.vjp(lower_bounds_fn, lb_logits)

    n_ada = ada_w.shape[-1]
    mod_part = _ada_fwd(c_all, ada_w.reshape(n_sub, D, n_ada), ada_b.reshape(n_sub, 1, n_ada), "ada_fwd")
    mod_all = _allgather8(mod_part.reshape(n_sub * Bg, n_ada), "gather_mod").reshape(8, n_sub, Bg, n_ada)
    mod = jnp.concatenate([mod_all[2 * k] for k in range(4)], axis=-1)
    mod = lax.dynamic_slice_in_dim(mod, me * B, B, axis=1)
    shift = [mod[j, :, None, :D] for j in range(n_sub)]
    scale = [mod[j, :, None, D : 2 * D] for j in range(n_sub)]
    gate = [mod[j, :, None, 2 * D :] for j in range(n_sub)]

    ln_rows = 2 * n_sub
    ln_local = _pad_rows(jnp.concatenate([ln_g.reshape(n_sub, dq), ln_b.reshape(n_sub, dq)], axis=0), -(-ln_rows // 8) * 8)
    ln_pad = jnp.zeros((ln_local.shape[0], -(-dq // LANES) * LANES), F32).at[:, :dq].set(ln_local)
    ln_all = _allgather8(ln_pad, "gather_ln")
    ln_full = jnp.concatenate([ln_all[2 * k, :ln_rows, :dq] for k in range(4)], axis=1)
    lng = [ln_full[j][None, :] for j in range(n_sub)]
    lnb = [ln_full[n_sub + j][None, :] for j in range(n_sub)]

    main = dict(mla_w_in=mla_w_in, mla_w_qb=mla_w_qb, mla_w_kvb=mla_w_kvb, mla_w_o=mla_w_o, hgrn_w_in=hgrn_w_in,
                hgrn_w_o=hgrn_w_o, ffn_w_in=ffn_w_in, ffn_w_out=ffn_w_out)
    names = list(main)

    def group_kinds(layer, part):
        if part:
            return [("ffn_w_in", layer), ("ffn_w_out", layer)]
        mixer = ["mla_w_in", "mla_w_qb", "mla_w_kvb", "mla_w_o"] if layer % 2 == 0 else ["hgrn_w_in", "hgrn_w_o"]
        return [(k, layer // 2) for k in mixer]

    gathers = {}
    after = mod_all[0, 0, :8, :LANES] + ln_all[0, :8, :LANES]
    for layer in range(depth):
        for part in range(2):
            lands = [lax.dynamic_update_index_in_dim(lax.empty((4,) + main[k].shape[1:], BF16), main[k][i].astype(BF16), k_me, 0)
                     for k, i in group_kinds(layer, part)]
            ssem, rsem, lands, after = _gather_start(lands, after, f"gather_start_l{layer}p{part}")
            gathers[layer, part] = (ssem, rsem, lands)
    scale[0] = scale[0] + after[0, 0]

    def row_w(g):
        return g.reshape(1, g.shape[0] * g.shape[1], g.shape[2])

    def full_w_in(g):
        return jnp.transpose(g, (1, 0, 2)).reshape(1, g.shape[1], 4 * g.shape[2])

    ang = positions.astype(F32)[..., None] * (ROPE_THETA ** (-jnp.arange(0, QK_ROPE, 2, dtype=F32) / QK_ROPE))
    cos, sin = jnp.cos(ang), jnp.sin(ang)

    gq = [mla_q_norm[j][None, :] for j in range(n_mla)]
    gkv = [mla_kv_norm[j][None, :] for j in range(n_mla)]
    gn = [hgrn_g_norm[j][None, :] for j in range(n_hgrn)]

    def r2(a):
        return a.reshape(T, a.shape[-1])

    def r3(a):
        return a.reshape(B, S, a.shape[-1])

    saved = []
    xs = x
    for layer in range(depth):
        j = layer // 2
        sub = 2 * layer
        tag = f"l{layer}"
        ssem, rsem, lands = gathers[layer, 0]
        lands = _gather_wait(ssem, rsem, lands, xs if layer else scale[0], f"gather_wait_{tag}p0")
        wl = {k: g for (k, _), g in zip(group_kinds(layer, 0), lands)}
        if layer == 0:
            h = _modulate(xs, scale[sub], shift[sub], f"mod_{tag}a")
        if layer % 2 == 0:
            wl["mla_w_in"] = full_w_in(wl["mla_w_in"])
            proj = r3(_mm_nn(r2(h), wl["mla_w_in"], F32, f"mla_in_{tag}"))
            qn, kvn = _mla_mid_fwd(proj, gq[j], gkv[j], f"mla_mid_{tag}")
            q = r3(_mm_nn(r2(qn), wl["mla_w_qb"], F32, f"mla_qb_{tag}"))
            kv = r3(_mm_nn(r2(kvn), wl["mla_w_kvb"], F32, f"mla_kvb_{tag}"))
            qh, kh, vh = _mla_prep_fwd(q, kv, proj, cos, sin, f"mla_prep_{tag}")
            o, lse = _attn_fwd(qh, kh, vh, f"attn_{tag}")
            wl["mla_w_o"] = row_w(wl["mla_w_o"])
            y = r3(_mm_nn(r2(o), wl["mla_w_o"], F32, f"mla_o_{tag}"))
            mix = (h, proj, qn, kvn, qh, kh, vh, o, lse)
        else:
            proj = r3(_mm_nn(r2(h), wl["hgrn_w_in"], F32, f"hgrn_in_{tag}"))
            og, o_pre, states = _hgrn_fwd(proj, lower_bounds[j][None, :], gn[j], f"hgrn_{tag}")
            wl["hgrn_w_o"] = row_w(wl["hgrn_w_o"])
            y = r3(_mm_nn(r2(og), wl["hgrn_w_o"], F32, f"hgrn_o_{tag}"))
            mix = (h, proj, og, o_pre, states)
        x1, h2 = _ln_mod_fwd(alpha, xs, y, gate[sub], lng[sub], lnb[sub], scale[sub + 1], shift[sub + 1], f"ln_{tag}a")
        ssem, rsem, lands = gathers[layer, 1]
        lands = _gather_wait(ssem, rsem, lands, x1, f"gather_wait_{tag}p1")
        wl.update({k: g for (k, _), g in zip(group_kinds(layer, 1), lands)})
        a, ug, uu = [r3(t_) for t_ in _ffn_in(r2(h2), wl["ffn_w_in"], f"ffn_in_{tag}")]
        wl["ffn_w_out"] = row_w(wl["ffn_w_out"])
        y2 = r3(_mm_nn(r2(a), wl["ffn_w_out"], F32, f"ffn_out_{tag}"))
        if layer + 1 < depth:
            x2, h_next = _ln_mod_fwd(alpha, x1, y2, gate[sub + 1], lng[sub + 1], lnb[sub + 1], scale[sub + 2], shift[sub + 2], f"ln_{tag}b")
        else:
            x2, h_next = _ln_fwd(alpha, x1, y2, gate[sub + 1], lng[sub + 1], lnb[sub + 1], f"ln_{tag}b"), None
        saved.append((xs, y, x1, y2, mix, h2, ug, uu, a, wl))
        xs, h = x2, h_next

    loss_local, dout = _loss_head(xs, loss_target, "loss_head")
    loss = lax.psum(loss_local, ("x", "y", "c"))

    gw = {k: [None] * main[k].shape[0] for k in names}
    land = {k: lax.empty((3,) + main[k].shape, BF16) for k in names}
    scatters = []
    d_shift, d_scale, d_gate = [None] * n_sub, [None] * n_sub, [None] * n_sub
    d_lng, d_lnb = [None] * n_sub, [None] * n_sub
    d_gq, d_gkv, d_gn, d_lbnd = [None] * n_mla, [None] * n_mla, [None] * n_hgrn, [None] * n_hgrn

    def rows4(g):
        return g.reshape(4, g.shape[1] // 4, g.shape[2])

    def scatter_kinds(layer, part):
        if part == 1 or layer % 2:
            return group_kinds(layer, part)
        mixer = group_kinds(layer, 0)
        return mixer[:1] if part == 0 else mixer[1:]

    def start_scatter(layer, part, params, at):
        kinds = scatter_kinds(layer, part)
        ssem, rsem, slabs_t, lands_t, token = _scatter_start(
            [gw[k][i] for k, i in kinds], [land[k] for k, _ in kinds], [i for _, i in kinds], f"scatter_start_l{layer}p{part}")
        for (k, i), s_t, l_t in zip(kinds, slabs_t, lands_t):
            gw[k][i], land[k] = s_t, l_t
        scatters.append((layer, part, ssem, rsem))
        if params is not None:
            params[at] = params[at] + token[0, 0]

    for layer in reversed(range(depth)):
        j = layer // 2
        sub = 2 * layer
        tag = f"l{layer}"
        xs, y, x1, y2, mix, h2, ug, uu, a, wl = saved[layer]
        if layer + 1 == depth:
            dxr, dy2, d_gate[sub + 1], d_lng[sub + 1], d_lnb[sub + 1] = _ln_bwd(
                alpha, dout, x1, y2, gate[sub + 1], lng[sub + 1], lnb[sub + 1], f"ln_bwd_{tag}b")
        else:
            dxr, dy2, d_gate[sub + 1], d_lng[sub + 1], d_lnb[sub + 1], d_scale[sub + 2], d_shift[sub + 2] = _ln_mod_bwd(
                alpha, dh, dxr, scale[sub + 2], x1, y2, gate[sub + 1], lng[sub + 1], lnb[sub + 1], f"ln_bwd_{tag}b")
        da = r3(_mm_nt(r2(dy2), wl["ffn_w_out"], F32, f"ffn_out_dx_{tag}"))
        gw["ffn_w_out"][layer] = rows4(_mm_tn(r2(a), r2(dy2), 1, BF16, f"ffn_out_dw_{tag}"))
        du = _swiglu_bwd(ug, uu, da, f"swiglu_bwd_{tag}")
        dh2 = r3(_mm_nt(r2(du), wl["ffn_w_in"], F32, f"ffn_in_dx_{tag}"))
        gw["ffn_w_in"][layer] = _mm_tn(r2(h2), r2(du), 4, BF16, f"ffn_in_dw_{tag}")
        start_scatter(layer, 1, gate, sub)
        dxr, dy, d_gate[sub], d_lng[sub], d_lnb[sub], d_scale[sub + 1], d_shift[sub + 1] = _ln_mod_bwd(
            alpha, dh2, dxr, scale[sub + 1], xs, y, gate[sub], lng[sub], lnb[sub], f"ln_bwd_{tag}a")
        if layer % 2 == 0:
            h, proj, qn, kvn, qh, kh, vh, o, lse = mix
            do = r3(_mm_nt(r2(dy), wl["mla_w_o"], BF16, f"mla_o_dx_{tag}"))
            gw["mla_w_o"][j] = rows4(_mm_tn(r2(o), r2(dy), 1, BF16, f"mla_o_dw_{tag}"))
            dqh, dkh, dvh = _attn_bwd(qh, kh, vh, o, do, lse, f"attn_bwd_{tag}")
            dq_, dkv_, dkr = _mla_prep_bwd(dqh, dkh, dvh, cos, sin, f"mla_prep_bwd_{tag}")
            dqn = r3(_mm_nt(r2(dq_), wl["mla_w_qb"], F32, f"mla_qb_dx_{tag}"))
            gw["mla_w_qb"][j] = _mm_tn(r2(qn), r2(dq_), 4, BF16, f"mla_qb_dw_{tag}")
            dkvn = r3(_mm_nt(r2(dkv_), wl["mla_w_kvb"], F32, f"mla_kvb_dx_{tag}"))
            gw["mla_w_kvb"][j] = _mm_tn(r2(kvn), r2(dkv_), 4, BF16, f"mla_kvb_dw_{tag}")
            start_scatter(layer, 2, gq, j)
            dproj, dgq_, dgkv_ = _mla_mid_bwd(proj, dqn, dkvn, dkr, gq[j], gkv[j], f"mla_mid_bwd_{tag}")
            d_gq[j], d_gkv[j] = dgq_.sum(0), dgkv_.sum(0)
            dh = r3(_mm_nt(r2(dproj), wl["mla_w_in"], F32, f"mla_in_dx_{tag}"))
            gwin = _mm_tn(r2(h), r2(dproj), 1, BF16, f"mla_in_dw_{tag}")[0]
            gw["mla_w_in"][j] = jnp.transpose(gwin.reshape(gwin.shape[0], 4, gwin.shape[1] // 4), (1, 0, 2))
        else:
            h, proj, og, o_pre, states = mix
            dog = r3(_mm_nt(r2(dy), wl["hgrn_w_o"], F32, f"hgrn_o_dx_{tag}"))
            gw["hgrn_w_o"][j] = rows4(_mm_tn(r2(og), r2(dy), 1, BF16, f"hgrn_o_dw_{tag}"))
            dq_, df_, di_, dg_, dlb_, dgn_ = _hgrn_bwd(proj, lower_bounds[j][None, :], gn[j], o_pre, states, dog, f"hgrn_bwd_{tag}")
            dproj = jnp.concatenate([dq_, df_, di_, dg_], axis=-1)
            d_lbnd[j] = dlb_.sum(0).reshape(1, HK)
            d_gn[j] = dgn_.sum((0, 1))
            dh = r3(_mm_nt(r2(dproj), wl["hgrn_w_in"], F32, f"hgrn_in_dx_{tag}"))
            gw["hgrn_w_in"][j] = _mm_tn(r2(h), r2(dproj), 4, BF16, f"hgrn_in_dw_{tag}")
        start_scatter(layer, 0, gate if layer else None, sub - 1)
    grad_x, d_scale[0], d_shift[0] = _mod_bwd(dh, dxr, x, scale[0], "mod_bwd_l0a")

    for layer, part, ssem, rsem in scatters:
        kinds = scatter_kinds(layer, part)
        slabs_t, lands_t = _scatter_wait(
            ssem, rsem, [gw[k][i] for k, i in kinds], [land[k] for k, _ in kinds], [i for _, i in kinds], grad_x,
            f"scatter_wait_l{layer}p{part}")
        for (k, i), s_t, l_t in zip(kinds, slabs_t, lands_t):
            gw[k][i], land[k] = s_t, l_t
    sums = [_sum4(jnp.stack([lax.dynamic_index_in_dim(g, k_me, 0, keepdims=False) for g in gw[k]]), land[k], f"sum4_{k}")
            for k in names]

    dmod = jnp.stack([jnp.concatenate([d_shift[s_][:, 0], d_scale[s_][:, 0], d_gate[s_][:, 0]], axis=-1) for s_ in range(n_sub)])
    dmod_rows = _pad_rows(dmod.reshape(n_sub * B, 3 * D), -(-n_sub * B // 8) * 8)
    dmod_all = _allgather8(dmod_rows, "gather_dmod")[:, : n_sub * B].reshape(8, n_sub, B, 3 * D)
    dmod_all = jnp.transpose(dmod_all, (1, 0, 2, 3)).reshape(n_sub, Bg, 3 * D)
    dmod_mine = lax.dynamic_slice_in_dim(dmod_all, k_me * n_ada, n_ada, axis=2)
    g_ada_w, g_ada_b = _ada_bwd(c_all, dmod_mine, "ada_bwd")
    g_ada_w = g_ada_w.reshape(ada_w.shape)
    g_ada_b = g_ada_b.reshape(ada_b.shape)

    small = [jnp.stack(d_gq).reshape(-1), jnp.stack(d_gkv).reshape(-1), jnp.stack(d_gn).reshape(-1),
             jnp.stack(d_lbnd).reshape(-1), jnp.stack([d.sum(0) for d in d_lng]).reshape(-1),
             jnp.stack([d.sum(0) for d in d_lnb]).reshape(-1)]
    sizes = [s_.shape[0] for s_ in small]
    flat = jnp.concatenate(small)
    rows_small = -(-flat.shape[0] // (8 * LANES)) * 8
    flat = jnp.pad(flat, (0, rows_small * LANES - flat.shape[0])).reshape(rows_small, LANES)
    tot = _allgather8(flat, "gather_small")
    acc = tot[0]
    for d in range(1, 8):
        acc = acc + tot[d]
    acc = acc.reshape(-1)
    offs = [0]
    for s_ in sizes:
        offs.append(offs[-1] + s_)
    g_q_norm = acc[offs[0] : offs[1]].reshape(mla_q_norm.shape)
    g_kv_norm = acc[offs[1] : offs[2]].reshape(mla_kv_norm.shape)
    g_g_norm = acc[offs[2] : offs[3]].reshape(hgrn_g_norm.shape)
    g_lbnd = acc[offs[3] : offs[4]].reshape(n_hgrn, HK)
    g_lb_full = lower_bounds_vjp(g_lbnd)[0]
    g_hgrn_lb = lax.dynamic_slice_in_dim(g_lb_full, k_me * lbw, lbw, axis=1)
    g_lng = lax.dynamic_slice_in_dim(acc[offs[4] : offs[5]].reshape(n_sub, D), k_me * dq, dq, axis=1).reshape(ln_g.shape)
    g_lnb = lax.dynamic_slice_in_dim(acc[offs[5] : offs[6]].reshape(n_sub, D), k_me * dq, dq, axis=1).reshape(ln_b.shape)

    weights = dict(mla_w_in=mla_w_in, mla_q_norm=mla_q_norm, mla_w_qb=mla_w_qb, mla_kv_norm=mla_kv_norm, mla_w_kvb=mla_w_kvb,
                   mla_w_o=mla_w_o, hgrn_lb=hgrn_lb, hgrn_w_in=hgrn_w_in, hgrn_g_norm=hgrn_g_norm, hgrn_w_o=hgrn_w_o,
                   ffn_w_in=ffn_w_in, ffn_w_out=ffn_w_out, ada_w=ada_w, ada_b=ada_b, ln_g=ln_g, ln_b=ln_b)
    moms = dict(mla_w_in=(m_mla_w_in, v_mla_w_in), mla_q_norm=(m_mla_q_norm, v_mla_q_norm), mla_w_qb=(m_mla_w_qb, v_mla_w_qb),
                mla_kv_norm=(m_mla_kv_norm, v_mla_kv_norm), mla_w_kvb=(m_mla_w_kvb, v_mla_w_kvb), mla_w_o=(m_mla_w_o, v_mla_w_o),
                hgrn_lb=(m_hgrn_lb, v_hgrn_lb), hgrn_w_in=(m_hgrn_w_in, v_hgrn_w_in), hgrn_g_norm=(m_hgrn_g_norm, v_hgrn_g_norm),
                hgrn_w_o=(m_hgrn_w_o, v_hgrn_w_o), ffn_w_in=(m_ffn_w_in, v_ffn_w_in), ffn_w_out=(m_ffn_w_out, v_ffn_w_out),
                ada_w=(m_ada_w, v_ada_w), ada_b=(m_ada_b, v_ada_b), ln_g=(m_ln_g, v_ln_g), ln_b=(m_ln_b, v_ln_b))
    grads = dict(mla_q_norm=(g_q_norm,), mla_kv_norm=(g_kv_norm,), hgrn_lb=(g_hgrn_lb,), hgrn_g_norm=(g_g_norm,),
                 ada_w=(g_ada_w,), ada_b=(g_ada_b,), ln_g=(g_lng,), ln_b=(g_lnb,))

    def adamw(k):
        return _adamw(weights[k], [g_.reshape(weights[k].shape) for g_ in grads[k]], moms[k][0], moms[k][1], f"adamw_{k}")

    ssem, rsem, sums, others = _swap_start(sums, tot[0, :8] + dmod_all[0, :8, :LANES], "swap_start")
    res = {k: adamw(k) for k in grads}
    sums, others = _swap_wait(ssem, rsem, sums, others, res["ada_w"][1], "swap_wait")
    grads.update({k: (a_, b_) for k, a_, b_ in zip(names, sums, others)})
    res.update({k: adamw(k) for k in names})
    order = list(weights)
    return (loss, grad_x, *[res[k][0] for k in order], *[res[k][1] for k in order], *[res[k][2] for k in order],
            *[res[k][3] for k in order])
```

```python
import functools

import jax
import jax.numpy as jnp
from jax import lax
from jax.experimental import pallas as pl
from jax.experimental.pallas import tpu as pltpu

F32 = jnp.float32
BF16 = jnp.bfloat16
SDS = jax.ShapeDtypeStruct
MESH = pl.DeviceIdType.MESH
HI = lax.Precision.HIGHEST
MID = lax.Precision.HIGH

MLA_HEADS, QK_NOPE, QK_ROPE, V_HEAD = 16, 64, 32, 64
Q_LORA, KV_LORA = 768, 256
QK_DIM = QK_NOPE + QK_ROPE
ROPE_THETA = 10000.0
HGRN_K = 128
HGRN_CHUNK = 128
HGRN_SUB = 32
HGRN_PAR = 2
LN_EPS, RMS_EPS = 1e-5, 1e-6
ADAM_LR, ADAM_B1, ADAM_B2, ADAM_EPS, ADAM_WD, ADAM_STEP = 0.001, 0.9, 0.999, 1e-08, 0.01, 10
NEG = -1e30

VMEM_LIMIT_BYTES = 56 * 1024 * 1024
LANES = 128
SUBLANES = 8


def _cparams(*sem):
    return pltpu.CompilerParams(dimension_semantics=sem if sem else None, vmem_limit_bytes=VMEM_LIMIT_BYTES)


def _pick_tile(n, cap):
    best = 0
    for t in range(LANES, min(n, cap) + 1, LANES):
        if n % t == 0:
            best = t
    return best if best else n


def _bdot(a, b):
    return jnp.dot(a.astype(BF16), b.astype(BF16), preferred_element_type=F32)


def _bdot_nt(a, b):
    return lax.dot_general(a.astype(BF16), b.astype(BF16), (((1,), (1,)), ((), ())), preferred_element_type=F32)


def _bdot_tn(a, b):
    return lax.dot_general(a.astype(BF16), b.astype(BF16), (((0,), (0,)), ((), ())), preferred_element_type=F32)


def _hdot(a, b):
    return jnp.dot(a, b, precision=HI, preferred_element_type=F32)


def _mdot(a, b):
    return jnp.dot(a, b, precision=MID, preferred_element_type=F32)


def _mdot_nt(a, b):
    return lax.dot_general(a, b, (((1,), (1,)), ((), ())), precision=MID, preferred_element_type=F32)


def _mdot_tn(a, b):
    return lax.dot_general(a, b, (((0,), (0,)), ((), ())), precision=MID, preferred_element_type=F32)


def _mm_nn(a, w, out_dtype, name):
    M, K = a.shape
    G, _, n = w.shape
    tm = min(512, M)
    tn = _pick_tile(n, 1536)
    nps = n // tn

    def body(a_ref, w_ref, o_ref):
        o_ref[...] = _bdot(a_ref[...], w_ref[...]).astype(o_ref.dtype)

    return pl.pallas_call(
        body,
        grid=(G * nps, M // tm),
        in_specs=[
            pl.BlockSpec((tm, K), lambda j, i: (i, 0)),
            pl.BlockSpec((None, K, tn), lambda j, i: (j // nps, 0, j % nps)),
        ],
        out_specs=pl.BlockSpec((tm, tn), lambda j, i: (i, j)),
        out_shape=SDS((M, G * n), out_dtype),
        name=name,
        compiler_params=_cparams("parallel", "parallel"),
    )(a, w)


def _mm_nt(a, w, out_dtype, name):
    M = a.shape[0]
    G, K, n = w.shape
    tm = min(512, M)
    tk = _pick_tile(K, 1536)

    def body(a_ref, w_ref, o_ref, acc_ref):
        s = pl.program_id(2)

        @pl.when(s == 0)
        def _():
            acc_ref[...] = jnp.zeros_like(acc_ref)

        acc_ref[...] += _bdot_nt(a_ref[...], w_ref[...])

        @pl.when(s == G - 1)
        def _():
            o_ref[...] = acc_ref[...].astype(o_ref.dtype)

    return pl.pallas_call(
        body,
        grid=(K // tk, M // tm, G),
        in_specs=[
            pl.BlockSpec((tm, n), lambda kb, i, s: (i, s)),
            pl.BlockSpec((None, tk, n), lambda kb, i, s: (s, kb, 0)),
        ],
        out_specs=pl.BlockSpec((tm, tk), lambda kb, i, s: (i, kb)),
        out_shape=SDS((M, K), out_dtype),
        scratch_shapes=[pltpu.VMEM((tm, tk), F32)],
        name=name,
        compiler_params=_cparams("parallel", "parallel", "arbitrary"),
    )(a, w)


def _mm_tn(a, d, G, out_dtype, name):
    T, K = a.shape
    n = d.shape[1] // G
    tk = _pick_tile(K, 256)
    tn = _pick_tile(n, 1536)
    nps = n // tn

    def body(a_ref, d_ref, o_ref):
        o_ref[...] = _bdot_tn(a_ref[...], d_ref[...]).astype(o_ref.dtype)

    return pl.pallas_call(
        body,
        grid=(G * nps, K // tk),
        in_specs=[
            pl.BlockSpec((T, tk), lambda j, i: (0, i)),
            pl.BlockSpec((T, tn), lambda j, i: (0, j)),
        ],
        out_specs=pl.BlockSpec((None, tk, tn), lambda j, i: (j // nps, i, j % nps)),
        out_shape=SDS((G, K, n), out_dtype),
        name=name,
        compiler_params=_cparams("parallel", "parallel"),
    )(a, d)


def _rows_call(body, name, B, S, ins, outs, ts=256):
    ts = min(ts, S)
    in_specs, args = [], []
    for arr, kind in ins:
        W = arr.shape[-1]
        if kind == "row":
            in_specs.append(pl.BlockSpec((None, ts, W), lambda b, s: (b, s, 0)))
        elif kind == "ex":
            in_specs.append(pl.BlockSpec((None, 1, W), lambda b, s: (b, 0, 0)))
        else:
            in_specs.append(pl.BlockSpec((1, W), lambda b, s: (0, 0)))
        args.append(arr)
    out_specs, out_shape = [], []
    for W, dt, kind in outs:
        if kind == "row":
            out_specs.append(pl.BlockSpec((None, ts, W), lambda b, s: (b, s, 0)))
            out_shape.append(SDS((B, S, W), dt))
        else:
            out_specs.append(pl.BlockSpec((None, 1, W), lambda b, s: (b, 0, 0)))
            out_shape.append(SDS((B, 1, W), dt))
    return pl.pallas_call(
        body,
        grid=(B, S // ts),
        in_specs=in_specs,
        out_specs=out_specs,
        out_shape=out_shape,
        name=name,
        compiler_params=_cparams("parallel", "arbitrary"),
    )(*args)


def _acc(ref, val):
    @pl.when(pl.program_id(1) == 0)
    def _():
        ref[...] = jnp.zeros_like(ref)

    ref[...] += val


def _mod_fn(x, sc, sh):
    return x * (1.0 + sc) + sh


def _ln_fn(alpha, x, y, gate, g, b):
    z = alpha * x + (1.0 + gate) * y
    mu = jnp.mean(z, -1, keepdims=True)
    var = jnp.mean(jnp.square(z - mu), -1, keepdims=True)
    return (z - mu) * lax.rsqrt(var + LN_EPS) * g + b


def _modulate(x, sc, sh, name):
    B, S, D = x.shape

    def body(x_ref, sc_ref, sh_ref, h_ref):
        h_ref[...] = _mod_fn(x_ref[...], sc_ref[...], sh_ref[...]).astype(BF16)

    return _rows_call(body, name, B, S, [(x, "row"), (sc, "ex"), (sh, "ex")], [(D, BF16, "row")])[0]


def _ln_fwd(alpha, x, y, gate, g, b, name):
    B, S, D = x.shape

    def body(x_ref, y_ref, gate_ref, g_ref, b_ref, o_ref):
        o_ref[...] = _ln_fn(alpha, x_ref[...], y_ref[...], gate_ref[...], g_ref[...], b_ref[...])

    return _rows_call(
        body, name, B, S, [(x, "row"), (y, "row"), (gate, "ex"), (g, "par"), (b, "par")], [(D, F32, "row")]
    )[0]


def _ln_mod_fwd(alpha, x, y, gate, g, b, sc_next, sh_next, name):
    B, S, D = x.shape

    def body(x_ref, y_ref, gate_ref, g_ref, b_ref, sc_ref, sh_ref, o_ref, h_ref):
        out = _ln_fn(alpha, x_ref[...], y_ref[...], gate_ref[...], g_ref[...], b_ref[...])
        o_ref[...] = out
        h_ref[...] = _mod_fn(out, sc_ref[...], sh_ref[...]).astype(BF16)

    return _rows_call(
        body, name, B, S,
        [(x, "row"), (y, "row"), (gate, "ex"), (g, "par"), (b, "par"), (sc_next, "ex"), (sh_next, "ex")],
        [(D, F32, "row"), (D, BF16, "row")],
    )


def _ln_mod_bwd(alpha, dh, dxr_next, sc_next, x, y, gate, g, b, name):
    B, S, D = x.shape

    def body(dh_ref, dxr_ref, sc_ref, x_ref, y_ref, gate_ref, g_ref, b_ref,
             dx_ref, dy_ref, dgate_ref, dg_ref, db_ref, dsc_ref, dsh_ref):
        out, vjp = jax.vjp(
            functools.partial(_ln_fn, alpha), x_ref[...], y_ref[...], gate_ref[...], g_ref[...], b_ref[...]
        )
        dh_v = dh_ref[...]
        dx, dy, dgate, dg, db = vjp(dxr_ref[...] + dh_v * (1.0 + sc_ref[...]))
        dx_ref[...] = dx
        dy_ref[...] = dy.astype(BF16)
        _acc(dgate_ref, dgate)
        _acc(dg_ref, dg)
        _acc(db_ref, db)
        _acc(dsc_ref, jnp.sum(dh_v * out, axis=0, keepdims=True))
        _acc(dsh_ref, jnp.sum(dh_v, axis=0, keepdims=True))

    return _rows_call(
        body, name, B, S,
        [(dh, "row"), (dxr_next, "row"), (sc_next, "ex"), (x, "row"), (y, "row"), (gate, "ex"), (g, "par"), (b, "par")],
        [(D, F32, "row"), (D, BF16, "row")] + [(D, F32, "acc")] * 5,
    )


def _ln_bwd(alpha, dout, x, y, gate, g, b, name):
    B, S, D = x.shape

    def body(do_ref, x_ref, y_ref, gate_ref, g_ref, b_ref, dxr_ref, dy_ref, dgate_ref, dg_ref, db_ref):
        _, vjp = jax.vjp(
            functools.partial(_ln_fn, alpha), x_ref[...], y_ref[...], gate_ref[...], g_ref[...], b_ref[...]
        )
        dx, dy, dgate, dg, db = vjp(do_ref[...])
        dxr_ref[...] = dx
        dy_ref[...] = dy.astype(BF16)
        _acc(dgate_ref, dgate)
        _acc(dg_ref, dg)
        _acc(db_ref, db)

    return _rows_call(
        body,
        name,
        B,
        S,
        [(dout, "row"), (x, "row"), (y, "row"), (gate, "ex"), (g, "par"), (b, "par")],
        [(D, F32, "row"), (D, BF16, "row"), (D, F32, "acc"), (D, F32, "acc"), (D, F32, "acc")],
    )


def _mod_bwd(dh, dxr, x, sc, name):
    B, S, D = x.shape

    def body(dh_ref, dxr_ref, x_ref, sc_ref, dx_ref, dsc_ref, dsh_ref):
        dh_v = dh_ref[...]
        dx_ref[...] = dxr_ref[...] + dh_v * (1.0 + sc_ref[...])
        _acc(dsc_ref, jnp.sum(dh_v * x_ref[...], axis=0, keepdims=True))
        _acc(dsh_ref, jnp.sum(dh_v, axis=0, keepdims=True))

    return _rows_call(
        body,
        name,
        B,
        S,
        [(dh, "row"), (dxr, "row"), (x, "row"), (sc, "ex")],
        [(D, F32, "row"), (D, F32, "acc"), (D, F32, "acc")],
    )


def _loss_head(y, target, name):
    B, S, D = y.shape

    def body(y_ref, t_ref, l_ref, dy_ref):
        e = y_ref[...] - t_ref[...]
        dy_ref[...] = e * (1.0 / D)
        part = 0.5 * jnp.sum(jnp.sum(e * e, axis=1, keepdims=True) * (1.0 / D), axis=0, keepdims=True)
        _acc(l_ref, jnp.broadcast_to(part, (1, LANES)))

    loss, dy = _rows_call(
        body, name, B, S, [(y, "row"), (target, "row")], [(LANES, F32, "acc"), (D, F32, "row")]
    )
    return jnp.sum(loss[:, 0, 0]), dy


def _ffn_in(h, w, name):
    M, K = h.shape
    G, _, n = w.shape
    assert G == 4
    tm = min(512, M)
    tn = _pick_tile(n, 1536)
    nps = n // tn
    half = 2 * nps

    def body(h_ref, wg_ref, wu_ref, a_ref, g_ref, u_ref):
        hv = h_ref[...]
        g = _bdot(hv, wg_ref[...])
        u = _bdot(hv, wu_ref[...])
        a_ref[...] = (jax.nn.silu(g) * u).astype(BF16)
        g_ref[...] = g.astype(BF16)
        u_ref[...] = u.astype(BF16)

    out = pl.BlockSpec((tm, tn), lambda j, i: (i, j))
    return pl.pallas_call(
        body,
        grid=(half, M // tm),
        in_specs=[
            pl.BlockSpec((tm, K), lambda j, i: (i, 0)),
            pl.BlockSpec((None, K, tn), lambda j, i: (j // nps, 0, j % nps)),
            pl.BlockSpec((None, K, tn), lambda j, i: (2 + j // nps, 0, j % nps)),
        ],
        out_specs=[out, out, out],
        out_shape=[SDS((M, 2 * n), BF16)] * 3,
        name=name,
        compiler_params=_cparams("parallel", "parallel"),
    )(h, w, w)


def _swiglu_bwd(g, u, da, name):
    B, S, F = g.shape

    def body(g_ref, u_ref, da_ref, du_ref):
        _, vjp = jax.vjp(lambda gv, uv: jax.nn.silu(gv) * uv, g_ref[...].astype(F32), u_ref[...].astype(F32))
        dg, du = vjp(da_ref[...])
        du_ref[:, :F] = dg.astype(BF16)
        du_ref[:, F:] = du.astype(BF16)

    return _rows_call(body, name, B, S, [(g, "row"), (u, "row"), (da, "row")], [(2 * F, BF16, "row")])[0]


def _rms_fn(x, g):
    return x * lax.rsqrt(jnp.mean(jnp.square(x), -1, keepdims=True) + RMS_EPS) * g


def _mla_mid_fwd(proj, gq, gkv, name):
    B, S, _ = proj.shape

    def body(p_ref, gq_ref, gkv_ref, qn_ref, kvn_ref):
        p = p_ref[...]
        qn_ref[...] = _rms_fn(p[:, :Q_LORA], gq_ref[...]).astype(BF16)
        kvn_ref[...] = _rms_fn(p[:, Q_LORA : Q_LORA + KV_LORA], gkv_ref[...]).astype(BF16)

    return _rows_call(
        body, name, B, S, [(proj, "row"), (gq, "par"), (gkv, "par")], [(Q_LORA, BF16, "row"), (KV_LORA, BF16, "row")]
    )


def _mla_mid_bwd(proj, dqn, dkvn, dkr, gq, gkv, name):
    B, S, W = proj.shape

    def body(p_ref, dqn_ref, dkvn_ref, dkr_ref, gq_ref, gkv_ref, dp_ref, dgq_ref, dgkv_ref):
        p = p_ref[...]
        _, vq = jax.vjp(_rms_fn, p[:, :Q_LORA], gq_ref[...])
        dql, dgq = vq(dqn_ref[...])
        _, vkv = jax.vjp(_rms_fn, p[:, Q_LORA : Q_LORA + KV_LORA], gkv_ref[...])
        dkvl, dgkv = vkv(dkvn_ref[...])
        dp_ref[:, :Q_LORA] = dql.astype(BF16)
        dp_ref[:, Q_LORA : Q_LORA + KV_LORA] = dkvl.astype(BF16)
        dp_ref[:, Q_LORA + KV_LORA :] = dkr_ref[...].astype(BF16)
        _acc(dgq_ref, dgq)
        _acc(dgkv_ref, dgkv)

    return _rows_call(
        body,
        name,
        B,
        S,
        [(proj, "row"), (dqn, "row"), (dkvn, "row"), (dkr, "row"), (gq, "par"), (gkv, "par")],
        [(W, BF16, "row"), (Q_LORA, F32, "acc"), (KV_LORA, F32, "acc")],
    )


def _rope(x, cos, sin):
    h = QK_ROPE // 2
    x1, x2 = x[:, :h], x[:, h:]
    return jnp.concatenate([x1 * cos - x2 * sin, x1 * sin + x2 * cos], axis=1)


def _rope_t(dy, cos, sin):
    h = QK_ROPE // 2
    d1, d2 = dy[:, :h], dy[:, h:]
    return jnp.concatenate([d1 * cos + d2 * sin, d2 * cos - d1 * sin], axis=1)


def _heads_call(body, name, B, S, ins, outs, ts=256):
    ts = min(ts, S)
    in_specs, args = [], []
    for arr, kind in ins:
        if kind == "row":
            in_specs.append(pl.BlockSpec((None, ts, arr.shape[-1]), lambda b, s: (b, s, 0)))
        else:
            in_specs.append(pl.BlockSpec((arr.shape[0], None, ts, arr.shape[-1]), lambda b, s: (0, b, s, 0)))
        args.append(arr)
    out_specs, out_shape = [], []
    for shape, dt, kind in outs:
        if kind == "row":
            out_specs.append(pl.BlockSpec((None, ts, shape[-1]), lambda b, s: (b, s, 0)))
        else:
            out_specs.append(pl.BlockSpec((shape[0], None, ts, shape[-1]), lambda b, s: (0, b, s, 0)))
        out_shape.append(SDS(shape, dt))
    return pl.pallas_call(
        body,
        grid=(B, S // ts),
        in_specs=in_specs,
        out_specs=out_specs,
        out_shape=out_shape,
        name=name,
        compiler_params=_cparams("parallel", "parallel"),
    )(*args)


def _mla_prep_fwd(q, kv, proj, cos, sin, name):
    B, S, _ = q.shape
    H = MLA_HEADS

    def body(q_ref, kv_ref, p_ref, cos_ref, sin_ref, qh_ref, kh_ref, vh_ref):
        cos_v, sin_v = cos_ref[...], sin_ref[...]
        kr = _rope(p_ref[:, Q_LORA + KV_LORA :], cos_v, sin_v).astype(BF16)
        for h in range(H):
            qn = q_ref[:, h * QK_DIM : h * QK_DIM + QK_NOPE]
            qr = _rope(q_ref[:, h * QK_DIM + QK_NOPE : (h + 1) * QK_DIM], cos_v, sin_v)
            qh_ref[h] = jnp.concatenate([qn, qr], axis=1).astype(BF16)
            kn = kv_ref[:, h * 128 : h * 128 + QK_NOPE].astype(BF16)
            kh_ref[h] = jnp.concatenate([kn, kr], axis=1)
            vh_ref[h] = kv_ref[:, h * 128 + QK_NOPE : (h + 1) * 128].astype(BF16)

    return _heads_call(
        body,
        name,
        B,
        S,
        [(q, "row"), (kv, "row"), (proj, "row"), (cos, "row"), (sin, "row")],
        [((H, B, S, QK_DIM), BF16, "heads"), ((H, B, S, QK_DIM), BF16, "heads"), ((H, B, S, V_HEAD), BF16, "heads")],
    )


def _mla_prep_bwd(dqh, dkh, dvh, cos, sin, name):
    H, B, S, _ = dqh.shape

    def body(dqh_ref, dkh_ref, dvh_ref, cos_ref, sin_ref, dq_ref, dkv_ref, dkr_ref):
        cos_v, sin_v = cos_ref[...], sin_ref[...]
        dkr = jnp.zeros((cos_v.shape[0], QK_ROPE), F32)
        for h in range(H):
            dqv = dqh_ref[h].astype(F32)
            dq_ref[:, h * QK_DIM : h * QK_DIM + QK_NOPE] = dqv[:, :QK_NOPE].astype(BF16)
            dq_ref[:, h * QK_DIM + QK_NOPE : (h + 1) * QK_DIM] = _rope_t(dqv[:, QK_NOPE:], cos_v, sin_v).astype(BF16)
            dkv = dkh_ref[h].astype(F32)
            dkv_ref[:, h * 128 : h * 128 + QK_NOPE] = dkv[:, :QK_NOPE].astype(BF16)
            dkv_ref[:, h * 128 + QK_NOPE : (h + 1) * 128] = dvh_ref[h]
            dkr = dkr + dkv[:, QK_NOPE:]
        dkr_ref[...] = _rope_t(dkr, cos_v, sin_v)

    return _heads_call(
        body,
        name,
        B,
        S,
        [(dqh, "heads"), (dkh, "heads"), (dvh, "heads"), (cos, "row"), (sin, "row")],
        [((B, S, H * QK_DIM), BF16, "row"), ((B, S, H * 128), BF16, "row"), ((B, S, QK_ROPE), F32, "row")],
    )


LOG2E = 1.4426950408889634
ATTN_TILE = 1024
ATTN_DIAG_SUB = 256


def _attn_fwd(qh, kh, vh, name):
    H, B, S, _ = qh.shape
    t = min(ATTN_TILE, S)
    scale = QK_DIM**-0.5
    c2 = scale * LOG2E

    def body(q_ref, k_ref, v_ref, o_ref, lse_ref):
        i = pl.program_id(2)
        qs = [q_ref[0], q_ref[1]]

        def update(state, q, k, v, mask):
            m, l, acc = state
            s = _bdot_nt(q, k)
            if mask is not None:
                s = jnp.where(mask, s, NEG)
            m_new = jnp.maximum(m, jnp.max(s, axis=1, keepdims=True))
            p = jnp.exp2((s - m_new) * c2)
            a = jnp.exp2((m - m_new) * c2)
            return m_new, a * l + jnp.sum(p, axis=1, keepdims=True), a * acc + _bdot(p, v)

        def step(j, carry):
            rows = pl.ds(pl.multiple_of(j * t, t), t)
            return tuple(update(carry[hh], qs[hh], k_ref[hh, rows, :], v_ref[hh, rows, :], None) for hh in range(2))

        one = (jnp.full((t, 1), NEG, F32), jnp.zeros((t, 1), F32), jnp.zeros((t, V_HEAD), F32))
        carry = lax.fori_loop(0, i, step, (one, one))
        rows = pl.ds(pl.multiple_of(i * t, t), t)
        causal = lax.broadcasted_iota(jnp.int32, (t, t), 0) >= lax.broadcasted_iota(jnp.int32, (t, t), 1)
        carry = tuple(update(carry[hh], qs[hh], k_ref[hh, rows, :], v_ref[hh, rows, :], causal) for hh in range(2))
        outs = []
        for hh in range(2):
            m, l, acc = carry[hh]
            outs.append(acc / l)
            lse_ref[hh] = m * scale + jnp.log(l)
        o_ref[...] = jnp.concatenate(outs, axis=1).astype(BF16)

    return pl.pallas_call(
        body,
        grid=(B, H // 2, S // t),
        in_specs=[
            pl.BlockSpec((2, None, t, QK_DIM), lambda b, p, i: (p, b, i, 0)),
            pl.BlockSpec((2, None, S, QK_DIM), lambda b, p, i: (p, b, 0, 0)),
            pl.BlockSpec((2, None, S, V_HEAD), lambda b, p, i: (p, b, 0, 0)),
        ],
        out_specs=[
            pl.BlockSpec((None, t, 2 * V_HEAD), lambda b, p, i: (b, i, p)),
            pl.BlockSpec((2, None, t, 1), lambda b, p, i: (p, b, i, 0)),
        ],
        out_shape=[SDS((B, S, H * V_HEAD), BF16), SDS((H, B, S, 1), F32)],
        name=name,
        compiler_params=_cparams("parallel", "parallel", "arbitrary"),
    )(qh, kh, vh)


def _attn_bwd(qh, kh, vh, o, do, lse, name):
    H, B, S, _ = qh.shape
    t = min(ATTN_TILE, S)
    sub = min(ATTN_DIAG_SUB, t)
    nq = S // t
    scale = QK_DIM**-0.5
    c2 = scale * LOG2E

    def body(q_ref, k_ref, v_ref, o_ref, do_ref, lse_ref, dq_ref, dk_ref, dv_ref, dq_acc, delta_ref, lse2_ref):
        prod = o_ref[...].astype(F32) * do_ref[...].astype(F32)
        for hh in range(2):
            delta_ref[hh] = jnp.sum(prod[:, hh * V_HEAD : (hh + 1) * V_HEAD], axis=1, keepdims=True)
            lse2_ref[hh] = lse_ref[hh] * LOG2E
        dq_acc[...] = jnp.zeros_like(dq_acc)

        def kloop(j, _):
            krows = pl.ds(pl.multiple_of(j * t, t), t)
            ks = [k_ref[0, krows, :], k_ref[1, krows, :]]
            vs = [v_ref[0, krows, :], v_ref[1, krows, :]]

            def pair(hh, qrows, k, v, mask):
                q = q_ref[hh, qrows, :]
                do_h = do_ref[qrows, :][:, hh * V_HEAD : (hh + 1) * V_HEAD]
                p = jnp.exp2(_bdot_nt(q, k) * c2 - lse2_ref[hh, qrows, :])
                if mask is not None:
                    p = jnp.where(mask, p, 0.0)
                dv = _bdot_tn(p, do_h)
                ds = (p * (_bdot_nt(do_h, v) - delta_ref[hh, qrows, :])).astype(BF16)
                dq_acc[hh, qrows, :] += _bdot(ds, k)
                return _bdot_tn(ds, q), dv

            def qstep(i, carry):
                qrows = pl.ds(pl.multiple_of(i * t, t), t)
                out = []
                for hh in range(2):
                    dk, dv = pair(hh, qrows, ks[hh], vs[hh], None)
                    out.append((carry[hh][0] + dk, carry[hh][1] + dv))
                return tuple(out)

            def diagonal_step():
                out = []
                for hh in range(2):
                    dks, dvs = [], []
                    for c in range(t // sub):
                        r0 = c * sub
                        qrows = pl.ds(pl.multiple_of(j * t + r0, sub), t - r0)
                        mask = (lax.broadcasted_iota(jnp.int32, (t - r0, sub), 0)
                                >= lax.broadcasted_iota(jnp.int32, (t - r0, sub), 1))
                        dk, dv = pair(hh, qrows, ks[hh][r0 : r0 + sub], vs[hh][r0 : r0 + sub], mask)
                        dks.append(dk)
                        dvs.append(dv)
                    out.append((jnp.concatenate(dks, axis=0), jnp.concatenate(dvs, axis=0)))
                return tuple(out)

            carry = lax.fori_loop(j + 1, nq, qstep, diagonal_step())
            for hh in range(2):
                dk_ref[hh, krows, :] = (carry[hh][0] * scale).astype(BF16)
                dv_ref[hh, krows, :] = carry[hh][1].astype(BF16)
            return 0

        lax.fori_loop(0, nq, kloop, 0)
        dq_ref[...] = (dq_acc[...] * scale).astype(BF16)

    hspec = lambda w: pl.BlockSpec((2, None, S, w), lambda b, p: (p, b, 0, 0))
    ospec = pl.BlockSpec((None, S, 2 * V_HEAD), lambda b, p: (b, 0, p))
    return pl.pallas_call(
        body,
        grid=(B, H // 2),
        in_specs=[hspec(QK_DIM), hspec(QK_DIM), hspec(V_HEAD), ospec, ospec, hspec(1)],
        out_specs=[hspec(QK_DIM), hspec(QK_DIM), hspec(V_HEAD)],
        out_shape=[SDS((H, B, S, QK_DIM), BF16), SDS((H, B, S, QK_DIM), BF16), SDS((H, B, S, V_HEAD), BF16)],
        scratch_shapes=[pltpu.VMEM((2, S, QK_DIM), F32), pltpu.VMEM((2, S, 1), F32), pltpu.VMEM((2, S, 1), F32)],
        name=name,
        compiler_params=_cparams("parallel", "parallel"),
    )(qh, kh, vh, o, do, lse)


def _hgrn_pre(q, fx, lb):
    sig = jax.nn.sigmoid(fx)
    f = lb + (1.0 - lb) * sig
    return jax.nn.silu(q), 1.0 - f, jnp.log(f)


def _hgrn_gate(o, gg, gn):
    return _rms_fn(o, gn) * jax.nn.silu(gg)


def _tri(n, lower):
    r = lax.broadcasted_iota(jnp.int32, (n, n), 0)
    c = lax.broadcasted_iota(jnp.int32, (n, n), 1)
    return ((r >= c) if lower else (r <= c)).astype(F32)


def _hgrn_intra_fwd(qs, k, v, b):
    C, SB = qs.shape[0], min(HGRN_SUB, qs.shape[0])
    ridx = lax.broadcasted_iota(jnp.int32, (SUBLANES, 1), 0)
    outs = []
    for i in range(C // SB):
        r0 = i * SB
        qi, ki, vi, bi = qs[r0 : r0 + SB], k[r0 : r0 + SB], v[r0 : r0 + SB], b[r0 : r0 + SB]
        ng = SB // SUBLANES
        qg = [qi[g * SUBLANES : (g + 1) * SUBLANES] for g in range(ng)]
        bg = [bi[g * SUBLANES : (g + 1) * SUBLANES] for g in range(ng)]
        accg = [jnp.zeros((SUBLANES, v.shape[1]), F32) for _ in range(ng)]
        for s in range(SB):
            gs, so = divmod(s, SUBLANES)
            k_s, v_s, b_s = ki[s : s + 1], vi[s : s + 1], bi[s : s + 1]
            for tg in range(gs, ng):
                if tg == gs:
                    mask = ridx >= so
                    w = jnp.where(mask, qg[tg] * k_s * jnp.exp(jnp.where(mask, bg[tg] - b_s, 0.0)), 0.0)
                else:
                    w = qg[tg] * k_s * jnp.exp(bg[tg] - b_s)
                accg[tg] = accg[tg] + jnp.sum(w, axis=1, keepdims=True) * v_s
        acc = jnp.concatenate(accg, axis=0)
        if i > 0:
            ref = bi[0:1]
            qt = qi * jnp.exp(bi - ref)
            kt = k[:r0] * jnp.exp(ref - b[:r0])
            acc = acc + _bdot(_mdot_nt(qt, kt), v[:r0])
        outs.append(acc)
    return jnp.concatenate(outs, axis=0)


def _hgrn_intra_bwd(qs, k, v, b, do):
    C, SB = qs.shape[0], min(HGRN_SUB, qs.shape[0])
    nb = C // SB
    ridx = lax.broadcasted_iota(jnp.int32, (SUBLANES, 1), 0)
    dq_p = [None] * nb
    dk_p = [jnp.zeros((SB, k.shape[1]), F32) for _ in range(nb)]
    dv_p = [jnp.zeros((SB, v.shape[1]), F32) for _ in range(nb)]
    for i in range(nb):
        r0 = i * SB
        qi, ki, vi, bi, doi = qs[r0 : r0 + SB], k[r0 : r0 + SB], v[r0 : r0 + SB], b[r0 : r0 + SB], do[r0 : r0 + SB]
        ng = SB // SUBLANES
        qg = [qi[g * SUBLANES : (g + 1) * SUBLANES] for g in range(ng)]
        bg = [bi[g * SUBLANES : (g + 1) * SUBLANES] for g in range(ng)]
        dog = [doi[g * SUBLANES : (g + 1) * SUBLANES] for g in range(ng)]
        dqg =[jnp.zeros((SUBLANES, k.shape[1]), F32) for _ in range(ng)]
        dkg = [jnp.zeros((SUBLANES, k.shape[1]), F32) for _ in range(ng)]
        dvg = [jnp.zeros((SUBLANES, v.shape[1]), F32) for _ in range(ng)]
        for s in range(SB):
            gs, so = divmod(s, SUBLANES)
            k_s, v_s, b_s = ki[s : s + 1], vi[s : s + 1], bi[s : s + 1]
            dk_s = jnp.zeros((SUBLANES, k.shape[1]), F32)
            dv_s = jnp.zeros((SUBLANES, v.shape[1]), F32)
            for tg in range(gs, ng):
                if tg == gs:
                    mask = ridx >= so
                    e = jnp.where(mask, jnp.exp(jnp.where(mask, bg[tg] - b_s, 0.0)), 0.0)
                else:
                    e = jnp.exp(bg[tg] - b_s)
                da = jnp.sum(dog[tg] * v_s, axis=1, keepdims=True)
                qe = qg[tg] * e
                a = jnp.sum(qe * k_s, axis=1, keepdims=True)
                dqg[tg] = dqg[tg] + da * (k_s * e)
                dk_s = dk_s + da * qe
                dv_s = dv_s + a * dog[tg]
            dkg[gs] = jnp.where(ridx == so, dkg[gs] + jnp.sum(dk_s, axis=0, keepdims=True), dkg[gs])
            dvg[gs] = jnp.where(ridx == so, dvg[gs] + jnp.sum(dv_s, axis=0, keepdims=True), dvg[gs])
        dqi = jnp.concatenate(dqg, axis=0)
        dki = jnp.concatenate(dkg, axis=0)
        dvi = jnp.concatenate(dvg, axis=0)
        if i > 0:
            ref = bi[0:1]
            eq = jnp.exp(bi - ref)
            ek = jnp.exp(ref - b[:r0])
            qt = qi * eq
            kt = k[:r0] * ek
            A = _mdot_nt(qt, kt)
            dA = _bdot_nt(doi, v[:r0])
            dvl = _bdot_tn(A, doi)
            dqi = dqi + _mdot(dA, kt) * eq
            dkl = _mdot_tn(dA, qt) * ek
            for j in range(i):
                dk_p[j] = dk_p[j] + dkl[j * SB : (j + 1) * SB]
                dv_p[j] = dv_p[j] + dvl[j * SB : (j + 1) * SB]
        dq_p[i] = dqi
        dk_p[i] = dk_p[i] + dki
        dv_p[i] = dv_p[i] + dvi
    return jnp.concatenate(dq_p, axis=0), jnp.concatenate(dk_p, axis=0), jnp.concatenate(dv_p, axis=0)


def _hgrn_fwd(proj, lb, gn, name):
    B, S, W = proj.shape
    HK = W // 4
    H = HK // HGRN_K
    C = min(HGRN_CHUNK, S)
    N = S // C

    HP = HGRN_PAR if H % HGRN_PAR == 0 else 1
    WP = HP * HGRN_K

    def body(q_ref, f_ref, i_ref, g_ref, lb_ref, gn_ref, og_ref, o_ref, st_ref):
        gn_v = gn_ref[...]
        tril = _tri(C, True)

        def chunk(n, sts):
            rows = pl.ds(pl.multiple_of(n * C, C), C)
            out = []
            for hh in range(HP):
                ln = slice(hh * HGRN_K, (hh + 1) * HGRN_K)
                st = sts[hh]
                qs, k, g = _hgrn_pre(q_ref[rows, ln], f_ref[rows, ln], lb_ref[:, ln])
                v = i_ref[rows, ln]
                b = _hdot(tril, g)
                st_ref[hh, n] = st
                o = _hgrn_intra_fwd(qs, k, v, b) + _bdot_nt(qs * jnp.exp(b), st)
                bl = b[C - 1 : C]
                out.append(st * jnp.exp(bl) + _bdot_tn(v, k * jnp.exp(bl - b)))
                o_ref[rows, ln] = o
                og_ref[rows, ln] = _hgrn_gate(o, g_ref[rows, ln], gn_v).astype(BF16)
            return tuple(out)

        lax.fori_loop(0, N, chunk, tuple(jnp.zeros((HGRN_K, HGRN_K), F32) for _ in range(HP)))

    col = lambda part: pl.BlockSpec((None, S, WP), lambda b, h: (b, 0, part * (H // HP) + h))
    return pl.pallas_call(
        body,
        grid=(B, H // HP),
        in_specs=[col(0), col(1), col(2), col(3), pl.BlockSpec((1, WP), lambda b, h: (0, h)), pl.BlockSpec((1, HGRN_K), lambda b, h: (0, 0))],
        out_specs=[col(0), col(0), pl.BlockSpec((None, HP, N, HGRN_K, HGRN_K), lambda b, h: (b, h, 0, 0, 0))],
        out_shape=[SDS((B, S, HK), BF16), SDS((B, S, HK), F32), SDS((B, H, N, HGRN_K, HGRN_K), F32)],
        name=name,
        compiler_params=_cparams("parallel", "parallel"),
    )(proj, proj, proj, proj, lb, gn)


def _hgrn_bwd(proj, lb, gn, o_pre, states, dog, name):
    B, S, W = proj.shape
    HK = W // 4
    H = HK // HGRN_K
    C = min(HGRN_CHUNK, S)
    N = S // C

    HP = HGRN_PAR if H % HGRN_PAR == 0 else 1
    WP = HP * HGRN_K

    def body(q_ref, f_ref, i_ref, g_ref, lb_ref, gn_ref, o_ref, st_ref, dog_ref, dq_ref, df_ref, di_ref, dg_ref, dlb_ref, dgn_ref):
        gn_v = gn_ref[...]
        tril = _tri(C, True)
        triu = _tri(C, False)

        def chunk(idx, carry):
            n = N - 1 - idx
            rows = pl.ds(pl.multiple_of(n * C, C), C)
            out = []
            for hh in range(HP):
                ln = slice(hh * HGRN_K, (hh + 1) * HGRN_K)
                dst, dlb, dgn = carry[hh]
                (qs, k, g), pre_vjp = jax.vjp(_hgrn_pre, q_ref[rows, ln], f_ref[rows, ln], lb_ref[:, ln])
                v = i_ref[rows, ln]
                _, gate_vjp = jax.vjp(_hgrn_gate, o_ref[rows, ln], g_ref[rows, ln], gn_v)
                do, dgg, dgn_c = gate_vjp(dog_ref[rows, ln])
                b = _hdot(tril, g)
                st0 = st_ref[hh, n]
                eb = jnp.exp(b)
                bl = b[C - 1 : C]
                ebl = jnp.exp(bl)
                ekb = jnp.exp(bl - b)
                qe = qs * eb
                kt = k * ekb
                dqs, dk, dv = _hgrn_intra_bwd(qs, k, v, b, do)
                dqs = dqs + _bdot(do, st0) * eb
                dk = dk + _bdot(v, dst) * ekb
                dv = dv + _bdot_nt(kt, dst)
                st1 = st0 * ebl + _bdot_tn(v, kt)
                dbl = jnp.sum(st1 * dst, axis=0, keepdims=True)
                dst = dst * ebl + _bdot_tn(do, qe)
                dgl = _hdot(triu, qs * dqs - k * dk) + dbl
                dq_pre, dfx, dlb_c = pre_vjp((dqs, dk, dgl))
                dq_ref[rows, ln] = dq_pre.astype(BF16)
                df_ref[rows, ln] = dfx.astype(BF16)
                di_ref[rows, ln] = dv.astype(BF16)
                dg_ref[rows, ln] = dgg.astype(BF16)
                out.append((dst, dlb + dlb_c, dgn + dgn_c))
            return tuple(out)

        zero = jnp.zeros((1, HGRN_K), F32)
        one = (jnp.zeros((HGRN_K, HGRN_K), F32), zero, zero)
        res = lax.fori_loop(0, N, chunk, tuple(one for _ in range(HP)))
        for hh in range(HP):
            dlb_ref[hh] = res[hh][1]
            dgn_ref[hh] = res[hh][2]

    col = lambda part: pl.BlockSpec((None, S, WP), lambda b, h: (b, 0, part * (H // HP) + h))
    vec = pl.BlockSpec((None, HP, 1, HGRN_K), lambda b, h: (b, h, 0, 0))
    return pl.pallas_call(
        body,
        grid=(B, H // HP),
        in_specs=[
            col(0), col(1), col(2), col(3),
            pl.BlockSpec((1, WP), lambda b, h: (0, h)),
            pl.BlockSpec((1, HGRN_K), lambda b, h: (0, 0)),
            col(0),
            pl.BlockSpec((None, HP, N, HGRN_K, HGRN_K), lambda b, h: (b, h, 0, 0, 0)),
            col(0),
        ],
        out_specs=[col(0), col(0), col(0), col(0), vec, vec],
        out_shape=[SDS((B, S, HK), BF16)] * 4 + [SDS((B, H, 1, HGRN_K), F32)] * 2,
        name=name,
        compiler_params=_cparams("parallel", "parallel"),
    )(proj, proj, proj, proj, lb, gn, o_pre, states, dog)


def _ada_fwd(c_all, w, b, name):
    Bg, D = c_all.shape
    L, _, n = w.shape

    def body(c_ref, w_ref, b_ref, o_ref):
        o_ref[...] = _bdot(jax.nn.silu(c_ref[...]), w_ref[...]) + b_ref[...]

    return pl.pallas_call(
        body,
        grid=(L,),
        in_specs=[
            pl.BlockSpec((Bg, D), lambda l: (0, 0)),
            pl.BlockSpec((None, D, n), lambda l: (l, 0, 0)),
            pl.BlockSpec((None, 1, n), lambda l: (l, 0, 0)),
        ],
        out_specs=pl.BlockSpec((None, Bg, n), lambda l: (l, 0, 0)),
        out_shape=SDS((L, Bg, n), F32),
        name=name,
        compiler_params=_cparams("parallel"),
    )(c_all, w, b)


def _ada_bwd(c_all, dmod, name):
    Bg, D = c_all.shape
    L, _, n = dmod.shape

    def body(c_ref, d_ref, dw_ref, db_ref):
        d = d_ref[...]
        dw_ref[...] = _bdot_tn(jax.nn.silu(c_ref[...]), d)
        db_ref[...] = jnp.sum(d, axis=0, keepdims=True)

    return pl.pallas_call(
        body,
        grid=(L,),
        in_specs=[pl.BlockSpec((Bg, D), lambda l: (0, 0)), pl.BlockSpec((None, Bg, n), lambda l: (l, 0, 0))],
        out_specs=[pl.BlockSpec((None, D, n), lambda l: (l, 0, 0)), pl.BlockSpec((None, 1, n), lambda l: (l, 0, 0))],
        out_shape=[SDS((L, D, n), F32), SDS((L, 1, n), F32)],
        name=name,
        compiler_params=_cparams("parallel"),
    )(c_all, dmod)


def _adamw(w, gs, m, v, name):
    shape = w.shape
    cols = shape[-1]
    rows = w.size // cols
    tr = rows
    for cand in (512, 256, 128, 64, 32, 16, 8):
        if rows % cand == 0 and cand * cols * 4 <= 2 * 1024 * 1024:
            tr = cand
            break
    as2d = lambda a: a.reshape(rows, cols)
    ng = len(gs)
    c1 = 1.0 / (1.0 - ADAM_B1**ADAM_STEP)
    c2 = 1.0 / (1.0 - ADAM_B2**ADAM_STEP)

    def body(*refs):
        w_ref, m_ref, v_ref = refs[0], refs[1], refs[2]
        g_refs = refs[3 : 3 + ng]
        g_out, d_out, m_out, v_out = refs[3 + ng :]
        g = g_refs[0][...].astype(F32)
        for r in g_refs[1:]:
            g = g + r[...].astype(F32)
        m_new = ADAM_B1 * m_ref[...] + (1.0 - ADAM_B1) * g
        v_new = ADAM_B2 * v_ref[...] + (1.0 - ADAM_B2) * jnp.square(g)
        g_out[...] = g
        m_out[...] = m_new
        v_out[...] = v_new
        d_out[...] = -ADAM_LR * ((m_new * c1) / (jnp.sqrt(v_new * c2) + ADAM_EPS) + ADAM_WD * w_ref[...])

    spec = pl.BlockSpec((tr, cols), lambda i: (i, 0))
    outs = pl.pallas_call(
        body,
        grid=(rows // tr,),
        in_specs=[spec] * (3 + ng),
        out_specs=[spec] * 4,
        out_shape=[SDS((rows, cols), F32)] * 4,
        name=name,
        compiler_params=_cparams("parallel"),
    )(as2d(w), as2d(m), as2d(v), *[as2d(g) for g in gs])
    return tuple(o.reshape(shape) for o in outs)


def _sum4(own, recv, name):
    shape = own.shape
    cols = shape[-1]
    rows = own.size // cols
    tr = rows
    for cand in (512, 256, 128, 64, 32, 16):
        if rows % cand == 0 and cand * cols * 4 <= 2 * 1024 * 1024:
            tr = cand
            break

    def body(own_ref, recv_ref, o_ref):
        acc = own_ref[...].astype(F32)
        for r in range(3):
            acc = acc + recv_ref[r].astype(F32)
        o_ref[...] = acc

    out = pl.pallas_call(
        body,
        grid=(rows // tr,),
        in_specs=[pl.BlockSpec((tr, cols), lambda i: (i, 0)), pl.BlockSpec((3, tr, cols), lambda i: (0, i, 0))],
        out_specs=pl.BlockSpec((tr, cols), lambda i: (i, 0)),
        out_shape=SDS((rows, cols), F32),
        name=name,
        compiler_params=_cparams("parallel"),
    )(own.reshape(rows, cols), recv.reshape(3, rows, cols))
    return out.reshape(shape)


def _my_place():
    return lax.axis_index("x"), lax.axis_index("y"), lax.axis_index("c")


def _flip(v, bit):
    return 1 - v if bit else v


def _allgather8(x, name):
    r, n = x.shape

    def body(x_ref, o_ref, send_sems, recv_sems, local_sem):
        mx, my, mc = _my_place()
        me = 4 * mx + 2 * my + mc
        mine = pltpu.make_async_copy(x_ref, o_ref.at[me], local_sem)
        mine.start()
        sends = []
        for rel in range(1, 8):
            peer = (_flip(mx, rel & 4), _flip(my, rel & 2), _flip(mc, rel & 1))
            cp = pltpu.make_async_remote_copy(
                src_ref=x_ref, dst_ref=o_ref.at[me], send_sem=send_sems.at[rel - 1], recv_sem=recv_sems.at[rel - 1],
                device_id=peer, device_id_type=MESH,
            )
            cp.start()
            sends.append(cp)
        for rel in range(1, 8):
            px, py, pc = _flip(mx, rel & 4), _flip(my, rel & 2), _flip(mc, rel & 1)
            pltpu.make_async_remote_copy(
                src_ref=x_ref, dst_ref=o_ref.at[4 * px + 2 * py + pc], send_sem=send_sems.at[rel - 1],
                recv_sem=recv_sems.at[rel - 1], device_id=(px, py, pc), device_id_type=MESH,
            ).wait_recv()
        for cp in sends:
            cp.wait_send()
        mine.wait()

    return pl.pallas_call(
        body,
        out_shape=SDS((8, r, n), x.dtype),
        in_specs=[pl.BlockSpec(memory_space=pl.ANY)],
        out_specs=pl.BlockSpec(memory_space=pl.ANY),
        scratch_shapes=[pltpu.SemaphoreType.DMA((7,)), pltpu.SemaphoreType.DMA((7,)), pltpu.SemaphoreType.DMA],
        name=name,
    )(x)


_HBM = pl.BlockSpec(memory_space=pl.ANY)


_SEM = pl.BlockSpec(memory_space=pltpu.SEMAPHORE)
_HBM_ONLY = pl.BlockSpec(memory_space=pltpu.HBM)
_EFFECT = pltpu.SideEffectType.DATAFLOW_SIDE_EFFECTING


def _in_hbm(a):
    return pltpu.with_memory_space_constraint(a, pltpu.HBM)


def _gather_start(lands, after, name):
    n = len(lands)

    def body(*refs):
        land = refs[:n]
        send_sems, recv_sems = refs[n + 1], refs[n + 2]
        token = refs[-1]
        mx, my, mc = _my_place()
        for i in range(n):
            for rel in range(1, 4):
                pltpu.make_async_remote_copy(
                    src_ref=land[i].at[2 * mx + my], dst_ref=land[i].at[2 * mx + my],
                    send_sem=send_sems.at[3 * i + rel - 1], recv_sem=recv_sems.at[3 * i + rel - 1],
                    device_id=(_flip(mx, rel & 2), _flip(my, rel & 1), mc), device_id_type=MESH,
                ).start()
        token[...] = jnp.zeros_like(token)

    outs = pl.pallas_call(
        body,
        name=name,
        out_shape=(
            pltpu.SemaphoreType.DMA((3 * n,)), pltpu.SemaphoreType.DMA((3 * n,)),
            *[pltpu.HBM(a.shape, a.dtype) for a in lands], SDS((8, LANES), F32),
        ),
        in_specs=[_HBM_ONLY] * n + [_HBM],
        out_specs=(_SEM, _SEM, *[_HBM_ONLY] * n, pl.BlockSpec(memory_space=pltpu.VMEM)),
        input_output_aliases={i: 2 + i for i in range(n)},
        compiler_params=pltpu.CompilerParams(has_side_effects=_EFFECT),
    )(*[_in_hbm(a) for a in lands], after)
    return outs[0], outs[1], list(outs[2 : 2 + n]), outs[-1]


def _gather_wait(send_sems, recv_sems, lands, after, name):
    n = len(lands)

    def body(*refs):
        land = refs[:n]
        s_sems, r_sems = refs[n], refs[n + 1]
        mx, my, mc = _my_place()
        for i in range(n):
            for rel in range(1, 4):
                px, py = _flip(mx, rel & 2), _flip(my, rel & 1)
                cp = pltpu.make_async_remote_copy(
                    src_ref=land[i].at[2 * mx + my], dst_ref=land[i].at[2 * px + py],
                    send_sem=s_sems.at[3 * i + rel - 1], recv_sem=r_sems.at[3 * i + rel - 1],
                    device_id=(px, py, mc), device_id_type=MESH,
                )
                cp.wait_send()
                cp.wait_recv()

    outs = pl.pallas_call(
        body,
        name=name,
        out_shape=tuple(pltpu.HBM(a.shape, a.dtype) for a in lands),
        in_specs=[_HBM_ONLY] * n + [_SEM, _SEM, _HBM],
        out_specs=[_HBM_ONLY] * n,
        input_output_aliases={i: i for i in range(n)},
        compiler_params=pltpu.CompilerParams(has_side_effects=_EFFECT),
    )(*lands, send_sems, recv_sems, after)
    return list(outs)


def _scatter_start(slabs, lands, places, name):
    n = len(slabs)

    def body(*refs):
        ins, land = refs[:n], refs[n : 2 * n]
        send_sems, recv_sems = refs[2 * n], refs[2 * n + 1]
        token = refs[-1]
        mx, my, mc = _my_place()
        for i in range(n):
            for rel in range(1, 4):
                px, py = _flip(mx, rel & 2), _flip(my, rel & 1)
                pltpu.make_async_remote_copy(
                    src_ref=ins[i].at[2 * px + py], dst_ref=land[i].at[rel - 1, places[i]],
                    send_sem=send_sems.at[3 * i + rel - 1], recv_sem=recv_sems.at[3 * i + rel - 1],
                    device_id=(px, py, mc), device_id_type=MESH,
                ).start()
        token[...] = jnp.zeros_like(token)

    outs = pl.pallas_call(
        body,
        name=name,
        out_shape=(
            pltpu.SemaphoreType.DMA((3 * n,)), pltpu.SemaphoreType.DMA((3 * n,)),
            *[pltpu.HBM(a.shape, a.dtype) for a in slabs], *[pltpu.HBM(a.shape, a.dtype) for a in lands],
            SDS((8, LANES), F32),
        ),
        in_specs=[_HBM_ONLY] * (2 * n),
        out_specs=(_SEM, _SEM, *[_HBM_ONLY] * (2 * n), pl.BlockSpec(memory_space=pltpu.VMEM)),
        input_output_aliases={i: 2 + i for i in range(2 * n)},
        compiler_params=pltpu.CompilerParams(has_side_effects=_EFFECT),
    )(*[_in_hbm(a) for a in slabs], *[_in_hbm(a) for a in lands])
    return outs[0], outs[1], list(outs[2 : 2 + n]), list(outs[2 + n : 2 + 2 * n]), outs[-1]


def _scatter_wait(send_sems, recv_sems, slabs, lands, places, after, name):
    n = len(slabs)

    def body(*refs):
        ins, land = refs[:n], refs[n : 2 * n]
        s_sems, r_sems = refs[2 * n], refs[2 * n + 1]
        mx, my, mc = _my_place()
        for i in range(n):
            for rel in range(1, 4):
                px, py = _flip(mx, rel & 2), _flip(my, rel & 1)
                cp = pltpu.make_async_remote_copy(
                    src_ref=ins[i].at[2 * px + py], dst_ref=land[i].at[rel - 1, places[i]],
                    send_sem=s_sems.at[3 * i + rel - 1], recv_sem=r_sems.at[3 * i + rel - 1],
                    device_id=(px, py, mc), device_id_type=MESH,
                )
                cp.wait_send()
                cp.wait_recv()

    outs = pl.pallas_call(
        body,
        name=name,
        out_shape=(*[pltpu.HBM(a.shape, a.dtype) for a in slabs], *[pltpu.HBM(a.shape, a.dtype) for a in lands]),
        in_specs=[_HBM_ONLY] * (2 * n) + [_SEM, _SEM, _HBM],
        out_specs=[_HBM_ONLY] * (2 * n),
        input_output_aliases={i: i for i in range(2 * n)},
        compiler_params=pltpu.CompilerParams(has_side_effects=_EFFECT),
    )(*slabs, *lands, send_sems, recv_sems, after)
    return list(outs[:n]), list(outs[n:])


def _swap_start(parts, after, name):
    n = len(parts)
    lands = [lax.empty(a.shape, a.dtype) for a in parts]

    def body(*refs):
        ins, land = refs[:n], refs[n : 2 * n]
        send_sems, recv_sems = refs[2 * n + 1], refs[2 * n + 2]
        mx, my, mc = _my_place()
        for i in range(n):
            pltpu.make_async_remote_copy(
                src_ref=ins[i], dst_ref=land[i], send_sem=send_sems.at[i], recv_sem=recv_sems.at[i],
                device_id=(mx, my, 1 - mc), device_id_type=MESH,
            ).start()

    outs = pl.pallas_call(
        body,
        name=name,
        out_shape=(
            pltpu.SemaphoreType.DMA((n,)), pltpu.SemaphoreType.DMA((n,)),
            *[pltpu.HBM(a.shape, a.dtype) for a in parts], *[pltpu.HBM(a.shape, a.dtype) for a in lands],
        ),
        in_specs=[_HBM_ONLY] * (2 * n) + [_HBM],
        out_specs=(_SEM, _SEM, *[_HBM_ONLY] * (2 * n)),
        input_output_aliases={i: 2 + i for i in range(2 * n)},
        compiler_params=pltpu.CompilerParams(has_side_effects=_EFFECT),
    )(*[_in_hbm(a) for a in parts], *[_in_hbm(a) for a in lands], after)
    return outs[0], outs[1], list(outs[2 : 2 + n]), list(outs[2 + n :])


def _swap_wait(send_sems, recv_sems, parts, lands, after, name):
    n = len(parts)

    def body(*refs):
        ins, land = refs[:n], refs[n : 2 * n]
        s_sems, r_sems = refs[2 * n], refs[2 * n + 1]
        mx, my, mc = _my_place()
        for i in range(n):
            cp = pltpu.make_async_remote_copy(
                src_ref=ins[i], dst_ref=land[i], send_sem=s_sems.at[i], recv_sem=r_sems.at[i],
                device_id=(mx, my, 1 - mc), device_id_type=MESH,
            )
            cp.wait_send()
            cp.wait_recv()

    outs = pl.pallas_call(
        body,
        name=name,
        out_shape=(*[pltpu.HBM(a.shape, a.dtype) for a in parts], *[pltpu.HBM(a.shape, a.dtype) for a in lands]),
        in_specs=[_HBM_ONLY] * (2 * n) + [_SEM, _SEM, _HBM],
        out_specs=[_HBM_ONLY] * (2 * n),
        input_output_aliases={i: i for i in range(2 * n)},
        compiler_params=pltpu.CompilerParams(has_side_effects=_EFFECT),
    )(*parts, *lands, send_sems, recv_sems, after)
    return list(outs[:n]), list(outs[n:])


def _pad_rows(a, rows):
    return jnp.pad(a, ((0, rows - a.shape[0]), (0, 0)))


def kernel(x, c, positions, mla_w_in, mla_q_norm, mla_w_qb, mla_kv_norm, mla_w_kvb, mla_w_o, hgrn_lb, hgrn_w_in, hgrn_g_norm, hgrn_w_o, ffn_w_in, ffn_w_out, ada_w, ada_b, ln_g, ln_b, loss_target, m_mla_w_in, m_mla_q_norm, m_mla_w_qb, m_mla_kv_norm, m_mla_w_kvb, m_mla_w_o, m_hgrn_lb, m_hgrn_w_in, m_hgrn_g_norm, m_hgrn_w_o, m_ffn_w_in, m_ffn_w_out, m_ada_w, m_ada_b, m_ln_g, m_ln_b, v_mla_w_in, v_mla_q_norm, v_mla_w_qb, v_mla_kv_norm, v_mla_w_kvb, v_mla_w_o, v_hgrn_lb, v_hgrn_w_in, v_hgrn_g_norm, v_hgrn_w_o, v_ffn_w_in, v_ffn_w_out, v_ada_w, v_ada_b, v_ln_g, v_ln_b):
    B, S, D = x.shape
    T = B * S
    depth = ada_w.shape[0]
    n_mla, n_hgrn = mla_w_in.shape[0], hgrn_w_in.shape[0]
    n_sub = 2 * depth
    alpha = (2.0 * depth) ** 0.25
    mx, my, mc = _my_place()
    me = 4 * mx + 2 * my + mc
    k_me = 2 * mx + my
    Bg = 8 * B
    HK = hgrn_w_o.shape[1] * 4
    dq = D // 4

    lbw = hgrn_lb.shape[1]
    first = jnp.zeros((8, max(D, 4 * lbw)), F32)
    first = first.at[:B, :D].set(c).at[B : B + n_hgrn, :lbw].set(hgrn_lb)
    first_all = _allgather8(first, "gather_cond")
    c_all = first_all[:, :B, :D].reshape(Bg, D)
    lb_logits = jnp.concatenate([first_all[2 * k, B : B + n_hgrn, :lbw] for k in range(4)], axis=1)

    def lower_bounds_fn(logits):
        soft = jax.nn.softmax(logits, axis=0)
        return jnp.cumsum(soft, axis=0) - soft[0]

    lower_bounds, lower_bounds_vjp = jax.vjp(lower_bounds_fn, lb_logits)

    n_ada = ada_w.shape[-1]
    mod_part = _ada_fwd(c_all, ada_w.reshape(n_sub, D, n_ada), ada_b.reshape(n_sub, 1, n_ada), "ada_fwd")
    mod_all = _allgather8(mod_part.reshape(n_sub * Bg, n_ada), "gather_mod").reshape(8, n_sub, Bg, n_ada)
    mod = jnp.concatenate([mod_all[2 * k] for k in range(4)], axis=-1)
    mod = lax.dynamic_slice_in_dim(mod, me * B, B, axis=1)
    shift = [mod[j, :, None, :D] for j in range(n_sub)]
    scale = [mod[j, :, None, D : 2 * D] for j in range(n_sub)]
    gate = [mod[j, :, None, 2 * D :] for j in range(n_sub)]

    ln_rows = 2 * n_sub
    ln_local = _pad_rows(jnp.concatenate([ln_g.reshape(n_sub, dq), ln_b.reshape(n_sub, dq)], axis=0), -(-ln_rows // 8) * 8)
    ln_pad = jnp.zeros((ln_local.shape[0], -(-dq // LANES) * LANES), F32).at[:, :dq].set(ln_local)
    ln_all = _allgather8(ln_pad, "gather_ln")
    ln_full = jnp.concatenate([ln_all[2 * k, :ln_rows, :dq] for k in range(4)], axis=1)
    lng = [ln_full[j][None, :] for j in range(n_sub)]
    lnb = [ln_full[n_sub + j][None, :] for j in range(n_sub)]

    main = dict(mla_w_in=mla_w_in, mla_w_qb=mla_w_qb, mla_w_kvb=mla_w_kvb, mla_w_o=mla_w_o, hgrn_w_in=hgrn_w_in,
                hgrn_w_o=hgrn_w_o, ffn_w_in=ffn_w_in, ffn_w_out=ffn_w_out)
    names = list(main)

    def group_kinds(layer, part):
        if part:
            return [("ffn_w_in", layer), ("ffn_w_out", layer)]
        mixer = ["mla_w_in", "mla_w_qb", "mla_w_kvb", "mla_w_o"] if layer % 2 == 0 else ["hgrn_w_in", "hgrn_w_o"]
        return [(k, layer // 2) for k in mixer]

    gathers = {}
    after = mod_all[0, 0, :8, :LANES] + ln_all[0, :8, :LANES]
    for layer in range(depth):
        for part in range(2):
            lands = [lax.dynamic_update_index_in_dim(lax.empty((4,) + main[k].shape[1:], BF16), main[k][i].astype(BF16), k_me, 0)
                     for k, i in group_kinds(layer, part)]
            ssem, rsem, lands, after = _gather_start(lands, after, f"gather_start_l{layer}p{part}")
            gathers[layer, part] = (ssem, rsem, lands)
    scale[0] = scale[0] + after[0, 0]

    def row_w(g):
        return g.reshape(1, g.shape[0] * g.shape[1], g.shape[2])

    def full_w_in(g):
        return jnp.transpose(g, (1, 0, 2)).reshape(1, g.shape[1], 4 * g.shape[2])

    ang = positions.astype(F32)[..., None] * (ROPE_THETA ** (-jnp.arange(0, QK_ROPE, 2, dtype=F32) / QK_ROPE))
    cos, sin = jnp.cos(ang), jnp.sin(ang)

    gq = [mla_q_norm[j][None, :] for j in range(n_mla)]
    gkv = [mla_kv_norm[j][None, :] for j in range(n_mla)]
    gn = [hgrn_g_norm[j][None, :] for j in range(n_hgrn)]

    def r2(a):
        return a.reshape(T, a.shape[-1])

    def r3(a):
        return a.reshape(B, S, a.shape[-1])

    saved = []
    xs = x
    for layer in range(depth):
        j = layer // 2
        sub = 2 * layer
        tag = f"l{layer}"
        ssem, rsem, lands = gathers[layer, 0]
        lands = _gather_wait(ssem, rsem, lands, xs if layer else scale[0], f"gather_wait_{tag}p0")
        wl = {k: g for (k, _), g in zip(group_kinds(layer, 0), lands)}
        if layer == 0:
            h = _modulate(xs, scale[sub], shift[sub], f"mod_{tag}a")
        if layer % 2 == 0:
            wl["mla_w_in"] = full_w_in(wl["mla_w_in"])
            proj = r3(_mm_nn(r2(h), wl["mla_w_in"], F32, f"mla_in_{tag}"))
            qn, kvn = _mla_mid_fwd(proj, gq[j], gkv[j], f"mla_mid_{tag}")
            q = r3(_mm_nn(r2(qn), wl["mla_w_qb"], F32, f"mla_qb_{tag}"))
            kv = r3(_mm_nn(r2(kvn), wl["mla_w_kvb"], F32, f"mla_kvb_{tag}"))
            qh, kh, vh = _mla_prep_fwd(q, kv, proj, cos, sin, f"mla_prep_{tag}")
            o, lse = _attn_fwd(qh, kh, vh, f"attn_{tag}")
            wl["mla_w_o"] = row_w(wl["mla_w_o"])
            y = r3(_mm_nn(r2(o), wl["mla_w_o"], F32, f"mla_o_{tag}"))
            mix = (h, proj, qn, kvn, qh, kh, vh, o, lse)
        else:
            proj = r3(_mm_nn(r2(h), wl["hgrn_w_in"], F32, f"hgrn_in_{tag}"))
            og, o_pre, states = _hgrn_fwd(proj, lower_bounds[j][None, :], gn[j], f"hgrn_{tag}")
            wl["hgrn_w_o"] = row_w(wl["hgrn_w_o"])
            y = r3(_mm_nn(r2(og), wl["hgrn_w_o"], F32, f"hgrn_o_{tag}"))
            mix = (h, proj, og, o_pre, states)
        x1, h2 = _ln_mod_fwd(alpha, xs, y, gate[sub], lng[sub], lnb[sub], scale[sub + 1], shift[sub + 1], f"ln_{tag}a")
        ssem, rsem, lands = gathers[layer, 1]
        lands = _gather_wait(ssem, rsem, lands, x1, f"gather_wait_{tag}p1")
        wl.update({k: g for (k, _), g in zip(group_kinds(layer, 1), lands)})
        a, ug, uu = [r3(t_) for t_ in _ffn_in(r2(h2), wl["ffn_w_in"], f"ffn_in_{tag}")]
        wl["ffn_w_out"] = row_w(wl["ffn_w_out"])
        y2 = r3(_mm_nn(r2(a), wl["ffn_w_out"], F32, f"ffn_out_{tag}"))
        if layer + 1 < depth:
            x2, h_next = _ln_mod_fwd(alpha, x1, y2, gate[sub + 1], lng[sub + 1], lnb[sub + 1], scale[sub + 2], shift[sub + 2], f"ln_{tag}b")
        else:
            x2, h_next = _ln_fwd(alpha, x1, y2, gate[sub + 1], lng[sub + 1], lnb[sub + 1], f"ln_{tag}b"), None
        saved.append((xs, y, x1, y2, mix, h2, ug, uu, a, wl))
        xs, h = x2, h_next

    loss_local, dout = _loss_head(xs, loss_target, "loss_head")
    loss = lax.psum(loss_local, ("x", "y", "c"))

    gw = {k: [None] * main[k].shape[0] for k in names}
    land = {k: lax.empty((3,) + main[k].shape, BF16) for k in names}
    scatters = []
    d_shift, d_scale, d_gate = [None] * n_sub, [None] * n_sub, [None] * n_sub
    d_lng, d_lnb = [None] * n_sub, [None] * n_sub
    d_gq, d_gkv, d_gn, d_lbnd = [None] * n_mla, [None] * n_mla, [None] * n_hgrn, [None] * n_hgrn

    def rows4(g):
        return g.reshape(4, g.shape[1] // 4, g.shape[2])

    def scatter_kinds(layer, part):
        if part == 1 or layer % 2:
            return group_kinds(layer, part)
        mixer = group_kinds(layer, 0)
        return mixer[:1] if part == 0 else mixer[1:]

    def start_scatter(layer, part, params, at):
        kinds = scatter_kinds(layer, part)
        ssem, rsem, slabs_t, lands_t, token = _scatter_start(
            [gw[k][i] for k, i in kinds], [land[k] for k, _ in kinds], [i for _, i in kinds], f"scatter_start_l{layer}p{part}")
        for (k, i), s_t, l_t in zip(kinds, slabs_t, lands_t):
            gw[k][i], land[k] = s_t, l_t
        scatters.append((layer, part, ssem, rsem))
        if params is not None:
            params[at] = params[at] + token[0, 0]

    for layer in reversed(range(depth)):
        j = layer // 2
        sub = 2 * layer
        tag = f"l{layer}"
        xs, y, x1, y2, mix, h2, ug, uu, a, wl = saved[layer]
        if layer + 1 == depth:
            dxr, dy2, d_gate[sub + 1], d_lng[sub + 1], d_lnb[sub + 1] = _ln_bwd(
                alpha, dout, x1, y2, gate[sub + 1], lng[sub + 1], lnb[sub + 1], f"ln_bwd_{tag}b")
        else:
            dxr, dy2, d_gate[sub + 1], d_lng[sub + 1], d_lnb[sub + 1], d_scale[sub + 2], d_shift[sub + 2] = _ln_mod_bwd(
                alpha, dh, dxr, scale[sub + 2], x1, y2, gate[sub + 1], lng[sub + 1], lnb[sub + 1], f"ln_bwd_{tag}b")
        da = r3(_mm_nt(r2(dy2), wl["ffn_w_out"], F32, f"ffn_out_dx_{tag}"))
        gw["ffn_w_out"][layer] = rows4(_mm_tn(r2(a), r2(dy2), 1, BF16, f"ffn_out_dw_{tag}"))
        du = _swiglu_bwd(ug, uu, da, f"swiglu_bwd_{tag}")
        dh2 = r3(_mm_nt(r2(du), wl["ffn_w_in"], F32, f"ffn_in_dx_{tag}"))
        gw["ffn_w_in"][layer] = _mm_tn(r2(h2), r2(du), 4, BF16, f"ffn_in_dw_{tag}")
        start_scatter(layer, 1, gate, sub)
        dxr, dy, d_gate[sub], d_lng[sub], d_lnb[sub], d_scale[sub + 1], d_shift[sub + 1] = _ln_mod_bwd(
            alpha, dh2, dxr, scale[sub + 1], xs, y, gate[sub], lng[sub], lnb[sub], f"ln_bwd_{tag}a")
        if layer % 2 == 0:
            h, proj, qn, kvn, qh, kh, vh, o, lse = mix
            do = r3(_mm_nt(r2(dy), wl["mla_w_o"], BF16, f"mla_o_dx_{tag}"))
            gw["mla_w_o"][j] = rows4(_mm_tn(r2(o), r2(dy), 1, BF16, f"mla_o_dw_{tag}"))
            dqh, dkh, dvh = _attn_bwd(qh, kh, vh, o, do, lse, f"attn_bwd_{tag}")
            dq_, dkv_, dkr = _mla_prep_bwd(dqh, dkh, dvh, cos, sin, f"mla_prep_bwd_{tag}")
            dqn = r3(_mm_nt(r2(dq_), wl["mla_w_qb"], F32, f"mla_qb_dx_{tag}"))
            gw["mla_w_qb"][j] = _mm_tn(r2(qn), r2(dq_), 4, BF16, f"mla_qb_dw_{tag}")
            dkvn = r3(_mm_nt(r2(dkv_), wl["mla_w_kvb"], F32, f"mla_kvb_dx_{tag}"))
            gw["mla_w_kvb"][j] = _mm_tn(r2(kvn), r2(dkv_), 4, BF16, f"mla_kvb_dw_{tag}")
            start_scatter(layer, 2, gq, j)
            dproj, dgq_, dgkv_ = _mla_mid_bwd(proj, dqn, dkvn, dkr, gq[j], gkv[j], f"mla_mid_bwd_{tag}")
            d_gq[j], d_gkv[j] = dgq_.sum(0), dgkv_.sum(0)
            dh = r3(_mm_nt(r2(dproj), wl["mla_w_in"], F32, f"mla_in_dx_{tag}"))
            gwin = _mm_tn(r2(h), r2(dproj), 1, BF16, f"mla_in_dw_{tag}")[0]
            gw["mla_w_in"][j] = jnp.transpose(gwin.reshape(gwin.shape[0], 4, gwin.shape[1] // 4), (1, 0, 2))
        else:
            h, proj, og, o_pre, states = mix
            dog = r3(_mm_nt(r2(dy), wl["hgrn_w_o"], F32, f"hgrn_o_dx_{tag}"))
            gw["hgrn_w_o"][j] = rows4(_mm_tn(r2(og), r2(dy), 1, BF16, f"hgrn_o_dw_{tag}"))
            dq_, df_, di_, dg_, dlb_, dgn_ = _hgrn_bwd(proj, lower_bounds[j][None, :], gn[j], o_pre, states, dog, f"hgrn_bwd_{tag}")
            dproj = jnp.concatenate([dq_, df_, di_, dg_], axis=-1)
            d_lbnd[j] = dlb_.sum(0).reshape(1, HK)
            d_gn[j] = dgn_.sum((0, 1))
            dh = r3(_mm_nt(r2(dproj), wl["hgrn_w_in"], F32, f"hgrn_in_dx_{tag}"))
            gw["hgrn_w_in"][j] = _mm_tn(r2(h), r2(dproj), 4, BF16, f"hgrn_in_dw_{tag}")
        start_scatter(layer, 0, gate if layer else None, sub - 1)
    grad_x, d_scale[0], d_shift[0] = _mod_bwd(dh, dxr, x, scale[0], "mod_bwd_l0a")

    for layer, part, ssem, rsem in scatters:
        kinds = scatter_kinds(layer, part)
        slabs_t, lands_t = _scatter_wait(
            ssem, rsem, [gw[k][i] for k, i in kinds], [land[k] for k, _ in kinds], [i for _, i in kinds], grad_x,
            f"scatter_wait_l{layer}p{part}")
        for (k, i), s_t, l_t in zip(kinds, slabs_t, lands_t):
            gw[k][i], land[k] = s_t, l_t
    sums = [_sum4(jnp.stack([lax.dynamic_index_in_dim(g, k_me, 0, keepdims=False) for g in gw[k]]), land[k], f"sum4_{k}")
            for k in names]

    dmod = jnp.stack([jnp.concatenate([d_shift[s_][:, 0], d_scale[s_][:, 0], d_gate[s_][:, 0]], axis=-1) for s_ in range(n_sub)])
    dmod_rows = _pad_rows(dmod.reshape(n_sub * B, 3 * D), -(-n_sub * B // 8) * 8)
    dmod_all = _allgather8(dmod_rows, "gather_dmod")[:, : n_sub * B].reshape(8, n_sub, B, 3 * D)
    dmod_all = jnp.transpose(dmod_all, (1, 0, 2, 3)).reshape(n_sub, Bg, 3 * D)
    dmod_mine = lax.dynamic_slice_in_dim(dmod_all, k_me * n_ada, n_ada, axis=2)
    g_ada_w, g_ada_b = _ada_bwd(c_all, dmod_mine, "ada_bwd")
    g_ada_w = g_ada_w.reshape(ada_w.shape)
    g_ada_b = g_ada_b.reshape(ada_b.shape)

    small = [jnp.stack(d_gq).reshape(-1), jnp.stack(d_gkv).reshape(-1), jnp.stack(d_gn).reshape(-1),
             jnp.stack(d_lbnd).reshape(-1), jnp.stack([d.sum(0) for d in d_lng]).reshape(-1),
             jnp.stack([d.sum(0) for d in d_lnb]).reshape(-1)]
    sizes = [s_.shape[0] for s_ in small]
    flat = jnp.concatenate(small)
    rows_small = -(-flat.shape[0] // (8 * LANES)) * 8
    flat = jnp.pad(flat, (0, rows_small * LANES - flat.shape[0])).reshape(rows_small, LANES)
    tot = _allgather8(flat, "gather_small")
    acc = tot[0]
    for d in range(1, 8):
        acc = acc + tot[d]
    acc = acc.reshape(-1)
    offs = [0]
    for s_ in sizes:
        offs.append(offs[-1] + s_)
    g_q_norm = acc[offs[0] : offs[1]].reshape(mla_q_norm.shape)
    g_kv_norm = acc[offs[1] : offs[2]].reshape(mla_kv_norm.shape)
    g_g_norm = acc[offs[2] : offs[3]].reshape(hgrn_g_norm.shape)
    g_lbnd = acc[offs[3] : offs[4]].reshape(n_hgrn, HK)
    g_lb_full = lower_bounds_vjp(g_lbnd)[0]
    g_hgrn_lb = lax.dynamic_slice_in_dim(g_lb_full, k_me * lbw, lbw, axis=1)
    g_lng = lax.dynamic_slice_in_dim(acc[offs[4] : offs[5]].reshape(n_sub, D), k_me * dq, dq, axis=1).reshape(ln_g.shape)
    g_lnb = lax.dynamic_slice_in_dim(acc[offs[5] : offs[6]].reshape(n_sub, D), k_me * dq, dq, axis=1).reshape(ln_b.shape)

    weights = dict(mla_w_in=mla_w_in, mla_q_norm=mla_q_norm, mla_w_qb=mla_w_qb, mla_kv_norm=mla_kv_norm, mla_w_kvb=mla_w_kvb,
                   mla_w_o=mla_w_o, hgrn_lb=hgrn_lb, hgrn_w_in=hgrn_w_in, hgrn_g_norm=hgrn_g_norm, hgrn_w_o=hgrn_w_o,
                   ffn_w_in=ffn_w_in, ffn_w_out=ffn_w_out, ada_w=ada_w, ada_b=ada_b, ln_g=ln_g, ln_b=ln_b)
    moms = dict(mla_w_in=(m_mla_w_in, v_mla_w_in), mla_q_norm=(m_mla_q_norm, v_mla_q_norm), mla_w_qb=(m_mla_w_qb, v_mla_w_qb),
                mla_kv_norm=(m_mla_kv_norm, v_mla_kv_norm), mla_w_kvb=(m_mla_w_kvb, v_mla_w_kvb), mla_w_o=(m_mla_w_o, v_mla_w_o),
                hgrn_lb=(m_hgrn_lb, v_hgrn_lb), hgrn_w_in=(m_hgrn_w_in, v_hgrn_w_in), hgrn_g_norm=(m_hgrn_g_norm, v_hgrn_g_norm),
                hgrn_w_o=(m_hgrn_w_o, v_hgrn_w_o), ffn_w_in=(m_ffn_w_in, v_ffn_w_in), ffn_w_out=(m_ffn_w_out, v_ffn_w_out),
                ada_w=(m_ada_w, v_ada_w), ada_b=(m_ada_b, v_ada_b), ln_g=(m_ln_g, v_ln_g), ln_b=(m_ln_b, v_ln_b))
    grads = dict(mla_q_norm=(g_q_norm,), mla_kv_norm=(g_kv_norm,), hgrn_lb=(g_hgrn_lb,), hgrn_g_norm=(g_g_norm,),
                 ada_w=(g_ada_w,), ada_b=(g_ada_b,), ln_g=(g_lng,), ln_b=(g_lnb,))

    def adamw(k):
        return _adamw(weights[k], [g_.reshape(weights[k].shape) for g_ in grads[k]], moms[k][0], moms[k][1], f"adamw_{k}")

    ssem, rsem, sums, others = _swap_start(sums, tot[0, :8] + dmod_all[0, :8, :LANES], "swap_start")
    res = {k: adamw(k) for k in grads}
    sums, others = _swap_wait(ssem, rsem, sums, others, res["ada_w"][1], "swap_wait")
    grads.update({k: (a_, b_) for k, a_, b_ in zip(names, sums, others)})
    res.update({k: adamw(k) for k in names})
    order = list(weights)
    return (loss, grad_x, *[res[k][0] for k in order], *[res[k][1] for k in order], *[res[k][2] for k in order],
            *[res[k][3] for k in order])
```

```python
import functools

import jax
import jax.numpy as jnp
from jax import lax
from jax.experimental import pallas as pl
from jax.experimental.pallas import tpu as pltpu

F32 = jnp.float32
BF16 = jnp.bfloat16
SDS = jax.ShapeDtypeStruct
MESH = pl.DeviceIdType.MESH
HI = lax.Precision.HIGHEST
MID = lax.Precision.HIGH

MLA_HEADS, QK_NOPE, QK_ROPE, V_HEAD = 16, 64, 32, 64
Q_LORA, KV_LORA = 768, 256
QK_DIM = QK_NOPE + QK_ROPE
ROPE_THETA = 10000.0
HGRN_K = 128
HGRN_CHUNK = 128
HGRN_SUB = 32
HGRN_PAR = 2
LN_EPS, RMS_EPS = 1e-5, 1e-6
ADAM_LR, ADAM_B1, ADAM_B2, ADAM_EPS, ADAM_WD, ADAM_STEP = 0.001, 0.9, 0.999, 1e-08, 0.01, 10
NEG = -1e30

VMEM_LIMIT_BYTES = 56 * 1024 * 1024
NT_RESIDENT_BYTES = 12 * 1024 * 1024
LANES = 128
SUBLANES = 8


def _cparams(*sem):
    return pltpu.CompilerParams(dimension_semantics=sem if sem else None, vmem_limit_bytes=VMEM_LIMIT_BYTES)


def _pick_tile(n, cap):
    best = 0
    for t in range(LANES, min(n, cap) + 1, LANES):
        if n % t == 0:
            best = t
    return best if best else n


def _bdot(a, b):
    return jnp.dot(a.astype(BF16), b.astype(BF16), preferred_element_type=F32)


def _bdot_nt(a, b):
    return lax.dot_general(a.astype(BF16), b.astype(BF16), (((1,), (1,)), ((), ())), preferred_element_type=F32)


def _bdot_tn(a, b):
    return lax.dot_general(a.astype(BF16), b.astype(BF16), (((0,), (0,)), ((), ())), preferred_element_type=F32)


def _hdot(a, b):
    return jnp.dot(a, b, precision=HI, preferred_element_type=F32)


def _mdot(a, b):
    return jnp.dot(a, b, precision=MID, preferred_element_type=F32)


def _mdot_nt(a, b):
    return lax.dot_general(a, b, (((1,), (1,)), ((), ())), precision=MID, preferred_element_type=F32)


def _mdot_tn(a, b):
    return lax.dot_general(a, b, (((0,), (0,)), ((), ())), precision=MID, preferred_element_type=F32)


def _mm_nn(a, w, out_dtype, name):
    M, K = a.shape
    G, _, n = w.shape
    tm = min(512, M)
    tn = _pick_tile(n, 1536)
    nps = n // tn

    def body(a_ref, w_ref, o_ref):
        o_ref[...] = _bdot(a_ref[...], w_ref[...]).astype(o_ref.dtype)

    return pl.pallas_call(
        body,
        grid=(G * nps, M // tm),
        in_specs=[
            pl.BlockSpec((tm, K), lambda j, i: (i, 0)),
            pl.BlockSpec((None, K, tn), lambda j, i: (j // nps, 0, j % nps)),
        ],
        out_specs=pl.BlockSpec((tm, tn), lambda j, i: (i, j)),
        out_shape=SDS((M, G * n), out_dtype),
        name=name,
        compiler_params=_cparams("parallel", "parallel"),
    )(a, w)


def _mm_nt(a, w, out_dtype, name):
    M = a.shape[0]
    G, K, n = w.shape
    tm = min(512, M)
    tk = _pick_tile(K, 1536)

    if G > 1 and w.size * 2 <= NT_RESIDENT_BYTES:
        def body_all(a_ref, w_ref, o_ref):
            acc = _bdot_nt(a_ref[:, :n], w_ref[0])
            for s in range(1, G):
                acc = acc + _bdot_nt(a_ref[:, s * n : (s + 1) * n], w_ref[s])
            o_ref[...] = acc.astype(o_ref.dtype)

        return pl.pallas_call(
            body_all,
            grid=(M // tm,),
            in_specs=[pl.BlockSpec((tm, G * n), lambda i: (i, 0)), pl.BlockSpec((G, K, n), lambda i: (0, 0, 0))],
            out_specs=pl.BlockSpec((tm, K), lambda i: (i, 0)),
            out_shape=SDS((M, K), out_dtype),
            name=name,
            compiler_params=_cparams("parallel"),
        )(a, w)

    def body(a_ref, w_ref, o_ref, acc_ref):
        s = pl.program_id(2)

        @pl.when(s == 0)
        def _():
            acc_ref[...] = jnp.zeros_like(acc_ref)

        acc_ref[...] += _bdot_nt(a_ref[...], w_ref[...])

        @pl.when(s == G - 1)
        def _():
            o_ref[...] = acc_ref[...].astype(o_ref.dtype)

    return pl.pallas_call(
        body,
        grid=(K // tk, M // tm, G),
        in_specs=[
            pl.BlockSpec((tm, n), lambda kb, i, s: (i, s)),
            pl.BlockSpec((None, tk, n), lambda kb, i, s: (s, kb, 0)),
        ],
        out_specs=pl.BlockSpec((tm, tk), lambda kb, i, s: (i, kb)),
        out_shape=SDS((M, K), out_dtype),
        scratch_shapes=[pltpu.VMEM((tm, tk), F32)],
        name=name,
        compiler_params=_cparams("parallel", "parallel", "arbitrary"),
    )(a, w)


def _mm_tn(a, d, G, out_dtype, name):
    T, K = a.shape
    n = d.shape[1] // G
    tk = _pick_tile(K, 256)
    tn = _pick_tile(n, 1536)
    nps = n // tn

    def body(a_ref, d_ref, o_ref):
        o_ref[...] = _bdot_tn(a_ref[...], d_ref[...]).astype(o_ref.dtype)

    return pl.pallas_call(
        body,
        grid=(G * nps, K // tk),
        in_specs=[
            pl.BlockSpec((T, tk), lambda j, i: (0, i)),
            pl.BlockSpec((T, tn), lambda j, i: (0, j)),
        ],
        out_specs=pl.BlockSpec((None, tk, tn), lambda j, i: (j // nps, i, j % nps)),
        out_shape=SDS((G, K, n), out_dtype),
        name=name,
        compiler_params=_cparams("parallel", "parallel"),
    )(a, d)


def _rows_call(body, name, B, S, ins, outs, ts=256):
    ts = min(ts, S)
    in_specs, args = [], []
    for arr, kind in ins:
        W = arr.shape[-1]
        if kind == "row":
            in_specs.append(pl.BlockSpec((None, ts, W), lambda b, s: (b, s, 0)))
        elif kind == "ex":
            in_specs.append(pl.BlockSpec((None, 1, W), lambda b, s: (b, 0, 0)))
        else:
            in_specs.append(pl.BlockSpec((1, W), lambda b, s: (0, 0)))
        args.append(arr)
    out_specs, out_shape = [], []
    for W, dt, kind in outs:
        if kind == "row":
            out_specs.append(pl.BlockSpec((None, ts, W), lambda b, s: (b, s, 0)))
            out_shape.append(SDS((B, S, W), dt))
        else:
            out_specs.append(pl.BlockSpec((None, 1, W), lambda b, s: (b, 0, 0)))
            out_shape.append(SDS((B, 1, W), dt))
    return pl.pallas_call(
        body,
        grid=(B, S // ts),
        in_specs=in_specs,
        out_specs=out_specs,
        out_shape=out_shape,
        name=name,
        compiler_params=_cparams("parallel", "arbitrary"),
    )(*args)


def _acc(ref, val):
    @pl.when(pl.program_id(1) == 0)
    def _():
        ref[...] = jnp.zeros_like(ref)

    ref[...] += val


def _mod_fn(x, sc, sh):
    return x * (1.0 + sc) + sh


def _ln_fn(alpha, x, y, gate, g, b):
    z = alpha * x + (1.0 + gate) * y
    mu = jnp.mean(z, -1, keepdims=True)
    var = jnp.mean(jnp.square(z - mu), -1, keepdims=True)
    return (z - mu) * lax.rsqrt(var + LN_EPS) * g + b


def _modulate(x, sc, sh, name):
    B, S, D = x.shape

    def body(x_ref, sc_ref, sh_ref, h_ref):
        h_ref[...] = _mod_fn(x_ref[...], sc_ref[...], sh_ref[...]).astype(BF16)

    return _rows_call(body, name, B, S, [(x, "row"), (sc, "ex"), (sh, "ex")], [(D, BF16, "row")])[0]


def _ln_fwd(alpha, x, y, gate, g, b, name):
    B, S, D = x.shape

    def body(x_ref, y_ref, gate_ref, g_ref, b_ref, o_ref):
        o_ref[...] = _ln_fn(alpha, x_ref[...], y_ref[...], gate_ref[...], g_ref[...], b_ref[...])

    return _rows_call(
        body, name, B, S, [(x, "row"), (y, "row"), (gate, "ex"), (g, "par"), (b, "par")], [(D, F32, "row")]
    )[0]


def _ln_mod_fwd(alpha, x, y, gate, g, b, sc_next, sh_next, name):
    B, S, D = x.shape

    def body(x_ref, y_ref, gate_ref, g_ref, b_ref, sc_ref, sh_ref, o_ref, h_ref):
        out = _ln_fn(alpha, x_ref[...], y_ref[...], gate_ref[...], g_ref[...], b_ref[...])
        o_ref[...] = out
        h_ref[...] = _mod_fn(out, sc_ref[...], sh_ref[...]).astype(BF16)

    return _rows_call(
        body, name, B, S,
        [(x, "row"), (y, "row"), (gate, "ex"), (g, "par"), (b, "par"), (sc_next, "ex"), (sh_next, "ex")],
        [(D, F32, "row"), (D, BF16, "row")],
    )


def _ln_mod_bwd(alpha, dh, dxr_next, sc_next, x, y, gate, g, b, name):
    B, S, D = x.shape

    def body(dh_ref, dxr_ref, sc_ref, x_ref, y_ref, gate_ref, g_ref, b_ref,
             dx_ref, dy_ref, dgate_ref, dg_ref, db_ref, dsc_ref, dsh_ref):
        out, vjp = jax.vjp(
            functools.partial(_ln_fn, alpha), x_ref[...], y_ref[...], gate_ref[...], g_ref[...], b_ref[...]
        )
        dh_v = dh_ref[...]
        dx, dy, dgate, dg, db = vjp(dxr_ref[...] + dh_v * (1.0 + sc_ref[...]))
        dx_ref[...] = dx
        dy_ref[...] = dy.astype(BF16)
        _acc(dgate_ref, dgate)
        _acc(dg_ref, dg)
        _acc(db_ref, db)
        _acc(dsc_ref, jnp.sum(dh_v * out, axis=0, keepdims=True))
        _acc(dsh_ref, jnp.sum(dh_v, axis=0, keepdims=True))

    return _rows_call(
        body, name, B, S,
        [(dh, "row"), (dxr_next, "row"), (sc_next, "ex"), (x, "row"), (y, "row"), (gate, "ex"), (g, "par"), (b, "par")],
        [(D, F32, "row"), (D, BF16, "row")] + [(D, F32, "acc")] * 5,
    )


def _ln_bwd(alpha, dout, x, y, gate, g, b, name):
    B, S, D = x.shape

    def body(do_ref, x_ref, y_ref, gate_ref, g_ref, b_ref, dxr_ref, dy_ref, dgate_ref, dg_ref, db_ref):
        _, vjp = jax.vjp(
            functools.partial(_ln_fn, alpha), x_ref[...], y_ref[...], gate_ref[...], g_ref[...], b_ref[...]
        )
        dx, dy, dgate, dg, db = vjp(do_ref[...])
        dxr_ref[...] = dx
        dy_ref[...] = dy.astype(BF16)
        _acc(dgate_ref, dgate)
        _acc(dg_ref, dg)
        _acc(db_ref, db)

    return _rows_call(
        body,
        name,
        B,
        S,
        [(dout, "row"), (x, "row"), (y, "row"), (gate, "ex"), (g, "par"), (b, "par")],
        [(D, F32, "row"), (D, BF16, "row"), (D, F32, "acc"), (D, F32, "acc"), (D, F32, "acc")],
    )


def _mod_bwd(dh, dxr, x, sc, name):
    B, S, D = x.shape

    def body(dh_ref, dxr_ref, x_ref, sc_ref, dx_ref, dsc_ref, dsh_ref):
        dh_v = dh_ref[...]
        dx_ref[...] = dxr_ref[...] + dh_v * (1.0 + sc_ref[...])
        _acc(dsc_ref, jnp.sum(dh_v * x_ref[...], axis=0, keepdims=True))
        _acc(dsh_ref, jnp.sum(dh_v, axis=0, keepdims=True))

    return _rows_call(
        body,
        name,
        B,
        S,
        [(dh, "row"), (dxr, "row"), (x, "row"), (sc, "ex")],
        [(D, F32, "row"), (D, F32, "acc"), (D, F32, "acc")],
    )


def _loss_head(y, target, name):
    B, S, D = y.shape

    def body(y_ref, t_ref, l_ref, dy_ref):
        e = y_ref[...] - t_ref[...]
        dy_ref[...] = e * (1.0 / D)
        part = 0.5 * jnp.sum(jnp.sum(e * e, axis=1, keepdims=True) * (1.0 / D), axis=0, keepdims=True)
        _acc(l_ref, jnp.broadcast_to(part, (1, LANES)))

    loss, dy = _rows_call(
        body, name, B, S, [(y, "row"), (target, "row")], [(LANES, F32, "acc"), (D, F32, "row")]
    )
    return jnp.sum(loss[:, 0, 0]), dy


def _ffn_in(h, w, name):
    M, K = h.shape
    G, _, n = w.shape
    assert G == 4
    tm = min(512, M)
    tn = _pick_tile(n, 1536)
    nps = n // tn
    half = 2 * nps

    def body(h_ref, wg_ref, wu_ref, a_ref, g_ref, u_ref):
        hv = h_ref[...]
        g = _bdot(hv, wg_ref[...])
        u = _bdot(hv, wu_ref[...])
        a_ref[...] = (jax.nn.silu(g) * u).astype(BF16)
        g_ref[...] = g.astype(BF16)
        u_ref[...] = u.astype(BF16)

    out = pl.BlockSpec((tm, tn), lambda j, i: (i, j))
    return pl.pallas_call(
        body,
        grid=(half, M // tm),
        in_specs=[
            pl.BlockSpec((tm, K), lambda j, i: (i, 0)),
            pl.BlockSpec((None, K, tn), lambda j, i: (j // nps, 0, j % nps)),
            pl.BlockSpec((None, K, tn), lambda j, i: (2 + j // nps, 0, j % nps)),
        ],
        out_specs=[out, out, out],
        out_shape=[SDS((M, 2 * n), BF16)] * 3,
        name=name,
        compiler_params=_cparams("parallel", "parallel"),
    )(h, w, w)


def _swiglu_bwd(g, u, da, name):
    B, S, F = g.shape

    def body(g_ref, u_ref, da_ref, du_ref):
        _, vjp = jax.vjp(lambda gv, uv: jax.nn.silu(gv) * uv, g_ref[...].astype(F32), u_ref[...].astype(F32))
        dg, du = vjp(da_ref[...])
        du_ref[:, :F] = dg.astype(BF16)
        du_ref[:, F:] = du.astype(BF16)

    return _rows_call(body, name, B, S, [(g, "row"), (u, "row"), (da, "row")], [(2 * F, BF16, "row")])[0]


def _rms_fn(x, g):
    return x * lax.rsqrt(jnp.mean(jnp.square(x), -1, keepdims=True) + RMS_EPS) * g


def _mla_mid_fwd(proj, gq, gkv, name):
    B, S, _ = proj.shape

    def body(p_ref, gq_ref, gkv_ref, qn_ref, kvn_ref):
        p = p_ref[...]
        qn_ref[...] = _rms_fn(p[:, :Q_LORA], gq_ref[...]).astype(BF16)
        kvn_ref[...] = _rms_fn(p[:, Q_LORA : Q_LORA + KV_LORA], gkv_ref[...]).astype(BF16)

    return _rows_call(
        body, name, B, S, [(proj, "row"), (gq, "par"), (gkv, "par")], [(Q_LORA, BF16, "row"), (KV_LORA, BF16, "row")]
    )


def _mla_mid_bwd(proj, dqn, dkvn, dkr, gq, gkv, name):
    B, S, W = proj.shape

    def body(p_ref, dqn_ref, dkvn_ref, dkr_ref, gq_ref, gkv_ref, dp_ref, dgq_ref, dgkv_ref):
        p = p_ref[...]
        _, vq = jax.vjp(_rms_fn, p[:, :Q_LORA], gq_ref[...])
        dql, dgq = vq(dqn_ref[...])
        _, vkv = jax.vjp(_rms_fn, p[:, Q_LORA : Q_LORA + KV_LORA], gkv_ref[...])
        dkvl, dgkv = vkv(dkvn_ref[...])
        dp_ref[:, :Q_LORA] = dql.astype(BF16)
        dp_ref[:, Q_LORA : Q_LORA + KV_LORA] = dkvl.astype(BF16)
        dp_ref[:, Q_LORA + KV_LORA :] = dkr_ref[...].astype(BF16)
        _acc(dgq_ref, dgq)
        _acc(dgkv_ref, dgkv)

    return _rows_call(
        body,
        name,
        B,
        S,
        [(proj, "row"), (dqn, "row"), (dkvn, "row"), (dkr, "row"), (gq, "par"), (gkv, "par")],
        [(W, BF16, "row"), (Q_LORA, F32, "acc"), (KV_LORA, F32, "acc")],
    )


def _rope(x, cos, sin):
    h = QK_ROPE // 2
    x1, x2 = x[:, :h], x[:, h:]
    return jnp.concatenate([x1 * cos - x2 * sin, x1 * sin + x2 * cos], axis=1)


def _rope_t(dy, cos, sin):
    h = QK_ROPE // 2
    d1, d2 = dy[:, :h], dy[:, h:]
    return jnp.concatenate([d1 * cos + d2 * sin, d2 * cos - d1 * sin], axis=1)


def _heads_call(body, name, B, S, ins, outs, ts=256):
    ts = min(ts, S)
    in_specs, args = [], []
    for arr, kind in ins:
        if kind == "row":
            in_specs.append(pl.BlockSpec((None, ts, arr.shape[-1]), lambda b, s: (b, s, 0)))
        else:
            in_specs.append(pl.BlockSpec((arr.shape[0], None, ts, arr.shape[-1]), lambda b, s: (0, b, s, 0)))
        args.append(arr)
    out_specs, out_shape = [], []
    for shape, dt, kind in outs:
        if kind == "row":
            out_specs.append(pl.BlockSpec((None, ts, shape[-1]), lambda b, s: (b, s, 0)))
        else:
            out_specs.append(pl.BlockSpec((shape[0], None, ts, shape[-1]), lambda b, s: (0, b, s, 0)))
        out_shape.append(SDS(shape, dt))
    return pl.pallas_call(
        body,
        grid=(B, S // ts),
        in_specs=in_specs,
        out_specs=out_specs,
        out_shape=out_shape,
        name=name,
        compiler_params=_cparams("parallel", "parallel"),
    )(*args)


def _mla_prep_fwd(q, kv, proj, cos, sin, name):
    B, S, _ = q.shape
    H = MLA_HEADS

    def body(q_ref, kv_ref, p_ref, cos_ref, sin_ref, qh_ref, kh_ref, vh_ref):
        cos_v, sin_v = cos_ref[...], sin_ref[...]
        kr = _rope(p_ref[:, Q_LORA + KV_LORA :], cos_v, sin_v).astype(BF16)
        for h in range(H):
            qn = q_ref[:, h * QK_DIM : h * QK_DIM + QK_NOPE]
            qr = _rope(q_ref[:, h * QK_DIM + QK_NOPE : (h + 1) * QK_DIM], cos_v, sin_v)
            qh_ref[h] = jnp.concatenate([qn, qr], axis=1).astype(BF16)
            kn = kv_ref[:, h * 128 : h * 128 + QK_NOPE].astype(BF16)
            kh_ref[h] = jnp.concatenate([kn, kr], axis=1)
            vh_ref[h] = kv_ref[:, h * 128 + QK_NOPE : (h + 1) * 128].astype(BF16)

    return _heads_call(
        body,
        name,
        B,
        S,
        [(q, "row"), (kv, "row"), (proj, "row"), (cos, "row"), (sin, "row")],
        [((H, B, S, QK_DIM), BF16, "heads"), ((H, B, S, QK_DIM), BF16, "heads"), ((H, B, S, V_HEAD), BF16, "heads")],
    )


def _mla_prep_bwd(dqh, dkh, dvh, cos, sin, name):
    H, B, S, _ = dqh.shape

    def body(dqh_ref, dkh_ref, dvh_ref, cos_ref, sin_ref, dq_ref, dkv_ref, dkr_ref):
        cos_v, sin_v = cos_ref[...], sin_ref[...]
        dkr = jnp.zeros((cos_v.shape[0], QK_ROPE), F32)
        for h in range(H):
            dqv = dqh_ref[h].astype(F32)
            dq_ref[:, h * QK_DIM : h * QK_DIM + QK_NOPE] = dqv[:, :QK_NOPE].astype(BF16)
            dq_ref[:, h * QK_DIM + QK_NOPE : (h + 1) * QK_DIM] = _rope_t(dqv[:, QK_NOPE:], cos_v, sin_v).astype(BF16)
            dkv = dkh_ref[h].astype(F32)
            dkv_ref[:, h * 128 : h * 128 + QK_NOPE] = dkv[:, :QK_NOPE].astype(BF16)
            dkv_ref[:, h * 128 + QK_NOPE : (h + 1) * 128] = dvh_ref[h]
            dkr = dkr + dkv[:, QK_NOPE:]
        dkr_ref[...] = _rope_t(dkr, cos_v, sin_v)

    return _heads_call(
        body,
        name,
        B,
        S,
        [(dqh, "heads"), (dkh, "heads"), (dvh, "heads"), (cos, "row"), (sin, "row")],
        [((B, S, H * QK_DIM), BF16, "row"), ((B, S, H * 128), BF16, "row"), ((B, S, QK_ROPE), F32, "row")],
    )


LOG2E = 1.4426950408889634
ATTN_TILE = 1024
ATTN_DIAG_SUB = 512


def _attn_fwd(qh, kh, vh, name):
    H, B, S, _ = qh.shape
    t = min(ATTN_TILE, S)
    scale = QK_DIM**-0.5
    c2 = scale * LOG2E

    def body(q_ref, k_ref, v_ref, o_ref, lse_ref):
        i = pl.program_id(2)
        qs = [q_ref[0], q_ref[1]]

        def update(state, q, k, v, mask):
            m, l, acc = state
            s = _bdot_nt(q, k)
            if mask is not None:
                s = jnp.where(mask, s, NEG)
            m_new = jnp.maximum(m, jnp.max(s, axis=1, keepdims=True))
            p = jnp.exp2((s - m_new) * c2)
            a = jnp.exp2((m - m_new) * c2)
            return m_new, a * l + jnp.sum(p, axis=1, keepdims=True), a * acc + _bdot(p, v)

        def step(j, carry):
            rows = pl.ds(pl.multiple_of(j * t, t), t)
            return tuple(update(carry[hh], qs[hh], k_ref[hh, rows, :], v_ref[hh, rows, :], None) for hh in range(2))

        one = (jnp.full((t, 1), NEG, F32), jnp.zeros((t, 1), F32), jnp.zeros((t, V_HEAD), F32))
        carry = lax.fori_loop(0, i, step, (one, one))
        rows = pl.ds(pl.multiple_of(i * t, t), t)
        causal = lax.broadcasted_iota(jnp.int32, (t, t), 0) >= lax.broadcasted_iota(jnp.int32, (t, t), 1)
        carry = tuple(update(carry[hh], qs[hh], k_ref[hh, rows, :], v_ref[hh, rows, :], causal) for hh in range(2))
        outs = []
        for hh in range(2):
            m, l, acc = carry[hh]
            outs.append(acc / l)
            lse_ref[hh] = m * scale + jnp.log(l)
        o_ref[...] = jnp.concatenate(outs, axis=1).astype(BF16)

    return pl.pallas_call(
        body,
        grid=(B, H // 2, S // t),
        in_specs=[
            pl.BlockSpec((2, None, t, QK_DIM), lambda b, p, i: (p, b, i, 0)),
            pl.BlockSpec((2, None, S, QK_DIM), lambda b, p, i: (p, b, 0, 0)),
            pl.BlockSpec((2, None, S, V_HEAD), lambda b, p, i: (p, b, 0, 0)),
        ],
        out_specs=[
            pl.BlockSpec((None, t, 2 * V_HEAD), lambda b, p, i: (b, i, p)),
            pl.BlockSpec((2, None, t, 1), lambda b, p, i: (p, b, i, 0)),
        ],
        out_shape=[SDS((B, S, H * V_HEAD), BF16), SDS((H, B, S, 1), F32)],
        name=name,
        compiler_params=_cparams("parallel", "parallel", "arbitrary"),
    )(qh, kh, vh)


def _attn_bwd(qh, kh, vh, o, do, lse, name):
    H, B, S, _ = qh.shape
    t = min(ATTN_TILE, S)
    sub = min(ATTN_DIAG_SUB, t)
    nq = S // t
    scale = QK_DIM**-0.5
    c2 = scale * LOG2E

    def body(q_ref, k_ref, v_ref, o_ref, do_ref, lse_ref, dq_ref, dk_ref, dv_ref, dq_acc, delta_ref, lse2_ref):
        prod = o_ref[...].astype(F32) * do_ref[...].astype(F32)
        for hh in range(2):
            delta_ref[hh] = jnp.sum(prod[:, hh * V_HEAD : (hh + 1) * V_HEAD], axis=1, keepdims=True)
            lse2_ref[hh] = lse_ref[hh] * LOG2E
        dq_acc[...] = jnp.zeros_like(dq_acc)

        def kloop(j, _):
            krows = pl.ds(pl.multiple_of(j * t, t), t)
            ks = [k_ref[0, krows, :], k_ref[1, krows, :]]
            vs = [v_ref[0, krows, :], v_ref[1, krows, :]]

            def pair(hh, qrows, k, v, mask):
                q = q_ref[hh, qrows, :]
                do_h = do_ref[qrows, :][:, hh * V_HEAD : (hh + 1) * V_HEAD]
                p = jnp.exp2(_bdot_nt(q, k) * c2 - lse2_ref[hh, qrows, :])
                if mask is not None:
                    p = jnp.where(mask, p, 0.0)
                dv = _bdot_tn(p, do_h)
                ds = (p * (_bdot_nt(do_h, v) - delta_ref[hh, qrows, :])).astype(BF16)
                dq_acc[hh, qrows, :] += _bdot(ds, k)
                return _bdot_tn(ds, q), dv

            def qstep(i, carry):
                qrows = pl.ds(pl.multiple_of(i * t, t), t)
                out = []
                for hh in range(2):
                    dk, dv = pair(hh, qrows, ks[hh], vs[hh], None)
                    out.append((carry[hh][0] + dk, carry[hh][1] + dv))
                return tuple(out)

            def diagonal_step():
                out = []
                for hh in range(2):
                    dks, dvs = [], []
                    for c in range(t // sub):
                        r0 = c * sub
                        qrows = pl.ds(pl.multiple_of(j * t + r0, sub), t - r0)
                        mask = (lax.broadcasted_iota(jnp.int32, (t - r0, sub), 0)
                                >= lax.broadcasted_iota(jnp.int32, (t - r0, sub), 1))
                        dk, dv = pair(hh, qrows, ks[hh][r0 : r0 + sub], vs[hh][r0 : r0 + sub], mask)
                        dks.append(dk)
                        dvs.append(dv)
                    out.append((jnp.concatenate(dks, axis=0), jnp.concatenate(dvs, axis=0)))
                return tuple(out)

            carry = lax.fori_loop(j + 1, nq, qstep, diagonal_step())
            for hh in range(2):
                dk_ref[hh, krows, :] = (carry[hh][0] * scale).astype(BF16)
                dv_ref[hh, krows, :] = carry[hh][1].astype(BF16)
            return 0

        lax.fori_loop(0, nq, kloop, 0)
        dq_ref[...] = (dq_acc[...] * scale).astype(BF16)

    hspec = lambda w: pl.BlockSpec((2, None, S, w), lambda b, p: (p, b, 0, 0))
    ospec = pl.BlockSpec((None, S, 2 * V_HEAD), lambda b, p: (b, 0, p))
    return pl.pallas_call(
        body,
        grid=(B, H // 2),
        in_specs=[hspec(QK_DIM), hspec(QK_DIM), hspec(V_HEAD), ospec, ospec, hspec(1)],
        out_specs=[hspec(QK_DIM), hspec(QK_DIM), hspec(V_HEAD)],
        out_shape=[SDS((H, B, S, QK_DIM), BF16), SDS((H, B, S, QK_DIM), BF16), SDS((H, B, S, V_HEAD), BF16)],
        scratch_shapes=[pltpu.VMEM((2, S, QK_DIM), F32), pltpu.VMEM((2, S, 1), F32), pltpu.VMEM((2, S, 1), F32)],
        name=name,
        compiler_params=_cparams("parallel", "parallel"),
    )(qh, kh, vh, o, do, lse)


def _hgrn_pre(q, fx, lb):
    sig = jax.nn.sigmoid(fx)
    f = lb + (1.0 - lb) * sig
    return jax.nn.silu(q), 1.0 - f, jnp.log(f)


def _hgrn_gate(o, gg, gn):
    return _rms_fn(o, gn) * jax.nn.silu(gg)


def _tri(n, lower):
    r = lax.broadcasted_iota(jnp.int32, (n, n), 0)
    c = lax.broadcasted_iota(jnp.int32, (n, n), 1)
    return ((r >= c) if lower else (r <= c)).astype(F32)


def _hgrn_intra_fwd(qs, k, v, b):
    C, SB = qs.shape[0], min(HGRN_SUB, qs.shape[0])
    ridx = lax.broadcasted_iota(jnp.int32, (SUBLANES, 1), 0)
    outs = []
    for i in range(C // SB):
        r0 = i * SB
        qi, ki, vi, bi = qs[r0 : r0 + SB], k[r0 : r0 + SB], v[r0 : r0 + SB], b[r0 : r0 + SB]
        ng = SB // SUBLANES
        qg = [qi[g * SUBLANES : (g + 1) * SUBLANES] for g in range(ng)]
        bg = [bi[g * SUBLANES : (g + 1) * SUBLANES] for g in range(ng)]
        accg = [jnp.zeros((SUBLANES, v.shape[1]), F32) for _ in range(ng)]
        for s in range(SB):
            gs, so = divmod(s, SUBLANES)
            k_s, v_s, b_s = ki[s : s + 1], vi[s : s + 1], bi[s : s + 1]
            for tg in range(gs, ng):
                if tg == gs:
                    mask = ridx >= so
                    w = jnp.where(mask, qg[tg] * k_s * jnp.exp(jnp.where(mask, bg[tg] - b_s, 0.0)), 0.0)
                else:
                    w = qg[tg] * k_s * jnp.exp(bg[tg] - b_s)
                accg[tg] = accg[tg] + jnp.sum(w, axis=1, keepdims=True) * v_s
        acc = jnp.concatenate(accg, axis=0)
        if i > 0:
            ref = bi[0:1]
            qt = qi * jnp.exp(bi - ref)
            kt = k[:r0] * jnp.exp(ref - b[:r0])
            acc = acc + _bdot(_mdot_nt(qt, kt), v[:r0])
        outs.append(acc)
    return jnp.concatenate(outs, axis=0)


def _hgrn_intra_bwd(qs, k, v, b, do):
    C, SB = qs.shape[0], min(HGRN_SUB, qs.shape[0])
    nb = C // SB
    ridx = lax.broadcasted_iota(jnp.int32, (SUBLANES, 1), 0)
    dq_p = [None] * nb
    dk_p = [jnp.zeros((SB, k.shape[1]), F32) for _ in range(nb)]
    dv_p = [jnp.zeros((SB, v.shape[1]), F32) for _ in range(nb)]
    for i in range(nb):
        r0 = i * SB
        qi, ki, vi, bi, doi = qs[r0 : r0 + SB], k[r0 : r0 + SB], v[r0 : r0 + SB], b[r0 : r0 + SB], do[r0 : r0 + SB]
        ng = SB // SUBLANES
        qg = [qi[g * SUBLANES : (g + 1) * SUBLANES] for g in range(ng)]
        bg = [bi[g * SUBLANES : (g + 1) * SUBLANES] for g in range(ng)]
        dog = [doi[g * SUBLANES : (g + 1) * SUBLANES] for g in range(ng)]
        dqg =[jnp.zeros((SUBLANES, k.shape[1]), F32) for _ in range(ng)]
        dkg = [jnp.zeros((SUBLANES, k.shape[1]), F32) for _ in range(ng)]
        dvg = [jnp.zeros((SUBLANES, v.shape[1]), F32) for _ in range(ng)]
        for s in range(SB):
            gs, so = divmod(s, SUBLANES)
            k_s, v_s, b_s = ki[s : s + 1], vi[s : s + 1], bi[s : s + 1]
            dk_s = jnp.zeros((SUBLANES, k.shape[1]), F32)
            dv_s = jnp.zeros((SUBLANES, v.shape[1]), F32)
            for tg in range(gs, ng):
                if tg == gs:
                    mask = ridx >= so
                    e = jnp.where(mask, jnp.exp(jnp.where(mask, bg[tg] - b_s, 0.0)), 0.0)
                else:
                    e = jnp.exp(bg[tg] - b_s)
                da = jnp.sum(dog[tg] * v_s, axis=1, keepdims=True)
                qe = qg[tg] * e
                a = jnp.sum(qe * k_s, axis=1, keepdims=True)
                dqg[tg] = dqg[tg] + da * (k_s * e)
                dk_s = dk_s + da * qe
                dv_s = dv_s + a * dog[tg]
            dkg[gs] = jnp.where(ridx == so, dkg[gs] + jnp.sum(dk_s, axis=0, keepdims=True), dkg[gs])
            dvg[gs] = jnp.where(ridx == so, dvg[gs] + jnp.sum(dv_s, axis=0, keepdims=True), dvg[gs])
        dqi = jnp.concatenate(dqg, axis=0)
        dki = jnp.concatenate(dkg, axis=0)
        dvi = jnp.concatenate(dvg, axis=0)
        if i > 0:
            ref = bi[0:1]
            eq = jnp.exp(bi - ref)
            ek = jnp.exp(ref - b[:r0])
            qt = qi * eq
            kt = k[:r0] * ek
            A = _mdot_nt(qt, kt)
            dA = _bdot_nt(doi, v[:r0])
            dvl = _bdot_tn(A, doi)
            dqi = dqi + _mdot(dA, kt) * eq
            dkl = _mdot_tn(dA, qt) * ek
            for j in range(i):
                dk_p[j] = dk_p[j] + dkl[j * SB : (j + 1) * SB]
                dv_p[j] = dv_p[j] + dvl[j * SB : (j + 1) * SB]
        dq_p[i] = dqi
        dk_p[i] = dk_p[i] + dki
        dv_p[i] = dv_p[i] + dvi
    return jnp.concatenate(dq_p, axis=0), jnp.concatenate(dk_p, axis=0), jnp.concatenate(dv_p, axis=0)


def _hgrn_fwd(proj, lb, gn, name):
    B, S, W = proj.shape
    HK = W // 4
    H = HK // HGRN_K
    C = min(HGRN_CHUNK, S)
    N = S // C

    HP = HGRN_PAR if H % HGRN_PAR == 0 else 1
    WP = HP * HGRN_K

    def body(q_ref, f_ref, i_ref, g_ref, lb_ref, gn_ref, og_ref, o_ref, st_ref):
        gn_v = gn_ref[...]
        tril = _tri(C, True)

        def chunk(n, sts):
            rows = pl.ds(pl.multiple_of(n * C, C), C)
            out = []
            for hh in range(HP):
                ln = slice(hh * HGRN_K, (hh + 1) * HGRN_K)
                st = sts[hh]
                qs, k, g = _hgrn_pre(q_ref[rows, ln], f_ref[rows, ln], lb_ref[:, ln])
                v = i_ref[rows, ln]
                b = _hdot(tril, g)
                st_ref[hh, n] = st
                o = _hgrn_intra_fwd(qs, k, v, b) + _bdot_nt(qs * jnp.exp(b), st)
                bl = b[C - 1 : C]
                out.append(st * jnp.exp(bl) + _bdot_tn(v, k * jnp.exp(bl - b)))
                o_ref[rows, ln] = o
                og_ref[rows, ln] = _hgrn_gate(o, g_ref[rows, ln], gn_v).astype(BF16)
            return tuple(out)

        lax.fori_loop(0, N, chunk, tuple(jnp.zeros((HGRN_K, HGRN_K), F32) for _ in range(HP)))

    col = lambda part: pl.BlockSpec((None, S, WP), lambda b, h: (b, 0, part * (H // HP) + h))
    return pl.pallas_call(
        body,
        grid=(B, H // HP),
        in_specs=[col(0), col(1), col(2), col(3), pl.BlockSpec((1, WP), lambda b, h: (0, h)), pl.BlockSpec((1, HGRN_K), lambda b, h: (0, 0))],
        out_specs=[col(0), col(0), pl.BlockSpec((None, HP, N, HGRN_K, HGRN_K), lambda b, h: (b, h, 0, 0, 0))],
        out_shape=[SDS((B, S, HK), BF16), SDS((B, S, HK), F32), SDS((B, H, N, HGRN_K, HGRN_K), F32)],
        name=name,
        compiler_params=_cparams("parallel", "parallel"),
    )(proj, proj, proj, proj, lb, gn)


def _hgrn_bwd(proj, lb, gn, o_pre, states, dog, name):
    B, S, W = proj.shape
    HK = W // 4
    H = HK // HGRN_K
    C = min(HGRN_CHUNK, S)
    N = S // C

    HP = HGRN_PAR if H % HGRN_PAR == 0 else 1
    WP = HP * HGRN_K

    def body(q_ref, f_ref, i_ref, g_ref, lb_ref, gn_ref, o_ref, st_ref, dog_ref, dq_ref, df_ref, di_ref, dg_ref, dlb_ref, dgn_ref):
        gn_v = gn_ref[...]
        tril = _tri(C, True)
        triu = _tri(C, False)

        def chunk(idx, carry):
            n = N - 1 - idx
            rows = pl.ds(pl.multiple_of(n * C, C), C)
            out = []
            for hh in range(HP):
                ln = slice(hh * HGRN_K, (hh + 1) * HGRN_K)
                dst, dlb, dgn = carry[hh]
                (qs, k, g), pre_vjp = jax.vjp(_hgrn_pre, q_ref[rows, ln], f_ref[rows, ln], lb_ref[:, ln])
                v = i_ref[rows, ln]
                _, gate_vjp = jax.vjp(_hgrn_gate, o_ref[rows, ln], g_ref[rows, ln], gn_v)
                do, dgg, dgn_c = gate_vjp(dog_ref[rows, ln])
                b = _hdot(tril, g)
                st0 = st_ref[hh, n]
                eb = jnp.exp(b)
                bl = b[C - 1 : C]
                ebl = jnp.exp(bl)
                ekb = jnp.exp(bl - b)
                qe = qs * eb
                kt = k * ekb
                dqs, dk, dv = _hgrn_intra_bwd(qs, k, v, b, do)
                dqs = dqs + _bdot(do, st0) * eb
                dk = dk + _bdot(v, dst) * ekb
                dv = dv + _bdot_nt(kt, dst)
                st1 = st0 * ebl + _bdot_tn(v, kt)
                dbl = jnp.sum(st1 * dst, axis=0, keepdims=True)
                dst = dst * ebl + _bdot_tn(do, qe)
                dgl = _hdot(triu, qs * dqs - k * dk) + dbl
                dq_pre, dfx, dlb_c = pre_vjp((dqs, dk, dgl))
                dq_ref[rows, ln] = dq_pre.astype(BF16)
                df_ref[rows, ln] = dfx.astype(BF16)
                di_ref[rows, ln] = dv.astype(BF16)
                dg_ref[rows, ln] = dgg.astype(BF16)
                out.append((dst, dlb + dlb_c, dgn + dgn_c))
            return tuple(out)

        zero = jnp.zeros((1, HGRN_K), F32)
        one = (jnp.zeros((HGRN_K, HGRN_K), F32), zero, zero)
        res = lax.fori_loop(0, N, chunk, tuple(one for _ in range(HP)))
        for hh in range(HP):
            dlb_ref[hh] = res[hh][1]
            dgn_ref[hh] = res[hh][2]

    col = lambda part: pl.BlockSpec((None, S, WP), lambda b, h: (b, 0, part * (H // HP) + h))
    vec = pl.BlockSpec((None, HP, 1, HGRN_K), lambda b, h: (b, h, 0, 0))
    return pl.pallas_call(
        body,
        grid=(B, H // HP),
        in_specs=[
            col(0), col(1), col(2), col(3),
            pl.BlockSpec((1, WP), lambda b, h: (0, h)),
            pl.BlockSpec((1, HGRN_K), lambda b, h: (0, 0)),
            col(0),
            pl.BlockSpec((None, HP, N, HGRN_K, HGRN_K), lambda b, h: (b, h, 0, 0, 0)),
            col(0),
        ],
        out_specs=[col(0), col(0), col(0), col(0), vec, vec],
        out_shape=[SDS((B, S, HK), BF16)] * 4 + [SDS((B, H, 1, HGRN_K), F32)] * 2,
        name=name,
        compiler_params=_cparams("parallel", "parallel"),
    )(proj, proj, proj, proj, lb, gn, o_pre, states, dog)


def _ada_fwd(c_all, w, b, name):
    Bg, D = c_all.shape
    L, _, n = w.shape

    def body(c_ref, w_ref, b_ref, o_ref):
        o_ref[...] = _bdot(jax.nn.silu(c_ref[...]), w_ref[...]) + b_ref[...]

    return pl.pallas_call(
        body,
        grid=(L,),
        in_specs=[
            pl.BlockSpec((Bg, D), lambda l: (0, 0)),
            pl.BlockSpec((None, D, n), lambda l: (l, 0, 0)),
            pl.BlockSpec((None, 1, n), lambda l: (l, 0, 0)),
        ],
        out_specs=pl.BlockSpec((None, Bg, n), lambda l: (l, 0, 0)),
        out_shape=SDS((L, Bg, n), F32),
        name=name,
        compiler_params=_cparams("parallel"),
    )(c_all, w, b)


def _ada_bwd(c_all, dmod, name):
    Bg, D = c_all.shape
    L, _, n = dmod.shape

    def body(c_ref, d_ref, dw_ref, db_ref):
        d = d_ref[...]
        dw_ref[...] = _bdot_tn(jax.nn.silu(c_ref[...]), d)
        db_ref[...] = jnp.sum(d, axis=0, keepdims=True)

    return pl.pallas_call(
        body,
        grid=(L,),
        in_specs=[pl.BlockSpec((Bg, D), lambda l: (0, 0)), pl.BlockSpec((None, Bg, n), lambda l: (l, 0, 0))],
        out_specs=[pl.BlockSpec((None, D, n), lambda l: (l, 0, 0)), pl.BlockSpec((None, 1, n), lambda l: (l, 0, 0))],
        out_shape=[SDS((L, D, n), F32), SDS((L, 1, n), F32)],
        name=name,
        compiler_params=_cparams("parallel"),
    )(c_all, dmod)


def _adamw(w, gs, m, v, name):
    shape = w.shape
    cols = shape[-1]
    rows = w.size // cols
    tr = rows
    for cand in (512, 256, 128, 64, 32, 16, 8):
        if rows % cand == 0 and cand * cols * 4 <= 2 * 1024 * 1024:
            tr = cand
            break
    as2d = lambda a: a.reshape(rows, cols)
    ng = len(gs)
    c1 = 1.0 / (1.0 - ADAM_B1**ADAM_STEP)
    c2 = 1.0 / (1.0 - ADAM_B2**ADAM_STEP)

    def body(*refs):
        w_ref, m_ref, v_ref = refs[0], refs[1], refs[2]
        g_refs = refs[3 : 3 + ng]
        g_out, d_out, m_out, v_out = refs[3 + ng :]
        g = g_refs[0][...].astype(F32)
        for r in g_refs[1:]:
            g = g + r[...].astype(F32)
        m_new = ADAM_B1 * m_ref[...] + (1.0 - ADAM_B1) * g
        v_new = ADAM_B2 * v_ref[...] + (1.0 - ADAM_B2) * jnp.square(g)
        g_out[...] = g
        m_out[...] = m_new
        v_out[...] = v_new
        d_out[...] = -ADAM_LR * ((m_new * c1) / (jnp.sqrt(v_new * c2) + ADAM_EPS) + ADAM_WD * w_ref[...])

    spec = pl.BlockSpec((tr, cols), lambda i: (i, 0))
    outs = pl.pallas_call(
        body,
        grid=(rows // tr,),
        in_specs=[spec] * (3 + ng),
        out_specs=[spec] * 4,
        out_shape=[SDS((rows, cols), F32)] * 4,
        name=name,
        compiler_params=_cparams("parallel"),
    )(as2d(w), as2d(m), as2d(v), *[as2d(g) for g in gs])
    return tuple(o.reshape(shape) for o in outs)


def _sum4(own, recv, name):
    shape = own.shape
    cols = shape[-1]
    rows = own.size // cols
    tr = rows
    for cand in (512, 256, 128, 64, 32, 16):
        if rows % cand == 0 and cand * cols * 4 <= 2 * 1024 * 1024:
            tr = cand
            break

    def body(own_ref, recv_ref, o_ref):
        acc = own_ref[...].astype(F32)
        for r in range(3):
            acc = acc + recv_ref[r].astype(F32)
        o_ref[...] = acc

    out = pl.pallas_call(
        body,
        grid=(rows // tr,),
        in_specs=[pl.BlockSpec((tr, cols), lambda i: (i, 0)), pl.BlockSpec((3, tr, cols), lambda i: (0, i, 0))],
        out_specs=pl.BlockSpec((tr, cols), lambda i: (i, 0)),
        out_shape=SDS((rows, cols), F32),
        name=name,
        compiler_params=_cparams("parallel"),
    )(own.reshape(rows, cols), recv.reshape(3, rows, cols))
    return out.reshape(shape)


def _my_place():
    return lax.axis_index("x"), lax.axis_index("y"), lax.axis_index("c")


def _flip(v, bit):
    return 1 - v if bit else v


def _allgather8(x, name):
    r, n = x.shape

    def body(x_ref, o_ref, send_sems, recv_sems, local_sem):
        mx, my, mc = _my_place()
        me = 4 * mx + 2 * my + mc
        mine = pltpu.make_async_copy(x_ref, o_ref.at[me], local_sem)
        mine.start()
        sends = []
        for rel in range(1, 8):
            peer = (_flip(mx, rel & 4), _flip(my, rel & 2), _flip(mc, rel & 1))
            cp = pltpu.make_async_remote_copy(
                src_ref=x_ref, dst_ref=o_ref.at[me], send_sem=send_sems.at[rel - 1], recv_sem=recv_sems.at[rel - 1],
                device_id=peer, device_id_type=MESH,
            )
            cp.start()
            sends.append(cp)
        for rel in range(1, 8):
            px, py, pc = _flip(mx, rel & 4), _flip(my, rel & 2), _flip(mc, rel & 1)
            pltpu.make_async_remote_copy(
                src_ref=x_ref, dst_ref=o_ref.at[4 * px + 2 * py + pc], send_sem=send_sems.at[rel - 1],
                recv_sem=recv_sems.at[rel - 1], device_id=(px, py, pc), device_id_type=MESH,
            ).wait_recv()
        for cp in sends:
            cp.wait_send()
        mine.wait()

    return pl.pallas_call(
        body,
        out_shape=SDS((8, r, n), x.dtype),
        in_specs=[pl.BlockSpec(memory_space=pl.ANY)],
        out_specs=pl.BlockSpec(memory_space=pl.ANY),
        scratch_shapes=[pltpu.SemaphoreType.DMA((7,)), pltpu.SemaphoreType.DMA((7,)), pltpu.SemaphoreType.DMA],
        name=name,
    )(x)


_HBM = pl.BlockSpec(memory_space=pl.ANY)


_SEM = pl.BlockSpec(memory_space=pltpu.SEMAPHORE)
_HBM_ONLY = pl.BlockSpec(memory_space=pltpu.HBM)
_EFFECT = pltpu.SideEffectType.DATAFLOW_SIDE_EFFECTING


def _in_hbm(a):
    return pltpu.with_memory_space_constraint(a, pltpu.HBM)


def _gather_start(lands, after, name):
    n = len(lands)

    def body(*refs):
        land = refs[:n]
        send_sems, recv_sems = refs[n + 1], refs[n + 2]
        token = refs[-1]
        mx, my, mc = _my_place()
        for i in range(n):
            for rel in range(1, 4):
                pltpu.make_async_remote_copy(
                    src_ref=land[i].at[2 * mx + my], dst_ref=land[i].at[2 * mx + my],
                    send_sem=send_sems.at[3 * i + rel - 1], recv_sem=recv_sems.at[3 * i + rel - 1],
                    device_id=(_flip(mx, rel & 2), _flip(my, rel & 1), mc), device_id_type=MESH,
                ).start()
        token[...] = jnp.zeros_like(token)

    outs = pl.pallas_call(
        body,
        name=name,
        out_shape=(
            pltpu.SemaphoreType.DMA((3 * n,)), pltpu.SemaphoreType.DMA((3 * n,)),
            *[pltpu.HBM(a.shape, a.dtype) for a in lands], SDS((8, LANES), F32),
        ),
        in_specs=[_HBM_ONLY] * n + [_HBM],
        out_specs=(_SEM, _SEM, *[_HBM_ONLY] * n, pl.BlockSpec(memory_space=pltpu.VMEM)),
        input_output_aliases={i: 2 + i for i in range(n)},
        compiler_params=pltpu.CompilerParams(has_side_effects=_EFFECT),
    )(*[_in_hbm(a) for a in lands], after)
    return outs[0], outs[1], list(outs[2 : 2 + n]), outs[-1]


def _gather_wait(send_sems, recv_sems, lands, after, name):
    n = len(lands)

    def body(*refs):
        land = refs[:n]
        s_sems, r_sems = refs[n], refs[n + 1]
        mx, my, mc = _my_place()
        for i in range(n):
            for rel in range(1, 4):
                px, py = _flip(mx, rel & 2), _flip(my, rel & 1)
                cp = pltpu.make_async_remote_copy(
                    src_ref=land[i].at[2 * mx + my], dst_ref=land[i].at[2 * px + py],
                    send_sem=s_sems.at[3 * i + rel - 1], recv_sem=r_sems.at[3 * i + rel - 1],
                    device_id=(px, py, mc), device_id_type=MESH,
                )
                cp.wait_send()
                cp.wait_recv()

    outs = pl.pallas_call(
        body,
        name=name,
        out_shape=tuple(pltpu.HBM(a.shape, a.dtype) for a in lands),
        in_specs=[_HBM_ONLY] * n + [_SEM, _SEM, _HBM],
        out_specs=[_HBM_ONLY] * n,
        input_output_aliases={i: i for i in range(n)},
        compiler_params=pltpu.CompilerParams(has_side_effects=_EFFECT),
    )(*lands, send_sems, recv_sems, after)
    return list(outs)


def _scatter_start(slabs, lands, places, name):
    n = len(slabs)

    def body(*refs):
        ins, land = refs[:n], refs[n : 2 * n]
        send_sems, recv_sems = refs[2 * n], refs[2 * n + 1]
        token = refs[-1]
        mx, my, mc = _my_place()
        for i in range(n):
            for rel in range(1, 4):
                px, py = _flip(mx, rel & 2), _flip(my, rel & 1)
                pltpu.make_async_remote_copy(
                    src_ref=ins[i].at[2 * px + py], dst_ref=land[i].at[rel - 1, places[i]],
                    send_sem=send_sems.at[3 * i + rel - 1], recv_sem=recv_sems.at[3 * i + rel - 1],
                    device_id=(px, py, mc), device_id_type=MESH,
                ).start()
        token[...] = jnp.zeros_like(token)

    outs = pl.pallas_call(
        body,
        name=name,
        out_shape=(
            pltpu.SemaphoreType.DMA((3 * n,)), pltpu.SemaphoreType.DMA((3 * n,)),
            *[pltpu.HBM(a.shape, a.dtype) for a in slabs], *[pltpu.HBM(a.shape, a.dtype) for a in lands],
            SDS((8, LANES), F32),
        ),
        in_specs=[_HBM_ONLY] * (2 * n),
        out_specs=(_SEM, _SEM, *[_HBM_ONLY] * (2 * n), pl.BlockSpec(memory_space=pltpu.VMEM)),
        input_output_aliases={i: 2 + i for i in range(2 * n)},
        compiler_params=pltpu.CompilerParams(has_side_effects=_EFFECT),
    )(*[_in_hbm(a) for a in slabs], *[_in_hbm(a) for a in lands])
    return outs[0], outs[1], list(outs[2 : 2 + n]), list(outs[2 + n : 2 + 2 * n]), outs[-1]


def _scatter_wait(send_sems, recv_sems, slabs, lands, places, after, name):
    n = len(slabs)

    def body(*refs):
        ins, land = refs[:n], refs[n : 2 * n]
        s_sems, r_sems = refs[2 * n], refs[2 * n + 1]
        mx, my, mc = _my_place()
        for i in range(n):
            for rel in range(1, 4):
                px, py = _flip(mx, rel & 2), _flip(my, rel & 1)
                cp = pltpu.make_async_remote_copy(
                    src_ref=ins[i].at[2 * px + py], dst_ref=land[i].at[rel - 1, places[i]],
                    send_sem=s_sems.at[3 * i + rel - 1], recv_sem=r_sems.at[3 * i + rel - 1],
                    device_id=(px, py, mc), device_id_type=MESH,
                )
                cp.wait_send()
                cp.wait_recv()

    outs = pl.pallas_call(
        body,
        name=name,
        out_shape=(*[pltpu.HBM(a.shape, a.dtype) for a in slabs], *[pltpu.HBM(a.shape, a.dtype) for a in lands]),
        in_specs=[_HBM_ONLY] * (2 * n) + [_SEM, _SEM, _HBM],
        out_specs=[_HBM_ONLY] * (2 * n),
        input_output_aliases={i: i for i in range(2 * n)},
        compiler_params=pltpu.CompilerParams(has_side_effects=_EFFECT),
    )(*slabs, *lands, send_sems, recv_sems, after)
    return list(outs[:n]), list(outs[n:])


def _swap_start(parts, after, name):
    n = len(parts)
    lands = [lax.empty(a.shape, a.dtype) for a in parts]

    def body(*refs):
        ins, land = refs[:n], refs[n : 2 * n]
        send_sems, recv_sems = refs[2 * n + 1], refs[2 * n + 2]
        mx, my, mc = _my_place()
        for i in range(n):
            pltpu.make_async_remote_copy(
                src_ref=ins[i], dst_ref=land[i], send_sem=send_sems.at[i], recv_sem=recv_sems.at[i],
                device_id=(mx, my, 1 - mc), device_id_type=MESH,
            ).start()

    outs = pl.pallas_call(
        body,
        name=name,
        out_shape=(
            pltpu.SemaphoreType.DMA((n,)), pltpu.SemaphoreType.DMA((n,)),
            *[pltpu.HBM(a.shape, a.dtype) for a in parts], *[pltpu.HBM(a.shape, a.dtype) for a in lands],
        ),
        in_specs=[_HBM_ONLY] * (2 * n) + [_HBM],
        out_specs=(_SEM, _SEM, *[_HBM_ONLY] * (2 * n)),
        input_output_aliases={i: 2 + i for i in range(2 * n)},
        compiler_params=pltpu.CompilerParams(has_side_effects=_EFFECT),
    )(*[_in_hbm(a) for a in parts], *[_in_hbm(a) for a in lands], after)
    return outs[0], outs[1], list(outs[2 : 2 + n]), list(outs[2 + n :])


def _swap_wait(send_sems, recv_sems, parts, lands, after, name):
    n = len(parts)

    def body(*refs):
        ins, land = refs[:n], refs[n : 2 * n]
        s_sems, r_sems = refs[2 * n], refs[2 * n + 1]
        mx, my, mc = _my_place()
        for i in range(n):
            cp = pltpu.make_async_remote_copy(
                src_ref=ins[i], dst_ref=land[i], send_sem=s_sems.at[i], recv_sem=r_sems.at[i],
                device_id=(mx, my, 1 - mc), device_id_type=MESH,
            )
            cp.wait_send()
            cp.wait_recv()

    outs = pl.pallas_call(
        body,
        name=name,
        out_shape=(*[pltpu.HBM(a.shape, a.dtype) for a in parts], *[pltpu.HBM(a.shape, a.dtype) for a in lands]),
        in_specs=[_HBM_ONLY] * (2 * n) + [_SEM, _SEM, _HBM],
        out_specs=[_HBM_ONLY] * (2 * n),
        input_output_aliases={i: i for i in range(2 * n)},
        compiler_params=pltpu.CompilerParams(has_side_effects=_EFFECT),
    )(*parts, *lands, send_sems, recv_sems, after)
    return list(outs[:n]), list(outs[n:])


def _pad_rows(a, rows):
    return jnp.pad(a, ((0, rows - a.shape[0]), (0, 0)))


def kernel(x, c, positions, mla_w_in, mla_q_norm, mla_w_qb, mla_kv_norm, mla_w_kvb, mla_w_o, hgrn_lb, hgrn_w_in, hgrn_g_norm, hgrn_w_o, ffn_w_in, ffn_w_out, ada_w, ada_b, ln_g, ln_b, loss_target, m_mla_w_in, m_mla_q_norm, m_mla_w_qb, m_mla_kv_norm, m_mla_w_kvb, m_mla_w_o, m_hgrn_lb, m_hgrn_w_in, m_hgrn_g_norm, m_hgrn_w_o, m_ffn_w_in, m_ffn_w_out, m_ada_w, m_ada_b, m_ln_g, m_ln_b, v_mla_w_in, v_mla_q_norm, v_mla_w_qb, v_mla_kv_norm, v_mla_w_kvb, v_mla_w_o, v_hgrn_lb, v_hgrn_w_in, v_hgrn_g_norm, v_hgrn_w_o, v_ffn_w_in, v_ffn_w_out, v_ada_w, v_ada_b, v_ln_g, v_ln_b):
    B, S, D = x.shape
    T = B * S
    depth = ada_w.shape[0]
    n_mla, n_hgrn = mla_w_in.shape[0], hgrn_w_in.shape[0]
    n_sub = 2 * depth
    alpha = (2.0 * depth) ** 0.25
    mx, my, mc = _my_place()
    me = 4 * mx + 2 * my + mc
    k_me = 2 * mx + my
    Bg = 8 * B
    HK = hgrn_w_o.shape[1] * 4
    dq = D // 4

    lbw = hgrn_lb.shape[1]
    first = jnp.zeros((8, max(D, 4 * lbw)), F32)
    first = first.at[:B, :D].set(c).at[B : B + n_hgrn, :lbw].set(hgrn_lb)
    first_all = _allgather8(first, "gather_cond")
    c_all = first_all[:, :B, :D].reshape(Bg, D)
    lb_logits = jnp.concatenate([first_all[2 * k, B : B + n_hgrn, :lbw] for k in range(4)], axis=1)

    def lower_bounds_fn(logits):
        soft = jax.nn.softmax(logits, axis=0)
        return jnp.cumsum(soft, axis=0) - soft[0]

    lower_bounds, lower_bounds_vjp = jax.vjp(lower_bounds_fn, lb_logits)

    n_ada = ada_w.shape[-1]
    mod_part = _ada_fwd(c_all, ada_w.reshape(n_sub, D, n_ada), ada_b.reshape(n_sub, 1, n_ada), "ada_fwd")
    mod_all = _allgather8(mod_part.reshape(n_sub * Bg, n_ada), "gather_mod").reshape(8, n_sub, Bg, n_ada)
    mod = jnp.concatenate([mod_all[2 * k] for k in range(4)], axis=-1)
    mod = lax.dynamic_slice_in_dim(mod, me * B, B, axis=1)
    shift = [mod[j, :, None, :D] for j in range(n_sub)]
    scale = [mod[j, :, None, D : 2 * D] for j in range(n_sub)]
    gate = [mod[j, :, None, 2 * D :] for j in range(n_sub)]

    ln_rows = 2 * n_sub
    ln_local = _pad_rows(jnp.concatenate([ln_g.reshape(n_sub, dq), ln_b.reshape(n_sub, dq)], axis=0), -(-ln_rows // 8) * 8)
    ln_pad = jnp.zeros((ln_local.shape[0], -(-dq // LANES) * LANES), F32).at[:, :dq].set(ln_local)
    ln_all = _allgather8(ln_pad, "gather_ln")
    ln_full = jnp.concatenate([ln_all[2 * k, :ln_rows, :dq] for k in range(4)], axis=1)
    lng = [ln_full[j][None, :] for j in range(n_sub)]
    lnb = [ln_full[n_sub + j][None, :] for j in range(n_sub)]

    main = dict(mla_w_in=mla_w_in, mla_w_qb=mla_w_qb, mla_w_kvb=mla_w_kvb, mla_w_o=mla_w_o, hgrn_w_in=hgrn_w_in,
                hgrn_w_o=hgrn_w_o, ffn_w_in=ffn_w_in, ffn_w_out=ffn_w_out)
    names = list(main)

    def group_kinds(layer, part):
        if part:
            return [("ffn_w_in", layer), ("ffn_w_out", layer)]
        mixer = ["mla_w_in", "mla_w_qb", "mla_w_kvb", "mla_w_o"] if layer % 2 == 0 else ["hgrn_w_in", "hgrn_w_o"]
        return [(k, layer // 2) for k in mixer]

    gathers = {}
    after = mod_all[0, 0, :8, :LANES] + ln_all[0, :8, :LANES]
    for layer in range(depth):
        for part in range(2):
            lands = [lax.dynamic_update_index_in_dim(lax.empty((4,) + main[k].shape[1:], BF16), main[k][i].astype(BF16), k_me, 0)
                     for k, i in group_kinds(layer, part)]
            ssem, rsem, lands, after = _gather_start(lands, after, f"gather_start_l{layer}p{part}")
            gathers[layer, part] = (ssem, rsem, lands)
    scale[0] = scale[0] + after[0, 0]

    def row_w(g):
        return g.reshape(1, g.shape[0] * g.shape[1], g.shape[2])

    def full_w_in(g):
        return jnp.transpose(g, (1, 0, 2)).reshape(1, g.shape[1], 4 * g.shape[2])

    ang = positions.astype(F32)[..., None] * (ROPE_THETA ** (-jnp.arange(0, QK_ROPE, 2, dtype=F32) / QK_ROPE))
    cos, sin = jnp.cos(ang), jnp.sin(ang)

    gq = [mla_q_norm[j][None, :] for j in range(n_mla)]
    gkv = [mla_kv_norm[j][None, :] for j in range(n_mla)]
    gn = [hgrn_g_norm[j][None, :] for j in range(n_hgrn)]

    def r2(a):
        return a.reshape(T, a.shape[-1])

    def r3(a):
        return a.reshape(B, S, a.shape[-1])

    saved = []
    xs = x
    for layer in range(depth):
        j = layer // 2
        sub = 2 * layer
        tag = f"l{layer}"
        ssem, rsem, lands = gathers[layer, 0]
        lands = _gather_wait(ssem, rsem, lands, xs if layer else scale[0], f"gather_wait_{tag}p0")
        wl = {k: g for (k, _), g in zip(group_kinds(layer, 0), lands)}
        if layer == 0:
            h = _modulate(xs, scale[sub], shift[sub], f"mod_{tag}a")
        if layer % 2 == 0:
            wl["mla_w_in"] = full_w_in(wl["mla_w_in"])
            proj = r3(_mm_nn(r2(h), wl["mla_w_in"], F32, f"mla_in_{tag}"))
            qn, kvn = _mla_mid_fwd(proj, gq[j], gkv[j], f"mla_mid_{tag}")
            q = r3(_mm_nn(r2(qn), wl["mla_w_qb"], F32, f"mla_qb_{tag}"))
            kv = r3(_mm_nn(r2(kvn), wl["mla_w_kvb"], F32, f"mla_kvb_{tag}"))
            qh, kh, vh = _mla_prep_fwd(q, kv, proj, cos, sin, f"mla_prep_{tag}")
            o, lse = _attn_fwd(qh, kh, vh, f"attn_{tag}")
            wl["mla_w_o"] = row_w(wl["mla_w_o"])
            y = r3(_mm_nn(r2(o), wl["mla_w_o"], F32, f"mla_o_{tag}"))
            mix = (h, proj, qn, kvn, qh, kh, vh, o, lse)
        else:
            proj = r3(_mm_nn(r2(h), wl["hgrn_w_in"], F32, f"hgrn_in_{tag}"))
            og, o_pre, states = _hgrn_fwd(proj, lower_bounds[j][None, :], gn[j], f"hgrn_{tag}")
            wl["hgrn_w_o"] = row_w(wl["hgrn_w_o"])
            y = r3(_mm_nn(r2(og), wl["hgrn_w_o"], F32, f"hgrn_o_{tag}"))
            mix = (h, proj, og, o_pre, states)
        x1, h2 = _ln_mod_fwd(alpha, xs, y, gate[sub], lng[sub], lnb[sub], scale[sub + 1], shift[sub + 1], f"ln_{tag}a")
        ssem, rsem, lands = gathers[layer, 1]
        lands = _gather_wait(ssem, rsem, lands, x1, f"gather_wait_{tag}p1")
        wl.update({k: g for (k, _), g in zip(group_kinds(layer, 1), lands)})
        a, ug, uu = [r3(t_) for t_ in _ffn_in(r2(h2), wl["ffn_w_in"], f"ffn_in_{tag}")]
        wl["ffn_w_out"] = row_w(wl["ffn_w_out"])
        y2 = r3(_mm_nn(r2(a), wl["ffn_w_out"], F32, f"ffn_out_{tag}"))
        if layer + 1 < depth:
            x2, h_next = _ln_mod_fwd(alpha, x1, y2, gate[sub + 1], lng[sub + 1], lnb[sub + 1], scale[sub + 2], shift[sub + 2], f"ln_{tag}b")
        else:
            x2, h_next = _ln_fwd(alpha, x1, y2, gate[sub + 1], lng[sub + 1], lnb[sub + 1], f"ln_{tag}b"), None
        saved.append((xs, y, x1, y2, mix, h2, ug, uu, a, wl))
        xs, h = x2, h_next

    loss_local, dout = _loss_head(xs, loss_target, "loss_head")
    loss = lax.psum(loss_local, ("x", "y", "c"))

    gw = {k: [None] * main[k].shape[0] for k in names}
    land = {k: lax.empty((3,) + main[k].shape, BF16) for k in names}
    scatters = []
    d_shift, d_scale, d_gate = [None] * n_sub, [None] * n_sub, [None] * n_sub
    d_lng, d_lnb = [None] * n_sub, [None] * n_sub
    d_gq, d_gkv, d_gn, d_lbnd = [None] * n_mla, [None] * n_mla, [None] * n_hgrn, [None] * n_hgrn

    def rows4(g):
        return g.reshape(4, g.shape[1] // 4, g.shape[2])

    def scatter_kinds(layer, part):
        if part == 1 or layer % 2:
            return group_kinds(layer, part)
        mixer = group_kinds(layer, 0)
        return mixer[:1] if part == 0 else mixer[1:]

    def start_scatter(layer, part, params, at):
        kinds = scatter_kinds(layer, part)
        ssem, rsem, slabs_t, lands_t, token = _scatter_start(
            [gw[k][i] for k, i in kinds], [land[k] for k, _ in kinds], [i for _, i in kinds], f"scatter_start_l{layer}p{part}")
        for (k, i), s_t, l_t in zip(kinds, slabs_t, lands_t):
            gw[k][i], land[k] = s_t, l_t
        scatters.append((layer, part, ssem, rsem))
        if params is not None:
            params[at] = params[at] + token[0, 0]

    for layer in reversed(range(depth)):
        j = layer // 2
        sub = 2 * layer
        tag = f"l{layer}"
        xs, y, x1, y2, mix, h2, ug, uu, a, wl = saved[layer]
        if layer + 1 == depth:
            dxr, dy2, d_gate[sub + 1], d_lng[sub + 1], d_lnb[sub + 1] = _ln_bwd(
                alpha, dout, x1, y2, gate[sub + 1], lng[sub + 1], lnb[sub + 1], f"ln_bwd_{tag}b")
        else:
            dxr, dy2, d_gate[sub + 1], d_lng[sub + 1], d_lnb[sub + 1], d_scale[sub + 2], d_shift[sub + 2] = _ln_mod_bwd(
                alpha, dh, dxr, scale[sub + 2], x1, y2, gate[sub + 1], lng[sub + 1], lnb[sub + 1], f"ln_bwd_{tag}b")
        da = r3(_mm_nt(r2(dy2), wl["ffn_w_out"], F32, f"ffn_out_dx_{tag}"))
        gw["ffn_w_out"][layer] = rows4(_mm_tn(r2(a), r2(dy2), 1, BF16, f"ffn_out_dw_{tag}"))
        du = _swiglu_bwd(ug, uu, da, f"swiglu_bwd_{tag}")
        dh2 = r3(_mm_nt(r2(du), wl["ffn_w_in"], F32, f"ffn_in_dx_{tag}"))
        gw["ffn_w_in"][layer] = _mm_tn(r2(h2), r2(du), 4, BF16, f"ffn_in_dw_{tag}")
        start_scatter(layer, 1, gate, sub)
        dxr, dy, d_gate[sub], d_lng[sub], d_lnb[sub], d_scale[sub + 1], d_shift[sub + 1] = _ln_mod_bwd(
            alpha, dh2, dxr, scale[sub + 1], xs, y, gate[sub], lng[sub], lnb[sub], f"ln_bwd_{tag}a")
        if layer % 2 == 0:
            h, proj, qn, kvn, qh, kh, vh, o, lse = mix
            do = r3(_mm_nt(r2(dy), wl["mla_w_o"], BF16, f"mla_o_dx_{tag}"))
            gw["mla_w_o"][j] = rows4(_mm_tn(r2(o), r2(dy), 1, BF16, f"mla_o_dw_{tag}"))
            dqh, dkh, dvh = _attn_bwd(qh, kh, vh, o, do, lse, f"attn_bwd_{tag}")
            dq_, dkv_, dkr = _mla_prep_bwd(dqh, dkh, dvh, cos, sin, f"mla_prep_bwd_{tag}")
            dqn = r3(_mm_nt(r2(dq_), wl["mla_w_qb"], F32, f"mla_qb_dx_{tag}"))
            gw["mla_w_qb"][j] = _mm_tn(r2(qn), r2(dq_), 4, BF16, f"mla_qb_dw_{tag}")
            dkvn = r3(_mm_nt(r2(dkv_), wl["mla_w_kvb"], F32, f"mla_kvb_dx_{tag}"))
            gw["mla_w_kvb"][j] = _mm_tn(r2(kvn), r2(dkv_), 4, BF16, f"mla_kvb_dw_{tag}")
            start_scatter(layer, 2, gq, j)
            dproj, dgq_, dgkv_ = _mla_mid_bwd(proj, dqn, dkvn, dkr, gq[j], gkv[j], f"mla_mid_bwd_{tag}")
            d_gq[j], d_gkv[j] = dgq_.sum(0), dgkv_.sum(0)
            dh = r3(_mm_nt(r2(dproj), wl["mla_w_in"], F32, f"mla_in_dx_{tag}"))
            gwin = _mm_tn(r2(h), r2(dproj), 1, BF16, f"mla_in_dw_{tag}")[0]
            gw["mla_w_in"][j] = jnp.transpose(gwin.reshape(gwin.shape[0], 4, gwin.shape[1] // 4), (1, 0, 2))
        else:
            h, proj, og, o_pre, states = mix
            dog = r3(_mm_nt(r2(dy), wl["hgrn_w_o"], F32, f"hgrn_o_dx_{tag}"))
            gw["hgrn_w_o"][j] = rows4(_mm_tn(r2(og), r2(dy), 1, BF16, f"hgrn_o_dw_{tag}"))
            dq_, df_, di_, dg_, dlb_, dgn_ = _hgrn_bwd(proj, lower_bounds[j][None, :], gn[j], o_pre, states, dog, f"hgrn_bwd_{tag}")
            dproj = jnp.concatenate([dq_, df_, di_, dg_], axis=-1)
            d_lbnd[j] = dlb_.sum(0).reshape(1, HK)
            d_gn[j] = dgn_.sum((0, 1))
            dh = r3(_mm_nt(r2(dproj), wl["hgrn_w_in"], F32, f"hgrn_in_dx_{tag}"))
            gw["hgrn_w_in"][j] = _mm_tn(r2(h), r2(dproj), 4, BF16, f"hgrn_in_dw_{tag}")
        start_scatter(layer, 0, gate if layer else None, sub - 1)
    grad_x, d_scale[0], d_shift[0] = _mod_bwd(dh, dxr, x, scale[0], "mod_bwd_l0a")

    for layer, part, ssem, rsem in scatters:
        kinds = scatter_kinds(layer, part)
        slabs_t, lands_t = _scatter_wait(
            ssem, rsem, [gw[k][i] for k, i in kinds], [land[k] for k, _ in kinds], [i for _, i in kinds], grad_x,
            f"scatter_wait_l{layer}p{part}")
        for (k, i), s_t, l_t in zip(kinds, slabs_t, lands_t):
            gw[k][i], land[k] = s_t, l_t
    sums = [_sum4(jnp.stack([lax.dynamic_index_in_dim(g, k_me, 0, keepdims=False) for g in gw[k]]), land[k], f"sum4_{k}")
            for k in names]

    dmod = jnp.stack([jnp.concatenate([d_shift[s_][:, 0], d_scale[s_][:, 0], d_gate[s_][:, 0]], axis=-1) for s_ in range(n_sub)])
    dmod_rows = _pad_rows(dmod.reshape(n_sub * B, 3 * D), -(-n_sub * B // 8) * 8)
    dmod_all = _allgather8(dmod_rows, "gather_dmod")[:, : n_sub * B].reshape(8, n_sub, B, 3 * D)
    dmod_all = jnp.transpose(dmod_all, (1, 0, 2, 3)).reshape(n_sub, Bg, 3 * D)
    dmod_mine = lax.dynamic_slice_in_dim(dmod_all, k_me * n_ada, n_ada, axis=2)
    g_ada_w, g_ada_b = _ada_bwd(c_all, dmod_mine, "ada_bwd")
    g_ada_w = g_ada_w.reshape(ada_w.shape)
    g_ada_b = g_ada_b.reshape(ada_b.shape)

    small = [jnp.stack(d_gq).reshape(-1), jnp.stack(d_gkv).reshape(-1), jnp.stack(d_gn).reshape(-1),
             jnp.stack(d_lbnd).reshape(-1), jnp.stack([d.sum(0) for d in d_lng]).reshape(-1),
             jnp.stack([d.sum(0) for d in d_lnb]).reshape(-1)]
    sizes = [s_.shape[0] for s_ in small]
    flat = jnp.concatenate(small)
    rows_small = -(-flat.shape[0] // (8 * LANES)) * 8
    flat = jnp.pad(flat, (0, rows_small * LANES - flat.shape[0])).reshape(rows_small, LANES)
    tot = _allgather8(flat, "gather_small")
    acc = tot[0]
    for d in range(1, 8):
        acc = acc + tot[d]
    acc = acc.reshape(-1)
    offs = [0]
    for s_ in sizes:
        offs.append(offs[-1] + s_)
    g_q_norm = acc[offs[0] : offs[1]].reshape(mla_q_norm.shape)
    g_kv_norm = acc[offs[1] : offs[2]].reshape(mla_kv_norm.shape)
    g_g_norm = acc[offs[2] : offs[3]].reshape(hgrn_g_norm.shape)
    g_lbnd = acc[offs[3] : offs[4]].reshape(n_hgrn, HK)
    g_lb_full = lower_bounds_vjp(g_lbnd)[0]
    g_hgrn_lb = lax.dynamic_slice_in_dim(g_lb_full, k_me * lbw, lbw, axis=1)
    g_lng = lax.dynamic_slice_in_dim(acc[offs[4] : offs[5]].reshape(n_sub, D), k_me * dq, dq, axis=1).reshape(ln_g.shape)
    g_lnb = lax.dynamic_slice_in_dim(acc[offs[5] : offs[6]].reshape(n_sub, D), k_me * dq, dq, axis=1).reshape(ln_b.shape)

    weights = dict(mla_w_in=mla_w_in, mla_q_norm=mla_q_norm, mla_w_qb=mla_w_qb, mla_kv_norm=mla_kv_norm, mla_w_kvb=mla_w_kvb,
                   mla_w_o=mla_w_o, hgrn_lb=hgrn_lb, hgrn_w_in=hgrn_w_in, hgrn_g_norm=hgrn_g_norm, hgrn_w_o=hgrn_w_o,
                   ffn_w_in=ffn_w_in, ffn_w_out=ffn_w_out, ada_w=ada_w, ada_b=ada_b, ln_g=ln_g, ln_b=ln_b)
    moms = dict(mla_w_in=(m_mla_w_in, v_mla_w_in), mla_q_norm=(m_mla_q_norm, v_mla_q_norm), mla_w_qb=(m_mla_w_qb, v_mla_w_qb),
                mla_kv_norm=(m_mla_kv_norm, v_mla_kv_norm), mla_w_kvb=(m_mla_w_kvb, v_mla_w_kvb), mla_w_o=(m_mla_w_o, v_mla_w_o),
                hgrn_lb=(m_hgrn_lb, v_hgrn_lb), hgrn_w_in=(m_hgrn_w_in, v_hgrn_w_in), hgrn_g_norm=(m_hgrn_g_norm, v_hgrn_g_norm),
                hgrn_w_o=(m_hgrn_w_o, v_hgrn_w_o), ffn_w_in=(m_ffn_w_in, v_ffn_w_in), ffn_w_out=(m_ffn_w_out, v_ffn_w_out),
                ada_w=(m_ada_w, v_ada_w), ada_b=(m_ada_b, v_ada_b), ln_g=(m_ln_g, v_ln_g), ln_b=(m_ln_b, v_ln_b))
    grads = dict(mla_q_norm=(g_q_norm,), mla_kv_norm=(g_kv_norm,), hgrn_lb=(g_hgrn_lb,), hgrn_g_norm=(g_g_norm,),
                 ada_w=(g_ada_w,), ada_b=(g_ada_b,), ln_g=(g_lng,), ln_b=(g_lnb,))

    def adamw(k):
        return _adamw(weights[k], [g_.reshape(weights[k].shape) for g_ in grads[k]], moms[k][0], moms[k][1], f"adamw_{k}")

    ssem, rsem, sums, others = _swap_start(sums, tot[0, :8] + dmod_all[0, :8, :LANES], "swap_start")
    res = {k: adamw(k) for k in grads}
    sums, others = _swap_wait(ssem, rsem, sums, others, res["ada_w"][1], "swap_wait")
    grads.update({k: (a_, b_) for k, a_, b_ in zip(names, sums, others)})
    res.update({k: adamw(k) for k in names})
    order = list(weights)
    return (loss, grad_x, *[res[k][0] for k in order], *[res[k][1] for k in order], *[res[k][2] for k in order],
            *[res[k][3] for k in order])
```

```python
import functools

import jax
import jax.numpy as jnp
from jax import lax
from jax.experimental import pallas as pl
from jax.experimental.pallas import tpu as pltpu

F32 = jnp.float32
BF16 = jnp.bfloat16
SDS = jax.ShapeDtypeStruct
MESH = pl.DeviceIdType.MESH
HI = lax.Precision.HIGHEST
MID = lax.Precision.HIGH

MLA_HEADS, QK_NOPE, QK_ROPE, V_HEAD = 16, 64, 32, 64
Q_LORA, KV_LORA = 768, 256
QK_DIM = QK_NOPE + QK_ROPE
ROPE_THETA = 10000.0
HGRN_K = 128
HGRN_CHUNK = 128
HGRN_SUB = 32
HGRN_PAR = 2
LN_EPS, RMS_EPS = 1e-5, 1e-6
ADAM_LR, ADAM_B1, ADAM_B2, ADAM_EPS, ADAM_WD, ADAM_STEP = 0.001, 0.9, 0.999, 1e-08, 0.01, 10
NEG = -1e30

VMEM_LIMIT_BYTES = 56 * 1024 * 1024
RESIDENT_WEIGHT_BYTES = 12 * 1024 * 1024
LANES = 128
SUBLANES = 8


def _cparams(*sem):
    return pltpu.CompilerParams(dimension_semantics=sem if sem else None, vmem_limit_bytes=VMEM_LIMIT_BYTES)


def _pick_tile(n, cap):
    best = 0
    for t in range(LANES, min(n, cap) + 1, LANES):
        if n % t == 0:
            best = t
    return best if best else n


def _bdot(a, b):
    return jnp.dot(a.astype(BF16), b.astype(BF16), preferred_element_type=F32)


def _bdot_nt(a, b):
    return lax.dot_general(a.astype(BF16), b.astype(BF16), (((1,), (1,)), ((), ())), preferred_element_type=F32)


def _bdot_tn(a, b):
    return lax.dot_general(a.astype(BF16), b.astype(BF16), (((0,), (0,)), ((), ())), preferred_element_type=F32)


def _hdot(a, b):
    return jnp.dot(a, b, precision=HI, preferred_element_type=F32)


def _mdot(a, b):
    return jnp.dot(a, b, precision=MID, preferred_element_type=F32)


def _mdot_nt(a, b):
    return lax.dot_general(a, b, (((1,), (1,)), ((), ())), precision=MID, preferred_element_type=F32)


def _mdot_tn(a, b):
    return lax.dot_general(a, b, (((0,), (0,)), ((), ())), precision=MID, preferred_element_type=F32)


def _mm_nn(a, w, out_dtype, name):
    M, K = a.shape
    G, _, n = w.shape
    tm = min(512, M)
    tn = _pick_tile(n, 1536)
    nps = n // tn

    if G > 1 and w.size * 2 <= RESIDENT_WEIGHT_BYTES and n % LANES == 0:
        def body_all(a_ref, w_ref, o_ref):
            av = a_ref[...]
            for s in range(G):
                o_ref[:, s * n : (s + 1) * n] = _bdot(av, w_ref[s]).astype(o_ref.dtype)

        return pl.pallas_call(
            body_all,
            grid=(M // tm,),
            in_specs=[pl.BlockSpec((tm, K), lambda i: (i, 0)), pl.BlockSpec((G, K, n), lambda i: (0, 0, 0))],
            out_specs=pl.BlockSpec((tm, G * n), lambda i: (i, 0)),
            out_shape=SDS((M, G * n), out_dtype),
            name=name,
            compiler_params=_cparams("parallel"),
        )(a, w)

    def body(a_ref, w_ref, o_ref):
        o_ref[...] = _bdot(a_ref[...], w_ref[...]).astype(o_ref.dtype)

    return pl.pallas_call(
        body,
        grid=(G * nps, M // tm),
        in_specs=[
            pl.BlockSpec((tm, K), lambda j, i: (i, 0)),
            pl.BlockSpec((None, K, tn), lambda j, i: (j // nps, 0, j % nps)),
        ],
        out_specs=pl.BlockSpec((tm, tn), lambda j, i: (i, j)),
        out_shape=SDS((M, G * n), out_dtype),
        name=name,
        compiler_params=_cparams("parallel", "parallel"),
    )(a, w)


def _mm_nt(a, w, out_dtype, name):
    M = a.shape[0]
    G, K, n = w.shape
    tm = min(512, M)
    tk = _pick_tile(K, 1536)

    if w.size * 2 <= RESIDENT_WEIGHT_BYTES:
        def body_all(a_ref, w_ref, o_ref):
            acc = _bdot_nt(a_ref[:, :n], w_ref[0])
            for s in range(1, G):
                acc = acc + _bdot_nt(a_ref[:, s * n : (s + 1) * n], w_ref[s])
            o_ref[...] = acc.astype(o_ref.dtype)

        return pl.pallas_call(
            body_all,
            grid=(M // tm,),
            in_specs=[pl.BlockSpec((tm, G * n), lambda i: (i, 0)), pl.BlockSpec((G, K, n), lambda i: (0, 0, 0))],
            out_specs=pl.BlockSpec((tm, K), lambda i: (i, 0)),
            out_shape=SDS((M, K), out_dtype),
            name=name,
            compiler_params=_cparams("parallel"),
        )(a, w)

    def body(a_ref, w_ref, o_ref, acc_ref):
        s = pl.program_id(2)

        @pl.when(s == 0)
        def _():
            acc_ref[...] = jnp.zeros_like(acc_ref)

        acc_ref[...] += _bdot_nt(a_ref[...], w_ref[...])

        @pl.when(s == G - 1)
        def _():
            o_ref[...] = acc_ref[...].astype(o_ref.dtype)

    return pl.pallas_call(
        body,
        grid=(K // tk, M // tm, G),
        in_specs=[
            pl.BlockSpec((tm, n), lambda kb, i, s: (i, s)),
            pl.BlockSpec((None, tk, n), lambda kb, i, s: (s, kb, 0)),
        ],
        out_specs=pl.BlockSpec((tm, tk), lambda kb, i, s: (i, kb)),
        out_shape=SDS((M, K), out_dtype),
        scratch_shapes=[pltpu.VMEM((tm, tk), F32)],
        name=name,
        compiler_params=_cparams("parallel", "parallel", "arbitrary"),
    )(a, w)


def _mm_tn(a, d, G, out_dtype, name):
    T, K = a.shape
    n = d.shape[1] // G
    tk = _pick_tile(K, 256)
    tn = _pick_tile(n, 1536)
    nps = n // tn

    def body(a_ref, d_ref, o_ref):
        o_ref[...] = _bdot_tn(a_ref[...], d_ref[...]).astype(o_ref.dtype)

    return pl.pallas_call(
        body,
        grid=(G * nps, K // tk),
        in_specs=[
            pl.BlockSpec((T, tk), lambda j, i: (0, i)),
            pl.BlockSpec((T, tn), lambda j, i: (0, j)),
        ],
        out_specs=pl.BlockSpec((None, tk, tn), lambda j, i: (j // nps, i, j % nps)),
        out_shape=SDS((G, K, n), out_dtype),
        name=name,
        compiler_params=_cparams("parallel", "parallel"),
    )(a, d)


def _rows_call(body, name, B, S, ins, outs, ts=256):
    ts = min(ts, S)
    in_specs, args = [], []
    for arr, kind in ins:
        W = arr.shape[-1]
        if kind == "row":
            in_specs.append(pl.BlockSpec((None, ts, W), lambda b, s: (b, s, 0)))
        elif kind == "ex":
            in_specs.append(pl.BlockSpec((None, 1, W), lambda b, s: (b, 0, 0)))
        else:
            in_specs.append(pl.BlockSpec((1, W), lambda b, s: (0, 0)))
        args.append(arr)
    out_specs, out_shape = [], []
    for W, dt, kind in outs:
        if kind == "row":
            out_specs.append(pl.BlockSpec((None, ts, W), lambda b, s: (b, s, 0)))
            out_shape.append(SDS((B, S, W), dt))
        else:
            out_specs.append(pl.BlockSpec((None, 1, W), lambda b, s: (b, 0, 0)))
            out_shape.append(SDS((B, 1, W), dt))
    return pl.pallas_call(
        body,
        grid=(B, S // ts),
        in_specs=in_specs,
        out_specs=out_specs,
        out_shape=out_shape,
        name=name,
        compiler_params=_cparams("parallel", "arbitrary"),
    )(*args)


def _acc(ref, val):
    @pl.when(pl.program_id(1) == 0)
    def _():
        ref[...] = jnp.zeros_like(ref)

    ref[...] += val


def _mod_fn(x, sc, sh):
    return x * (1.0 + sc) + sh


def _ln_fn(alpha, x, y, gate, g, b):
    z = alpha * x + (1.0 + gate) * y
    mu = jnp.mean(z, -1, keepdims=True)
    var = jnp.mean(jnp.square(z - mu), -1, keepdims=True)
    return (z - mu) * lax.rsqrt(var + LN_EPS) * g + b


def _modulate(x, sc, sh, name):
    B, S, D = x.shape

    def body(x_ref, sc_ref, sh_ref, h_ref):
        h_ref[...] = _mod_fn(x_ref[...], sc_ref[...], sh_ref[...]).astype(BF16)

    return _rows_call(body, name, B, S, [(x, "row"), (sc, "ex"), (sh, "ex")], [(D, BF16, "row")])[0]


def _ln_fwd(alpha, x, y, gate, g, b, name):
    B, S, D = x.shape

    def body(x_ref, y_ref, gate_ref, g_ref, b_ref, o_ref):
        o_ref[...] = _ln_fn(alpha, x_ref[...], y_ref[...], gate_ref[...], g_ref[...], b_ref[...])

    return _rows_call(
        body, name, B, S, [(x, "row"), (y, "row"), (gate, "ex"), (g, "par"), (b, "par")], [(D, F32, "row")]
    )[0]


def _ln_mod_fwd(alpha, x, y, gate, g, b, sc_next, sh_next, name):
    B, S, D = x.shape

    def body(x_ref, y_ref, gate_ref, g_ref, b_ref, sc_ref, sh_ref, o_ref, h_ref):
        out = _ln_fn(alpha, x_ref[...], y_ref[...], gate_ref[...], g_ref[...], b_ref[...])
        o_ref[...] = out
        h_ref[...] = _mod_fn(out, sc_ref[...], sh_ref[...]).astype(BF16)

    return _rows_call(
        body, name, B, S,
        [(x, "row"), (y, "row"), (gate, "ex"), (g, "par"), (b, "par"), (sc_next, "ex"), (sh_next, "ex")],
        [(D, F32, "row"), (D, BF16, "row")],
    )


def _ln_mod_bwd(alpha, dh, dxr_next, sc_next, x, y, gate, g, b, name):
    B, S, D = x.shape

    def body(dh_ref, dxr_ref, sc_ref, x_ref, y_ref, gate_ref, g_ref, b_ref,
             dx_ref, dy_ref, dgate_ref, dg_ref, db_ref, dsc_ref, dsh_ref):
        out, vjp = jax.vjp(
            functools.partial(_ln_fn, alpha), x_ref[...], y_ref[...], gate_ref[...], g_ref[...], b_ref[...]
        )
        dh_v = dh_ref[...]
        dx, dy, dgate, dg, db = vjp(dxr_ref[...] + dh_v * (1.0 + sc_ref[...]))
        dx_ref[...] = dx
        dy_ref[...] = dy.astype(BF16)
        _acc(dgate_ref, dgate)
        _acc(dg_ref, dg)
        _acc(db_ref, db)
        _acc(dsc_ref, jnp.sum(dh_v * out, axis=0, keepdims=True))
        _acc(dsh_ref, jnp.sum(dh_v, axis=0, keepdims=True))

    return _rows_call(
        body, name, B, S,
        [(dh, "row"), (dxr_next, "row"), (sc_next, "ex"), (x, "row"), (y, "row"), (gate, "ex"), (g, "par"), (b, "par")],
        [(D, F32, "row"), (D, BF16, "row")] + [(D, F32, "acc")] * 5,
    )


def _ln_bwd(alpha, dout, x, y, gate, g, b, name):
    B, S, D = x.shape

    def body(do_ref, x_ref, y_ref, gate_ref, g_ref, b_ref, dxr_ref, dy_ref, dgate_ref, dg_ref, db_ref):
        _, vjp = jax.vjp(
            functools.partial(_ln_fn, alpha), x_ref[...], y_ref[...], gate_ref[...], g_ref[...], b_ref[...]
        )
        dx, dy, dgate, dg, db = vjp(do_ref[...])
        dxr_ref[...] = dx
        dy_ref[...] = dy.astype(BF16)
        _acc(dgate_ref, dgate)
        _acc(dg_ref, dg)
        _acc(db_ref, db)

    return _rows_call(
        body,
        name,
        B,
        S,
        [(dout, "row"), (x, "row"), (y, "row"), (gate, "ex"), (g, "par"), (b, "par")],
        [(D, F32, "row"), (D, BF16, "row"), (D, F32, "acc"), (D, F32, "acc"), (D, F32, "acc")],
    )


def _mod_bwd(dh, dxr, x, sc, name):
    B, S, D = x.shape

    def body(dh_ref, dxr_ref, x_ref, sc_ref, dx_ref, dsc_ref, dsh_ref):
        dh_v = dh_ref[...]
        dx_ref[...] = dxr_ref[...] + dh_v * (1.0 + sc_ref[...])
        _acc(dsc_ref, jnp.sum(dh_v * x_ref[...], axis=0, keepdims=True))
        _acc(dsh_ref, jnp.sum(dh_v, axis=0, keepdims=True))

    return _rows_call(
        body,
        name,
        B,
        S,
        [(dh, "row"), (dxr, "row"), (x, "row"), (sc, "ex")],
        [(D, F32, "row"), (D, F32, "acc"), (D, F32, "acc")],
    )


def _loss_head(y, target, name):
    B, S, D = y.shape

    def body(y_ref, t_ref, l_ref, dy_ref):
        e = y_ref[...] - t_ref[...]
        dy_ref[...] = e * (1.0 / D)
        part = 0.5 * jnp.sum(jnp.sum(e * e, axis=1, keepdims=True) * (1.0 / D), axis=0, keepdims=True)
        _acc(l_ref, jnp.broadcast_to(part, (1, LANES)))

    loss, dy = _rows_call(
        body, name, B, S, [(y, "row"), (target, "row")], [(LANES, F32, "acc"), (D, F32, "row")]
    )
    return jnp.sum(loss[:, 0, 0]), dy


def _ffn_in(h, w, name):
    M, K = h.shape
    G, _, n = w.shape
    assert G == 4
    tm = min(512, M)
    tn = _pick_tile(n, 1536)
    nps = n // tn
    half = 2 * nps

    def body(h_ref, wg_ref, wu_ref, a_ref, g_ref, u_ref):
        hv = h_ref[...]
        g = _bdot(hv, wg_ref[...])
        u = _bdot(hv, wu_ref[...])
        a_ref[...] = (jax.nn.silu(g) * u).astype(BF16)
        g_ref[...] = g.astype(BF16)
        u_ref[...] = u.astype(BF16)

    out = pl.BlockSpec((tm, tn), lambda j, i: (i, j))
    return pl.pallas_call(
        body,
        grid=(half, M // tm),
        in_specs=[
            pl.BlockSpec((tm, K), lambda j, i: (i, 0)),
            pl.BlockSpec((None, K, tn), lambda j, i: (j // nps, 0, j % nps)),
            pl.BlockSpec((None, K, tn), lambda j, i: (2 + j // nps, 0, j % nps)),
        ],
        out_specs=[out, out, out],
        out_shape=[SDS((M, 2 * n), BF16)] * 3,
        name=name,
        compiler_params=_cparams("parallel", "parallel"),
    )(h, w, w)


def _swiglu_bwd(g, u, da, name):
    B, S, F = g.shape

    def body(g_ref, u_ref, da_ref, du_ref):
        _, vjp = jax.vjp(lambda gv, uv: jax.nn.silu(gv) * uv, g_ref[...].astype(F32), u_ref[...].astype(F32))
        dg, du = vjp(da_ref[...])
        du_ref[:, :F] = dg.astype(BF16)
        du_ref[:, F:] = du.astype(BF16)

    return _rows_call(body, name, B, S, [(g, "row"), (u, "row"), (da, "row")], [(2 * F, BF16, "row")])[0]


def _rms_fn(x, g):
    return x * lax.rsqrt(jnp.mean(jnp.square(x), -1, keepdims=True) + RMS_EPS) * g


def _mla_mid_fwd(proj, gq, gkv, name):
    B, S, _ = proj.shape

    def body(p_ref, gq_ref, gkv_ref, qn_ref, kvn_ref):
        p = p_ref[...]
        qn_ref[...] = _rms_fn(p[:, :Q_LORA], gq_ref[...]).astype(BF16)
        kvn_ref[...] = _rms_fn(p[:, Q_LORA : Q_LORA + KV_LORA], gkv_ref[...]).astype(BF16)

    return _rows_call(
        body, name, B, S, [(proj, "row"), (gq, "par"), (gkv, "par")], [(Q_LORA, BF16, "row"), (KV_LORA, BF16, "row")]
    )


def _mla_mid_bwd(proj, dqn, dkvn, dkr, gq, gkv, name):
    B, S, W = proj.shape

    def body(p_ref, dqn_ref, dkvn_ref, dkr_ref, gq_ref, gkv_ref, dp_ref, dgq_ref, dgkv_ref):
        p = p_ref[...]
        _, vq = jax.vjp(_rms_fn, p[:, :Q_LORA], gq_ref[...])
        dql, dgq = vq(dqn_ref[...])
        _, vkv = jax.vjp(_rms_fn, p[:, Q_LORA : Q_LORA + KV_LORA], gkv_ref[...])
        dkvl, dgkv = vkv(dkvn_ref[...])
        dp_ref[:, :Q_LORA] = dql.astype(BF16)
        dp_ref[:, Q_LORA : Q_LORA + KV_LORA] = dkvl.astype(BF16)
        dp_ref[:, Q_LORA + KV_LORA :] = dkr_ref[...].astype(BF16)
        _acc(dgq_ref, dgq)
        _acc(dgkv_ref, dgkv)

    return _rows_call(
        body,
        name,
        B,
        S,
        [(proj, "row"), (dqn, "row"), (dkvn, "row"), (dkr, "row"), (gq, "par"), (gkv, "par")],
        [(W, BF16, "row"), (Q_LORA, F32, "acc"), (KV_LORA, F32, "acc")],
    )


def _rope(x, cos, sin):
    h = QK_ROPE // 2
    x1, x2 = x[:, :h], x[:, h:]
    return jnp.concatenate([x1 * cos - x2 * sin, x1 * sin + x2 * cos], axis=1)


def _rope_t(dy, cos, sin):
    h = QK_ROPE // 2
    d1, d2 = dy[:, :h], dy[:, h:]
    return jnp.concatenate([d1 * cos + d2 * sin, d2 * cos - d1 * sin], axis=1)


def _heads_call(body, name, B, S, ins, outs, ts=256):
    ts = min(ts, S)
    in_specs, args = [], []
    for arr, kind in ins:
        if kind == "row":
            in_specs.append(pl.BlockSpec((None, ts, arr.shape[-1]), lambda b, s: (b, s, 0)))
        else:
            in_specs.append(pl.BlockSpec((arr.shape[0], None, ts, arr.shape[-1]), lambda b, s: (0, b, s, 0)))
        args.append(arr)
    out_specs, out_shape = [], []
    for shape, dt, kind in outs:
        if kind == "row":
            out_specs.append(pl.BlockSpec((None, ts, shape[-1]), lambda b, s: (b, s, 0)))
        else:
            out_specs.append(pl.BlockSpec((shape[0], None, ts, shape[-1]), lambda b, s: (0, b, s, 0)))
        out_shape.append(SDS(shape, dt))
    return pl.pallas_call(
        body,
        grid=(B, S // ts),
        in_specs=in_specs,
        out_specs=out_specs,
        out_shape=out_shape,
        name=name,
        compiler_params=_cparams("parallel", "parallel"),
    )(*args)


def _mla_prep_fwd(q, kv, proj, cos, sin, name):
    B, S, _ = q.shape
    H = MLA_HEADS

    def body(q_ref, kv_ref, p_ref, cos_ref, sin_ref, qh_ref, kh_ref, vh_ref):
        cos_v, sin_v = cos_ref[...], sin_ref[...]
        kr = _rope(p_ref[:, Q_LORA + KV_LORA :], cos_v, sin_v).astype(BF16)
        for h in range(H):
            qn = q_ref[:, h * QK_DIM : h * QK_DIM + QK_NOPE]
            qr = _rope(q_ref[:, h * QK_DIM + QK_NOPE : (h + 1) * QK_DIM], cos_v, sin_v)
            qh_ref[h] = jnp.concatenate([qn, qr], axis=1).astype(BF16)
            kn = kv_ref[:, h * 128 : h * 128 + QK_NOPE].astype(BF16)
            kh_ref[h] = jnp.concatenate([kn, kr], axis=1)
            vh_ref[h] = kv_ref[:, h * 128 + QK_NOPE : (h + 1) * 128].astype(BF16)

    return _heads_call(
        body,
        name,
        B,
        S,
        [(q, "row"), (kv, "row"), (proj, "row"), (cos, "row"), (sin, "row")],
        [((H, B, S, QK_DIM), BF16, "heads"), ((H, B, S, QK_DIM), BF16, "heads"), ((H, B, S, V_HEAD), BF16, "heads")],
    )


def _mla_prep_bwd(dqh, dkh, dvh, cos, sin, name):
    H, B, S, _ = dqh.shape

    def body(dqh_ref, dkh_ref, dvh_ref, cos_ref, sin_ref, dq_ref, dkv_ref, dkr_ref):
        cos_v, sin_v = cos_ref[...], sin_ref[...]
        dkr = jnp.zeros((cos_v.shape[0], QK_ROPE), F32)
        for h in range(H):
            dqv = dqh_ref[h].astype(F32)
            dq_ref[:, h * QK_DIM : h * QK_DIM + QK_NOPE] = dqv[:, :QK_NOPE].astype(BF16)
            dq_ref[:, h * QK_DIM + QK_NOPE : (h + 1) * QK_DIM] = _rope_t(dqv[:, QK_NOPE:], cos_v, sin_v).astype(BF16)
            dkv = dkh_ref[h].astype(F32)
            dkv_ref[:, h * 128 : h * 128 + QK_NOPE] = dkv[:, :QK_NOPE].astype(BF16)
            dkv_ref[:, h * 128 + QK_NOPE : (h + 1) * 128] = dvh_ref[h]
            dkr = dkr + dkv[:, QK_NOPE:]
        dkr_ref[...] = _rope_t(dkr, cos_v, sin_v)

    return _heads_call(
        body,
        name,
        B,
        S,
        [(dqh, "heads"), (dkh, "heads"), (dvh, "heads"), (cos, "row"), (sin, "row")],
        [((B, S, H * QK_DIM), BF16, "row"), ((B, S, H * 128), BF16, "row"), ((B, S, QK_ROPE), F32, "row")],
    )


LOG2E = 1.4426950408889634
ATTN_TILE = 1024
ATTN_DIAG_SUB = 512


def _attn_fwd(qh, kh, vh, name):
    H, B, S, _ = qh.shape
    t = min(ATTN_TILE, S)
    scale = QK_DIM**-0.5
    c2 = scale * LOG2E

    def body(q_ref, k_ref, v_ref, o_ref, lse_ref):
        i = pl.program_id(2)
        qs = [q_ref[0], q_ref[1]]

        def update(state, q, k, v, mask):
            m, l, acc = state
            s = _bdot_nt(q, k)
            if mask is not None:
                s = jnp.where(mask, s, NEG)
            m_new = jnp.maximum(m, jnp.max(s, axis=1, keepdims=True))
            p = jnp.exp2((s - m_new) * c2)
            a = jnp.exp2((m - m_new) * c2)
            return m_new, a * l + jnp.sum(p, axis=1, keepdims=True), a * acc + _bdot(p, v)

        def step(j, carry):
            rows = pl.ds(pl.multiple_of(j * t, t), t)
            return tuple(update(carry[hh], qs[hh], k_ref[hh, rows, :], v_ref[hh, rows, :], None) for hh in range(2))

        one = (jnp.full((t, 1), NEG, F32), jnp.zeros((t, 1), F32), jnp.zeros((t, V_HEAD), F32))
        carry = lax.fori_loop(0, i, step, (one, one))
        rows = pl.ds(pl.multiple_of(i * t, t), t)
        causal = lax.broadcasted_iota(jnp.int32, (t, t), 0) >= lax.broadcasted_iota(jnp.int32, (t, t), 1)
        carry = tuple(update(carry[hh], qs[hh], k_ref[hh, rows, :], v_ref[hh, rows, :], causal) for hh in range(2))
        outs = []
        for hh in range(2):
            m, l, acc = carry[hh]
            outs.append(acc / l)
            lse_ref[hh] = m * scale + jnp.log(l)
        o_ref[...] = jnp.concatenate(outs, axis=1).astype(BF16)

    return pl.pallas_call(
        body,
        grid=(B, H // 2, S // t),
        in_specs=[
            pl.BlockSpec((2, None, t, QK_DIM), lambda b, p, i: (p, b, i, 0)),
            pl.BlockSpec((2, None, S, QK_DIM), lambda b, p, i: (p, b, 0, 0)),
            pl.BlockSpec((2, None, S, V_HEAD), lambda b, p, i: (p, b, 0, 0)),
        ],
        out_specs=[
            pl.BlockSpec((None, t, 2 * V_HEAD), lambda b, p, i: (b, i, p)),
            pl.BlockSpec((2, None, t, 1), lambda b, p, i: (p, b, i, 0)),
        ],
        out_shape=[SDS((B, S, H * V_HEAD), BF16), SDS((H, B, S, 1), F32)],
        name=name,
        compiler_params=_cparams("parallel", "parallel", "arbitrary"),
    )(qh, kh, vh)


def _attn_bwd(qh, kh, vh, o, do, lse, name):
    H, B, S, _ = qh.shape
    t = min(ATTN_TILE, S)
    sub = min(ATTN_DIAG_SUB, t)
    nq = S // t
    scale = QK_DIM**-0.5
    c2 = scale * LOG2E

    def body(q_ref, k_ref, v_ref, o_ref, do_ref, lse_ref, dq_ref, dk_ref, dv_ref, dq_acc, delta_ref, lse2_ref):
        prod = o_ref[...].astype(F32) * do_ref[...].astype(F32)
        for hh in range(2):
            delta_ref[hh] = jnp.sum(prod[:, hh * V_HEAD : (hh + 1) * V_HEAD], axis=1, keepdims=True)
            lse2_ref[hh] = lse_ref[hh] * LOG2E
        dq_acc[...] = jnp.zeros_like(dq_acc)

        def kloop(j, _):
            krows = pl.ds(pl.multiple_of(j * t, t), t)
            ks = [k_ref[0, krows, :], k_ref[1, krows, :]]
            vs = [v_ref[0, krows, :], v_ref[1, krows, :]]

            def pair(hh, qrows, k, v, mask):
                q = q_ref[hh, qrows, :]
                do_h = do_ref[qrows, :][:, hh * V_HEAD : (hh + 1) * V_HEAD]
                p = jnp.exp2(_bdot_nt(q, k) * c2 - lse2_ref[hh, qrows, :])
                if mask is not None:
                    p = jnp.where(mask, p, 0.0)
                dv = _bdot_tn(p, do_h)
                ds = (p * (_bdot_nt(do_h, v) - delta_ref[hh, qrows, :])).astype(BF16)
                dq_acc[hh, qrows, :] += _bdot(ds, k)
                return _bdot_tn(ds, q), dv

            def qstep(i, carry):
                qrows = pl.ds(pl.multiple_of(i * t, t), t)
                out = []
                for hh in range(2):
                    dk, dv = pair(hh, qrows, ks[hh], vs[hh], None)
                    out.append((carry[hh][0] + dk, carry[hh][1] + dv))
                return tuple(out)

            def diagonal_step():
                out = []
                for hh in range(2):
                    dks, dvs = [], []
                    for c in range(t // sub):
                        r0 = c * sub
                        qrows = pl.ds(pl.multiple_of(j * t + r0, sub), t - r0)
                        mask = (lax.broadcasted_iota(jnp.int32, (t - r0, sub), 0)
                                >= lax.broadcasted_iota(jnp.int32, (t - r0, sub), 1))
                        dk, dv = pair(hh, qrows, ks[hh][r0 : r0 + sub], vs[hh][r0 : r0 + sub], mask)
                        dks.append(dk)
                        dvs.append(dv)
                    out.append((jnp.concatenate(dks, axis=0), jnp.concatenate(dvs, axis=0)))
                return tuple(out)

            carry = lax.fori_loop(j + 1, nq, qstep, diagonal_step())
            for hh in range(2):
                dk_ref[hh, krows, :] = (carry[hh][0] * scale).astype(BF16)
                dv_ref[hh, krows, :] = carry[hh][1].astype(BF16)
            return 0

        lax.fori_loop(0, nq, kloop, 0)
        dq_ref[...] = (dq_acc[...] * scale).astype(BF16)

    hspec = lambda w: pl.BlockSpec((2, None, S, w), lambda b, p: (p, b, 0, 0))
    ospec = pl.BlockSpec((None, S, 2 * V_HEAD), lambda b, p: (b, 0, p))
    return pl.pallas_call(
        body,
        grid=(B, H // 2),
        in_specs=[hspec(QK_DIM), hspec(QK_DIM), hspec(V_HEAD), ospec, ospec, hspec(1)],
        out_specs=[hspec(QK_DIM), hspec(QK_DIM), hspec(V_HEAD)],
        out_shape=[SDS((H, B, S, QK_DIM), BF16), SDS((H, B, S, QK_DIM), BF16), SDS((H, B, S, V_HEAD), BF16)],
        scratch_shapes=[pltpu.VMEM((2, S, QK_DIM), F32), pltpu.VMEM((2, S, 1), F32), pltpu.VMEM((2, S, 1), F32)],
        name=name,
        compiler_params=_cparams("parallel", "parallel"),
    )(qh, kh, vh, o, do, lse)


def _hgrn_pre(q, fx, lb):
    sig = jax.nn.sigmoid(fx)
    f = lb + (1.0 - lb) * sig
    return jax.nn.silu(q), 1.0 - f, jnp.log(f)


def _hgrn_gate(o, gg, gn):
    return _rms_fn(o, gn) * jax.nn.silu(gg)


def _tri(n, lower):
    r = lax.broadcasted_iota(jnp.int32, (n, n), 0)
    c = lax.broadcasted_iota(jnp.int32, (n, n), 1)
    return ((r >= c) if lower else (r <= c)).astype(F32)


def _hgrn_intra_fwd(qs, k, v, b):
    C, SB = qs.shape[0], min(HGRN_SUB, qs.shape[0])
    ridx = lax.broadcasted_iota(jnp.int32, (SUBLANES, 1), 0)
    outs = []
    for i in range(C // SB):
        r0 = i * SB
        qi, ki, vi, bi = qs[r0 : r0 + SB], k[r0 : r0 + SB], v[r0 : r0 + SB], b[r0 : r0 + SB]
        ng = SB // SUBLANES
        qg = [qi[g * SUBLANES : (g + 1) * SUBLANES] for g in range(ng)]
        bg = [bi[g * SUBLANES : (g + 1) * SUBLANES] for g in range(ng)]
        accg = [jnp.zeros((SUBLANES, v.shape[1]), F32) for _ in range(ng)]
        for s in range(SB):
            gs, so = divmod(s, SUBLANES)
            k_s, v_s, b_s = ki[s : s + 1], vi[s : s + 1], bi[s : s + 1]
            for tg in range(gs, ng):
                if tg == gs:
                    mask = ridx >= so
                    w = jnp.where(mask, qg[tg] * k_s * jnp.exp(jnp.where(mask, bg[tg] - b_s, 0.0)), 0.0)
                else:
                    w = qg[tg] * k_s * jnp.exp(bg[tg] - b_s)
                accg[tg] = accg[tg] + jnp.sum(w, axis=1, keepdims=True) * v_s
        acc = jnp.concatenate(accg, axis=0)
        if i > 0:
            ref = bi[0:1]
            qt = qi * jnp.exp(bi - ref)
            kt = k[:r0] * jnp.exp(ref - b[:r0])
            acc = acc + _bdot(_mdot_nt(qt, kt), v[:r0])
        outs.append(acc)
    return jnp.concatenate(outs, axis=0)


def _hgrn_intra_bwd(qs, k, v, b, do):
    C, SB = qs.shape[0], min(HGRN_SUB, qs.shape[0])
    nb = C // SB
    ridx = lax.broadcasted_iota(jnp.int32, (SUBLANES, 1), 0)
    dq_p = [None] * nb
    dk_p = [jnp.zeros((SB, k.shape[1]), F32) for _ in range(nb)]
    dv_p = [jnp.zeros((SB, v.shape[1]), F32) for _ in range(nb)]
    for i in range(nb):
        r0 = i * SB
        qi, ki, vi, bi, doi = qs[r0 : r0 + SB], k[r0 : r0 + SB], v[r0 : r0 + SB], b[r0 : r0 + SB], do[r0 : r0 + SB]
        ng = SB // SUBLANES
        qg = [qi[g * SUBLANES : (g + 1) * SUBLANES] for g in range(ng)]
        bg = [bi[g * SUBLANES : (g + 1) * SUBLANES] for g in range(ng)]
        dog = [doi[g * SUBLANES : (g + 1) * SUBLANES] for g in range(ng)]
        dqg =[jnp.zeros((SUBLANES, k.shape[1]), F32) for _ in range(ng)]
        dkg = [jnp.zeros((SUBLANES, k.shape[1]), F32) for _ in range(ng)]
        dvg = [jnp.zeros((SUBLANES, v.shape[1]), F32) for _ in range(ng)]
        for s in range(SB):
            gs, so = divmod(s, SUBLANES)
            k_s, v_s, b_s = ki[s : s + 1], vi[s : s + 1], bi[s : s + 1]
            dk_s = jnp.zeros((SUBLANES, k.shape[1]), F32)
            dv_s = jnp.zeros((SUBLANES, v.shape[1]), F32)
            for tg in range(gs, ng):
                if tg == gs:
                    mask = ridx >= so
                    e = jnp.where(mask, jnp.exp(jnp.where(mask, bg[tg] - b_s, 0.0)), 0.0)
                else:
                    e = jnp.exp(bg[tg] - b_s)
                da = jnp.sum(dog[tg] * v_s, axis=1, keepdims=True)
                qe = qg[tg] * e
                a = jnp.sum(qe * k_s, axis=1, keepdims=True)
                dqg[tg] = dqg[tg] + da * (k_s * e)
                dk_s = dk_s + da * qe
                dv_s = dv_s + a * dog[tg]
            dkg[gs] = jnp.where(ridx == so, dkg[gs] + jnp.sum(dk_s, axis=0, keepdims=True), dkg[gs])
            dvg[gs] = jnp.where(ridx == so, dvg[gs] + jnp.sum(dv_s, axis=0, keepdims=True), dvg[gs])
        dqi = jnp.concatenate(dqg, axis=0)
        dki = jnp.concatenate(dkg, axis=0)
        dvi = jnp.concatenate(dvg, axis=0)
        if i > 0:
            ref = bi[0:1]
            eq = jnp.exp(bi - ref)
            ek = jnp.exp(ref - b[:r0])
            qt = qi * eq
            kt = k[:r0] * ek
            A = _mdot_nt(qt, kt)
            dA = _bdot_nt(doi, v[:r0])
            dvl = _bdot_tn(A, doi)
            dqi = dqi + _mdot(dA, kt) * eq
            dkl = _mdot_tn(dA, qt) * ek
            for j in range(i):
                dk_p[j] = dk_p[j] + dkl[j * SB : (j + 1) * SB]
                dv_p[j] = dv_p[j] + dvl[j * SB : (j + 1) * SB]
        dq_p[i] = dqi
        dk_p[i] = dk_p[i] + dki
        dv_p[i] = dv_p[i] + dvi
    return jnp.concatenate(dq_p, axis=0), jnp.concatenate(dk_p, axis=0), jnp.concatenate(dv_p, axis=0)


def _hgrn_fwd(proj, lb, gn, name):
    B, S, W = proj.shape
    HK = W // 4
    H = HK // HGRN_K
    C = min(HGRN_CHUNK, S)
    N = S // C

    HP = HGRN_PAR if H % HGRN_PAR == 0 else 1
    WP = HP * HGRN_K

    def body(q_ref, f_ref, i_ref, g_ref, lb_ref, gn_ref, og_ref, o_ref, st_ref):
        gn_v = gn_ref[...]
        tril = _tri(C, True)

        def chunk(n, sts):
            rows = pl.ds(pl.multiple_of(n * C, C), C)
            out = []
            for hh in range(HP):
                ln = slice(hh * HGRN_K, (hh + 1) * HGRN_K)
                st = sts[hh]
                qs, k, g = _hgrn_pre(q_ref[rows, ln], f_ref[rows, ln], lb_ref[:, ln])
                v = i_ref[rows, ln]
                b = _hdot(tril, g)
                st_ref[hh, n] = st
                o = _hgrn_intra_fwd(qs, k, v, b) + _bdot_nt(qs * jnp.exp(b), st)
                bl = b[C - 1 : C]
                out.append(st * jnp.exp(bl) + _bdot_tn(v, k * jnp.exp(bl - b)))
                o_ref[rows, ln] = o
                og_ref[rows, ln] = _hgrn_gate(o, g_ref[rows, ln], gn_v).astype(BF16)
            return tuple(out)

        lax.fori_loop(0, N, chunk, tuple(jnp.zeros((HGRN_K, HGRN_K), F32) for _ in range(HP)))

    col = lambda part: pl.BlockSpec((None, S, WP), lambda b, h: (b, 0, part * (H // HP) + h))
    return pl.pallas_call(
        body,
        grid=(B, H // HP),
        in_specs=[col(0), col(1), col(2), col(3), pl.BlockSpec((1, WP), lambda b, h: (0, h)), pl.BlockSpec((1, HGRN_K), lambda b, h: (0, 0))],
        out_specs=[col(0), col(0), pl.BlockSpec((None, HP, N, HGRN_K, HGRN_K), lambda b, h: (b, h, 0, 0, 0))],
        out_shape=[SDS((B, S, HK), BF16), SDS((B, S, HK), F32), SDS((B, H, N, HGRN_K, HGRN_K), F32)],
        name=name,
        compiler_params=_cparams("parallel", "parallel"),
    )(proj, proj, proj, proj, lb, gn)


def _hgrn_bwd(proj, lb, gn, o_pre, states, dog, name):
    B, S, W = proj.shape
    HK = W // 4
    H = HK // HGRN_K
    C = min(HGRN_CHUNK, S)
    N = S // C

    HP = HGRN_PAR if H % HGRN_PAR == 0 else 1
    WP = HP * HGRN_K

    def body(q_ref, f_ref, i_ref, g_ref, lb_ref, gn_ref, o_ref, st_ref, dog_ref, dq_ref, df_ref, di_ref, dg_ref, dlb_ref, dgn_ref):
        gn_v = gn_ref[...]
        tril = _tri(C, True)
        triu = _tri(C, False)

        def chunk(idx, carry):
            n = N - 1 - idx
            rows = pl.ds(pl.multiple_of(n * C, C), C)
            out = []
            for hh in range(HP):
                ln = slice(hh * HGRN_K, (hh + 1) * HGRN_K)
                dst, dlb, dgn = carry[hh]
                (qs, k, g), pre_vjp = jax.vjp(_hgrn_pre, q_ref[rows, ln], f_ref[rows, ln], lb_ref[:, ln])
                v = i_ref[rows, ln]
                _, gate_vjp = jax.vjp(_hgrn_gate, o_ref[rows, ln], g_ref[rows, ln], gn_v)
                do, dgg, dgn_c = gate_vjp(dog_ref[rows, ln])
                b = _hdot(tril, g)
                st0 = st_ref[hh, n]
                eb = jnp.exp(b)
                bl = b[C - 1 : C]
                ebl = jnp.exp(bl)
                ekb = jnp.exp(bl - b)
                qe = qs * eb
                kt = k * ekb
                dqs, dk, dv = _hgrn_intra_bwd(qs, k, v, b, do)
                dqs = dqs + _bdot(do, st0) * eb
                dk = dk + _bdot(v, dst) * ekb
                dv = dv + _bdot_nt(kt, dst)
                st1 = st0 * ebl + _bdot_tn(v, kt)
                dbl = jnp.sum(st1 * dst, axis=0, keepdims=True)
                dst = dst * ebl + _bdot_tn(do, qe)
                dgl = _hdot(triu, qs * dqs - k * dk) + dbl
                dq_pre, dfx, dlb_c = pre_vjp((dqs, dk, dgl))
                dq_ref[rows, ln] = dq_pre.astype(BF16)
                df_ref[rows, ln] = dfx.astype(BF16)
                di_ref[rows, ln] = dv.astype(BF16)
                dg_ref[rows, ln] = dgg.astype(BF16)
                out.append((dst, dlb + dlb_c, dgn + dgn_c))
            return tuple(out)

        zero = jnp.zeros((1, HGRN_K), F32)
        one = (jnp.zeros((HGRN_K, HGRN_K), F32), zero, zero)
        res = lax.fori_loop(0, N, chunk, tuple(one for _ in range(HP)))
        for hh in range(HP):
            dlb_ref[hh] = res[hh][1]
            dgn_ref[hh] = res[hh][2]

    col = lambda part: pl.BlockSpec((None, S, WP), lambda b, h: (b, 0, part * (H // HP) + h))
    vec = pl.BlockSpec((None, HP, 1, HGRN_K), lambda b, h: (b, h, 0, 0))
    return pl.pallas_call(
        body,
        grid=(B, H // HP),
        in_specs=[
            col(0), col(1), col(2), col(3),
            pl.BlockSpec((1, WP), lambda b, h: (0, h)),
            pl.BlockSpec((1, HGRN_K), lambda b, h: (0, 0)),
            col(0),
            pl.BlockSpec((None, HP, N, HGRN_K, HGRN_K), lambda b, h: (b, h, 0, 0, 0)),
            col(0),
        ],
        out_specs=[col(0), col(0), col(0), col(0), vec, vec],
        out_shape=[SDS((B, S, HK), BF16)] * 4 + [SDS((B, H, 1, HGRN_K), F32)] * 2,
        name=name,
        compiler_params=_cparams("parallel", "parallel"),
    )(proj, proj, proj, proj, lb, gn, o_pre, states, dog)


def _ada_fwd(c_all, w, b, name):
    Bg, D = c_all.shape
    L, _, n = w.shape

    def body(c_ref, w_ref, b_ref, o_ref):
        o_ref[...] = _bdot(jax.nn.silu(c_ref[...]), w_ref[...]) + b_ref[...]

    return pl.pallas_call(
        body,
        grid=(L,),
        in_specs=[
            pl.BlockSpec((Bg, D), lambda l: (0, 0)),
            pl.BlockSpec((None, D, n), lambda l: (l, 0, 0)),
            pl.BlockSpec((None, 1, n), lambda l: (l, 0, 0)),
        ],
        out_specs=pl.BlockSpec((None, Bg, n), lambda l: (l, 0, 0)),
        out_shape=SDS((L, Bg, n), F32),
        name=name,
        compiler_params=_cparams("parallel"),
    )(c_all, w, b)


def _ada_bwd(c_all, dmod, name):
    Bg, D = c_all.shape
    L, _, n = dmod.shape

    def body(c_ref, d_ref, dw_ref, db_ref):
        d = d_ref[...]
        dw_ref[...] = _bdot_tn(jax.nn.silu(c_ref[...]), d)
        db_ref[...] = jnp.sum(d, axis=0, keepdims=True)

    return pl.pallas_call(
        body,
        grid=(L,),
        in_specs=[pl.BlockSpec((Bg, D), lambda l: (0, 0)), pl.BlockSpec((None, Bg, n), lambda l: (l, 0, 0))],
        out_specs=[pl.BlockSpec((None, D, n), lambda l: (l, 0, 0)), pl.BlockSpec((None, 1, n), lambda l: (l, 0, 0))],
        out_shape=[SDS((L, D, n), F32), SDS((L, 1, n), F32)],
        name=name,
        compiler_params=_cparams("parallel"),
    )(c_all, dmod)


def _adamw(w, gs, m, v, name):
    shape = w.shape
    cols = shape[-1]
    rows = w.size // cols
    tr = rows
    for cand in (512, 256, 128, 64, 32, 16, 8):
        if rows % cand == 0 and cand * cols * 4 <= 2 * 1024 * 1024:
            tr = cand
            break
    as2d = lambda a: a.reshape(rows, cols)
    ng = len(gs)
    c1 = 1.0 / (1.0 - ADAM_B1**ADAM_STEP)
    c2 = 1.0 / (1.0 - ADAM_B2**ADAM_STEP)

    def body(*refs):
        w_ref, m_ref, v_ref = refs[0], refs[1], refs[2]
        g_refs = refs[3 : 3 + ng]
        g_out, d_out, m_out, v_out = refs[3 + ng :]
        g = g_refs[0][...].astype(F32)
        for r in g_refs[1:]:
            g = g + r[...].astype(F32)
        m_new = ADAM_B1 * m_ref[...] + (1.0 - ADAM_B1) * g
        v_new = ADAM_B2 * v_ref[...] + (1.0 - ADAM_B2) * jnp.square(g)
        g_out[...] = g
        m_out[...] = m_new
        v_out[...] = v_new
        d_out[...] = -ADAM_LR * ((m_new * c1) / (jnp.sqrt(v_new * c2) + ADAM_EPS) + ADAM_WD * w_ref[...])

    spec = pl.BlockSpec((tr, cols), lambda i: (i, 0))
    outs = pl.pallas_call(
        body,
        grid=(rows // tr,),
        in_specs=[spec] * (3 + ng),
        out_specs=[spec] * 4,
        out_shape=[SDS((rows, cols), F32)] * 4,
        name=name,
        compiler_params=_cparams("parallel"),
    )(as2d(w), as2d(m), as2d(v), *[as2d(g) for g in gs])
    return tuple(o.reshape(shape) for o in outs)


def _sum4(own, recv, name):
    shape = own.shape
    cols = shape[-1]
    rows = own.size // cols
    tr = rows
    for cand in (512, 256, 128, 64, 32, 16):
        if rows % cand == 0 and cand * cols * 4 <= 2 * 1024 * 1024:
            tr = cand
            break

    def body(own_ref, recv_ref, o_ref):
        acc = own_ref[...].astype(F32)
        for r in range(3):
            acc = acc + recv_ref[r].astype(F32)
        o_ref[...] = acc

    out = pl.pallas_call(
        body,
        grid=(rows // tr,),
        in_specs=[pl.BlockSpec((tr, cols), lambda i: (i, 0)), pl.BlockSpec((3, tr, cols), lambda i: (0, i, 0))],
        out_specs=pl.BlockSpec((tr, cols), lambda i: (i, 0)),
        out_shape=SDS((rows, cols), F32),
        name=name,
        compiler_params=_cparams("parallel"),
    )(own.reshape(rows, cols), recv.reshape(3, rows, cols))
    return out.reshape(shape)


def _my_place():
    return lax.axis_index("x"), lax.axis_index("y"), lax.axis_index("c")


def _flip(v, bit):
    return 1 - v if bit else v


def _allgather8(x, name):
    r, n = x.shape

    def body(x_ref, o_ref, send_sems, recv_sems, local_sem):
        mx, my, mc = _my_place()
        me = 4 * mx + 2 * my + mc
        mine = pltpu.make_async_copy(x_ref, o_ref.at[me], local_sem)
        mine.start()
        sends = []
        for rel in range(1, 8):
            peer = (_flip(mx, rel & 4), _flip(my, rel & 2), _flip(mc, rel & 1))
            cp = pltpu.make_async_remote_copy(
                src_ref=x_ref, dst_ref=o_ref.at[me], send_sem=send_sems.at[rel - 1], recv_sem=recv_sems.at[rel - 1],
                device_id=peer, device_id_type=MESH,
            )
            cp.start()
            sends.append(cp)
        for rel in range(1, 8):
            px, py, pc = _flip(mx, rel & 4), _flip(my, rel & 2), _flip(mc, rel & 1)
            pltpu.make_async_remote_copy(
                src_ref=x_ref, dst_ref=o_ref.at[4 * px + 2 * py + pc], send_sem=send_sems.at[rel - 1],
                recv_sem=recv_sems.at[rel - 1], device_id=(px, py, pc), device_id_type=MESH,
            ).wait_recv()
        for cp in sends:
            cp.wait_send()
        mine.wait()

    return pl.pallas_call(
        body,
        out_shape=SDS((8, r, n), x.dtype),
        in_specs=[pl.BlockSpec(memory_space=pl.ANY)],
        out_specs=pl.BlockSpec(memory_space=pl.ANY),
        scratch_shapes=[pltpu.SemaphoreType.DMA((7,)), pltpu.SemaphoreType.DMA((7,)), pltpu.SemaphoreType.DMA],
        name=name,
    )(x)


_HBM = pl.BlockSpec(memory_space=pl.ANY)


_SEM = pl.BlockSpec(memory_space=pltpu.SEMAPHORE)
_HBM_ONLY = pl.BlockSpec(memory_space=pltpu.HBM)
_EFFECT = pltpu.SideEffectType.DATAFLOW_SIDE_EFFECTING


def _in_hbm(a):
    return pltpu.with_memory_space_constraint(a, pltpu.HBM)


def _gather_start(lands, after, name):
    n = len(lands)

    def body(*refs):
        land = refs[:n]
        send_sems, recv_sems = refs[n + 1], refs[n + 2]
        token = refs[-1]
        mx, my, mc = _my_place()
        for i in range(n):
            for rel in range(1, 4):
                pltpu.make_async_remote_copy(
                    src_ref=land[i].at[2 * mx + my], dst_ref=land[i].at[2 * mx + my],
                    send_sem=send_sems.at[3 * i + rel - 1], recv_sem=recv_sems.at[3 * i + rel - 1],
                    device_id=(_flip(mx, rel & 2), _flip(my, rel & 1), mc), device_id_type=MESH,
                ).start()
        token[...] = jnp.zeros_like(token)

    outs = pl.pallas_call(
        body,
        name=name,
        out_shape=(
            pltpu.SemaphoreType.DMA((3 * n,)), pltpu.SemaphoreType.DMA((3 * n,)),
            *[pltpu.HBM(a.shape, a.dtype) for a in lands], SDS((8, LANES), F32),
        ),
        in_specs=[_HBM_ONLY] * n + [_HBM],
        out_specs=(_SEM, _SEM, *[_HBM_ONLY] * n, pl.BlockSpec(memory_space=pltpu.VMEM)),
        input_output_aliases={i: 2 + i for i in range(n)},
        compiler_params=pltpu.CompilerParams(has_side_effects=_EFFECT),
    )(*[_in_hbm(a) for a in lands], after)
    return outs[0], outs[1], list(outs[2 : 2 + n]), outs[-1]


def _gather_wait(send_sems, recv_sems, lands, after, name):
    n = len(lands)

    def body(*refs):
        land = refs[:n]
        s_sems, r_sems = refs[n], refs[n + 1]
        mx, my, mc = _my_place()
        for i in range(n):
            for rel in range(1, 4):
                px, py = _flip(mx, rel & 2), _flip(my, rel & 1)
                cp = pltpu.make_async_remote_copy(
                    src_ref=land[i].at[2 * mx + my], dst_ref=land[i].at[2 * px + py],
                    send_sem=s_sems.at[3 * i + rel - 1], recv_sem=r_sems.at[3 * i + rel - 1],
                    device_id=(px, py, mc), device_id_type=MESH,
                )
                cp.wait_send()
                cp.wait_recv()

    outs = pl.pallas_call(
        body,
        name=name,
        out_shape=tuple(pltpu.HBM(a.shape, a.dtype) for a in lands),
        in_specs=[_HBM_ONLY] * n + [_SEM, _SEM, _HBM],
        out_specs=[_HBM_ONLY] * n,
        input_output_aliases={i: i for i in range(n)},
        compiler_params=pltpu.CompilerParams(has_side_effects=_EFFECT),
    )(*lands, send_sems, recv_sems, after)
    return list(outs)


def _scatter_start(slabs, lands, places, name):
    n = len(slabs)

    def body(*refs):
        ins, land = refs[:n], refs[n : 2 * n]
        send_sems, recv_sems = refs[2 * n], refs[2 * n + 1]
        token = refs[-1]
        mx, my, mc = _my_place()
        for i in range(n):
            for rel in range(1, 4):
                px, py = _flip(mx, rel & 2), _flip(my, rel & 1)
                pltpu.make_async_remote_copy(
                    src_ref=ins[i].at[2 * px + py], dst_ref=land[i].at[rel - 1, places[i]],
                    send_sem=send_sems.at[3 * i + rel - 1], recv_sem=recv_sems.at[3 * i + rel - 1],
                    device_id=(px, py, mc), device_id_type=MESH,
                ).start()
        token[...] = jnp.zeros_like(token)

    outs = pl.pallas_call(
        body,
        name=name,
        out_shape=(
            pltpu.SemaphoreType.DMA((3 * n,)), pltpu.SemaphoreType.DMA((3 * n,)),
            *[pltpu.HBM(a.shape, a.dtype) for a in slabs], *[pltpu.HBM(a.shape, a.dtype) for a in lands],
            SDS((8, LANES), F32),
        ),
        in_specs=[_HBM_ONLY] * (2 * n),
        out_specs=(_SEM, _SEM, *[_HBM_ONLY] * (2 * n), pl.BlockSpec(memory_space=pltpu.VMEM)),
        input_output_aliases={i: 2 + i for i in range(2 * n)},
        compiler_params=pltpu.CompilerParams(has_side_effects=_EFFECT),
    )(*[_in_hbm(a) for a in slabs], *[_in_hbm(a) for a in lands])
    return outs[0], outs[1], list(outs[2 : 2 + n]), list(outs[2 + n : 2 + 2 * n]), outs[-1]


def _scatter_wait(send_sems, recv_sems, slabs, lands, places, after, name):
    n = len(slabs)

    def body(*refs):
        ins, land = refs[:n], refs[n : 2 * n]
        s_sems, r_sems = refs[2 * n], refs[2 * n + 1]
        mx, my, mc = _my_place()
        for i in range(n):
            for rel in range(1, 4):
                px, py = _flip(mx, rel & 2), _flip(my, rel & 1)
                cp = pltpu.make_async_remote_copy(
                    src_ref=ins[i].at[2 * px + py], dst_ref=land[i].at[rel - 1, places[i]],
                    send_sem=s_sems.at[3 * i + rel - 1], recv_sem=r_sems.at[3 * i + rel - 1],
                    device_id=(px, py, mc), device_id_type=MESH,
                )
                cp.wait_send()
                cp.wait_recv()

    outs = pl.pallas_call(
        body,
        name=name,
        out_shape=(*[pltpu.HBM(a.shape, a.dtype) for a in slabs], *[pltpu.HBM(a.shape, a.dtype) for a in lands]),
        in_specs=[_HBM_ONLY] * (2 * n) + [_SEM, _SEM, _HBM],
        out_specs=[_HBM_ONLY] * (2 * n),
        input_output_aliases={i: i for i in range(2 * n)},
        compiler_params=pltpu.CompilerParams(has_side_effects=_EFFECT),
    )(*slabs, *lands, send_sems, recv_sems, after)
    return list(outs[:n]), list(outs[n:])


def _swap_start(parts, after, name):
    n = len(parts)
    lands = [lax.empty(a.shape, a.dtype) for a in parts]

    def body(*refs):
        ins, land = refs[:n], refs[n : 2 * n]
        send_sems, recv_sems = refs[2 * n + 1], refs[2 * n + 2]
        mx, my, mc = _my_place()
        for i in range(n):
            pltpu.make_async_remote_copy(
                src_ref=ins[i], dst_ref=land[i], send_sem=send_sems.at[i], recv_sem=recv_sems.at[i],
                device_id=(mx, my, 1 - mc), device_id_type=MESH,
            ).start()

    outs = pl.pallas_call(
        body,
        name=name,
        out_shape=(
            pltpu.SemaphoreType.DMA((n,)), pltpu.SemaphoreType.DMA((n,)),
            *[pltpu.HBM(a.shape, a.dtype) for a in parts], *[pltpu.HBM(a.shape, a.dtype) for a in lands],
        ),
        in_specs=[_HBM_ONLY] * (2 * n) + [_HBM],
        out_specs=(_SEM, _SEM, *[_HBM_ONLY] * (2 * n)),
        input_output_aliases={i: 2 + i for i in range(2 * n)},
        compiler_params=pltpu.CompilerParams(has_side_effects=_EFFECT),
    )(*[_in_hbm(a) for a in parts], *[_in_hbm(a) for a in lands], after)
    return outs[0], outs[1], list(outs[2 : 2 + n]), list(outs[2 + n :])


def _swap_wait(send_sems, recv_sems, parts, lands, after, name):
    n = len(parts)

    def body(*refs):
        ins, land = refs[:n], refs[n : 2 * n]
        s_sems, r_sems = refs[2 * n], refs[2 * n + 1]
        mx, my, mc = _my_place()
        for i in range(n):
            cp = pltpu.make_async_remote_copy(
                src_ref=ins[i], dst_ref=land[i], send_sem=s_sems.at[i], recv_sem=r_sems.at[i],
                device_id=(mx, my, 1 - mc), device_id_type=MESH,
            )
            cp.wait_send()
            cp.wait_recv()

    outs = pl.pallas_call(
        body,
        name=name,
        out_shape=(*[pltpu.HBM(a.shape, a.dtype) for a in parts], *[pltpu.HBM(a.shape, a.dtype) for a in lands]),
        in_specs=[_HBM_ONLY] * (2 * n) + [_SEM, _SEM, _HBM],
        out_specs=[_HBM_ONLY] * (2 * n),
        input_output_aliases={i: i for i in range(2 * n)},
        compiler_params=pltpu.CompilerParams(has_side_effects=_EFFECT),
    )(*parts, *lands, send_sems, recv_sems, after)
    return list(outs[:n]), list(outs[n:])


def _pad_rows(a, rows):
    return jnp.pad(a, ((0, rows - a.shape[0]), (0, 0)))


def kernel(x, c, positions, mla_w_in, mla_q_norm, mla_w_qb, mla_kv_norm, mla_w_kvb, mla_w_o, hgrn_lb, hgrn_w_in, hgrn_g_norm, hgrn_w_o, ffn_w_in, ffn_w_out, ada_w, ada_b, ln_g, ln_b, loss_target, m_mla_w_in, m_mla_q_norm, m_mla_w_qb, m_mla_kv_norm, m_mla_w_kvb, m_mla_w_o, m_hgrn_lb, m_hgrn_w_in, m_hgrn_g_norm, m_hgrn_w_o, m_ffn_w_in, m_ffn_w_out, m_ada_w, m_ada_b, m_ln_g, m_ln_b, v_mla_w_in, v_mla_q_norm, v_mla_w_qb, v_mla_kv_norm, v_mla_w_kvb, v_mla_w_o, v_hgrn_lb, v_hgrn_w_in, v_hgrn_g_norm, v_hgrn_w_o, v_ffn_w_in, v_ffn_w_out, v_ada_w, v_ada_b, v_ln_g, v_ln_b):
    B, S, D = x.shape
    T = B * S
    depth = ada_w.shape[0]
    n_mla, n_hgrn = mla_w_in.shape[0], hgrn_w_in.shape[0]
    n_sub = 2 * depth
    alpha = (2.0 * depth) ** 0.25
    mx, my, mc = _my_place()
    me = 4 * mx + 2 * my + mc
    k_me = 2 * mx + my
    Bg = 8 * B
    HK = hgrn_w_o.shape[1] * 4
    dq = D // 4

    lbw = hgrn_lb.shape[1]
    first = jnp.zeros((8, max(D, 4 * lbw)), F32)
    first = first.at[:B, :D].set(c).at[B : B + n_hgrn, :lbw].set(hgrn_lb)
    first_all = _allgather8(first, "gather_cond")
    c_all = first_all[:, :B, :D].reshape(Bg, D)
    lb_logits = jnp.concatenate([first_all[2 * k, B : B + n_hgrn, :lbw] for k in range(4)], axis=1)

    def lower_bounds_fn(logits):
        soft = jax.nn.softmax(logits, axis=0)
        return jnp.cumsum(soft, axis=0) - soft[0]

    lower_bounds, lower_bounds_vjp = jax.vjp(lower_bounds_fn, lb_logits)

    n_ada = ada_w.shape[-1]
    mod_part = _ada_fwd(c_all, ada_w.reshape(n_sub, D, n_ada), ada_b.reshape(n_sub, 1, n_ada), "ada_fwd")
    mod_all = _allgather8(mod_part.reshape(n_sub * Bg, n_ada), "gather_mod").reshape(8, n_sub, Bg, n_ada)
    mod = jnp.concatenate([mod_all[2 * k] for k in range(4)], axis=-1)
    mod = lax.dynamic_slice_in_dim(mod, me * B, B, axis=1)
    shift = [mod[j, :, None, :D] for j in range(n_sub)]
    scale = [mod[j, :, None, D : 2 * D] for j in range(n_sub)]
    gate = [mod[j, :, None, 2 * D :] for j in range(n_sub)]

    ln_rows = 2 * n_sub
    ln_local = _pad_rows(jnp.concatenate([ln_g.reshape(n_sub, dq), ln_b.reshape(n_sub, dq)], axis=0), -(-ln_rows // 8) * 8)
    ln_pad = jnp.zeros((ln_local.shape[0], -(-dq // LANES) * LANES), F32).at[:, :dq].set(ln_local)
    ln_all = _allgather8(ln_pad, "gather_ln")
    ln_full = jnp.concatenate([ln_all[2 * k, :ln_rows, :dq] for k in range(4)], axis=1)
    lng = [ln_full[j][None, :] for j in range(n_sub)]
    lnb = [ln_full[n_sub + j][None, :] for j in range(n_sub)]

    main = dict(mla_w_in=mla_w_in, mla_w_qb=mla_w_qb, mla_w_kvb=mla_w_kvb, mla_w_o=mla_w_o, hgrn_w_in=hgrn_w_in,
                hgrn_w_o=hgrn_w_o, ffn_w_in=ffn_w_in, ffn_w_out=ffn_w_out)
    names = list(main)

    def group_kinds(layer, part):
        if part:
            return [("ffn_w_in", layer), ("ffn_w_out", layer)]
        mixer = ["mla_w_in", "mla_w_qb", "mla_w_kvb", "mla_w_o"] if layer % 2 == 0 else ["hgrn_w_in", "hgrn_w_o"]
        return [(k, layer // 2) for k in mixer]

    gathers = {}
    after = mod_all[0, 0, :8, :LANES] + ln_all[0, :8, :LANES]
    for layer in range(depth):
        for part in range(2):
            lands = [lax.dynamic_update_index_in_dim(lax.empty((4,) + main[k].shape[1:], BF16), main[k][i].astype(BF16), k_me, 0)
                     for k, i in group_kinds(layer, part)]
            ssem, rsem, lands, after = _gather_start(lands, after, f"gather_start_l{layer}p{part}")
            gathers[layer, part] = (ssem, rsem, lands)
    scale[0] = scale[0] + after[0, 0]

    def row_w(g):
        return g.reshape(1, g.shape[0] * g.shape[1], g.shape[2])

    def full_w_in(g):
        return jnp.transpose(g, (1, 0, 2)).reshape(1, g.shape[1], 4 * g.shape[2])

    ang = positions.astype(F32)[..., None] * (ROPE_THETA ** (-jnp.arange(0, QK_ROPE, 2, dtype=F32) / QK_ROPE))
    cos, sin = jnp.cos(ang), jnp.sin(ang)

    gq = [mla_q_norm[j][None, :] for j in range(n_mla)]
    gkv = [mla_kv_norm[j][None, :] for j in range(n_mla)]
    gn = [hgrn_g_norm[j][None, :] for j in range(n_hgrn)]

    def r2(a):
        return a.reshape(T, a.shape[-1])

    def r3(a):
        return a.reshape(B, S, a.shape[-1])

    saved = []
    xs = x
    for layer in range(depth):
        j = layer // 2
        sub = 2 * layer
        tag = f"l{layer}"
        ssem, rsem, lands = gathers[layer, 0]
        lands = _gather_wait(ssem, rsem, lands, xs if layer else scale[0], f"gather_wait_{tag}p0")
        wl = {k: g for (k, _), g in zip(group_kinds(layer, 0), lands)}
        if layer == 0:
            h = _modulate(xs, scale[sub], shift[sub], f"mod_{tag}a")
        if layer % 2 == 0:
            wl["mla_w_in"] = full_w_in(wl["mla_w_in"])
            proj = r3(_mm_nn(r2(h), wl["mla_w_in"], F32, f"mla_in_{tag}"))
            qn, kvn = _mla_mid_fwd(proj, gq[j], gkv[j], f"mla_mid_{tag}")
            q = r3(_mm_nn(r2(qn), wl["mla_w_qb"], F32, f"mla_qb_{tag}"))
            kv = r3(_mm_nn(r2(kvn), wl["mla_w_kvb"], F32, f"mla_kvb_{tag}"))
            qh, kh, vh = _mla_prep_fwd(q, kv, proj, cos, sin, f"mla_prep_{tag}")
            o, lse = _attn_fwd(qh, kh, vh, f"attn_{tag}")
            wl["mla_w_o"] = row_w(wl["mla_w_o"])
            y = r3(_mm_nn(r2(o), wl["mla_w_o"], F32, f"mla_o_{tag}"))
            mix = (h, proj, qn, kvn, qh, kh, vh, o, lse)
        else:
            proj = r3(_mm_nn(r2(h), wl["hgrn_w_in"], F32, f"hgrn_in_{tag}"))
            og, o_pre, states = _hgrn_fwd(proj, lower_bounds[j][None, :], gn[j], f"hgrn_{tag}")
            wl["hgrn_w_o"] = row_w(wl["hgrn_w_o"])
            y = r3(_mm_nn(r2(og), wl["hgrn_w_o"], F32, f"hgrn_o_{tag}"))
            mix = (h, proj, og, o_pre, states)
        x1, h2 = _ln_mod_fwd(alpha, xs, y, gate[sub], lng[sub], lnb[sub], scale[sub + 1], shift[sub + 1], f"ln_{tag}a")
        ssem, rsem, lands = gathers[layer, 1]
        lands = _gather_wait(ssem, rsem, lands, x1, f"gather_wait_{tag}p1")
        wl.update({k: g for (k, _), g in zip(group_kinds(layer, 1), lands)})
        a, ug, uu = [r3(t_) for t_ in _ffn_in(r2(h2), wl["ffn_w_in"], f"ffn_in_{tag}")]
        wl["ffn_w_out"] = row_w(wl["ffn_w_out"])
        y2 = r3(_mm_nn(r2(a), wl["ffn_w_out"], F32, f"ffn_out_{tag}"))
        if layer + 1 < depth:
            x2, h_next = _ln_mod_fwd(alpha, x1, y2, gate[sub + 1], lng[sub + 1], lnb[sub + 1], scale[sub + 2], shift[sub + 2], f"ln_{tag}b")
        else:
            x2, h_next = _ln_fwd(alpha, x1, y2, gate[sub + 1], lng[sub + 1], lnb[sub + 1], f"ln_{tag}b"), None
        saved.append((xs, y, x1, y2, mix, h2, ug, uu, a, wl))
        xs, h = x2, h_next

    loss_local, dout = _loss_head(xs, loss_target, "loss_head")
    loss = lax.psum(loss_local, ("x", "y", "c"))

    gw = {k: [None] * main[k].shape[0] for k in names}
    land = {k: lax.empty((3,) + main[k].shape, BF16) for k in names}
    scatters = []
    d_shift, d_scale, d_gate = [None] * n_sub, [None] * n_sub, [None] * n_sub
    d_lng, d_lnb = [None] * n_sub, [None] * n_sub
    d_gq, d_gkv, d_gn, d_lbnd = [None] * n_mla, [None] * n_mla, [None] * n_hgrn, [None] * n_hgrn

    def rows4(g):
        return g.reshape(4, g.shape[1] // 4, g.shape[2])

    def scatter_kinds(layer, part):
        if part == 1 or layer % 2:
            return group_kinds(layer, part)
        mixer = group_kinds(layer, 0)
        return mixer[:1] if part == 0 else mixer[1:]

    def start_scatter(layer, part, params, at):
        kinds = scatter_kinds(layer, part)
        ssem, rsem, slabs_t, lands_t, token = _scatter_start(
            [gw[k][i] for k, i in kinds], [land[k] for k, _ in kinds], [i for _, i in kinds], f"scatter_start_l{layer}p{part}")
        for (k, i), s_t, l_t in zip(kinds, slabs_t, lands_t):
            gw[k][i], land[k] = s_t, l_t
        scatters.append((layer, part, ssem, rsem))
        if params is not None:
            params[at] = params[at] + token[0, 0]

    for layer in reversed(range(depth)):
        j = layer // 2
        sub = 2 * layer
        tag = f"l{layer}"
        xs, y, x1, y2, mix, h2, ug, uu, a, wl = saved[layer]
        if layer + 1 == depth:
            dxr, dy2, d_gate[sub + 1], d_lng[sub + 1], d_lnb[sub + 1] = _ln_bwd(
                alpha, dout, x1, y2, gate[sub + 1], lng[sub + 1], lnb[sub + 1], f"ln_bwd_{tag}b")
        else:
            dxr, dy2, d_gate[sub + 1], d_lng[sub + 1], d_lnb[sub + 1], d_scale[sub + 2], d_shift[sub + 2] = _ln_mod_bwd(
                alpha, dh, dxr, scale[sub + 2], x1, y2, gate[sub + 1], lng[sub + 1], lnb[sub + 1], f"ln_bwd_{tag}b")
        da = r3(_mm_nt(r2(dy2), wl["ffn_w_out"], F32, f"ffn_out_dx_{tag}"))
        gw["ffn_w_out"][layer] = rows4(_mm_tn(r2(a), r2(dy2), 1, BF16, f"ffn_out_dw_{tag}"))
        du = _swiglu_bwd(ug, uu, da, f"swiglu_bwd_{tag}")
        dh2 = r3(_mm_nt(r2(du), wl["ffn_w_in"], F32, f"ffn_in_dx_{tag}"))
        gw["ffn_w_in"][layer] = _mm_tn(r2(h2), r2(du), 4, BF16, f"ffn_in_dw_{tag}")
        start_scatter(layer, 1, gate, sub)
        dxr, dy, d_gate[sub], d_lng[sub], d_lnb[sub], d_scale[sub + 1], d_shift[sub + 1] = _ln_mod_bwd(
            alpha, dh2, dxr, scale[sub + 1], xs, y, gate[sub], lng[sub], lnb[sub], f"ln_bwd_{tag}a")
        if layer % 2 == 0:
            h, proj, qn, kvn, qh, kh, vh, o, lse = mix
            do = r3(_mm_nt(r2(dy), wl["mla_w_o"], BF16, f"mla_o_dx_{tag}"))
            gw["mla_w_o"][j] = rows4(_mm_tn(r2(o), r2(dy), 1, BF16, f"mla_o_dw_{tag}"))
            dqh, dkh, dvh = _attn_bwd(qh, kh, vh, o, do, lse, f"attn_bwd_{tag}")
            dq_, dkv_, dkr = _mla_prep_bwd(dqh, dkh, dvh, cos, sin, f"mla_prep_bwd_{tag}")
            dqn = r3(_mm_nt(r2(dq_), wl["mla_w_qb"], F32, f"mla_qb_dx_{tag}"))
            gw["mla_w_qb"][j] = _mm_tn(r2(qn), r2(dq_), 4, BF16, f"mla_qb_dw_{tag}")
            dkvn = r3(_mm_nt(r2(dkv_), wl["mla_w_kvb"], F32, f"mla_kvb_dx_{tag}"))
            gw["mla_w_kvb"][j] = _mm_tn(r2(kvn), r2(dkv_), 4, BF16, f"mla_kvb_dw_{tag}")
            start_scatter(layer, 2, gq, j)
            dproj, dgq_, dgkv_ = _mla_mid_bwd(proj, dqn, dkvn, dkr, gq[j], gkv[j], f"mla_mid_bwd_{tag}")
            d_gq[j], d_gkv[j] = dgq_.sum(0), dgkv_.sum(0)
            dh = r3(_mm_nt(r2(dproj), wl["mla_w_in"], F32, f"mla_in_dx_{tag}"))
            gwin = _mm_tn(r2(h), r2(dproj), 1, BF16, f"mla_in_dw_{tag}")[0]
            gw["mla_w_in"][j] = jnp.transpose(gwin.reshape(gwin.shape[0], 4, gwin.shape[1] // 4), (1, 0, 2))
        else:
            h, proj, og, o_pre, states = mix
            dog = r3(_mm_nt(r2(dy), wl["hgrn_w_o"], F32, f"hgrn_o_dx_{tag}"))
            gw["hgrn_w_o"][j] = rows4(_mm_tn(r2(og), r2(dy), 1, BF16, f"hgrn_o_dw_{tag}"))
            dq_, df_, di_, dg_, dlb_, dgn_ = _hgrn_bwd(proj, lower_bounds[j][None, :], gn[j], o_pre, states, dog, f"hgrn_bwd_{tag}")
            dproj = jnp.concatenate([dq_, df_, di_, dg_], axis=-1)
            d_lbnd[j] = dlb_.sum(0).reshape(1, HK)
            d_gn[j] = dgn_.sum((0, 1))
            dh = r3(_mm_nt(r2(dproj), wl["hgrn_w_in"], F32, f"hgrn_in_dx_{tag}"))
            gw["hgrn_w_in"][j] = _mm_tn(r2(h), r2(dproj), 4, BF16, f"hgrn_in_dw_{tag}")
        start_scatter(layer, 0, gate if layer else None, sub - 1)
    grad_x, d_scale[0], d_shift[0] = _mod_bwd(dh, dxr, x, scale[0], "mod_bwd_l0a")

    for layer, part, ssem, rsem in scatters:
        kinds = scatter_kinds(layer, part)
        slabs_t, lands_t = _scatter_wait(
            ssem, rsem, [gw[k][i] for k, i in kinds], [land[k] for k, _ in kinds], [i for _, i in kinds], grad_x,
            f"scatter_wait_l{layer}p{part}")
        for (k, i), s_t, l_t in zip(kinds, slabs_t, lands_t):
            gw[k][i], land[k] = s_t, l_t
    sums = [_sum4(jnp.stack([lax.dynamic_index_in_dim(g, k_me, 0, keepdims=False) for g in gw[k]]), land[k], f"sum4_{k}")
            for k in names]

    dmod = jnp.stack([jnp.concatenate([d_shift[s_][:, 0], d_scale[s_][:, 0], d_gate[s_][:, 0]], axis=-1) for s_ in range(n_sub)])
    dmod_rows = _pad_rows(dmod.reshape(n_sub * B, 3 * D), -(-n_sub * B // 8) * 8)
    dmod_all = _allgather8(dmod_rows, "gather_dmod")[:, : n_sub * B].reshape(8, n_sub, B, 3 * D)
    dmod_all = jnp.transpose(dmod_all, (1, 0, 2, 3)).reshape(n_sub, Bg, 3 * D)
    dmod_mine = lax.dynamic_slice_in_dim(dmod_all, k_me * n_ada, n_ada, axis=2)
    g_ada_w, g_ada_b = _ada_bwd(c_all, dmod_mine, "ada_bwd")
    g_ada_w = g_ada_w.reshape(ada_w.shape)
    g_ada_b = g_ada_b.reshape(ada_b.shape)

    small = [jnp.stack(d_gq).reshape(-1), jnp.stack(d_gkv).reshape(-1), jnp.stack(d_gn).reshape(-1),
             jnp.stack(d_lbnd).reshape(-1), jnp.stack([d.sum(0) for d in d_lng]).reshape(-1),
             jnp.stack([d.sum(0) for d in d_lnb]).reshape(-1)]
    sizes = [s_.shape[0] for s_ in small]
    flat = jnp.concatenate(small)
    rows_small = -(-flat.shape[0] // (8 * LANES)) * 8
    flat = jnp.pad(flat, (0, rows_small * LANES - flat.shape[0])).reshape(rows_small, LANES)
    tot = _allgather8(flat, "gather_small")
    acc = tot[0]
    for d in range(1, 8):
        acc = acc + tot[d]
    acc = acc.reshape(-1)
    offs = [0]
    for s_ in sizes:
        offs.append(offs[-1] + s_)
    g_q_norm = acc[offs[0] : offs[1]].reshape(mla_q_norm.shape)
    g_kv_norm = acc[offs[1] : offs[2]].reshape(mla_kv_norm.shape)
    g_g_norm = acc[offs[2] : offs[3]].reshape(hgrn_g_norm.shape)
    g_lbnd = acc[offs[3] : offs[4]].reshape(n_hgrn, HK)
    g_lb_full = lower_bounds_vjp(g_lbnd)[0]
    g_hgrn_lb = lax.dynamic_slice_in_dim(g_lb_full, k_me * lbw, lbw, axis=1)
    g_lng = lax.dynamic_slice_in_dim(acc[offs[4] : offs[5]].reshape(n_sub, D), k_me * dq, dq, axis=1).reshape(ln_g.shape)
    g_lnb = lax.dynamic_slice_in_dim(acc[offs[5] : offs[6]].reshape(n_sub, D), k_me * dq, dq, axis=1).reshape(ln_b.shape)

    weights = dict(mla_w_in=mla_w_in, mla_q_norm=mla_q_norm, mla_w_qb=mla_w_qb, mla_kv_norm=mla_kv_norm, mla_w_kvb=mla_w_kvb,
                   mla_w_o=mla_w_o, hgrn_lb=hgrn_lb, hgrn_w_in=hgrn_w_in, hgrn_g_norm=hgrn_g_norm, hgrn_w_o=hgrn_w_o,
                   ffn_w_in=ffn_w_in, ffn_w_out=ffn_w_out, ada_w=ada_w, ada_b=ada_b, ln_g=ln_g, ln_b=ln_b)
    moms = dict(mla_w_in=(m_mla_w_in, v_mla_w_in), mla_q_norm=(m_mla_q_norm, v_mla_q_norm), mla_w_qb=(m_mla_w_qb, v_mla_w_qb),
                mla_kv_norm=(m_mla_kv_norm, v_mla_kv_norm), mla_w_kvb=(m_mla_w_kvb, v_mla_w_kvb), mla_w_o=(m_mla_w_o, v_mla_w_o),
                hgrn_lb=(m_hgrn_lb, v_hgrn_lb), hgrn_w_in=(m_hgrn_w_in, v_hgrn_w_in), hgrn_g_norm=(m_hgrn_g_norm, v_hgrn_g_norm),
                hgrn_w_o=(m_hgrn_w_o, v_hgrn_w_o), ffn_w_in=(m_ffn_w_in, v_ffn_w_in), ffn_w_out=(m_ffn_w_out, v_ffn_w_out),
                ada_w=(m_ada_w, v_ada_w), ada_b=(m_ada_b, v_ada_b), ln_g=(m_ln_g, v_ln_g), ln_b=(m_ln_b, v_ln_b))
    grads = dict(mla_q_norm=(g_q_norm,), mla_kv_norm=(g_kv_norm,), hgrn_lb=(g_hgrn_lb,), hgrn_g_norm=(g_g_norm,),
                 ada_w=(g_ada_w,), ada_b=(g_ada_b,), ln_g=(g_lng,), ln_b=(g_lnb,))

    def adamw(k):
        return _adamw(weights[k], [g_.reshape(weights[k].shape) for g_ in grads[k]], moms[k][0], moms[k][1], f"adamw_{k}")

    ssem, rsem, sums, others = _swap_start(sums, tot[0, :8] + dmod_all[0, :8, :LANES], "swap_start")
    res = {k: adamw(k) for k in grads}
    sums, others = _swap_wait(ssem, rsem, sums, others, res["ada_w"][1], "swap_wait")
    grads.update({k: (a_, b_) for k, a_, b_ in zip(names, sums, others)})
    res.update({k: adamw(k) for k in names})
    order = list(weights)
    return (loss, grad_x, *[res[k][0] for k in order], *[res[k][1] for k in order], *[res[k][2] for k in order],
            *[res[k][3] for k in order])
```

```python
import functools

import jax
import jax.numpy as jnp
from jax import lax
from jax.experimental import pallas as pl
from jax.experimental.pallas import tpu as pltpu

F32 = jnp.float32
BF16 = jnp.bfloat16
SDS = jax.ShapeDtypeStruct
MESH = pl.DeviceIdType.MESH
HI = lax.Precision.HIGHEST
MID = lax.Precision.HIGH

MLA_HEADS, QK_NOPE, QK_ROPE, V_HEAD = 16, 64, 32, 64
Q_LORA, KV_LORA = 768, 256
QK_DIM = QK_NOPE + QK_ROPE
ROPE_THETA = 10000.0
HGRN_K = 128
HGRN_CHUNK = 128
HGRN_SUB = 32
HGRN_PAR = 2
LN_EPS, RMS_EPS = 1e-5, 1e-6
ADAM_LR, ADAM_B1, ADAM_B2, ADAM_EPS, ADAM_WD, ADAM_STEP = 0.001, 0.9, 0.999, 1e-08, 0.01, 10
NEG = -1e30

VMEM_LIMIT_BYTES = 56 * 1024 * 1024
RESIDENT_WEIGHT_BYTES = 12 * 1024 * 1024
LANES = 128
SUBLANES = 8


def _cparams(*sem):
    return pltpu.CompilerParams(dimension_semantics=sem if sem else None, vmem_limit_bytes=VMEM_LIMIT_BYTES)


def _pick_tile(n, cap):
    best = 0
    for t in range(LANES, min(n, cap) + 1, LANES):
        if n % t == 0:
            best = t
    return best if best else n


def _bdot(a, b):
    return jnp.dot(a.astype(BF16), b.astype(BF16), preferred_element_type=F32)


def _bdot_nt(a, b):
    return lax.dot_general(a.astype(BF16), b.astype(BF16), (((1,), (1,)), ((), ())), preferred_element_type=F32)


def _bdot_tn(a, b):
    return lax.dot_general(a.astype(BF16), b.astype(BF16), (((0,), (0,)), ((), ())), preferred_element_type=F32)


def _hdot(a, b):
    return jnp.dot(a, b, precision=HI, preferred_element_type=F32)


def _mdot(a, b):
    return jnp.dot(a, b, precision=MID, preferred_element_type=F32)


def _mdot_nt(a, b):
    return lax.dot_general(a, b, (((1,), (1,)), ((), ())), precision=MID, preferred_element_type=F32)


def _mdot_tn(a, b):
    return lax.dot_general(a, b, (((0,), (0,)), ((), ())), precision=MID, preferred_element_type=F32)


def _mm_nn(a, w, out_dtype, name):
    M, K = a.shape
    G, _, n = w.shape
    tm = min(512, M)
    tn = _pick_tile(n, 1536)
    nps = n // tn

    if G > 1 and w.size * 2 <= RESIDENT_WEIGHT_BYTES and n % LANES == 0:
        def body_all(a_ref, w_ref, o_ref):
            av = a_ref[...]
            for s in range(G):
                o_ref[:, s * n : (s + 1) * n] = _bdot(av, w_ref[s]).astype(o_ref.dtype)

        return pl.pallas_call(
            body_all,
            grid=(M // tm,),
            in_specs=[pl.BlockSpec((tm, K), lambda i: (i, 0)), pl.BlockSpec((G, K, n), lambda i: (0, 0, 0))],
            out_specs=pl.BlockSpec((tm, G * n), lambda i: (i, 0)),
            out_shape=SDS((M, G * n), out_dtype),
            name=name,
            compiler_params=_cparams("parallel"),
        )(a, w)

    def body(a_ref, w_ref, o_ref):
        o_ref[...] = _bdot(a_ref[...], w_ref[...]).astype(o_ref.dtype)

    return pl.pallas_call(
        body,
        grid=(G * nps, M // tm),
        in_specs=[
            pl.BlockSpec((tm, K), lambda j, i: (i, 0)),
            pl.BlockSpec((None, K, tn), lambda j, i: (j // nps, 0, j % nps)),
        ],
        out_specs=pl.BlockSpec((tm, tn), lambda j, i: (i, j)),
        out_shape=SDS((M, G * n), out_dtype),
        name=name,
        compiler_params=_cparams("parallel", "parallel"),
    )(a, w)


def _mm_nt(a, w, out_dtype, name):
    M = a.shape[0]
    G, K, n = w.shape
    tm = min(512, M)
    tk = _pick_tile(K, 1536)

    if w.size * 2 <= RESIDENT_WEIGHT_BYTES:
        def body_all(a_ref, w_ref, o_ref):
            acc = _bdot_nt(a_ref[:, :n], w_ref[0])
            for s in range(1, G):
                acc = acc + _bdot_nt(a_ref[:, s * n : (s + 1) * n], w_ref[s])
            o_ref[...] = acc.astype(o_ref.dtype)

        return pl.pallas_call(
            body_all,
            grid=(M // tm,),
            in_specs=[pl.BlockSpec((tm, G * n), lambda i: (i, 0)), pl.BlockSpec((G, K, n), lambda i: (0, 0, 0))],
            out_specs=pl.BlockSpec((tm, K), lambda i: (i, 0)),
            out_shape=SDS((M, K), out_dtype),
            name=name,
            compiler_params=_cparams("parallel"),
        )(a, w)

    def body(a_ref, w_ref, o_ref, acc_ref):
        s = pl.program_id(2)

        @pl.when(s == 0)
        def _():
            acc_ref[...] = jnp.zeros_like(acc_ref)

        acc_ref[...] += _bdot_nt(a_ref[...], w_ref[...])

        @pl.when(s == G - 1)
        def _():
            o_ref[...] = acc_ref[...].astype(o_ref.dtype)

    return pl.pallas_call(
        body,
        grid=(K // tk, M // tm, G),
        in_specs=[
            pl.BlockSpec((tm, n), lambda kb, i, s: (i, s)),
            pl.BlockSpec((None, tk, n), lambda kb, i, s: (s, kb, 0)),
        ],
        out_specs=pl.BlockSpec((tm, tk), lambda kb, i, s: (i, kb)),
        out_shape=SDS((M, K), out_dtype),
        scratch_shapes=[pltpu.VMEM((tm, tk), F32)],
        name=name,
        compiler_params=_cparams("parallel", "parallel", "arbitrary"),
    )(a, w)


def _mm_tn(a, d, G, out_dtype, name):
    T, K = a.shape
    n = d.shape[1] // G
    tk = _pick_tile(K, 512)
    tn = _pick_tile(n, 1536)
    nps = n // tn

    def body(a_ref, d_ref, o_ref):
        o_ref[...] = _bdot_tn(a_ref[...], d_ref[...]).astype(o_ref.dtype)

    return pl.pallas_call(
        body,
        grid=(G * nps, K // tk),
        in_specs=[
            pl.BlockSpec((T, tk), lambda j, i: (0, i)),
            pl.BlockSpec((T, tn), lambda j, i: (0, j)),
        ],
        out_specs=pl.BlockSpec((None, tk, tn), lambda j, i: (j // nps, i, j % nps)),
        out_shape=SDS((G, K, n), out_dtype),
        name=name,
        compiler_params=_cparams("parallel", "parallel"),
    )(a, d)


def _rows_call(body, name, B, S, ins, outs, ts=256):
    ts = min(ts, S)
    in_specs, args = [], []
    for arr, kind in ins:
        W = arr.shape[-1]
        if kind == "row":
            in_specs.append(pl.BlockSpec((None, ts, W), lambda b, s: (b, s, 0)))
        elif kind == "ex":
            in_specs.append(pl.BlockSpec((None, 1, W), lambda b, s: (b, 0, 0)))
        else:
            in_specs.append(pl.BlockSpec((1, W), lambda b, s: (0, 0)))
        args.append(arr)
    out_specs, out_shape = [], []
    for W, dt, kind in outs:
        if kind == "row":
            out_specs.append(pl.BlockSpec((None, ts, W), lambda b, s: (b, s, 0)))
            out_shape.append(SDS((B, S, W), dt))
        else:
            out_specs.append(pl.BlockSpec((None, 1, W), lambda b, s: (b, 0, 0)))
            out_shape.append(SDS((B, 1, W), dt))
    return pl.pallas_call(
        body,
        grid=(B, S // ts),
        in_specs=in_specs,
        out_specs=out_specs,
        out_shape=out_shape,
        name=name,
        compiler_params=_cparams("parallel", "arbitrary"),
    )(*args)


def _acc(ref, val):
    @pl.when(pl.program_id(1) == 0)
    def _():
        ref[...] = jnp.zeros_like(ref)

    ref[...] += val


def _mod_fn(x, sc, sh):
    return x * (1.0 + sc) + sh


def _ln_fn(alpha, x, y, gate, g, b):
    z = alpha * x + (1.0 + gate) * y
    mu = jnp.mean(z, -1, keepdims=True)
    var = jnp.mean(jnp.square(z - mu), -1, keepdims=True)
    return (z - mu) * lax.rsqrt(var + LN_EPS) * g + b


def _modulate(x, sc, sh, name):
    B, S, D = x.shape

    def body(x_ref, sc_ref, sh_ref, h_ref):
        h_ref[...] = _mod_fn(x_ref[...], sc_ref[...], sh_ref[...]).astype(BF16)

    return _rows_call(body, name, B, S, [(x, "row"), (sc, "ex"), (sh, "ex")], [(D, BF16, "row")])[0]


def _ln_fwd(alpha, x, y, gate, g, b, name):
    B, S, D = x.shape

    def body(x_ref, y_ref, gate_ref, g_ref, b_ref, o_ref):
        o_ref[...] = _ln_fn(alpha, x_ref[...], y_ref[...], gate_ref[...], g_ref[...], b_ref[...])

    return _rows_call(
        body, name, B, S, [(x, "row"), (y, "row"), (gate, "ex"), (g, "par"), (b, "par")], [(D, F32, "row")]
    )[0]


def _ln_mod_fwd(alpha, x, y, gate, g, b, sc_next, sh_next, name):
    B, S, D = x.shape

    def body(x_ref, y_ref, gate_ref, g_ref, b_ref, sc_ref, sh_ref, o_ref, h_ref):
        out = _ln_fn(alpha, x_ref[...], y_ref[...], gate_ref[...], g_ref[...], b_ref[...])
        o_ref[...] = out
        h_ref[...] = _mod_fn(out, sc_ref[...], sh_ref[...]).astype(BF16)

    return _rows_call(
        body, name, B, S,
        [(x, "row"), (y, "row"), (gate, "ex"), (g, "par"), (b, "par"), (sc_next, "ex"), (sh_next, "ex")],
        [(D, F32, "row"), (D, BF16, "row")],
    )


def _ln_mod_bwd(alpha, dh, dxr_next, sc_next, x, y, gate, g, b, name):
    B, S, D = x.shape

    def body(dh_ref, dxr_ref, sc_ref, x_ref, y_ref, gate_ref, g_ref, b_ref,
             dx_ref, dy_ref, dgate_ref, dg_ref, db_ref, dsc_ref, dsh_ref):
        out, vjp = jax.vjp(
            functools.partial(_ln_fn, alpha), x_ref[...], y_ref[...], gate_ref[...], g_ref[...], b_ref[...]
        )
        dh_v = dh_ref[...]
        dx, dy, dgate, dg, db = vjp(dxr_ref[...] + dh_v * (1.0 + sc_ref[...]))
        dx_ref[...] = dx
        dy_ref[...] = dy.astype(BF16)
        _acc(dgate_ref, dgate)
        _acc(dg_ref, dg)
        _acc(db_ref, db)
        _acc(dsc_ref, jnp.sum(dh_v * out, axis=0, keepdims=True))
        _acc(dsh_ref, jnp.sum(dh_v, axis=0, keepdims=True))

    return _rows_call(
        body, name, B, S,
        [(dh, "row"), (dxr_next, "row"), (sc_next, "ex"), (x, "row"), (y, "row"), (gate, "ex"), (g, "par"), (b, "par")],
        [(D, F32, "row"), (D, BF16, "row")] + [(D, F32, "acc")] * 5,
    )


def _ln_bwd(alpha, dout, x, y, gate, g, b, name):
    B, S, D = x.shape

    def body(do_ref, x_ref, y_ref, gate_ref, g_ref, b_ref, dxr_ref, dy_ref, dgate_ref, dg_ref, db_ref):
        _, vjp = jax.vjp(
            functools.partial(_ln_fn, alpha), x_ref[...], y_ref[...], gate_ref[...], g_ref[...], b_ref[...]
        )
        dx, dy, dgate, dg, db = vjp(do_ref[...])
        dxr_ref[...] = dx
        dy_ref[...] = dy.astype(BF16)
        _acc(dgate_ref, dgate)
        _acc(dg_ref, dg)
        _acc(db_ref, db)

    return _rows_call(
        body,
        name,
        B,
        S,
        [(dout, "row"), (x, "row"), (y, "row"), (gate, "ex"), (g, "par"), (b, "par")],
        [(D, F32, "row"), (D, BF16, "row"), (D, F32, "acc"), (D, F32, "acc"), (D, F32, "acc")],
    )


def _mod_bwd(dh, dxr, x, sc, name):
    B, S, D = x.shape

    def body(dh_ref, dxr_ref, x_ref, sc_ref, dx_ref, dsc_ref, dsh_ref):
        dh_v = dh_ref[...]
        dx_ref[...] = dxr_ref[...] + dh_v * (1.0 + sc_ref[...])
        _acc(dsc_ref, jnp.sum(dh_v * x_ref[...], axis=0, keepdims=True))
        _acc(dsh_ref, jnp.sum(dh_v, axis=0, keepdims=True))

    return _rows_call(
        body,
        name,
        B,
        S,
        [(dh, "row"), (dxr, "row"), (x, "row"), (sc, "ex")],
        [(D, F32, "row"), (D, F32, "acc"), (D, F32, "acc")],
    )


def _loss_head(y, target, name):
    B, S, D = y.shape

    def body(y_ref, t_ref, l_ref, dy_ref):
        e = y_ref[...] - t_ref[...]
        dy_ref[...] = e * (1.0 / D)
        part = 0.5 * jnp.sum(jnp.sum(e * e, axis=1, keepdims=True) * (1.0 / D), axis=0, keepdims=True)
        _acc(l_ref, jnp.broadcast_to(part, (1, LANES)))

    loss, dy = _rows_call(
        body, name, B, S, [(y, "row"), (target, "row")], [(LANES, F32, "acc"), (D, F32, "row")]
    )
    return jnp.sum(loss[:, 0, 0]), dy


def _ffn_in(h, w, name):
    M, K = h.shape
    G, _, n = w.shape
    assert G == 4
    tm = min(512, M)
    tn = _pick_tile(n, 1536)
    nps = n // tn
    half = 2 * nps

    def body(h_ref, wg_ref, wu_ref, a_ref, g_ref, u_ref):
        hv = h_ref[...]
        g = _bdot(hv, wg_ref[...])
        u = _bdot(hv, wu_ref[...])
        a_ref[...] = (jax.nn.silu(g) * u).astype(BF16)
        g_ref[...] = g.astype(BF16)
        u_ref[...] = u.astype(BF16)

    out = pl.BlockSpec((tm, tn), lambda j, i: (i, j))
    return pl.pallas_call(
        body,
        grid=(half, M // tm),
        in_specs=[
            pl.BlockSpec((tm, K), lambda j, i: (i, 0)),
            pl.BlockSpec((None, K, tn), lambda j, i: (j // nps, 0, j % nps)),
            pl.BlockSpec((None, K, tn), lambda j, i: (2 + j // nps, 0, j % nps)),
        ],
        out_specs=[out, out, out],
        out_shape=[SDS((M, 2 * n), BF16)] * 3,
        name=name,
        compiler_params=_cparams("parallel", "parallel"),
    )(h, w, w)


def _swiglu_bwd(g, u, da, name):
    B, S, F = g.shape

    def body(g_ref, u_ref, da_ref, du_ref):
        _, vjp = jax.vjp(lambda gv, uv: jax.nn.silu(gv) * uv, g_ref[...].astype(F32), u_ref[...].astype(F32))
        dg, du = vjp(da_ref[...])
        du_ref[:, :F] = dg.astype(BF16)
        du_ref[:, F:] = du.astype(BF16)

    return _rows_call(body, name, B, S, [(g, "row"), (u, "row"), (da, "row")], [(2 * F, BF16, "row")])[0]


def _rms_fn(x, g):
    return x * lax.rsqrt(jnp.mean(jnp.square(x), -1, keepdims=True) + RMS_EPS) * g


def _mla_mid_fwd(proj, gq, gkv, name):
    B, S, _ = proj.shape

    def body(p_ref, gq_ref, gkv_ref, qn_ref, kvn_ref):
        p = p_ref[...]
        qn_ref[...] = _rms_fn(p[:, :Q_LORA], gq_ref[...]).astype(BF16)
        kvn_ref[...] = _rms_fn(p[:, Q_LORA : Q_LORA + KV_LORA], gkv_ref[...]).astype(BF16)

    return _rows_call(
        body, name, B, S, [(proj, "row"), (gq, "par"), (gkv, "par")], [(Q_LORA, BF16, "row"), (KV_LORA, BF16, "row")]
    )


def _mla_mid_bwd(proj, dqn, dkvn, dkr, gq, gkv, name):
    B, S, W = proj.shape

    def body(p_ref, dqn_ref, dkvn_ref, dkr_ref, gq_ref, gkv_ref, dp_ref, dgq_ref, dgkv_ref):
        p = p_ref[...]
        _, vq = jax.vjp(_rms_fn, p[:, :Q_LORA], gq_ref[...])
        dql, dgq = vq(dqn_ref[...])
        _, vkv = jax.vjp(_rms_fn, p[:, Q_LORA : Q_LORA + KV_LORA], gkv_ref[...])
        dkvl, dgkv = vkv(dkvn_ref[...])
        dp_ref[:, :Q_LORA] = dql.astype(BF16)
        dp_ref[:, Q_LORA : Q_LORA + KV_LORA] = dkvl.astype(BF16)
        dp_ref[:, Q_LORA + KV_LORA :] = dkr_ref[...].astype(BF16)
        _acc(dgq_ref, dgq)
        _acc(dgkv_ref, dgkv)

    return _rows_call(
        body,
        name,
        B,
        S,
        [(proj, "row"), (dqn, "row"), (dkvn, "row"), (dkr, "row"), (gq, "par"), (gkv, "par")],
        [(W, BF16, "row"), (Q_LORA, F32, "acc"), (KV_LORA, F32, "acc")],
    )


def _rope(x, cos, sin):
    h = QK_ROPE // 2
    x1, x2 = x[:, :h], x[:, h:]
    return jnp.concatenate([x1 * cos - x2 * sin, x1 * sin + x2 * cos], axis=1)


def _rope_t(dy, cos, sin):
    h = QK_ROPE // 2
    d1, d2 = dy[:, :h], dy[:, h:]
    return jnp.concatenate([d1 * cos + d2 * sin, d2 * cos - d1 * sin], axis=1)


def _heads_call(body, name, B, S, ins, outs, ts=256):
    ts = min(ts, S)
    in_specs, args = [], []
    for arr, kind in ins:
        if kind == "row":
            in_specs.append(pl.BlockSpec((None, ts, arr.shape[-1]), lambda b, s: (b, s, 0)))
        else:
            in_specs.append(pl.BlockSpec((arr.shape[0], None, ts, arr.shape[-1]), lambda b, s: (0, b, s, 0)))
        args.append(arr)
    out_specs, out_shape = [], []
    for shape, dt, kind in outs:
        if kind == "row":
            out_specs.append(pl.BlockSpec((None, ts, shape[-1]), lambda b, s: (b, s, 0)))
        else:
            out_specs.append(pl.BlockSpec((shape[0], None, ts, shape[-1]), lambda b, s: (0, b, s, 0)))
        out_shape.append(SDS(shape, dt))
    return pl.pallas_call(
        body,
        grid=(B, S // ts),
        in_specs=in_specs,
        out_specs=out_specs,
        out_shape=out_shape,
        name=name,
        compiler_params=_cparams("parallel", "parallel"),
    )(*args)


def _mla_prep_fwd(q, kv, proj, cos, sin, name):
    B, S, _ = q.shape
    H = MLA_HEADS

    def body(q_ref, kv_ref, p_ref, cos_ref, sin_ref, qh_ref, kh_ref, vh_ref):
        cos_v, sin_v = cos_ref[...], sin_ref[...]
        kr = _rope(p_ref[:, Q_LORA + KV_LORA :], cos_v, sin_v).astype(BF16)
        for h in range(H):
            qn = q_ref[:, h * QK_DIM : h * QK_DIM + QK_NOPE]
            qr = _rope(q_ref[:, h * QK_DIM + QK_NOPE : (h + 1) * QK_DIM], cos_v, sin_v)
            qh_ref[h] = jnp.concatenate([qn, qr], axis=1).astype(BF16)
            kn = kv_ref[:, h * 128 : h * 128 + QK_NOPE].astype(BF16)
            kh_ref[h] = jnp.concatenate([kn, kr], axis=1)
            vh_ref[h] = kv_ref[:, h * 128 + QK_NOPE : (h + 1) * 128].astype(BF16)

    return _heads_call(
        body,
        name,
        B,
        S,
        [(q, "row"), (kv, "row"), (proj, "row"), (cos, "row"), (sin, "row")],
        [((H, B, S, QK_DIM), BF16, "heads"), ((H, B, S, QK_DIM), BF16, "heads"), ((H, B, S, V_HEAD), BF16, "heads")],
    )


def _mla_prep_bwd(dqh, dkh, dvh, cos, sin, name):
    H, B, S, _ = dqh.shape

    def body(dqh_ref, dkh_ref, dvh_ref, cos_ref, sin_ref, dq_ref, dkv_ref, dkr_ref):
        cos_v, sin_v = cos_ref[...], sin_ref[...]
        dkr = jnp.zeros((cos_v.shape[0], QK_ROPE), F32)
        for h in range(H):
            dqv = dqh_ref[h].astype(F32)
            dq_ref[:, h * QK_DIM : h * QK_DIM + QK_NOPE] = dqv[:, :QK_NOPE].astype(BF16)
            dq_ref[:, h * QK_DIM + QK_NOPE : (h + 1) * QK_DIM] = _rope_t(dqv[:, QK_NOPE:], cos_v, sin_v).astype(BF16)
            dkv = dkh_ref[h].astype(F32)
            dkv_ref[:, h * 128 : h * 128 + QK_NOPE] = dkv[:, :QK_NOPE].astype(BF16)
            dkv_ref[:, h * 128 + QK_NOPE : (h + 1) * 128] = dvh_ref[h]
            dkr = dkr + dkv[:, QK_NOPE:]
        dkr_ref[...] = _rope_t(dkr, cos_v, sin_v)

    return _heads_call(
        body,
        name,
        B,
        S,
        [(dqh, "heads"), (dkh, "heads"), (dvh, "heads"), (cos, "row"), (sin, "row")],
        [((B, S, H * QK_DIM), BF16, "row"), ((B, S, H * 128), BF16, "row"), ((B, S, QK_ROPE), F32, "row")],
    )


LOG2E = 1.4426950408889634
ATTN_TILE = 1024
ATTN_DIAG_SUB = 512


def _attn_fwd(qh, kh, vh, name):
    H, B, S, _ = qh.shape
    t = min(ATTN_TILE, S)
    scale = QK_DIM**-0.5
    c2 = scale * LOG2E

    def body(q_ref, k_ref, v_ref, o_ref, lse_ref):
        i = pl.program_id(2)
        qs = [q_ref[0], q_ref[1]]

        def update(state, q, k, v, mask):
            m, l, acc = state
            s = _bdot_nt(q, k)
            if mask is not None:
                s = jnp.where(mask, s, NEG)
            m_new = jnp.maximum(m, jnp.max(s, axis=1, keepdims=True))
            p = jnp.exp2((s - m_new) * c2)
            a = jnp.exp2((m - m_new) * c2)
            return m_new, a * l + jnp.sum(p, axis=1, keepdims=True), a * acc + _bdot(p, v)

        def step(j, carry):
            rows = pl.ds(pl.multiple_of(j * t, t), t)
            return tuple(update(carry[hh], qs[hh], k_ref[hh, rows, :], v_ref[hh, rows, :], None) for hh in range(2))

        one = (jnp.full((t, 1), NEG, F32), jnp.zeros((t, 1), F32), jnp.zeros((t, V_HEAD), F32))
        carry = lax.fori_loop(0, i, step, (one, one))
        rows = pl.ds(pl.multiple_of(i * t, t), t)
        causal = lax.broadcasted_iota(jnp.int32, (t, t), 0) >= lax.broadcasted_iota(jnp.int32, (t, t), 1)
        carry = tuple(update(carry[hh], qs[hh], k_ref[hh, rows, :], v_ref[hh, rows, :], causal) for hh in range(2))
        outs = []
        for hh in range(2):
            m, l, acc = carry[hh]
            outs.append(acc / l)
            lse_ref[hh] = m * scale + jnp.log(l)
        o_ref[...] = jnp.concatenate(outs, axis=1).astype(BF16)

    return pl.pallas_call(
        body,
        grid=(B, H // 2, S // t),
        in_specs=[
            pl.BlockSpec((2, None, t, QK_DIM), lambda b, p, i: (p, b, i, 0)),
            pl.BlockSpec((2, None, S, QK_DIM), lambda b, p, i: (p, b, 0, 0)),
            pl.BlockSpec((2, None, S, V_HEAD), lambda b, p, i: (p, b, 0, 0)),
        ],
        out_specs=[
            pl.BlockSpec((None, t, 2 * V_HEAD), lambda b, p, i: (b, i, p)),
            pl.BlockSpec((2, None, t, 1), lambda b, p, i: (p, b, i, 0)),
        ],
        out_shape=[SDS((B, S, H * V_HEAD), BF16), SDS((H, B, S, 1), F32)],
        name=name,
        compiler_params=_cparams("parallel", "parallel", "arbitrary"),
    )(qh, kh, vh)


def _attn_bwd(qh, kh, vh, o, do, lse, name):
    H, B, S, _ = qh.shape
    t = min(ATTN_TILE, S)
    sub = min(ATTN_DIAG_SUB, t)
    nq = S // t
    scale = QK_DIM**-0.5
    c2 = scale * LOG2E

    def body(q_ref, k_ref, v_ref, o_ref, do_ref, lse_ref, dq_ref, dk_ref, dv_ref, dq_acc, delta_ref, lse2_ref):
        prod = o_ref[...].astype(F32) * do_ref[...].astype(F32)
        for hh in range(2):
            delta_ref[hh] = jnp.sum(prod[:, hh * V_HEAD : (hh + 1) * V_HEAD], axis=1, keepdims=True)
            lse2_ref[hh] = lse_ref[hh] * LOG2E
        dq_acc[...] = jnp.zeros_like(dq_acc)

        def kloop(j, _):
            krows = pl.ds(pl.multiple_of(j * t, t), t)
            ks = [k_ref[0, krows, :], k_ref[1, krows, :]]
            vs = [v_ref[0, krows, :], v_ref[1, krows, :]]

            def pair(hh, qrows, k, v, mask):
                q = q_ref[hh, qrows, :]
                do_h = do_ref[qrows, :][:, hh * V_HEAD : (hh + 1) * V_HEAD]
                p = jnp.exp2(_bdot_nt(q, k) * c2 - lse2_ref[hh, qrows, :])
                if mask is not None:
                    p = jnp.where(mask, p, 0.0)
                dv = _bdot_tn(p, do_h)
                ds = (p * (_bdot_nt(do_h, v) - delta_ref[hh, qrows, :])).astype(BF16)
                dq_acc[hh, qrows, :] += _bdot(ds, k)
                return _bdot_tn(ds, q), dv

            def qstep(i, carry):
                qrows = pl.ds(pl.multiple_of(i * t, t), t)
                out = []
                for hh in range(2):
                    dk, dv = pair(hh, qrows, ks[hh], vs[hh], None)
                    out.append((carry[hh][0] + dk, carry[hh][1] + dv))
                return tuple(out)

            def diagonal_step():
                out = []
                for hh in range(2):
                    dks, dvs = [], []
                    for c in range(t // sub):
                        r0 = c * sub
                        qrows = pl.ds(pl.multiple_of(j * t + r0, sub), t - r0)
                        mask = (lax.broadcasted_iota(jnp.int32, (t - r0, sub), 0)
                                >= lax.broadcasted_iota(jnp.int32, (t - r0, sub), 1))
                        dk, dv = pair(hh, qrows, ks[hh][r0 : r0 + sub], vs[hh][r0 : r0 + sub], mask)
                        dks.append(dk)
                        dvs.append(dv)
                    out.append((jnp.concatenate(dks, axis=0), jnp.concatenate(dvs, axis=0)))
                return tuple(out)

            carry = lax.fori_loop(j + 1, nq, qstep, diagonal_step())
            for hh in range(2):
                dk_ref[hh, krows, :] = (carry[hh][0] * scale).astype(BF16)
                dv_ref[hh, krows, :] = carry[hh][1].astype(BF16)
            return 0

        lax.fori_loop(0, nq, kloop, 0)
        dq_ref[...] = (dq_acc[...] * scale).astype(BF16)

    hspec = lambda w: pl.BlockSpec((2, None, S, w), lambda b, p: (p, b, 0, 0))
    ospec = pl.BlockSpec((None, S, 2 * V_HEAD), lambda b, p: (b, 0, p))
    return pl.pallas_call(
        body,
        grid=(B, H // 2),
        in_specs=[hspec(QK_DIM), hspec(QK_DIM), hspec(V_HEAD), ospec, ospec, hspec(1)],
        out_specs=[hspec(QK_DIM), hspec(QK_DIM), hspec(V_HEAD)],
        out_shape=[SDS((H, B, S, QK_DIM), BF16), SDS((H, B, S, QK_DIM), BF16), SDS((H, B, S, V_HEAD), BF16)],
        scratch_shapes=[pltpu.VMEM((2, S, QK_DIM), F32), pltpu.VMEM((2, S, 1), F32), pltpu.VMEM((2, S, 1), F32)],
        name=name,
        compiler_params=_cparams("parallel", "parallel"),
    )(qh, kh, vh, o, do, lse)


def _hgrn_pre(q, fx, lb):
    sig = jax.nn.sigmoid(fx)
    f = lb + (1.0 - lb) * sig
    return jax.nn.silu(q), 1.0 - f, jnp.log(f)


def _hgrn_gate(o, gg, gn):
    return _rms_fn(o, gn) * jax.nn.silu(gg)


def _tri(n, lower):
    r = lax.broadcasted_iota(jnp.int32, (n, n), 0)
    c = lax.broadcasted_iota(jnp.int32, (n, n), 1)
    return ((r >= c) if lower else (r <= c)).astype(F32)


def _hgrn_intra_fwd(qs, k, v, b):
    C, SB = qs.shape[0], min(HGRN_SUB, qs.shape[0])
    ridx = lax.broadcasted_iota(jnp.int32, (SUBLANES, 1), 0)
    outs = []
    for i in range(C // SB):
        r0 = i * SB
        qi, ki, vi, bi = qs[r0 : r0 + SB], k[r0 : r0 + SB], v[r0 : r0 + SB], b[r0 : r0 + SB]
        ng = SB // SUBLANES
        qg = [qi[g * SUBLANES : (g + 1) * SUBLANES] for g in range(ng)]
        bg = [bi[g * SUBLANES : (g + 1) * SUBLANES] for g in range(ng)]
        accg = [jnp.zeros((SUBLANES, v.shape[1]), F32) for _ in range(ng)]
        for s in range(SB):
            gs, so = divmod(s, SUBLANES)
            k_s, v_s, b_s = ki[s : s + 1], vi[s : s + 1], bi[s : s + 1]
            for tg in range(gs, ng):
                if tg == gs:
                    mask = ridx >= so
                    w = jnp.where(mask, qg[tg] * k_s * jnp.exp(jnp.where(mask, bg[tg] - b_s, 0.0)), 0.0)
                else:
                    w = qg[tg] * k_s * jnp.exp(bg[tg] - b_s)
                accg[tg] = accg[tg] + jnp.sum(w, axis=1, keepdims=True) * v_s
        acc = jnp.concatenate(accg, axis=0)
        if i > 0:
            ref = bi[0:1]
            qt = qi * jnp.exp(bi - ref)
            kt = k[:r0] * jnp.exp(ref - b[:r0])
            acc = acc + _bdot(_mdot_nt(qt, kt), v[:r0])
        outs.append(acc)
    return jnp.concatenate(outs, axis=0)


def _hgrn_intra_bwd(qs, k, v, b, do):
    C, SB = qs.shape[0], min(HGRN_SUB, qs.shape[0])
    nb = C // SB
    ridx = lax.broadcasted_iota(jnp.int32, (SUBLANES, 1), 0)
    dq_p = [None] * nb
    dk_p = [jnp.zeros((SB, k.shape[1]), F32) for _ in range(nb)]
    dv_p = [jnp.zeros((SB, v.shape[1]), F32) for _ in range(nb)]
    for i in range(nb):
        r0 = i * SB
        qi, ki, vi, bi, doi = qs[r0 : r0 + SB], k[r0 : r0 + SB], v[r0 : r0 + SB], b[r0 : r0 + SB], do[r0 : r0 + SB]
        ng = SB // SUBLANES
        qg = [qi[g * SUBLANES : (g + 1) * SUBLANES] for g in range(ng)]
        bg = [bi[g * SUBLANES : (g + 1) * SUBLANES] for g in range(ng)]
        dog = [doi[g * SUBLANES : (g + 1) * SUBLANES] for g in range(ng)]
        dqg =[jnp.zeros((SUBLANES, k.shape[1]), F32) for _ in range(ng)]
        dkg = [jnp.zeros((SUBLANES, k.shape[1]), F32) for _ in range(ng)]
        dvg = [jnp.zeros((SUBLANES, v.shape[1]), F32) for _ in range(ng)]
        for s in range(SB):
            gs, so = divmod(s, SUBLANES)
            k_s, v_s, b_s = ki[s : s + 1], vi[s : s + 1], bi[s : s + 1]
            dk_s = jnp.zeros((SUBLANES, k.shape[1]), F32)
            dv_s = jnp.zeros((SUBLANES, v.shape[1]), F32)
            for tg in range(gs, ng):
                if tg == gs:
                    mask = ridx >= so
                    e = jnp.where(mask, jnp.exp(jnp.where(mask, bg[tg] - b_s, 0.0)), 0.0)
                else:
                    e = jnp.exp(bg[tg] - b_s)
                da = jnp.sum(dog[tg] * v_s, axis=1, keepdims=True)
                qe = qg[tg] * e
                a = jnp.sum(qe * k_s, axis=1, keepdims=True)
                dqg[tg] = dqg[tg] + da * (k_s * e)
                dk_s = dk_s + da * qe
                dv_s = dv_s + a * dog[tg]
            dkg[gs] = jnp.where(ridx == so, dkg[gs] + jnp.sum(dk_s, axis=0, keepdims=True), dkg[gs])
            dvg[gs] = jnp.where(ridx == so, dvg[gs] + jnp.sum(dv_s, axis=0, keepdims=True), dvg[gs])
        dqi = jnp.concatenate(dqg, axis=0)
        dki = jnp.concatenate(dkg, axis=0)
        dvi = jnp.concatenate(dvg, axis=0)
        if i > 0:
            ref = bi[0:1]
            eq = jnp.exp(bi - ref)
            ek = jnp.exp(ref - b[:r0])
            qt = qi * eq
            kt = k[:r0] * ek
            A = _mdot_nt(qt, kt)
            dA = _bdot_nt(doi, v[:r0])
            dvl = _bdot_tn(A, doi)
            dqi = dqi + _mdot(dA, kt) * eq
            dkl = _mdot_tn(dA, qt) * ek
            for j in range(i):
                dk_p[j] = dk_p[j] + dkl[j * SB : (j + 1) * SB]
                dv_p[j] = dv_p[j] + dvl[j * SB : (j + 1) * SB]
        dq_p[i] = dqi
        dk_p[i] = dk_p[i] + dki
        dv_p[i] = dv_p[i] + dvi
    return jnp.concatenate(dq_p, axis=0), jnp.concatenate(dk_p, axis=0), jnp.concatenate(dv_p, axis=0)


def _hgrn_fwd(proj, lb, gn, name):
    B, S, W = proj.shape
    HK = W // 4
    H = HK // HGRN_K
    C = min(HGRN_CHUNK, S)
    N = S // C

    HP = HGRN_PAR if H % HGRN_PAR == 0 else 1
    WP = HP * HGRN_K

    def body(q_ref, f_ref, i_ref, g_ref, lb_ref, gn_ref, og_ref, o_ref, st_ref):
        gn_v = gn_ref[...]
        tril = _tri(C, True)

        def chunk(n, sts):
            rows = pl.ds(pl.multiple_of(n * C, C), C)
            out = []
            for hh in range(HP):
                ln = slice(hh * HGRN_K, (hh + 1) * HGRN_K)
                st = sts[hh]
                qs, k, g = _hgrn_pre(q_ref[rows, ln], f_ref[rows, ln], lb_ref[:, ln])
                v = i_ref[rows, ln]
                b = _hdot(tril, g)
                st_ref[hh, n] = st
                o = _hgrn_intra_fwd(qs, k, v, b) + _bdot_nt(qs * jnp.exp(b), st)
                bl = b[C - 1 : C]
                out.append(st * jnp.exp(bl) + _bdot_tn(v, k * jnp.exp(bl - b)))
                o_ref[rows, ln] = o
                og_ref[rows, ln] = _hgrn_gate(o, g_ref[rows, ln], gn_v).astype(BF16)
            return tuple(out)

        lax.fori_loop(0, N, chunk, tuple(jnp.zeros((HGRN_K, HGRN_K), F32) for _ in range(HP)))

    col = lambda part: pl.BlockSpec((None, S, WP), lambda b, h: (b, 0, part * (H // HP) + h))
    return pl.pallas_call(
        body,
        grid=(B, H // HP),
        in_specs=[col(0), col(1), col(2), col(3), pl.BlockSpec((1, WP), lambda b, h: (0, h)), pl.BlockSpec((1, HGRN_K), lambda b, h: (0, 0))],
        out_specs=[col(0), col(0), pl.BlockSpec((None, HP, N, HGRN_K, HGRN_K), lambda b, h: (b, h, 0, 0, 0))],
        out_shape=[SDS((B, S, HK), BF16), SDS((B, S, HK), F32), SDS((B, H, N, HGRN_K, HGRN_K), F32)],
        name=name,
        compiler_params=_cparams("parallel", "parallel"),
    )(proj, proj, proj, proj, lb, gn)


def _hgrn_bwd(proj, lb, gn, o_pre, states, dog, name):
    B, S, W = proj.shape
    HK = W // 4
    H = HK // HGRN_K
    C = min(HGRN_CHUNK, S)
    N = S // C

    HP = HGRN_PAR if H % HGRN_PAR == 0 else 1
    WP = HP * HGRN_K

    def body(q_ref, f_ref, i_ref, g_ref, lb_ref, gn_ref, o_ref, st_ref, dog_ref, dq_ref, df_ref, di_ref, dg_ref, dlb_ref, dgn_ref):
        gn_v = gn_ref[...]
        tril = _tri(C, True)
        triu = _tri(C, False)

        def chunk(idx, carry):
            n = N - 1 - idx
            rows = pl.ds(pl.multiple_of(n * C, C), C)
            out = []
            for hh in range(HP):
                ln = slice(hh * HGRN_K, (hh + 1) * HGRN_K)
                dst, dlb, dgn = carry[hh]
                (qs, k, g), pre_vjp = jax.vjp(_hgrn_pre, q_ref[rows, ln], f_ref[rows, ln], lb_ref[:, ln])
                v = i_ref[rows, ln]
                _, gate_vjp = jax.vjp(_hgrn_gate, o_ref[rows, ln], g_ref[rows, ln], gn_v)
                do, dgg, dgn_c = gate_vjp(dog_ref[rows, ln])
                b = _hdot(tril, g)
                st0 = st_ref[hh, n]
                eb = jnp.exp(b)
                bl = b[C - 1 : C]
                ebl = jnp.exp(bl)
                ekb = jnp.exp(bl - b)
                qe = qs * eb
                kt = k * ekb
                dqs, dk, dv = _hgrn_intra_bwd(qs, k, v, b, do)
                dqs = dqs + _bdot(do, st0) * eb
                dk = dk + _bdot(v, dst) * ekb
                dv = dv + _bdot_nt(kt, dst)
                st1 = st0 * ebl + _bdot_tn(v, kt)
                dbl = jnp.sum(st1 * dst, axis=0, keepdims=True)
                dst = dst * ebl + _bdot_tn(do, qe)
                dgl = _hdot(triu, qs * dqs - k * dk) + dbl
                dq_pre, dfx, dlb_c = pre_vjp((dqs, dk, dgl))
                dq_ref[rows, ln] = dq_pre.astype(BF16)
                df_ref[rows, ln] = dfx.astype(BF16)
                di_ref[rows, ln] = dv.astype(BF16)
                dg_ref[rows, ln] = dgg.astype(BF16)
                out.append((dst, dlb + dlb_c, dgn + dgn_c))
            return tuple(out)

        zero = jnp.zeros((1, HGRN_K), F32)
        one = (jnp.zeros((HGRN_K, HGRN_K), F32), zero, zero)
        res = lax.fori_loop(0, N, chunk, tuple(one for _ in range(HP)))
        for hh in range(HP):
            dlb_ref[hh] = res[hh][1]
            dgn_ref[hh] = res[hh][2]

    col = lambda part: pl.BlockSpec((None, S, WP), lambda b, h: (b, 0, part * (H // HP) + h))
    vec = pl.BlockSpec((None, HP, 1, HGRN_K), lambda b, h: (b, h, 0, 0))
    return pl.pallas_call(
        body,
        grid=(B, H // HP),
        in_specs=[
            col(0), col(1), col(2), col(3),
            pl.BlockSpec((1, WP), lambda b, h: (0, h)),
            pl.BlockSpec((1, HGRN_K), lambda b, h: (0, 0)),
            col(0),
            pl.BlockSpec((None, HP, N, HGRN_K, HGRN_K), lambda b, h: (b, h, 0, 0, 0)),
            col(0),
        ],
        out_specs=[col(0), col(0), col(0), col(0), vec, vec],
        out_shape=[SDS((B, S, HK), BF16)] * 4 + [SDS((B, H, 1, HGRN_K), F32)] * 2,
        name=name,
        compiler_params=_cparams("parallel", "parallel"),
    )(proj, proj, proj, proj, lb, gn, o_pre, states, dog)


def _ada_fwd(c_all, w, b, name):
    Bg, D = c_all.shape
    L, _, n = w.shape

    def body(c_ref, w_ref, b_ref, o_ref):
        o_ref[...] = _bdot(jax.nn.silu(c_ref[...]), w_ref[...]) + b_ref[...]

    return pl.pallas_call(
        body,
        grid=(L,),
        in_specs=[
            pl.BlockSpec((Bg, D), lambda l: (0, 0)),
            pl.BlockSpec((None, D, n), lambda l: (l, 0, 0)),
            pl.BlockSpec((None, 1, n), lambda l: (l, 0, 0)),
        ],
        out_specs=pl.BlockSpec((None, Bg, n), lambda l: (l, 0, 0)),
        out_shape=SDS((L, Bg, n), F32),
        name=name,
        compiler_params=_cparams("parallel"),
    )(c_all, w, b)


def _ada_bwd(c_all, dmod, name):
    Bg, D = c_all.shape
    L, _, n = dmod.shape

    def body(c_ref, d_ref, dw_ref, db_ref):
        d = d_ref[...]
        dw_ref[...] = _bdot_tn(jax.nn.silu(c_ref[...]), d)
        db_ref[...] = jnp.sum(d, axis=0, keepdims=True)

    return pl.pallas_call(
        body,
        grid=(L,),
        in_specs=[pl.BlockSpec((Bg, D), lambda l: (0, 0)), pl.BlockSpec((None, Bg, n), lambda l: (l, 0, 0))],
        out_specs=[pl.BlockSpec((None, D, n), lambda l: (l, 0, 0)), pl.BlockSpec((None, 1, n), lambda l: (l, 0, 0))],
        out_shape=[SDS((L, D, n), F32), SDS((L, 1, n), F32)],
        name=name,
        compiler_params=_cparams("parallel"),
    )(c_all, dmod)


def _adamw(w, gs, m, v, name):
    shape = w.shape
    cols = shape[-1]
    rows = w.size // cols
    tr = rows
    for cand in (512, 256, 128, 64, 32, 16, 8):
        if rows % cand == 0 and cand * cols * 4 <= 2 * 1024 * 1024:
            tr = cand
            break
    as2d = lambda a: a.reshape(rows, cols)
    ng = len(gs)
    c1 = 1.0 / (1.0 - ADAM_B1**ADAM_STEP)
    c2 = 1.0 / (1.0 - ADAM_B2**ADAM_STEP)

    def body(*refs):
        w_ref, m_ref, v_ref = refs[0], refs[1], refs[2]
        g_refs = refs[3 : 3 + ng]
        g_out, d_out, m_out, v_out = refs[3 + ng :]
        g = g_refs[0][...].astype(F32)
        for r in g_refs[1:]:
            g = g + r[...].astype(F32)
        m_new = ADAM_B1 * m_ref[...] + (1.0 - ADAM_B1) * g
        v_new = ADAM_B2 * v_ref[...] + (1.0 - ADAM_B2) * jnp.square(g)
        g_out[...] = g
        m_out[...] = m_new
        v_out[...] = v_new
        d_out[...] = -ADAM_LR * ((m_new * c1) / (jnp.sqrt(v_new * c2) + ADAM_EPS) + ADAM_WD * w_ref[...])

    spec = pl.BlockSpec((tr, cols), lambda i: (i, 0))
    outs = pl.pallas_call(
        body,
        grid=(rows // tr,),
        in_specs=[spec] * (3 + ng),
        out_specs=[spec] * 4,
        out_shape=[SDS((rows, cols), F32)] * 4,
        name=name,
        compiler_params=_cparams("parallel"),
    )(as2d(w), as2d(m), as2d(v), *[as2d(g) for g in gs])
    return tuple(o.reshape(shape) for o in outs)


def _sum4(own, recv, name):
    shape = own.shape
    cols = shape[-1]
    rows = own.size // cols
    tr = rows
    for cand in (512, 256, 128, 64, 32, 16):
        if rows % cand == 0 and cand * cols * 4 <= 2 * 1024 * 1024:
            tr = cand
            break

    def body(own_ref, recv_ref, o_ref):
        acc = own_ref[...].astype(F32)
        for r in range(3):
            acc = acc + recv_ref[r].astype(F32)
        o_ref[...] = acc

    out = pl.pallas_call(
        body,
        grid=(rows // tr,),
        in_specs=[pl.BlockSpec((tr, cols), lambda i: (i, 0)), pl.BlockSpec((3, tr, cols), lambda i: (0, i, 0))],
        out_specs=pl.BlockSpec((tr, cols), lambda i: (i, 0)),
        out_shape=SDS((rows, cols), F32),
        name=name,
        compiler_params=_cparams("parallel"),
    )(own.reshape(rows, cols), recv.reshape(3, rows, cols))
    return out.reshape(shape)


def _my_place():
    return lax.axis_index("x"), lax.axis_index("y"), lax.axis_index("c")


def _flip(v, bit):
    return 1 - v if bit else v


def _allgather8(x, name):
    r, n = x.shape

    def body(x_ref, o_ref, send_sems, recv_sems, local_sem):
        mx, my, mc = _my_place()
        me = 4 * mx + 2 * my + mc
        mine = pltpu.make_async_copy(x_ref, o_ref.at[me], local_sem)
        mine.start()
        sends = []
        for rel in range(1, 8):
            peer = (_flip(mx, rel & 4), _flip(my, rel & 2), _flip(mc, rel & 1))
            cp = pltpu.make_async_remote_copy(
                src_ref=x_ref, dst_ref=o_ref.at[me], send_sem=send_sems.at[rel - 1], recv_sem=recv_sems.at[rel - 1],
                device_id=peer, device_id_type=MESH,
            )
            cp.start()
            sends.append(cp)
        for rel in range(1, 8):
            px, py, pc = _flip(mx, rel & 4), _flip(my, rel & 2), _flip(mc, rel & 1)
            pltpu.make_async_remote_copy(
                src_ref=x_ref, dst_ref=o_ref.at[4 * px + 2 * py + pc], send_sem=send_sems.at[rel - 1],
                recv_sem=recv_sems.at[rel - 1], device_id=(px, py, pc), device_id_type=MESH,
            ).wait_recv()
        for cp in sends:
            cp.wait_send()
        mine.wait()

    return pl.pallas_call(
        body,
        out_shape=SDS((8, r, n), x.dtype),
        in_specs=[pl.BlockSpec(memory_space=pl.ANY)],
        out_specs=pl.BlockSpec(memory_space=pl.ANY),
        scratch_shapes=[pltpu.SemaphoreType.DMA((7,)), pltpu.SemaphoreType.DMA((7,)), pltpu.SemaphoreType.DMA],
        name=name,
    )(x)


_HBM = pl.BlockSpec(memory_space=pl.ANY)


_SEM = pl.BlockSpec(memory_space=pltpu.SEMAPHORE)
_HBM_ONLY = pl.BlockSpec(memory_space=pltpu.HBM)
_EFFECT = pltpu.SideEffectType.DATAFLOW_SIDE_EFFECTING


def _in_hbm(a):
    return pltpu.with_memory_space_constraint(a, pltpu.HBM)


def _gather_start(lands, after, name):
    n = len(lands)

    def body(*refs):
        land = refs[:n]
        send_sems, recv_sems = refs[n + 1], refs[n + 2]
        token = refs[-1]
        mx, my, mc = _my_place()
        for i in range(n):
            for rel in range(1, 4):
                pltpu.make_async_remote_copy(
                    src_ref=land[i].at[2 * mx + my], dst_ref=land[i].at[2 * mx + my],
                    send_sem=send_sems.at[3 * i + rel - 1], recv_sem=recv_sems.at[3 * i + rel - 1],
                    device_id=(_flip(mx, rel & 2), _flip(my, rel & 1), mc), device_id_type=MESH,
                ).start()
        token[...] = jnp.zeros_like(token)

    outs = pl.pallas_call(
        body,
        name=name,
        out_shape=(
            pltpu.SemaphoreType.DMA((3 * n,)), pltpu.SemaphoreType.DMA((3 * n,)),
            *[pltpu.HBM(a.shape, a.dtype) for a in lands], SDS((8, LANES), F32),
        ),
        in_specs=[_HBM_ONLY] * n + [_HBM],
        out_specs=(_SEM, _SEM, *[_HBM_ONLY] * n, pl.BlockSpec(memory_space=pltpu.VMEM)),
        input_output_aliases={i: 2 + i for i in range(n)},
        compiler_params=pltpu.CompilerParams(has_side_effects=_EFFECT),
    )(*[_in_hbm(a) for a in lands], after)
    return outs[0], outs[1], list(outs[2 : 2 + n]), outs[-1]


def _gather_wait(send_sems, recv_sems, lands, after, name):
    n = len(lands)

    def body(*refs):
        land = refs[:n]
        s_sems, r_sems = refs[n], refs[n + 1]
        mx, my, mc = _my_place()
        for i in range(n):
            for rel in range(1, 4):
                px, py = _flip(mx, rel & 2), _flip(my, rel & 1)
                cp = pltpu.make_async_remote_copy(
                    src_ref=land[i].at[2 * mx + my], dst_ref=land[i].at[2 * px + py],
                    send_sem=s_sems.at[3 * i + rel - 1], recv_sem=r_sems.at[3 * i + rel - 1],
                    device_id=(px, py, mc), device_id_type=MESH,
                )
                cp.wait_send()
                cp.wait_recv()

    outs = pl.pallas_call(
        body,
        name=name,
        out_shape=tuple(pltpu.HBM(a.shape, a.dtype) for a in lands),
        in_specs=[_HBM_ONLY] * n + [_SEM, _SEM, _HBM],
        out_specs=[_HBM_ONLY] * n,
        input_output_aliases={i: i for i in range(n)},
        compiler_params=pltpu.CompilerParams(has_side_effects=_EFFECT),
    )(*lands, send_sems, recv_sems, after)
    return list(outs)


def _scatter_start(slabs, lands, places, name):
    n = len(slabs)

    def body(*refs):
        ins, land = refs[:n], refs[n : 2 * n]
        send_sems, recv_sems = refs[2 * n], refs[2 * n + 1]
        token = refs[-1]
        mx, my, mc = _my_place()
        for i in range(n):
            for rel in range(1, 4):
                px, py = _flip(mx, rel & 2), _flip(my, rel & 1)
                pltpu.make_async_remote_copy(
                    src_ref=ins[i].at[2 * px + py], dst_ref=land[i].at[rel - 1, places[i]],
                    send_sem=send_sems.at[3 * i + rel - 1], recv_sem=recv_sems.at[3 * i + rel - 1],
                    device_id=(px, py, mc), device_id_type=MESH,
                ).start()
        token[...] = jnp.zeros_like(token)

    outs = pl.pallas_call(
        body,
        name=name,
        out_shape=(
            pltpu.SemaphoreType.DMA((3 * n,)), pltpu.SemaphoreType.DMA((3 * n,)),
            *[pltpu.HBM(a.shape, a.dtype) for a in slabs], *[pltpu.HBM(a.shape, a.dtype) for a in lands],
            SDS((8, LANES), F32),
        ),
        in_specs=[_HBM_ONLY] * (2 * n),
        out_specs=(_SEM, _SEM, *[_HBM_ONLY] * (2 * n), pl.BlockSpec(memory_space=pltpu.VMEM)),
        input_output_aliases={i: 2 + i for i in range(2 * n)},
        compiler_params=pltpu.CompilerParams(has_side_effects=_EFFECT),
    )(*[_in_hbm(a) for a in slabs], *[_in_hbm(a) for a in lands])
    return outs[0], outs[1], list(outs[2 : 2 + n]), list(outs[2 + n : 2 + 2 * n]), outs[-1]


def _scatter_wait(send_sems, recv_sems, slabs, lands, places, after, name):
    n = len(slabs)

    def body(*refs):
        ins, land = refs[:n], refs[n : 2 * n]
        s_sems, r_sems = refs[2 * n], refs[2 * n + 1]
        mx, my, mc = _my_place()
        for i in range(n):
            for rel in range(1, 4):
                px, py = _flip(mx, rel & 2), _flip(my, rel & 1)
                cp = pltpu.make_async_remote_copy(
                    src_ref=ins[i].at[2 * px + py], dst_ref=land[i].at[rel - 1, places[i]],
                    send_sem=s_sems.at[3 * i + rel - 1], recv_sem=r_sems.at[3 * i + rel - 1],
                    device_id=(px, py, mc), device_id_type=MESH,
                )
                cp.wait_send()
                cp.wait_recv()

    outs = pl.pallas_call(
        body,
        name=name,
        out_shape=(*[pltpu.HBM(a.shape, a.dtype) for a in slabs], *[pltpu.HBM(a.shape, a.dtype) for a in lands]),
        in_specs=[_HBM_ONLY] * (2 * n) + [_SEM, _SEM, _HBM],
        out_specs=[_HBM_ONLY] * (2 * n),
        input_output_aliases={i: i for i in range(2 * n)},
        compiler_params=pltpu.CompilerParams(has_side_effects=_EFFECT),
    )(*slabs, *lands, send_sems, recv_sems, after)
    return list(outs[:n]), list(outs[n:])


def _swap_start(parts, after, name):
    n = len(parts)
    lands = [lax.empty(a.shape, a.dtype) for a in parts]

    def body(*refs):
        ins, land = refs[:n], refs[n : 2 * n]
        send_sems, recv_sems = refs[2 * n + 1], refs[2 * n + 2]
        mx, my, mc = _my_place()
        for i in range(n):
            pltpu.make_async_remote_copy(
                src_ref=ins[i], dst_ref=land[i], send_sem=send_sems.at[i], recv_sem=recv_sems.at[i],
                device_id=(mx, my, 1 - mc), device_id_type=MESH,
            ).start()

    outs = pl.pallas_call(
        body,
        name=name,
        out_shape=(
            pltpu.SemaphoreType.DMA((n,)), pltpu.SemaphoreType.DMA((n,)),
            *[pltpu.HBM(a.shape, a.dtype) for a in parts], *[pltpu.HBM(a.shape, a.dtype) for a in lands],
        ),
        in_specs=[_HBM_ONLY] * (2 * n) + [_HBM],
        out_specs=(_SEM, _SEM, *[_HBM_ONLY] * (2 * n)),
        input_output_aliases={i: 2 + i for i in range(2 * n)},
        compiler_params=pltpu.CompilerParams(has_side_effects=_EFFECT),
    )(*[_in_hbm(a) for a in parts], *[_in_hbm(a) for a in lands], after)
    return outs[0], outs[1], list(outs[2 : 2 + n]), list(outs[2 + n :])


def _swap_wait(send_sems, recv_sems, parts, lands, after, name):
    n = len(parts)

    def body(*refs):
        ins, land = refs[:n], refs[n : 2 * n]
        s_sems, r_sems = refs[2 * n], refs[2 * n + 1]
        mx, my, mc = _my_place()
        for i in range(n):
            cp = pltpu.make_async_remote_copy(
                src_ref=ins[i], dst_ref=land[i], send_sem=s_sems.at[i], recv_sem=r_sems.at[i],
                device_id=(mx, my, 1 - mc), device_id_type=MESH,
            )
            cp.wait_send()
            cp.wait_recv()

    outs = pl.pallas_call(
        body,
        name=name,
        out_shape=(*[pltpu.HBM(a.shape, a.dtype) for a in parts], *[pltpu.HBM(a.shape, a.dtype) for a in lands]),
        in_specs=[_HBM_ONLY] * (2 * n) + [_SEM, _SEM, _HBM],
        out_specs=[_HBM_ONLY] * (2 * n),
        input_output_aliases={i: i for i in range(2 * n)},
        compiler_params=pltpu.CompilerParams(has_side_effects=_EFFECT),
    )(*parts, *lands, send_sems, recv_sems, after)
    return list(outs[:n]), list(outs[n:])


def _pad_rows(a, rows):
    return jnp.pad(a, ((0, rows - a.shape[0]), (0, 0)))


def kernel(x, c, positions, mla_w_in, mla_q_norm, mla_w_qb, mla_kv_norm, mla_w_kvb, mla_w_o, hgrn_lb, hgrn_w_in, hgrn_g_norm, hgrn_w_o, ffn_w_in, ffn_w_out, ada_w, ada_b, ln_g, ln_b, loss_target, m_mla_w_in, m_mla_q_norm, m_mla_w_qb, m_mla_kv_norm, m_mla_w_kvb, m_mla_w_o, m_hgrn_lb, m_hgrn_w_in, m_hgrn_g_norm, m_hgrn_w_o, m_ffn_w_in, m_ffn_w_out, m_ada_w, m_ada_b, m_ln_g, m_ln_b, v_mla_w_in, v_mla_q_norm, v_mla_w_qb, v_mla_kv_norm, v_mla_w_kvb, v_mla_w_o, v_hgrn_lb, v_hgrn_w_in, v_hgrn_g_norm, v_hgrn_w_o, v_ffn_w_in, v_ffn_w_out, v_ada_w, v_ada_b, v_ln_g, v_ln_b):
    B, S, D = x.shape
    T = B * S
    depth = ada_w.shape[0]
    n_mla, n_hgrn = mla_w_in.shape[0], hgrn_w_in.shape[0]
    n_sub = 2 * depth
    alpha = (2.0 * depth) ** 0.25
    mx, my, mc = _my_place()
    me = 4 * mx + 2 * my + mc
    k_me = 2 * mx + my
    Bg = 8 * B
    HK = hgrn_w_o.shape[1] * 4
    dq = D // 4

    lbw = hgrn_lb.shape[1]
    first = jnp.zeros((8, max(D, 4 * lbw)), F32)
    first = first.at[:B, :D].set(c).at[B : B + n_hgrn, :lbw].set(hgrn_lb)
    first_all = _allgather8(first, "gather_cond")
    c_all = first_all[:, :B, :D].reshape(Bg, D)
    lb_logits = jnp.concatenate([first_all[2 * k, B : B + n_hgrn, :lbw] for k in range(4)], axis=1)

    def lower_bounds_fn(logits):
        soft = jax.nn.softmax(logits, axis=0)
        return jnp.cumsum(soft, axis=0) - soft[0]

    lower_bounds, lower_bounds_vjp = jax.vjp(lower_bounds_fn, lb_logits)

    n_ada = ada_w.shape[-1]
    mod_part = _ada_fwd(c_all, ada_w.reshape(n_sub, D, n_ada), ada_b.reshape(n_sub, 1, n_ada), "ada_fwd")
    mod_all = _allgather8(mod_part.reshape(n_sub * Bg, n_ada), "gather_mod").reshape(8, n_sub, Bg, n_ada)
    mod = jnp.concatenate([mod_all[2 * k] for k in range(4)], axis=-1)
    mod = lax.dynamic_slice_in_dim(mod, me * B, B, axis=1)
    shift = [mod[j, :, None, :D] for j in range(n_sub)]
    scale = [mod[j, :, None, D : 2 * D] for j in range(n_sub)]
    gate = [mod[j, :, None, 2 * D :] for j in range(n_sub)]

    ln_rows = 2 * n_sub
    ln_local = _pad_rows(jnp.concatenate([ln_g.reshape(n_sub, dq), ln_b.reshape(n_sub, dq)], axis=0), -(-ln_rows // 8) * 8)
    ln_pad = jnp.zeros((ln_local.shape[0], -(-dq // LANES) * LANES), F32).at[:, :dq].set(ln_local)
    ln_all = _allgather8(ln_pad, "gather_ln")
    ln_full = jnp.concatenate([ln_all[2 * k, :ln_rows, :dq] for k in range(4)], axis=1)
    lng = [ln_full[j][None, :] for j in range(n_sub)]
    lnb = [ln_full[n_sub + j][None, :] for j in range(n_sub)]

    main = dict(mla_w_in=mla_w_in, mla_w_qb=mla_w_qb, mla_w_kvb=mla_w_kvb, mla_w_o=mla_w_o, hgrn_w_in=hgrn_w_in,
                hgrn_w_o=hgrn_w_o, ffn_w_in=ffn_w_in, ffn_w_out=ffn_w_out)
    names = list(main)

    def group_kinds(layer, part):
        if part:
            return [("ffn_w_in", layer), ("ffn_w_out", layer)]
        mixer = ["mla_w_in", "mla_w_qb", "mla_w_kvb", "mla_w_o"] if layer % 2 == 0 else ["hgrn_w_in", "hgrn_w_o"]
        return [(k, layer // 2) for k in mixer]

    gathers = {}
    after = mod_all[0, 0, :8, :LANES] + ln_all[0, :8, :LANES]
    for layer in range(depth):
        for part in range(2):
            lands = [lax.dynamic_update_index_in_dim(lax.empty((4,) + main[k].shape[1:], BF16), main[k][i].astype(BF16), k_me, 0)
                     for k, i in group_kinds(layer, part)]
            ssem, rsem, lands, after = _gather_start(lands, after, f"gather_start_l{layer}p{part}")
            gathers[layer, part] = (ssem, rsem, lands)
    scale[0] = scale[0] + after[0, 0]

    def row_w(g):
        return g.reshape(1, g.shape[0] * g.shape[1], g.shape[2])

    def full_w_in(g):
        return jnp.transpose(g, (1, 0, 2)).reshape(1, g.shape[1], 4 * g.shape[2])

    ang = positions.astype(F32)[..., None] * (ROPE_THETA ** (-jnp.arange(0, QK_ROPE, 2, dtype=F32) / QK_ROPE))
    cos, sin = jnp.cos(ang), jnp.sin(ang)

    gq = [mla_q_norm[j][None, :] for j in range(n_mla)]
    gkv = [mla_kv_norm[j][None, :] for j in range(n_mla)]
    gn = [hgrn_g_norm[j][None, :] for j in range(n_hgrn)]

    def r2(a):
        return a.reshape(T, a.shape[-1])

    def r3(a):
        return a.reshape(B, S, a.shape[-1])

    saved = []
    xs = x
    for layer in range(depth):
        j = layer // 2
        sub = 2 * layer
        tag = f"l{layer}"
        ssem, rsem, lands = gathers[layer, 0]
        lands = _gather_wait(ssem, rsem, lands, xs if layer else scale[0], f"gather_wait_{tag}p0")
        wl = {k: g for (k, _), g in zip(group_kinds(layer, 0), lands)}
        if layer == 0:
            h = _modulate(xs, scale[sub], shift[sub], f"mod_{tag}a")
        if layer % 2 == 0:
            wl["mla_w_in"] = full_w_in(wl["mla_w_in"])
            proj = r3(_mm_nn(r2(h), wl["mla_w_in"], F32, f"mla_in_{tag}"))
            qn, kvn = _mla_mid_fwd(proj, gq[j], gkv[j], f"mla_mid_{tag}")
            q = r3(_mm_nn(r2(qn), wl["mla_w_qb"], F32, f"mla_qb_{tag}"))
            kv = r3(_mm_nn(r2(kvn), wl["mla_w_kvb"], F32, f"mla_kvb_{tag}"))
            qh, kh, vh = _mla_prep_fwd(q, kv, proj, cos, sin, f"mla_prep_{tag}")
            o, lse = _attn_fwd(qh, kh, vh, f"attn_{tag}")
            wl["mla_w_o"] = row_w(wl["mla_w_o"])
            y = r3(_mm_nn(r2(o), wl["mla_w_o"], F32, f"mla_o_{tag}"))
            mix = (h, proj, qn, kvn, qh, kh, vh, o, lse)
        else:
            proj = r3(_mm_nn(r2(h), wl["hgrn_w_in"], F32, f"hgrn_in_{tag}"))
            og, o_pre, states = _hgrn_fwd(proj, lower_bounds[j][None, :], gn[j], f"hgrn_{tag}")
            wl["hgrn_w_o"] = row_w(wl["hgrn_w_o"])
            y = r3(_mm_nn(r2(og), wl["hgrn_w_o"], F32, f"hgrn_o_{tag}"))
            mix = (h, proj, og, o_pre, states)
        x1, h2 = _ln_mod_fwd(alpha, xs, y, gate[sub], lng[sub], lnb[sub], scale[sub + 1], shift[sub + 1], f"ln_{tag}a")
        ssem, rsem, lands = gathers[layer, 1]
        lands = _gather_wait(ssem, rsem, lands, x1, f"gather_wait_{tag}p1")
        wl.update({k: g for (k, _), g in zip(group_kinds(layer, 1), lands)})
        a, ug, uu = [r3(t_) for t_ in _ffn_in(r2(h2), wl["ffn_w_in"], f"ffn_in_{tag}")]
        wl["ffn_w_out"] = row_w(wl["ffn_w_out"])
        y2 = r3(_mm_nn(r2(a), wl["ffn_w_out"], F32, f"ffn_out_{tag}"))
        if layer + 1 < depth:
            x2, h_next = _ln_mod_fwd(alpha, x1, y2, gate[sub + 1], lng[sub + 1], lnb[sub + 1], scale[sub + 2], shift[sub + 2], f"ln_{tag}b")
        else:
            x2, h_next = _ln_fwd(alpha, x1, y2, gate[sub + 1], lng[sub + 1], lnb[sub + 1], f"ln_{tag}b"), None
        saved.append((xs, y, x1, y2, mix, h2, ug, uu, a, wl))
        xs, h = x2, h_next

    loss_local, dout = _loss_head(xs, loss_target, "loss_head")
    loss = lax.psum(loss_local, ("x", "y", "c"))

    gw = {k: [None] * main[k].shape[0] for k in names}
    land = {k: lax.empty((3,) + main[k].shape, BF16) for k in names}
    scatters = []
    d_shift, d_scale, d_gate = [None] * n_sub, [None] * n_sub, [None] * n_sub
    d_lng, d_lnb = [None] * n_sub, [None] * n_sub
    d_gq, d_gkv, d_gn, d_lbnd = [None] * n_mla, [None] * n_mla, [None] * n_hgrn, [None] * n_hgrn

    def rows4(g):
        return g.reshape(4, g.shape[1] // 4, g.shape[2])

    def scatter_kinds(layer, part):
        if part == 1 or layer % 2:
            return group_kinds(layer, part)
        mixer = group_kinds(layer, 0)
        return mixer[:1] if part == 0 else mixer[1:]

    def start_scatter(layer, part, params, at):
        kinds = scatter_kinds(layer, part)
        ssem, rsem, slabs_t, lands_t, token = _scatter_start(
            [gw[k][i] for k, i in kinds], [land[k] for k, _ in kinds], [i for _, i in kinds], f"scatter_start_l{layer}p{part}")
        for (k, i), s_t, l_t in zip(kinds, slabs_t, lands_t):
            gw[k][i], land[k] = s_t, l_t
        scatters.append((layer, part, ssem, rsem))
        if params is not None:
            params[at] = params[at] + token[0, 0]

    for layer in reversed(range(depth)):
        j = layer // 2
        sub = 2 * layer
        tag = f"l{layer}"
        xs, y, x1, y2, mix, h2, ug, uu, a, wl = saved[layer]
        if layer + 1 == depth:
            dxr, dy2, d_gate[sub + 1], d_lng[sub + 1], d_lnb[sub + 1] = _ln_bwd(
                alpha, dout, x1, y2, gate[sub + 1], lng[sub + 1], lnb[sub + 1], f"ln_bwd_{tag}b")
        else:
            dxr, dy2, d_gate[sub + 1], d_lng[sub + 1], d_lnb[sub + 1], d_scale[sub + 2], d_shift[sub + 2] = _ln_mod_bwd(
                alpha, dh, dxr, scale[sub + 2], x1, y2, gate[sub + 1], lng[sub + 1], lnb[sub + 1], f"ln_bwd_{tag}b")
        da = r3(_mm_nt(r2(dy2), wl["ffn_w_out"], F32, f"ffn_out_dx_{tag}"))
        gw["ffn_w_out"][layer] = rows4(_mm_tn(r2(a), r2(dy2), 1, BF16, f"ffn_out_dw_{tag}"))
        du = _swiglu_bwd(ug, uu, da, f"swiglu_bwd_{tag}")
        dh2 = r3(_mm_nt(r2(du), wl["ffn_w_in"], F32, f"ffn_in_dx_{tag}"))
        gw["ffn_w_in"][layer] = _mm_tn(r2(h2), r2(du), 4, BF16, f"ffn_in_dw_{tag}")
        start_scatter(layer, 1, gate, sub)
        dxr, dy, d_gate[sub], d_lng[sub], d_lnb[sub], d_scale[sub + 1], d_shift[sub + 1] = _ln_mod_bwd(
            alpha, dh2, dxr, scale[sub + 1], xs, y, gate[sub], lng[sub], lnb[sub], f"ln_bwd_{tag}a")
        if layer % 2 == 0:
            h, proj, qn, kvn, qh, kh, vh, o, lse = mix
            do = r3(_mm_nt(r2(dy), wl["mla_w_o"], BF16, f"mla_o_dx_{tag}"))
            gw["mla_w_o"][j] = rows4(_mm_tn(r2(o), r2(dy), 1, BF16, f"mla_o_dw_{tag}"))
            dqh, dkh, dvh = _attn_bwd(qh, kh, vh, o, do, lse, f"attn_bwd_{tag}")
            dq_, dkv_, dkr = _mla_prep_bwd(dqh, dkh, dvh, cos, sin, f"mla_prep_bwd_{tag}")
            dqn = r3(_mm_nt(r2(dq_), wl["mla_w_qb"], F32, f"mla_qb_dx_{tag}"))
            gw["mla_w_qb"][j] = _mm_tn(r2(qn), r2(dq_), 4, BF16, f"mla_qb_dw_{tag}")
            dkvn = r3(_mm_nt(r2(dkv_), wl["mla_w_kvb"], F32, f"mla_kvb_dx_{tag}"))
            gw["mla_w_kvb"][j] = _mm_tn(r2(kvn), r2(dkv_), 4, BF16, f"mla_kvb_dw_{tag}")
            start_scatter(layer, 2, gq, j)
            dproj, dgq_, dgkv_ = _mla_mid_bwd(proj, dqn, dkvn, dkr, gq[j], gkv[j], f"mla_mid_bwd_{tag}")
            d_gq[j], d_gkv[j] = dgq_.sum(0), dgkv_.sum(0)
            dh = r3(_mm_nt(r2(dproj), wl["mla_w_in"], F32, f"mla_in_dx_{tag}"))
            gwin = _mm_tn(r2(h), r2(dproj), 1, BF16, f"mla_in_dw_{tag}")[0]
            gw["mla_w_in"][j] = jnp.transpose(gwin.reshape(gwin.shape[0], 4, gwin.shape[1] // 4), (1, 0, 2))
        else:
            h, proj, og, o_pre, states = mix
            dog = r3(_mm_nt(r2(dy), wl["hgrn_w_o"], F32, f"hgrn_o_dx_{tag}"))
            gw["hgrn_w_o"][j] = rows4(_mm_tn(r2(og), r2(dy), 1, BF16, f"hgrn_o_dw_{tag}"))
            dq_, df_, di_, dg_, dlb_, dgn_ = _hgrn_bwd(proj, lower_bounds[j][None, :], gn[j], o_pre, states, dog, f"hgrn_bwd_{tag}")
            dproj = jnp.concatenate([dq_, df_, di_, dg_], axis=-1)
            d_lbnd[j] = dlb_.sum(0).reshape(1, HK)
            d_gn[j] = dgn_.sum((0, 1))
            dh = r3(_mm_nt(r2(dproj), wl["hgrn_w_in"], F32, f"hgrn_in_dx_{tag}"))
            gw["hgrn_w_in"][j] = _mm_tn(r2(h), r2(dproj), 4, BF16, f"hgrn_in_dw_{tag}")
        start_scatter(layer, 0, gate if layer else None, sub - 1)
    grad_x, d_scale[0], d_shift[0] = _mod_bwd(dh, dxr, x, scale[0], "mod_bwd_l0a")

    for layer, part, ssem, rsem in scatters:
        kinds = scatter_kinds(layer, part)
        slabs_t, lands_t = _scatter_wait(
            ssem, rsem, [gw[k][i] for k, i in kinds], [land[k] for k, _ in kinds], [i for _, i in kinds], grad_x,
            f"scatter_wait_l{layer}p{part}")
        for (k, i), s_t, l_t in zip(kinds, slabs_t, lands_t):
            gw[k][i], land[k] = s_t, l_t
    sums = [_sum4(jnp.stack([lax.dynamic_index_in_dim(g, k_me, 0, keepdims=False) for g in gw[k]]), land[k], f"sum4_{k}")
            for k in names]

    dmod = jnp.stack([jnp.concatenate([d_shift[s_][:, 0], d_scale[s_][:, 0], d_gate[s_][:, 0]], axis=-1) for s_ in range(n_sub)])
    dmod_rows = _pad_rows(dmod.reshape(n_sub * B, 3 * D), -(-n_sub * B // 8) * 8)
    dmod_all = _allgather8(dmod_rows, "gather_dmod")[:, : n_sub * B].reshape(8, n_sub, B, 3 * D)
    dmod_all = jnp.transpose(dmod_all, (1, 0, 2, 3)).reshape(n_sub, Bg, 3 * D)
    dmod_mine = lax.dynamic_slice_in_dim(dmod_all, k_me * n_ada, n_ada, axis=2)
    g_ada_w, g_ada_b = _ada_bwd(c_all, dmod_mine, "ada_bwd")
    g_ada_w = g_ada_w.reshape(ada_w.shape)
    g_ada_b = g_ada_b.reshape(ada_b.shape)

    small = [jnp.stack(d_gq).reshape(-1), jnp.stack(d_gkv).reshape(-1), jnp.stack(d_gn).reshape(-1),
             jnp.stack(d_lbnd).reshape(-1), jnp.stack([d.sum(0) for d in d_lng]).reshape(-1),
             jnp.stack([d.sum(0) for d in d_lnb]).reshape(-1)]
    sizes = [s_.shape[0] for s_ in small]
    flat = jnp.concatenate(small)
    rows_small = -(-flat.shape[0] // (8 * LANES)) * 8
    flat = jnp.pad(flat, (0, rows_small * LANES - flat.shape[0])).reshape(rows_small, LANES)
    tot = _allgather8(flat, "gather_small")
    acc = tot[0]
    for d in range(1, 8):
        acc = acc + tot[d]
    acc = acc.reshape(-1)
    offs = [0]
    for s_ in sizes:
        offs.append(offs[-1] + s_)
    g_q_norm = acc[offs[0] : offs[1]].reshape(mla_q_norm.shape)
    g_kv_norm = acc[offs[1] : offs[2]].reshape(mla_kv_norm.shape)
    g_g_norm = acc[offs[2] : offs[3]].reshape(hgrn_g_norm.shape)
    g_lbnd = acc[offs[3] : offs[4]].reshape(n_hgrn, HK)
    g_lb_full = lower_bounds_vjp(g_lbnd)[0]
    g_hgrn_lb = lax.dynamic_slice_in_dim(g_lb_full, k_me * lbw, lbw, axis=1)
    g_lng = lax.dynamic_slice_in_dim(acc[offs[4] : offs[5]].reshape(n_sub, D), k_me * dq, dq, axis=1).reshape(ln_g.shape)
    g_lnb = lax.dynamic_slice_in_dim(acc[offs[5] : offs[6]].reshape(n_sub, D), k_me * dq, dq, axis=1).reshape(ln_b.shape)

    weights = dict(mla_w_in=mla_w_in, mla_q_norm=mla_q_norm, mla_w_qb=mla_w_qb, mla_kv_norm=mla_kv_norm, mla_w_kvb=mla_w_kvb,
                   mla_w_o=mla_w_o, hgrn_lb=hgrn_lb, hgrn_w_in=hgrn_w_in, hgrn_g_norm=hgrn_g_norm, hgrn_w_o=hgrn_w_o,
                   ffn_w_in=ffn_w_in, ffn_w_out=ffn_w_out, ada_w=ada_w, ada_b=ada_b, ln_g=ln_g, ln_b=ln_b)
    moms = dict(mla_w_in=(m_mla_w_in, v_mla_w_in), mla_q_norm=(m_mla_q_norm, v_mla_q_norm), mla_w_qb=(m_mla_w_qb, v_mla_w_qb),
                mla_kv_norm=(m_mla_kv_norm, v_mla_kv_norm), mla_w_kvb=(m_mla_w_kvb, v_mla_w_kvb), mla_w_o=(m_mla_w_o, v_mla_w_o),
                hgrn_lb=(m_hgrn_lb, v_hgrn_lb), hgrn_w_in=(m_hgrn_w_in, v_hgrn_w_in), hgrn_g_norm=(m_hgrn_g_norm, v_hgrn_g_norm),
                hgrn_w_o=(m_hgrn_w_o, v_hgrn_w_o), ffn_w_in=(m_ffn_w_in, v_ffn_w_in), ffn_w_out=(m_ffn_w_out, v_ffn_w_out),
                ada_w=(m_ada_w, v_ada_w), ada_b=(m_ada_b, v_ada_b), ln_g=(m_ln_g, v_ln_g), ln_b=(m_ln_b, v_ln_b))
    grads = dict(mla_q_norm=(g_q_norm,), mla_kv_norm=(g_kv_norm,), hgrn_lb=(g_hgrn_lb,), hgrn_g_norm=(g_g_norm,),
                 ada_w=(g_ada_w,), ada_b=(g_ada_b,), ln_g=(g_lng,), ln_b=(g_lnb,))

    def adamw(k):
        return _adamw(weights[k], [g_.reshape(weights[k].shape) for g_ in grads[k]], moms[k][0], moms[k][1], f"adamw_{k}")

    ssem, rsem, sums, others = _swap_start(sums, tot[0, :8] + dmod_all[0, :8, :LANES], "swap_start")
    res = {k: adamw(k) for k in grads}
    sums, others = _swap_wait(ssem, rsem, sums, others, res["ada_w"][1], "swap_wait")
    grads.update({k: (a_, b_) for k, a_, b_ in zip(names, sums, others)})
    res.update({k: adamw(k) for k in names})
    order = list(weights)
    return (loss, grad_x, *[res[k][0] for k in order], *[res[k][1] for k in order], *[res[k][2] for k in order],
            *[res[k][3] for k in order])
```

```python
import functools

import jax
import jax.numpy as jnp
from jax import lax
from jax.experimental import pallas as pl
from jax.experimental.pallas import tpu as pltpu

F32 = jnp.float32
BF16 = jnp.bfloat16
SDS = jax.ShapeDtypeStruct
MESH = pl.DeviceIdType.MESH
HI = lax.Precision.HIGHEST
MID = lax.Precision.HIGH

MLA_HEADS, QK_NOPE, QK_ROPE, V_HEAD = 16, 64, 32, 64
Q_LORA, KV_LORA = 768, 256
QK_DIM = QK_NOPE + QK_ROPE
ROPE_THETA = 10000.0
HGRN_K = 128
HGRN_CHUNK = 128
HGRN_SUB = 32
HGRN_PAR = 2
LN_EPS, RMS_EPS = 1e-5, 1e-6
ADAM_LR, ADAM_B1, ADAM_B2, ADAM_EPS, ADAM_WD, ADAM_STEP = 0.001, 0.9, 0.999, 1e-08, 0.01, 10
NEG = -1e30

VMEM_LIMIT_BYTES = 56 * 1024 * 1024
RESIDENT_WEIGHT_BYTES = 12 * 1024 * 1024
LANES = 128
SUBLANES = 8


def _cparams(*sem):
    return pltpu.CompilerParams(dimension_semantics=sem if sem else None, vmem_limit_bytes=VMEM_LIMIT_BYTES)


def _pick_tile(n, cap):
    best = 0
    for t in range(LANES, min(n, cap) + 1, LANES):
        if n % t == 0:
            best = t
    return best if best else n


def _bdot(a, b):
    return jnp.dot(a.astype(BF16), b.astype(BF16), preferred_element_type=F32)


def _bdot_nt(a, b):
    return lax.dot_general(a.astype(BF16), b.astype(BF16), (((1,), (1,)), ((), ())), preferred_element_type=F32)


def _bdot_tn(a, b):
    return lax.dot_general(a.astype(BF16), b.astype(BF16), (((0,), (0,)), ((), ())), preferred_element_type=F32)


def _hdot(a, b):
    return jnp.dot(a, b, precision=HI, preferred_element_type=F32)


def _mdot(a, b):
    return jnp.dot(a, b, precision=MID, preferred_element_type=F32)


def _mdot_nt(a, b):
    return lax.dot_general(a, b, (((1,), (1,)), ((), ())), precision=MID, preferred_element_type=F32)


def _mdot_tn(a, b):
    return lax.dot_general(a, b, (((0,), (0,)), ((), ())), precision=MID, preferred_element_type=F32)


def _mm_nn(a, w, out_dtype, name):
    M, K = a.shape
    G, _, n = w.shape
    tm = min(512, M)
    tn = _pick_tile(n, 1536)
    nps = n // tn

    if G > 1 and w.size * 2 <= RESIDENT_WEIGHT_BYTES and n % LANES == 0:
        def body_all(a_ref, w_ref, o_ref):
            av = a_ref[...]
            for s in range(G):
                o_ref[:, s * n : (s + 1) * n] = _bdot(av, w_ref[s]).astype(o_ref.dtype)

        return pl.pallas_call(
            body_all,
            grid=(M // tm,),
            in_specs=[pl.BlockSpec((tm, K), lambda i: (i, 0)), pl.BlockSpec((G, K, n), lambda i: (0, 0, 0))],
            out_specs=pl.BlockSpec((tm, G * n), lambda i: (i, 0)),
            out_shape=SDS((M, G * n), out_dtype),
            name=name,
            compiler_params=_cparams("parallel"),
        )(a, w)

    def body(a_ref, w_ref, o_ref):
        o_ref[...] = _bdot(a_ref[...], w_ref[...]).astype(o_ref.dtype)

    return pl.pallas_call(
        body,
        grid=(G * nps, M // tm),
        in_specs=[
            pl.BlockSpec((tm, K), lambda j, i: (i, 0)),
            pl.BlockSpec((None, K, tn), lambda j, i: (j // nps, 0, j % nps)),
        ],
        out_specs=pl.BlockSpec((tm, tn), lambda j, i: (i, j)),
        out_shape=SDS((M, G * n), out_dtype),
        name=name,
        compiler_params=_cparams("parallel", "parallel"),
    )(a, w)


def _mm_nt(a, w, out_dtype, name):
    M = a.shape[0]
    G, K, n = w.shape
    tm = min(512, M)
    tk = _pick_tile(K, 1536)

    if w.size * 2 <= RESIDENT_WEIGHT_BYTES:
        def body_all(a_ref, w_ref, o_ref):
            acc = _bdot_nt(a_ref[:, :n], w_ref[0])
            for s in range(1, G):
                acc = acc + _bdot_nt(a_ref[:, s * n : (s + 1) * n], w_ref[s])
            o_ref[...] = acc.astype(o_ref.dtype)

        return pl.pallas_call(
            body_all,
            grid=(M // tm,),
            in_specs=[pl.BlockSpec((tm, G * n), lambda i: (i, 0)), pl.BlockSpec((G, K, n), lambda i: (0, 0, 0))],
            out_specs=pl.BlockSpec((tm, K), lambda i: (i, 0)),
            out_shape=SDS((M, K), out_dtype),
            name=name,
            compiler_params=_cparams("parallel"),
        )(a, w)

    def body(a_ref, w_ref, o_ref, acc_ref):
        s = pl.program_id(2)

        @pl.when(s == 0)
        def _():
            acc_ref[...] = jnp.zeros_like(acc_ref)

        acc_ref[...] += _bdot_nt(a_ref[...], w_ref[...])

        @pl.when(s == G - 1)
        def _():
            o_ref[...] = acc_ref[...].astype(o_ref.dtype)

    return pl.pallas_call(
        body,
        grid=(K // tk, M // tm, G),
        in_specs=[
            pl.BlockSpec((tm, n), lambda kb, i, s: (i, s)),
            pl.BlockSpec((None, tk, n), lambda kb, i, s: (s, kb, 0)),
        ],
        out_specs=pl.BlockSpec((tm, tk), lambda kb, i, s: (i, kb)),
        out_shape=SDS((M, K), out_dtype),
        scratch_shapes=[pltpu.VMEM((tm, tk), F32)],
        name=name,
        compiler_params=_cparams("parallel", "parallel", "arbitrary"),
    )(a, w)


def _mm_tn(a, d, G, out_dtype, name):
    T, K = a.shape
    n = d.shape[1] // G
    tk = _pick_tile(K, 512)
    tn = _pick_tile(n, 1536)
    nps = n // tn

    def body(a_ref, d_ref, o_ref):
        o_ref[...] = _bdot_tn(a_ref[...], d_ref[...]).astype(o_ref.dtype)

    return pl.pallas_call(
        body,
        grid=(G * nps, K // tk),
        in_specs=[
            pl.BlockSpec((T, tk), lambda j, i: (0, i)),
            pl.BlockSpec((T, tn), lambda j, i: (0, j)),
        ],
        out_specs=pl.BlockSpec((None, tk, tn), lambda j, i: (j // nps, i, j % nps)),
        out_shape=SDS((G, K, n), out_dtype),
        name=name,
        compiler_params=_cparams("parallel", "parallel"),
    )(a, d)


def _rows_call(body, name, B, S, ins, outs, ts=512):
    ts = min(ts, S)
    in_specs, args = [], []
    for arr, kind in ins:
        W = arr.shape[-1]
        if kind == "row":
            in_specs.append(pl.BlockSpec((None, ts, W), lambda b, s: (b, s, 0)))
        elif kind == "ex":
            in_specs.append(pl.BlockSpec((None, 1, W), lambda b, s: (b, 0, 0)))
        else:
            in_specs.append(pl.BlockSpec((1, W), lambda b, s: (0, 0)))
        args.append(arr)
    out_specs, out_shape = [], []
    for W, dt, kind in outs:
        if kind == "row":
            out_specs.append(pl.BlockSpec((None, ts, W), lambda b, s: (b, s, 0)))
            out_shape.append(SDS((B, S, W), dt))
        else:
            out_specs.append(pl.BlockSpec((None, 1, W), lambda b, s: (b, 0, 0)))
            out_shape.append(SDS((B, 1, W), dt))
    return pl.pallas_call(
        body,
        grid=(B, S // ts),
        in_specs=in_specs,
        out_specs=out_specs,
        out_shape=out_shape,
        name=name,
        compiler_params=_cparams("parallel", "arbitrary"),
    )(*args)


def _acc(ref, val):
    @pl.when(pl.program_id(1) == 0)
    def _():
        ref[...] = jnp.zeros_like(ref)

    ref[...] += val


def _mod_fn(x, sc, sh):
    return x * (1.0 + sc) + sh


def _ln_fn(alpha, x, y, gate, g, b):
    z = alpha * x + (1.0 + gate) * y
    mu = jnp.mean(z, -1, keepdims=True)
    var = jnp.mean(jnp.square(z - mu), -1, keepdims=True)
    return (z - mu) * lax.rsqrt(var + LN_EPS) * g + b


def _modulate(x, sc, sh, name):
    B, S, D = x.shape

    def body(x_ref, sc_ref, sh_ref, h_ref):
        h_ref[...] = _mod_fn(x_ref[...], sc_ref[...], sh_ref[...]).astype(BF16)

    return _rows_call(body, name, B, S, [(x, "row"), (sc, "ex"), (sh, "ex")], [(D, BF16, "row")])[0]


def _ln_fwd(alpha, x, y, gate, g, b, name):
    B, S, D = x.shape

    def body(x_ref, y_ref, gate_ref, g_ref, b_ref, o_ref):
        o_ref[...] = _ln_fn(alpha, x_ref[...], y_ref[...], gate_ref[...], g_ref[...], b_ref[...])

    return _rows_call(
        body, name, B, S, [(x, "row"), (y, "row"), (gate, "ex"), (g, "par"), (b, "par")], [(D, F32, "row")]
    )[0]


def _ln_mod_fwd(alpha, x, y, gate, g, b, sc_next, sh_next, name):
    B, S, D = x.shape

    def body(x_ref, y_ref, gate_ref, g_ref, b_ref, sc_ref, sh_ref, o_ref, h_ref):
        out = _ln_fn(alpha, x_ref[...], y_ref[...], gate_ref[...], g_ref[...], b_ref[...])
        o_ref[...] = out
        h_ref[...] = _mod_fn(out, sc_ref[...], sh_ref[...]).astype(BF16)

    return _rows_call(
        body, name, B, S,
        [(x, "row"), (y, "row"), (gate, "ex"), (g, "par"), (b, "par"), (sc_next, "ex"), (sh_next, "ex")],
        [(D, F32, "row"), (D, BF16, "row")],
    )


def _ln_mod_bwd(alpha, dh, dxr_next, sc_next, x, y, gate, g, b, name):
    B, S, D = x.shape

    def body(dh_ref, dxr_ref, sc_ref, x_ref, y_ref, gate_ref, g_ref, b_ref,
             dx_ref, dy_ref, dgate_ref, dg_ref, db_ref, dsc_ref, dsh_ref):
        out, vjp = jax.vjp(
            functools.partial(_ln_fn, alpha), x_ref[...], y_ref[...], gate_ref[...], g_ref[...], b_ref[...]
        )
        dh_v = dh_ref[...]
        dx, dy, dgate, dg, db = vjp(dxr_ref[...] + dh_v * (1.0 + sc_ref[...]))
        dx_ref[...] = dx
        dy_ref[...] = dy.astype(BF16)
        _acc(dgate_ref, dgate)
        _acc(dg_ref, dg)
        _acc(db_ref, db)
        _acc(dsc_ref, jnp.sum(dh_v * out, axis=0, keepdims=True))
        _acc(dsh_ref, jnp.sum(dh_v, axis=0, keepdims=True))

    return _rows_call(
        body, name, B, S,
        [(dh, "row"), (dxr_next, "row"), (sc_next, "ex"), (x, "row"), (y, "row"), (gate, "ex"), (g, "par"), (b, "par")],
        [(D, F32, "row"), (D, BF16, "row")] + [(D, F32, "acc")] * 5,
    )


def _ln_bwd(alpha, dout, x, y, gate, g, b, name):
    B, S, D = x.shape

    def body(do_ref, x_ref, y_ref, gate_ref, g_ref, b_ref, dxr_ref, dy_ref, dgate_ref, dg_ref, db_ref):
        _, vjp = jax.vjp(
            functools.partial(_ln_fn, alpha), x_ref[...], y_ref[...], gate_ref[...], g_ref[...], b_ref[...]
        )
        dx, dy, dgate, dg, db = vjp(do_ref[...])
        dxr_ref[...] = dx
        dy_ref[...] = dy.astype(BF16)
        _acc(dgate_ref, dgate)
        _acc(dg_ref, dg)
        _acc(db_ref, db)

    return _rows_call(
        body,
        name,
        B,
        S,
        [(dout, "row"), (x, "row"), (y, "row"), (gate, "ex"), (g, "par"), (b, "par")],
        [(D, F32, "row"), (D, BF16, "row"), (D, F32, "acc"), (D, F32, "acc"), (D, F32, "acc")],
    )


def _mod_bwd(dh, dxr, x, sc, name):
    B, S, D = x.shape

    def body(dh_ref, dxr_ref, x_ref, sc_ref, dx_ref, dsc_ref, dsh_ref):
        dh_v = dh_ref[...]
        dx_ref[...] = dxr_ref[...] + dh_v * (1.0 + sc_ref[...])
        _acc(dsc_ref, jnp.sum(dh_v * x_ref[...], axis=0, keepdims=True))
        _acc(dsh_ref, jnp.sum(dh_v, axis=0, keepdims=True))

    return _rows_call(
        body,
        name,
        B,
        S,
        [(dh, "row"), (dxr, "row"), (x, "row"), (sc, "ex")],
        [(D, F32, "row"), (D, F32, "acc"), (D, F32, "acc")],
    )


def _loss_head(y, target, name):
    B, S, D = y.shape

    def body(y_ref, t_ref, l_ref, dy_ref):
        e = y_ref[...] - t_ref[...]
        dy_ref[...] = e * (1.0 / D)
        part = 0.5 * jnp.sum(jnp.sum(e * e, axis=1, keepdims=True) * (1.0 / D), axis=0, keepdims=True)
        _acc(l_ref, jnp.broadcast_to(part, (1, LANES)))

    loss, dy = _rows_call(
        body, name, B, S, [(y, "row"), (target, "row")], [(LANES, F32, "acc"), (D, F32, "row")]
    )
    return jnp.sum(loss[:, 0, 0]), dy


def _ffn_in(h, w, name):
    M, K = h.shape
    G, _, n = w.shape
    assert G == 4
    tm = min(512, M)
    tn = _pick_tile(n, 1536)
    nps = n // tn
    half = 2 * nps

    def body(h_ref, wg_ref, wu_ref, a_ref, g_ref, u_ref):
        hv = h_ref[...]
        g = _bdot(hv, wg_ref[...])
        u = _bdot(hv, wu_ref[...])
        a_ref[...] = (jax.nn.silu(g) * u).astype(BF16)
        g_ref[...] = g.astype(BF16)
        u_ref[...] = u.astype(BF16)

    out = pl.BlockSpec((tm, tn), lambda j, i: (i, j))
    return pl.pallas_call(
        body,
        grid=(half, M // tm),
        in_specs=[
            pl.BlockSpec((tm, K), lambda j, i: (i, 0)),
            pl.BlockSpec((None, K, tn), lambda j, i: (j // nps, 0, j % nps)),
            pl.BlockSpec((None, K, tn), lambda j, i: (2 + j // nps, 0, j % nps)),
        ],
        out_specs=[out, out, out],
        out_shape=[SDS((M, 2 * n), BF16)] * 3,
        name=name,
        compiler_params=_cparams("parallel", "parallel"),
    )(h, w, w)


def _swiglu_bwd(g, u, da, name):
    B, S, F = g.shape

    def body(g_ref, u_ref, da_ref, du_ref):
        _, vjp = jax.vjp(lambda gv, uv: jax.nn.silu(gv) * uv, g_ref[...].astype(F32), u_ref[...].astype(F32))
        dg, du = vjp(da_ref[...])
        du_ref[:, :F] = dg.astype(BF16)
        du_ref[:, F:] = du.astype(BF16)

    return _rows_call(body, name, B, S, [(g, "row"), (u, "row"), (da, "row")], [(2 * F, BF16, "row")])[0]


def _rms_fn(x, g):
    return x * lax.rsqrt(jnp.mean(jnp.square(x), -1, keepdims=True) + RMS_EPS) * g


def _mla_mid_fwd(proj, gq, gkv, name):
    B, S, _ = proj.shape

    def body(p_ref, gq_ref, gkv_ref, qn_ref, kvn_ref):
        p = p_ref[...]
        qn_ref[...] = _rms_fn(p[:, :Q_LORA], gq_ref[...]).astype(BF16)
        kvn_ref[...] = _rms_fn(p[:, Q_LORA : Q_LORA + KV_LORA], gkv_ref[...]).astype(BF16)

    return _rows_call(
        body, name, B, S, [(proj, "row"), (gq, "par"), (gkv, "par")], [(Q_LORA, BF16, "row"), (KV_LORA, BF16, "row")]
    )


def _mla_mid_bwd(proj, dqn, dkvn, dkr, gq, gkv, name):
    B, S, W = proj.shape

    def body(p_ref, dqn_ref, dkvn_ref, dkr_ref, gq_ref, gkv_ref, dp_ref, dgq_ref, dgkv_ref):
        p = p_ref[...]
        _, vq = jax.vjp(_rms_fn, p[:, :Q_LORA], gq_ref[...])
        dql, dgq = vq(dqn_ref[...])
        _, vkv = jax.vjp(_rms_fn, p[:, Q_LORA : Q_LORA + KV_LORA], gkv_ref[...])
        dkvl, dgkv = vkv(dkvn_ref[...])
        dp_ref[:, :Q_LORA] = dql.astype(BF16)
        dp_ref[:, Q_LORA : Q_LORA + KV_LORA] = dkvl.astype(BF16)
        dp_ref[:, Q_LORA + KV_LORA :] = dkr_ref[...].astype(BF16)
        _acc(dgq_ref, dgq)
        _acc(dgkv_ref, dgkv)

    return _rows_call(
        body,
        name,
        B,
        S,
        [(proj, "row"), (dqn, "row"), (dkvn, "row"), (dkr, "row"), (gq, "par"), (gkv, "par")],
        [(W, BF16, "row"), (Q_LORA, F32, "acc"), (KV_LORA, F32, "acc")],
    )


def _rope(x, cos, sin):
    h = QK_ROPE // 2
    x1, x2 = x[:, :h], x[:, h:]
    return jnp.concatenate([x1 * cos - x2 * sin, x1 * sin + x2 * cos], axis=1)


def _rope_t(dy, cos, sin):
    h = QK_ROPE // 2
    d1, d2 = dy[:, :h], dy[:, h:]
    return jnp.concatenate([d1 * cos + d2 * sin, d2 * cos - d1 * sin], axis=1)


def _heads_call(body, name, B, S, ins, outs, ts=256):
    ts = min(ts, S)
    in_specs, args = [], []
    for arr, kind in ins:
        if kind == "row":
            in_specs.append(pl.BlockSpec((None, ts, arr.shape[-1]), lambda b, s: (b, s, 0)))
        else:
            in_specs.append(pl.BlockSpec((arr.shape[0], None, ts, arr.shape[-1]), lambda b, s: (0, b, s, 0)))
        args.append(arr)
    out_specs, out_shape = [], []
    for shape, dt, kind in outs:
        if kind == "row":
            out_specs.append(pl.BlockSpec((None, ts, shape[-1]), lambda b, s: (b, s, 0)))
        else:
            out_specs.append(pl.BlockSpec((shape[0], None, ts, shape[-1]), lambda b, s: (0, b, s, 0)))
        out_shape.append(SDS(shape, dt))
    return pl.pallas_call(
        body,
        grid=(B, S // ts),
        in_specs=in_specs,
        out_specs=out_specs,
        out_shape=out_shape,
        name=name,
        compiler_params=_cparams("parallel", "parallel"),
    )(*args)


def _mla_prep_fwd(q, kv, proj, cos, sin, name):
    B, S, _ = q.shape
    H = MLA_HEADS

    def body(q_ref, kv_ref, p_ref, cos_ref, sin_ref, qh_ref, kh_ref, vh_ref):
        cos_v, sin_v = cos_ref[...], sin_ref[...]
        kr = _rope(p_ref[:, Q_LORA + KV_LORA :], cos_v, sin_v).astype(BF16)
        for h in range(H):
            qn = q_ref[:, h * QK_DIM : h * QK_DIM + QK_NOPE]
            qr = _rope(q_ref[:, h * QK_DIM + QK_NOPE : (h + 1) * QK_DIM], cos_v, sin_v)
            qh_ref[h] = jnp.concatenate([qn, qr], axis=1).astype(BF16)
            kn = kv_ref[:, h * 128 : h * 128 + QK_NOPE].astype(BF16)
            kh_ref[h] = jnp.concatenate([kn, kr], axis=1)
            vh_ref[h] = kv_ref[:, h * 128 + QK_NOPE : (h + 1) * 128].astype(BF16)

    return _heads_call(
        body,
        name,
        B,
        S,
        [(q, "row"), (kv, "row"), (proj, "row"), (cos, "row"), (sin, "row")],
        [((H, B, S, QK_DIM), BF16, "heads"), ((H, B, S, QK_DIM), BF16, "heads"), ((H, B, S, V_HEAD), BF16, "heads")],
    )


def _mla_prep_bwd(dqh, dkh, dvh, cos, sin, name):
    H, B, S, _ = dqh.shape

    def body(dqh_ref, dkh_ref, dvh_ref, cos_ref, sin_ref, dq_ref, dkv_ref, dkr_ref):
        cos_v, sin_v = cos_ref[...], sin_ref[...]
        dkr = jnp.zeros((cos_v.shape[0], QK_ROPE), F32)
        for h in range(H):
            dqv = dqh_ref[h].astype(F32)
            dq_ref[:, h * QK_DIM : h * QK_DIM + QK_NOPE] = dqv[:, :QK_NOPE].astype(BF16)
            dq_ref[:, h * QK_DIM + QK_NOPE : (h + 1) * QK_DIM] = _rope_t(dqv[:, QK_NOPE:], cos_v, sin_v).astype(BF16)
            dkv = dkh_ref[h].astype(F32)
            dkv_ref[:, h * 128 : h * 128 + QK_NOPE] = dkv[:, :QK_NOPE].astype(BF16)
            dkv_ref[:, h * 128 + QK_NOPE : (h + 1) * 128] = dvh_ref[h]
            dkr = dkr + dkv[:, QK_NOPE:]
        dkr_ref[...] = _rope_t(dkr, cos_v, sin_v)

    return _heads_call(
        body,
        name,
        B,
        S,
        [(dqh, "heads"), (dkh, "heads"), (dvh, "heads"), (cos, "row"), (sin, "row")],
        [((B, S, H * QK_DIM), BF16, "row"), ((B, S, H * 128), BF16, "row"), ((B, S, QK_ROPE), F32, "row")],
    )


LOG2E = 1.4426950408889634
ATTN_TILE = 1024
ATTN_DIAG_SUB = 512


def _attn_fwd(qh, kh, vh, name):
    H, B, S, _ = qh.shape
    t = min(ATTN_TILE, S)
    scale = QK_DIM**-0.5
    c2 = scale * LOG2E

    def body(q_ref, k_ref, v_ref, o_ref, lse_ref):
        i = pl.program_id(2)
        qs = [q_ref[0], q_ref[1]]

        def update(state, q, k, v, mask):
            m, l, acc = state
            s = _bdot_nt(q, k)
            if mask is not None:
                s = jnp.where(mask, s, NEG)
            m_new = jnp.maximum(m, jnp.max(s, axis=1, keepdims=True))
            p = jnp.exp2((s - m_new) * c2)
            a = jnp.exp2((m - m_new) * c2)
            return m_new, a * l + jnp.sum(p, axis=1, keepdims=True), a * acc + _bdot(p, v)

        def step(j, carry):
            rows = pl.ds(pl.multiple_of(j * t, t), t)
            return tuple(update(carry[hh], qs[hh], k_ref[hh, rows, :], v_ref[hh, rows, :], None) for hh in range(2))

        one = (jnp.full((t, 1), NEG, F32), jnp.zeros((t, 1), F32), jnp.zeros((t, V_HEAD), F32))
        carry = lax.fori_loop(0, i, step, (one, one))
        rows = pl.ds(pl.multiple_of(i * t, t), t)
        causal = lax.broadcasted_iota(jnp.int32, (t, t), 0) >= lax.broadcasted_iota(jnp.int32, (t, t), 1)
        carry = tuple(update(carry[hh], qs[hh], k_ref[hh, rows, :], v_ref[hh, rows, :], causal) for hh in range(2))
        outs = []
        for hh in range(2):
            m, l, acc = carry[hh]
            outs.append(acc / l)
            lse_ref[hh] = m * scale + jnp.log(l)
        o_ref[...] = jnp.concatenate(outs, axis=1).astype(BF16)

    return pl.pallas_call(
        body,
        grid=(B, H // 2, S // t),
        in_specs=[
            pl.BlockSpec((2, None, t, QK_DIM), lambda b, p, i: (p, b, i, 0)),
            pl.BlockSpec((2, None, S, QK_DIM), lambda b, p, i: (p, b, 0, 0)),
            pl.BlockSpec((2, None, S, V_HEAD), lambda b, p, i: (p, b, 0, 0)),
        ],
        out_specs=[
            pl.BlockSpec((None, t, 2 * V_HEAD), lambda b, p, i: (b, i, p)),
            pl.BlockSpec((2, None, t, 1), lambda b, p, i: (p, b, i, 0)),
        ],
        out_shape=[SDS((B, S, H * V_HEAD), BF16), SDS((H, B, S, 1), F32)],
        name=name,
        compiler_params=_cparams("parallel", "parallel", "arbitrary"),
    )(qh, kh, vh)


def _attn_bwd(qh, kh, vh, o, do, lse, name):
    H, B, S, _ = qh.shape
    t = min(ATTN_TILE, S)
    sub = min(ATTN_DIAG_SUB, t)
    nq = S // t
    scale = QK_DIM**-0.5
    c2 = scale * LOG2E

    def body(q_ref, k_ref, v_ref, o_ref, do_ref, lse_ref, dq_ref, dk_ref, dv_ref, dq_acc, delta_ref, lse2_ref):
        prod = o_ref[...].astype(F32) * do_ref[...].astype(F32)
        for hh in range(2):
            delta_ref[hh] = jnp.sum(prod[:, hh * V_HEAD : (hh + 1) * V_HEAD], axis=1, keepdims=True)
            lse2_ref[hh] = lse_ref[hh] * LOG2E
        dq_acc[...] = jnp.zeros_like(dq_acc)

        def kloop(j, _):
            krows = pl.ds(pl.multiple_of(j * t, t), t)
            ks = [k_ref[0, krows, :], k_ref[1, krows, :]]
            vs = [v_ref[0, krows, :], v_ref[1, krows, :]]

            def pair(hh, qrows, k, v, mask):
                q = q_ref[hh, qrows, :]
                do_h = do_ref[qrows, :][:, hh * V_HEAD : (hh + 1) * V_HEAD]
                p = jnp.exp2(_bdot_nt(q, k) * c2 - lse2_ref[hh, qrows, :])
                if mask is not None:
                    p = jnp.where(mask, p, 0.0)
                dv = _bdot_tn(p, do_h)
                ds = (p * (_bdot_nt(do_h, v) - delta_ref[hh, qrows, :])).astype(BF16)
                dq_acc[hh, qrows, :] += _bdot(ds, k)
                return _bdot_tn(ds, q), dv

            def qstep(i, carry):
                qrows = pl.ds(pl.multiple_of(i * t, t), t)
                out = []
                for hh in range(2):
                    dk, dv = pair(hh, qrows, ks[hh], vs[hh], None)
                    out.append((carry[hh][0] + dk, carry[hh][1] + dv))
                return tuple(out)

            def diagonal_step():
                out = []
                for hh in range(2):
                    dks, dvs = [], []
                    for c in range(t // sub):
                        r0 = c * sub
                        qrows = pl.ds(pl.multiple_of(j * t + r0, sub), t - r0)
                        mask = (lax.broadcasted_iota(jnp.int32, (t - r0, sub), 0)
                                >= lax.broadcasted_iota(jnp.int32, (t - r0, sub), 1))
                        dk, dv = pair(hh, qrows, ks[hh][r0 : r0 + sub], vs[hh][r0 : r0 + sub], mask)
                        dks.append(dk)
                        dvs.append(dv)
                    out.append((jnp.concatenate(dks, axis=0), jnp.concatenate(dvs, axis=0)))
                return tuple(out)

            carry = lax.fori_loop(j + 1, nq, qstep, diagonal_step())
            for hh in range(2):
                dk_ref[hh, krows, :] = (carry[hh][0] * scale).astype(BF16)
                dv_ref[hh, krows, :] = carry[hh][1].astype(BF16)
            return 0

        lax.fori_loop(0, nq, kloop, 0)
        dq_ref[...] = (dq_acc[...] * scale).astype(BF16)

    hspec = lambda w: pl.BlockSpec((2, None, S, w), lambda b, p: (p, b, 0, 0))
    ospec = pl.BlockSpec((None, S, 2 * V_HEAD), lambda b, p: (b, 0, p))
    return pl.pallas_call(
        body,
        grid=(B, H // 2),
        in_specs=[hspec(QK_DIM), hspec(QK_DIM), hspec(V_HEAD), ospec, ospec, hspec(1)],
        out_specs=[hspec(QK_DIM), hspec(QK_DIM), hspec(V_HEAD)],
        out_shape=[SDS((H, B, S, QK_DIM), BF16), SDS((H, B, S, QK_DIM), BF16), SDS((H, B, S, V_HEAD), BF16)],
        scratch_shapes=[pltpu.VMEM((2, S, QK_DIM), F32), pltpu.VMEM((2, S, 1), F32), pltpu.VMEM((2, S, 1), F32)],
        name=name,
        compiler_params=_cparams("parallel", "parallel"),
    )(qh, kh, vh, o, do, lse)


def _hgrn_pre(q, fx, lb):
    sig = jax.nn.sigmoid(fx)
    f = lb + (1.0 - lb) * sig
    return jax.nn.silu(q), 1.0 - f, jnp.log(f)


def _hgrn_gate(o, gg, gn):
    return _rms_fn(o, gn) * jax.nn.silu(gg)


def _tri(n, lower):
    r = lax.broadcasted_iota(jnp.int32, (n, n), 0)
    c = lax.broadcasted_iota(jnp.int32, (n, n), 1)
    return ((r >= c) if lower else (r <= c)).astype(F32)


def _hgrn_intra_fwd(qs, k, v, b):
    C, SB = qs.shape[0], min(HGRN_SUB, qs.shape[0])
    ridx = lax.broadcasted_iota(jnp.int32, (SUBLANES, 1), 0)
    outs = []
    for i in range(C // SB):
        r0 = i * SB
        qi, ki, vi, bi = qs[r0 : r0 + SB], k[r0 : r0 + SB], v[r0 : r0 + SB], b[r0 : r0 + SB]
        ng = SB // SUBLANES
        qg = [qi[g * SUBLANES : (g + 1) * SUBLANES] for g in range(ng)]
        bg = [bi[g * SUBLANES : (g + 1) * SUBLANES] for g in range(ng)]
        accg = [jnp.zeros((SUBLANES, v.shape[1]), F32) for _ in range(ng)]
        for s in range(SB):
            gs, so = divmod(s, SUBLANES)
            k_s, v_s, b_s = ki[s : s + 1], vi[s : s + 1], bi[s : s + 1]
            for tg in range(gs, ng):
                if tg == gs:
                    mask = ridx >= so
                    w = jnp.where(mask, qg[tg] * k_s * jnp.exp(jnp.where(mask, bg[tg] - b_s, 0.0)), 0.0)
                else:
                    w = qg[tg] * k_s * jnp.exp(bg[tg] - b_s)
                accg[tg] = accg[tg] + jnp.sum(w, axis=1, keepdims=True) * v_s
        acc = jnp.concatenate(accg, axis=0)
        if i > 0:
            ref = bi[0:1]
            qt = qi * jnp.exp(bi - ref)
            kt = k[:r0] * jnp.exp(ref - b[:r0])
            acc = acc + _bdot(_mdot_nt(qt, kt), v[:r0])
        outs.append(acc)
    return jnp.concatenate(outs, axis=0)


def _hgrn_intra_bwd(qs, k, v, b, do):
    C, SB = qs.shape[0], min(HGRN_SUB, qs.shape[0])
    nb = C // SB
    ridx = lax.broadcasted_iota(jnp.int32, (SUBLANES, 1), 0)
    dq_p = [None] * nb
    dk_p = [jnp.zeros((SB, k.shape[1]), F32) for _ in range(nb)]
    dv_p = [jnp.zeros((SB, v.shape[1]), F32) for _ in range(nb)]
    for i in range(nb):
        r0 = i * SB
        qi, ki, vi, bi, doi = qs[r0 : r0 + SB], k[r0 : r0 + SB], v[r0 : r0 + SB], b[r0 : r0 + SB], do[r0 : r0 + SB]
        ng = SB // SUBLANES
        qg = [qi[g * SUBLANES : (g + 1) * SUBLANES] for g in range(ng)]
        bg = [bi[g * SUBLANES : (g + 1) * SUBLANES] for g in range(ng)]
        dog = [doi[g * SUBLANES : (g + 1) * SUBLANES] for g in range(ng)]
        dqg =[jnp.zeros((SUBLANES, k.shape[1]), F32) for _ in range(ng)]
        dkg = [jnp.zeros((SUBLANES, k.shape[1]), F32) for _ in range(ng)]
        dvg = [jnp.zeros((SUBLANES, v.shape[1]), F32) for _ in range(ng)]
        for s in range(SB):
            gs, so = divmod(s, SUBLANES)
            k_s, v_s, b_s = ki[s : s + 1], vi[s : s + 1], bi[s : s + 1]
            dk_s = jnp.zeros((SUBLANES, k.shape[1]), F32)
            dv_s = jnp.zeros((SUBLANES, v.shape[1]), F32)
            for tg in range(gs, ng):
                if tg == gs:
                    mask = ridx >= so
                    e = jnp.where(mask, jnp.exp(jnp.where(mask, bg[tg] - b_s, 0.0)), 0.0)
                else:
                    e = jnp.exp(bg[tg] - b_s)
                da = jnp.sum(dog[tg] * v_s, axis=1, keepdims=True)
                qe = qg[tg] * e
                a = jnp.sum(qe * k_s, axis=1, keepdims=True)
                dqg[tg] = dqg[tg] + da * (k_s * e)
                dk_s = dk_s + da * qe
                dv_s = dv_s + a * dog[tg]
            dkg[gs] = jnp.where(ridx == so, dkg[gs] + jnp.sum(dk_s, axis=0, keepdims=True), dkg[gs])
            dvg[gs] = jnp.where(ridx == so, dvg[gs] + jnp.sum(dv_s, axis=0, keepdims=True), dvg[gs])
        dqi = jnp.concatenate(dqg, axis=0)
        dki = jnp.concatenate(dkg, axis=0)
        dvi = jnp.concatenate(dvg, axis=0)
        if i > 0:
            ref = bi[0:1]
            eq = jnp.exp(bi - ref)
            ek = jnp.exp(ref - b[:r0])
            qt = qi * eq
            kt = k[:r0] * ek
            A = _mdot_nt(qt, kt)
            dA = _bdot_nt(doi, v[:r0])
            dvl = _bdot_tn(A, doi)
            dqi = dqi + _mdot(dA, kt) * eq
            dkl = _mdot_tn(dA, qt) * ek
            for j in range(i):
                dk_p[j] = dk_p[j] + dkl[j * SB : (j + 1) * SB]
                dv_p[j] = dv_p[j] + dvl[j * SB : (j + 1) * SB]
        dq_p[i] = dqi
        dk_p[i] = dk_p[i] + dki
        dv_p[i] = dv_p[i] + dvi
    return jnp.concatenate(dq_p, axis=0), jnp.concatenate(dk_p, axis=0), jnp.concatenate(dv_p, axis=0)


def _hgrn_fwd(proj, lb, gn, name):
    B, S, W = proj.shape
    HK = W // 4
    H = HK // HGRN_K
    C = min(HGRN_CHUNK, S)
    N = S // C

    HP = HGRN_PAR if H % HGRN_PAR == 0 else 1
    WP = HP * HGRN_K

    def body(q_ref, f_ref, i_ref, g_ref, lb_ref, gn_ref, og_ref, o_ref, st_ref):
        gn_v = gn_ref[...]
        tril = _tri(C, True)

        def chunk(n, sts):
            rows = pl.ds(pl.multiple_of(n * C, C), C)
            out = []
            for hh in range(HP):
                ln = slice(hh * HGRN_K, (hh + 1) * HGRN_K)
                st = sts[hh]
                qs, k, g = _hgrn_pre(q_ref[rows, ln], f_ref[rows, ln], lb_ref[:, ln])
                v = i_ref[rows, ln]
                b = _hdot(tril, g)
                st_ref[hh, n] = st
                o = _hgrn_intra_fwd(qs, k, v, b) + _bdot_nt(qs * jnp.exp(b), st)
                bl = b[C - 1 : C]
                out.append(st * jnp.exp(bl) + _bdot_tn(v, k * jnp.exp(bl - b)))
                o_ref[rows, ln] = o
                og_ref[rows, ln] = _hgrn_gate(o, g_ref[rows, ln], gn_v).astype(BF16)
            return tuple(out)

        lax.fori_loop(0, N, chunk, tuple(jnp.zeros((HGRN_K, HGRN_K), F32) for _ in range(HP)))

    col = lambda part: pl.BlockSpec((None, S, WP), lambda b, h: (b, 0, part * (H // HP) + h))
    return pl.pallas_call(
        body,
        grid=(B, H // HP),
        in_specs=[col(0), col(1), col(2), col(3), pl.BlockSpec((1, WP), lambda b, h: (0, h)), pl.BlockSpec((1, HGRN_K), lambda b, h: (0, 0))],
        out_specs=[col(0), col(0), pl.BlockSpec((None, HP, N, HGRN_K, HGRN_K), lambda b, h: (b, h, 0, 0, 0))],
        out_shape=[SDS((B, S, HK), BF16), SDS((B, S, HK), F32), SDS((B, H, N, HGRN_K, HGRN_K), F32)],
        name=name,
        compiler_params=_cparams("parallel", "parallel"),
    )(proj, proj, proj, proj, lb, gn)


def _hgrn_bwd(proj, lb, gn, o_pre, states, dog, name):
    B, S, W = proj.shape
    HK = W // 4
    H = HK // HGRN_K
    C = min(HGRN_CHUNK, S)
    N = S // C

    HP = HGRN_PAR if H % HGRN_PAR == 0 else 1
    WP = HP * HGRN_K

    def body(q_ref, f_ref, i_ref, g_ref, lb_ref, gn_ref, o_ref, st_ref, dog_ref, dq_ref, df_ref, di_ref, dg_ref, dlb_ref, dgn_ref):
        gn_v = gn_ref[...]
        tril = _tri(C, True)
        triu = _tri(C, False)

        def chunk(idx, carry):
            n = N - 1 - idx
            rows = pl.ds(pl.multiple_of(n * C, C), C)
            out = []
            for hh in range(HP):
                ln = slice(hh * HGRN_K, (hh + 1) * HGRN_K)
                dst, dlb, dgn = carry[hh]
                (qs, k, g), pre_vjp = jax.vjp(_hgrn_pre, q_ref[rows, ln], f_ref[rows, ln], lb_ref[:, ln])
                v = i_ref[rows, ln]
                _, gate_vjp = jax.vjp(_hgrn_gate, o_ref[rows, ln], g_ref[rows, ln], gn_v)
                do, dgg, dgn_c = gate_vjp(dog_ref[rows, ln])
                b = _hdot(tril, g)
                st0 = st_ref[hh, n]
                eb = jnp.exp(b)
                bl = b[C - 1 : C]
                ebl = jnp.exp(bl)
                ekb = jnp.exp(bl - b)
                qe = qs * eb
                kt = k * ekb
                dqs, dk, dv = _hgrn_intra_bwd(qs, k, v, b, do)
                dqs = dqs + _bdot(do, st0) * eb
                dk = dk + _bdot(v, dst) * ekb
                dv = dv + _bdot_nt(kt, dst)
                st1 = st0 * ebl + _bdot_tn(v, kt)
                dbl = jnp.sum(st1 * dst, axis=0, keepdims=True)
                dst = dst * ebl + _bdot_tn(do, qe)
                dgl = _hdot(triu, qs * dqs - k * dk) + dbl
                dq_pre, dfx, dlb_c = pre_vjp((dqs, dk, dgl))
                dq_ref[rows, ln] = dq_pre.astype(BF16)
                df_ref[rows, ln] = dfx.astype(BF16)
                di_ref[rows, ln] = dv.astype(BF16)
                dg_ref[rows, ln] = dgg.astype(BF16)
                out.append((dst, dlb + dlb_c, dgn + dgn_c))
            return tuple(out)

        zero = jnp.zeros((1, HGRN_K), F32)
        one = (jnp.zeros((HGRN_K, HGRN_K), F32), zero, zero)
        res = lax.fori_loop(0, N, chunk, tuple(one for _ in range(HP)))
        for hh in range(HP):
            dlb_ref[hh] = res[hh][1]
            dgn_ref[hh] = res[hh][2]

    col = lambda part: pl.BlockSpec((None, S, WP), lambda b, h: (b, 0, part * (H // HP) + h))
    vec = pl.BlockSpec((None, HP, 1, HGRN_K), lambda b, h: (b, h, 0, 0))
    return pl.pallas_call(
        body,
        grid=(B, H // HP),
        in_specs=[
            col(0), col(1), col(2), col(3),
            pl.BlockSpec((1, WP), lambda b, h: (0, h)),
            pl.BlockSpec((1, HGRN_K), lambda b, h: (0, 0)),
            col(0),
            pl.BlockSpec((None, HP, N, HGRN_K, HGRN_K), lambda b, h: (b, h, 0, 0, 0)),
            col(0),
        ],
        out_specs=[col(0), col(0), col(0), col(0), vec, vec],
        out_shape=[SDS((B, S, HK), BF16)] * 4 + [SDS((B, H, 1, HGRN_K), F32)] * 2,
        name=name,
        compiler_params=_cparams("parallel", "parallel"),
    )(proj, proj, proj, proj, lb, gn, o_pre, states, dog)


def _ada_fwd(c_all, w, b, name):
    Bg, D = c_all.shape
    L, _, n = w.shape

    def body(c_ref, w_ref, b_ref, o_ref):
        o_ref[...] = _bdot(jax.nn.silu(c_ref[...]), w_ref[...]) + b_ref[...]

    return pl.pallas_call(
        body,
        grid=(L,),
        in_specs=[
            pl.BlockSpec((Bg, D), lambda l: (0, 0)),
            pl.BlockSpec((None, D, n), lambda l: (l, 0, 0)),
            pl.BlockSpec((None, 1, n), lambda l: (l, 0, 0)),
        ],
        out_specs=pl.BlockSpec((None, Bg, n), lambda l: (l, 0, 0)),
        out_shape=SDS((L, Bg, n), F32),
        name=name,
        compiler_params=_cparams("parallel"),
    )(c_all, w, b)


def _ada_bwd(c_all, dmod, name):
    Bg, D = c_all.shape
    L, _, n = dmod.shape

    def body(c_ref, d_ref, dw_ref, db_ref):
        d = d_ref[...]
        dw_ref[...] = _bdot_tn(jax.nn.silu(c_ref[...]), d)
        db_ref[...] = jnp.sum(d, axis=0, keepdims=True)

    return pl.pallas_call(
        body,
        grid=(L,),
        in_specs=[pl.BlockSpec((Bg, D), lambda l: (0, 0)), pl.BlockSpec((None, Bg, n), lambda l: (l, 0, 0))],
        out_specs=[pl.BlockSpec((None, D, n), lambda l: (l, 0, 0)), pl.BlockSpec((None, 1, n), lambda l: (l, 0, 0))],
        out_shape=[SDS((L, D, n), F32), SDS((L, 1, n), F32)],
        name=name,
        compiler_params=_cparams("parallel"),
    )(c_all, dmod)


def _adamw(w, gs, m, v, name):
    shape = w.shape
    cols = shape[-1]
    rows = w.size // cols
    tr = rows
    for cand in (512, 256, 128, 64, 32, 16, 8):
        if rows % cand == 0 and cand * cols * 4 <= 2 * 1024 * 1024:
            tr = cand
            break
    as2d = lambda a: a.reshape(rows, cols)
    ng = len(gs)
    c1 = 1.0 / (1.0 - ADAM_B1**ADAM_STEP)
    c2 = 1.0 / (1.0 - ADAM_B2**ADAM_STEP)

    def body(*refs):
        w_ref, m_ref, v_ref = refs[0], refs[1], refs[2]
        g_refs = refs[3 : 3 + ng]
        g_out, d_out, m_out, v_out = refs[3 + ng :]
        g = g_refs[0][...].astype(F32)
        for r in g_refs[1:]:
            g = g + r[...].astype(F32)
        m_new = ADAM_B1 * m_ref[...] + (1.0 - ADAM_B1) * g
        v_new = ADAM_B2 * v_ref[...] + (1.0 - ADAM_B2) * jnp.square(g)
        g_out[...] = g
        m_out[...] = m_new
        v_out[...] = v_new
        d_out[...] = -ADAM_LR * ((m_new * c1) / (jnp.sqrt(v_new * c2) + ADAM_EPS) + ADAM_WD * w_ref[...])

    spec = pl.BlockSpec((tr, cols), lambda i: (i, 0))
    outs = pl.pallas_call(
        body,
        grid=(rows // tr,),
        in_specs=[spec] * (3 + ng),
        out_specs=[spec] * 4,
        out_shape=[SDS((rows, cols), F32)] * 4,
        name=name,
        compiler_params=_cparams("parallel"),
    )(as2d(w), as2d(m), as2d(v), *[as2d(g) for g in gs])
    return tuple(o.reshape(shape) for o in outs)


def _sum4(own, recv, name):
    shape = own.shape
    cols = shape[-1]
    rows = own.size // cols
    tr = rows
    for cand in (512, 256, 128, 64, 32, 16):
        if rows % cand == 0 and cand * cols * 4 <= 2 * 1024 * 1024:
            tr = cand
            break

    def body(own_ref, recv_ref, o_ref):
        acc = own_ref[...].astype(F32)
        for r in range(3):
            acc = acc + recv_ref[r].astype(F32)
        o_ref[...] = acc

    out = pl.pallas_call(
        body,
        grid=(rows // tr,),
        in_specs=[pl.BlockSpec((tr, cols), lambda i: (i, 0)), pl.BlockSpec((3, tr, cols), lambda i: (0, i, 0))],
        out_specs=pl.BlockSpec((tr, cols), lambda i: (i, 0)),
        out_shape=SDS((rows, cols), F32),
        name=name,
        compiler_params=_cparams("parallel"),
    )(own.reshape(rows, cols), recv.reshape(3, rows, cols))
    return out.reshape(shape)


def _my_place():
    return lax.axis_index("x"), lax.axis_index("y"), lax.axis_index("c")


def _flip(v, bit):
    return 1 - v if bit else v


def _allgather8(x, name):
    r, n = x.shape

    def body(x_ref, o_ref, send_sems, recv_sems, local_sem):
        mx, my, mc = _my_place()
        me = 4 * mx + 2 * my + mc
        mine = pltpu.make_async_copy(x_ref, o_ref.at[me], local_sem)
        mine.start()
        sends = []
        for rel in range(1, 8):
            peer = (_flip(mx, rel & 4), _flip(my, rel & 2), _flip(mc, rel & 1))
            cp = pltpu.make_async_remote_copy(
                src_ref=x_ref, dst_ref=o_ref.at[me], send_sem=send_sems.at[rel - 1], recv_sem=recv_sems.at[rel - 1],
                device_id=peer, device_id_type=MESH,
            )
            cp.start()
            sends.append(cp)
        for rel in range(1, 8):
            px, py, pc = _flip(mx, rel & 4), _flip(my, rel & 2), _flip(mc, rel & 1)
            pltpu.make_async_remote_copy(
                src_ref=x_ref, dst_ref=o_ref.at[4 * px + 2 * py + pc], send_sem=send_sems.at[rel - 1],
                recv_sem=recv_sems.at[rel - 1], device_id=(px, py, pc), device_id_type=MESH,
            ).wait_recv()
        for cp in sends:
            cp.wait_send()
        mine.wait()

    return pl.pallas_call(
        body,
        out_shape=SDS((8, r, n), x.dtype),
        in_specs=[pl.BlockSpec(memory_space=pl.ANY)],
        out_specs=pl.BlockSpec(memory_space=pl.ANY),
        scratch_shapes=[pltpu.SemaphoreType.DMA((7,)), pltpu.SemaphoreType.DMA((7,)), pltpu.SemaphoreType.DMA],
        name=name,
    )(x)


_HBM = pl.BlockSpec(memory_space=pl.ANY)


_SEM = pl.BlockSpec(memory_space=pltpu.SEMAPHORE)
_HBM_ONLY = pl.BlockSpec(memory_space=pltpu.HBM)
_EFFECT = pltpu.SideEffectType.DATAFLOW_SIDE_EFFECTING


def _in_hbm(a):
    return pltpu.with_memory_space_constraint(a, pltpu.HBM)


def _gather_start(lands, after, name):
    n = len(lands)

    def body(*refs):
        land = refs[:n]
        send_sems, recv_sems = refs[n + 1], refs[n + 2]
        token = refs[-1]
        mx, my, mc = _my_place()
        for i in range(n):
            for rel in range(1, 4):
                pltpu.make_async_remote_copy(
                    src_ref=land[i].at[2 * mx + my], dst_ref=land[i].at[2 * mx + my],
                    send_sem=send_sems.at[3 * i + rel - 1], recv_sem=recv_sems.at[3 * i + rel - 1],
                    device_id=(_flip(mx, rel & 2), _flip(my, rel & 1), mc), device_id_type=MESH,
                ).start()
        token[...] = jnp.zeros_like(token)

    outs = pl.pallas_call(
        body,
        name=name,
        out_shape=(
            pltpu.SemaphoreType.DMA((3 * n,)), pltpu.SemaphoreType.DMA((3 * n,)),
            *[pltpu.HBM(a.shape, a.dtype) for a in lands], SDS((8, LANES), F32),
        ),
        in_specs=[_HBM_ONLY] * n + [_HBM],
        out_specs=(_SEM, _SEM, *[_HBM_ONLY] * n, pl.BlockSpec(memory_space=pltpu.VMEM)),
        input_output_aliases={i: 2 + i for i in range(n)},
        compiler_params=pltpu.CompilerParams(has_side_effects=_EFFECT),
    )(*[_in_hbm(a) for a in lands], after)
    return outs[0], outs[1], list(outs[2 : 2 + n]), outs[-1]


def _gather_wait(send_sems, recv_sems, lands, after, name):
    n = len(lands)

    def body(*refs):
        land = refs[:n]
        s_sems, r_sems = refs[n], refs[n + 1]
        mx, my, mc = _my_place()
        for i in range(n):
            for rel in range(1, 4):
                px, py = _flip(mx, rel & 2), _flip(my, rel & 1)
                cp = pltpu.make_async_remote_copy(
                    src_ref=land[i].at[2 * mx + my], dst_ref=land[i].at[2 * px + py],
                    send_sem=s_sems.at[3 * i + rel - 1], recv_sem=r_sems.at[3 * i + rel - 1],
                    device_id=(px, py, mc), device_id_type=MESH,
                )
                cp.wait_send()
                cp.wait_recv()

    outs = pl.pallas_call(
        body,
        name=name,
        out_shape=tuple(pltpu.HBM(a.shape, a.dtype) for a in lands),
        in_specs=[_HBM_ONLY] * n + [_SEM, _SEM, _HBM],
        out_specs=[_HBM_ONLY] * n,
        input_output_aliases={i: i for i in range(n)},
        compiler_params=pltpu.CompilerParams(has_side_effects=_EFFECT),
    )(*lands, send_sems, recv_sems, after)
    return list(outs)


def _scatter_start(slabs, lands, places, name):
    n = len(slabs)

    def body(*refs):
        ins, land = refs[:n], refs[n : 2 * n]
        send_sems, recv_sems = refs[2 * n], refs[2 * n + 1]
        token = refs[-1]
        mx, my, mc = _my_place()
        for i in range(n):
            for rel in range(1, 4):
                px, py = _flip(mx, rel & 2), _flip(my, rel & 1)
                pltpu.make_async_remote_copy(
                    src_ref=ins[i].at[2 * px + py], dst_ref=land[i].at[rel - 1, places[i]],
                    send_sem=send_sems.at[3 * i + rel - 1], recv_sem=recv_sems.at[3 * i + rel - 1],
                    device_id=(px, py, mc), device_id_type=MESH,
                ).start()
        token[...] = jnp.zeros_like(token)

    outs = pl.pallas_call(
        body,
        name=name,
        out_shape=(
            pltpu.SemaphoreType.DMA((3 * n,)), pltpu.SemaphoreType.DMA((3 * n,)),
            *[pltpu.HBM(a.shape, a.dtype) for a in slabs], *[pltpu.HBM(a.shape, a.dtype) for a in lands],
            SDS((8, LANES), F32),
        ),
        in_specs=[_HBM_ONLY] * (2 * n),
        out_specs=(_SEM, _SEM, *[_HBM_ONLY] * (2 * n), pl.BlockSpec(memory_space=pltpu.VMEM)),
        input_output_aliases={i: 2 + i for i in range(2 * n)},
        compiler_params=pltpu.CompilerParams(has_side_effects=_EFFECT),
    )(*[_in_hbm(a) for a in slabs], *[_in_hbm(a) for a in lands])
    return outs[0], outs[1], list(outs[2 : 2 + n]), list(outs[2 + n : 2 + 2 * n]), outs[-1]


def _scatter_wait(send_sems, recv_sems, slabs, lands, places, after, name):
    n = len(slabs)

    def body(*refs):
        ins, land = refs[:n], refs[n : 2 * n]
        s_sems, r_sems = refs[2 * n], refs[2 * n + 1]
        mx, my, mc = _my_place()
        for i in range(n):
            for rel in range(1, 4):
                px, py = _flip(mx, rel & 2), _flip(my, rel & 1)
                cp = pltpu.make_async_remote_copy(
                    src_ref=ins[i].at[2 * px + py], dst_ref=land[i].at[rel - 1, places[i]],
                    send_sem=s_sems.at[3 * i + rel - 1], recv_sem=r_sems.at[3 * i + rel - 1],
                    device_id=(px, py, mc), device_id_type=MESH,
                )
                cp.wait_send()
                cp.wait_recv()

    outs = pl.pallas_call(
        body,
        name=name,
        out_shape=(*[pltpu.HBM(a.shape, a.dtype) for a in slabs], *[pltpu.HBM(a.shape, a.dtype) for a in lands]),
        in_specs=[_HBM_ONLY] * (2 * n) + [_SEM, _SEM, _HBM],
        out_specs=[_HBM_ONLY] * (2 * n),
        input_output_aliases={i: i for i in range(2 * n)},
        compiler_params=pltpu.CompilerParams(has_side_effects=_EFFECT),
    )(*slabs, *lands, send_sems, recv_sems, after)
    return list(outs[:n]), list(outs[n:])


def _swap_start(parts, after, name):
    n = len(parts)
    lands = [lax.empty(a.shape, a.dtype) for a in parts]

    def body(*refs):
        ins, land = refs[:n], refs[n : 2 * n]
        send_sems, recv_sems = refs[2 * n + 1], refs[2 * n + 2]
        mx, my, mc = _my_place()
        for i in range(n):
            pltpu.make_async_remote_copy(
                src_ref=ins[i], dst_ref=land[i], send_sem=send_sems.at[i], recv_sem=recv_sems.at[i],
                device_id=(mx, my, 1 - mc), device_id_type=MESH,
            ).start()

    outs = pl.pallas_call(
        body,
        name=name,
        out_shape=(
            pltpu.SemaphoreType.DMA((n,)), pltpu.SemaphoreType.DMA((n,)),
            *[pltpu.HBM(a.shape, a.dtype) for a in parts], *[pltpu.HBM(a.shape, a.dtype) for a in lands],
        ),
        in_specs=[_HBM_ONLY] * (2 * n) + [_HBM],
        out_specs=(_SEM, _SEM, *[_HBM_ONLY] * (2 * n)),
        input_output_aliases={i: 2 + i for i in range(2 * n)},
        compiler_params=pltpu.CompilerParams(has_side_effects=_EFFECT),
    )(*[_in_hbm(a) for a in parts], *[_in_hbm(a) for a in lands], after)
    return outs[0], outs[1], list(outs[2 : 2 + n]), list(outs[2 + n :])


def _swap_wait(send_sems, recv_sems, parts, lands, after, name):
    n = len(parts)

    def body(*refs):
        ins, land = refs[:n], refs[n : 2 * n]
        s_sems, r_sems = refs[2 * n], refs[2 * n + 1]
        mx, my, mc = _my_place()
        for i in range(n):
            cp = pltpu.make_async_remote_copy(
                src_ref=ins[i], dst_ref=land[i], send_sem=s_sems.at[i], recv_sem=r_sems.at[i],
                device_id=(mx, my, 1 - mc), device_id_type=MESH,
            )
            cp.wait_send()
            cp.wait_recv()

    outs = pl.pallas_call(
        body,
        name=name,
        out_shape=(*[pltpu.HBM(a.shape, a.dtype) for a in parts], *[pltpu.HBM(a.shape, a.dtype) for a in lands]),
        in_specs=[_HBM_ONLY] * (2 * n) + [_SEM, _SEM, _HBM],
        out_specs=[_HBM_ONLY] * (2 * n),
        input_output_aliases={i: i for i in range(2 * n)},
        compiler_params=pltpu.CompilerParams(has_side_effects=_EFFECT),
    )(*parts, *lands, send_sems, recv_sems, after)
    return list(outs[:n]), list(outs[n:])


def _pad_rows(a, rows):
    return jnp.pad(a, ((0, rows - a.shape[0]), (0, 0)))


def kernel(x, c, positions, mla_w_in, mla_q_norm, mla_w_qb, mla_kv_norm, mla_w_kvb, mla_w_o, hgrn_lb, hgrn_w_in, hgrn_g_norm, hgrn_w_o, ffn_w_in, ffn_w_out, ada_w, ada_b, ln_g, ln_b, loss_target, m_mla_w_in, m_mla_q_norm, m_mla_w_qb, m_mla_kv_norm, m_mla_w_kvb, m_mla_w_o, m_hgrn_lb, m_hgrn_w_in, m_hgrn_g_norm, m_hgrn_w_o, m_ffn_w_in, m_ffn_w_out, m_ada_w, m_ada_b, m_ln_g, m_ln_b, v_mla_w_in, v_mla_q_norm, v_mla_w_qb, v_mla_kv_norm, v_mla_w_kvb, v_mla_w_o, v_hgrn_lb, v_hgrn_w_in, v_hgrn_g_norm, v_hgrn_w_o, v_ffn_w_in, v_ffn_w_out, v_ada_w, v_ada_b, v_ln_g, v_ln_b):
    B, S, D = x.shape
    T = B * S
    depth = ada_w.shape[0]
    n_mla, n_hgrn = mla_w_in.shape[0], hgrn_w_in.shape[0]
    n_sub = 2 * depth
    alpha = (2.0 * depth) ** 0.25
    mx, my, mc = _my_place()
    me = 4 * mx + 2 * my + mc
    k_me = 2 * mx + my
    Bg = 8 * B
    HK = hgrn_w_o.shape[1] * 4
    dq = D // 4

    lbw = hgrn_lb.shape[1]
    first = jnp.zeros((8, max(D, 4 * lbw)), F32)
    first = first.at[:B, :D].set(c).at[B : B + n_hgrn, :lbw].set(hgrn_lb)
    first_all = _allgather8(first, "gather_cond")
    c_all = first_all[:, :B, :D].reshape(Bg, D)
    lb_logits = jnp.concatenate([first_all[2 * k, B : B + n_hgrn, :lbw] for k in range(4)], axis=1)

    def lower_bounds_fn(logits):
        soft = jax.nn.softmax(logits, axis=0)
        return jnp.cumsum(soft, axis=0) - soft[0]

    lower_bounds, lower_bounds_vjp = jax.vjp(lower_bounds_fn, lb_logits)

    n_ada = ada_w.shape[-1]
    mod_part = _ada_fwd(c_all, ada_w.reshape(n_sub, D, n_ada), ada_b.reshape(n_sub, 1, n_ada), "ada_fwd")
    mod_all = _allgather8(mod_part.reshape(n_sub * Bg, n_ada), "gather_mod").reshape(8, n_sub, Bg, n_ada)
    mod = jnp.concatenate([mod_all[2 * k] for k in range(4)], axis=-1)
    mod = lax.dynamic_slice_in_dim(mod, me * B, B, axis=1)
    shift = [mod[j, :, None, :D] for j in range(n_sub)]
    scale = [mod[j, :, None, D : 2 * D] for j in range(n_sub)]
    gate = [mod[j, :, None, 2 * D :] for j in range(n_sub)]

    ln_rows = 2 * n_sub
    ln_local = _pad_rows(jnp.concatenate([ln_g.reshape(n_sub, dq), ln_b.reshape(n_sub, dq)], axis=0), -(-ln_rows // 8) * 8)
    ln_pad = jnp.zeros((ln_local.shape[0], -(-dq // LANES) * LANES), F32).at[:, :dq].set(ln_local)
    ln_all = _allgather8(ln_pad, "gather_ln")
    ln_full = jnp.concatenate([ln_all[2 * k, :ln_rows, :dq] for k in range(4)], axis=1)
    lng = [ln_full[j][None, :] for j in range(n_sub)]
    lnb = [ln_full[n_sub + j][None, :] for j in range(n_sub)]

    main = dict(mla_w_in=mla_w_in, mla_w_qb=mla_w_qb, mla_w_kvb=mla_w_kvb, mla_w_o=mla_w_o, hgrn_w_in=hgrn_w_in,
                hgrn_w_o=hgrn_w_o, ffn_w_in=ffn_w_in, ffn_w_out=ffn_w_out)
    names = list(main)

    def group_kinds(layer, part):
        if part:
            return [("ffn_w_in", layer), ("ffn_w_out", layer)]
        mixer = ["mla_w_in", "mla_w_qb", "mla_w_kvb", "mla_w_o"] if layer % 2 == 0 else ["hgrn_w_in", "hgrn_w_o"]
        return [(k, layer // 2) for k in mixer]

    gathers = {}
    after = mod_all[0, 0, :8, :LANES] + ln_all[0, :8, :LANES]
    for layer in range(depth):
        for part in range(2):
            lands = [lax.dynamic_update_index_in_dim(lax.empty((4,) + main[k].shape[1:], BF16), main[k][i].astype(BF16), k_me, 0)
                     for k, i in group_kinds(layer, part)]
            ssem, rsem, lands, after = _gather_start(lands, after, f"gather_start_l{layer}p{part}")
            gathers[layer, part] = (ssem, rsem, lands)
    scale[0] = scale[0] + after[0, 0]

    def row_w(g):
        return g.reshape(1, g.shape[0] * g.shape[1], g.shape[2])

    def full_w_in(g):
        return jnp.transpose(g, (1, 0, 2)).reshape(1, g.shape[1], 4 * g.shape[2])

    ang = positions.astype(F32)[..., None] * (ROPE_THETA ** (-jnp.arange(0, QK_ROPE, 2, dtype=F32) / QK_ROPE))
    cos, sin = jnp.cos(ang), jnp.sin(ang)

    gq = [mla_q_norm[j][None, :] for j in range(n_mla)]
    gkv = [mla_kv_norm[j][None, :] for j in range(n_mla)]
    gn = [hgrn_g_norm[j][None, :] for j in range(n_hgrn)]

    def r2(a):
        return a.reshape(T, a.shape[-1])

    def r3(a):
        return a.reshape(B, S, a.shape[-1])

    saved = []
    xs = x
    for layer in range(depth):
        j = layer // 2
        sub = 2 * layer
        tag = f"l{layer}"
        ssem, rsem, lands = gathers[layer, 0]
        lands = _gather_wait(ssem, rsem, lands, xs if layer else scale[0], f"gather_wait_{tag}p0")
        wl = {k: g for (k, _), g in zip(group_kinds(layer, 0), lands)}
        if layer == 0:
            h = _modulate(xs, scale[sub], shift[sub], f"mod_{tag}a")
        if layer % 2 == 0:
            wl["mla_w_in"] = full_w_in(wl["mla_w_in"])
            proj = r3(_mm_nn(r2(h), wl["mla_w_in"], F32, f"mla_in_{tag}"))
            qn, kvn = _mla_mid_fwd(proj, gq[j], gkv[j], f"mla_mid_{tag}")
            q = r3(_mm_nn(r2(qn), wl["mla_w_qb"], F32, f"mla_qb_{tag}"))
            kv = r3(_mm_nn(r2(kvn), wl["mla_w_kvb"], F32, f"mla_kvb_{tag}"))
            qh, kh, vh = _mla_prep_fwd(q, kv, proj, cos, sin, f"mla_prep_{tag}")
            o, lse = _attn_fwd(qh, kh, vh, f"attn_{tag}")
            wl["mla_w_o"] = row_w(wl["mla_w_o"])
            y = r3(_mm_nn(r2(o), wl["mla_w_o"], F32, f"mla_o_{tag}"))
            mix = (h, proj, qn, kvn, qh, kh, vh, o, lse)
        else:
            proj = r3(_mm_nn(r2(h), wl["hgrn_w_in"], F32, f"hgrn_in_{tag}"))
            og, o_pre, states = _hgrn_fwd(proj, lower_bounds[j][None, :], gn[j], f"hgrn_{tag}")
            wl["hgrn_w_o"] = row_w(wl["hgrn_w_o"])
            y = r3(_mm_nn(r2(og), wl["hgrn_w_o"], F32, f"hgrn_o_{tag}"))
            mix = (h, proj, og, o_pre, states)
        x1, h2 = _ln_mod_fwd(alpha, xs, y, gate[sub], lng[sub], lnb[sub], scale[sub + 1], shift[sub + 1], f"ln_{tag}a")
        ssem, rsem, lands = gathers[layer, 1]
        lands = _gather_wait(ssem, rsem, lands, x1, f"gather_wait_{tag}p1")
        wl.update({k: g for (k, _), g in zip(group_kinds(layer, 1), lands)})
        a, ug, uu = [r3(t_) for t_ in _ffn_in(r2(h2), wl["ffn_w_in"], f"ffn_in_{tag}")]
        wl["ffn_w_out"] = row_w(wl["ffn_w_out"])
        y2 = r3(_mm_nn(r2(a), wl["ffn_w_out"], F32, f"ffn_out_{tag}"))
        if layer + 1 < depth:
            x2, h_next = _ln_mod_fwd(alpha, x1, y2, gate[sub + 1], lng[sub + 1], lnb[sub + 1], scale[sub + 2], shift[sub + 2], f"ln_{tag}b")
        else:
            x2, h_next = _ln_fwd(alpha, x1, y2, gate[sub + 1], lng[sub + 1], lnb[sub + 1], f"ln_{tag}b"), None
        saved.append((xs, y, x1, y2, mix, h2, ug, uu, a, wl))
        xs, h = x2, h_next

    loss_local, dout = _loss_head(xs, loss_target, "loss_head")
    loss = lax.psum(loss_local, ("x", "y", "c"))

    gw = {k: [None] * main[k].shape[0] for k in names}
    land = {k: lax.empty((3,) + main[k].shape, BF16) for k in names}
    scatters = []
    d_shift, d_scale, d_gate = [None] * n_sub, [None] * n_sub, [None] * n_sub
    d_lng, d_lnb = [None] * n_sub, [None] * n_sub
    d_gq, d_gkv, d_gn, d_lbnd = [None] * n_mla, [None] * n_mla, [None] * n_hgrn, [None] * n_hgrn

    def rows4(g):
        return g.reshape(4, g.shape[1] // 4, g.shape[2])

    def scatter_kinds(layer, part):
        if part == 1 or layer % 2:
            return group_kinds(layer, part)
        mixer = group_kinds(layer, 0)
        return mixer[:1] if part == 0 else mixer[1:]

    def start_scatter(layer, part, params, at):
        kinds = scatter_kinds(layer, part)
        ssem, rsem, slabs_t, lands_t, token = _scatter_start(
            [gw[k][i] for k, i in kinds], [land[k] for k, _ in kinds], [i for _, i in kinds], f"scatter_start_l{layer}p{part}")
        for (k, i), s_t, l_t in zip(kinds, slabs_t, lands_t):
            gw[k][i], land[k] = s_t, l_t
        scatters.append((layer, part, ssem, rsem))
        if params is not None:
            params[at] = params[at] + token[0, 0]

    for layer in reversed(range(depth)):
        j = layer // 2
        sub = 2 * layer
        tag = f"l{layer}"
        xs, y, x1, y2, mix, h2, ug, uu, a, wl = saved[layer]
        if layer + 1 == depth:
            dxr, dy2, d_gate[sub + 1], d_lng[sub + 1], d_lnb[sub + 1] = _ln_bwd(
                alpha, dout, x1, y2, gate[sub + 1], lng[sub + 1], lnb[sub + 1], f"ln_bwd_{tag}b")
        else:
            dxr, dy2, d_gate[sub + 1], d_lng[sub + 1], d_lnb[sub + 1], d_scale[sub + 2], d_shift[sub + 2] = _ln_mod_bwd(
                alpha, dh, dxr, scale[sub + 2], x1, y2, gate[sub + 1], lng[sub + 1], lnb[sub + 1], f"ln_bwd_{tag}b")
        da = r3(_mm_nt(r2(dy2), wl["ffn_w_out"], F32, f"ffn_out_dx_{tag}"))
        gw["ffn_w_out"][layer] = rows4(_mm_tn(r2(a), r2(dy2), 1, BF16, f"ffn_out_dw_{tag}"))
        du = _swiglu_bwd(ug, uu, da, f"swiglu_bwd_{tag}")
        dh2 = r3(_mm_nt(r2(du), wl["ffn_w_in"], F32, f"ffn_in_dx_{tag}"))
        gw["ffn_w_in"][layer] = _mm_tn(r2(h2), r2(du), 4, BF16, f"ffn_in_dw_{tag}")
        start_scatter(layer, 1, gate, sub)
        dxr, dy, d_gate[sub], d_lng[sub], d_lnb[sub], d_scale[sub + 1], d_shift[sub + 1] = _ln_mod_bwd(
            alpha, dh2, dxr, scale[sub + 1], xs, y, gate[sub], lng[sub], lnb[sub], f"ln_bwd_{tag}a")
        if layer % 2 == 0:
            h, proj, qn, kvn, qh, kh, vh, o, lse = mix
            do = r3(_mm_nt(r2(dy), wl["mla_w_o"], BF16, f"mla_o_dx_{tag}"))
            gw["mla_w_o"][j] = rows4(_mm_tn(r2(o), r2(dy), 1, BF16, f"mla_o_dw_{tag}"))
            dqh, dkh, dvh = _attn_bwd(qh, kh, vh, o, do, lse, f"attn_bwd_{tag}")
            dq_, dkv_, dkr = _mla_prep_bwd(dqh, dkh, dvh, cos, sin, f"mla_prep_bwd_{tag}")
            dqn = r3(_mm_nt(r2(dq_), wl["mla_w_qb"], F32, f"mla_qb_dx_{tag}"))
            gw["mla_w_qb"][j] = _mm_tn(r2(qn), r2(dq_), 4, BF16, f"mla_qb_dw_{tag}")
            dkvn = r3(_mm_nt(r2(dkv_), wl["mla_w_kvb"], F32, f"mla_kvb_dx_{tag}"))
            gw["mla_w_kvb"][j] = _mm_tn(r2(kvn), r2(dkv_), 4, BF16, f"mla_kvb_dw_{tag}")
            start_scatter(layer, 2, gq, j)
            dproj, dgq_, dgkv_ = _mla_mid_bwd(proj, dqn, dkvn, dkr, gq[j], gkv[j], f"mla_mid_bwd_{tag}")
            d_gq[j], d_gkv[j] = dgq_.sum(0), dgkv_.sum(0)
            dh = r3(_mm_nt(r2(dproj), wl["mla_w_in"], F32, f"mla_in_dx_{tag}"))
            gwin = _mm_tn(r2(h), r2(dproj), 1, BF16, f"mla_in_dw_{tag}")[0]
            gw["mla_w_in"][j] = jnp.transpose(gwin.reshape(gwin.shape[0], 4, gwin.shape[1] // 4), (1, 0, 2))
        else:
            h, proj, og, o_pre, states = mix
            dog = r3(_mm_nt(r2(dy), wl["hgrn_w_o"], F32, f"hgrn_o_dx_{tag}"))
            gw["hgrn_w_o"][j] = rows4(_mm_tn(r2(og), r2(dy), 1, BF16, f"hgrn_o_dw_{tag}"))
            dq_, df_, di_, dg_, dlb_, dgn_ = _hgrn_bwd(proj, lower_bounds[j][None, :], gn[j], o_pre, states, dog, f"hgrn_bwd_{tag}")
            dproj = jnp.concatenate([dq_, df_, di_, dg_], axis=-1)
            d_lbnd[j] = dlb_.sum(0).reshape(1, HK)
            d_gn[j] = dgn_.sum((0, 1))
            dh = r3(_mm_nt(r2(dproj), wl["hgrn_w_in"], F32, f"hgrn_in_dx_{tag}"))
            gw["hgrn_w_in"][j] = _mm_tn(r2(h), r2(dproj), 4, BF16, f"hgrn_in_dw_{tag}")
        start_scatter(layer, 0, gate if layer else None, sub - 1)
    grad_x, d_scale[0], d_shift[0] = _mod_bwd(dh, dxr, x, scale[0], "mod_bwd_l0a")

    for layer, part, ssem, rsem in scatters:
        kinds = scatter_kinds(layer, part)
        slabs_t, lands_t = _scatter_wait(
            ssem, rsem, [gw[k][i] for k, i in kinds], [land[k] for k, _ in kinds], [i for _, i in kinds], grad_x,
            f"scatter_wait_l{layer}p{part}")
        for (k, i), s_t, l_t in zip(kinds, slabs_t, lands_t):
            gw[k][i], land[k] = s_t, l_t
    sums = [_sum4(jnp.stack([lax.dynamic_index_in_dim(g, k_me, 0, keepdims=False) for g in gw[k]]), land[k], f"sum4_{k}")
            for k in names]

    dmod = jnp.stack([jnp.concatenate([d_shift[s_][:, 0], d_scale[s_][:, 0], d_gate[s_][:, 0]], axis=-1) for s_ in range(n_sub)])
    dmod_rows = _pad_rows(dmod.reshape(n_sub * B, 3 * D), -(-n_sub * B // 8) * 8)
    dmod_all = _allgather8(dmod_rows, "gather_dmod")[:, : n_sub * B].reshape(8, n_sub, B, 3 * D)
    dmod_all = jnp.transpose(dmod_all, (1, 0, 2, 3)).reshape(n_sub, Bg, 3 * D)
    dmod_mine = lax.dynamic_slice_in_dim(dmod_all, k_me * n_ada, n_ada, axis=2)
    g_ada_w, g_ada_b = _ada_bwd(c_all, dmod_mine, "ada_bwd")
    g_ada_w = g_ada_w.reshape(ada_w.shape)
    g_ada_b = g_ada_b.reshape(ada_b.shape)

    small = [jnp.stack(d_gq).reshape(-1), jnp.stack(d_gkv).reshape(-1), jnp.stack(d_gn).reshape(-1),
             jnp.stack(d_lbnd).reshape(-1), jnp.stack([d.sum(0) for d in d_lng]).reshape(-1),
             jnp.stack([d.sum(0) for d in d_lnb]).reshape(-1)]
    sizes = [s_.shape[0] for s_ in small]
    flat = jnp.concatenate(small)
    rows_small = -(-flat.shape[0] // (8 * LANES)) * 8
    flat = jnp.pad(flat, (0, rows_small * LANES - flat.shape[0])).reshape(rows_small, LANES)
    tot = _allgather8(flat, "gather_small")
    acc = tot[0]
    for d in range(1, 8):
        acc = acc + tot[d]
    acc = acc.reshape(-1)
    offs = [0]
    for s_ in sizes:
        offs.append(offs[-1] + s_)
    g_q_norm = acc[offs[0] : offs[1]].reshape(mla_q_norm.shape)
    g_kv_norm = acc[offs[1] : offs[2]].reshape(mla_kv_norm.shape)
    g_g_norm = acc[offs[2] : offs[3]].reshape(hgrn_g_norm.shape)
    g_lbnd = acc[offs[3] : offs[4]].reshape(n_hgrn, HK)
    g_lb_full = lower_bounds_vjp(g_lbnd)[0]
    g_hgrn_lb = lax.dynamic_slice_in_dim(g_lb_full, k_me * lbw, lbw, axis=1)
    g_lng = lax.dynamic_slice_in_dim(acc[offs[4] : offs[5]].reshape(n_sub, D), k_me * dq, dq, axis=1).reshape(ln_g.shape)
    g_lnb = lax.dynamic_slice_in_dim(acc[offs[5] : offs[6]].reshape(n_sub, D), k_me * dq, dq, axis=1).reshape(ln_b.shape)

    weights = dict(mla_w_in=mla_w_in, mla_q_norm=mla_q_norm, mla_w_qb=mla_w_qb, mla_kv_norm=mla_kv_norm, mla_w_kvb=mla_w_kvb,
                   mla_w_o=mla_w_o, hgrn_lb=hgrn_lb, hgrn_w_in=hgrn_w_in, hgrn_g_norm=hgrn_g_norm, hgrn_w_o=hgrn_w_o,
                   ffn_w_in=ffn_w_in, ffn_w_out=ffn_w_out, ada_w=ada_w, ada_b=ada_b, ln_g=ln_g, ln_b=ln_b)
    moms = dict(mla_w_in=(m_mla_w_in, v_mla_w_in), mla_q_norm=(m_mla_q_norm, v_mla_q_norm), mla_w_qb=(m_mla_w_qb, v_mla_w_qb),
                mla_kv_norm=(m_mla_kv_norm, v_mla_kv_norm), mla_w_kvb=(m_mla_w_kvb, v_mla_w_kvb), mla_w_o=(m_mla_w_o, v_mla_w_o),
                hgrn_lb=(m_hgrn_lb, v_hgrn_lb), hgrn_w_in=(m_hgrn_w_in, v_hgrn_w_in), hgrn_g_norm=(m_hgrn_g_norm, v_hgrn_g_norm),
                hgrn_w_o=(m_hgrn_w_o, v_hgrn_w_o), ffn_w_in=(m_ffn_w_in, v_ffn_w_in), ffn_w_out=(m_ffn_w_out, v_ffn_w_out),
                ada_w=(m_ada_w, v_ada_w), ada_b=(m_ada_b, v_ada_b), ln_g=(m_ln_g, v_ln_g), ln_b=(m_ln_b, v_ln_b))
    grads = dict(mla_q_norm=(g_q_norm,), mla_kv_norm=(g_kv_norm,), hgrn_lb=(g_hgrn_lb,), hgrn_g_norm=(g_g_norm,),
                 ada_w=(g_ada_w,), ada_b=(g_ada_b,), ln_g=(g_lng,), ln_b=(g_lnb,))

    def adamw(k):
        return _adamw(weights[k], [g_.reshape(weights[k].shape) for g_ in grads[k]], moms[k][0], moms[k][1], f"adamw_{k}")

    ssem, rsem, sums, others = _swap_start(sums, tot[0, :8] + dmod_all[0, :8, :LANES], "swap_start")
    res = {k: adamw(k) for k in grads}
    sums, others = _swap_wait(ssem, rsem, sums, others, res["ada_w"][1], "swap_wait")
    grads.update({k: (a_, b_) for k, a_, b_ in zip(names, sums, others)})
    res.update({k: adamw(k) for k in names})
    order = list(weights)
    return (loss, grad_x, *[res[k][0] for k in order], *[res[k][1] for k in order], *[res[k][2] for k in order],
            *[res[k][3] for k in order])
```

```python
import functools

import jax
import jax.numpy as jnp
from jax import lax
from jax.experimental import pallas as pl
from jax.experimental.pallas import tpu as pltpu

F32 = jnp.float32
BF16 = jnp.bfloat16
SDS = jax.ShapeDtypeStruct
MESH = pl.DeviceIdType.MESH
HI = lax.Precision.HIGHEST
MID = lax.Precision.HIGH

MLA_HEADS, QK_NOPE, QK_ROPE, V_HEAD = 16, 64, 32, 64
Q_LORA, KV_LORA = 768, 256
QK_DIM = QK_NOPE + QK_ROPE
ROPE_THETA = 10000.0
HGRN_K = 128
HGRN_CHUNK = 128
HGRN_SUB = 32
HGRN_PAR = 2
LN_EPS, RMS_EPS = 1e-5, 1e-6
ADAM_LR, ADAM_B1, ADAM_B2, ADAM_EPS, ADAM_WD, ADAM_STEP = 0.001, 0.9, 0.999, 1e-08, 0.01, 10
NEG = -1e30

VMEM_LIMIT_BYTES = 56 * 1024 * 1024
RESIDENT_WEIGHT_BYTES = 12 * 1024 * 1024
LANES = 128
SUBLANES = 8


def _cparams(*sem):
    return pltpu.CompilerParams(dimension_semantics=sem if sem else None, vmem_limit_bytes=VMEM_LIMIT_BYTES)


def _pick_tile(n, cap):
    best = 0
    for t in range(LANES, min(n, cap) + 1, LANES):
        if n % t == 0:
            best = t
    return best if best else n


def _bdot(a, b):
    return jnp.dot(a.astype(BF16), b.astype(BF16), preferred_element_type=F32)


def _bdot_nt(a, b):
    return lax.dot_general(a.astype(BF16), b.astype(BF16), (((1,), (1,)), ((), ())), preferred_element_type=F32)


def _bdot_tn(a, b):
    return lax.dot_general(a.astype(BF16), b.astype(BF16), (((0,), (0,)), ((), ())), preferred_element_type=F32)


def _hdot(a, b):
    return jnp.dot(a, b, precision=HI, preferred_element_type=F32)


def _mdot(a, b):
    return jnp.dot(a, b, precision=MID, preferred_element_type=F32)


def _mdot_nt(a, b):
    return lax.dot_general(a, b, (((1,), (1,)), ((), ())), precision=MID, preferred_element_type=F32)


def _mdot_tn(a, b):
    return lax.dot_general(a, b, (((0,), (0,)), ((), ())), precision=MID, preferred_element_type=F32)


def _mm_nn(a, w, out_dtype, name):
    M, K = a.shape
    G, _, n = w.shape
    tm = min(512, M)
    tn = _pick_tile(n, 1536)
    nps = n // tn

    if G > 1 and w.size * 2 <= RESIDENT_WEIGHT_BYTES and n % LANES == 0:
        def body_all(a_ref, w_ref, o_ref):
            av = a_ref[...]
            for s in range(G):
                o_ref[:, s * n : (s + 1) * n] = _bdot(av, w_ref[s]).astype(o_ref.dtype)

        return pl.pallas_call(
            body_all,
            grid=(M // tm,),
            in_specs=[pl.BlockSpec((tm, K), lambda i: (i, 0)), pl.BlockSpec((G, K, n), lambda i: (0, 0, 0))],
            out_specs=pl.BlockSpec((tm, G * n), lambda i: (i, 0)),
            out_shape=SDS((M, G * n), out_dtype),
            name=name,
            compiler_params=_cparams("parallel"),
        )(a, w)

    def body(a_ref, w_ref, o_ref):
        o_ref[...] = _bdot(a_ref[...], w_ref[...]).astype(o_ref.dtype)

    return pl.pallas_call(
        body,
        grid=(G * nps, M // tm),
        in_specs=[
            pl.BlockSpec((tm, K), lambda j, i: (i, 0)),
            pl.BlockSpec((None, K, tn), lambda j, i: (j // nps, 0, j % nps)),
        ],
        out_specs=pl.BlockSpec((tm, tn), lambda j, i: (i, j)),
        out_shape=SDS((M, G * n), out_dtype),
        name=name,
        compiler_params=_cparams("parallel", "parallel"),
    )(a, w)


def _mm_nt(a, w, out_dtype, name):
    M = a.shape[0]
    G, K, n = w.shape
    tm = min(512, M)
    tk = _pick_tile(K, 1536)

    if w.size * 2 <= RESIDENT_WEIGHT_BYTES:
        def body_all(a_ref, w_ref, o_ref):
            acc = _bdot_nt(a_ref[:, :n], w_ref[0])
            for s in range(1, G):
                acc = acc + _bdot_nt(a_ref[:, s * n : (s + 1) * n], w_ref[s])
            o_ref[...] = acc.astype(o_ref.dtype)

        return pl.pallas_call(
            body_all,
            grid=(M // tm,),
            in_specs=[pl.BlockSpec((tm, G * n), lambda i: (i, 0)), pl.BlockSpec((G, K, n), lambda i: (0, 0, 0))],
            out_specs=pl.BlockSpec((tm, K), lambda i: (i, 0)),
            out_shape=SDS((M, K), out_dtype),
            name=name,
            compiler_params=_cparams("parallel"),
        )(a, w)

    def body(a_ref, w_ref, o_ref, acc_ref):
        s = pl.program_id(2)

        @pl.when(s == 0)
        def _():
            acc_ref[...] = jnp.zeros_like(acc_ref)

        acc_ref[...] += _bdot_nt(a_ref[...], w_ref[...])

        @pl.when(s == G - 1)
        def _():
            o_ref[...] = acc_ref[...].astype(o_ref.dtype)

    return pl.pallas_call(
        body,
        grid=(K // tk, M // tm, G),
        in_specs=[
            pl.BlockSpec((tm, n), lambda kb, i, s: (i, s)),
            pl.BlockSpec((None, tk, n), lambda kb, i, s: (s, kb, 0)),
        ],
        out_specs=pl.BlockSpec((tm, tk), lambda kb, i, s: (i, kb)),
        out_shape=SDS((M, K), out_dtype),
        scratch_shapes=[pltpu.VMEM((tm, tk), F32)],
        name=name,
        compiler_params=_cparams("parallel", "parallel", "arbitrary"),
    )(a, w)


def _mm_tn(a, d, G, out_dtype, name):
    T, K = a.shape
    n = d.shape[1] // G
    tk = _pick_tile(K, 512)
    tn = _pick_tile(n, 1536)
    nps = n // tn

    def body(a_ref, d_ref, o_ref):
        o_ref[...] = _bdot_tn(a_ref[...], d_ref[...]).astype(o_ref.dtype)

    return pl.pallas_call(
        body,
        grid=(G * nps, K // tk),
        in_specs=[
            pl.BlockSpec((T, tk), lambda j, i: (0, i)),
            pl.BlockSpec((T, tn), lambda j, i: (0, j)),
        ],
        out_specs=pl.BlockSpec((None, tk, tn), lambda j, i: (j // nps, i, j % nps)),
        out_shape=SDS((G, K, n), out_dtype),
        name=name,
        compiler_params=_cparams("parallel", "parallel"),
    )(a, d)


def _rows_call(body, name, B, S, ins, outs, ts=512):
    ts = min(ts, S)
    in_specs, args = [], []
    for arr, kind in ins:
        W = arr.shape[-1]
        if kind == "row":
            in_specs.append(pl.BlockSpec((None, ts, W), lambda b, s: (b, s, 0)))
        elif kind == "ex":
            in_specs.append(pl.BlockSpec((None, 1, W), lambda b, s: (b, 0, 0)))
        else:
            in_specs.append(pl.BlockSpec((1, W), lambda b, s: (0, 0)))
        args.append(arr)
    out_specs, out_shape = [], []
    for W, dt, kind in outs:
        if kind == "row":
            out_specs.append(pl.BlockSpec((None, ts, W), lambda b, s: (b, s, 0)))
            out_shape.append(SDS((B, S, W), dt))
        else:
            out_specs.append(pl.BlockSpec((None, 1, W), lambda b, s: (b, 0, 0)))
            out_shape.append(SDS((B, 1, W), dt))
    return pl.pallas_call(
        body,
        grid=(B, S // ts),
        in_specs=in_specs,
        out_specs=out_specs,
        out_shape=out_shape,
        name=name,
        compiler_params=_cparams("parallel", "arbitrary"),
    )(*args)


def _acc(ref, val):
    @pl.when(pl.program_id(1) == 0)
    def _():
        ref[...] = jnp.zeros_like(ref)

    ref[...] += val


def _mod_fn(x, sc, sh):
    return x * (1.0 + sc) + sh


def _ln_fn(alpha, x, y, gate, g, b):
    z = alpha * x + (1.0 + gate) * y
    mu = jnp.mean(z, -1, keepdims=True)
    var = jnp.mean(jnp.square(z - mu), -1, keepdims=True)
    return (z - mu) * lax.rsqrt(var + LN_EPS) * g + b


def _modulate(x, sc, sh, name):
    B, S, D = x.shape

    def body(x_ref, sc_ref, sh_ref, h_ref):
        h_ref[...] = _mod_fn(x_ref[...], sc_ref[...], sh_ref[...]).astype(BF16)

    return _rows_call(body, name, B, S, [(x, "row"), (sc, "ex"), (sh, "ex")], [(D, BF16, "row")])[0]


def _ln_fwd(alpha, x, y, gate, g, b, name):
    B, S, D = x.shape

    def body(x_ref, y_ref, gate_ref, g_ref, b_ref, o_ref):
        o_ref[...] = _ln_fn(alpha, x_ref[...], y_ref[...], gate_ref[...], g_ref[...], b_ref[...])

    return _rows_call(
        body, name, B, S, [(x, "row"), (y, "row"), (gate, "ex"), (g, "par"), (b, "par")], [(D, F32, "row")]
    )[0]


def _ln_mod_fwd(alpha, x, y, gate, g, b, sc_next, sh_next, name):
    B, S, D = x.shape

    def body(x_ref, y_ref, gate_ref, g_ref, b_ref, sc_ref, sh_ref, o_ref, h_ref):
        out = _ln_fn(alpha, x_ref[...], y_ref[...], gate_ref[...], g_ref[...], b_ref[...])
        o_ref[...] = out
        h_ref[...] = _mod_fn(out, sc_ref[...], sh_ref[...]).astype(BF16)

    return _rows_call(
        body, name, B, S,
        [(x, "row"), (y, "row"), (gate, "ex"), (g, "par"), (b, "par"), (sc_next, "ex"), (sh_next, "ex")],
        [(D, F32, "row"), (D, BF16, "row")],
    )


def _ln_mod_bwd(alpha, dh, dxr_next, sc_next, x, y, gate, g, b, name):
    B, S, D = x.shape

    def body(dh_ref, dxr_ref, sc_ref, x_ref, y_ref, gate_ref, g_ref, b_ref,
             dx_ref, dy_ref, dgate_ref, dg_ref, db_ref, dsc_ref, dsh_ref):
        out, vjp = jax.vjp(
            functools.partial(_ln_fn, alpha), x_ref[...], y_ref[...], gate_ref[...], g_ref[...], b_ref[...]
        )
        dh_v = dh_ref[...]
        dx, dy, dgate, dg, db = vjp(dxr_ref[...] + dh_v * (1.0 + sc_ref[...]))
        dx_ref[...] = dx
        dy_ref[...] = dy.astype(BF16)
        _acc(dgate_ref, dgate)
        _acc(dg_ref, dg)
        _acc(db_ref, db)
        _acc(dsc_ref, jnp.sum(dh_v * out, axis=0, keepdims=True))
        _acc(dsh_ref, jnp.sum(dh_v, axis=0, keepdims=True))

    return _rows_call(
        body, name, B, S,
        [(dh, "row"), (dxr_next, "row"), (sc_next, "ex"), (x, "row"), (y, "row"), (gate, "ex"), (g, "par"), (b, "par")],
        [(D, F32, "row"), (D, BF16, "row")] + [(D, F32, "acc")] * 5,
    )


def _ln_bwd(alpha, dout, x, y, gate, g, b, name):
    B, S, D = x.shape

    def body(do_ref, x_ref, y_ref, gate_ref, g_ref, b_ref, dxr_ref, dy_ref, dgate_ref, dg_ref, db_ref):
        _, vjp = jax.vjp(
            functools.partial(_ln_fn, alpha), x_ref[...], y_ref[...], gate_ref[...], g_ref[...], b_ref[...]
        )
        dx, dy, dgate, dg, db = vjp(do_ref[...])
        dxr_ref[...] = dx
        dy_ref[...] = dy.astype(BF16)
        _acc(dgate_ref, dgate)
        _acc(dg_ref, dg)
        _acc(db_ref, db)

    return _rows_call(
        body,
        name,
        B,
        S,
        [(dout, "row"), (x, "row"), (y, "row"), (gate, "ex"), (g, "par"), (b, "par")],
        [(D, F32, "row"), (D, BF16, "row"), (D, F32, "acc"), (D, F32, "acc"), (D, F32, "acc")],
    )


def _mod_bwd(dh, dxr, x, sc, name):
    B, S, D = x.shape

    def body(dh_ref, dxr_ref, x_ref, sc_ref, dx_ref, dsc_ref, dsh_ref):
        dh_v = dh_ref[...]
        dx_ref[...] = dxr_ref[...] + dh_v * (1.0 + sc_ref[...])
        _acc(dsc_ref, jnp.sum(dh_v * x_ref[...], axis=0, keepdims=True))
        _acc(dsh_ref, jnp.sum(dh_v, axis=0, keepdims=True))

    return _rows_call(
        body,
        name,
        B,
        S,
        [(dh, "row"), (dxr, "row"), (x, "row"), (sc, "ex")],
        [(D, F32, "row"), (D, F32, "acc"), (D, F32, "acc")],
    )


def _loss_head(y, target, name):
    B, S, D = y.shape

    def body(y_ref, t_ref, l_ref, dy_ref):
        e = y_ref[...] - t_ref[...]
        dy_ref[...] = e * (1.0 / D)
        part = 0.5 * jnp.sum(jnp.sum(e * e, axis=1, keepdims=True) * (1.0 / D), axis=0, keepdims=True)
        _acc(l_ref, jnp.broadcast_to(part, (1, LANES)))

    loss, dy = _rows_call(
        body, name, B, S, [(y, "row"), (target, "row")], [(LANES, F32, "acc"), (D, F32, "row")]
    )
    return jnp.sum(loss[:, 0, 0]), dy


def _ffn_in(h, w, name):
    M, K = h.shape
    G, _, n = w.shape
    assert G == 4
    tm = min(512, M)
    tn = _pick_tile(n, 1536)
    nps = n // tn
    half = 2 * nps

    def body(h_ref, wg_ref, wu_ref, a_ref, g_ref, u_ref):
        hv = h_ref[...]
        g = _bdot(hv, wg_ref[...])
        u = _bdot(hv, wu_ref[...])
        a_ref[...] = (jax.nn.silu(g) * u).astype(BF16)
        g_ref[...] = g.astype(BF16)
        u_ref[...] = u.astype(BF16)

    out = pl.BlockSpec((tm, tn), lambda j, i: (i, j))
    return pl.pallas_call(
        body,
        grid=(half, M // tm),
        in_specs=[
            pl.BlockSpec((tm, K), lambda j, i: (i, 0)),
            pl.BlockSpec((None, K, tn), lambda j, i: (j // nps, 0, j % nps)),
            pl.BlockSpec((None, K, tn), lambda j, i: (2 + j // nps, 0, j % nps)),
        ],
        out_specs=[out, out, out],
        out_shape=[SDS((M, 2 * n), BF16)] * 3,
        name=name,
        compiler_params=_cparams("parallel", "parallel"),
    )(h, w, w)


def _swiglu_bwd(g, u, da, name):
    B, S, F = g.shape

    def body(g_ref, u_ref, da_ref, du_ref):
        _, vjp = jax.vjp(lambda gv, uv: jax.nn.silu(gv) * uv, g_ref[...].astype(F32), u_ref[...].astype(F32))
        dg, du = vjp(da_ref[...])
        du_ref[:, :F] = dg.astype(BF16)
        du_ref[:, F:] = du.astype(BF16)

    return _rows_call(body, name, B, S, [(g, "row"), (u, "row"), (da, "row")], [(2 * F, BF16, "row")])[0]


def _rms_fn(x, g):
    return x * lax.rsqrt(jnp.mean(jnp.square(x), -1, keepdims=True) + RMS_EPS) * g


def _mla_mid_fwd(proj, gq, gkv, name):
    B, S, _ = proj.shape

    def body(p_ref, gq_ref, gkv_ref, qn_ref, kvn_ref):
        p = p_ref[...]
        qn_ref[...] = _rms_fn(p[:, :Q_LORA], gq_ref[...]).astype(BF16)
        kvn_ref[...] = _rms_fn(p[:, Q_LORA : Q_LORA + KV_LORA], gkv_ref[...]).astype(BF16)

    return _rows_call(
        body, name, B, S, [(proj, "row"), (gq, "par"), (gkv, "par")], [(Q_LORA, BF16, "row"), (KV_LORA, BF16, "row")]
    )


def _mla_mid_bwd(proj, dqn, dkvn, dkr, gq, gkv, name):
    B, S, W = proj.shape

    def body(p_ref, dqn_ref, dkvn_ref, dkr_ref, gq_ref, gkv_ref, dp_ref, dgq_ref, dgkv_ref):
        p = p_ref[...]
        _, vq = jax.vjp(_rms_fn, p[:, :Q_LORA], gq_ref[...])
        dql, dgq = vq(dqn_ref[...])
        _, vkv = jax.vjp(_rms_fn, p[:, Q_LORA : Q_LORA + KV_LORA], gkv_ref[...])
        dkvl, dgkv = vkv(dkvn_ref[...])
        dp_ref[:, :Q_LORA] = dql.astype(BF16)
        dp_ref[:, Q_LORA : Q_LORA + KV_LORA] = dkvl.astype(BF16)
        dp_ref[:, Q_LORA + KV_LORA :] = dkr_ref[...].astype(BF16)
        _acc(dgq_ref, dgq)
        _acc(dgkv_ref, dgkv)

    return _rows_call(
        body,
        name,
        B,
        S,
        [(proj, "row"), (dqn, "row"), (dkvn, "row"), (dkr, "row"), (gq, "par"), (gkv, "par")],
        [(W, BF16, "row"), (Q_LORA, F32, "acc"), (KV_LORA, F32, "acc")],
    )


def _rope(x, cos, sin):
    h = QK_ROPE // 2
    x1, x2 = x[:, :h], x[:, h:]
    return jnp.concatenate([x1 * cos - x2 * sin, x1 * sin + x2 * cos], axis=1)


def _rope_t(dy, cos, sin):
    h = QK_ROPE // 2
    d1, d2 = dy[:, :h], dy[:, h:]
    return jnp.concatenate([d1 * cos + d2 * sin, d2 * cos - d1 * sin], axis=1)


def _heads_call(body, name, B, S, ins, outs, ts=512):
    ts = min(ts, S)
    in_specs, args = [], []
    for arr, kind in ins:
        if kind == "row":
            in_specs.append(pl.BlockSpec((None, ts, arr.shape[-1]), lambda b, s: (b, s, 0)))
        else:
            in_specs.append(pl.BlockSpec((arr.shape[0], None, ts, arr.shape[-1]), lambda b, s: (0, b, s, 0)))
        args.append(arr)
    out_specs, out_shape = [], []
    for shape, dt, kind in outs:
        if kind == "row":
            out_specs.append(pl.BlockSpec((None, ts, shape[-1]), lambda b, s: (b, s, 0)))
        else:
            out_specs.append(pl.BlockSpec((shape[0], None, ts, shape[-1]), lambda b, s: (0, b, s, 0)))
        out_shape.append(SDS(shape, dt))
    return pl.pallas_call(
        body,
        grid=(B, S // ts),
        in_specs=in_specs,
        out_specs=out_specs,
        out_shape=out_shape,
        name=name,
        compiler_params=_cparams("parallel", "parallel"),
    )(*args)


def _mla_prep_fwd(q, kv, proj, cos, sin, name):
    B, S, _ = q.shape
    H = MLA_HEADS

    def body(q_ref, kv_ref, p_ref, cos_ref, sin_ref, qh_ref, kh_ref, vh_ref):
        cos_v, sin_v = cos_ref[...], sin_ref[...]
        kr = _rope(p_ref[:, Q_LORA + KV_LORA :], cos_v, sin_v).astype(BF16)
        for h in range(H):
            qn = q_ref[:, h * QK_DIM : h * QK_DIM + QK_NOPE]
            qr = _rope(q_ref[:, h * QK_DIM + QK_NOPE : (h + 1) * QK_DIM], cos_v, sin_v)
            qh_ref[h] = jnp.concatenate([qn, qr], axis=1).astype(BF16)
            kn = kv_ref[:, h * 128 : h * 128 + QK_NOPE].astype(BF16)
            kh_ref[h] = jnp.concatenate([kn, kr], axis=1)
            vh_ref[h] = kv_ref[:, h * 128 + QK_NOPE : (h + 1) * 128].astype(BF16)

    return _heads_call(
        body,
        name,
        B,
        S,
        [(q, "row"), (kv, "row"), (proj, "row"), (cos, "row"), (sin, "row")],
        [((H, B, S, QK_DIM), BF16, "heads"), ((H, B, S, QK_DIM), BF16, "heads"), ((H, B, S, V_HEAD), BF16, "heads")],
    )


def _mla_prep_bwd(dqh, dkh, dvh, cos, sin, name):
    H, B, S, _ = dqh.shape

    def body(dqh_ref, dkh_ref, dvh_ref, cos_ref, sin_ref, dq_ref, dkv_ref, dkr_ref):
        cos_v, sin_v = cos_ref[...], sin_ref[...]
        dkr = jnp.zeros((cos_v.shape[0], QK_ROPE), F32)
        for h in range(H):
            dqv = dqh_ref[h].astype(F32)
            dq_ref[:, h * QK_DIM : h * QK_DIM + QK_NOPE] = dqv[:, :QK_NOPE].astype(BF16)
            dq_ref[:, h * QK_DIM + QK_NOPE : (h + 1) * QK_DIM] = _rope_t(dqv[:, QK_NOPE:], cos_v, sin_v).astype(BF16)
            dkv = dkh_ref[h].astype(F32)
            dkv_ref[:, h * 128 : h * 128 + QK_NOPE] = dkv[:, :QK_NOPE].astype(BF16)
            dkv_ref[:, h * 128 + QK_NOPE : (h + 1) * 128] = dvh_ref[h]
            dkr = dkr + dkv[:, QK_NOPE:]
        dkr_ref[...] = _rope_t(dkr, cos_v, sin_v)

    return _heads_call(
        body,
        name,
        B,
        S,
        [(dqh, "heads"), (dkh, "heads"), (dvh, "heads"), (cos, "row"), (sin, "row")],
        [((B, S, H * QK_DIM), BF16, "row"), ((B, S, H * 128), BF16, "row"), ((B, S, QK_ROPE), F32, "row")],
    )


LOG2E = 1.4426950408889634
ATTN_TILE = 1024
ATTN_DIAG_SUB = 512


def _attn_fwd(qh, kh, vh, name):
    H, B, S, _ = qh.shape
    t = min(ATTN_TILE, S)
    scale = QK_DIM**-0.5
    c2 = scale * LOG2E

    def body(q_ref, k_ref, v_ref, o_ref, lse_ref):
        i = pl.program_id(2)
        qs = [q_ref[0], q_ref[1]]

        def update(state, q, k, v, mask):
            m, l, acc = state
            s = _bdot_nt(q, k)
            if mask is not None:
                s = jnp.where(mask, s, NEG)
            m_new = jnp.maximum(m, jnp.max(s, axis=1, keepdims=True))
            p = jnp.exp2((s - m_new) * c2)
            a = jnp.exp2((m - m_new) * c2)
            return m_new, a * l + jnp.sum(p, axis=1, keepdims=True), a * acc + _bdot(p, v)

        def step(j, carry):
            rows = pl.ds(pl.multiple_of(j * t, t), t)
            return tuple(update(carry[hh], qs[hh], k_ref[hh, rows, :], v_ref[hh, rows, :], None) for hh in range(2))

        one = (jnp.full((t, 1), NEG, F32), jnp.zeros((t, 1), F32), jnp.zeros((t, V_HEAD), F32))
        carry = lax.fori_loop(0, i, step, (one, one))
        rows = pl.ds(pl.multiple_of(i * t, t), t)
        causal = lax.broadcasted_iota(jnp.int32, (t, t), 0) >= lax.broadcasted_iota(jnp.int32, (t, t), 1)
        carry = tuple(update(carry[hh], qs[hh], k_ref[hh, rows, :], v_ref[hh, rows, :], causal) for hh in range(2))
        outs = []
        for hh in range(2):
            m, l, acc = carry[hh]
            outs.append(acc / l)
            lse_ref[hh] = m * scale + jnp.log(l)
        o_ref[...] = jnp.concatenate(outs, axis=1).astype(BF16)

    return pl.pallas_call(
        body,
        grid=(B, H // 2, S // t),
        in_specs=[
            pl.BlockSpec((2, None, t, QK_DIM), lambda b, p, i: (p, b, i, 0)),
            pl.BlockSpec((2, None, S, QK_DIM), lambda b, p, i: (p, b, 0, 0)),
            pl.BlockSpec((2, None, S, V_HEAD), lambda b, p, i: (p, b, 0, 0)),
        ],
        out_specs=[
            pl.BlockSpec((None, t, 2 * V_HEAD), lambda b, p, i: (b, i, p)),
            pl.BlockSpec((2, None, t, 1), lambda b, p, i: (p, b, i, 0)),
        ],
        out_shape=[SDS((B, S, H * V_HEAD), BF16), SDS((H, B, S, 1), F32)],
        name=name,
        compiler_params=_cparams("parallel", "parallel", "arbitrary"),
    )(qh, kh, vh)


def _attn_bwd(qh, kh, vh, o, do, lse, name):
    H, B, S, _ = qh.shape
    t = min(ATTN_TILE, S)
    sub = min(ATTN_DIAG_SUB, t)
    nq = S // t
    scale = QK_DIM**-0.5
    c2 = scale * LOG2E

    def body(q_ref, k_ref, v_ref, o_ref, do_ref, lse_ref, dq_ref, dk_ref, dv_ref, dq_acc, delta_ref, lse2_ref):
        prod = o_ref[...].astype(F32) * do_ref[...].astype(F32)
        for hh in range(2):
            delta_ref[hh] = jnp.sum(prod[:, hh * V_HEAD : (hh + 1) * V_HEAD], axis=1, keepdims=True)
            lse2_ref[hh] = lse_ref[hh] * LOG2E
        dq_acc[...] = jnp.zeros_like(dq_acc)

        def kloop(j, _):
            krows = pl.ds(pl.multiple_of(j * t, t), t)
            ks = [k_ref[0, krows, :], k_ref[1, krows, :]]
            vs = [v_ref[0, krows, :], v_ref[1, krows, :]]

            def pair(hh, qrows, k, v, mask):
                q = q_ref[hh, qrows, :]
                do_h = do_ref[qrows, :][:, hh * V_HEAD : (hh + 1) * V_HEAD]
                p = jnp.exp2(_bdot_nt(q, k) * c2 - lse2_ref[hh, qrows, :])
                if mask is not None:
                    p = jnp.where(mask, p, 0.0)
                dv = _bdot_tn(p, do_h)
                ds = (p * (_bdot_nt(do_h, v) - delta_ref[hh, qrows, :])).astype(BF16)
                dq_acc[hh, qrows, :] += _bdot(ds, k)
                return _bdot_tn(ds, q), dv

            def qstep(i, carry):
                qrows = pl.ds(pl.multiple_of(i * t, t), t)
                out = []
                for hh in range(2):
                    dk, dv = pair(hh, qrows, ks[hh], vs[hh], None)
                    out.append((carry[hh][0] + dk, carry[hh][1] + dv))
                return tuple(out)

            def diagonal_step():
                out = []
                for hh in range(2):
                    dks, dvs = [], []
                    for c in range(t // sub):
                        r0 = c * sub
                        qrows = pl.ds(pl.multiple_of(j * t + r0, sub), t - r0)
                        mask = (lax.broadcasted_iota(jnp.int32, (t - r0, sub), 0)
                                >= lax.broadcasted_iota(jnp.int32, (t - r0, sub), 1))
                        dk, dv = pair(hh, qrows, ks[hh][r0 : r0 + sub], vs[hh][r0 : r0 + sub], mask)
                        dks.append(dk)
                        dvs.append(dv)
                    out.append((jnp.concatenate(dks, axis=0), jnp.concatenate(dvs, axis=0)))
                return tuple(out)

            carry = lax.fori_loop(j + 1, nq, qstep, diagonal_step())
            for hh in range(2):
                dk_ref[hh, krows, :] = (carry[hh][0] * scale).astype(BF16)
                dv_ref[hh, krows, :] = carry[hh][1].astype(BF16)
            return 0

        lax.fori_loop(0, nq, kloop, 0)
        dq_ref[...] = (dq_acc[...] * scale).astype(BF16)

    hspec = lambda w: pl.BlockSpec((2, None, S, w), lambda b, p: (p, b, 0, 0))
    ospec = pl.BlockSpec((None, S, 2 * V_HEAD), lambda b, p: (b, 0, p))
    return pl.pallas_call(
        body,
        grid=(B, H // 2),
        in_specs=[hspec(QK_DIM), hspec(QK_DIM), hspec(V_HEAD), ospec, ospec, hspec(1)],
        out_specs=[hspec(QK_DIM), hspec(QK_DIM), hspec(V_HEAD)],
        out_shape=[SDS((H, B, S, QK_DIM), BF16), SDS((H, B, S, QK_DIM), BF16), SDS((H, B, S, V_HEAD), BF16)],
        scratch_shapes=[pltpu.VMEM((2, S, QK_DIM), F32), pltpu.VMEM((2, S, 1), F32), pltpu.VMEM((2, S, 1), F32)],
        name=name,
        compiler_params=_cparams("parallel", "parallel"),
    )(qh, kh, vh, o, do, lse)


def _hgrn_pre(q, fx, lb):
    sig = jax.nn.sigmoid(fx)
    f = lb + (1.0 - lb) * sig
    return jax.nn.silu(q), 1.0 - f, jnp.log(f)


def _hgrn_gate(o, gg, gn):
    return _rms_fn(o, gn) * jax.nn.silu(gg)


def _tri(n, lower):
    r = lax.broadcasted_iota(jnp.int32, (n, n), 0)
    c = lax.broadcasted_iota(jnp.int32, (n, n), 1)
    return ((r >= c) if lower else (r <= c)).astype(F32)


def _hgrn_intra_fwd(qs, k, v, b):
    C, SB = qs.shape[0], min(HGRN_SUB, qs.shape[0])
    ridx = lax.broadcasted_iota(jnp.int32, (SUBLANES, 1), 0)
    outs = []
    for i in range(C // SB):
        r0 = i * SB
        qi, ki, vi, bi = qs[r0 : r0 + SB], k[r0 : r0 + SB], v[r0 : r0 + SB], b[r0 : r0 + SB]
        ng = SB // SUBLANES
        qg = [qi[g * SUBLANES : (g + 1) * SUBLANES] for g in range(ng)]
        bg = [bi[g * SUBLANES : (g + 1) * SUBLANES] for g in range(ng)]
        accg = [jnp.zeros((SUBLANES, v.shape[1]), F32) for _ in range(ng)]
        for s in range(SB):
            gs, so = divmod(s, SUBLANES)
            k_s, v_s, b_s = ki[s : s + 1], vi[s : s + 1], bi[s : s + 1]
            for tg in range(gs, ng):
                if tg == gs:
                    mask = ridx >= so
                    w = jnp.where(mask, qg[tg] * k_s * jnp.exp(jnp.where(mask, bg[tg] - b_s, 0.0)), 0.0)
                else:
                    w = qg[tg] * k_s * jnp.exp(bg[tg] - b_s)
                accg[tg] = accg[tg] + jnp.sum(w, axis=1, keepdims=True) * v_s
        acc = jnp.concatenate(accg, axis=0)
        if i > 0:
            ref = bi[0:1]
            qt = qi * jnp.exp(bi - ref)
            kt = k[:r0] * jnp.exp(ref - b[:r0])
            acc = acc + _bdot(_mdot_nt(qt, kt), v[:r0])
        outs.append(acc)
    return jnp.concatenate(outs, axis=0)


def _hgrn_intra_bwd(qs, k, v, b, do):
    C, SB = qs.shape[0], min(HGRN_SUB, qs.shape[0])
    nb = C // SB
    ridx = lax.broadcasted_iota(jnp.int32, (SUBLANES, 1), 0)
    dq_p = [None] * nb
    dk_p = [jnp.zeros((SB, k.shape[1]), F32) for _ in range(nb)]
    dv_p = [jnp.zeros((SB, v.shape[1]), F32) for _ in range(nb)]
    for i in range(nb):
        r0 = i * SB
        qi, ki, vi, bi, doi = qs[r0 : r0 + SB], k[r0 : r0 + SB], v[r0 : r0 + SB], b[r0 : r0 + SB], do[r0 : r0 + SB]
        ng = SB // SUBLANES
        qg = [qi[g * SUBLANES : (g + 1) * SUBLANES] for g in range(ng)]
        bg = [bi[g * SUBLANES : (g + 1) * SUBLANES] for g in range(ng)]
        dog = [doi[g * SUBLANES : (g + 1) * SUBLANES] for g in range(ng)]
        dqg =[jnp.zeros((SUBLANES, k.shape[1]), F32) for _ in range(ng)]
        dkg = [jnp.zeros((SUBLANES, k.shape[1]), F32) for _ in range(ng)]
        dvg = [jnp.zeros((SUBLANES, v.shape[1]), F32) for _ in range(ng)]
        for s in range(SB):
            gs, so = divmod(s, SUBLANES)
            k_s, v_s, b_s = ki[s : s + 1], vi[s : s + 1], bi[s : s + 1]
            dk_s = jnp.zeros((SUBLANES, k.shape[1]), F32)
            dv_s = jnp.zeros((SUBLANES, v.shape[1]), F32)
            for tg in range(gs, ng):
                if tg == gs:
                    mask = ridx >= so
                    e = jnp.where(mask, jnp.exp(jnp.where(mask, bg[tg] - b_s, 0.0)), 0.0)
                else:
                    e = jnp.exp(bg[tg] - b_s)
                da = jnp.sum(dog[tg] * v_s, axis=1, keepdims=True)
                qe = qg[tg] * e
                a = jnp.sum(qe * k_s, axis=1, keepdims=True)
                dqg[tg] = dqg[tg] + da * (k_s * e)
                dk_s = dk_s + da * qe
                dv_s = dv_s + a * dog[tg]
            dkg[gs] = jnp.where(ridx == so, dkg[gs] + jnp.sum(dk_s, axis=0, keepdims=True), dkg[gs])
            dvg[gs] = jnp.where(ridx == so, dvg[gs] + jnp.sum(dv_s, axis=0, keepdims=True), dvg[gs])
        dqi = jnp.concatenate(dqg, axis=0)
        dki = jnp.concatenate(dkg, axis=0)
        dvi = jnp.concatenate(dvg, axis=0)
        if i > 0:
            ref = bi[0:1]
            eq = jnp.exp(bi - ref)
            ek = jnp.exp(ref - b[:r0])
            qt = qi * eq
            kt = k[:r0] * ek
            A = _mdot_nt(qt, kt)
            dA = _bdot_nt(doi, v[:r0])
            dvl = _bdot_tn(A, doi)
            dqi = dqi + _mdot(dA, kt) * eq
            dkl = _mdot_tn(dA, qt) * ek
            for j in range(i):
                dk_p[j] = dk_p[j] + dkl[j * SB : (j + 1) * SB]
                dv_p[j] = dv_p[j] + dvl[j * SB : (j + 1) * SB]
        dq_p[i] = dqi
        dk_p[i] = dk_p[i] + dki
        dv_p[i] = dv_p[i] + dvi
    return jnp.concatenate(dq_p, axis=0), jnp.concatenate(dk_p, axis=0), jnp.concatenate(dv_p, axis=0)


def _hgrn_fwd(proj, lb, gn, name):
    B, S, W = proj.shape
    HK = W // 4
    H = HK // HGRN_K
    C = min(HGRN_CHUNK, S)
    N = S // C

    HP = HGRN_PAR if H % HGRN_PAR == 0 else 1
    WP = HP * HGRN_K

    def body(q_ref, f_ref, i_ref, g_ref, lb_ref, gn_ref, og_ref, o_ref, st_ref):
        gn_v = gn_ref[...]
        tril = _tri(C, True)

        def chunk(n, sts):
            rows = pl.ds(pl.multiple_of(n * C, C), C)
            out = []
            for hh in range(HP):
                ln = slice(hh * HGRN_K, (hh + 1) * HGRN_K)
                st = sts[hh]
                qs, k, g = _hgrn_pre(q_ref[rows, ln], f_ref[rows, ln], lb_ref[:, ln])
                v = i_ref[rows, ln]
                b = _hdot(tril, g)
                st_ref[hh, n] = st
                o = _hgrn_intra_fwd(qs, k, v, b) + _bdot_nt(qs * jnp.exp(b), st)
                bl = b[C - 1 : C]
                out.append(st * jnp.exp(bl) + _bdot_tn(v, k * jnp.exp(bl - b)))
                o_ref[rows, ln] = o
                og_ref[rows, ln] = _hgrn_gate(o, g_ref[rows, ln], gn_v).astype(BF16)
            return tuple(out)

        lax.fori_loop(0, N, chunk, tuple(jnp.zeros((HGRN_K, HGRN_K), F32) for _ in range(HP)))

    col = lambda part: pl.BlockSpec((None, S, WP), lambda b, h: (b, 0, part * (H // HP) + h))
    return pl.pallas_call(
        body,
        grid=(B, H // HP),
        in_specs=[col(0), col(1), col(2), col(3), pl.BlockSpec((1, WP), lambda b, h: (0, h)), pl.BlockSpec((1, HGRN_K), lambda b, h: (0, 0))],
        out_specs=[col(0), col(0), pl.BlockSpec((None, HP, N, HGRN_K, HGRN_K), lambda b, h: (b, h, 0, 0, 0))],
        out_shape=[SDS((B, S, HK), BF16), SDS((B, S, HK), F32), SDS((B, H, N, HGRN_K, HGRN_K), F32)],
        name=name,
        compiler_params=_cparams("parallel", "parallel"),
    )(proj, proj, proj, proj, lb, gn)


def _hgrn_bwd(proj, lb, gn, o_pre, states, dog, name):
    B, S, W = proj.shape
    HK = W // 4
    H = HK // HGRN_K
    C = min(HGRN_CHUNK, S)
    N = S // C

    HP = HGRN_PAR if H % HGRN_PAR == 0 else 1
    WP = HP * HGRN_K

    def body(q_ref, f_ref, i_ref, g_ref, lb_ref, gn_ref, o_ref, st_ref, dog_ref, dq_ref, df_ref, di_ref, dg_ref, dlb_ref, dgn_ref):
        gn_v = gn_ref[...]
        tril = _tri(C, True)
        triu = _tri(C, False)

        def chunk(idx, carry):
            n = N - 1 - idx
            rows = pl.ds(pl.multiple_of(n * C, C), C)
            out = []
            for hh in range(HP):
                ln = slice(hh * HGRN_K, (hh + 1) * HGRN_K)
                dst, dlb, dgn = carry[hh]
                (qs, k, g), pre_vjp = jax.vjp(_hgrn_pre, q_ref[rows, ln], f_ref[rows, ln], lb_ref[:, ln])
                v = i_ref[rows, ln]
                _, gate_vjp = jax.vjp(_hgrn_gate, o_ref[rows, ln], g_ref[rows, ln], gn_v)
                do, dgg, dgn_c = gate_vjp(dog_ref[rows, ln])
                b = _hdot(tril, g)
                st0 = st_ref[hh, n]
                eb = jnp.exp(b)
                bl = b[C - 1 : C]
                ebl = jnp.exp(bl)
                ekb = jnp.exp(bl - b)
                qe = qs * eb
                kt = k * ekb
                dqs, dk, dv = _hgrn_intra_bwd(qs, k, v, b, do)
                dqs = dqs + _bdot(do, st0) * eb
                dk = dk + _bdot(v, dst) * ekb
                dv = dv + _bdot_nt(kt, dst)
                st1 = st0 * ebl + _bdot_tn(v, kt)
                dbl = jnp.sum(st1 * dst, axis=0, keepdims=True)
                dst = dst * ebl + _bdot_tn(do, qe)
                dgl = _hdot(triu, qs * dqs - k * dk) + dbl
                dq_pre, dfx, dlb_c = pre_vjp((dqs, dk, dgl))
                dq_ref[rows, ln] = dq_pre.astype(BF16)
                df_ref[rows, ln] = dfx.astype(BF16)
                di_ref[rows, ln] = dv.astype(BF16)
                dg_ref[rows, ln] = dgg.astype(BF16)
                out.append((dst, dlb + dlb_c, dgn + dgn_c))
            return tuple(out)

        zero = jnp.zeros((1, HGRN_K), F32)
        one = (jnp.zeros((HGRN_K, HGRN_K), F32), zero, zero)
        res = lax.fori_loop(0, N, chunk, tuple(one for _ in range(HP)))
        for hh in range(HP):
            dlb_ref[hh] = res[hh][1]
            dgn_ref[hh] = res[hh][2]

    col = lambda part: pl.BlockSpec((None, S, WP), lambda b, h: (b, 0, part * (H // HP) + h))
    vec = pl.BlockSpec((None, HP, 1, HGRN_K), lambda b, h: (b, h, 0, 0))
    return pl.pallas_call(
        body,
        grid=(B, H // HP),
        in_specs=[
            col(0), col(1), col(2), col(3),
            pl.BlockSpec((1, WP), lambda b, h: (0, h)),
            pl.BlockSpec((1, HGRN_K), lambda b, h: (0, 0)),
            col(0),
            pl.BlockSpec((None, HP, N, HGRN_K, HGRN_K), lambda b, h: (b, h, 0, 0, 0)),
            col(0),
        ],
        out_specs=[col(0), col(0), col(0), col(0), vec, vec],
        out_shape=[SDS((B, S, HK), BF16)] * 4 + [SDS((B, H, 1, HGRN_K), F32)] * 2,
        name=name,
        compiler_params=_cparams("parallel", "parallel"),
    )(proj, proj, proj, proj, lb, gn, o_pre, states, dog)


def _ada_fwd(c_all, w, b, name):
    Bg, D = c_all.shape
    L, _, n = w.shape

    def body(c_ref, w_ref, b_ref, o_ref):
        o_ref[...] = _bdot(jax.nn.silu(c_ref[...]), w_ref[...]) + b_ref[...]

    return pl.pallas_call(
        body,
        grid=(L,),
        in_specs=[
            pl.BlockSpec((Bg, D), lambda l: (0, 0)),
            pl.BlockSpec((None, D, n), lambda l: (l, 0, 0)),
            pl.BlockSpec((None, 1, n), lambda l: (l, 0, 0)),
        ],
        out_specs=pl.BlockSpec((None, Bg, n), lambda l: (l, 0, 0)),
        out_shape=SDS((L, Bg, n), F32),
        name=name,
        compiler_params=_cparams("parallel"),
    )(c_all, w, b)


def _ada_bwd(c_all, dmod, name):
    Bg, D = c_all.shape
    L, _, n = dmod.shape

    def body(c_ref, d_ref, dw_ref, db_ref):
        d = d_ref[...]
        dw_ref[...] = _bdot_tn(jax.nn.silu(c_ref[...]), d)
        db_ref[...] = jnp.sum(d, axis=0, keepdims=True)

    return pl.pallas_call(
        body,
        grid=(L,),
        in_specs=[pl.BlockSpec((Bg, D), lambda l: (0, 0)), pl.BlockSpec((None, Bg, n), lambda l: (l, 0, 0))],
        out_specs=[pl.BlockSpec((None, D, n), lambda l: (l, 0, 0)), pl.BlockSpec((None, 1, n), lambda l: (l, 0, 0))],
        out_shape=[SDS((L, D, n), F32), SDS((L, 1, n), F32)],
        name=name,
        compiler_params=_cparams("parallel"),
    )(c_all, dmod)


def _adamw(w, gs, m, v, name):
    shape = w.shape
    cols = shape[-1]
    rows = w.size // cols
    tr = rows
    for cand in (512, 256, 128, 64, 32, 16, 8):
        if rows % cand == 0 and cand * cols * 4 <= 2 * 1024 * 1024:
            tr = cand
            break
    as2d = lambda a: a.reshape(rows, cols)
    ng = len(gs)
    c1 = 1.0 / (1.0 - ADAM_B1**ADAM_STEP)
    c2 = 1.0 / (1.0 - ADAM_B2**ADAM_STEP)

    def body(*refs):
        w_ref, m_ref, v_ref = refs[0], refs[1], refs[2]
        g_refs = refs[3 : 3 + ng]
        g_out, d_out, m_out, v_out = refs[3 + ng :]
        g = g_refs[0][...].astype(F32)
        for r in g_refs[1:]:
            g = g + r[...].astype(F32)
        m_new = ADAM_B1 * m_ref[...] + (1.0 - ADAM_B1) * g
        v_new = ADAM_B2 * v_ref[...] + (1.0 - ADAM_B2) * jnp.square(g)
        g_out[...] = g
        m_out[...] = m_new
        v_out[...] = v_new
        d_out[...] = -ADAM_LR * ((m_new * c1) / (jnp.sqrt(v_new * c2) + ADAM_EPS) + ADAM_WD * w_ref[...])

    spec = pl.BlockSpec((tr, cols), lambda i: (i, 0))
    outs = pl.pallas_call(
        body,
        grid=(rows // tr,),
        in_specs=[spec] * (3 + ng),
        out_specs=[spec] * 4,
        out_shape=[SDS((rows, cols), F32)] * 4,
        name=name,
        compiler_params=_cparams("parallel"),
    )(as2d(w), as2d(m), as2d(v), *[as2d(g) for g in gs])
    return tuple(o.reshape(shape) for o in outs)


def _sum4(own, recv, name):
    shape = own.shape
    cols = shape[-1]
    rows = own.size // cols
    tr = rows
    for cand in (512, 256, 128, 64, 32, 16):
        if rows % cand == 0 and cand * cols * 4 <= 2 * 1024 * 1024:
            tr = cand
            break

    def body(own_ref, recv_ref, o_ref):
        acc = own_ref[...].astype(F32)
        for r in range(3):
            acc = acc + recv_ref[r].astype(F32)
        o_ref[...] = acc

    out = pl.pallas_call(
        body,
        grid=(rows // tr,),
        in_specs=[pl.BlockSpec((tr, cols), lambda i: (i, 0)), pl.BlockSpec((3, tr, cols), lambda i: (0, i, 0))],
        out_specs=pl.BlockSpec((tr, cols), lambda i: (i, 0)),
        out_shape=SDS((rows, cols), F32),
        name=name,
        compiler_params=_cparams("parallel"),
    )(own.reshape(rows, cols), recv.reshape(3, rows, cols))
    return out.reshape(shape)


def _my_place():
    return lax.axis_index("x"), lax.axis_index("y"), lax.axis_index("c")


def _flip(v, bit):
    return 1 - v if bit else v


def _allgather8(x, name):
    r, n = x.shape

    def body(x_ref, o_ref, send_sems, recv_sems, local_sem):
        mx, my, mc = _my_place()
        me = 4 * mx + 2 * my + mc
        mine = pltpu.make_async_copy(x_ref, o_ref.at[me], local_sem)
        mine.start()
        sends = []
        for rel in range(1, 8):
            peer = (_flip(mx, rel & 4), _flip(my, rel & 2), _flip(mc, rel & 1))
            cp = pltpu.make_async_remote_copy(
                src_ref=x_ref, dst_ref=o_ref.at[me], send_sem=send_sems.at[rel - 1], recv_sem=recv_sems.at[rel - 1],
                device_id=peer, device_id_type=MESH,
            )
            cp.start()
            sends.append(cp)
        for rel in range(1, 8):
            px, py, pc = _flip(mx, rel & 4), _flip(my, rel & 2), _flip(mc, rel & 1)
            pltpu.make_async_remote_copy(
                src_ref=x_ref, dst_ref=o_ref.at[4 * px + 2 * py + pc], send_sem=send_sems.at[rel - 1],
                recv_sem=recv_sems.at[rel - 1], device_id=(px, py, pc), device_id_type=MESH,
            ).wait_recv()
        for cp in sends:
            cp.wait_send()
        mine.wait()

    return pl.pallas_call(
        body,
        out_shape=SDS((8, r, n), x.dtype),
        in_specs=[pl.BlockSpec(memory_space=pl.ANY)],
        out_specs=pl.BlockSpec(memory_space=pl.ANY),
        scratch_shapes=[pltpu.SemaphoreType.DMA((7,)), pltpu.SemaphoreType.DMA((7,)), pltpu.SemaphoreType.DMA],
        name=name,
    )(x)


_HBM = pl.BlockSpec(memory_space=pl.ANY)


_SEM = pl.BlockSpec(memory_space=pltpu.SEMAPHORE)
_HBM_ONLY = pl.BlockSpec(memory_space=pltpu.HBM)
_EFFECT = pltpu.SideEffectType.DATAFLOW_SIDE_EFFECTING


def _in_hbm(a):
    return pltpu.with_memory_space_constraint(a, pltpu.HBM)


def _gather_start(lands, after, name):
    n = len(lands)

    def body(*refs):
        land = refs[:n]
        send_sems, recv_sems = refs[n + 1], refs[n + 2]
        token = refs[-1]
        mx, my, mc = _my_place()
        for i in range(n):
            for rel in range(1, 4):
                pltpu.make_async_remote_copy(
                    src_ref=land[i].at[2 * mx + my], dst_ref=land[i].at[2 * mx + my],
                    send_sem=send_sems.at[3 * i + rel - 1], recv_sem=recv_sems.at[3 * i + rel - 1],
                    device_id=(_flip(mx, rel & 2), _flip(my, rel & 1), mc), device_id_type=MESH,
                ).start()
        token[...] = jnp.zeros_like(token)

    outs = pl.pallas_call(
        body,
        name=name,
        out_shape=(
            pltpu.SemaphoreType.DMA((3 * n,)), pltpu.SemaphoreType.DMA((3 * n,)),
            *[pltpu.HBM(a.shape, a.dtype) for a in lands], SDS((8, LANES), F32),
        ),
        in_specs=[_HBM_ONLY] * n + [_HBM],
        out_specs=(_SEM, _SEM, *[_HBM_ONLY] * n, pl.BlockSpec(memory_space=pltpu.VMEM)),
        input_output_aliases={i: 2 + i for i in range(n)},
        compiler_params=pltpu.CompilerParams(has_side_effects=_EFFECT),
    )(*[_in_hbm(a) for a in lands], after)
    return outs[0], outs[1], list(outs[2 : 2 + n]), outs[-1]


def _gather_wait(send_sems, recv_sems, lands, after, name):
    n = len(lands)

    def body(*refs):
        land = refs[:n]
        s_sems, r_sems = refs[n], refs[n + 1]
        mx, my, mc = _my_place()
        for i in range(n):
            for rel in range(1, 4):
                px, py = _flip(mx, rel & 2), _flip(my, rel & 1)
                cp = pltpu.make_async_remote_copy(
                    src_ref=land[i].at[2 * mx + my], dst_ref=land[i].at[2 * px + py],
                    send_sem=s_sems.at[3 * i + rel - 1], recv_sem=r_sems.at[3 * i + rel - 1],
                    device_id=(px, py, mc), device_id_type=MESH,
                )
                cp.wait_send()
                cp.wait_recv()

    outs = pl.pallas_call(
        body,
        name=name,
        out_shape=tuple(pltpu.HBM(a.shape, a.dtype) for a in lands),
        in_specs=[_HBM_ONLY] * n + [_SEM, _SEM, _HBM],
        out_specs=[_HBM_ONLY] * n,
        input_output_aliases={i: i for i in range(n)},
        compiler_params=pltpu.CompilerParams(has_side_effects=_EFFECT),
    )(*lands, send_sems, recv_sems, after)
    return list(outs)


def _scatter_start(slabs, lands, places, name):
    n = len(slabs)

    def body(*refs):
        ins, land = refs[:n], refs[n : 2 * n]
        send_sems, recv_sems = refs[2 * n], refs[2 * n + 1]
        token = refs[-1]
        mx, my, mc = _my_place()
        for i in range(n):
            for rel in range(1, 4):
                px, py = _flip(mx, rel & 2), _flip(my, rel & 1)
                pltpu.make_async_remote_copy(
                    src_ref=ins[i].at[2 * px + py], dst_ref=land[i].at[rel - 1, places[i]],
                    send_sem=send_sems.at[3 * i + rel - 1], recv_sem=recv_sems.at[3 * i + rel - 1],
                    device_id=(px, py, mc), device_id_type=MESH,
                ).start()
        token[...] = jnp.zeros_like(token)

    outs = pl.pallas_call(
        body,
        name=name,
        out_shape=(
            pltpu.SemaphoreType.DMA((3 * n,)), pltpu.SemaphoreType.DMA((3 * n,)),
            *[pltpu.HBM(a.shape, a.dtype) for a in slabs], *[pltpu.HBM(a.shape, a.dtype) for a in lands],
            SDS((8, LANES), F32),
        ),
        in_specs=[_HBM_ONLY] * (2 * n),
        out_specs=(_SEM, _SEM, *[_HBM_ONLY] * (2 * n), pl.BlockSpec(memory_space=pltpu.VMEM)),
        input_output_aliases={i: 2 + i for i in range(2 * n)},
        compiler_params=pltpu.CompilerParams(has_side_effects=_EFFECT),
    )(*[_in_hbm(a) for a in slabs], *[_in_hbm(a) for a in lands])
    return outs[0], outs[1], list(outs[2 : 2 + n]), list(outs[2 + n : 2 + 2 * n]), outs[-1]


def _scatter_wait(send_sems, recv_sems, slabs, lands, places, after, name):
    n = len(slabs)

    def body(*refs):
        ins, land = refs[:n], refs[n : 2 * n]
        s_sems, r_sems = refs[2 * n], refs[2 * n + 1]
        mx, my, mc = _my_place()
        for i in range(n):
            for rel in range(1, 4):
                px, py = _flip(mx, rel & 2), _flip(my, rel & 1)
                cp = pltpu.make_async_remote_copy(
                    src_ref=ins[i].at[2 * px + py], dst_ref=land[i].at[rel - 1, places[i]],
                    send_sem=s_sems.at[3 * i + rel - 1], recv_sem=r_sems.at[3 * i + rel - 1],
                    device_id=(px, py, mc), device_id_type=MESH,
                )
                cp.wait_send()
                cp.wait_recv()

    outs = pl.pallas_call(
        body,
        name=name,
        out_shape=(*[pltpu.HBM(a.shape, a.dtype) for a in slabs], *[pltpu.HBM(a.shape, a.dtype) for a in lands]),
        in_specs=[_HBM_ONLY] * (2 * n) + [_SEM, _SEM, _HBM],
        out_specs=[_HBM_ONLY] * (2 * n),
        input_output_aliases={i: i for i in range(2 * n)},
        compiler_params=pltpu.CompilerParams(has_side_effects=_EFFECT),
    )(*slabs, *lands, send_sems, recv_sems, after)
    return list(outs[:n]), list(outs[n:])


def _swap_start(parts, after, name):
    n = len(parts)
    lands = [lax.empty(a.shape, a.dtype) for a in parts]

    def body(*refs):
        ins, land = refs[:n], refs[n : 2 * n]
        send_sems, recv_sems = refs[2 * n + 1], refs[2 * n + 2]
        mx, my, mc = _my_place()
        for i in range(n):
            pltpu.make_async_remote_copy(
                src_ref=ins[i], dst_ref=land[i], send_sem=send_sems.at[i], recv_sem=recv_sems.at[i],
                device_id=(mx, my, 1 - mc), device_id_type=MESH,
            ).start()

    outs = pl.pallas_call(
        body,
        name=name,
        out_shape=(
            pltpu.SemaphoreType.DMA((n,)), pltpu.SemaphoreType.DMA((n,)),
            *[pltpu.HBM(a.shape, a.dtype) for a in parts], *[pltpu.HBM(a.shape, a.dtype) for a in lands],
        ),
        in_specs=[_HBM_ONLY] * (2 * n) + [_HBM],
        out_specs=(_SEM, _SEM, *[_HBM_ONLY] * (2 * n)),
        input_output_aliases={i: 2 + i for i in range(2 * n)},
        compiler_params=pltpu.CompilerParams(has_side_effects=_EFFECT),
    )(*[_in_hbm(a) for a in parts], *[_in_hbm(a) for a in lands], after)
    return outs[0], outs[1], list(outs[2 : 2 + n]), list(outs[2 + n :])


def _swap_wait(send_sems, recv_sems, parts, lands, after, name):
    n = len(parts)

    def body(*refs):
        ins, land = refs[:n], refs[n : 2 * n]
        s_sems, r_sems = refs[2 * n], refs[2 * n + 1]
        mx, my, mc = _my_place()
        for i in range(n):
            cp = pltpu.make_async_remote_copy(
                src_ref=ins[i], dst_ref=land[i], send_sem=s_sems.at[i], recv_sem=r_sems.at[i],
                device_id=(mx, my, 1 - mc), device_id_type=MESH,
            )
            cp.wait_send()
            cp.wait_recv()

    outs = pl.pallas_call(
        body,
        name=name,
        out_shape=(*[pltpu.HBM(a.shape, a.dtype) for a in parts], *[pltpu.HBM(a.shape, a.dtype) for a in lands]),
        in_specs=[_HBM_ONLY] * (2 * n) + [_SEM, _SEM, _HBM],
        out_specs=[_HBM_ONLY] * (2 * n),
        input_output_aliases={i: i for i in range(2 * n)},
        compiler_params=pltpu.CompilerParams(has_side_effects=_EFFECT),
    )(*parts, *lands, send_sems, recv_sems, after)
    return list(outs[:n]), list(outs[n:])


def _pad_rows(a, rows):
    return jnp.pad(a, ((0, rows - a.shape[0]), (0, 0)))


def kernel(x, c, positions, mla_w_in, mla_q_norm, mla_w_qb, mla_kv_norm, mla_w_kvb, mla_w_o, hgrn_lb, hgrn_w_in, hgrn_g_norm, hgrn_w_o, ffn_w_in, ffn_w_out, ada_w, ada_b, ln_g, ln_b, loss_target, m_mla_w_in, m_mla_q_norm, m_mla_w_qb, m_mla_kv_norm, m_mla_w_kvb, m_mla_w_o, m_hgrn_lb, m_hgrn_w_in, m_hgrn_g_norm, m_hgrn_w_o, m_ffn_w_in, m_ffn_w_out, m_ada_w, m_ada_b, m_ln_g, m_ln_b, v_mla_w_in, v_mla_q_norm, v_mla_w_qb, v_mla_kv_norm, v_mla_w_kvb, v_mla_w_o, v_hgrn_lb, v_hgrn_w_in, v_hgrn_g_norm, v_hgrn_w_o, v_ffn_w_in, v_ffn_w_out, v_ada_w, v_ada_b, v_ln_g, v_ln_b):
    B, S, D = x.shape
    T = B * S
    depth = ada_w.shape[0]
    n_mla, n_hgrn = mla_w_in.shape[0], hgrn_w_in.shape[0]
    n_sub = 2 * depth
    alpha = (2.0 * depth) ** 0.25
    mx, my, mc = _my_place()
    me = 4 * mx + 2 * my + mc
    k_me = 2 * mx + my
    Bg = 8 * B
    HK = hgrn_w_o.shape[1] * 4
    dq = D // 4

    lbw = hgrn_lb.shape[1]
    first = jnp.zeros((8, max(D, 4 * lbw)), F32)
    first = first.at[:B, :D].set(c).at[B : B + n_hgrn, :lbw].set(hgrn_lb)
    first_all = _allgather8(first, "gather_cond")
    c_all = first_all[:, :B, :D].reshape(Bg, D)
    lb_logits = jnp.concatenate([first_all[2 * k, B : B + n_hgrn, :lbw] for k in range(4)], axis=1)

    def lower_bounds_fn(logits):
        soft = jax.nn.softmax(logits, axis=0)
        return jnp.cumsum(soft, axis=0) - soft[0]

    lower_bounds, lower_bounds_vjp = jax.vjp(lower_bounds_fn, lb_logits)

    n_ada = ada_w.shape[-1]
    mod_part = _ada_fwd(c_all, ada_w.reshape(n_sub, D, n_ada), ada_b.reshape(n_sub, 1, n_ada), "ada_fwd")
    mod_all = _allgather8(mod_part.reshape(n_sub * Bg, n_ada), "gather_mod").reshape(8, n_sub, Bg, n_ada)
    mod = jnp.concatenate([mod_all[2 * k] for k in range(4)], axis=-1)
    mod = lax.dynamic_slice_in_dim(mod, me * B, B, axis=1)
    shift = [mod[j, :, None, :D] for j in range(n_sub)]
    scale = [mod[j, :, None, D : 2 * D] for j in range(n_sub)]
    gate = [mod[j, :, None, 2 * D :] for j in range(n_sub)]

    ln_rows = 2 * n_sub
    ln_local = _pad_rows(jnp.concatenate([ln_g.reshape(n_sub, dq), ln_b.reshape(n_sub, dq)], axis=0), -(-ln_rows // 8) * 8)
    ln_pad = jnp.zeros((ln_local.shape[0], -(-dq // LANES) * LANES), F32).at[:, :dq].set(ln_local)
    ln_all = _allgather8(ln_pad, "gather_ln")
    ln_full = jnp.concatenate([ln_all[2 * k, :ln_rows, :dq] for k in range(4)], axis=1)
    lng = [ln_full[j][None, :] for j in range(n_sub)]
    lnb = [ln_full[n_sub + j][None, :] for j in range(n_sub)]

    main = dict(mla_w_in=mla_w_in, mla_w_qb=mla_w_qb, mla_w_kvb=mla_w_kvb, mla_w_o=mla_w_o, hgrn_w_in=hgrn_w_in,
                hgrn_w_o=hgrn_w_o, ffn_w_in=ffn_w_in, ffn_w_out=ffn_w_out)
    names = list(main)

    def group_kinds(layer, part):
        if part:
            return [("ffn_w_in", layer), ("ffn_w_out", layer)]
        mixer = ["mla_w_in", "mla_w_qb", "mla_w_kvb", "mla_w_o"] if layer % 2 == 0 else ["hgrn_w_in", "hgrn_w_o"]
        return [(k, layer // 2) for k in mixer]

    gathers = {}
    after = mod_all[0, 0, :8, :LANES] + ln_all[0, :8, :LANES]
    for layer in range(depth):
        for part in range(2):
            lands = [lax.dynamic_update_index_in_dim(lax.empty((4,) + main[k].shape[1:], BF16), main[k][i].astype(BF16), k_me, 0)
                     for k, i in group_kinds(layer, part)]
            ssem, rsem, lands, after = _gather_start(lands, after, f"gather_start_l{layer}p{part}")
            gathers[layer, part] = (ssem, rsem, lands)
    scale[0] = scale[0] + after[0, 0]

    def row_w(g):
        return g.reshape(1, g.shape[0] * g.shape[1], g.shape[2])

    def full_w_in(g):
        return jnp.transpose(g, (1, 0, 2)).reshape(1, g.shape[1], 4 * g.shape[2])

    ang = positions.astype(F32)[..., None] * (ROPE_THETA ** (-jnp.arange(0, QK_ROPE, 2, dtype=F32) / QK_ROPE))
    cos, sin = jnp.cos(ang), jnp.sin(ang)

    gq = [mla_q_norm[j][None, :] for j in range(n_mla)]
    gkv = [mla_kv_norm[j][None, :] for j in range(n_mla)]
    gn = [hgrn_g_norm[j][None, :] for j in range(n_hgrn)]

    def r2(a):
        return a.reshape(T, a.shape[-1])

    def r3(a):
        return a.reshape(B, S, a.shape[-1])

    saved = []
    xs = x
    for layer in range(depth):
        j = layer // 2
        sub = 2 * layer
        tag = f"l{layer}"
        ssem, rsem, lands = gathers[layer, 0]
        lands = _gather_wait(ssem, rsem, lands, xs if layer else scale[0], f"gather_wait_{tag}p0")
        wl = {k: g for (k, _), g in zip(group_kinds(layer, 0), lands)}
        if layer == 0:
            h = _modulate(xs, scale[sub], shift[sub], f"mod_{tag}a")
        if layer % 2 == 0:
            wl["mla_w_in"] = full_w_in(wl["mla_w_in"])
            proj = r3(_mm_nn(r2(h), wl["mla_w_in"], F32, f"mla_in_{tag}"))
            qn, kvn = _mla_mid_fwd(proj, gq[j], gkv[j], f"mla_mid_{tag}")
            q = r3(_mm_nn(r2(qn), wl["mla_w_qb"], F32, f"mla_qb_{tag}"))
            kv = r3(_mm_nn(r2(kvn), wl["mla_w_kvb"], F32, f"mla_kvb_{tag}"))
            qh, kh, vh = _mla_prep_fwd(q, kv, proj, cos, sin, f"mla_prep_{tag}")
            o, lse = _attn_fwd(qh, kh, vh, f"attn_{tag}")
            wl["mla_w_o"] = row_w(wl["mla_w_o"])
            y = r3(_mm_nn(r2(o), wl["mla_w_o"], F32, f"mla_o_{tag}"))
            mix = (h, proj, qn, kvn, qh, kh, vh, o, lse)
        else:
            proj = r3(_mm_nn(r2(h), wl["hgrn_w_in"], F32, f"hgrn_in_{tag}"))
            og, o_pre, states = _hgrn_fwd(proj, lower_bounds[j][None, :], gn[j], f"hgrn_{tag}")
            wl["hgrn_w_o"] = row_w(wl["hgrn_w_o"])
            y = r3(_mm_nn(r2(og), wl["hgrn_w_o"], F32, f"hgrn_o_{tag}"))
            mix = (h, proj, og, o_pre, states)
        x1, h2 = _ln_mod_fwd(alpha, xs, y, gate[sub], lng[sub], lnb[sub], scale[sub + 1], shift[sub + 1], f"ln_{tag}a")
        ssem, rsem, lands = gathers[layer, 1]
        lands = _gather_wait(ssem, rsem, lands, x1, f"gather_wait_{tag}p1")
        wl.update({k: g for (k, _), g in zip(group_kinds(layer, 1), lands)})
        a, ug, uu = [r3(t_) for t_ in _ffn_in(r2(h2), wl["ffn_w_in"], f"ffn_in_{tag}")]
        wl["ffn_w_out"] = row_w(wl["ffn_w_out"])
        y2 = r3(_mm_nn(r2(a), wl["ffn_w_out"], F32, f"ffn_out_{tag}"))
        if layer + 1 < depth:
            x2, h_next = _ln_mod_fwd(alpha, x1, y2, gate[sub + 1], lng[sub + 1], lnb[sub + 1], scale[sub + 2], shift[sub + 2], f"ln_{tag}b")
        else:
            x2, h_next = _ln_fwd(alpha, x1, y2, gate[sub + 1], lng[sub + 1], lnb[sub + 1], f"ln_{tag}b"), None
        saved.append((xs, y, x1, y2, mix, h2, ug, uu, a, wl))
        xs, h = x2, h_next

    loss_local, dout = _loss_head(xs, loss_target, "loss_head")
    loss = lax.psum(loss_local, ("x", "y", "c"))

    gw = {k: [None] * main[k].shape[0] for k in names}
    land = {k: lax.empty((3,) + main[k].shape, BF16) for k in names}
    scatters = []
    d_shift, d_scale, d_gate = [None] * n_sub, [None] * n_sub, [None] * n_sub
    d_lng, d_lnb = [None] * n_sub, [None] * n_sub
    d_gq, d_gkv, d_gn, d_lbnd = [None] * n_mla, [None] * n_mla, [None] * n_hgrn, [None] * n_hgrn

    def rows4(g):
        return g.reshape(4, g.shape[1] // 4, g.shape[2])

    def scatter_kinds(layer, part):
        if part == 1 or layer % 2:
            return group_kinds(layer, part)
        mixer = group_kinds(layer, 0)
        return mixer[:1] if part == 0 else mixer[1:]

    def start_scatter(layer, part, params, at):
        kinds = scatter_kinds(layer, part)
        ssem, rsem, slabs_t, lands_t, token = _scatter_start(
            [gw[k][i] for k, i in kinds], [land[k] for k, _ in kinds], [i for _, i in kinds], f"scatter_start_l{layer}p{part}")
        for (k, i), s_t, l_t in zip(kinds, slabs_t, lands_t):
            gw[k][i], land[k] = s_t, l_t
        scatters.append((layer, part, ssem, rsem))
        if params is not None:
            params[at] = params[at] + token[0, 0]

    for layer in reversed(range(depth)):
        j = layer // 2
        sub = 2 * layer
        tag = f"l{layer}"
        xs, y, x1, y2, mix, h2, ug, uu, a, wl = saved[layer]
        if layer + 1 == depth:
            dxr, dy2, d_gate[sub + 1], d_lng[sub + 1], d_lnb[sub + 1] = _ln_bwd(
                alpha, dout, x1, y2, gate[sub + 1], lng[sub + 1], lnb[sub + 1], f"ln_bwd_{tag}b")
        else:
            dxr, dy2, d_gate[sub + 1], d_lng[sub + 1], d_lnb[sub + 1], d_scale[sub + 2], d_shift[sub + 2] = _ln_mod_bwd(
                alpha, dh, dxr, scale[sub + 2], x1, y2, gate[sub + 1], lng[sub + 1], lnb[sub + 1], f"ln_bwd_{tag}b")
        da = r3(_mm_nt(r2(dy2), wl["ffn_w_out"], F32, f"ffn_out_dx_{tag}"))
        gw["ffn_w_out"][layer] = rows4(_mm_tn(r2(a), r2(dy2), 1, BF16, f"ffn_out_dw_{tag}"))
        du = _swiglu_bwd(ug, uu, da, f"swiglu_bwd_{tag}")
        dh2 = r3(_mm_nt(r2(du), wl["ffn_w_in"], F32, f"ffn_in_dx_{tag}"))
        gw["ffn_w_in"][layer] = _mm_tn(r2(h2), r2(du), 4, BF16, f"ffn_in_dw_{tag}")
        start_scatter(layer, 1, gate, sub)
        dxr, dy, d_gate[sub], d_lng[sub], d_lnb[sub], d_scale[sub + 1], d_shift[sub + 1] = _ln_mod_bwd(
            alpha, dh2, dxr, scale[sub + 1], xs, y, gate[sub], lng[sub], lnb[sub], f"ln_bwd_{tag}a")
        if layer % 2 == 0:
            h, proj, qn, kvn, qh, kh, vh, o, lse = mix
            do = r3(_mm_nt(r2(dy), wl["mla_w_o"], BF16, f"mla_o_dx_{tag}"))
            gw["mla_w_o"][j] = rows4(_mm_tn(r2(o), r2(dy), 1, BF16, f"mla_o_dw_{tag}"))
            dqh, dkh, dvh = _attn_bwd(qh, kh, vh, o, do, lse, f"attn_bwd_{tag}")
            dq_, dkv_, dkr = _mla_prep_bwd(dqh, dkh, dvh, cos, sin, f"mla_prep_bwd_{tag}")
            dqn = r3(_mm_nt(r2(dq_), wl["mla_w_qb"], F32, f"mla_qb_dx_{tag}"))
            gw["mla_w_qb"][j] = _mm_tn(r2(qn), r2(dq_), 4, BF16, f"mla_qb_dw_{tag}")
            dkvn = r3(_mm_nt(r2(dkv_), wl["mla_w_kvb"], F32, f"mla_kvb_dx_{tag}"))
            gw["mla_w_kvb"][j] = _mm_tn(r2(kvn), r2(dkv_), 4, BF16, f"mla_kvb_dw_{tag}")
            start_scatter(layer, 2, gq, j)
            dproj, dgq_, dgkv_ = _mla_mid_bwd(proj, dqn, dkvn, dkr, gq[j], gkv[j], f"mla_mid_bwd_{tag}")
            d_gq[j], d_gkv[j] = dgq_.sum(0), dgkv_.sum(0)
            dh = r3(_mm_nt(r2(dproj), wl["mla_w_in"], F32, f"mla_in_dx_{tag}"))
            gwin = _mm_tn(r2(h), r2(dproj), 1, BF16, f"mla_in_dw_{tag}")[0]
            gw["mla_w_in"][j] = jnp.transpose(gwin.reshape(gwin.shape[0], 4, gwin.shape[1] // 4), (1, 0, 2))
        else:
            h, proj, og, o_pre, states = mix
            dog = r3(_mm_nt(r2(dy), wl["hgrn_w_o"], F32, f"hgrn_o_dx_{tag}"))
            gw["hgrn_w_o"][j] = rows4(_mm_tn(r2(og), r2(dy), 1, BF16, f"hgrn_o_dw_{tag}"))
            dq_, df_, di_, dg_, dlb_, dgn_ = _hgrn_bwd(proj, lower_bounds[j][None, :], gn[j], o_pre, states, dog, f"hgrn_bwd_{tag}")
            dproj = jnp.concatenate([dq_, df_, di_, dg_], axis=-1)
            d_lbnd[j] = dlb_.sum(0).reshape(1, HK)
            d_gn[j] = dgn_.sum((0, 1))
            dh = r3(_mm_nt(r2(dproj), wl["hgrn_w_in"], F32, f"hgrn_in_dx_{tag}"))
            gw["hgrn_w_in"][j] = _mm_tn(r2(h), r2(dproj), 4, BF16, f"hgrn_in_dw_{tag}")
        start_scatter(layer, 0, gate if layer else None, sub - 1)
    grad_x, d_scale[0], d_shift[0] = _mod_bwd(dh, dxr, x, scale[0], "mod_bwd_l0a")

    for layer, part, ssem, rsem in scatters:
        kinds = scatter_kinds(layer, part)
        slabs_t, lands_t = _scatter_wait(
            ssem, rsem, [gw[k][i] for k, i in kinds], [land[k] for k, _ in kinds], [i for _, i in kinds], grad_x,
            f"scatter_wait_l{layer}p{part}")
        for (k, i), s_t, l_t in zip(kinds, slabs_t, lands_t):
            gw[k][i], land[k] = s_t, l_t
    sums = [_sum4(jnp.stack([lax.dynamic_index_in_dim(g, k_me, 0, keepdims=False) for g in gw[k]]), land[k], f"sum4_{k}")
            for k in names]

    dmod = jnp.stack([jnp.concatenate([d_shift[s_][:, 0], d_scale[s_][:, 0], d_gate[s_][:, 0]], axis=-1) for s_ in range(n_sub)])
    dmod_rows = _pad_rows(dmod.reshape(n_sub * B, 3 * D), -(-n_sub * B // 8) * 8)
    dmod_all = _allgather8(dmod_rows, "gather_dmod")[:, : n_sub * B].reshape(8, n_sub, B, 3 * D)
    dmod_all = jnp.transpose(dmod_all, (1, 0, 2, 3)).reshape(n_sub, Bg, 3 * D)
    dmod_mine = lax.dynamic_slice_in_dim(dmod_all, k_me * n_ada, n_ada, axis=2)
    g_ada_w, g_ada_b = _ada_bwd(c_all, dmod_mine, "ada_bwd")
    g_ada_w = g_ada_w.reshape(ada_w.shape)
    g_ada_b = g_ada_b.reshape(ada_b.shape)

    small = [jnp.stack(d_gq).reshape(-1), jnp.stack(d_gkv).reshape(-1), jnp.stack(d_gn).reshape(-1),
             jnp.stack(d_lbnd).reshape(-1), jnp.stack([d.sum(0) for d in d_lng]).reshape(-1),
             jnp.stack([d.sum(0) for d in d_lnb]).reshape(-1)]
    sizes = [s_.shape[0] for s_ in small]
    flat = jnp.concatenate(small)
    rows_small = -(-flat.shape[0] // (8 * LANES)) * 8
    flat = jnp.pad(flat, (0, rows_small * LANES - flat.shape[0])).reshape(rows_small, LANES)
    tot = _allgather8(flat, "gather_small")
    acc = tot[0]
    for d in range(1, 8):
        acc = acc + tot[d]
    acc = acc.reshape(-1)
    offs = [0]
    for s_ in sizes:
        offs.append(offs[-1] + s_)
    g_q_norm = acc[offs[0] : offs[1]].reshape(mla_q_norm.shape)
    g_kv_norm = acc[offs[1] : offs[2]].reshape(mla_kv_norm.shape)
    g_g_norm = acc[offs[2] : offs[3]].reshape(hgrn_g_norm.shape)
    g_lbnd = acc[offs[3] : offs[4]].reshape(n_hgrn, HK)
    g_lb_full = lower_bounds_vjp(g_lbnd)[0]
    g_hgrn_lb = lax.dynamic_slice_in_dim(g_lb_full, k_me * lbw, lbw, axis=1)
    g_lng = lax.dynamic_slice_in_dim(acc[offs[4] : offs[5]].reshape(n_sub, D), k_me * dq, dq, axis=1).reshape(ln_g.shape)
    g_lnb = lax.dynamic_slice_in_dim(acc[offs[5] : offs[6]].reshape(n_sub, D), k_me * dq, dq, axis=1).reshape(ln_b.shape)

    weights = dict(mla_w_in=mla_w_in, mla_q_norm=mla_q_norm, mla_w_qb=mla_w_qb, mla_kv_norm=mla_kv_norm, mla_w_kvb=mla_w_kvb,
                   mla_w_o=mla_w_o, hgrn_lb=hgrn_lb, hgrn_w_in=hgrn_w_in, hgrn_g_norm=hgrn_g_norm, hgrn_w_o=hgrn_w_o,
                   ffn_w_in=ffn_w_in, ffn_w_out=ffn_w_out, ada_w=ada_w, ada_b=ada_b, ln_g=ln_g, ln_b=ln_b)
    moms = dict(mla_w_in=(m_mla_w_in, v_mla_w_in), mla_q_norm=(m_mla_q_norm, v_mla_q_norm), mla_w_qb=(m_mla_w_qb, v_mla_w_qb),
                mla_kv_norm=(m_mla_kv_norm, v_mla_kv_norm), mla_w_kvb=(m_mla_w_kvb, v_mla_w_kvb), mla_w_o=(m_mla_w_o, v_mla_w_o),
                hgrn_lb=(m_hgrn_lb, v_hgrn_lb), hgrn_w_in=(m_hgrn_w_in, v_hgrn_w_in), hgrn_g_norm=(m_hgrn_g_norm, v_hgrn_g_norm),
                hgrn_w_o=(m_hgrn_w_o, v_hgrn_w_o), ffn_w_in=(m_ffn_w_in, v_ffn_w_in), ffn_w_out=(m_ffn_w_out, v_ffn_w_out),
                ada_w=(m_ada_w, v_ada_w), ada_b=(m_ada_b, v_ada_b), ln_g=(m_ln_g, v_ln_g), ln_b=(m_ln_b, v_ln_b))
    grads = dict(mla_q_norm=(g_q_norm,), mla_kv_norm=(g_kv_norm,), hgrn_lb=(g_hgrn_lb,), hgrn_g_norm=(g_g_norm,),
                 ada_w=(g_ada_w,), ada_b=(g_ada_b,), ln_g=(g_lng,), ln_b=(g_lnb,))

    def adamw(k):
        return _adamw(weights[k], [g_.reshape(weights[k].shape) for g_ in grads[k]], moms[k][0], moms[k][1], f"adamw_{k}")

    ssem, rsem, sums, others = _swap_start(sums, tot[0, :8] + dmod_all[0, :8, :LANES], "swap_start")
    res = {k: adamw(k) for k in grads}
    sums, others = _swap_wait(ssem, rsem, sums, others, res["ada_w"][1], "swap_wait")
    grads.update({k: (a_, b_) for k, a_, b_ in zip(names, sums, others)})
    res.update({k: adamw(k) for k in names})
    order = list(weights)
    return (loss, grad_x, *[res[k][0] for k in order], *[res[k][1] for k in order], *[res[k][2] for k in order],
            *[res[k][3] for k in order])
```
